```python
import math
import jax, jax.numpy as jnp
from jax import lax
import numpy as np

D_MODEL = 1024
BATCH = 4
SEQ = 4096
DEPTH = 1

D_MIX = D_MODEL
MOBA_HEADS = 8
MOBA_HEAD_DIM = 64
MOBA_WIDTH = MOBA_HEADS * MOBA_HEAD_DIM
MOBA_BLOCK = 256
MOBA_TOPK = 3
MOBA_QCHUNK = 32
GLA_HEADS = 4
GLA_DK = 64
GLA_DV = 128
GLA_KEY_WIDTH = GLA_HEADS * GLA_DK
GLA_WIDTH = GLA_HEADS * GLA_DV
GLA_GATE_RANK = 16
GLA_GATE_TAU = 16.0
GLA_CHUNK = 64
REL_BUCKETS = 32
REL_MAX_DIST = 128
N_EXPERTS = 256
TOP_K = 8
N_GROUPS = 8
TOPK_GROUPS = 4
EXPERT_FF = 256
SHARED_FF = 256
ROUTED_SCALE = 2.5
MOE_ROW_BLOCK = 128
NORM_EPS = 1e-6
D_IN = 3 * MOBA_WIDTH + 2 * GLA_KEY_WIDTH + 2 * GLA_WIDTH + GLA_GATE_RANK

kernel_name = "hybrid_moba_gla_moe_block"


def rms_norm(x, g):
    xf = x.astype(jnp.float32)
    y = xf * lax.rsqrt(jnp.mean(xf * xf, axis=-1, keepdims=True) + NORM_EPS)
    return (y * g.astype(jnp.float32)).astype(x.dtype)


def t5_bucket(rel):
    rel = jnp.maximum(rel, 0)
    max_exact = REL_BUCKETS // 2
    relf = jnp.maximum(rel, 1).astype(jnp.float32)
    large = max_exact + (jnp.log(relf / max_exact) / math.log(REL_MAX_DIST / max_exact)
                         * (REL_BUCKETS - max_exact)).astype(jnp.int32)
    large = jnp.minimum(large, REL_BUCKETS - 1)
    return jnp.where(rel < max_exact, rel, large)


def _gather_blocks(blocks, idx):
    return jax.vmap(jax.vmap(lambda kb, i: kb[i]))(blocks, idx)


def moba_attention(q, k, v, rel_bias):
    B, H, S, Dh = q.shape
    n_blk = -(-S // MOBA_BLOCK)
    s_pad = n_blk * MOBA_BLOCK
    pad = ((0, 0), (0, 0), (0, s_pad - S), (0, 0))
    q, k, v = jnp.pad(q, pad), jnp.pad(k, pad), jnp.pad(v, pad)
    k_blocks = k.reshape(B, H, n_blk, MOBA_BLOCK, Dh)
    v_blocks = v.reshape(B, H, n_blk, MOBA_BLOCK, Dh)
    k_mean = jnp.mean(k_blocks.astype(jnp.float32), axis=3).astype(k.dtype)
    scale = Dh ** -0.5
    bias_t = rel_bias.T
    h_idx = jnp.arange(H)[:, None, None, None]
    k_sel = min(MOBA_TOPK, n_blk)
    blk_off = jnp.arange(MOBA_BLOCK)

    def chunk(ci):
        q0 = ci * MOBA_QCHUNK
        blk = q0 // MOBA_BLOCK
        qc = lax.dynamic_slice_in_dim(q, q0, MOBA_QCHUNK, axis=2)
        q_pos = q0 + jnp.arange(MOBA_QCHUNK)
        gate = jnp.einsum('bhqd,bhnd->bhqn', qc, k_mean).astype(jnp.float32)
        gate = jnp.where(jnp.arange(n_blk) < blk, gate, -jnp.inf)
        _, sel = lax.top_k(gate, k_sel)
        sel_valid = jnp.arange(k_sel) < blk
        k_g = _gather_blocks(k_blocks, sel)
        v_g = _gather_blocks(v_blocks, sel)
        k_pos_sel = sel[..., None] * MOBA_BLOCK + blk_off
        s_sel = jnp.einsum('bhqd,bhqjkd->bhqjk', qc, k_g).astype(jnp.float32) * scale
        s_sel = s_sel + bias_t[h_idx, t5_bucket(q_pos[:, None, None] - k_pos_sel)]
        s_sel = jnp.where(sel_valid[:, None], s_sel, -jnp.inf)
        s_sel = s_sel.reshape(B, H, MOBA_QCHUNK, k_sel * MOBA_BLOCK)
        b0 = blk * MOBA_BLOCK
        k_own = lax.dynamic_slice_in_dim(k, b0, MOBA_BLOCK, axis=2)
        v_own = lax.dynamic_slice_in_dim(v, b0, MOBA_BLOCK, axis=2)
        rel_own = q_pos[:, None] - (b0 + blk_off)[None, :]
        s_own = jnp.einsum('bhqd,bhkd->bhqk', qc, k_own).astype(jnp.float32) * scale
        s_own = s_own + bias_t[:, t5_bucket(rel_own)][None]
        s_own = jnp.where(rel_own >= 0, s_own, -jnp.inf)
        p = jax.nn.softmax(jnp.concatenate([s_sel, s_own], axis=-1), axis=-1)
        p_sel = p[..., :k_sel * MOBA_BLOCK].reshape(B, H, MOBA_QCHUNK, k_sel, MOBA_BLOCK).astype(v.dtype)
        p_own = p[..., k_sel * MOBA_BLOCK:].astype(v.dtype)
        return (jnp.einsum('bhqjk,bhqjkd->bhqd', p_sel, v_g)
                + jnp.einsum('bhqk,bhkd->bhqd', p_own, v_own))

    out = lax.map(chunk, jnp.arange(s_pad // MOBA_QCHUNK))
    return jnp.moveaxis(out, 0, 2).reshape(B, H, s_pad, Dh)[:, :, :S]


def gla_chunked(q, k, v, log_a):
    B, H, S, dk = q.shape
    dv = v.shape[-1]
    C = GLA_CHUNK
    nc = S // C

    def chunks(t):
        return jnp.moveaxis(t.astype(jnp.float32).reshape(B, H, nc, C, t.shape[-1]), 2, 0)

    causal = jnp.tril(jnp.ones((C, C), dtype=bool))[:, :, None]

    def step(state, inp):
        qc, kc, vc, ac = inp
        b = jnp.cumsum(ac, axis=2)
        diff = b[:, :, :, None, :] - b[:, :, None, :, :]
        decay = jnp.exp(jnp.where(causal, diff, -jnp.inf))
        scores = jnp.einsum('bhid,bhjd,bhijd->bhij', qc, kc, decay)
        o = (jnp.einsum('bhij,bhjv->bhiv', scores, vc)
             + jnp.einsum('bhid,bhdv->bhiv', qc * jnp.exp(b), state))
        b_last = b[:, :, -1:, :]
        state = (jnp.exp(b_last[:, :, 0, :])[..., None] * state
                 + jnp.einsum('bhjd,bhjv->bhdv', kc * jnp.exp(b_last - b), vc))
        return state, o

    state0 = jnp.zeros((B, H, dk, dv), jnp.float32)
    _, o = lax.scan(step, state0, (chunks(q), chunks(k), chunks(v), chunks(log_a)))
    return jnp.moveaxis(o, 0, 2).reshape(B, H, S, dv).astype(v.dtype)


def hybrid_mixer(h, w_in, q_norm_w, k_norm_w, rel_bias, w_alpha, b_alpha, moba_out_w, gla_out_w, w_out):
    B, S, _ = h.shape
    proj = h @ w_in
    sizes = [MOBA_WIDTH, MOBA_WIDTH, MOBA_WIDTH, GLA_KEY_WIDTH, GLA_KEY_WIDTH, GLA_WIDTH, GLA_WIDTH, GLA_GATE_RANK]
    offsets = [int(o) for o in np.cumsum(sizes)[:-1]]
    mq, mk, mv, gq, gk, gv, gg, ga = jnp.split(proj, offsets, axis=-1)

    def heads(t, n):
        return t.reshape(B, S, n, -1).transpose(0, 2, 1, 3)

    mq = rms_norm(heads(mq, MOBA_HEADS), q_norm_w)
    mk = rms_norm(heads(mk, MOBA_HEADS), k_norm_w)
    o_a = moba_attention(mq, mk, heads(mv, MOBA_HEADS), rel_bias)
    o_a = rms_norm(o_a, moba_out_w).transpose(0, 2, 1, 3).reshape(B, S, MOBA_WIDTH)
    log_a = jax.nn.log_sigmoid((ga @ w_alpha + b_alpha).astype(jnp.float32)) / GLA_GATE_TAU
    o_b = gla_chunked(heads(gq, GLA_HEADS) * (GLA_DK ** -0.5), heads(gk, GLA_HEADS),
                      heads(gv, GLA_HEADS), heads(log_a, GLA_HEADS))
    o_b = rms_norm(o_b, gla_out_w).transpose(0, 2, 1, 3).reshape(B, S, GLA_WIDTH)
    o_b = o_b * jax.nn.silu(gg)
    return jnp.concatenate([o_a, o_b], axis=-1) @ w_out


def moe_ffn(h, w_router, e_bias, w1, w3, w2, ws1, ws3, ws2):
    B, S, D = h.shape
    T = B * S
    R = MOE_ROW_BLOCK
    xt = h.reshape(T, D)
    scores = jax.nn.sigmoid((xt @ w_router).astype(jnp.float32))
    choice = scores + e_bias.astype(jnp.float32)
    grp = choice.reshape(T, N_GROUPS, N_EXPERTS // N_GROUPS)
    grp_score = lax.top_k(grp, 2)[0].sum(-1)
    _, grp_idx = lax.top_k(grp_score, TOPK_GROUPS)
    grp_mask = jnp.any(grp_idx[..., None] == jnp.arange(N_GROUPS), axis=1)
    masked = jnp.where(jnp.repeat(grp_mask, N_EXPERTS // N_GROUPS, axis=1), choice, -jnp.inf)
    _, top_idx = lax.top_k(masked, TOP_K)
    top_w = jnp.take_along_axis(scores, top_idx, axis=1)
    top_w = top_w / jnp.sum(top_w, axis=-1, keepdims=True) * ROUTED_SCALE
    TK = T * TOP_K
    e_flat = top_idx.reshape(TK)
    tok_flat = jnp.repeat(jnp.arange(T, dtype=jnp.int32), TOP_K)
    w_flat = top_w.reshape(TK)
    order = jnp.argsort(e_flat)
    e_sorted = e_flat[order]
    counts = jnp.bincount(e_flat, length=N_EXPERTS)
    start = jnp.cumsum(counts) - counts
    padded = (counts + R - 1) // R * R
    pend = jnp.cumsum(padded)
    pstart = pend - padded
    dest = pstart[e_sorted] + (jnp.arange(TK) - start[e_sorted])
    n_rows = (TK + N_EXPERTS * (R - 1) + R - 1) // R * R
    n_blocks = n_rows // R
    row_tok = jnp.zeros((n_rows,), jnp.int32).at[dest].set(tok_flat[order])
    row_w = jnp.zeros((n_rows,), jnp.float32).at[dest].set(w_flat[order])
    blk_expert = jnp.minimum(jnp.searchsorted(pend, jnp.arange(n_blocks) * R, side='right'), N_EXPERTS - 1)

    def run_block(args):
        e, toks = args
        xb = xt[toks]
        return (jax.nn.silu(xb @ w1[e]) * (xb @ w3[e])) @ w2[e]

    rows = lax.map(run_block, (blk_expert, row_tok.reshape(n_blocks, R)))
    rows = rows.reshape(n_rows, D) * row_w[:, None].astype(rows.dtype)
    routed = jax.ops.segment_sum(rows, row_tok, num_segments=T)
    shared = (jax.nn.silu(xt @ ws1) * (xt @ ws3)) @ ws2
    return (shared + routed).reshape(B, S, D)


def setup_inputs(seed: int = 0) -> dict:
    key = jax.random.key(seed)
    ks = jax.random.split(key, 24)
    f32 = jnp.float32
    L = DEPTH

    def nrm(k, shape, scale):
        return jax.random.normal(k, shape, f32) * scale

    return dict(
        x=nrm(ks[0], (BATCH, SEQ, D_MODEL), 1.0),
        c=nrm(ks[1], (BATCH, D_MODEL), 1.0),
        w_ada=nrm(ks[2], (L, D_MODEL, 6 * D_MODEL), 0.5 * D_MODEL ** -0.5),
        b_ada=nrm(ks[3], (L, 6 * D_MODEL), 0.02),
        norm1_w=1.0 + nrm(ks[4], (L, D_MODEL), 0.02),
        norm2_w=1.0 + nrm(ks[5], (L, D_MODEL), 0.02),
        w_in=nrm(ks[6], (L, D_MODEL, D_IN), D_MODEL ** -0.5),
        q_norm_w=1.0 + nrm(ks[7], (L, MOBA_HEAD_DIM), 0.02),
        k_norm_w=1.0 + nrm(ks[8], (L, MOBA_HEAD_DIM), 0.02),
        rel_bias=nrm(ks[9], (REL_BUCKETS, MOBA_HEADS), 0.5),
        w_alpha=nrm(ks[10], (L, GLA_GATE_RANK, GLA_KEY_WIDTH), GLA_GATE_RANK ** -0.5),
        b_alpha=nrm(ks[11], (L, GLA_KEY_WIDTH), 0.1),
        moba_out_w=1.0 + nrm(ks[12], (L, MOBA_HEAD_DIM), 0.02),
        gla_out_w=1.0 + nrm(ks[13], (L, GLA_DV), 0.02),
        w_out=nrm(ks[14], (L, D_MIX, D_MODEL), D_MIX ** -0.5),
        w_router=nrm(ks[15], (L, D_MODEL, N_EXPERTS), D_MODEL ** -0.5),
        e_bias=nrm(ks[16], (L, N_EXPERTS), 0.01),
        w1=nrm(ks[17], (L, N_EXPERTS, D_MODEL, EXPERT_FF), D_MODEL ** -0.5),
        w3=nrm(ks[18], (L, N_EXPERTS, D_MODEL, EXPERT_FF), D_MODEL ** -0.5),
        w2=nrm(ks[19], (L, N_EXPERTS, EXPERT_FF, D_MODEL), EXPERT_FF ** -0.5),
        ws1=nrm(ks[20], (L, D_MODEL, SHARED_FF), D_MODEL ** -0.5),
        ws3=nrm(ks[21], (L, D_MODEL, SHARED_FF), D_MODEL ** -0.5),
        ws2=nrm(ks[22], (L, SHARED_FF, D_MODEL), SHARED_FF ** -0.5),
    )


def reference(x, c, w_ada, b_ada, norm1_w, norm2_w, w_in, q_norm_w, k_norm_w, rel_bias, w_alpha, b_alpha,
              moba_out_w, gla_out_w, w_out, w_router, e_bias, w1, w3, w2, ws1, ws3, ws2):
    for l in range(DEPTH):
        mod = jax.nn.silu(c) @ w_ada[l] + b_ada[l]
        sh1, sc1, g1, sh2, sc2, g2 = jnp.split(mod[:, None, :], 6, axis=-1)
        h = rms_norm(x, norm1_w[l]) * (1.0 + sc1) + sh1
        x = x + g1 * hybrid_mixer(h, w_in[l], q_norm_w[l], k_norm_w[l], rel_bias, w_alpha[l], b_alpha[l],
                                  moba_out_w[l], gla_out_w[l], w_out[l])
        h = rms_norm(x, norm2_w[l]) * (1.0 + sc2) + sh2
        x = x + g2 * moe_ffn(h, w_router[l], e_bias[l], w1[l], w3[l], w2[l], ws1[l], ws3[l], ws2[l])
    return x
```

```python
import functools
import math

import numpy as np
import jax
import jax.numpy as jnp
from jax import lax
from jax.experimental import pallas as pl
from jax.experimental.pallas import tpu as pltpu

D_MODEL = 1024
MOBA_HEADS = 8
MOBA_HEAD_DIM = 64
MOBA_WIDTH = MOBA_HEADS * MOBA_HEAD_DIM
MOBA_BLOCK = 256
MOBA_TOPK = 3
GLA_HEADS = 4
GLA_DK = 64
GLA_DV = 128
GLA_KEY_WIDTH = GLA_HEADS * GLA_DK
GLA_WIDTH = GLA_HEADS * GLA_DV
GLA_GATE_RANK = 16
GLA_GATE_TAU = 16.0
GLA_CHUNK = 64
REL_BUCKETS = 32
REL_MAX_DIST = 128
N_EXPERTS = 256
TOP_K = 8
N_GROUPS = 8
TOPK_GROUPS = 4
GROUP_SIZE = N_EXPERTS // N_GROUPS
EXPERT_FF = 256
SHARED_FF = 256
ROUTED_SCALE = 2.5
NORM_EPS = 1e-6

D_MAIN = 3 * MOBA_WIDTH + 2 * GLA_KEY_WIDTH + 2 * GLA_WIDTH
LANES = 128
VMEM_LIMIT = 56 * 1024 * 1024

ROW_TILE = 512
ROUTE_TILE = 256
DISPATCH_TILE = 256
COMBINE_TILE = 128
EXPERT_TILE = 256

F32 = jnp.float32
BF16 = jnp.bfloat16
NT_DIMS = (((1,), (1,)), ((), ()))
TN_DIMS = (((0,), (0,)), ((), ()))


def _params(n_axes):
    return pltpu.CompilerParams(dimension_semantics=("arbitrary",) * n_axes,
                                vmem_limit_bytes=VMEM_LIMIT)


def _silu(v):
    return v * jax.nn.sigmoid(v)


def _mod_kernel(c_ref, w_ref, b_ref, o_ref):
    o_ref[...] = jnp.dot(_silu(c_ref[...]), w_ref[...], preferred_element_type=F32) + b_ref[...]


def _mod(c, w, b):
    rows = 8
    cp = jnp.zeros((rows, D_MODEL), F32).at[:c.shape[0]].set(c)
    n = w.shape[1]
    tn = 1024
    out = pl.pallas_call(
        _mod_kernel,
        grid=(n // tn,),
        in_specs=[pl.BlockSpec((rows, D_MODEL), lambda j: (0, 0)),
                  pl.BlockSpec((D_MODEL, tn), lambda j: (0, j)),
                  pl.BlockSpec((1, tn), lambda j: (0, j))],
        out_specs=pl.BlockSpec((rows, tn), lambda j: (0, j)),
        out_shape=jax.ShapeDtypeStruct((rows, n), F32),
        compiler_params=_params(1),
        name="mod",
    )(cp, w, b.reshape(1, n))
    return out[:c.shape[0]]


def _group_rms_inv(a, group):
    lane = lax.broadcasted_iota(jnp.int32, (1, a.shape[1]), 1)
    a2 = a * a
    inv = jnp.zeros_like(a)
    for g in range(a.shape[1] // group):
        m = (lane >= g * group) & (lane < (g + 1) * group)
        ss = jnp.sum(jnp.where(m, a2, 0.0), axis=-1, keepdims=True)
        inv = jnp.where(m, lax.rsqrt(ss * (1.0 / group) + NORM_EPS), inv)
    return inv


def _inproj_kernel(x_ref, sc_ref, sh_ref, nw_ref, w_ref, wga_ref, qw_ref, kw_ref, o_ref, ga_ref):
    x = x_ref[...]
    ms = jnp.mean(x * x, axis=-1, keepdims=True)
    h = x * lax.rsqrt(ms + NORM_EPS) * nw_ref[...]
    h = h * (1.0 + sc_ref[...]) + sh_ref[...]
    hb = h.astype(BF16)
    cw = 256
    for j in range(D_MAIN // cw):
        acc = jnp.dot(hb, w_ref[:, j * cw:(j + 1) * cw], preferred_element_type=F32)
        if j < 2 * MOBA_WIDTH // cw:
            nw = qw_ref if j < MOBA_WIDTH // cw else kw_ref
            acc = acc * _group_rms_inv(acc, MOBA_HEAD_DIM) * nw[...]
        o_ref[:, j * cw:(j + 1) * cw] = acc.astype(BF16)
    ga_ref[...] = jnp.dot(hb, wga_ref[...], preferred_element_type=F32)


def _inproj(x2, sc, sh, nw, w_main, w_ga, qw, kw, seq):
    t = x2.shape[0]
    tpb = seq // ROW_TILE
    vec = lambda: pl.BlockSpec((None, 1, D_MODEL), lambda i: (i // tpb, 0, 0))
    full = lambda a: pl.BlockSpec(a.shape, lambda i: (0,) * a.ndim)
    return pl.pallas_call(
        _inproj_kernel,
        grid=(t // ROW_TILE,),
        in_specs=[pl.BlockSpec((ROW_TILE, D_MODEL), lambda i: (i, 0)), vec(), vec(),
                  full(nw), full(w_main), full(w_ga), full(qw), full(kw)],
        out_specs=[pl.BlockSpec((ROW_TILE, D_MAIN), lambda i: (i, 0)),
                   pl.BlockSpec((ROW_TILE, LANES), lambda i: (i, 0))],
        out_shape=[jax.ShapeDtypeStruct((t, D_MAIN), BF16),
                   jax.ShapeDtypeStruct((t, LANES), F32)],
        compiler_params=_params(1),
        name="inproj",
    )(x2, sc, sh, nw, w_main, w_ga, qw, kw)


def _t5_bucket_np(rel):
    max_exact = REL_BUCKETS // 2
    relf = np.maximum(rel, 1).astype(np.float64)
    large = max_exact + (np.log(relf / max_exact) / math.log(REL_MAX_DIST / max_exact)
                         * (REL_BUCKETS - max_exact)).astype(np.int32)
    large = np.minimum(large, REL_BUCKETS - 1)
    return np.where(rel < max_exact, rel, large)


def _moba_bias_tables(rel_bias):
    j = np.arange(MOBA_BLOCK)[:, None]
    i = np.arange(MOBA_BLOCK)[None, :]
    own_idx = _t5_bucket_np(np.maximum(i - j, 0))
    prev_idx = _t5_bucket_np(MOBA_BLOCK + i - j)
    rb = rel_bias.astype(F32)
    own = jnp.transpose(rb[own_idx], (2, 0, 1))
    own = jnp.where(jnp.asarray(j <= i)[None], own, -jnp.inf)
    prev = jnp.transpose(rb[prev_idx], (2, 0, 1))
    assert int(_t5_bucket_np(np.array([MOBA_BLOCK + 1]))[0]) == REL_BUCKETS - 1
    far = rb[REL_BUCKETS - 1]
    return own, prev, far


def _moba_kernel(far_ref, q_ref, k_ref, v_ref, own_ref, prev_ref, ow_ref, o_ref,
                 kmean_ref, vt_ref, acc_ref, m_ref, sel_ref):
    hp = pl.program_id(1)
    i = pl.program_id(2)
    nblk = k_ref.shape[0] // MOBA_BLOCK
    hd = MOBA_HEAD_DIM

    @pl.when(i == 0)
    def _prepare():
        row = lax.broadcasted_iota(jnp.int32, (LANES, MOBA_BLOCK), 0)
        for n in range(nblk):
            kb = k_ref[n * MOBA_BLOCK:(n + 1) * MOBA_BLOCK, :].astype(F32)
            kmean_ref[n:n + 1, :] = jnp.mean(kb, axis=0, keepdims=True)
            vt = v_ref[n * MOBA_BLOCK:(n + 1) * MOBA_BLOCK, :].astype(F32).T
            vt_ref[0, n] = jnp.where(row < hd, vt, 1.0).astype(BF16)
            vt_ref[1, n] = jnp.where(row < hd, 1.0, vt).astype(BF16)

    q = q_ref[...]
    lane = lax.broadcasted_iota(jnp.int32, q.shape, 1)
    zero = jnp.zeros_like(q)
    qh = (jnp.where(lane < hd, q, zero), jnp.where(lane < hd, zero, q))

    blk = lax.broadcasted_iota(jnp.int32, (nblk, MOBA_BLOCK), 0)
    for h in range(2):
        gt = lax.dot_general(kmean_ref[...], qh[h].astype(F32), NT_DIMS, preferred_element_type=F32)
        gt = jnp.where(blk < i, gt, -jnp.inf)
        cnt = jnp.zeros(gt.shape, jnp.int32)
        for m in range(nblk):
            gm = gt[m:m + 1, :]
            cnt = cnt + jnp.where((gm > gt) | ((gm == gt) & (blk > m)), 1, 0)
        sel_ref[h] = jnp.where((blk < i) & (cnt < MOBA_TOPK), 1.0, 0.0)

    def attend(n, h, bias_tab, masked, far_bias, first):
        kb = k_ref[pl.ds(pl.multiple_of(n * MOBA_BLOCK, MOBA_BLOCK), MOBA_BLOCK), :]
        s = lax.dot_general(kb, qh[h], NT_DIMS, preferred_element_type=F32)
        if bias_tab is not None:
            s = s + bias_tab[h]
        mx = jnp.max(s, axis=0, keepdims=True)
        if far_bias is not None:
            mx = mx + far_bias
        if first:
            m_new = mx
            shift = m_new
        else:
            m_old = m_ref[h]
            m_new = jnp.maximum(m_old, mx)
            shift = m_new
            if masked:
                keep = sel_ref[h, pl.ds(n, 1), :] > 0.5
                m_new = jnp.where(keep, m_new, m_old)
                shift = m_new if far_bias is None else m_new - far_bias
                shift = jnp.where(keep, shift, jnp.inf)
        p = jnp.exp(s - shift).astype(BF16)
        pv = jnp.dot(vt_ref[h, n], p, preferred_element_type=F32)
        if first:
            acc_ref[h] = pv
        else:
            acc_ref[h] = acc_ref[h] * jnp.exp(m_old - m_new) + pv
        m_ref[h] = m_new

    for h in range(2):
        attend(i, h, own_ref, False, None, True)

    @pl.when(i >= 1)
    def _previous_block():
        for h in range(2):
            attend(i - 1, h, prev_ref, True, None, False)

    def far_body(n, carry):
        for h in range(2):
            attend(n, h, None, True, far_ref[2 * hp + h], False)
        return carry

    lax.fori_loop(0, i - 1, far_body, 0)

    a0 = acc_ref[0]
    a1 = acc_ref[1]
    row = lax.broadcasted_iota(jnp.int32, a0.shape, 0)
    ot = jnp.where(row < hd, a0 / a0[hd:hd + 1, :], a1 / a1[0:1, :])
    o2 = ot * ot
    ss0 = jnp.sum(jnp.where(row < hd, o2, 0.0), axis=0, keepdims=True)
    ss1 = jnp.sum(jnp.where(row < hd, 0.0, o2), axis=0, keepdims=True)
    inv = jnp.where(row < hd, lax.rsqrt(ss0 * (1.0 / hd) + NORM_EPS), lax.rsqrt(ss1 * (1.0 / hd) + NORM_EPS))
    o_ref[...] = ((ot * inv).T * ow_ref[...]).astype(o_ref.dtype)


def _moba(proj3, own, prev, far, ow):
    b, s, _ = proj3.shape
    nblk = s // MOBA_BLOCK
    npair = MOBA_HEADS // 2
    kcol = MOBA_WIDTH // LANES
    return pl.pallas_call(
        _moba_kernel,
        grid=(b, npair, nblk),
        in_specs=[pl.BlockSpec(memory_space=pltpu.SMEM),
                  pl.BlockSpec((None, MOBA_BLOCK, LANES), lambda bb, hp, i: (bb, i, hp)),
                  pl.BlockSpec((None, s, LANES), lambda bb, hp, i: (bb, 0, kcol + hp)),
                  pl.BlockSpec((None, s, LANES), lambda bb, hp, i: (bb, 0, 2 * kcol + hp)),
                  pl.BlockSpec((2, MOBA_BLOCK, MOBA_BLOCK), lambda bb, hp, i: (hp, 0, 0)),
                  pl.BlockSpec((2, MOBA_BLOCK, MOBA_BLOCK), lambda bb, hp, i: (hp, 0, 0)),
                  pl.BlockSpec((1, LANES), lambda bb, hp, i: (0, 0))],
        out_specs=pl.BlockSpec((None, MOBA_BLOCK, LANES), lambda bb, hp, i: (bb, i, hp)),
        out_shape=jax.ShapeDtypeStruct((b, s, MOBA_WIDTH), BF16),
        scratch_shapes=[pltpu.VMEM((nblk, LANES), F32),
                        pltpu.VMEM((2, nblk, LANES, MOBA_BLOCK), BF16),
                        pltpu.VMEM((2, LANES, MOBA_BLOCK), F32),
                        pltpu.VMEM((2, 1, MOBA_BLOCK), F32),
                        pltpu.VMEM((2, nblk, MOBA_BLOCK), F32)],
        compiler_params=_params(3),
        name="moba",
    )(far, proj3, proj3, proj3, own, prev, ow)


def _split3(v):
    hi = v.astype(BF16)
    r1 = v - hi.astype(F32)
    mid = r1.astype(BF16)
    lo = (r1 - mid.astype(F32)).astype(BF16)
    return hi, mid, lo


def _gla_kernel(q_ref, k_ref, v_ref, g_ref, ga_ref, wal_ref, bal_ref, gw_ref, o_ref, b_ref, st_ref):
    seq = q_ref.shape[0]
    c = GLA_CHUNK
    pc = 256

    rr = lax.broadcasted_iota(jnp.int32, (pc, pc), 0)
    cc = lax.broadcasted_iota(jnp.int32, (pc, pc), 1)
    tri = jnp.where((rr >= cc) & (rr // c == cc // c), 1.0, 0.0).astype(BF16)

    def decay_body(j, carry):
        r0 = pl.multiple_of(j * pc, pc)
        xg = jnp.dot(ga_ref[pl.ds(r0, pc), :], wal_ref[...], preferred_element_type=F32) + bal_ref[...]
        la = (jnp.minimum(xg, 0.0) - jnp.log(1.0 + jnp.exp(-jnp.abs(xg)))) * (1.0 / GLA_GATE_TAU)
        hi, mid, lo = _split3(la)
        b_ref[pl.ds(r0, pc), :] = (jnp.dot(tri, hi, preferred_element_type=F32)
                                   + jnp.dot(tri, mid, preferred_element_type=F32)
                                   + jnp.dot(tri, lo, preferred_element_type=F32))
        return carry

    lax.fori_loop(0, seq // pc, decay_body, 0)

    st_ref[...] = jnp.zeros_like(st_ref)
    lane = lax.broadcasted_iota(jnp.int32, (c, LANES), 1)
    head_mask = (lane < GLA_DK, lane >= GLA_DK)
    causal = lax.broadcasted_iota(jnp.int32, (c, c), 0) >= lax.broadcasted_iota(jnp.int32, (c, c), 1)

    def chunk_body(ci, carry):
        r0 = pl.multiple_of(ci * c, c)
        b = b_ref[pl.ds(r0, c), :]
        ref_row = b[c // 2 - 1:c // 2, :]
        last = b[c - 1:c, :]
        q = q_ref[pl.ds(r0, c), :].astype(F32) * (GLA_DK ** -0.5)
        k = k_ref[pl.ds(r0, c), :].astype(F32)
        qt = q * jnp.exp(b - ref_row)
        kt = (k * jnp.exp(ref_row - b)).astype(BF16)
        qs = q * jnp.exp(b)
        ke = (k * jnp.exp(last - b)).astype(BF16)
        e_last = jnp.exp(last)
        for h in range(2):
            cols = slice(h * GLA_DV, (h + 1) * GLA_DV)
            a = lax.dot_general(jnp.where(head_mask[h], qt, 0.0).astype(BF16), kt, NT_DIMS,
                                preferred_element_type=F32)
            a = jnp.where(causal, a, 0.0).astype(BF16)
            v = v_ref[pl.ds(r0, c), cols]
            st = st_ref[h]
            o = jnp.dot(a, v, preferred_element_type=F32)
            o = o + lax.dot_general(jnp.where(head_mask[h], qs, 0.0).astype(BF16), st.astype(BF16), NT_DIMS,
                                    preferred_element_type=F32)
            st_ref[h] = st * e_last + lax.dot_general(v, ke, TN_DIMS, preferred_element_type=F32)
            ms = jnp.mean(o * o, axis=-1, keepdims=True)
            on = o * lax.rsqrt(ms + NORM_EPS) * gw_ref[...]
            g = g_ref[pl.ds(r0, c), cols].astype(F32)
            o_ref[pl.ds(r0, c), cols] = (on * _silu(g)).astype(o_ref.dtype)
        return carry

    lax.fori_loop(0, seq // c, chunk_body, 0)


def _gla(proj3, ga3, wal, bal, gw):
    b, s, _ = proj3.shape
    npair = GLA_HEADS // 2
    qcol = 3 * MOBA_WIDTH // LANES
    kcol = qcol + GLA_KEY_WIDTH // LANES
    vcol = (3 * MOBA_WIDTH + 2 * GLA_KEY_WIDTH) // (2 * GLA_DV)
    gcol = vcol + npair
    return pl.pallas_call(
        _gla_kernel,
        grid=(b, npair),
        in_specs=[pl.BlockSpec((None, s, LANES), lambda bb, hp: (bb, 0, qcol + hp)),
                  pl.BlockSpec((None, s, LANES), lambda bb, hp: (bb, 0, kcol + hp)),
                  pl.BlockSpec((None, s, 2 * GLA_DV), lambda bb, hp: (bb, 0, vcol + hp)),
                  pl.BlockSpec((None, s, 2 * GLA_DV), lambda bb, hp: (bb, 0, gcol + hp)),
                  pl.BlockSpec((None, s, LANES), lambda bb, hp: (bb, 0, 0)),
                  pl.BlockSpec((LANES, LANES), lambda bb, hp: (0, hp)),
                  pl.BlockSpec((1, LANES), lambda bb, hp: (0, hp)),
                  pl.BlockSpec((1, GLA_DV), lambda bb, hp: (0, 0))],
        out_specs=pl.BlockSpec((None, s, 2 * GLA_DV), lambda bb, hp: (bb, 0, hp)),
        out_shape=jax.ShapeDtypeStruct((b, s, GLA_WIDTH), BF16),
        scratch_shapes=[pltpu.VMEM((s, LANES), F32),
                        pltpu.VMEM((2, GLA_DV, LANES), F32)],
        compiler_params=_params(2),
        name="gla",
    )(proj3, proj3, proj3, proj3, ga3, wal, bal, gw)


def _outproj_kernel(oa_ref, ob_ref, x_ref, g1_ref, sc_ref, sh_ref, g2_ref, nw_ref, wo_ref,
                    ws1_ref, ws3_ref, ws2_ref, wrt_ref, base_ref, h_ref, st_ref):
    mix = (jnp.dot(oa_ref[...], wo_ref[:MOBA_WIDTH, :], preferred_element_type=F32)
           + jnp.dot(ob_ref[...], wo_ref[MOBA_WIDTH:, :], preferred_element_type=F32))
    x1 = x_ref[...] + g1_ref[...] * mix
    ms = jnp.mean(x1 * x1, axis=-1, keepdims=True)
    h = x1 * lax.rsqrt(ms + NORM_EPS) * nw_ref[...]
    h = h * (1.0 + sc_ref[...]) + sh_ref[...]
    h_ref[...] = h
    hb = h.astype(BF16)
    a = jnp.dot(hb, ws1_ref[...], preferred_element_type=F32)
    u = jnp.dot(hb, ws3_ref[...], preferred_element_type=F32)
    shared = jnp.dot((_silu(a) * u).astype(BF16), ws2_ref[...], preferred_element_type=F32)
    base_ref[...] = x1 + g2_ref[...] * shared
    logits_t = lax.dot_general(wrt_ref[...], hb, NT_DIMS, preferred_element_type=F32)
    st_ref[...] = jax.nn.sigmoid(logits_t)


def _outproj(oa, ob, x2, g1, sc, sh, g2, nw, wo, ws1, ws3, ws2, wrt, seq):
    t = x2.shape[0]
    tpb = seq // ROW_TILE
    vec = lambda: pl.BlockSpec((None, 1, D_MODEL), lambda i: (i // tpb, 0, 0))
    full = lambda a: pl.BlockSpec(a.shape, lambda i: (0,) * a.ndim)
    rows = lambda w: pl.BlockSpec((ROW_TILE, w), lambda i: (i, 0))
    return pl.pallas_call(
        _outproj_kernel,
        grid=(t // ROW_TILE,),
        in_specs=[rows(MOBA_WIDTH), rows(GLA_WIDTH), rows(D_MODEL), vec(), vec(), vec(), vec(),
                  full(nw), full(wo), full(ws1), full(ws3), full(ws2), full(wrt)],
        out_specs=[rows(D_MODEL), rows(D_MODEL), pl.BlockSpec((N_EXPERTS, ROW_TILE), lambda i: (0, i))],
        out_shape=[jax.ShapeDtypeStruct((t, D_MODEL), F32),
                   jax.ShapeDtypeStruct((t, D_MODEL), F32),
                   jax.ShapeDtypeStruct((N_EXPERTS, t), F32)],
        compiler_params=_params(1),
        name="outproj",
    )(oa, ob, x2, g1, sc, sh, g2, nw, wo, ws1, ws3, ws2, wrt)


def _route_kernel(s_ref, eb_ref, idx_ref, w_ref, rank_ref, cnt_ref, carry_ref):
    i = pl.program_id(0)
    ne, nt = s_ref.shape

    @pl.when(i == 0)
    def _init():
        carry_ref[...] = jnp.zeros_like(carry_ref)

    s = s_ref[...]
    choice = s + eb_ref[...]
    gio = lax.broadcasted_iota(jnp.int32, (GROUP_SIZE, nt), 0)
    gscore = []
    for g in range(N_GROUPS):
        cg = choice[g * GROUP_SIZE:(g + 1) * GROUP_SIZE, :]
        top1 = jnp.max(cg, axis=0, keepdims=True)
        first = jnp.min(jnp.where(cg == top1, gio, GROUP_SIZE), axis=0, keepdims=True)
        top2 = jnp.max(jnp.where(gio == first, -jnp.inf, cg), axis=0, keepdims=True)
        gscore.append(top1 + top2)
    gs = jnp.concatenate(gscore, axis=0)
    gidx = lax.broadcasted_iota(jnp.int32, gs.shape, 0)
    beaten = jnp.zeros(gs.shape, jnp.int32)
    for m in range(N_GROUPS):
        gm = gs[m:m + 1, :]
        beaten = beaten + jnp.where((gm > gs) | ((gm == gs) & (gidx > m)), 1, 0)
    gkeep = beaten < TOPK_GROUPS
    masked = jnp.concatenate(
        [jnp.where(gkeep[g:g + 1, :], choice[g * GROUP_SIZE:(g + 1) * GROUP_SIZE, :], -jnp.inf)
         for g in range(N_GROUPS)], axis=0)

    eio = lax.broadcasted_iota(jnp.int32, (ne, nt), 0)
    picked = jnp.zeros((ne, nt), F32)
    idx_rows, w_rows, hits = [], [], []
    for _ in range(TOP_K):
        mx = jnp.max(masked, axis=0, keepdims=True)
        idx = jnp.min(jnp.where(masked == mx, eio, ne), axis=0, keepdims=True)
        hit = eio == idx
        w_rows.append(jnp.sum(jnp.where(hit, s, 0.0), axis=0, keepdims=True))
        idx_rows.append(idx)
        hits.append(hit)
        masked = jnp.where(hit, -jnp.inf, masked)
        picked = jnp.where(hit, 1.0, picked)
    wk = jnp.concatenate(w_rows, axis=0)
    w_ref[...] = wk / jnp.sum(wk, axis=0, keepdims=True) * ROUTED_SCALE
    idx_ref[...] = jnp.concatenate(idx_rows, axis=0)

    tr = lax.broadcasted_iota(jnp.int32, (nt, nt), 0)
    tc = lax.broadcasted_iota(jnp.int32, (nt, nt), 1)
    before = jnp.where(tr < tc, 1.0, 0.0).astype(BF16)
    pb = picked.astype(BF16)
    pos = carry_ref[...] + jnp.dot(pb, before, preferred_element_type=F32)
    rank_ref[...] = jnp.concatenate(
        [jnp.sum(jnp.where(hit, pos, 0.0), axis=0, keepdims=True) for hit in hits], axis=0).astype(jnp.int32)
    total = carry_ref[...] + jnp.dot(pb, jnp.ones((nt, nt), BF16), preferred_element_type=F32)
    carry_ref[...] = total
    cnt_ref[...] = total


def _route(scores_t, eb):
    ne, t = scores_t.shape
    nt = ROUTE_TILE
    tok = lambda dt: jax.ShapeDtypeStruct((TOP_K, t), dt)
    return pl.pallas_call(
        _route_kernel,
        grid=(t // nt,),
        in_specs=[pl.BlockSpec((ne, nt), lambda i: (0, i)),
                  pl.BlockSpec((ne, nt), lambda i: (0, 0))],
        out_specs=[pl.BlockSpec((TOP_K, nt), lambda i: (0, i)),
                   pl.BlockSpec((TOP_K, nt), lambda i: (0, i)),
                   pl.BlockSpec((TOP_K, nt), lambda i: (0, i)),
                   pl.BlockSpec((ne, nt), lambda i: (0, 0))],
        out_shape=[tok(jnp.int32), tok(F32), tok(jnp.int32), jax.ShapeDtypeStruct((ne, nt), F32)],
        scratch_shapes=[pltpu.VMEM((ne, nt), F32)],
        compiler_params=_params(1),
        name="route",
    )(scores_t, eb)


def _dispatch_kernel(dest_ref, h_ref, xs_ref, sem):
    i = pl.program_id(0)
    nt = h_ref.shape[0]

    def row_copy(t, d):
        return pltpu.make_async_copy(h_ref.at[pl.ds(t, 1), :], xs_ref.at[pl.ds(d, 1), :], sem)

    def start_body(t, carry):
        base = (i * nt + t) * TOP_K
        for k in range(TOP_K):
            row_copy(t, dest_ref[base + k]).start()
        return carry

    lax.fori_loop(0, nt, start_body, 0)

    def wait_body(t, carry):
        for k in range(TOP_K):
            row_copy(t, 0).wait()
        return carry

    lax.fori_loop(0, nt, wait_body, 0)


def _dispatch(dest_flat, h2, n_rows):
    t = h2.shape[0]
    nt = DISPATCH_TILE
    return pl.pallas_call(
        _dispatch_kernel,
        grid_spec=pltpu.PrefetchScalarGridSpec(
            num_scalar_prefetch=1,
            grid=(t // nt,),
            in_specs=[pl.BlockSpec((nt, D_MODEL), lambda i, d: (i, 0))],
            out_specs=pl.BlockSpec(memory_space=pl.ANY),
            scratch_shapes=[pltpu.SemaphoreType.DMA],
        ),
        out_shape=jax.ShapeDtypeStruct((n_rows, D_MODEL), F32),
        compiler_params=_params(1),
        name="dispatch",
    )(dest_flat, h2)


def _expert_kernel(te_ref, nu_ref, xs_ref, w1_ref, w3_ref, w2_ref, ys_ref, w1b, w3b, w2b):
    i = pl.program_id(0)
    new_expert = (i == 0) | (te_ref[i] != te_ref[jnp.maximum(i - 1, 0)])

    @pl.when(new_expert)
    def _cast_weights():
        w1b[...] = w1_ref[...].astype(BF16)
        w3b[...] = w3_ref[...].astype(BF16)
        w2b[...] = w2_ref[...].astype(BF16)

    @pl.when(i < nu_ref[0])
    def _compute():
        xb = xs_ref[...].astype(BF16)
        a = jnp.dot(xb, w1b[...], preferred_element_type=F32)
        u = jnp.dot(xb, w3b[...], preferred_element_type=F32)
        ys_ref[...] = jnp.dot((_silu(a) * u).astype(BF16), w2b[...], preferred_element_type=F32)


def _experts(tile_expert, n_used, xs, w1, w3, w2, layer):
    n_rows = xs.shape[0]
    r = EXPERT_TILE
    row_map = lambda i, te, nu: (jnp.minimum(i, nu[0] - 1), 0)
    w_map = lambda i, te, nu: (layer, te[i], 0, 0)
    return pl.pallas_call(
        _expert_kernel,
        grid_spec=pltpu.PrefetchScalarGridSpec(
            num_scalar_prefetch=2,
            grid=(n_rows // r,),
            in_specs=[pl.BlockSpec((r, D_MODEL), row_map),
                      pl.BlockSpec((None, None, D_MODEL, EXPERT_FF), w_map),
                      pl.BlockSpec((None, None, D_MODEL, EXPERT_FF), w_map),
                      pl.BlockSpec((None, None, EXPERT_FF, D_MODEL), w_map)],
            out_specs=pl.BlockSpec((r, D_MODEL), row_map),
            scratch_shapes=[pltpu.VMEM((D_MODEL, EXPERT_FF), BF16),
                            pltpu.VMEM((D_MODEL, EXPERT_FF), BF16),
                            pltpu.VMEM((EXPERT_FF, D_MODEL), BF16)],
        ),
        out_shape=jax.ShapeDtypeStruct((n_rows, D_MODEL), F32),
        compiler_params=_params(1),
        name="experts",
    )(tile_expert, n_used, xs, w1, w3, w2)


def _combine_kernel(dest_ref, base_ref, g2_ref, w_ref, ys_ref, o_ref, buf, sem):
    i = pl.program_id(0)
    nt = base_ref.shape[0]

    def row_copy(t, k, d):
        return pltpu.make_async_copy(ys_ref.at[pl.ds(d, 1), :], buf.at[k, pl.ds(t, 1), :], sem)

    def start_body(t, carry):
        base = (i * nt + t) * TOP_K
        for k in range(TOP_K):
            row_copy(t, k, dest_ref[base + k]).start()
        return carry

    lax.fori_loop(0, nt, start_body, 0)

    def wait_body(t, carry):
        for k in range(TOP_K):
            row_copy(t, k, 0).wait()
        return carry

    lax.fori_loop(0, nt, wait_body, 0)

    acc = w_ref[:, 0:1] * buf[0]
    for k in range(1, TOP_K):
        acc = acc + w_ref[:, k:k + 1] * buf[k]
    o_ref[...] = base_ref[...] + g2_ref[...] * acc


def _combine(dest_flat, base, g2, w_tok, ys, seq):
    t = base.shape[0]
    nt = COMBINE_TILE
    tpb = seq // nt
    return pl.pallas_call(
        _combine_kernel,
        grid_spec=pltpu.PrefetchScalarGridSpec(
            num_scalar_prefetch=1,
            grid=(t // nt,),
            in_specs=[pl.BlockSpec((nt, D_MODEL), lambda i, d: (i, 0)),
                      pl.BlockSpec((None, 1, D_MODEL), lambda i, d: (i // tpb, 0, 0)),
                      pl.BlockSpec((nt, TOP_K), lambda i, d: (i, 0)),
                      pl.BlockSpec(memory_space=pl.ANY)],
            out_specs=pl.BlockSpec((nt, D_MODEL), lambda i, d: (i, 0)),
            scratch_shapes=[pltpu.VMEM((TOP_K, nt, D_MODEL), F32),
                            pltpu.SemaphoreType.DMA],
        ),
        out_shape=jax.ShapeDtypeStruct((t, D_MODEL), F32),
        compiler_params=_params(1),
        name="combine",
    )(dest_flat, base, g2, w_tok, ys)


def _layer(layer, x, c, w_ada, b_ada, norm1_w, norm2_w, w_in, q_norm_w, k_norm_w, rel_bias, w_alpha, b_alpha,
           moba_out_w, gla_out_w, w_out, w_router, e_bias, w1, w3, w2, ws1, ws3, ws2):
    b, s, d = x.shape
    t = b * s
    x2 = x.reshape(t, d)

    mod = _mod(c, w_ada, b_ada)
    sh1, sc1, g1, sh2, sc2, g2 = [mod[:, j * d:(j + 1) * d].reshape(b, 1, d) for j in range(6)]

    w_main = w_in[:, :D_MAIN].astype(BF16)
    w_ga = jnp.zeros((d, LANES), BF16).at[:, :GLA_GATE_RANK].set(w_in[:, D_MAIN:].astype(BF16))
    per_chunk = 256 // MOBA_HEAD_DIM
    qw = jnp.tile(q_norm_w.astype(F32), per_chunk).reshape(1, 256) * (MOBA_HEAD_DIM ** -0.5)
    kw = jnp.tile(k_norm_w.astype(F32), per_chunk).reshape(1, 256)
    proj, ga = _inproj(x2, sc1, sh1, norm1_w.reshape(1, d), w_main, w_ga, qw, kw, s)
    proj3 = proj.reshape(b, s, D_MAIN)

    own, prev, far = _moba_bias_tables(rel_bias)
    ow = jnp.tile(moba_out_w.astype(F32), 2).reshape(1, LANES)
    o_a = _moba(proj3, own, prev, far, ow)

    wal = jnp.zeros((LANES, GLA_KEY_WIDTH), F32).at[:GLA_GATE_RANK].set(w_alpha)
    o_b = _gla(proj3, ga.reshape(b, s, LANES), wal, b_alpha.reshape(1, GLA_KEY_WIDTH),
               gla_out_w.reshape(1, GLA_DV))

    base, h2, scores_t = _outproj(
        o_a.reshape(t, MOBA_WIDTH), o_b.reshape(t, GLA_WIDTH), x2, g1, sc2, sh2, g2,
        norm2_w.reshape(1, d), w_out.astype(BF16), ws1.astype(BF16), ws3.astype(BF16), ws2.astype(BF16),
        w_router.T.astype(BF16), s)

    eb = jnp.broadcast_to(e_bias.astype(F32)[:, None], (N_EXPERTS, ROUTE_TILE))
    idx_t, w_t, rank_t, counts = _route(scores_t, eb)

    r = EXPERT_TILE
    n_tiles = (t * TOP_K + N_EXPERTS * (r - 1) + r - 1) // r
    n_rows = n_tiles * r
    cnt = counts[:, 0].astype(jnp.int32)
    padded = (cnt + r - 1) // r * r
    pend = jnp.cumsum(padded)
    pstart = pend - padded
    dest = (jnp.take(pstart, idx_t) + rank_t).T.reshape(t * TOP_K)
    tile_expert = jnp.minimum(
        jnp.searchsorted(pend, jnp.arange(n_tiles, dtype=jnp.int32) * r, side='right'), N_EXPERTS - 1
    ).astype(jnp.int32)
    n_used = (pend[-1:] // r).astype(jnp.int32)

    xs = _dispatch(dest, h2, n_rows)
    ys = _experts(tile_expert, n_used, xs, w1, w3, w2, layer)
    out = _combine(dest, base, g2, w_t.T, ys, s)
    return out.reshape(b, s, d)


def kernel(x, c, w_ada, b_ada, norm1_w, norm2_w, w_in, q_norm_w, k_norm_w, rel_bias, w_alpha, b_alpha,
           moba_out_w, gla_out_w, w_out, w_router, e_bias, w1, w3, w2, ws1, ws3, ws2):
    for l in range(w_ada.shape[0]):
        x = _layer(l, x, c, w_ada[l], b_ada[l], norm1_w[l], norm2_w[l], w_in[l], q_norm_w[l], k_norm_w[l],
                   rel_bias, w_alpha[l], b_alpha[l], moba_out_w[l], gla_out_w[l], w_out[l], w_router[l],
                   e_bias[l], w1, w3, w2, ws1[l], ws3[l], ws2[l])
    return x
```

```python
import functools
import math

import numpy as np
import jax
import jax.numpy as jnp
from jax import lax
from jax.experimental import pallas as pl
from jax.experimental.pallas import tpu as pltpu

D_MODEL = 1024
MOBA_HEADS = 8
MOBA_HEAD_DIM = 64
MOBA_WIDTH = MOBA_HEADS * MOBA_HEAD_DIM
MOBA_BLOCK = 256
MOBA_TOPK = 3
GLA_HEADS = 4
GLA_DK = 64
GLA_DV = 128
GLA_KEY_WIDTH = GLA_HEADS * GLA_DK
GLA_WIDTH = GLA_HEADS * GLA_DV
GLA_GATE_RANK = 16
GLA_GATE_TAU = 16.0
GLA_CHUNK = 64
REL_BUCKETS = 32
REL_MAX_DIST = 128
N_EXPERTS = 256
TOP_K = 8
N_GROUPS = 8
TOPK_GROUPS = 4
GROUP_SIZE = N_EXPERTS // N_GROUPS
EXPERT_FF = 256
SHARED_FF = 256
ROUTED_SCALE = 2.5
NORM_EPS = 1e-6

D_MAIN = 3 * MOBA_WIDTH + 2 * GLA_KEY_WIDTH + 2 * GLA_WIDTH
LANES = 128
VMEM_LIMIT = 56 * 1024 * 1024

ROW_TILE = 512
ROUTE_TILE = 256
DISPATCH_TILE = 256
COMBINE_TILE = 128
EXPERT_TILE = 256

F32 = jnp.float32
BF16 = jnp.bfloat16
NT_DIMS = (((1,), (1,)), ((), ()))
TN_DIMS = (((0,), (0,)), ((), ()))


def _params(n_axes):
    return pltpu.CompilerParams(dimension_semantics=("arbitrary",) * n_axes,
                                vmem_limit_bytes=VMEM_LIMIT)


def _silu(v):
    return v * jax.nn.sigmoid(v)


def _mod_kernel(c_ref, w_ref, b_ref, o_ref):
    o_ref[...] = jnp.dot(_silu(c_ref[...]), w_ref[...], preferred_element_type=F32) + b_ref[...]


def _mod(c, w, b):
    rows = 8
    cp = jnp.zeros((rows, D_MODEL), F32).at[:c.shape[0]].set(c)
    n = w.shape[1]
    tn = 1024
    out = pl.pallas_call(
        _mod_kernel,
        grid=(n // tn,),
        in_specs=[pl.BlockSpec((rows, D_MODEL), lambda j: (0, 0)),
                  pl.BlockSpec((D_MODEL, tn), lambda j: (0, j)),
                  pl.BlockSpec((1, tn), lambda j: (0, j))],
        out_specs=pl.BlockSpec((rows, tn), lambda j: (0, j)),
        out_shape=jax.ShapeDtypeStruct((rows, n), F32),
        compiler_params=_params(1),
        name="mod",
    )(cp, w, b.reshape(1, n))
    return out[:c.shape[0]]


def _group_rms_inv(a, group):
    lane = lax.broadcasted_iota(jnp.int32, (1, a.shape[1]), 1)
    a2 = a * a
    inv = jnp.zeros_like(a)
    for g in range(a.shape[1] // group):
        m = (lane >= g * group) & (lane < (g + 1) * group)
        ss = jnp.sum(jnp.where(m, a2, 0.0), axis=-1, keepdims=True)
        inv = jnp.where(m, lax.rsqrt(ss * (1.0 / group) + NORM_EPS), inv)
    return inv


def _inproj_kernel(x_ref, sc_ref, sh_ref, nw_ref, w_ref, wga_ref, qw_ref, kw_ref, o_ref, ga_ref):
    x = x_ref[...]
    ms = jnp.mean(x * x, axis=-1, keepdims=True)
    h = x * lax.rsqrt(ms + NORM_EPS) * nw_ref[...]
    h = h * (1.0 + sc_ref[...]) + sh_ref[...]
    hb = h.astype(BF16)
    cw = 256
    for j in range(D_MAIN // cw):
        acc = jnp.dot(hb, w_ref[:, j * cw:(j + 1) * cw], preferred_element_type=F32)
        if j < 2 * MOBA_WIDTH // cw:
            nw = qw_ref if j < MOBA_WIDTH // cw else kw_ref
            acc = acc * _group_rms_inv(acc, MOBA_HEAD_DIM) * nw[...]
        o_ref[:, j * cw:(j + 1) * cw] = acc.astype(BF16)
    ga_ref[...] = jnp.dot(hb, wga_ref[...], preferred_element_type=F32)


def _inproj(x2, sc, sh, nw, w_main, w_ga, qw, kw, seq):
    t = x2.shape[0]
    tpb = seq // ROW_TILE
    vec = lambda: pl.BlockSpec((None, 1, D_MODEL), lambda i: (i // tpb, 0, 0))
    full = lambda a: pl.BlockSpec(a.shape, lambda i: (0,) * a.ndim)
    return pl.pallas_call(
        _inproj_kernel,
        grid=(t // ROW_TILE,),
        in_specs=[pl.BlockSpec((ROW_TILE, D_MODEL), lambda i: (i, 0)), vec(), vec(),
                  full(nw), full(w_main), full(w_ga), full(qw), full(kw)],
        out_specs=[pl.BlockSpec((ROW_TILE, D_MAIN), lambda i: (i, 0)),
                   pl.BlockSpec((ROW_TILE, LANES), lambda i: (i, 0))],
        out_shape=[jax.ShapeDtypeStruct((t, D_MAIN), BF16),
                   jax.ShapeDtypeStruct((t, LANES), F32)],
        compiler_params=_params(1),
        name="inproj",
    )(x2, sc, sh, nw, w_main, w_ga, qw, kw)


def _t5_bucket_np(rel):
    max_exact = REL_BUCKETS // 2
    relf = np.maximum(rel, 1).astype(np.float64)
    large = max_exact + (np.log(relf / max_exact) / math.log(REL_MAX_DIST / max_exact)
                         * (REL_BUCKETS - max_exact)).astype(np.int32)
    large = np.minimum(large, REL_BUCKETS - 1)
    return np.where(rel < max_exact, rel, large)


def _bias_kernel(rb_ref, idx_ref, o_ref):
    h = pl.program_id(0)
    idx = idx_ref[...]
    tab = jnp.full(idx.shape, -jnp.inf, F32)
    for bk in range(REL_BUCKETS):
        tab = jnp.where(idx == bk, rb_ref[bk * MOBA_HEADS + h], tab)
    o_ref[...] = tab


def _moba_bias_tables(rel_bias):
    j = np.arange(MOBA_BLOCK)[:, None]
    i = np.arange(MOBA_BLOCK)[None, :]
    own_idx = np.where(j <= i, _t5_bucket_np(np.maximum(i - j, 0)), -1)
    prev_idx = _t5_bucket_np(MOBA_BLOCK + i - j)
    idx = jnp.asarray(np.concatenate([prev_idx, own_idx], axis=0).astype(np.int32))
    assert int(_t5_bucket_np(np.array([MOBA_BLOCK + 1]))[0]) == REL_BUCKETS - 1
    rb = rel_bias.astype(F32)
    near = pl.pallas_call(
        _bias_kernel,
        grid=(MOBA_HEADS,),
        in_specs=[pl.BlockSpec(memory_space=pltpu.SMEM),
                  pl.BlockSpec(idx.shape, lambda h: (0, 0))],
        out_specs=pl.BlockSpec((None,) + idx.shape, lambda h: (h, 0, 0)),
        out_shape=jax.ShapeDtypeStruct((MOBA_HEADS,) + idx.shape, F32),
        compiler_params=_params(1),
        name="bias",
    )(rb.reshape(-1), idx)
    return near, rb[REL_BUCKETS - 1]


FAR_GROUP = 4


def _moba_kernel(far_ref, q_ref, k_ref, v_ref, near_ref, ow_ref, o_ref,
                 kmean_ref, vt_ref, vtg_ref, acc_ref, m_ref, sel_ref):
    hp = pl.program_id(1)
    i = pl.program_id(2)
    nblk = k_ref.shape[0] // MOBA_BLOCK
    hd = MOBA_HEAD_DIM
    bs = MOBA_BLOCK

    @pl.when(i == 0)
    def _prepare():
        row = lax.broadcasted_iota(jnp.int32, (LANES, MOBA_BLOCK), 0)
        for n in range(nblk):
            kb = k_ref[n * MOBA_BLOCK:(n + 1) * MOBA_BLOCK, :].astype(F32)
            kmean_ref[n:n + 1, :] = jnp.mean(kb, axis=0, keepdims=True)
            vt = v_ref[n * MOBA_BLOCK:(n + 1) * MOBA_BLOCK, :].astype(F32).T
            vt0 = jnp.where(row < hd, vt, 1.0).astype(BF16)
            vt1 = jnp.where(row < hd, 1.0, vt).astype(BF16)
            vt_ref[0, n] = vt0
            vt_ref[1, n] = vt1
            gcols = slice((n % FAR_GROUP) * bs, (n % FAR_GROUP + 1) * bs)
            vtg_ref[0, n // FAR_GROUP, :, gcols] = vt0
            vtg_ref[1, n // FAR_GROUP, :, gcols] = vt1

    q = q_ref[...]
    lane = lax.broadcasted_iota(jnp.int32, q.shape, 1)
    zero = jnp.zeros_like(q)
    qh = (jnp.where(lane < hd, q, zero), jnp.where(lane < hd, zero, q))

    blk = lax.broadcasted_iota(jnp.int32, (nblk, MOBA_BLOCK), 0)
    for h in range(2):
        gt = lax.dot_general(kmean_ref[...], qh[h].astype(F32), NT_DIMS, preferred_element_type=F32)
        gt = jnp.where(blk < i, gt, -jnp.inf)
        cnt = jnp.zeros(gt.shape, jnp.int32)
        for m in range(nblk):
            gm = gt[m:m + 1, :]
            cnt = cnt + jnp.where((gm > gt) | ((gm == gt) & (blk > m)), 1, 0)
        keep = (blk < i) & (cnt < MOBA_TOPK)
        sel_ref[0, h] = jnp.where(keep, 1.0, 0.0)
        sel_ref[1, h] = jnp.where(keep & (blk < i - 1), 1.0, 0.0)

    @pl.when(i == 0)
    def _own_block_only():
        kb = k_ref[0:bs, :]
        for h in range(2):
            s = lax.dot_general(kb, qh[h], NT_DIMS, preferred_element_type=F32) + near_ref[h, bs:2 * bs, :]
            m_new = jnp.max(s, axis=0, keepdims=True)
            p = jnp.exp(s - m_new).astype(BF16)
            acc_ref[h] = jnp.dot(vt_ref[h, 0], p, preferred_element_type=F32)
            m_ref[h] = m_new

    @pl.when(i >= 1)
    def _previous_and_own_block():
        k_prev = k_ref[pl.ds(pl.multiple_of((i - 1) * bs, bs), bs), :]
        k_own = k_ref[pl.ds(pl.multiple_of(i * bs, bs), bs), :]
        for h in range(2):
            s_own = lax.dot_general(k_own, qh[h], NT_DIMS, preferred_element_type=F32) + near_ref[h, bs:2 * bs, :]
            m_own = jnp.max(s_own, axis=0, keepdims=True)
            pv_own = jnp.dot(vt_ref[h, i], jnp.exp(s_own - m_own).astype(BF16), preferred_element_type=F32)
            s_prev = lax.dot_general(k_prev, qh[h], NT_DIMS, preferred_element_type=F32) + near_ref[h, 0:bs, :]
            keep = sel_ref[0, h, pl.ds(i - 1, 1), :] > 0.5
            mx = jnp.max(s_prev, axis=0, keepdims=True)
            p_prev = jnp.exp(s_prev - jnp.where(keep, mx, jnp.inf)).astype(BF16)
            pv_prev = jnp.dot(vt_ref[h, i - 1], p_prev, preferred_element_type=F32)
            m_new = jnp.maximum(m_own, jnp.where(keep, mx, -jnp.inf))
            acc_ref[h] = (pv_own * jnp.exp(m_own - m_new)
                          + pv_prev * jnp.exp(jnp.where(keep, mx, -jnp.inf) - m_new))
            m_ref[h] = m_new

    def far_body(g, carry):
        gk = FAR_GROUP * bs
        kb = k_ref[pl.ds(pl.multiple_of(g * gk, gk), gk), :]
        ss = [lax.dot_general(kb, qh[h], NT_DIMS, preferred_element_type=F32) for h in range(2)]
        for h in range(2):
            s = ss[h]
            fb = far_ref[2 * hp + h]
            m_old = m_ref[h]
            m_new = m_old
            keeps = []
            for j in range(FAR_GROUP):
                keep = sel_ref[1, h, pl.ds(g * FAR_GROUP + j, 1), :] > 0.5
                mx = jnp.max(s[j * bs:(j + 1) * bs], axis=0, keepdims=True) + fb
                m_new = jnp.maximum(m_new, jnp.where(keep, mx, -jnp.inf))
                keeps.append(keep)
            p = jnp.concatenate(
                [jnp.exp(s[j * bs:(j + 1) * bs] - jnp.where(keeps[j], m_new - fb, jnp.inf)).astype(BF16)
                 for j in range(FAR_GROUP)], axis=0)
            pv = jnp.dot(vtg_ref[h, g], p, preferred_element_type=F32)
            acc_ref[h] = acc_ref[h] * jnp.exp(m_old - m_new) + pv
            m_ref[h] = m_new
        return carry

    lax.fori_loop(0, (i + FAR_GROUP - 2) // FAR_GROUP, far_body, 0)

    a0 = acc_ref[0]
    a1 = acc_ref[1]
    row = lax.broadcasted_iota(jnp.int32, a0.shape, 0)
    ot = jnp.where(row < hd, a0 / a0[hd:hd + 1, :], a1 / a1[0:1, :])
    o2 = ot * ot
    ss0 = jnp.sum(jnp.where(row < hd, o2, 0.0), axis=0, keepdims=True)
    ss1 = jnp.sum(jnp.where(row < hd, 0.0, o2), axis=0, keepdims=True)
    inv = jnp.where(row < hd, lax.rsqrt(ss0 * (1.0 / hd) + NORM_EPS), lax.rsqrt(ss1 * (1.0 / hd) + NORM_EPS))
    o_ref[...] = ((ot * inv).T * ow_ref[...]).astype(o_ref.dtype)


def _moba(proj3, near, far, ow):
    b, s, _ = proj3.shape
    nblk = s // MOBA_BLOCK
    assert nblk % FAR_GROUP == 0
    npair = MOBA_HEADS // 2
    kcol = MOBA_WIDTH // LANES
    return pl.pallas_call(
        _moba_kernel,
        grid=(b, npair, nblk),
        in_specs=[pl.BlockSpec(memory_space=pltpu.SMEM),
                  pl.BlockSpec((None, MOBA_BLOCK, LANES), lambda bb, hp, i: (bb, i, hp)),
                  pl.BlockSpec((None, s, LANES), lambda bb, hp, i: (bb, 0, kcol + hp)),
                  pl.BlockSpec((None, s, LANES), lambda bb, hp, i: (bb, 0, 2 * kcol + hp)),
                  pl.BlockSpec((2, 2 * MOBA_BLOCK, MOBA_BLOCK), lambda bb, hp, i: (hp, 0, 0)),
                  pl.BlockSpec((1, LANES), lambda bb, hp, i: (0, 0))],
        out_specs=pl.BlockSpec((None, MOBA_BLOCK, LANES), lambda bb, hp, i: (bb, i, hp)),
        out_shape=jax.ShapeDtypeStruct((b, s, MOBA_WIDTH), BF16),
        scratch_shapes=[pltpu.VMEM((nblk, LANES), F32),
                        pltpu.VMEM((2, nblk, LANES, MOBA_BLOCK), BF16),
                        pltpu.VMEM((2, nblk // FAR_GROUP, LANES, FAR_GROUP * MOBA_BLOCK), BF16),
                        pltpu.VMEM((2, LANES, MOBA_BLOCK), F32),
                        pltpu.VMEM((2, 1, MOBA_BLOCK), F32),
                        pltpu.VMEM((2, 2, nblk, MOBA_BLOCK), F32)],
        compiler_params=_params(3),
        name="moba",
    )(far, proj3, proj3, proj3, near, ow)


def _split3(v):
    hi = v.astype(BF16)
    r1 = v - hi.astype(F32)
    mid = r1.astype(BF16)
    lo = (r1 - mid.astype(F32)).astype(BF16)
    return hi, mid, lo


def _gla_kernel(q_ref, k_ref, v_ref, g_ref, ga_ref, wal_ref, bal_ref, gw_ref, o_ref, b_ref, st_ref):
    seq = q_ref.shape[0]
    c = GLA_CHUNK
    pc = 256

    rr = lax.broadcasted_iota(jnp.int32, (pc, pc), 0)
    cc = lax.broadcasted_iota(jnp.int32, (pc, pc), 1)
    tri = jnp.where((rr >= cc) & (rr // c == cc // c), 1.0, 0.0).astype(BF16)

    def decay_body(j, carry):
        r0 = pl.multiple_of(j * pc, pc)
        xg = jnp.dot(ga_ref[pl.ds(r0, pc), :], wal_ref[...], preferred_element_type=F32) + bal_ref[...]
        la = (jnp.minimum(xg, 0.0) - jnp.log(1.0 + jnp.exp(-jnp.abs(xg)))) * (1.0 / GLA_GATE_TAU)
        hi, mid, lo = _split3(la)
        b_ref[pl.ds(r0, pc), :] = (jnp.dot(tri, hi, preferred_element_type=F32)
                                   + jnp.dot(tri, mid, preferred_element_type=F32)
                                   + jnp.dot(tri, lo, preferred_element_type=F32))
        return carry

    lax.fori_loop(0, seq // pc, decay_body, 0)

    st_ref[...] = jnp.zeros_like(st_ref)
    lane = lax.broadcasted_iota(jnp.int32, (c, LANES), 1)
    head_mask = (lane < GLA_DK, lane >= GLA_DK)
    causal = lax.broadcasted_iota(jnp.int32, (c, c), 0) >= lax.broadcasted_iota(jnp.int32, (c, c), 1)

    def chunk_body(ci, carry):
        r0 = pl.multiple_of(ci * c, c)
        b = b_ref[pl.ds(r0, c), :]
        ref_row = b[c // 2 - 1:c // 2, :]
        last = b[c - 1:c, :]
        q = q_ref[pl.ds(r0, c), :].astype(F32) * (GLA_DK ** -0.5)
        k = k_ref[pl.ds(r0, c), :].astype(F32)
        qt = q * jnp.exp(b - ref_row)
        kt = (k * jnp.exp(ref_row - b)).astype(BF16)
        qs = q * jnp.exp(b)
        ke = (k * jnp.exp(last - b)).astype(BF16)
        e_last = jnp.exp(last)
        for h in range(2):
            cols = slice(h * GLA_DV, (h + 1) * GLA_DV)
            a = lax.dot_general(jnp.where(head_mask[h], qt, 0.0).astype(BF16), kt, NT_DIMS,
                                preferred_element_type=F32)
            a = jnp.where(causal, a, 0.0).astype(BF16)
            v = v_ref[pl.ds(r0, c), cols]
            st = st_ref[h]
            o = jnp.dot(a, v, preferred_element_type=F32)
            o = o + lax.dot_general(jnp.where(head_mask[h], qs, 0.0).astype(BF16), st.astype(BF16), NT_DIMS,
                                    preferred_element_type=F32)
            st_ref[h] = st * e_last + lax.dot_general(v, ke, TN_DIMS, preferred_element_type=F32)
            ms = jnp.mean(o * o, axis=-1, keepdims=True)
            on = o * lax.rsqrt(ms + NORM_EPS) * gw_ref[...]
            g = g_ref[pl.ds(r0, c), cols].astype(F32)
            o_ref[pl.ds(r0, c), cols] = (on * _silu(g)).astype(o_ref.dtype)
        return carry

    lax.fori_loop(0, seq // c, chunk_body, 0)


def _gla(proj3, ga3, wal, bal, gw):
    b, s, _ = proj3.shape
    npair = GLA_HEADS // 2
    qcol = 3 * MOBA_WIDTH // LANES
    kcol = qcol + GLA_KEY_WIDTH // LANES
    vcol = (3 * MOBA_WIDTH + 2 * GLA_KEY_WIDTH) // (2 * GLA_DV)
    gcol = vcol + npair
    return pl.pallas_call(
        _gla_kernel,
        grid=(b, npair),
        in_specs=[pl.BlockSpec((None, s, LANES), lambda bb, hp: (bb, 0, qcol + hp)),
                  pl.BlockSpec((None, s, LANES), lambda bb, hp: (bb, 0, kcol + hp)),
                  pl.BlockSpec((None, s, 2 * GLA_DV), lambda bb, hp: (bb, 0, vcol + hp)),
                  pl.BlockSpec((None, s, 2 * GLA_DV), lambda bb, hp: (bb, 0, gcol + hp)),
                  pl.BlockSpec((None, s, LANES), lambda bb, hp: (bb, 0, 0)),
                  pl.BlockSpec((LANES, LANES), lambda bb, hp: (0, hp)),
                  pl.BlockSpec((1, LANES), lambda bb, hp: (0, hp)),
                  pl.BlockSpec((1, GLA_DV), lambda bb, hp: (0, 0))],
        out_specs=pl.BlockSpec((None, s, 2 * GLA_DV), lambda bb, hp: (bb, 0, hp)),
        out_shape=jax.ShapeDtypeStruct((b, s, GLA_WIDTH), BF16),
        scratch_shapes=[pltpu.VMEM((s, LANES), F32),
                        pltpu.VMEM((2, GLA_DV, LANES), F32)],
        compiler_params=_params(2),
        name="gla",
    )(proj3, proj3, proj3, proj3, ga3, wal, bal, gw)


HALF = D_MODEL // 2


def _pack_rows(v):
    def bf16_bits(a):
        u = lax.bitcast_convert_type(a, jnp.uint32)
        return (u + (jnp.uint32(0x7FFF) + ((u >> 16) & jnp.uint32(1)))) >> 16
    return bf16_bits(v[:, :HALF]) | (bf16_bits(v[:, HALF:]) << 16)


def _unpack_rows(w):
    return (lax.bitcast_convert_type(w << 16, F32),
            lax.bitcast_convert_type(w & jnp.uint32(0xFFFF0000), F32))


def _outproj_kernel(oa_ref, ob_ref, x_ref, g1_ref, sc_ref, sh_ref, g2_ref, nw_ref, wo_ref,
                    ws1_ref, ws3_ref, ws2_ref, wrt_ref, base_ref, h_ref, st_ref):
    mix = (jnp.dot(oa_ref[...], wo_ref[:MOBA_WIDTH, :], preferred_element_type=F32)
           + jnp.dot(ob_ref[...], wo_ref[MOBA_WIDTH:, :], preferred_element_type=F32))
    x1 = x_ref[...] + g1_ref[...] * mix
    ms = jnp.mean(x1 * x1, axis=-1, keepdims=True)
    h = x1 * lax.rsqrt(ms + NORM_EPS) * nw_ref[...]
    h = h * (1.0 + sc_ref[...]) + sh_ref[...]
    h_ref[...] = _pack_rows(h)
    hb = h.astype(BF16)
    a = jnp.dot(hb, ws1_ref[...], preferred_element_type=F32)
    u = jnp.dot(hb, ws3_ref[...], preferred_element_type=F32)
    shared = jnp.dot((_silu(a) * u).astype(BF16), ws2_ref[...], preferred_element_type=F32)
    base_ref[...] = x1 + g2_ref[...] * shared
    logits_t = lax.dot_general(wrt_ref[...], hb, NT_DIMS, preferred_element_type=F32)
    st_ref[...] = jax.nn.sigmoid(logits_t)


def _outproj(oa, ob, x2, g1, sc, sh, g2, nw, wo, ws1, ws3, ws2, wrt, seq):
    t = x2.shape[0]
    tpb = seq // ROW_TILE
    vec = lambda: pl.BlockSpec((None, 1, D_MODEL), lambda i: (i // tpb, 0, 0))
    full = lambda a: pl.BlockSpec(a.shape, lambda i: (0,) * a.ndim)
    rows = lambda w: pl.BlockSpec((ROW_TILE, w), lambda i: (i, 0))
    return pl.pallas_call(
        _outproj_kernel,
        grid=(t // ROW_TILE,),
        in_specs=[rows(MOBA_WIDTH), rows(GLA_WIDTH), rows(D_MODEL), vec(), vec(), vec(), vec(),
                  full(nw), full(wo), full(ws1), full(ws3), full(ws2), full(wrt)],
        out_specs=[rows(D_MODEL), rows(HALF), pl.BlockSpec((N_EXPERTS, ROW_TILE), lambda i: (0, i))],
        out_shape=[jax.ShapeDtypeStruct((t, D_MODEL), F32),
                   jax.ShapeDtypeStruct((t, HALF), jnp.uint32),
                   jax.ShapeDtypeStruct((N_EXPERTS, t), F32)],
        compiler_params=_params(1),
        name="outproj",
    )(oa, ob, x2, g1, sc, sh, g2, nw, wo, ws1, ws3, ws2, wrt)


SLOT_CODE_BASE = 1 << 16


def _route_kernel(s_ref, eb_ref, code_ref, w_ref, cnt_ref, carry_ref):
    i = pl.program_id(0)
    ne, nt = s_ref.shape

    @pl.when(i == 0)
    def _init():
        carry_ref[...] = jnp.zeros_like(carry_ref)

    s = s_ref[...]
    choice = s + eb_ref[...]
    gio = lax.broadcasted_iota(jnp.int32, (GROUP_SIZE, nt), 0)
    gscore = []
    for g in range(N_GROUPS):
        cg = choice[g * GROUP_SIZE:(g + 1) * GROUP_SIZE, :]
        top1 = jnp.max(cg, axis=0, keepdims=True)
        first = jnp.min(jnp.where(cg == top1, gio, GROUP_SIZE), axis=0, keepdims=True)
        top2 = jnp.max(jnp.where(gio == first, -jnp.inf, cg), axis=0, keepdims=True)
        gscore.append(top1 + top2)
    gs = jnp.concatenate(gscore, axis=0)
    gidx = lax.broadcasted_iota(jnp.int32, gs.shape, 0)
    beaten = jnp.zeros(gs.shape, jnp.int32)
    for m in range(N_GROUPS):
        gm = gs[m:m + 1, :]
        beaten = beaten + jnp.where((gm > gs) | ((gm == gs) & (gidx > m)), 1, 0)
    gkeep = beaten < TOPK_GROUPS
    masked = jnp.concatenate(
        [jnp.where(gkeep[g:g + 1, :], choice[g * GROUP_SIZE:(g + 1) * GROUP_SIZE, :], -jnp.inf)
         for g in range(N_GROUPS)], axis=0)

    eio = lax.broadcasted_iota(jnp.int32, (ne, nt), 0)
    picked = jnp.zeros((ne, nt), F32)
    idx_rows, w_rows, hits = [], [], []
    for _ in range(TOP_K):
        mx = jnp.max(masked, axis=0, keepdims=True)
        idx = jnp.min(jnp.where(masked == mx, eio, ne), axis=0, keepdims=True)
        hit = eio == idx
        w_rows.append(jnp.sum(jnp.where(hit, s, 0.0), axis=0, keepdims=True))
        idx_rows.append(idx)
        hits.append(hit)
        masked = jnp.where(hit, -jnp.inf, masked)
        picked = jnp.where(hit, 1.0, picked)
    wk = jnp.concatenate(w_rows, axis=0)
    w_ref[...] = wk / jnp.sum(wk, axis=0, keepdims=True) * ROUTED_SCALE

    tr = lax.broadcasted_iota(jnp.int32, (nt, nt), 0)
    tc = lax.broadcasted_iota(jnp.int32, (nt, nt), 1)
    before = jnp.where(tr < tc, 1.0, 0.0).astype(BF16)
    pb = picked.astype(BF16)
    pos = carry_ref[...] + jnp.dot(pb, before, preferred_element_type=F32)
    rank = jnp.concatenate(
        [jnp.sum(jnp.where(hit, pos, 0.0), axis=0, keepdims=True) for hit in hits], axis=0).astype(jnp.int32)
    code_ref[...] = jnp.concatenate(idx_rows, axis=0) * SLOT_CODE_BASE + rank
    total = carry_ref[...] + jnp.dot(pb, jnp.ones((nt, nt), BF16), preferred_element_type=F32)
    carry_ref[...] = total
    cnt_ref[...] = total


def _route(scores_t, eb):
    ne, t = scores_t.shape
    assert t <= SLOT_CODE_BASE
    nt = ROUTE_TILE
    tok = lambda dt: jax.ShapeDtypeStruct((TOP_K, t), dt)
    return pl.pallas_call(
        _route_kernel,
        grid=(t // nt,),
        in_specs=[pl.BlockSpec((ne, nt), lambda i: (0, i)),
                  pl.BlockSpec((ne, nt), lambda i: (0, 0))],
        out_specs=[pl.BlockSpec((TOP_K, nt), lambda i: (0, i)),
                   pl.BlockSpec((TOP_K, nt), lambda i: (0, i)),
                   pl.BlockSpec((ne, nt), lambda i: (0, 0))],
        out_shape=[tok(jnp.int32), tok(F32), jax.ShapeDtypeStruct((ne, nt), F32)],
        scratch_shapes=[pltpu.VMEM((ne, nt), F32)],
        compiler_params=_params(1),
        name="route",
    )(scores_t, eb)


def _slot(code_ref, pstart_ref, j):
    code = code_ref[j]
    return pstart_ref[code // SLOT_CODE_BASE] + code % SLOT_CODE_BASE


def _dispatch_kernel(code_ref, pstart_ref, h_ref, xs_ref, sem):
    i = pl.program_id(0)
    nt = h_ref.shape[0]

    def row_copy(t, d):
        return pltpu.make_async_copy(h_ref.at[pl.ds(t, 1), :], xs_ref.at[pl.ds(d, 1), :], sem)

    def start_body(t, carry):
        base = (i * nt + t) * TOP_K
        for k in range(TOP_K):
            row_copy(t, _slot(code_ref, pstart_ref, base + k)).start(priority=k % 2)
        return carry

    lax.fori_loop(0, nt, start_body, 0)

    def wait_body(t, carry):
        for k in range(TOP_K):
            row_copy(t, 0).wait()
        return carry

    lax.fori_loop(0, nt, wait_body, 0)


def _dispatch(code_flat, pstart, h2, n_rows):
    t = h2.shape[0]
    nt = DISPATCH_TILE
    return pl.pallas_call(
        _dispatch_kernel,
        grid_spec=pltpu.PrefetchScalarGridSpec(
            num_scalar_prefetch=2,
            grid=(t // nt,),
            in_specs=[pl.BlockSpec((nt, HALF), lambda i, c, p: (i, 0))],
            out_specs=pl.BlockSpec(memory_space=pl.ANY),
            scratch_shapes=[pltpu.SemaphoreType.DMA],
        ),
        out_shape=jax.ShapeDtypeStruct((n_rows, HALF), jnp.uint32),
        compiler_params=_params(1),
        name="dispatch",
    )(code_flat, pstart, h2)


def _expert_kernel(te_ref, nu_ref, xs_ref, w1_ref, w3_ref, w2_ref, ys_ref, w1b, w3b, w2b):
    i = pl.program_id(0)
    new_expert = (i == 0) | (te_ref[i] != te_ref[jnp.maximum(i - 1, 0)])

    @pl.when(new_expert)
    def _cast_weights():
        w1b[...] = w1_ref[...].astype(BF16)
        w3b[...] = w3_ref[...].astype(BF16)
        w2b[...] = w2_ref[...].astype(BF16)

    @pl.when(i < nu_ref[0])
    def _compute():
        lo, hi = _unpack_rows(xs_ref[...])
        lo, hi = lo.astype(BF16), hi.astype(BF16)
        a = (jnp.dot(lo, w1b[:HALF, :], preferred_element_type=F32)
             + jnp.dot(hi, w1b[HALF:, :], preferred_element_type=F32))
        u = (jnp.dot(lo, w3b[:HALF, :], preferred_element_type=F32)
             + jnp.dot(hi, w3b[HALF:, :], preferred_element_type=F32))
        y = jnp.dot((_silu(a) * u).astype(BF16), w2b[...], preferred_element_type=F32)
        ys_ref[...] = _pack_rows(y)


def _experts(tile_expert, n_used, xs, w1, w3, w2, layer):
    n_rows = xs.shape[0]
    r = EXPERT_TILE
    row_map = lambda i, te, nu: (jnp.minimum(i, nu[0] - 1), 0)
    w_map = lambda i, te, nu: (layer, te[i], 0, 0)
    return pl.pallas_call(
        _expert_kernel,
        grid_spec=pltpu.PrefetchScalarGridSpec(
            num_scalar_prefetch=2,
            grid=(n_rows // r,),
            in_specs=[pl.BlockSpec((r, HALF), row_map),
                      pl.BlockSpec((None, None, D_MODEL, EXPERT_FF), w_map),
                      pl.BlockSpec((None, None, D_MODEL, EXPERT_FF), w_map),
                      pl.BlockSpec((None, None, EXPERT_FF, D_MODEL), w_map)],
            out_specs=pl.BlockSpec((r, HALF), row_map),
            scratch_shapes=[pltpu.VMEM((D_MODEL, EXPERT_FF), BF16),
                            pltpu.VMEM((D_MODEL, EXPERT_FF), BF16),
                            pltpu.VMEM((EXPERT_FF, D_MODEL), BF16)],
        ),
        out_shape=jax.ShapeDtypeStruct((n_rows, HALF), jnp.uint32),
        compiler_params=_params(1),
        name="experts",
    )(tile_expert, n_used, xs, w1, w3, w2)


def _combine_kernel(code_ref, pstart_ref, base_ref, g2_ref, w_ref, ys_ref, o_ref, buf, sem):
    i = pl.program_id(0)
    nt = base_ref.shape[0]

    def row_copy(t, k, d):
        return pltpu.make_async_copy(ys_ref.at[pl.ds(d, 1), :], buf.at[k, pl.ds(t, 1), :], sem)

    def start_body(t, carry):
        base = (i * nt + t) * TOP_K
        for k in range(TOP_K):
            row_copy(t, k, _slot(code_ref, pstart_ref, base + k)).start(priority=k % 2)
        return carry

    lax.fori_loop(0, nt, start_body, 0)

    def wait_body(t, carry):
        for k in range(TOP_K):
            row_copy(t, k, 0).wait()
        return carry

    lax.fori_loop(0, nt, wait_body, 0)

    acc_lo = acc_hi = None
    for k in range(TOP_K):
        lo, hi = _unpack_rows(buf[k])
        wk = w_ref[:, k:k + 1]
        acc_lo = wk * lo if acc_lo is None else acc_lo + wk * lo
        acc_hi = wk * hi if acc_hi is None else acc_hi + wk * hi
    o_ref[:, :HALF] = base_ref[:, :HALF] + g2_ref[:, :HALF] * acc_lo
    o_ref[:, HALF:] = base_ref[:, HALF:] + g2_ref[:, HALF:] * acc_hi


def _combine(code_flat, pstart, base, g2, w_tok, ys, seq):
    t = base.shape[0]
    nt = COMBINE_TILE
    tpb = seq // nt
    return pl.pallas_call(
        _combine_kernel,
        grid_spec=pltpu.PrefetchScalarGridSpec(
            num_scalar_prefetch=2,
            grid=(t // nt,),
            in_specs=[pl.BlockSpec((nt, D_MODEL), lambda i, c, p: (i, 0)),
                      pl.BlockSpec((None, 1, D_MODEL), lambda i, c, p: (i // tpb, 0, 0)),
                      pl.BlockSpec((nt, TOP_K), lambda i, c, p: (i, 0)),
                      pl.BlockSpec(memory_space=pl.ANY)],
            out_specs=pl.BlockSpec((nt, D_MODEL), lambda i, c, p: (i, 0)),
            scratch_shapes=[pltpu.VMEM((TOP_K, nt, HALF), jnp.uint32),
                            pltpu.SemaphoreType.DMA],
        ),
        out_shape=jax.ShapeDtypeStruct((t, D_MODEL), F32),
        compiler_params=_params(1),
        name="combine",
    )(code_flat, pstart, base, g2, w_tok, ys)


def _layer(layer, x, c, w_ada, b_ada, norm1_w, norm2_w, w_in, q_norm_w, k_norm_w, rel_bias, w_alpha, b_alpha,
           moba_out_w, gla_out_w, w_out, w_router, e_bias, w1, w3, w2, ws1, ws3, ws2):
    b, s, d = x.shape
    t = b * s
    x2 = x.reshape(t, d)

    mod = _mod(c, w_ada, b_ada)
    sh1, sc1, g1, sh2, sc2, g2 = [mod[:, j * d:(j + 1) * d].reshape(b, 1, d) for j in range(6)]

    w_main = w_in[:, :D_MAIN].astype(BF16)
    w_ga = jnp.zeros((d, LANES), BF16).at[:, :GLA_GATE_RANK].set(w_in[:, D_MAIN:].astype(BF16))
    per_chunk = 256 // MOBA_HEAD_DIM
    qw = jnp.tile(q_norm_w.astype(F32), per_chunk).reshape(1, 256) * (MOBA_HEAD_DIM ** -0.5)
    kw = jnp.tile(k_norm_w.astype(F32), per_chunk).reshape(1, 256)
    proj, ga = _inproj(x2, sc1, sh1, norm1_w.reshape(1, d), w_main, w_ga, qw, kw, s)
    proj3 = proj.reshape(b, s, D_MAIN)

    near, far = _moba_bias_tables(rel_bias)
    ow = jnp.tile(moba_out_w.astype(F32), 2).reshape(1, LANES)
    o_a = _moba(proj3, near, far, ow)

    wal = jnp.zeros((LANES, GLA_KEY_WIDTH), F32).at[:GLA_GATE_RANK].set(w_alpha)
    o_b = _gla(proj3, ga.reshape(b, s, LANES), wal, b_alpha.reshape(1, GLA_KEY_WIDTH),
               gla_out_w.reshape(1, GLA_DV))

    base, h2, scores_t = _outproj(
        o_a.reshape(t, MOBA_WIDTH), o_b.reshape(t, GLA_WIDTH), x2, g1, sc2, sh2, g2,
        norm2_w.reshape(1, d), w_out.astype(BF16), ws1.astype(BF16), ws3.astype(BF16), ws2.astype(BF16),
        w_router.T.astype(BF16), s)

    eb = jnp.broadcast_to(e_bias.astype(F32)[:, None], (N_EXPERTS, ROUTE_TILE))
    code_t, w_t, counts = _route(scores_t, eb)

    r = EXPERT_TILE
    n_tiles = (t * TOP_K + N_EXPERTS * (r - 1) + r - 1) // r
    n_rows = n_tiles * r
    cnt = counts[:, 0].astype(jnp.int32)
    padded = (cnt + r - 1) // r * r
    pend = jnp.cumsum(padded)
    pstart = pend - padded
    tile_row = jnp.arange(n_tiles, dtype=jnp.int32) * r
    tile_expert = jnp.minimum(jnp.sum((pend[None, :] <= tile_row[:, None]).astype(jnp.int32), axis=1),
                              N_EXPERTS - 1)
    n_used = (pend[-1:] // r).astype(jnp.int32)
    code = code_t.T.reshape(t * TOP_K)

    xs = _dispatch(code, pstart, h2, n_rows)
    ys = _experts(tile_expert, n_used, xs, w1, w3, w2, layer)
    out = _combine(code, pstart, base, g2, w_t.T, ys, s)
    return out.reshape(b, s, d)


def kernel(x, c, w_ada, b_ada, norm1_w, norm2_w, w_in, q_norm_w, k_norm_w, rel_bias, w_alpha, b_alpha,
           moba_out_w, gla_out_w, w_out, w_router, e_bias, w1, w3, w2, ws1, ws3, ws2):
    for l in range(w_ada.shape[0]):
        x = _layer(l, x, c, w_ada[l], b_ada[l], norm1_w[l], norm2_w[l], w_in[l], q_norm_w[l], k_norm_w[l],
                   rel_bias, w_alpha[l], b_alpha[l], moba_out_w[l], gla_out_w[l], w_out[l], w_router[l],
                   e_bias[l], w1, w3, w2, ws1[l], ws3[l], ws2[l])
    return x
```

```python
import functools
import math

import numpy as np
import jax
import jax.numpy as jnp
from jax import lax
from jax.experimental import pallas as pl
from jax.experimental.pallas import tpu as pltpu

D_MODEL = 1024
MOBA_HEADS = 8
MOBA_HEAD_DIM = 64
MOBA_WIDTH = MOBA_HEADS * MOBA_HEAD_DIM
MOBA_BLOCK = 256
MOBA_TOPK = 3
GLA_HEADS = 4
GLA_DK = 64
GLA_DV = 128
GLA_KEY_WIDTH = GLA_HEADS * GLA_DK
GLA_WIDTH = GLA_HEADS * GLA_DV
GLA_GATE_RANK = 16
GLA_GATE_TAU = 16.0
GLA_CHUNK = 64
REL_BUCKETS = 32
REL_MAX_DIST = 128
N_EXPERTS = 256
TOP_K = 8
N_GROUPS = 8
TOPK_GROUPS = 4
GROUP_SIZE = N_EXPERTS // N_GROUPS
EXPERT_FF = 256
SHARED_FF = 256
ROUTED_SCALE = 2.5
NORM_EPS = 1e-6

D_MAIN = 3 * MOBA_WIDTH + 2 * GLA_KEY_WIDTH + 2 * GLA_WIDTH
LANES = 128
VMEM_LIMIT = 56 * 1024 * 1024

ROW_TILE = 512
ROUTE_TILE = 256
DISPATCH_TILE = 256
COMBINE_TILE = 128
EXPERT_TILE = 256

F32 = jnp.float32
BF16 = jnp.bfloat16
NT_DIMS = (((1,), (1,)), ((), ()))
TN_DIMS = (((0,), (0,)), ((), ()))


def _params(n_axes):
    return pltpu.CompilerParams(dimension_semantics=("arbitrary",) * n_axes,
                                vmem_limit_bytes=VMEM_LIMIT)


def _silu(v):
    return v * jax.nn.sigmoid(v)


def _mod_kernel(c_ref, w_ref, b_ref, o_ref):
    o_ref[...] = jnp.dot(_silu(c_ref[...]), w_ref[...], preferred_element_type=F32) + b_ref[...]


def _mod(c, w, b):
    rows = 8
    cp = jnp.zeros((rows, D_MODEL), F32).at[:c.shape[0]].set(c)
    n = w.shape[1]
    tn = 1024
    out = pl.pallas_call(
        _mod_kernel,
        grid=(n // tn,),
        in_specs=[pl.BlockSpec((rows, D_MODEL), lambda j: (0, 0)),
                  pl.BlockSpec((D_MODEL, tn), lambda j: (0, j)),
                  pl.BlockSpec((1, tn), lambda j: (0, j))],
        out_specs=pl.BlockSpec((rows, tn), lambda j: (0, j)),
        out_shape=jax.ShapeDtypeStruct((rows, n), F32),
        compiler_params=_params(1),
        name="mod",
    )(cp, w, b.reshape(1, n))
    return out[:c.shape[0]]


def _group_rms_inv(a, group):
    lane = lax.broadcasted_iota(jnp.int32, (1, a.shape[1]), 1)
    a2 = a * a
    inv = jnp.zeros_like(a)
    for g in range(a.shape[1] // group):
        m = (lane >= g * group) & (lane < (g + 1) * group)
        ss = jnp.sum(jnp.where(m, a2, 0.0), axis=-1, keepdims=True)
        inv = jnp.where(m, lax.rsqrt(ss * (1.0 / group) + NORM_EPS), inv)
    return inv


def _inproj_kernel(x_ref, sc_ref, sh_ref, nw_ref, w_ref, wga_ref, qw_ref, kw_ref, o_ref, ga_ref):
    x = x_ref[...]
    ms = jnp.mean(x * x, axis=-1, keepdims=True)
    h = x * lax.rsqrt(ms + NORM_EPS) * nw_ref[...]
    h = h * (1.0 + sc_ref[...]) + sh_ref[...]
    hb = h.astype(BF16)
    cw = 256
    for j in range(D_MAIN // cw):
        acc = jnp.dot(hb, w_ref[:, j * cw:(j + 1) * cw], preferred_element_type=F32)
        if j < 2 * MOBA_WIDTH // cw:
            nw = qw_ref if j < MOBA_WIDTH // cw else kw_ref
            acc = acc * _group_rms_inv(acc, MOBA_HEAD_DIM) * nw[...]
        o_ref[:, j * cw:(j + 1) * cw] = acc.astype(BF16)
    ga_ref[...] = jnp.dot(hb, wga_ref[...], preferred_element_type=F32)


def _inproj(x2, sc, sh, nw, w_main, w_ga, qw, kw, seq):
    t = x2.shape[0]
    tpb = seq // ROW_TILE
    vec = lambda: pl.BlockSpec((None, 1, D_MODEL), lambda i: (i // tpb, 0, 0))
    full = lambda a: pl.BlockSpec(a.shape, lambda i: (0,) * a.ndim)
    return pl.pallas_call(
        _inproj_kernel,
        grid=(t // ROW_TILE,),
        in_specs=[pl.BlockSpec((ROW_TILE, D_MODEL), lambda i: (i, 0)), vec(), vec(),
                  full(nw), full(w_main), full(w_ga), full(qw), full(kw)],
        out_specs=[pl.BlockSpec((ROW_TILE, D_MAIN), lambda i: (i, 0)),
                   pl.BlockSpec((ROW_TILE, LANES), lambda i: (i, 0))],
        out_shape=[jax.ShapeDtypeStruct((t, D_MAIN), BF16),
                   jax.ShapeDtypeStruct((t, LANES), F32)],
        compiler_params=_params(1),
        name="inproj",
    )(x2, sc, sh, nw, w_main, w_ga, qw, kw)


def _t5_bucket_np(rel):
    max_exact = REL_BUCKETS // 2
    relf = np.maximum(rel, 1).astype(np.float64)
    large = max_exact + (np.log(relf / max_exact) / math.log(REL_MAX_DIST / max_exact)
                         * (REL_BUCKETS - max_exact)).astype(np.int32)
    large = np.minimum(large, REL_BUCKETS - 1)
    return np.where(rel < max_exact, rel, large)


def _bias_kernel(rb_ref, idx_ref, o_ref):
    h = pl.program_id(0)
    idx = idx_ref[...]
    tab = jnp.full(idx.shape, -jnp.inf, F32)
    for bk in range(REL_BUCKETS):
        tab = jnp.where(idx == bk, rb_ref[bk * MOBA_HEADS + h], tab)
    o_ref[...] = tab


def _moba_bias_tables(rel_bias):
    j = np.arange(MOBA_BLOCK)[:, None]
    i = np.arange(MOBA_BLOCK)[None, :]
    own_idx = np.where(j <= i, _t5_bucket_np(np.maximum(i - j, 0)), -1)
    prev_idx = _t5_bucket_np(MOBA_BLOCK + i - j)
    idx = jnp.asarray(np.concatenate([prev_idx, own_idx], axis=0).astype(np.int32))
    assert int(_t5_bucket_np(np.array([MOBA_BLOCK + 1]))[0]) == REL_BUCKETS - 1
    rb = rel_bias.astype(F32)
    near = pl.pallas_call(
        _bias_kernel,
        grid=(MOBA_HEADS,),
        in_specs=[pl.BlockSpec(memory_space=pltpu.SMEM),
                  pl.BlockSpec(idx.shape, lambda h: (0, 0))],
        out_specs=pl.BlockSpec((None,) + idx.shape, lambda h: (h, 0, 0)),
        out_shape=jax.ShapeDtypeStruct((MOBA_HEADS,) + idx.shape, F32),
        compiler_params=_params(1),
        name="bias",
    )(rb.reshape(-1), idx)
    return near, rb[REL_BUCKETS - 1]


FAR_GROUP = 4


def _moba_kernel(far_ref, q_ref, k_ref, v_ref, near_ref, ow_ref, o_ref,
                 kmean_ref, vt_ref, vtg_ref, acc_ref, m_ref, sel_ref):
    hp = pl.program_id(1)
    i = pl.program_id(2)
    nblk = k_ref.shape[0] // MOBA_BLOCK
    hd = MOBA_HEAD_DIM
    bs = MOBA_BLOCK

    @pl.when(i == 0)
    def _prepare():
        row = lax.broadcasted_iota(jnp.int32, (LANES, MOBA_BLOCK), 0)
        for n in range(nblk):
            kb = k_ref[n * MOBA_BLOCK:(n + 1) * MOBA_BLOCK, :].astype(F32)
            kmean_ref[n:n + 1, :] = jnp.mean(kb, axis=0, keepdims=True)
            vt = v_ref[n * MOBA_BLOCK:(n + 1) * MOBA_BLOCK, :].astype(F32).T
            vt0 = jnp.where(row < hd, vt, 1.0).astype(BF16)
            vt1 = jnp.where(row < hd, 1.0, vt).astype(BF16)
            vt_ref[0, n] = vt0
            vt_ref[1, n] = vt1
            gcols = slice((n % FAR_GROUP) * bs, (n % FAR_GROUP + 1) * bs)
            vtg_ref[0, n // FAR_GROUP, :, gcols] = vt0
            vtg_ref[1, n // FAR_GROUP, :, gcols] = vt1

    q = q_ref[...]
    lane = lax.broadcasted_iota(jnp.int32, q.shape, 1)
    zero = jnp.zeros_like(q)
    qh = (jnp.where(lane < hd, q, zero), jnp.where(lane < hd, zero, q))

    blk = lax.broadcasted_iota(jnp.int32, (nblk, MOBA_BLOCK), 0)
    for h in range(2):
        gt = lax.dot_general(kmean_ref[...], qh[h].astype(F32), NT_DIMS, preferred_element_type=F32)
        gt = jnp.where(blk < i, gt, -jnp.inf)
        cnt = jnp.zeros(gt.shape, jnp.int32)
        for m in range(nblk):
            gm = gt[m:m + 1, :]
            cnt = cnt + jnp.where((gm > gt) | ((gm == gt) & (blk > m)), 1, 0)
        keep = (blk < i) & (cnt < MOBA_TOPK)
        sel_ref[0, h] = jnp.where(keep, 1.0, 0.0)
        sel_ref[1, h] = jnp.where(keep & (blk < i - 1), 1.0, 0.0)

    @pl.when(i == 0)
    def _own_block_only():
        kb = k_ref[0:bs, :]
        for h in range(2):
            s = lax.dot_general(kb, qh[h], NT_DIMS, preferred_element_type=F32) + near_ref[h, bs:2 * bs, :]
            m_new = jnp.max(s, axis=0, keepdims=True)
            p = jnp.exp(s - m_new).astype(BF16)
            acc_ref[h] = jnp.dot(vt_ref[h, 0], p, preferred_element_type=F32)
            m_ref[h] = m_new

    @pl.when(i >= 1)
    def _previous_and_own_block():
        k_prev = k_ref[pl.ds(pl.multiple_of((i - 1) * bs, bs), bs), :]
        k_own = k_ref[pl.ds(pl.multiple_of(i * bs, bs), bs), :]
        for h in range(2):
            s_own = lax.dot_general(k_own, qh[h], NT_DIMS, preferred_element_type=F32) + near_ref[h, bs:2 * bs, :]
            m_own = jnp.max(s_own, axis=0, keepdims=True)
            pv_own = jnp.dot(vt_ref[h, i], jnp.exp(s_own - m_own).astype(BF16), preferred_element_type=F32)
            s_prev = lax.dot_general(k_prev, qh[h], NT_DIMS, preferred_element_type=F32) + near_ref[h, 0:bs, :]
            keep = sel_ref[0, h, pl.ds(i - 1, 1), :] > 0.5
            mx = jnp.max(s_prev, axis=0, keepdims=True)
            p_prev = jnp.exp(s_prev - jnp.where(keep, mx, jnp.inf)).astype(BF16)
            pv_prev = jnp.dot(vt_ref[h, i - 1], p_prev, preferred_element_type=F32)
            m_new = jnp.maximum(m_own, jnp.where(keep, mx, -jnp.inf))
            acc_ref[h] = (pv_own * jnp.exp(m_own - m_new)
                          + pv_prev * jnp.exp(jnp.where(keep, mx, -jnp.inf) - m_new))
            m_ref[h] = m_new

    def far_body(g, carry):
        gk = FAR_GROUP * bs
        kb = k_ref[pl.ds(pl.multiple_of(g * gk, gk), gk), :]
        ss = [lax.dot_general(kb, qh[h], NT_DIMS, preferred_element_type=F32) for h in range(2)]
        for h in range(2):
            s = ss[h]
            fb = far_ref[2 * hp + h]
            m_old = m_ref[h]
            m_new = m_old
            keeps = []
            for j in range(FAR_GROUP):
                keep = sel_ref[1, h, pl.ds(g * FAR_GROUP + j, 1), :] > 0.5
                mx = jnp.max(s[j * bs:(j + 1) * bs], axis=0, keepdims=True) + fb
                m_new = jnp.maximum(m_new, jnp.where(keep, mx, -jnp.inf))
                keeps.append(keep)
            p = jnp.concatenate(
                [jnp.exp(s[j * bs:(j + 1) * bs] - jnp.where(keeps[j], m_new - fb, jnp.inf)).astype(BF16)
                 for j in range(FAR_GROUP)], axis=0)
            pv = jnp.dot(vtg_ref[h, g], p, preferred_element_type=F32)
            acc_ref[h] = acc_ref[h] * jnp.exp(m_old - m_new) + pv
            m_ref[h] = m_new
        return carry

    lax.fori_loop(0, (i + FAR_GROUP - 2) // FAR_GROUP, far_body, 0)

    a0 = acc_ref[0]
    a1 = acc_ref[1]
    row = lax.broadcasted_iota(jnp.int32, a0.shape, 0)
    ot = jnp.where(row < hd, a0 / a0[hd:hd + 1, :], a1 / a1[0:1, :])
    o2 = ot * ot
    ss0 = jnp.sum(jnp.where(row < hd, o2, 0.0), axis=0, keepdims=True)
    ss1 = jnp.sum(jnp.where(row < hd, 0.0, o2), axis=0, keepdims=True)
    inv = jnp.where(row < hd, lax.rsqrt(ss0 * (1.0 / hd) + NORM_EPS), lax.rsqrt(ss1 * (1.0 / hd) + NORM_EPS))
    o_ref[...] = ((ot * inv).T * ow_ref[...]).astype(o_ref.dtype)


def _moba(proj3, near, far, ow):
    b, s, _ = proj3.shape
    nblk = s // MOBA_BLOCK
    assert nblk % FAR_GROUP == 0
    npair = MOBA_HEADS // 2
    kcol = MOBA_WIDTH // LANES
    return pl.pallas_call(
        _moba_kernel,
        grid=(b, npair, nblk),
        in_specs=[pl.BlockSpec(memory_space=pltpu.SMEM),
                  pl.BlockSpec((None, MOBA_BLOCK, LANES), lambda bb, hp, i: (bb, i, hp)),
                  pl.BlockSpec((None, s, LANES), lambda bb, hp, i: (bb, 0, kcol + hp)),
                  pl.BlockSpec((None, s, LANES), lambda bb, hp, i: (bb, 0, 2 * kcol + hp)),
                  pl.BlockSpec((2, 2 * MOBA_BLOCK, MOBA_BLOCK), lambda bb, hp, i: (hp, 0, 0)),
                  pl.BlockSpec((1, LANES), lambda bb, hp, i: (0, 0))],
        out_specs=pl.BlockSpec((None, MOBA_BLOCK, LANES), lambda bb, hp, i: (bb, i, hp)),
        out_shape=jax.ShapeDtypeStruct((b, s, MOBA_WIDTH), BF16),
        scratch_shapes=[pltpu.VMEM((nblk, LANES), F32),
                        pltpu.VMEM((2, nblk, LANES, MOBA_BLOCK), BF16),
                        pltpu.VMEM((2, nblk // FAR_GROUP, LANES, FAR_GROUP * MOBA_BLOCK), BF16),
                        pltpu.VMEM((2, LANES, MOBA_BLOCK), F32),
                        pltpu.VMEM((2, 1, MOBA_BLOCK), F32),
                        pltpu.VMEM((2, 2, nblk, MOBA_BLOCK), F32)],
        compiler_params=_params(3),
        name="moba",
    )(far, proj3, proj3, proj3, near, ow)


def _split3(v):
    hi = v.astype(BF16)
    r1 = v - hi.astype(F32)
    mid = r1.astype(BF16)
    lo = (r1 - mid.astype(F32)).astype(BF16)
    return hi, mid, lo


def _gla_kernel(q_ref, k_ref, v_ref, g_ref, ga_ref, wal_ref, bal_ref, gw_ref, o_ref, b_ref, st_ref):
    seq = q_ref.shape[0]
    c = GLA_CHUNK
    pc = 256

    rr = lax.broadcasted_iota(jnp.int32, (pc, pc), 0)
    cc = lax.broadcasted_iota(jnp.int32, (pc, pc), 1)
    tri = jnp.where((rr >= cc) & (rr // c == cc // c), 1.0, 0.0).astype(BF16)

    def decay_body(j, carry):
        r0 = pl.multiple_of(j * pc, pc)
        xg = jnp.dot(ga_ref[pl.ds(r0, pc), :], wal_ref[...], preferred_element_type=F32) + bal_ref[...]
        la = (jnp.minimum(xg, 0.0) - jnp.log(1.0 + jnp.exp(-jnp.abs(xg)))) * (1.0 / GLA_GATE_TAU)
        hi, mid, lo = _split3(la)
        b_ref[pl.ds(r0, pc), :] = (jnp.dot(tri, hi, preferred_element_type=F32)
                                   + jnp.dot(tri, mid, preferred_element_type=F32)
                                   + jnp.dot(tri, lo, preferred_element_type=F32))
        return carry

    lax.fori_loop(0, seq // pc, decay_body, 0)

    st_ref[...] = jnp.zeros_like(st_ref)
    lane = lax.broadcasted_iota(jnp.int32, (c, LANES), 1)
    head_mask = (lane < GLA_DK, lane >= GLA_DK)
    causal = lax.broadcasted_iota(jnp.int32, (c, c), 0) >= lax.broadcasted_iota(jnp.int32, (c, c), 1)

    def chunk_body(ci, carry):
        r0 = pl.multiple_of(ci * c, c)
        b = b_ref[pl.ds(r0, c), :]
        ref_row = b[c // 2 - 1:c // 2, :]
        last = b[c - 1:c, :]
        q = q_ref[pl.ds(r0, c), :].astype(F32) * (GLA_DK ** -0.5)
        k = k_ref[pl.ds(r0, c), :].astype(F32)
        qt = q * jnp.exp(b - ref_row)
        kt = (k * jnp.exp(ref_row - b)).astype(BF16)
        qs = q * jnp.exp(b)
        ke = (k * jnp.exp(last - b)).astype(BF16)
        e_last = jnp.exp(last)
        for h in range(2):
            cols = slice(h * GLA_DV, (h + 1) * GLA_DV)
            a = lax.dot_general(jnp.where(head_mask[h], qt, 0.0).astype(BF16), kt, NT_DIMS,
                                preferred_element_type=F32)
            a = jnp.where(causal, a, 0.0).astype(BF16)
            v = v_ref[pl.ds(r0, c), cols]
            st = st_ref[h]
            o = jnp.dot(a, v, preferred_element_type=F32)
            o = o + lax.dot_general(jnp.where(head_mask[h], qs, 0.0).astype(BF16), st.astype(BF16), NT_DIMS,
                                    preferred_element_type=F32)
            st_ref[h] = st * e_last + lax.dot_general(v, ke, TN_DIMS, preferred_element_type=F32)
            ms = jnp.mean(o * o, axis=-1, keepdims=True)
            on = o * lax.rsqrt(ms + NORM_EPS) * gw_ref[...]
            g = g_ref[pl.ds(r0, c), cols].astype(F32)
            o_ref[pl.ds(r0, c), cols] = (on * _silu(g)).astype(o_ref.dtype)
        return carry

    lax.fori_loop(0, seq // c, chunk_body, 0)


def _gla(proj3, ga3, wal, bal, gw):
    b, s, _ = proj3.shape
    npair = GLA_HEADS // 2
    qcol = 3 * MOBA_WIDTH // LANES
    kcol = qcol + GLA_KEY_WIDTH // LANES
    vcol = (3 * MOBA_WIDTH + 2 * GLA_KEY_WIDTH) // (2 * GLA_DV)
    gcol = vcol + npair
    return pl.pallas_call(
        _gla_kernel,
        grid=(b, npair),
        in_specs=[pl.BlockSpec((None, s, LANES), lambda bb, hp: (bb, 0, qcol + hp)),
                  pl.BlockSpec((None, s, LANES), lambda bb, hp: (bb, 0, kcol + hp)),
                  pl.BlockSpec((None, s, 2 * GLA_DV), lambda bb, hp: (bb, 0, vcol + hp)),
                  pl.BlockSpec((None, s, 2 * GLA_DV), lambda bb, hp: (bb, 0, gcol + hp)),
                  pl.BlockSpec((None, s, LANES), lambda bb, hp: (bb, 0, 0)),
                  pl.BlockSpec((LANES, LANES), lambda bb, hp: (0, hp)),
                  pl.BlockSpec((1, LANES), lambda bb, hp: (0, hp)),
                  pl.BlockSpec((1, GLA_DV), lambda bb, hp: (0, 0))],
        out_specs=pl.BlockSpec((None, s, 2 * GLA_DV), lambda bb, hp: (bb, 0, hp)),
        out_shape=jax.ShapeDtypeStruct((b, s, GLA_WIDTH), BF16),
        scratch_shapes=[pltpu.VMEM((s, LANES), F32),
                        pltpu.VMEM((2, GLA_DV, LANES), F32)],
        compiler_params=_params(2),
        name="gla",
    )(proj3, proj3, proj3, proj3, ga3, wal, bal, gw)


HALF = D_MODEL // 2


def _pack_rows(v):
    def bf16_bits(a):
        u = lax.bitcast_convert_type(a, jnp.uint32)
        return (u + (jnp.uint32(0x7FFF) + ((u >> 16) & jnp.uint32(1)))) >> 16
    return bf16_bits(v[:, :HALF]) | (bf16_bits(v[:, HALF:]) << 16)


def _unpack_rows(w):
    return (lax.bitcast_convert_type(w << 16, F32),
            lax.bitcast_convert_type(w & jnp.uint32(0xFFFF0000), F32))


def _outproj_kernel(oa_ref, ob_ref, x_ref, g1_ref, sc_ref, sh_ref, g2_ref, nw_ref, wo_ref,
                    ws1_ref, ws3_ref, ws2_ref, wrt_ref, base_ref, h_ref, st_ref):
    mix = (jnp.dot(oa_ref[...], wo_ref[:MOBA_WIDTH, :], preferred_element_type=F32)
           + jnp.dot(ob_ref[...], wo_ref[MOBA_WIDTH:, :], preferred_element_type=F32))
    x1 = x_ref[...] + g1_ref[...] * mix
    ms = jnp.mean(x1 * x1, axis=-1, keepdims=True)
    h = x1 * lax.rsqrt(ms + NORM_EPS) * nw_ref[...]
    h = h * (1.0 + sc_ref[...]) + sh_ref[...]
    h_ref[...] = _pack_rows(h)
    hb = h.astype(BF16)
    a = jnp.dot(hb, ws1_ref[...], preferred_element_type=F32)
    u = jnp.dot(hb, ws3_ref[...], preferred_element_type=F32)
    shared = jnp.dot((_silu(a) * u).astype(BF16), ws2_ref[...], preferred_element_type=F32)
    base_ref[...] = x1 + g2_ref[...] * shared
    logits_t = lax.dot_general(wrt_ref[...], hb, NT_DIMS, preferred_element_type=F32)
    st_ref[...] = jax.nn.sigmoid(logits_t)


def _outproj(oa, ob, x2, g1, sc, sh, g2, nw, wo, ws1, ws3, ws2, wrt, seq):
    t = x2.shape[0]
    tpb = seq // ROW_TILE
    vec = lambda: pl.BlockSpec((None, 1, D_MODEL), lambda i: (i // tpb, 0, 0))
    full = lambda a: pl.BlockSpec(a.shape, lambda i: (0,) * a.ndim)
    rows = lambda w: pl.BlockSpec((ROW_TILE, w), lambda i: (i, 0))
    return pl.pallas_call(
        _outproj_kernel,
        grid=(t // ROW_TILE,),
        in_specs=[rows(MOBA_WIDTH), rows(GLA_WIDTH), rows(D_MODEL), vec(), vec(), vec(), vec(),
                  full(nw), full(wo), full(ws1), full(ws3), full(ws2), full(wrt)],
        out_specs=[rows(D_MODEL), rows(HALF), pl.BlockSpec((N_EXPERTS, ROW_TILE), lambda i: (0, i))],
        out_shape=[jax.ShapeDtypeStruct((t, D_MODEL), F32),
                   jax.ShapeDtypeStruct((t, HALF), jnp.uint32),
                   jax.ShapeDtypeStruct((N_EXPERTS, t), F32)],
        compiler_params=_params(1),
        name="outproj",
    )(oa, ob, x2, g1, sc, sh, g2, nw, wo, ws1, ws3, ws2, wrt)


SLOT_CODE_SHIFT = 16
SLOT_CODE_BASE = 1 << SLOT_CODE_SHIFT


def _route_kernel(s_ref, eb_ref, code_ref, w_ref, cnt_ref, carry_ref):
    i = pl.program_id(0)
    ne, nt = s_ref.shape

    @pl.when(i == 0)
    def _init():
        carry_ref[...] = jnp.zeros_like(carry_ref)

    s = s_ref[...]
    choice = s + eb_ref[...]
    gio = lax.broadcasted_iota(jnp.int32, (GROUP_SIZE, nt), 0)
    gscore = []
    for g in range(N_GROUPS):
        cg = choice[g * GROUP_SIZE:(g + 1) * GROUP_SIZE, :]
        top1 = jnp.max(cg, axis=0, keepdims=True)
        first = jnp.min(jnp.where(cg == top1, gio, GROUP_SIZE), axis=0, keepdims=True)
        top2 = jnp.max(jnp.where(gio == first, -jnp.inf, cg), axis=0, keepdims=True)
        gscore.append(top1 + top2)
    gs = jnp.concatenate(gscore, axis=0)
    gidx = lax.broadcasted_iota(jnp.int32, gs.shape, 0)
    beaten = jnp.zeros(gs.shape, jnp.int32)
    for m in range(N_GROUPS):
        gm = gs[m:m + 1, :]
        beaten = beaten + jnp.where((gm > gs) | ((gm == gs) & (gidx > m)), 1, 0)
    gkeep = beaten < TOPK_GROUPS
    masked = jnp.concatenate(
        [jnp.where(gkeep[g:g + 1, :], choice[g * GROUP_SIZE:(g + 1) * GROUP_SIZE, :], -jnp.inf)
         for g in range(N_GROUPS)], axis=0)

    eio = lax.broadcasted_iota(jnp.int32, (ne, nt), 0)
    picked = jnp.zeros((ne, nt), F32)
    idx_rows, w_rows, hits = [], [], []
    for _ in range(TOP_K):
        mx = jnp.max(masked, axis=0, keepdims=True)
        idx = jnp.min(jnp.where(masked == mx, eio, ne), axis=0, keepdims=True)
        hit = eio == idx
        w_rows.append(jnp.sum(jnp.where(hit, s, 0.0), axis=0, keepdims=True))
        idx_rows.append(idx)
        hits.append(hit)
        masked = jnp.where(hit, -jnp.inf, masked)
        picked = jnp.where(hit, 1.0, picked)
    wk = jnp.concatenate(w_rows, axis=0)
    w_ref[...] = wk / jnp.sum(wk, axis=0, keepdims=True) * ROUTED_SCALE

    tr = lax.broadcasted_iota(jnp.int32, (nt, nt), 0)
    tc = lax.broadcasted_iota(jnp.int32, (nt, nt), 1)
    before = jnp.where(tr < tc, 1.0, 0.0).astype(BF16)
    pb = picked.astype(BF16)
    pos = carry_ref[...] + jnp.dot(pb, before, preferred_element_type=F32)
    rank = jnp.concatenate(
        [jnp.sum(jnp.where(hit, pos, 0.0), axis=0, keepdims=True) for hit in hits], axis=0).astype(jnp.int32)
    code_ref[...] = jnp.concatenate(idx_rows, axis=0) * SLOT_CODE_BASE + rank
    total = carry_ref[...] + jnp.dot(pb, jnp.ones((nt, nt), BF16), preferred_element_type=F32)
    carry_ref[...] = total
    cnt_ref[...] = total


def _route(scores_t, eb):
    ne, t = scores_t.shape
    assert t <= SLOT_CODE_BASE
    nt = ROUTE_TILE
    tok = lambda dt: jax.ShapeDtypeStruct((TOP_K, t), dt)
    return pl.pallas_call(
        _route_kernel,
        grid=(t // nt,),
        in_specs=[pl.BlockSpec((ne, nt), lambda i: (0, i)),
                  pl.BlockSpec((ne, nt), lambda i: (0, 0))],
        out_specs=[pl.BlockSpec((TOP_K, nt), lambda i: (0, i)),
                   pl.BlockSpec((TOP_K, nt), lambda i: (0, i)),
                   pl.BlockSpec((ne, nt), lambda i: (0, 0))],
        out_shape=[tok(jnp.int32), tok(F32), jax.ShapeDtypeStruct((ne, nt), F32)],
        scratch_shapes=[pltpu.VMEM((ne, nt), F32)],
        compiler_params=_params(1),
        name="route",
    )(scores_t, eb)


SLOT_TILE = 2048


def _slots_kernel(pstart_ref, code_ref, o_ref):
    code = code_ref[...]
    expert = lax.shift_right_logical(code, SLOT_CODE_SHIFT)

    def body(e, acc):
        return jnp.where(expert == e, pstart_ref[e], acc)

    start = lax.fori_loop(0, N_EXPERTS, body, jnp.zeros_like(code), unroll=8)
    o_ref[...] = start + (code & (SLOT_CODE_BASE - 1))


def _slots(pstart, code_t):
    k, t = code_t.shape
    return pl.pallas_call(
        _slots_kernel,
        grid_spec=pltpu.PrefetchScalarGridSpec(
            num_scalar_prefetch=1,
            grid=(t // SLOT_TILE,),
            in_specs=[pl.BlockSpec((k, SLOT_TILE), lambda i, p: (0, i))],
            out_specs=pl.BlockSpec((k, SLOT_TILE), lambda i, p: (0, i)),
        ),
        out_shape=jax.ShapeDtypeStruct((k, t), jnp.int32),
        compiler_params=_params(1),
        name="slots",
    )(pstart, code_t)


def _dispatch_kernel(dest_ref, h_ref, xs_ref, sem):
    i = pl.program_id(0)
    nt = h_ref.shape[0]

    def row_copy(t, d):
        return pltpu.make_async_copy(h_ref.at[pl.ds(t, 1), :], xs_ref.at[pl.ds(d, 1), :], sem)

    def start_body(t, carry):
        base = (i * nt + t) * TOP_K
        for k in range(TOP_K):
            row_copy(t, dest_ref[base + k]).start(priority=k % 2)
        return carry

    lax.fori_loop(0, nt, start_body, 0)

    def wait_body(t, carry):
        for k in range(TOP_K):
            row_copy(t, 0).wait()
        return carry

    lax.fori_loop(0, nt, wait_body, 0)


def _dispatch(dest_flat, h2, n_rows):
    t = h2.shape[0]
    nt = DISPATCH_TILE
    return pl.pallas_call(
        _dispatch_kernel,
        grid_spec=pltpu.PrefetchScalarGridSpec(
            num_scalar_prefetch=1,
            grid=(t // nt,),
            in_specs=[pl.BlockSpec((nt, HALF), lambda i, d: (i, 0))],
            out_specs=pl.BlockSpec(memory_space=pl.ANY),
            scratch_shapes=[pltpu.SemaphoreType.DMA],
        ),
        out_shape=jax.ShapeDtypeStruct((n_rows, HALF), jnp.uint32),
        compiler_params=_params(1),
        name="dispatch",
    )(dest_flat, h2)


def _expert_kernel(te_ref, nu_ref, xs_ref, w1_ref, w3_ref, w2_ref, ys_ref, w1b, w3b, w2b):
    i = pl.program_id(0)
    new_expert = (i == 0) | (te_ref[i] != te_ref[jnp.maximum(i - 1, 0)])

    @pl.when(new_expert)
    def _cast_weights():
        w1b[...] = w1_ref[...].astype(BF16)
        w3b[...] = w3_ref[...].astype(BF16)
        w2b[...] = w2_ref[...].astype(BF16)

    @pl.when(i < nu_ref[0])
    def _compute():
        lo, hi = _unpack_rows(xs_ref[...])
        lo, hi = lo.astype(BF16), hi.astype(BF16)
        a = (jnp.dot(lo, w1b[:HALF, :], preferred_element_type=F32)
             + jnp.dot(hi, w1b[HALF:, :], preferred_element_type=F32))
        u = (jnp.dot(lo, w3b[:HALF, :], preferred_element_type=F32)
             + jnp.dot(hi, w3b[HALF:, :], preferred_element_type=F32))
        y = jnp.dot((_silu(a) * u).astype(BF16), w2b[...], preferred_element_type=F32)
        ys_ref[...] = _pack_rows(y)


def _experts(tile_expert, n_used, xs, w1, w3, w2, layer):
    n_rows = xs.shape[0]
    r = EXPERT_TILE
    row_map = lambda i, te, nu: (jnp.minimum(i, nu[0] - 1), 0)
    w_map = lambda i, te, nu: (layer, te[i], 0, 0)
    return pl.pallas_call(
        _expert_kernel,
        grid_spec=pltpu.PrefetchScalarGridSpec(
            num_scalar_prefetch=2,
            grid=(n_rows // r,),
            in_specs=[pl.BlockSpec((r, HALF), row_map),
                      pl.BlockSpec((None, None, D_MODEL, EXPERT_FF), w_map),
                      pl.BlockSpec((None, None, D_MODEL, EXPERT_FF), w_map),
                      pl.BlockSpec((None, None, EXPERT_FF, D_MODEL), w_map)],
            out_specs=pl.BlockSpec((r, HALF), row_map),
            scratch_shapes=[pltpu.VMEM((D_MODEL, EXPERT_FF), BF16),
                            pltpu.VMEM((D_MODEL, EXPERT_FF), BF16),
                            pltpu.VMEM((EXPERT_FF, D_MODEL), BF16)],
        ),
        out_shape=jax.ShapeDtypeStruct((n_rows, HALF), jnp.uint32),
        compiler_params=_params(1),
        name="experts",
    )(tile_expert, n_used, xs, w1, w3, w2)


def _combine_kernel(dest_ref, base_ref, g2_ref, w_ref, ys_ref, o_ref, buf, sem):
    i = pl.program_id(0)
    nt = base_ref.shape[0]

    def row_copy(t, k, d):
        return pltpu.make_async_copy(ys_ref.at[pl.ds(d, 1), :], buf.at[k, pl.ds(t, 1), :], sem)

    def start_body(t, carry):
        base = (i * nt + t) * TOP_K
        for k in range(TOP_K):
            row_copy(t, k, dest_ref[base + k]).start(priority=k % 2)
        return carry

    lax.fori_loop(0, nt, start_body, 0)

    def wait_body(t, carry):
        for k in range(TOP_K):
            row_copy(t, k, 0).wait()
        return carry

    lax.fori_loop(0, nt, wait_body, 0)

    acc_lo = acc_hi = None
    for k in range(TOP_K):
        lo, hi = _unpack_rows(buf[k])
        wk = w_ref[:, k:k + 1]
        acc_lo = wk * lo if acc_lo is None else acc_lo + wk * lo
        acc_hi = wk * hi if acc_hi is None else acc_hi + wk * hi
    o_ref[:, :HALF] = base_ref[:, :HALF] + g2_ref[:, :HALF] * acc_lo
    o_ref[:, HALF:] = base_ref[:, HALF:] + g2_ref[:, HALF:] * acc_hi


def _combine(dest_flat, base, g2, w_tok, ys, seq):
    t = base.shape[0]
    nt = COMBINE_TILE
    tpb = seq // nt
    return pl.pallas_call(
        _combine_kernel,
        grid_spec=pltpu.PrefetchScalarGridSpec(
            num_scalar_prefetch=1,
            grid=(t // nt,),
            in_specs=[pl.BlockSpec((nt, D_MODEL), lambda i, d: (i, 0)),
                      pl.BlockSpec((None, 1, D_MODEL), lambda i, d: (i // tpb, 0, 0)),
                      pl.BlockSpec((nt, TOP_K), lambda i, d: (i, 0)),
                      pl.BlockSpec(memory_space=pl.ANY)],
            out_specs=pl.BlockSpec((nt, D_MODEL), lambda i, d: (i, 0)),
            scratch_shapes=[pltpu.VMEM((TOP_K, nt, HALF), jnp.uint32),
                            pltpu.SemaphoreType.DMA],
        ),
        out_shape=jax.ShapeDtypeStruct((t, D_MODEL), F32),
        compiler_params=_params(1),
        name="combine",
    )(dest_flat, base, g2, w_tok, ys)


def _layer(layer, x, c, w_ada, b_ada, norm1_w, norm2_w, w_in, q_norm_w, k_norm_w, rel_bias, w_alpha, b_alpha,
           moba_out_w, gla_out_w, w_out, w_router, e_bias, w1, w3, w2, ws1, ws3, ws2):
    b, s, d = x.shape
    t = b * s
    x2 = x.reshape(t, d)

    mod = _mod(c, w_ada, b_ada)
    sh1, sc1, g1, sh2, sc2, g2 = [mod[:, j * d:(j + 1) * d].reshape(b, 1, d) for j in range(6)]

    w_main = w_in[:, :D_MAIN].astype(BF16)
    w_ga = jnp.zeros((d, LANES), BF16).at[:, :GLA_GATE_RANK].set(w_in[:, D_MAIN:].astype(BF16))
    per_chunk = 256 // MOBA_HEAD_DIM
    qw = jnp.tile(q_norm_w.astype(F32), per_chunk).reshape(1, 256) * (MOBA_HEAD_DIM ** -0.5)
    kw = jnp.tile(k_norm_w.astype(F32), per_chunk).reshape(1, 256)
    proj, ga = _inproj(x2, sc1, sh1, norm1_w.reshape(1, d), w_main, w_ga, qw, kw, s)
    proj3 = proj.reshape(b, s, D_MAIN)

    near, far = _moba_bias_tables(rel_bias)
    ow = jnp.tile(moba_out_w.astype(F32), 2).reshape(1, LANES)
    o_a = _moba(proj3, near, far, ow)

    wal = jnp.zeros((LANES, GLA_KEY_WIDTH), F32).at[:GLA_GATE_RANK].set(w_alpha)
    o_b = _gla(proj3, ga.reshape(b, s, LANES), wal, b_alpha.reshape(1, GLA_KEY_WIDTH),
               gla_out_w.reshape(1, GLA_DV))

    base, h2, scores_t = _outproj(
        o_a.reshape(t, MOBA_WIDTH), o_b.reshape(t, GLA_WIDTH), x2, g1, sc2, sh2, g2,
        norm2_w.reshape(1, d), w_out.astype(BF16), ws1.astype(BF16), ws3.astype(BF16), ws2.astype(BF16),
        w_router.T.astype(BF16), s)

    eb = jnp.broadcast_to(e_bias.astype(F32)[:, None], (N_EXPERTS, ROUTE_TILE))
    code_t, w_t, counts = _route(scores_t, eb)

    r = EXPERT_TILE
    n_tiles = (t * TOP_K + N_EXPERTS * (r - 1) + r - 1) // r
    n_rows = n_tiles * r
    cnt = counts[:, 0].astype(jnp.int32)
    padded = (cnt + r - 1) // r * r
    pend = jnp.cumsum(padded)
    pstart = pend - padded
    tile_row = jnp.arange(n_tiles, dtype=jnp.int32) * r
    tile_expert = jnp.minimum(jnp.sum((pend[None, :] <= tile_row[:, None]).astype(jnp.int32), axis=1),
                              N_EXPERTS - 1)
    n_used = (pend[-1:] // r).astype(jnp.int32)
    dest = _slots(pstart, code_t).T.reshape(t * TOP_K)

    xs = _dispatch(dest, h2, n_rows)
    ys = _experts(tile_expert, n_used, xs, w1, w3, w2, layer)
    out = _combine(dest, base, g2, w_t.T, ys, s)
    return out.reshape(b, s, d)


def kernel(x, c, w_ada, b_ada, norm1_w, norm2_w, w_in, q_norm_w, k_norm_w, rel_bias, w_alpha, b_alpha,
           moba_out_w, gla_out_w, w_out, w_router, e_bias, w1, w3, w2, ws1, ws3, ws2):
    for l in range(w_ada.shape[0]):
        x = _layer(l, x, c, w_ada[l], b_ada[l], norm1_w[l], norm2_w[l], w_in[l], q_norm_w[l], k_norm_w[l],
                   rel_bias, w_alpha[l], b_alpha[l], moba_out_w[l], gla_out_w[l], w_out[l], w_router[l],
                   e_bias[l], w1, w3, w2, ws1[l], ws3[l], ws2[l])
    return x
```

```python
import functools
import math

import numpy as np
import jax
import jax.numpy as jnp
from jax import lax
from jax.experimental import pallas as pl
from jax.experimental.pallas import tpu as pltpu

D_MODEL = 1024
MOBA_HEADS = 8
MOBA_HEAD_DIM = 64
MOBA_WIDTH = MOBA_HEADS * MOBA_HEAD_DIM
MOBA_BLOCK = 256
MOBA_TOPK = 3
GLA_HEADS = 4
GLA_DK = 64
GLA_DV = 128
GLA_KEY_WIDTH = GLA_HEADS * GLA_DK
GLA_WIDTH = GLA_HEADS * GLA_DV
GLA_GATE_RANK = 16
GLA_GATE_TAU = 16.0
GLA_CHUNK = 64
REL_BUCKETS = 32
REL_MAX_DIST = 128
N_EXPERTS = 256
TOP_K = 8
N_GROUPS = 8
TOPK_GROUPS = 4
GROUP_SIZE = N_EXPERTS // N_GROUPS
EXPERT_FF = 256
SHARED_FF = 256
ROUTED_SCALE = 2.5
NORM_EPS = 1e-6

D_MAIN = 3 * MOBA_WIDTH + 2 * GLA_KEY_WIDTH + 2 * GLA_WIDTH
LANES = 128
VMEM_LIMIT = 56 * 1024 * 1024

ROW_TILE = 512
ROUTE_TILE = 256
DISPATCH_TILE = 256
COMBINE_TILE = 128
EXPERT_TILE = 256

F32 = jnp.float32
BF16 = jnp.bfloat16
NT_DIMS = (((1,), (1,)), ((), ()))
TN_DIMS = (((0,), (0,)), ((), ()))


def _params(n_axes):
    return pltpu.CompilerParams(dimension_semantics=("arbitrary",) * n_axes,
                                vmem_limit_bytes=VMEM_LIMIT)


def _silu(v):
    return v * jax.nn.sigmoid(v)


def _mod_kernel(c_ref, w_ref, b_ref, o_ref):
    o_ref[...] = jnp.dot(_silu(c_ref[...]), w_ref[...], preferred_element_type=F32) + b_ref[...]


def _mod(c, w, b):
    rows = 8
    cp = jnp.zeros((rows, D_MODEL), F32).at[:c.shape[0]].set(c)
    n = w.shape[1]
    tn = 1024
    out = pl.pallas_call(
        _mod_kernel,
        grid=(n // tn,),
        in_specs=[pl.BlockSpec((rows, D_MODEL), lambda j: (0, 0)),
                  pl.BlockSpec((D_MODEL, tn), lambda j: (0, j)),
                  pl.BlockSpec((1, tn), lambda j: (0, j))],
        out_specs=pl.BlockSpec((rows, tn), lambda j: (0, j)),
        out_shape=jax.ShapeDtypeStruct((rows, n), F32),
        compiler_params=_params(1),
        name="mod",
    )(cp, w, b.reshape(1, n))
    return out[:c.shape[0]]


def _group_rms_inv(a, group):
    lane = lax.broadcasted_iota(jnp.int32, (1, a.shape[1]), 1)
    a2 = a * a
    inv = jnp.zeros_like(a)
    for g in range(a.shape[1] // group):
        m = (lane >= g * group) & (lane < (g + 1) * group)
        ss = jnp.sum(jnp.where(m, a2, 0.0), axis=-1, keepdims=True)
        inv = jnp.where(m, lax.rsqrt(ss * (1.0 / group) + NORM_EPS), inv)
    return inv


def _inproj_kernel(x_ref, sc_ref, sh_ref, nw_ref, w_ref, wga_ref, qw_ref, kw_ref, o_ref, ga_ref):
    x = x_ref[...]
    ms = jnp.mean(x * x, axis=-1, keepdims=True)
    h = x * lax.rsqrt(ms + NORM_EPS) * nw_ref[...]
    h = h * (1.0 + sc_ref[...]) + sh_ref[...]
    hb = h.astype(BF16)
    cw = 256
    for j in range(D_MAIN // cw):
        acc = jnp.dot(hb, w_ref[:, j * cw:(j + 1) * cw], preferred_element_type=F32)
        if j < 2 * MOBA_WIDTH // cw:
            nw = qw_ref if j < MOBA_WIDTH // cw else kw_ref
            acc = acc * _group_rms_inv(acc, MOBA_HEAD_DIM) * nw[...]
        o_ref[:, j * cw:(j + 1) * cw] = acc.astype(BF16)
    ga_ref[...] = jnp.dot(hb, wga_ref[...], preferred_element_type=F32)


def _inproj(x2, sc, sh, nw, w_main, w_ga, qw, kw, seq):
    t = x2.shape[0]
    tpb = seq // ROW_TILE
    vec = lambda: pl.BlockSpec((None, 1, D_MODEL), lambda i: (i // tpb, 0, 0))
    full = lambda a: pl.BlockSpec(a.shape, lambda i: (0,) * a.ndim)
    return pl.pallas_call(
        _inproj_kernel,
        grid=(t // ROW_TILE,),
        in_specs=[pl.BlockSpec((ROW_TILE, D_MODEL), lambda i: (i, 0)), vec(), vec(),
                  full(nw), full(w_main), full(w_ga), full(qw), full(kw)],
        out_specs=[pl.BlockSpec((ROW_TILE, D_MAIN), lambda i: (i, 0)),
                   pl.BlockSpec((ROW_TILE, LANES), lambda i: (i, 0))],
        out_shape=[jax.ShapeDtypeStruct((t, D_MAIN), BF16),
                   jax.ShapeDtypeStruct((t, LANES), F32)],
        compiler_params=_params(1),
        name="inproj",
    )(x2, sc, sh, nw, w_main, w_ga, qw, kw)


def _t5_bucket_np(rel):
    max_exact = REL_BUCKETS // 2
    relf = np.maximum(rel, 1).astype(np.float64)
    large = max_exact + (np.log(relf / max_exact) / math.log(REL_MAX_DIST / max_exact)
                         * (REL_BUCKETS - max_exact)).astype(np.int32)
    large = np.minimum(large, REL_BUCKETS - 1)
    return np.where(rel < max_exact, rel, large)


def _bias_kernel(rb_ref, idx_ref, o_ref):
    h = pl.program_id(0)
    idx = idx_ref[...]
    tab = jnp.full(idx.shape, -jnp.inf, F32)
    for bk in range(REL_BUCKETS):
        tab = jnp.where(idx == bk, rb_ref[bk * MOBA_HEADS + h], tab)
    o_ref[...] = tab


def _moba_bias_tables(rel_bias):
    j = np.arange(MOBA_BLOCK)[:, None]
    i = np.arange(MOBA_BLOCK)[None, :]
    own_idx = np.where(j <= i, _t5_bucket_np(np.maximum(i - j, 0)), -1)
    prev_idx = _t5_bucket_np(MOBA_BLOCK + i - j)
    idx = jnp.asarray(np.concatenate([prev_idx, own_idx], axis=0).astype(np.int32))
    assert int(_t5_bucket_np(np.array([MOBA_BLOCK + 1]))[0]) == REL_BUCKETS - 1
    rb = rel_bias.astype(F32)
    near = pl.pallas_call(
        _bias_kernel,
        grid=(MOBA_HEADS,),
        in_specs=[pl.BlockSpec(memory_space=pltpu.SMEM),
                  pl.BlockSpec(idx.shape, lambda h: (0, 0))],
        out_specs=pl.BlockSpec((None,) + idx.shape, lambda h: (h, 0, 0)),
        out_shape=jax.ShapeDtypeStruct((MOBA_HEADS,) + idx.shape, F32),
        compiler_params=_params(1),
        name="bias",
    )(rb.reshape(-1), idx)
    return near, rb[REL_BUCKETS - 1]


FAR_GROUP = 4


def _moba_kernel(far_ref, q_ref, k_ref, v_ref, near_ref, ow_ref, o_ref,
                 kmean_ref, vt_ref, vtg_ref, acc_ref, m_ref, sel_ref):
    hp = pl.program_id(1)
    i = pl.program_id(2)
    nblk = k_ref.shape[0] // MOBA_BLOCK
    hd = MOBA_HEAD_DIM
    bs = MOBA_BLOCK

    @pl.when(i == 0)
    def _prepare():
        row = lax.broadcasted_iota(jnp.int32, (LANES, MOBA_BLOCK), 0)
        for n in range(nblk):
            kb = k_ref[n * MOBA_BLOCK:(n + 1) * MOBA_BLOCK, :].astype(F32)
            kmean_ref[n:n + 1, :] = jnp.mean(kb, axis=0, keepdims=True)
            vt = v_ref[n * MOBA_BLOCK:(n + 1) * MOBA_BLOCK, :].astype(F32).T
            vt0 = jnp.where(row < hd, vt, 1.0).astype(BF16)
            vt1 = jnp.where(row < hd, 1.0, vt).astype(BF16)
            vt_ref[0, n] = vt0
            vt_ref[1, n] = vt1
            gcols = slice((n % FAR_GROUP) * bs, (n % FAR_GROUP + 1) * bs)
            vtg_ref[0, n // FAR_GROUP, :, gcols] = vt0
            vtg_ref[1, n // FAR_GROUP, :, gcols] = vt1

    q = q_ref[...]
    lane = lax.broadcasted_iota(jnp.int32, q.shape, 1)
    zero = jnp.zeros_like(q)
    qh = (jnp.where(lane < hd, q, zero), jnp.where(lane < hd, zero, q))

    blk = lax.broadcasted_iota(jnp.int32, (nblk, MOBA_BLOCK), 0)
    for h in range(2):
        gt = lax.dot_general(kmean_ref[...], qh[h].astype(F32), NT_DIMS, preferred_element_type=F32)
        gt = jnp.where(blk < i, gt, -jnp.inf)
        cnt = jnp.zeros(gt.shape, jnp.int32)
        for m in range(nblk):
            gm = gt[m:m + 1, :]
            cnt = cnt + jnp.where((gm > gt) | ((gm == gt) & (blk > m)), 1, 0)
        keep = (blk < i) & (cnt < MOBA_TOPK)
        sel_ref[0, h] = jnp.where(keep, 1.0, 0.0)
        sel_ref[1, h] = jnp.where(keep & (blk < i - 1), 1.0, 0.0)

    @pl.when(i == 0)
    def _own_block_only():
        kb = k_ref[0:bs, :]
        for h in range(2):
            s = lax.dot_general(kb, qh[h], NT_DIMS, preferred_element_type=F32) + near_ref[h, bs:2 * bs, :]
            m_new = jnp.max(s, axis=0, keepdims=True)
            p = jnp.exp(s - m_new).astype(BF16)
            acc_ref[h] = jnp.dot(vt_ref[h, 0], p, preferred_element_type=F32)
            m_ref[h] = m_new

    @pl.when(i >= 1)
    def _previous_and_own_block():
        k_prev = k_ref[pl.ds(pl.multiple_of((i - 1) * bs, bs), bs), :]
        k_own = k_ref[pl.ds(pl.multiple_of(i * bs, bs), bs), :]
        for h in range(2):
            s_own = lax.dot_general(k_own, qh[h], NT_DIMS, preferred_element_type=F32) + near_ref[h, bs:2 * bs, :]
            m_own = jnp.max(s_own, axis=0, keepdims=True)
            pv_own = jnp.dot(vt_ref[h, i], jnp.exp(s_own - m_own).astype(BF16), preferred_element_type=F32)
            s_prev = lax.dot_general(k_prev, qh[h], NT_DIMS, preferred_element_type=F32) + near_ref[h, 0:bs, :]
            keep = sel_ref[0, h, pl.ds(i - 1, 1), :] > 0.5
            mx = jnp.max(s_prev, axis=0, keepdims=True)
            p_prev = jnp.exp(s_prev - jnp.where(keep, mx, jnp.inf)).astype(BF16)
            pv_prev = jnp.dot(vt_ref[h, i - 1], p_prev, preferred_element_type=F32)
            m_new = jnp.maximum(m_own, jnp.where(keep, mx, -jnp.inf))
            acc_ref[h] = (pv_own * jnp.exp(m_own - m_new)
                          + pv_prev * jnp.exp(jnp.where(keep, mx, -jnp.inf) - m_new))
            m_ref[h] = m_new

    def far_body(g, carry):
        gk = FAR_GROUP * bs
        kb = k_ref[pl.ds(pl.multiple_of(g * gk, gk), gk), :]
        ss = [lax.dot_general(kb, qh[h], NT_DIMS, preferred_element_type=F32) for h in range(2)]
        for h in range(2):
            s = ss[h]
            fb = far_ref[2 * hp + h]
            m_old = m_ref[h]
            m_new = m_old
            keeps = []
            for j in range(FAR_GROUP):
                keep = sel_ref[1, h, pl.ds(g * FAR_GROUP + j, 1), :] > 0.5
                mx = jnp.max(s[j * bs:(j + 1) * bs], axis=0, keepdims=True) + fb
                m_new = jnp.maximum(m_new, jnp.where(keep, mx, -jnp.inf))
                keeps.append(keep)
            p = jnp.concatenate(
                [jnp.exp(s[j * bs:(j + 1) * bs] - jnp.where(keeps[j], m_new - fb, jnp.inf)).astype(BF16)
                 for j in range(FAR_GROUP)], axis=0)
            pv = jnp.dot(vtg_ref[h, g], p, preferred_element_type=F32)
            acc_ref[h] = acc_ref[h] * jnp.exp(m_old - m_new) + pv
            m_ref[h] = m_new
        return carry

    lax.fori_loop(0, (i + FAR_GROUP - 2) // FAR_GROUP, far_body, 0)

    a0 = acc_ref[0]
    a1 = acc_ref[1]
    row = lax.broadcasted_iota(jnp.int32, a0.shape, 0)
    ot = jnp.where(row < hd, a0 / a0[hd:hd + 1, :], a1 / a1[0:1, :])
    o2 = ot * ot
    ss0 = jnp.sum(jnp.where(row < hd, o2, 0.0), axis=0, keepdims=True)
    ss1 = jnp.sum(jnp.where(row < hd, 0.0, o2), axis=0, keepdims=True)
    inv = jnp.where(row < hd, lax.rsqrt(ss0 * (1.0 / hd) + NORM_EPS), lax.rsqrt(ss1 * (1.0 / hd) + NORM_EPS))
    o_ref[...] = ((ot * inv).T * ow_ref[...]).astype(o_ref.dtype)


def _moba(proj3, near, far, ow):
    b, s, _ = proj3.shape
    nblk = s // MOBA_BLOCK
    assert nblk % FAR_GROUP == 0
    npair = MOBA_HEADS // 2
    kcol = MOBA_WIDTH // LANES
    return pl.pallas_call(
        _moba_kernel,
        grid=(b, npair, nblk),
        in_specs=[pl.BlockSpec(memory_space=pltpu.SMEM),
                  pl.BlockSpec((None, MOBA_BLOCK, LANES), lambda bb, hp, i: (bb, i, hp)),
                  pl.BlockSpec((None, s, LANES), lambda bb, hp, i: (bb, 0, kcol + hp)),
                  pl.BlockSpec((None, s, LANES), lambda bb, hp, i: (bb, 0, 2 * kcol + hp)),
                  pl.BlockSpec((2, 2 * MOBA_BLOCK, MOBA_BLOCK), lambda bb, hp, i: (hp, 0, 0)),
                  pl.BlockSpec((1, LANES), lambda bb, hp, i: (0, 0))],
        out_specs=pl.BlockSpec((None, MOBA_BLOCK, LANES), lambda bb, hp, i: (bb, i, hp)),
        out_shape=jax.ShapeDtypeStruct((b, s, MOBA_WIDTH), BF16),
        scratch_shapes=[pltpu.VMEM((nblk, LANES), F32),
                        pltpu.VMEM((2, nblk, LANES, MOBA_BLOCK), BF16),
                        pltpu.VMEM((2, nblk // FAR_GROUP, LANES, FAR_GROUP * MOBA_BLOCK), BF16),
                        pltpu.VMEM((2, LANES, MOBA_BLOCK), F32),
                        pltpu.VMEM((2, 1, MOBA_BLOCK), F32),
                        pltpu.VMEM((2, 2, nblk, MOBA_BLOCK), F32)],
        compiler_params=_params(3),
        name="moba",
    )(far, proj3, proj3, proj3, near, ow)


def _split3(v):
    hi = v.astype(BF16)
    r1 = v - hi.astype(F32)
    mid = r1.astype(BF16)
    lo = (r1 - mid.astype(F32)).astype(BF16)
    return hi, mid, lo


def _gla_kernel(q_ref, k_ref, v_ref, g_ref, ga_ref, wal_ref, bal_ref, gw_ref, o_ref, b_ref, st_ref):
    seq = q_ref.shape[0]
    c = GLA_CHUNK
    pc = 256

    rr = lax.broadcasted_iota(jnp.int32, (pc, pc), 0)
    cc = lax.broadcasted_iota(jnp.int32, (pc, pc), 1)
    tri = jnp.where((rr >= cc) & (rr // c == cc // c), 1.0, 0.0).astype(BF16)

    def decay_body(j, carry):
        r0 = pl.multiple_of(j * pc, pc)
        xg = jnp.dot(ga_ref[pl.ds(r0, pc), :], wal_ref[...], preferred_element_type=F32) + bal_ref[...]
        la = (jnp.minimum(xg, 0.0) - jnp.log(1.0 + jnp.exp(-jnp.abs(xg)))) * (1.0 / GLA_GATE_TAU)
        hi, mid, lo = _split3(la)
        b_ref[pl.ds(r0, pc), :] = (jnp.dot(tri, hi, preferred_element_type=F32)
                                   + jnp.dot(tri, mid, preferred_element_type=F32)
                                   + jnp.dot(tri, lo, preferred_element_type=F32))
        return carry

    lax.fori_loop(0, seq // pc, decay_body, 0)

    st_ref[...] = jnp.zeros_like(st_ref)
    lane = lax.broadcasted_iota(jnp.int32, (c, LANES), 1)
    head_mask = (lane < GLA_DK, lane >= GLA_DK)
    causal = lax.broadcasted_iota(jnp.int32, (c, c), 0) >= lax.broadcasted_iota(jnp.int32, (c, c), 1)

    def chunk_body(ci, carry):
        r0 = pl.multiple_of(ci * c, c)
        b = b_ref[pl.ds(r0, c), :]
        ref_row = b[c // 2 - 1:c // 2, :]
        last = b[c - 1:c, :]
        q = q_ref[pl.ds(r0, c), :].astype(F32) * (GLA_DK ** -0.5)
        k = k_ref[pl.ds(r0, c), :].astype(F32)
        qt = q * jnp.exp(b - ref_row)
        kt = (k * jnp.exp(ref_row - b)).astype(BF16)
        qs = q * jnp.exp(b)
        ke = (k * jnp.exp(last - b)).astype(BF16)
        e_last = jnp.exp(last)
        for h in range(2):
            cols = slice(h * GLA_DV, (h + 1) * GLA_DV)
            a = lax.dot_general(jnp.where(head_mask[h], qt, 0.0).astype(BF16), kt, NT_DIMS,
                                preferred_element_type=F32)
            a = jnp.where(causal, a, 0.0).astype(BF16)
            v = v_ref[pl.ds(r0, c), cols]
            st = st_ref[h]
            o = jnp.dot(a, v, preferred_element_type=F32)
            o = o + lax.dot_general(jnp.where(head_mask[h], qs, 0.0).astype(BF16), st.astype(BF16), NT_DIMS,
                                    preferred_element_type=F32)
            st_ref[h] = st * e_last + lax.dot_general(v, ke, TN_DIMS, preferred_element_type=F32)
            ms = jnp.mean(o * o, axis=-1, keepdims=True)
            on = o * lax.rsqrt(ms + NORM_EPS) * gw_ref[...]
            g = g_ref[pl.ds(r0, c), cols].astype(F32)
            o_ref[pl.ds(r0, c), cols] = (on * _silu(g)).astype(o_ref.dtype)
        return carry

    lax.fori_loop(0, seq // c, chunk_body, 0)


def _gla(proj3, ga3, wal, bal, gw):
    b, s, _ = proj3.shape
    npair = GLA_HEADS // 2
    qcol = 3 * MOBA_WIDTH // LANES
    kcol = qcol + GLA_KEY_WIDTH // LANES
    vcol = (3 * MOBA_WIDTH + 2 * GLA_KEY_WIDTH) // (2 * GLA_DV)
    gcol = vcol + npair
    return pl.pallas_call(
        _gla_kernel,
        grid=(b, npair),
        in_specs=[pl.BlockSpec((None, s, LANES), lambda bb, hp: (bb, 0, qcol + hp)),
                  pl.BlockSpec((None, s, LANES), lambda bb, hp: (bb, 0, kcol + hp)),
                  pl.BlockSpec((None, s, 2 * GLA_DV), lambda bb, hp: (bb, 0, vcol + hp)),
                  pl.BlockSpec((None, s, 2 * GLA_DV), lambda bb, hp: (bb, 0, gcol + hp)),
                  pl.BlockSpec((None, s, LANES), lambda bb, hp: (bb, 0, 0)),
                  pl.BlockSpec((LANES, LANES), lambda bb, hp: (0, hp)),
                  pl.BlockSpec((1, LANES), lambda bb, hp: (0, hp)),
                  pl.BlockSpec((1, GLA_DV), lambda bb, hp: (0, 0))],
        out_specs=pl.BlockSpec((None, s, 2 * GLA_DV), lambda bb, hp: (bb, 0, hp)),
        out_shape=jax.ShapeDtypeStruct((b, s, GLA_WIDTH), BF16),
        scratch_shapes=[pltpu.VMEM((s, LANES), F32),
                        pltpu.VMEM((2, GLA_DV, LANES), F32)],
        compiler_params=_params(2),
        name="gla",
    )(proj3, proj3, proj3, proj3, ga3, wal, bal, gw)


HALF = D_MODEL // 2


def _pack_rows(v):
    def bf16_bits(a):
        u = lax.bitcast_convert_type(a, jnp.uint32)
        return (u + (jnp.uint32(0x7FFF) + ((u >> 16) & jnp.uint32(1)))) >> 16
    return bf16_bits(v[:, :HALF]) | (bf16_bits(v[:, HALF:]) << 16)


def _unpack_rows(w):
    return (lax.bitcast_convert_type(w << 16, F32),
            lax.bitcast_convert_type(w & jnp.uint32(0xFFFF0000), F32))


def _outproj_kernel(oa_ref, ob_ref, x_ref, g1_ref, sc_ref, sh_ref, g2_ref, nw_ref, wo_ref,
                    ws1_ref, ws3_ref, ws2_ref, wrt_ref, base_ref, h_ref, st_ref):
    mix = (jnp.dot(oa_ref[...], wo_ref[:MOBA_WIDTH, :], preferred_element_type=F32)
           + jnp.dot(ob_ref[...], wo_ref[MOBA_WIDTH:, :], preferred_element_type=F32))
    x1 = x_ref[...] + g1_ref[...] * mix
    ms = jnp.mean(x1 * x1, axis=-1, keepdims=True)
    h = x1 * lax.rsqrt(ms + NORM_EPS) * nw_ref[...]
    h = h * (1.0 + sc_ref[...]) + sh_ref[...]
    h_ref[...] = _pack_rows(h)
    hb = h.astype(BF16)
    a = jnp.dot(hb, ws1_ref[...], preferred_element_type=F32)
    u = jnp.dot(hb, ws3_ref[...], preferred_element_type=F32)
    shared = jnp.dot((_silu(a) * u).astype(BF16), ws2_ref[...], preferred_element_type=F32)
    base_ref[...] = x1 + g2_ref[...] * shared
    logits_t = lax.dot_general(wrt_ref[...], hb, NT_DIMS, preferred_element_type=F32)
    st_ref[...] = jax.nn.sigmoid(logits_t)


def _outproj(oa, ob, x2, g1, sc, sh, g2, nw, wo, ws1, ws3, ws2, wrt, seq):
    t = x2.shape[0]
    tpb = seq // ROW_TILE
    vec = lambda: pl.BlockSpec((None, 1, D_MODEL), lambda i: (i // tpb, 0, 0))
    full = lambda a: pl.BlockSpec(a.shape, lambda i: (0,) * a.ndim)
    rows = lambda w: pl.BlockSpec((ROW_TILE, w), lambda i: (i, 0))
    return pl.pallas_call(
        _outproj_kernel,
        grid=(t // ROW_TILE,),
        in_specs=[rows(MOBA_WIDTH), rows(GLA_WIDTH), rows(D_MODEL), vec(), vec(), vec(), vec(),
                  full(nw), full(wo), full(ws1), full(ws3), full(ws2), full(wrt)],
        out_specs=[rows(D_MODEL), rows(HALF), pl.BlockSpec((N_EXPERTS, ROW_TILE), lambda i: (0, i))],
        out_shape=[jax.ShapeDtypeStruct((t, D_MODEL), F32),
                   jax.ShapeDtypeStruct((t, HALF), jnp.uint32),
                   jax.ShapeDtypeStruct((N_EXPERTS, t), F32)],
        compiler_params=_params(1),
        name="outproj",
    )(oa, ob, x2, g1, sc, sh, g2, nw, wo, ws1, ws3, ws2, wrt)


SLOT_CODE_SHIFT = 16
SLOT_CODE_BASE = 1 << SLOT_CODE_SHIFT


def _route_kernel(s_ref, eb_ref, code_ref, w_ref, cnt_ref, carry_ref):
    i = pl.program_id(0)
    ne, nt = s_ref.shape

    @pl.when(i == 0)
    def _init():
        carry_ref[...] = jnp.zeros_like(carry_ref)

    s = s_ref[...]
    choice = s + eb_ref[...]
    gio = lax.broadcasted_iota(jnp.int32, (GROUP_SIZE, nt), 0)
    gscore = []
    for g in range(N_GROUPS):
        cg = choice[g * GROUP_SIZE:(g + 1) * GROUP_SIZE, :]
        top1 = jnp.max(cg, axis=0, keepdims=True)
        first = jnp.min(jnp.where(cg == top1, gio, GROUP_SIZE), axis=0, keepdims=True)
        top2 = jnp.max(jnp.where(gio == first, -jnp.inf, cg), axis=0, keepdims=True)
        gscore.append(top1 + top2)
    gs = jnp.concatenate(gscore, axis=0)
    gidx = lax.broadcasted_iota(jnp.int32, gs.shape, 0)
    beaten = jnp.zeros(gs.shape, jnp.int32)
    for m in range(N_GROUPS):
        gm = gs[m:m + 1, :]
        beaten = beaten + jnp.where((gm > gs) | ((gm == gs) & (gidx > m)), 1, 0)
    gkeep = beaten < TOPK_GROUPS
    masked = jnp.concatenate(
        [jnp.where(gkeep[g:g + 1, :], choice[g * GROUP_SIZE:(g + 1) * GROUP_SIZE, :], -jnp.inf)
         for g in range(N_GROUPS)], axis=0)

    eio = lax.broadcasted_iota(jnp.int32, (ne, nt), 0)
    picked = jnp.zeros((ne, nt), F32)
    idx_rows, w_rows, hits = [], [], []
    for _ in range(TOP_K):
        mx = jnp.max(masked, axis=0, keepdims=True)
        idx = jnp.min(jnp.where(masked == mx, eio, ne), axis=0, keepdims=True)
        hit = eio == idx
        w_rows.append(jnp.sum(jnp.where(hit, s, 0.0), axis=0, keepdims=True))
        idx_rows.append(idx)
        hits.append(hit)
        masked = jnp.where(hit, -jnp.inf, masked)
        picked = jnp.where(hit, 1.0, picked)
    wk = jnp.concatenate(w_rows, axis=0)
    w_ref[...] = wk / jnp.sum(wk, axis=0, keepdims=True) * ROUTED_SCALE

    tr = lax.broadcasted_iota(jnp.int32, (nt, nt), 0)
    tc = lax.broadcasted_iota(jnp.int32, (nt, nt), 1)
    before = jnp.where(tr < tc, 1.0, 0.0).astype(BF16)
    pb = picked.astype(BF16)
    pos = carry_ref[...] + jnp.dot(pb, before, preferred_element_type=F32)
    rank = jnp.concatenate(
        [jnp.sum(jnp.where(hit, pos, 0.0), axis=0, keepdims=True) for hit in hits], axis=0).astype(jnp.int32)
    code_ref[...] = jnp.concatenate(idx_rows, axis=0) * SLOT_CODE_BASE + rank
    total = carry_ref[...] + jnp.dot(pb, jnp.ones((nt, nt), BF16), preferred_element_type=F32)
    carry_ref[...] = total
    cnt_ref[...] = total


def _route(scores_t, eb):
    ne, t = scores_t.shape
    assert t <= SLOT_CODE_BASE
    nt = ROUTE_TILE
    tok = lambda dt: jax.ShapeDtypeStruct((TOP_K, t), dt)
    return pl.pallas_call(
        _route_kernel,
        grid=(t // nt,),
        in_specs=[pl.BlockSpec((ne, nt), lambda i: (0, i)),
                  pl.BlockSpec((ne, nt), lambda i: (0, 0))],
        out_specs=[pl.BlockSpec((TOP_K, nt), lambda i: (0, i)),
                   pl.BlockSpec((TOP_K, nt), lambda i: (0, i)),
                   pl.BlockSpec((ne, nt), lambda i: (0, 0))],
        out_shape=[tok(jnp.int32), tok(F32), jax.ShapeDtypeStruct((ne, nt), F32)],
        scratch_shapes=[pltpu.VMEM((ne, nt), F32)],
        compiler_params=_params(1),
        name="route",
    )(scores_t, eb)


SLOT_TILE = 2048


def _slots_kernel(pstart_ref, code_ref, o_ref):
    code = code_ref[...]
    expert = lax.shift_right_logical(code, SLOT_CODE_SHIFT)

    def body(e, acc):
        return jnp.where(expert == e, pstart_ref[e], acc)

    start = lax.fori_loop(0, N_EXPERTS, body, jnp.zeros_like(code), unroll=8)
    o_ref[...] = start + (code & (SLOT_CODE_BASE - 1))


def _slots(pstart, code_t):
    k, t = code_t.shape
    return pl.pallas_call(
        _slots_kernel,
        grid_spec=pltpu.PrefetchScalarGridSpec(
            num_scalar_prefetch=1,
            grid=(t // SLOT_TILE,),
            in_specs=[pl.BlockSpec((k, SLOT_TILE), lambda i, p: (0, i))],
            out_specs=pl.BlockSpec((k, SLOT_TILE), lambda i, p: (0, i)),
        ),
        out_shape=jax.ShapeDtypeStruct((k, t), jnp.int32),
        compiler_params=_params(1),
        name="slots",
    )(pstart, code_t)


SC_WINDOW = 128


def _sc_gather_rows(table, idx_flat):
    from jax.experimental.pallas import tpu_sc as plsc
    info = plsc.get_sparse_core_info()
    nw = info.num_cores * info.num_subcores
    n = idx_flat.shape[0]
    width = table.shape[1]
    per_worker = n // nw
    assert per_worker * nw == n and per_worker % SC_WINDOW == 0
    mesh = plsc.VectorSubcoreMesh(core_axis_name="c", subcore_axis_name="s")

    def body(table_hbm, idx_hbm, out_hbm, idx_v, rows_v, sem):
        wid = lax.axis_index("s") * info.num_cores + lax.axis_index("c")
        base = wid * per_worker

        @pl.loop(0, per_worker // SC_WINDOW)
        def _(w):
            off = pl.multiple_of(base + w * SC_WINDOW, SC_WINDOW)
            pltpu.sync_copy(idx_hbm.at[pl.ds(off, SC_WINDOW)], idx_v)
            pltpu.async_copy(table_hbm.at[idx_v], rows_v, sem).wait()
            pltpu.sync_copy(rows_v, out_hbm.at[pl.ds(off, SC_WINDOW)])

    return pl.kernel(
        body,
        out_type=jax.ShapeDtypeStruct((n, width), table.dtype),
        mesh=mesh,
        scratch_types=[pltpu.VMEM((SC_WINDOW,), jnp.int32),
                       pltpu.VMEM((SC_WINDOW, width), table.dtype),
                       pltpu.SemaphoreType.DMA],
        name="sc_gather",
    )(table, idx_flat)


def _dispatch_kernel(dest_ref, h_ref, xs_ref, sem):
    i = pl.program_id(0)
    nt = h_ref.shape[0]

    def row_copy(t, d):
        return pltpu.make_async_copy(h_ref.at[pl.ds(t, 1), :], xs_ref.at[pl.ds(d, 1), :], sem)

    def start_body(t, carry):
        base = (i * nt + t) * TOP_K
        for k in range(TOP_K):
            row_copy(t, dest_ref[base + k]).start(priority=k % 2)
        return carry

    lax.fori_loop(0, nt, start_body, 0)

    def wait_body(t, carry):
        for k in range(TOP_K):
            row_copy(t, 0).wait()
        return carry

    lax.fori_loop(0, nt, wait_body, 0)


def _dispatch(dest_flat, h2, n_rows):
    t = h2.shape[0]
    nt = DISPATCH_TILE
    return pl.pallas_call(
        _dispatch_kernel,
        grid_spec=pltpu.PrefetchScalarGridSpec(
            num_scalar_prefetch=1,
            grid=(t // nt,),
            in_specs=[pl.BlockSpec((nt, HALF), lambda i, d: (i, 0))],
            out_specs=pl.BlockSpec(memory_space=pl.ANY),
            scratch_shapes=[pltpu.SemaphoreType.DMA],
        ),
        out_shape=jax.ShapeDtypeStruct((n_rows, HALF), jnp.uint32),
        compiler_params=_params(1),
        name="dispatch",
    )(dest_flat, h2)


def _expert_kernel(te_ref, nu_ref, xs_ref, w1_ref, w3_ref, w2_ref, ys_ref, w1b, w3b, w2b):
    i = pl.program_id(0)
    new_expert = (i == 0) | (te_ref[i] != te_ref[jnp.maximum(i - 1, 0)])

    @pl.when(new_expert)
    def _cast_weights():
        w1b[...] = w1_ref[...].astype(BF16)
        w3b[...] = w3_ref[...].astype(BF16)
        w2b[...] = w2_ref[...].astype(BF16)

    @pl.when(i < nu_ref[0])
    def _compute():
        lo, hi = _unpack_rows(xs_ref[...])
        lo, hi = lo.astype(BF16), hi.astype(BF16)
        a = (jnp.dot(lo, w1b[:HALF, :], preferred_element_type=F32)
             + jnp.dot(hi, w1b[HALF:, :], preferred_element_type=F32))
        u = (jnp.dot(lo, w3b[:HALF, :], preferred_element_type=F32)
             + jnp.dot(hi, w3b[HALF:, :], preferred_element_type=F32))
        y = jnp.dot((_silu(a) * u).astype(BF16), w2b[...], preferred_element_type=F32)
        ys_ref[...] = _pack_rows(y)


def _experts(tile_expert, n_used, xs, w1, w3, w2, layer):
    n_rows = xs.shape[0]
    r = EXPERT_TILE
    row_map = lambda i, te, nu: (jnp.minimum(i, nu[0] - 1), 0)
    w_map = lambda i, te, nu: (layer, te[i], 0, 0)
    return pl.pallas_call(
        _expert_kernel,
        grid_spec=pltpu.PrefetchScalarGridSpec(
            num_scalar_prefetch=2,
            grid=(n_rows // r,),
            in_specs=[pl.BlockSpec((r, HALF), row_map),
                      pl.BlockSpec((None, None, D_MODEL, EXPERT_FF), w_map),
                      pl.BlockSpec((None, None, D_MODEL, EXPERT_FF), w_map),
                      pl.BlockSpec((None, None, EXPERT_FF, D_MODEL), w_map)],
            out_specs=pl.BlockSpec((r, HALF), row_map),
            scratch_shapes=[pltpu.VMEM((D_MODEL, EXPERT_FF), BF16),
                            pltpu.VMEM((D_MODEL, EXPERT_FF), BF16),
                            pltpu.VMEM((EXPERT_FF, D_MODEL), BF16)],
        ),
        out_shape=jax.ShapeDtypeStruct((n_rows, HALF), jnp.uint32),
        compiler_params=_params(1),
        name="experts",
    )(tile_expert, n_used, xs, w1, w3, w2)


def _combine_kernel(dest_ref, base_ref, g2_ref, w_ref, ys_ref, o_ref, buf, sem):
    i = pl.program_id(0)
    nt = base_ref.shape[0]

    def row_copy(t, k, d):
        return pltpu.make_async_copy(ys_ref.at[pl.ds(d, 1), :], buf.at[k, pl.ds(t, 1), :], sem)

    def start_body(t, carry):
        base = (i * nt + t) * TOP_K
        for k in range(TOP_K):
            row_copy(t, k, dest_ref[base + k]).start(priority=k % 2)
        return carry

    lax.fori_loop(0, nt, start_body, 0)

    def wait_body(t, carry):
        for k in range(TOP_K):
            row_copy(t, k, 0).wait()
        return carry

    lax.fori_loop(0, nt, wait_body, 0)

    acc_lo = acc_hi = None
    for k in range(TOP_K):
        lo, hi = _unpack_rows(buf[k])
        wk = w_ref[:, k:k + 1]
        acc_lo = wk * lo if acc_lo is None else acc_lo + wk * lo
        acc_hi = wk * hi if acc_hi is None else acc_hi + wk * hi
    o_ref[:, :HALF] = base_ref[:, :HALF] + g2_ref[:, :HALF] * acc_lo
    o_ref[:, HALF:] = base_ref[:, HALF:] + g2_ref[:, HALF:] * acc_hi


def _combine_dense_kernel(base_ref, g2_ref, w_ref, yg_ref, o_ref):
    acc_lo = acc_hi = None
    for k in range(TOP_K):
        lo, hi = _unpack_rows(yg_ref[:, k * HALF:(k + 1) * HALF])
        wk = w_ref[:, k:k + 1]
        acc_lo = wk * lo if acc_lo is None else acc_lo + wk * lo
        acc_hi = wk * hi if acc_hi is None else acc_hi + wk * hi
    o_ref[:, :HALF] = base_ref[:, :HALF] + g2_ref[:, :HALF] * acc_lo
    o_ref[:, HALF:] = base_ref[:, HALF:] + g2_ref[:, HALF:] * acc_hi


def _combine_dense(base, g2, w_tok, yg, seq):
    t = base.shape[0]
    nt = ROUTE_TILE
    tpb = seq // nt
    return pl.pallas_call(
        _combine_dense_kernel,
        grid=(t // nt,),
        in_specs=[pl.BlockSpec((nt, D_MODEL), lambda i: (i, 0)),
                  pl.BlockSpec((None, 1, D_MODEL), lambda i: (i // tpb, 0, 0)),
                  pl.BlockSpec((nt, TOP_K), lambda i: (i, 0)),
                  pl.BlockSpec((nt, TOP_K * HALF), lambda i: (i, 0))],
        out_specs=pl.BlockSpec((nt, D_MODEL), lambda i: (i, 0)),
        out_shape=jax.ShapeDtypeStruct((t, D_MODEL), F32),
        compiler_params=_params(1),
        name="combine_dense",
    )(base, g2, w_tok, yg)


def _combine(dest_flat, base, g2, w_tok, ys, seq):
    t = base.shape[0]
    nt = COMBINE_TILE
    tpb = seq // nt
    return pl.pallas_call(
        _combine_kernel,
        grid_spec=pltpu.PrefetchScalarGridSpec(
            num_scalar_prefetch=1,
            grid=(t // nt,),
            in_specs=[pl.BlockSpec((nt, D_MODEL), lambda i, d: (i, 0)),
                      pl.BlockSpec((None, 1, D_MODEL), lambda i, d: (i // tpb, 0, 0)),
                      pl.BlockSpec((nt, TOP_K), lambda i, d: (i, 0)),
                      pl.BlockSpec(memory_space=pl.ANY)],
            out_specs=pl.BlockSpec((nt, D_MODEL), lambda i, d: (i, 0)),
            scratch_shapes=[pltpu.VMEM((TOP_K, nt, HALF), jnp.uint32),
                            pltpu.SemaphoreType.DMA],
        ),
        out_shape=jax.ShapeDtypeStruct((t, D_MODEL), F32),
        compiler_params=_params(1),
        name="combine",
    )(dest_flat, base, g2, w_tok, ys)


def _layer(layer, x, c, w_ada, b_ada, norm1_w, norm2_w, w_in, q_norm_w, k_norm_w, rel_bias, w_alpha, b_alpha,
           moba_out_w, gla_out_w, w_out, w_router, e_bias, w1, w3, w2, ws1, ws3, ws2):
    b, s, d = x.shape
    t = b * s
    x2 = x.reshape(t, d)

    mod = _mod(c, w_ada, b_ada)
    sh1, sc1, g1, sh2, sc2, g2 = [mod[:, j * d:(j + 1) * d].reshape(b, 1, d) for j in range(6)]

    w_main = w_in[:, :D_MAIN].astype(BF16)
    w_ga = jnp.zeros((d, LANES), BF16).at[:, :GLA_GATE_RANK].set(w_in[:, D_MAIN:].astype(BF16))
    per_chunk = 256 // MOBA_HEAD_DIM
    qw = jnp.tile(q_norm_w.astype(F32), per_chunk).reshape(1, 256) * (MOBA_HEAD_DIM ** -0.5)
    kw = jnp.tile(k_norm_w.astype(F32), per_chunk).reshape(1, 256)
    proj, ga = _inproj(x2, sc1, sh1, norm1_w.reshape(1, d), w_main, w_ga, qw, kw, s)
    proj3 = proj.reshape(b, s, D_MAIN)

    near, far = _moba_bias_tables(rel_bias)
    ow = jnp.tile(moba_out_w.astype(F32), 2).reshape(1, LANES)
    o_a = _moba(proj3, near, far, ow)

    wal = jnp.zeros((LANES, GLA_KEY_WIDTH), F32).at[:GLA_GATE_RANK].set(w_alpha)
    o_b = _gla(proj3, ga.reshape(b, s, LANES), wal, b_alpha.reshape(1, GLA_KEY_WIDTH),
               gla_out_w.reshape(1, GLA_DV))

    base, h2, scores_t = _outproj(
        o_a.reshape(t, MOBA_WIDTH), o_b.reshape(t, GLA_WIDTH), x2, g1, sc2, sh2, g2,
        norm2_w.reshape(1, d), w_out.astype(BF16), ws1.astype(BF16), ws3.astype(BF16), ws2.astype(BF16),
        w_router.T.astype(BF16), s)

    eb = jnp.broadcast_to(e_bias.astype(F32)[:, None], (N_EXPERTS, ROUTE_TILE))
    code_t, w_t, counts = _route(scores_t, eb)

    r = EXPERT_TILE
    n_tiles = (t * TOP_K + N_EXPERTS * (r - 1) + r - 1) // r
    n_rows = n_tiles * r
    cnt = counts[:, 0].astype(jnp.int32)
    padded = (cnt + r - 1) // r * r
    pend = jnp.cumsum(padded)
    pstart = pend - padded
    tile_row = jnp.arange(n_tiles, dtype=jnp.int32) * r
    tile_expert = jnp.minimum(jnp.sum((pend[None, :] <= tile_row[:, None]).astype(jnp.int32), axis=1),
                              N_EXPERTS - 1)
    n_used = (pend[-1:] // r).astype(jnp.int32)
    dest = _slots(pstart, code_t).T.reshape(t * TOP_K)

    xs = _dispatch(dest, h2, n_rows)
    ys = _experts(tile_expert, n_used, xs, w1, w3, w2, layer)
    yg = _sc_gather_rows(ys, dest).reshape(t, TOP_K * HALF)
    out = _combine_dense(base, g2, w_t.T, yg, s)
    return out.reshape(b, s, d)


def kernel(x, c, w_ada, b_ada, norm1_w, norm2_w, w_in, q_norm_w, k_norm_w, rel_bias, w_alpha, b_alpha,
           moba_out_w, gla_out_w, w_out, w_router, e_bias, w1, w3, w2, ws1, ws3, ws2):
    for l in range(w_ada.shape[0]):
        x = _layer(l, x, c, w_ada[l], b_ada[l], norm1_w[l], norm2_w[l], w_in[l], q_norm_w[l], k_norm_w[l],
                   rel_bias, w_alpha[l], b_alpha[l], moba_out_w[l], gla_out_w[l], w_out[l], w_router[l],
                   e_bias[l], w1, w3, w2, ws1[l], ws3[l], ws2[l])
    return x
```

```python
import functools
import math

import numpy as np
import jax
import jax.numpy as jnp
from jax import lax
from jax.experimental import pallas as pl
from jax.experimental.pallas import tpu as pltpu

D_MODEL = 1024
MOBA_HEADS = 8
MOBA_HEAD_DIM = 64
MOBA_WIDTH = MOBA_HEADS * MOBA_HEAD_DIM
MOBA_BLOCK = 256
MOBA_TOPK = 3
GLA_HEADS = 4
GLA_DK = 64
GLA_DV = 128
GLA_KEY_WIDTH = GLA_HEADS * GLA_DK
GLA_WIDTH = GLA_HEADS * GLA_DV
GLA_GATE_RANK = 16
GLA_GATE_TAU = 16.0
GLA_CHUNK = 64
REL_BUCKETS = 32
REL_MAX_DIST = 128
N_EXPERTS = 256
TOP_K = 8
N_GROUPS = 8
TOPK_GROUPS = 4
GROUP_SIZE = N_EXPERTS // N_GROUPS
EXPERT_FF = 256
SHARED_FF = 256
ROUTED_SCALE = 2.5
NORM_EPS = 1e-6

D_MAIN = 3 * MOBA_WIDTH + 2 * GLA_KEY_WIDTH + 2 * GLA_WIDTH
LANES = 128
VMEM_LIMIT = 56 * 1024 * 1024

ROW_TILE = 512
ROUTE_TILE = 256
DISPATCH_TILE = 256
COMBINE_TILE = 128
EXPERT_TILE = 256

F32 = jnp.float32
BF16 = jnp.bfloat16
NT_DIMS = (((1,), (1,)), ((), ()))
TN_DIMS = (((0,), (0,)), ((), ()))


def _params(n_axes):
    return pltpu.CompilerParams(dimension_semantics=("arbitrary",) * n_axes,
                                vmem_limit_bytes=VMEM_LIMIT)


def _silu(v):
    return v * jax.nn.sigmoid(v)


def _mod_kernel(c_ref, w_ref, b_ref, o_ref):
    o_ref[...] = jnp.dot(_silu(c_ref[...]), w_ref[...], preferred_element_type=F32) + b_ref[...]


def _mod(c, w, b):
    rows = 8
    cp = jnp.zeros((rows, D_MODEL), F32).at[:c.shape[0]].set(c)
    n = w.shape[1]
    tn = 1024
    out = pl.pallas_call(
        _mod_kernel,
        grid=(n // tn,),
        in_specs=[pl.BlockSpec((rows, D_MODEL), lambda j: (0, 0)),
                  pl.BlockSpec((D_MODEL, tn), lambda j: (0, j)),
                  pl.BlockSpec((1, tn), lambda j: (0, j))],
        out_specs=pl.BlockSpec((rows, tn), lambda j: (0, j)),
        out_shape=jax.ShapeDtypeStruct((rows, n), F32),
        compiler_params=_params(1),
        name="mod",
    )(cp, w, b.reshape(1, n))
    return out[:c.shape[0]]


def _group_rms_inv(a, group):
    lane = lax.broadcasted_iota(jnp.int32, (1, a.shape[1]), 1)
    a2 = a * a
    inv = jnp.zeros_like(a)
    for g in range(a.shape[1] // group):
        m = (lane >= g * group) & (lane < (g + 1) * group)
        ss = jnp.sum(jnp.where(m, a2, 0.0), axis=-1, keepdims=True)
        inv = jnp.where(m, lax.rsqrt(ss * (1.0 / group) + NORM_EPS), inv)
    return inv


def _inproj_kernel(x_ref, sc_ref, sh_ref, nw_ref, w_ref, wga_ref, qw_ref, kw_ref, o_ref, ga_ref):
    x = x_ref[...]
    ms = jnp.mean(x * x, axis=-1, keepdims=True)
    h = x * lax.rsqrt(ms + NORM_EPS) * nw_ref[...]
    h = h * (1.0 + sc_ref[...]) + sh_ref[...]
    hb = h.astype(BF16)
    cw = 256
    for j in range(D_MAIN // cw):
        acc = jnp.dot(hb, w_ref[:, j * cw:(j + 1) * cw], preferred_element_type=F32)
        if j < 2 * MOBA_WIDTH // cw:
            nw = qw_ref if j < MOBA_WIDTH // cw else kw_ref
            acc = acc * _group_rms_inv(acc, MOBA_HEAD_DIM) * nw[...]
        o_ref[:, j * cw:(j + 1) * cw] = acc.astype(BF16)
    ga_ref[...] = jnp.dot(hb, wga_ref[...], preferred_element_type=F32)


def _inproj(x2, sc, sh, nw, w_main, w_ga, qw, kw, seq):
    t = x2.shape[0]
    tpb = seq // ROW_TILE
    vec = lambda: pl.BlockSpec((None, 1, D_MODEL), lambda i: (i // tpb, 0, 0))
    full = lambda a: pl.BlockSpec(a.shape, lambda i: (0,) * a.ndim)
    return pl.pallas_call(
        _inproj_kernel,
        grid=(t // ROW_TILE,),
        in_specs=[pl.BlockSpec((ROW_TILE, D_MODEL), lambda i: (i, 0)), vec(), vec(),
                  full(nw), full(w_main), full(w_ga), full(qw), full(kw)],
        out_specs=[pl.BlockSpec((ROW_TILE, D_MAIN), lambda i: (i, 0)),
                   pl.BlockSpec((ROW_TILE, LANES), lambda i: (i, 0))],
        out_shape=[jax.ShapeDtypeStruct((t, D_MAIN), BF16),
                   jax.ShapeDtypeStruct((t, LANES), F32)],
        compiler_params=_params(1),
        name="inproj",
    )(x2, sc, sh, nw, w_main, w_ga, qw, kw)


def _t5_bucket_np(rel):
    max_exact = REL_BUCKETS // 2
    relf = np.maximum(rel, 1).astype(np.float64)
    large = max_exact + (np.log(relf / max_exact) / math.log(REL_MAX_DIST / max_exact)
                         * (REL_BUCKETS - max_exact)).astype(np.int32)
    large = np.minimum(large, REL_BUCKETS - 1)
    return np.where(rel < max_exact, rel, large)


def _bias_kernel(rb_ref, idx_ref, o_ref):
    h = pl.program_id(0)
    idx = idx_ref[...]
    tab = jnp.full(idx.shape, -jnp.inf, F32)
    for bk in range(REL_BUCKETS):
        tab = jnp.where(idx == bk, rb_ref[bk * MOBA_HEADS + h], tab)
    o_ref[...] = tab


def _moba_bias_tables(rel_bias):
    j = np.arange(MOBA_BLOCK)[:, None]
    i = np.arange(MOBA_BLOCK)[None, :]
    own_idx = np.where(j <= i, _t5_bucket_np(np.maximum(i - j, 0)), -1)
    prev_idx = _t5_bucket_np(MOBA_BLOCK + i - j)
    idx = jnp.asarray(np.concatenate([prev_idx, own_idx], axis=0).astype(np.int32))
    assert int(_t5_bucket_np(np.array([MOBA_BLOCK + 1]))[0]) == REL_BUCKETS - 1
    rb = rel_bias.astype(F32)
    near = pl.pallas_call(
        _bias_kernel,
        grid=(MOBA_HEADS,),
        in_specs=[pl.BlockSpec(memory_space=pltpu.SMEM),
                  pl.BlockSpec(idx.shape, lambda h: (0, 0))],
        out_specs=pl.BlockSpec((None,) + idx.shape, lambda h: (h, 0, 0)),
        out_shape=jax.ShapeDtypeStruct((MOBA_HEADS,) + idx.shape, F32),
        compiler_params=_params(1),
        name="bias",
    )(rb.reshape(-1), idx)
    return near, rb[REL_BUCKETS - 1]


FAR_GROUP = 4


def _moba_kernel(far_ref, q_ref, k_ref, v_ref, near_ref, ow_ref, o_ref,
                 kmean_ref, vt_ref, vtg_ref, acc_ref, m_ref, sel_ref):
    hp = pl.program_id(1)
    i = pl.program_id(2)
    nblk = k_ref.shape[0] // MOBA_BLOCK
    hd = MOBA_HEAD_DIM
    bs = MOBA_BLOCK

    @pl.when(i == 0)
    def _prepare():
        row = lax.broadcasted_iota(jnp.int32, (LANES, MOBA_BLOCK), 0)
        for n in range(nblk):
            kb = k_ref[n * MOBA_BLOCK:(n + 1) * MOBA_BLOCK, :].astype(F32)
            kmean_ref[n:n + 1, :] = jnp.mean(kb, axis=0, keepdims=True)
            vt = v_ref[n * MOBA_BLOCK:(n + 1) * MOBA_BLOCK, :].astype(F32).T
            vt0 = jnp.where(row < hd, vt, 1.0).astype(BF16)
            vt1 = jnp.where(row < hd, 1.0, vt).astype(BF16)
            vt_ref[0, n] = vt0
            vt_ref[1, n] = vt1
            gcols = slice((n % FAR_GROUP) * bs, (n % FAR_GROUP + 1) * bs)
            vtg_ref[0, n // FAR_GROUP, :, gcols] = vt0
            vtg_ref[1, n // FAR_GROUP, :, gcols] = vt1

    q = q_ref[...]
    lane = lax.broadcasted_iota(jnp.int32, q.shape, 1)
    zero = jnp.zeros_like(q)
    qh = (jnp.where(lane < hd, q, zero), jnp.where(lane < hd, zero, q))

    blk = lax.broadcasted_iota(jnp.int32, (nblk, MOBA_BLOCK), 0)
    for h in range(2):
        gt = lax.dot_general(kmean_ref[...], qh[h].astype(F32), NT_DIMS, preferred_element_type=F32)
        gt = jnp.where(blk < i, gt, -jnp.inf)
        cnt = jnp.zeros(gt.shape, jnp.int32)
        for m in range(nblk):
            gm = gt[m:m + 1, :]
            cnt = cnt + jnp.where((gm > gt) | ((gm == gt) & (blk > m)), 1, 0)
        keep = (blk < i) & (cnt < MOBA_TOPK)
        sel_ref[0, h] = jnp.where(keep, 1.0, 0.0)
        sel_ref[1, h] = jnp.where(keep & (blk < i - 1), 1.0, 0.0)

    @pl.when(i == 0)
    def _own_block_only():
        kb = k_ref[0:bs, :]
        for h in range(2):
            s = lax.dot_general(kb, qh[h], NT_DIMS, preferred_element_type=F32) + near_ref[h, bs:2 * bs, :]
            m_new = jnp.max(s, axis=0, keepdims=True)
            p = jnp.exp(s - m_new).astype(BF16)
            acc_ref[h] = jnp.dot(vt_ref[h, 0], p, preferred_element_type=F32)
            m_ref[h] = m_new

    @pl.when(i >= 1)
    def _previous_and_own_block():
        k_prev = k_ref[pl.ds(pl.multiple_of((i - 1) * bs, bs), bs), :]
        k_own = k_ref[pl.ds(pl.multiple_of(i * bs, bs), bs), :]
        for h in range(2):
            s_own = lax.dot_general(k_own, qh[h], NT_DIMS, preferred_element_type=F32) + near_ref[h, bs:2 * bs, :]
            m_own = jnp.max(s_own, axis=0, keepdims=True)
            pv_own = jnp.dot(vt_ref[h, i], jnp.exp(s_own - m_own).astype(BF16), preferred_element_type=F32)
            s_prev = lax.dot_general(k_prev, qh[h], NT_DIMS, preferred_element_type=F32) + near_ref[h, 0:bs, :]
            keep = sel_ref[0, h, pl.ds(i - 1, 1), :] > 0.5
            mx = jnp.max(s_prev, axis=0, keepdims=True)
            p_prev = jnp.exp(s_prev - jnp.where(keep, mx, jnp.inf)).astype(BF16)
            pv_prev = jnp.dot(vt_ref[h, i - 1], p_prev, preferred_element_type=F32)
            m_new = jnp.maximum(m_own, jnp.where(keep, mx, -jnp.inf))
            acc_ref[h] = (pv_own * jnp.exp(m_own - m_new)
                          + pv_prev * jnp.exp(jnp.where(keep, mx, -jnp.inf) - m_new))
            m_ref[h] = m_new

    def far_body(g, carry):
        gk = FAR_GROUP * bs
        kb = k_ref[pl.ds(pl.multiple_of(g * gk, gk), gk), :]
        ss = [lax.dot_general(kb, qh[h], NT_DIMS, preferred_element_type=F32) for h in range(2)]
        for h in range(2):
            s = ss[h]
            fb = far_ref[2 * hp + h]
            m_old = m_ref[h]
            m_new = m_old
            keeps = []
            for j in range(FAR_GROUP):
                keep = sel_ref[1, h, pl.ds(g * FAR_GROUP + j, 1), :] > 0.5
                mx = jnp.max(s[j * bs:(j + 1) * bs], axis=0, keepdims=True) + fb
                m_new = jnp.maximum(m_new, jnp.where(keep, mx, -jnp.inf))
                keeps.append(keep)
            p = jnp.concatenate(
                [jnp.exp(s[j * bs:(j + 1) * bs] - jnp.where(keeps[j], m_new - fb, jnp.inf)).astype(BF16)
                 for j in range(FAR_GROUP)], axis=0)
            pv = jnp.dot(vtg_ref[h, g], p, preferred_element_type=F32)
            acc_ref[h] = acc_ref[h] * jnp.exp(m_old - m_new) + pv
            m_ref[h] = m_new
        return carry

    lax.fori_loop(0, (i + FAR_GROUP - 2) // FAR_GROUP, far_body, 0)

    a0 = acc_ref[0]
    a1 = acc_ref[1]
    row = lax.broadcasted_iota(jnp.int32, a0.shape, 0)
    ot = jnp.where(row < hd, a0 / a0[hd:hd + 1, :], a1 / a1[0:1, :])
    o2 = ot * ot
    ss0 = jnp.sum(jnp.where(row < hd, o2, 0.0), axis=0, keepdims=True)
    ss1 = jnp.sum(jnp.where(row < hd, 0.0, o2), axis=0, keepdims=True)
    inv = jnp.where(row < hd, lax.rsqrt(ss0 * (1.0 / hd) + NORM_EPS), lax.rsqrt(ss1 * (1.0 / hd) + NORM_EPS))
    o_ref[...] = ((ot * inv).T * ow_ref[...]).astype(o_ref.dtype)


def _moba(proj3, near, far, ow):
    b, s, _ = proj3.shape
    nblk = s // MOBA_BLOCK
    assert nblk % FAR_GROUP == 0
    npair = MOBA_HEADS // 2
    kcol = MOBA_WIDTH // LANES
    return pl.pallas_call(
        _moba_kernel,
        grid=(b, npair, nblk),
        in_specs=[pl.BlockSpec(memory_space=pltpu.SMEM),
                  pl.BlockSpec((None, MOBA_BLOCK, LANES), lambda bb, hp, i: (bb, i, hp)),
                  pl.BlockSpec((None, s, LANES), lambda bb, hp, i: (bb, 0, kcol + hp)),
                  pl.BlockSpec((None, s, LANES), lambda bb, hp, i: (bb, 0, 2 * kcol + hp)),
                  pl.BlockSpec((2, 2 * MOBA_BLOCK, MOBA_BLOCK), lambda bb, hp, i: (hp, 0, 0)),
                  pl.BlockSpec((1, LANES), lambda bb, hp, i: (0, 0))],
        out_specs=pl.BlockSpec((None, MOBA_BLOCK, LANES), lambda bb, hp, i: (bb, i, hp)),
        out_shape=jax.ShapeDtypeStruct((b, s, MOBA_WIDTH), BF16),
        scratch_shapes=[pltpu.VMEM((nblk, LANES), F32),
                        pltpu.VMEM((2, nblk, LANES, MOBA_BLOCK), BF16),
                        pltpu.VMEM((2, nblk // FAR_GROUP, LANES, FAR_GROUP * MOBA_BLOCK), BF16),
                        pltpu.VMEM((2, LANES, MOBA_BLOCK), F32),
                        pltpu.VMEM((2, 1, MOBA_BLOCK), F32),
                        pltpu.VMEM((2, 2, nblk, MOBA_BLOCK), F32)],
        compiler_params=_params(3),
        name="moba",
    )(far, proj3, proj3, proj3, near, ow)


def _split3(v):
    hi = v.astype(BF16)
    r1 = v - hi.astype(F32)
    mid = r1.astype(BF16)
    lo = (r1 - mid.astype(F32)).astype(BF16)
    return hi, mid, lo


def _gla_kernel(q_ref, k_ref, v_ref, g_ref, ga_ref, wal_ref, bal_ref, gw_ref, o_ref, b_ref, st_ref):
    seq = q_ref.shape[0]
    c = GLA_CHUNK
    pc = 256

    rr = lax.broadcasted_iota(jnp.int32, (pc, pc), 0)
    cc = lax.broadcasted_iota(jnp.int32, (pc, pc), 1)
    tri = jnp.where((rr >= cc) & (rr // c == cc // c), 1.0, 0.0).astype(BF16)

    def decay_body(j, carry):
        r0 = pl.multiple_of(j * pc, pc)
        xg = jnp.dot(ga_ref[pl.ds(r0, pc), :], wal_ref[...], preferred_element_type=F32) + bal_ref[...]
        la = (jnp.minimum(xg, 0.0) - jnp.log(1.0 + jnp.exp(-jnp.abs(xg)))) * (1.0 / GLA_GATE_TAU)
        hi, mid, lo = _split3(la)
        b_ref[pl.ds(r0, pc), :] = (jnp.dot(tri, hi, preferred_element_type=F32)
                                   + jnp.dot(tri, mid, preferred_element_type=F32)
                                   + jnp.dot(tri, lo, preferred_element_type=F32))
        return carry

    lax.fori_loop(0, seq // pc, decay_body, 0)

    st_ref[...] = jnp.zeros_like(st_ref)
    lane = lax.broadcasted_iota(jnp.int32, (c, LANES), 1)
    head_mask = (lane < GLA_DK, lane >= GLA_DK)
    causal = lax.broadcasted_iota(jnp.int32, (c, c), 0) >= lax.broadcasted_iota(jnp.int32, (c, c), 1)

    def chunk_body(ci, carry):
        r0 = pl.multiple_of(ci * c, c)
        b = b_ref[pl.ds(r0, c), :]
        ref_row = b[c // 2 - 1:c // 2, :]
        last = b[c - 1:c, :]
        q = q_ref[pl.ds(r0, c), :].astype(F32) * (GLA_DK ** -0.5)
        k = k_ref[pl.ds(r0, c), :].astype(F32)
        qt = q * jnp.exp(b - ref_row)
        kt = (k * jnp.exp(ref_row - b)).astype(BF16)
        qs = q * jnp.exp(b)
        ke = (k * jnp.exp(last - b)).astype(BF16)
        e_last = jnp.exp(last)
        for h in range(2):
            cols = slice(h * GLA_DV, (h + 1) * GLA_DV)
            a = lax.dot_general(jnp.where(head_mask[h], qt, 0.0).astype(BF16), kt, NT_DIMS,
                                preferred_element_type=F32)
            a = jnp.where(causal, a, 0.0).astype(BF16)
            v = v_ref[pl.ds(r0, c), cols]
            st = st_ref[h]
            o = jnp.dot(a, v, preferred_element_type=F32)
            o = o + lax.dot_general(jnp.where(head_mask[h], qs, 0.0).astype(BF16), st.astype(BF16), NT_DIMS,
                                    preferred_element_type=F32)
            st_ref[h] = st * e_last + lax.dot_general(v, ke, TN_DIMS, preferred_element_type=F32)
            ms = jnp.mean(o * o, axis=-1, keepdims=True)
            on = o * lax.rsqrt(ms + NORM_EPS) * gw_ref[...]
            g = g_ref[pl.ds(r0, c), cols].astype(F32)
            o_ref[pl.ds(r0, c), cols] = (on * _silu(g)).astype(o_ref.dtype)
        return carry

    lax.fori_loop(0, seq // c, chunk_body, 0)


def _gla(proj3, ga3, wal, bal, gw):
    b, s, _ = proj3.shape
    npair = GLA_HEADS // 2
    qcol = 3 * MOBA_WIDTH // LANES
    kcol = qcol + GLA_KEY_WIDTH // LANES
    vcol = (3 * MOBA_WIDTH + 2 * GLA_KEY_WIDTH) // (2 * GLA_DV)
    gcol = vcol + npair
    return pl.pallas_call(
        _gla_kernel,
        grid=(b, npair),
        in_specs=[pl.BlockSpec((None, s, LANES), lambda bb, hp: (bb, 0, qcol + hp)),
                  pl.BlockSpec((None, s, LANES), lambda bb, hp: (bb, 0, kcol + hp)),
                  pl.BlockSpec((None, s, 2 * GLA_DV), lambda bb, hp: (bb, 0, vcol + hp)),
                  pl.BlockSpec((None, s, 2 * GLA_DV), lambda bb, hp: (bb, 0, gcol + hp)),
                  pl.BlockSpec((None, s, LANES), lambda bb, hp: (bb, 0, 0)),
                  pl.BlockSpec((LANES, LANES), lambda bb, hp: (0, hp)),
                  pl.BlockSpec((1, LANES), lambda bb, hp: (0, hp)),
                  pl.BlockSpec((1, GLA_DV), lambda bb, hp: (0, 0))],
        out_specs=pl.BlockSpec((None, s, 2 * GLA_DV), lambda bb, hp: (bb, 0, hp)),
        out_shape=jax.ShapeDtypeStruct((b, s, GLA_WIDTH), BF16),
        scratch_shapes=[pltpu.VMEM((s, LANES), F32),
                        pltpu.VMEM((2, GLA_DV, LANES), F32)],
        compiler_params=_params(2),
        name="gla",
    )(proj3, proj3, proj3, proj3, ga3, wal, bal, gw)


HALF = D_MODEL // 2


def _pack_rows(v):
    def bf16_bits(a):
        u = lax.bitcast_convert_type(a, jnp.uint32)
        return (u + (jnp.uint32(0x7FFF) + ((u >> 16) & jnp.uint32(1)))) >> 16
    return bf16_bits(v[:, :HALF]) | (bf16_bits(v[:, HALF:]) << 16)


def _unpack_rows(w):
    return (lax.bitcast_convert_type(w << 16, F32),
            lax.bitcast_convert_type(w & jnp.uint32(0xFFFF0000), F32))


def _outproj_kernel(oa_ref, ob_ref, x_ref, g1_ref, sc_ref, sh_ref, g2_ref, nw_ref, wo_ref,
                    ws1_ref, ws3_ref, ws2_ref, wrt_ref, base_ref, h_ref, st_ref):
    mix = (jnp.dot(oa_ref[...], wo_ref[:MOBA_WIDTH, :], preferred_element_type=F32)
           + jnp.dot(ob_ref[...], wo_ref[MOBA_WIDTH:, :], preferred_element_type=F32))
    x1 = x_ref[...] + g1_ref[...] * mix
    ms = jnp.mean(x1 * x1, axis=-1, keepdims=True)
    h = x1 * lax.rsqrt(ms + NORM_EPS) * nw_ref[...]
    h = h * (1.0 + sc_ref[...]) + sh_ref[...]
    h_ref[...] = _pack_rows(h)
    hb = h.astype(BF16)
    a = jnp.dot(hb, ws1_ref[...], preferred_element_type=F32)
    u = jnp.dot(hb, ws3_ref[...], preferred_element_type=F32)
    shared = jnp.dot((_silu(a) * u).astype(BF16), ws2_ref[...], preferred_element_type=F32)
    base_ref[...] = x1 + g2_ref[...] * shared
    logits_t = lax.dot_general(wrt_ref[...], hb, NT_DIMS, preferred_element_type=F32)
    st_ref[...] = jax.nn.sigmoid(logits_t)


def _outproj(oa, ob, x2, g1, sc, sh, g2, nw, wo, ws1, ws3, ws2, wrt, seq):
    t = x2.shape[0]
    tpb = seq // ROW_TILE
    vec = lambda: pl.BlockSpec((None, 1, D_MODEL), lambda i: (i // tpb, 0, 0))
    full = lambda a: pl.BlockSpec(a.shape, lambda i: (0,) * a.ndim)
    rows = lambda w: pl.BlockSpec((ROW_TILE, w), lambda i: (i, 0))
    return pl.pallas_call(
        _outproj_kernel,
        grid=(t // ROW_TILE,),
        in_specs=[rows(MOBA_WIDTH), rows(GLA_WIDTH), rows(D_MODEL), vec(), vec(), vec(), vec(),
                  full(nw), full(wo), full(ws1), full(ws3), full(ws2), full(wrt)],
        out_specs=[rows(D_MODEL), rows(HALF), pl.BlockSpec((N_EXPERTS, ROW_TILE), lambda i: (0, i))],
        out_shape=[jax.ShapeDtypeStruct((t, D_MODEL), F32),
                   jax.ShapeDtypeStruct((t, HALF), jnp.uint32),
                   jax.ShapeDtypeStruct((N_EXPERTS, t), F32)],
        compiler_params=_params(1),
        name="outproj",
    )(oa, ob, x2, g1, sc, sh, g2, nw, wo, ws1, ws3, ws2, wrt)


SLOT_CODE_SHIFT = 16
SLOT_CODE_BASE = 1 << SLOT_CODE_SHIFT


def _route_kernel(s_ref, eb_ref, code_ref, w_ref, cnt_ref, carry_ref):
    i = pl.program_id(0)
    ne, nt = s_ref.shape

    @pl.when(i == 0)
    def _init():
        carry_ref[...] = jnp.zeros_like(carry_ref)

    s = s_ref[...]
    choice = s + eb_ref[...]
    gio = lax.broadcasted_iota(jnp.int32, (GROUP_SIZE, nt), 0)
    gscore = []
    for g in range(N_GROUPS):
        cg = choice[g * GROUP_SIZE:(g + 1) * GROUP_SIZE, :]
        top1 = jnp.max(cg, axis=0, keepdims=True)
        first = jnp.min(jnp.where(cg == top1, gio, GROUP_SIZE), axis=0, keepdims=True)
        top2 = jnp.max(jnp.where(gio == first, -jnp.inf, cg), axis=0, keepdims=True)
        gscore.append(top1 + top2)
    gs = jnp.concatenate(gscore, axis=0)
    gidx = lax.broadcasted_iota(jnp.int32, gs.shape, 0)
    beaten = jnp.zeros(gs.shape, jnp.int32)
    for m in range(N_GROUPS):
        gm = gs[m:m + 1, :]
        beaten = beaten + jnp.where((gm > gs) | ((gm == gs) & (gidx > m)), 1, 0)
    gkeep = beaten < TOPK_GROUPS
    masked = jnp.concatenate(
        [jnp.where(gkeep[g:g + 1, :], choice[g * GROUP_SIZE:(g + 1) * GROUP_SIZE, :], -jnp.inf)
         for g in range(N_GROUPS)], axis=0)

    eio = lax.broadcasted_iota(jnp.int32, (ne, nt), 0)
    picked = jnp.zeros((ne, nt), F32)
    idx_rows, w_rows, hits = [], [], []
    for _ in range(TOP_K):
        mx = jnp.max(masked, axis=0, keepdims=True)
        idx = jnp.min(jnp.where(masked == mx, eio, ne), axis=0, keepdims=True)
        hit = eio == idx
        w_rows.append(jnp.sum(jnp.where(hit, s, 0.0), axis=0, keepdims=True))
        idx_rows.append(idx)
        hits.append(hit)
        masked = jnp.where(hit, -jnp.inf, masked)
        picked = jnp.where(hit, 1.0, picked)
    wk = jnp.concatenate(w_rows, axis=0)
    w_ref[...] = wk / jnp.sum(wk, axis=0, keepdims=True) * ROUTED_SCALE

    tr = lax.broadcasted_iota(jnp.int32, (nt, nt), 0)
    tc = lax.broadcasted_iota(jnp.int32, (nt, nt), 1)
    before = jnp.where(tr < tc, 1.0, 0.0).astype(BF16)
    pb = picked.astype(BF16)
    pos = carry_ref[...] + jnp.dot(pb, before, preferred_element_type=F32)
    rank = jnp.concatenate(
        [jnp.sum(jnp.where(hit, pos, 0.0), axis=0, keepdims=True) for hit in hits], axis=0).astype(jnp.int32)
    code_ref[...] = jnp.concatenate(idx_rows, axis=0) * SLOT_CODE_BASE + rank
    total = carry_ref[...] + jnp.dot(pb, jnp.ones((nt, nt), BF16), preferred_element_type=F32)
    carry_ref[...] = total
    cnt_ref[...] = total


def _route(scores_t, eb):
    ne, t = scores_t.shape
    assert t <= SLOT_CODE_BASE
    nt = ROUTE_TILE
    tok = lambda dt: jax.ShapeDtypeStruct((TOP_K, t), dt)
    return pl.pallas_call(
        _route_kernel,
        grid=(t // nt,),
        in_specs=[pl.BlockSpec((ne, nt), lambda i: (0, i)),
                  pl.BlockSpec((ne, nt), lambda i: (0, 0))],
        out_specs=[pl.BlockSpec((TOP_K, nt), lambda i: (0, i)),
                   pl.BlockSpec((TOP_K, nt), lambda i: (0, i)),
                   pl.BlockSpec((ne, nt), lambda i: (0, 0))],
        out_shape=[tok(jnp.int32), tok(F32), jax.ShapeDtypeStruct((ne, nt), F32)],
        scratch_shapes=[pltpu.VMEM((ne, nt), F32)],
        compiler_params=_params(1),
        name="route",
    )(scores_t, eb)


SLOT_TILE = 2048


def _slots_kernel(pstart_ref, code_ref, o_ref):
    code = code_ref[...]
    expert = lax.shift_right_logical(code, SLOT_CODE_SHIFT)

    def body(e, acc):
        return jnp.where(expert == e, pstart_ref[e], acc)

    start = lax.fori_loop(0, N_EXPERTS, body, jnp.zeros_like(code), unroll=8)
    o_ref[...] = start + (code & (SLOT_CODE_BASE - 1))


def _slots(pstart, code_t):
    k, t = code_t.shape
    return pl.pallas_call(
        _slots_kernel,
        grid_spec=pltpu.PrefetchScalarGridSpec(
            num_scalar_prefetch=1,
            grid=(t // SLOT_TILE,),
            in_specs=[pl.BlockSpec((k, SLOT_TILE), lambda i, p: (0, i))],
            out_specs=pl.BlockSpec((k, SLOT_TILE), lambda i, p: (0, i)),
        ),
        out_shape=jax.ShapeDtypeStruct((k, t), jnp.int32),
        compiler_params=_params(1),
        name="slots",
    )(pstart, code_t)


SC_WINDOW = 128


def _sc_gather_rows(table, idx_flat):
    from jax.experimental.pallas import tpu_sc as plsc
    info = plsc.get_sparse_core_info()
    nw = info.num_cores * info.num_subcores
    n = idx_flat.shape[0]
    width = table.shape[1]
    per_worker = n // nw
    assert per_worker * nw == n and per_worker % SC_WINDOW == 0
    mesh = plsc.VectorSubcoreMesh(core_axis_name="c", subcore_axis_name="s")

    def body(table_hbm, idx_hbm, out_hbm, idx_v, rows_v, sem):
        wid = lax.axis_index("s") * info.num_cores + lax.axis_index("c")
        base = wid * per_worker

        @pl.loop(0, per_worker // SC_WINDOW)
        def _(w):
            off = pl.multiple_of(base + w * SC_WINDOW, SC_WINDOW)
            pltpu.sync_copy(idx_hbm.at[pl.ds(off, SC_WINDOW)], idx_v)
            pltpu.async_copy(table_hbm.at[idx_v], rows_v, sem).wait()
            pltpu.sync_copy(rows_v, out_hbm.at[pl.ds(off, SC_WINDOW)])

    return pl.kernel(
        body,
        out_type=jax.ShapeDtypeStruct((n, width), table.dtype),
        mesh=mesh,
        scratch_types=[pltpu.VMEM((SC_WINDOW,), jnp.int32),
                       pltpu.VMEM((SC_WINDOW, width), table.dtype),
                       pltpu.SemaphoreType.DMA],
        name="sc_gather",
    )(table, idx_flat)


def _sc_scatter_rows(rows, idx_kt, n_out):
    from jax.experimental.pallas import tpu_sc as plsc
    info = plsc.get_sparse_core_info()
    nw = info.num_cores * info.num_subcores
    t, width = rows.shape
    nk = idx_kt.shape[0]
    per_worker = t // nw
    assert per_worker * nw == t and per_worker % SC_WINDOW == 0
    mesh = plsc.VectorSubcoreMesh(core_axis_name="c", subcore_axis_name="s")

    def body(rows_hbm, idx_hbm, out_hbm, idx_v, rows_v, sem):
        wid = lax.axis_index("s") * info.num_cores + lax.axis_index("c")
        base = wid * per_worker

        @pl.loop(0, per_worker // SC_WINDOW)
        def _(w):
            off = pl.multiple_of(base + w * SC_WINDOW, SC_WINDOW)
            pltpu.sync_copy(rows_hbm.at[pl.ds(off, SC_WINDOW)], rows_v)
            pltpu.sync_copy(idx_hbm.at[:, pl.ds(off, SC_WINDOW)], idx_v)
            copies = [pltpu.async_copy(rows_v, out_hbm.at[idx_v.at[k]], sem) for k in range(nk)]
            for cp in copies:
                cp.wait()

    return pl.kernel(
        body,
        out_type=jax.ShapeDtypeStruct((n_out, width), rows.dtype),
        mesh=mesh,
        scratch_types=[pltpu.VMEM((nk, SC_WINDOW), jnp.int32),
                       pltpu.VMEM((SC_WINDOW, width), rows.dtype),
                       pltpu.SemaphoreType.DMA],
        name="sc_scatter",
    )(rows, idx_kt)


def _dispatch_kernel(dest_ref, h_ref, xs_ref, sem):
    i = pl.program_id(0)
    nt = h_ref.shape[0]

    def row_copy(t, d):
        return pltpu.make_async_copy(h_ref.at[pl.ds(t, 1), :], xs_ref.at[pl.ds(d, 1), :], sem)

    def start_body(t, carry):
        base = (i * nt + t) * TOP_K
        for k in range(TOP_K):
            row_copy(t, dest_ref[base + k]).start(priority=k % 2)
        return carry

    lax.fori_loop(0, nt, start_body, 0)

    def wait_body(t, carry):
        for k in range(TOP_K):
            row_copy(t, 0).wait()
        return carry

    lax.fori_loop(0, nt, wait_body, 0)


def _dispatch(dest_flat, h2, n_rows):
    t = h2.shape[0]
    nt = DISPATCH_TILE
    return pl.pallas_call(
        _dispatch_kernel,
        grid_spec=pltpu.PrefetchScalarGridSpec(
            num_scalar_prefetch=1,
            grid=(t // nt,),
            in_specs=[pl.BlockSpec((nt, HALF), lambda i, d: (i, 0))],
            out_specs=pl.BlockSpec(memory_space=pl.ANY),
            scratch_shapes=[pltpu.SemaphoreType.DMA],
        ),
        out_shape=jax.ShapeDtypeStruct((n_rows, HALF), jnp.uint32),
        compiler_params=_params(1),
        name="dispatch",
    )(dest_flat, h2)


def _expert_kernel(te_ref, nu_ref, xs_ref, w1_ref, w3_ref, w2_ref, ys_ref, w1b, w3b, w2b):
    i = pl.program_id(0)
    new_expert = (i == 0) | (te_ref[i] != te_ref[jnp.maximum(i - 1, 0)])

    @pl.when(new_expert)
    def _cast_weights():
        w1b[...] = w1_ref[...].astype(BF16)
        w3b[...] = w3_ref[...].astype(BF16)
        w2b[...] = w2_ref[...].astype(BF16)

    @pl.when(i < nu_ref[0])
    def _compute():
        lo, hi = _unpack_rows(xs_ref[...])
        lo, hi = lo.astype(BF16), hi.astype(BF16)
        a = (jnp.dot(lo, w1b[:HALF, :], preferred_element_type=F32)
             + jnp.dot(hi, w1b[HALF:, :], preferred_element_type=F32))
        u = (jnp.dot(lo, w3b[:HALF, :], preferred_element_type=F32)
             + jnp.dot(hi, w3b[HALF:, :], preferred_element_type=F32))
        y = jnp.dot((_silu(a) * u).astype(BF16), w2b[...], preferred_element_type=F32)
        ys_ref[...] = _pack_rows(y)


def _experts(tile_expert, n_used, xs, w1, w3, w2, layer):
    n_rows = xs.shape[0]
    r = EXPERT_TILE
    row_map = lambda i, te, nu: (jnp.minimum(i, nu[0] - 1), 0)
    w_map = lambda i, te, nu: (layer, te[i], 0, 0)
    return pl.pallas_call(
        _expert_kernel,
        grid_spec=pltpu.PrefetchScalarGridSpec(
            num_scalar_prefetch=2,
            grid=(n_rows // r,),
            in_specs=[pl.BlockSpec((r, HALF), row_map),
                      pl.BlockSpec((None, None, D_MODEL, EXPERT_FF), w_map),
                      pl.BlockSpec((None, None, D_MODEL, EXPERT_FF), w_map),
                      pl.BlockSpec((None, None, EXPERT_FF, D_MODEL), w_map)],
            out_specs=pl.BlockSpec((r, HALF), row_map),
            scratch_shapes=[pltpu.VMEM((D_MODEL, EXPERT_FF), BF16),
                            pltpu.VMEM((D_MODEL, EXPERT_FF), BF16),
                            pltpu.VMEM((EXPERT_FF, D_MODEL), BF16)],
        ),
        out_shape=jax.ShapeDtypeStruct((n_rows, HALF), jnp.uint32),
        compiler_params=_params(1),
        name="experts",
    )(tile_expert, n_used, xs, w1, w3, w2)


def _combine_kernel(dest_ref, base_ref, g2_ref, w_ref, ys_ref, o_ref, buf, sem):
    i = pl.program_id(0)
    nt = base_ref.shape[0]

    def row_copy(t, k, d):
        return pltpu.make_async_copy(ys_ref.at[pl.ds(d, 1), :], buf.at[k, pl.ds(t, 1), :], sem)

    def start_body(t, carry):
        base = (i * nt + t) * TOP_K
        for k in range(TOP_K):
            row_copy(t, k, dest_ref[base + k]).start(priority=k % 2)
        return carry

    lax.fori_loop(0, nt, start_body, 0)

    def wait_body(t, carry):
        for k in range(TOP_K):
            row_copy(t, k, 0).wait()
        return carry

    lax.fori_loop(0, nt, wait_body, 0)

    acc_lo = acc_hi = None
    for k in range(TOP_K):
        lo, hi = _unpack_rows(buf[k])
        wk = w_ref[:, k:k + 1]
        acc_lo = wk * lo if acc_lo is None else acc_lo + wk * lo
        acc_hi = wk * hi if acc_hi is None else acc_hi + wk * hi
    o_ref[:, :HALF] = base_ref[:, :HALF] + g2_ref[:, :HALF] * acc_lo
    o_ref[:, HALF:] = base_ref[:, HALF:] + g2_ref[:, HALF:] * acc_hi


def _combine_dense_kernel(base_ref, g2_ref, w_ref, yg_ref, o_ref):
    acc_lo = acc_hi = None
    for k in range(TOP_K):
        lo, hi = _unpack_rows(yg_ref[k])
        wk = w_ref[:, k:k + 1]
        acc_lo = wk * lo if acc_lo is None else acc_lo + wk * lo
        acc_hi = wk * hi if acc_hi is None else acc_hi + wk * hi
    o_ref[:, :HALF] = base_ref[:, :HALF] + g2_ref[:, :HALF] * acc_lo
    o_ref[:, HALF:] = base_ref[:, HALF:] + g2_ref[:, HALF:] * acc_hi


def _combine_dense(base, g2, w_tok, yg, seq):
    t = base.shape[0]
    nt = ROUTE_TILE
    tpb = seq // nt
    return pl.pallas_call(
        _combine_dense_kernel,
        grid=(t // nt,),
        in_specs=[pl.BlockSpec((nt, D_MODEL), lambda i: (i, 0)),
                  pl.BlockSpec((None, 1, D_MODEL), lambda i: (i // tpb, 0, 0)),
                  pl.BlockSpec((nt, TOP_K), lambda i: (i, 0)),
                  pl.BlockSpec((TOP_K, nt, HALF), lambda i: (0, i, 0))],
        out_specs=pl.BlockSpec((nt, D_MODEL), lambda i: (i, 0)),
        out_shape=jax.ShapeDtypeStruct((t, D_MODEL), F32),
        compiler_params=_params(1),
        name="combine_dense",
    )(base, g2, w_tok, yg)


def _combine(dest_flat, base, g2, w_tok, ys, seq):
    t = base.shape[0]
    nt = COMBINE_TILE
    tpb = seq // nt
    return pl.pallas_call(
        _combine_kernel,
        grid_spec=pltpu.PrefetchScalarGridSpec(
            num_scalar_prefetch=1,
            grid=(t // nt,),
            in_specs=[pl.BlockSpec((nt, D_MODEL), lambda i, d: (i, 0)),
                      pl.BlockSpec((None, 1, D_MODEL), lambda i, d: (i // tpb, 0, 0)),
                      pl.BlockSpec((nt, TOP_K), lambda i, d: (i, 0)),
                      pl.BlockSpec(memory_space=pl.ANY)],
            out_specs=pl.BlockSpec((nt, D_MODEL), lambda i, d: (i, 0)),
            scratch_shapes=[pltpu.VMEM((TOP_K, nt, HALF), jnp.uint32),
                            pltpu.SemaphoreType.DMA],
        ),
        out_shape=jax.ShapeDtypeStruct((t, D_MODEL), F32),
        compiler_params=_params(1),
        name="combine",
    )(dest_flat, base, g2, w_tok, ys)


def _layer(layer, x, c, w_ada, b_ada, norm1_w, norm2_w, w_in, q_norm_w, k_norm_w, rel_bias, w_alpha, b_alpha,
           moba_out_w, gla_out_w, w_out, w_router, e_bias, w1, w3, w2, ws1, ws3, ws2):
    b, s, d = x.shape
    t = b * s
    x2 = x.reshape(t, d)

    mod = _mod(c, w_ada, b_ada)
    sh1, sc1, g1, sh2, sc2, g2 = [mod[:, j * d:(j + 1) * d].reshape(b, 1, d) for j in range(6)]

    w_main = w_in[:, :D_MAIN].astype(BF16)
    w_ga = jnp.zeros((d, LANES), BF16).at[:, :GLA_GATE_RANK].set(w_in[:, D_MAIN:].astype(BF16))
    per_chunk = 256 // MOBA_HEAD_DIM
    qw = jnp.tile(q_norm_w.astype(F32), per_chunk).reshape(1, 256) * (MOBA_HEAD_DIM ** -0.5)
    kw = jnp.tile(k_norm_w.astype(F32), per_chunk).reshape(1, 256)
    proj, ga = _inproj(x2, sc1, sh1, norm1_w.reshape(1, d), w_main, w_ga, qw, kw, s)
    proj3 = proj.reshape(b, s, D_MAIN)

    near, far = _moba_bias_tables(rel_bias)
    ow = jnp.tile(moba_out_w.astype(F32), 2).reshape(1, LANES)
    o_a = _moba(proj3, near, far, ow)

    wal = jnp.zeros((LANES, GLA_KEY_WIDTH), F32).at[:GLA_GATE_RANK].set(w_alpha)
    o_b = _gla(proj3, ga.reshape(b, s, LANES), wal, b_alpha.reshape(1, GLA_KEY_WIDTH),
               gla_out_w.reshape(1, GLA_DV))

    base, h2, scores_t = _outproj(
        o_a.reshape(t, MOBA_WIDTH), o_b.reshape(t, GLA_WIDTH), x2, g1, sc2, sh2, g2,
        norm2_w.reshape(1, d), w_out.astype(BF16), ws1.astype(BF16), ws3.astype(BF16), ws2.astype(BF16),
        w_router.T.astype(BF16), s)

    eb = jnp.broadcast_to(e_bias.astype(F32)[:, None], (N_EXPERTS, ROUTE_TILE))
    code_t, w_t, counts = _route(scores_t, eb)

    r = EXPERT_TILE
    n_tiles = (t * TOP_K + N_EXPERTS * (r - 1) + r - 1) // r
    n_rows = n_tiles * r
    cnt = counts[:, 0].astype(jnp.int32)
    padded = (cnt + r - 1) // r * r
    pend = jnp.cumsum(padded)
    pstart = pend - padded
    tile_row = jnp.arange(n_tiles, dtype=jnp.int32) * r
    tile_expert = jnp.minimum(jnp.sum((pend[None, :] <= tile_row[:, None]).astype(jnp.int32), axis=1),
                              N_EXPERTS - 1)
    n_used = (pend[-1:] // r).astype(jnp.int32)
    dest_t = _slots(pstart, code_t)

    xs = _sc_scatter_rows(h2, dest_t, n_rows)
    ys = _experts(tile_expert, n_used, xs, w1, w3, w2, layer)
    yg = _sc_gather_rows(ys, dest_t.reshape(TOP_K * t)).reshape(TOP_K, t, HALF)
    out = _combine_dense(base, g2, w_t.T, yg, s)
    return out.reshape(b, s, d)


def kernel(x, c, w_ada, b_ada, norm1_w, norm2_w, w_in, q_norm_w, k_norm_w, rel_bias, w_alpha, b_alpha,
           moba_out_w, gla_out_w, w_out, w_router, e_bias, w1, w3, w2, ws1, ws3, ws2):
    for l in range(w_ada.shape[0]):
        x = _layer(l, x, c, w_ada[l], b_ada[l], norm1_w[l], norm2_w[l], w_in[l], q_norm_w[l], k_norm_w[l],
                   rel_bias, w_alpha[l], b_alpha[l], moba_out_w[l], gla_out_w[l], w_out[l], w_router[l],
                   e_bias[l], w1, w3, w2, ws1[l], ws3[l], ws2[l])
    return x
```

```python
import functools
import math

import numpy as np
import jax
import jax.numpy as jnp
from jax import lax
from jax.experimental import pallas as pl
from jax.experimental.pallas import tpu as pltpu

D_MODEL = 1024
MOBA_HEADS = 8
MOBA_HEAD_DIM = 64
MOBA_WIDTH = MOBA_HEADS * MOBA_HEAD_DIM
MOBA_BLOCK = 256
MOBA_TOPK = 3
GLA_HEADS = 4
GLA_DK = 64
GLA_DV = 128
GLA_KEY_WIDTH = GLA_HEADS * GLA_DK
GLA_WIDTH = GLA_HEADS * GLA_DV
GLA_GATE_RANK = 16
GLA_GATE_TAU = 16.0
GLA_CHUNK = 64
REL_BUCKETS = 32
REL_MAX_DIST = 128
N_EXPERTS = 256
TOP_K = 8
N_GROUPS = 8
TOPK_GROUPS = 4
GROUP_SIZE = N_EXPERTS // N_GROUPS
EXPERT_FF = 256
SHARED_FF = 256
ROUTED_SCALE = 2.5
NORM_EPS = 1e-6

D_MAIN = 3 * MOBA_WIDTH + 2 * GLA_KEY_WIDTH + 2 * GLA_WIDTH
LANES = 128
VMEM_LIMIT = 56 * 1024 * 1024

ROW_TILE = 512
ROUTE_TILE = 256
EXPERT_TILE = 256

F32 = jnp.float32
BF16 = jnp.bfloat16
NT_DIMS = (((1,), (1,)), ((), ()))
TN_DIMS = (((0,), (0,)), ((), ()))


def _params(n_axes):
    return pltpu.CompilerParams(dimension_semantics=("arbitrary",) * n_axes,
                                vmem_limit_bytes=VMEM_LIMIT)


def _silu(v):
    return v * jax.nn.sigmoid(v)


def _mod_kernel(c_ref, w_ref, b_ref, o_ref):
    o_ref[...] = jnp.dot(_silu(c_ref[...]), w_ref[...], preferred_element_type=F32) + b_ref[...]


def _mod(c, w, b):
    rows = 8
    cp = jnp.zeros((rows, D_MODEL), F32).at[:c.shape[0]].set(c)
    n = w.shape[1]
    tn = 1024
    out = pl.pallas_call(
        _mod_kernel,
        grid=(n // tn,),
        in_specs=[pl.BlockSpec((rows, D_MODEL), lambda j: (0, 0)),
                  pl.BlockSpec((D_MODEL, tn), lambda j: (0, j)),
                  pl.BlockSpec((1, tn), lambda j: (0, j))],
        out_specs=pl.BlockSpec((rows, tn), lambda j: (0, j)),
        out_shape=jax.ShapeDtypeStruct((rows, n), F32),
        compiler_params=_params(1),
        name="mod",
    )(cp, w, b.reshape(1, n))
    return out[:c.shape[0]]


def _group_rms_inv(a, group):
    lane = lax.broadcasted_iota(jnp.int32, (1, a.shape[1]), 1)
    a2 = a * a
    inv = jnp.zeros_like(a)
    for g in range(a.shape[1] // group):
        m = (lane >= g * group) & (lane < (g + 1) * group)
        ss = jnp.sum(jnp.where(m, a2, 0.0), axis=-1, keepdims=True)
        inv = jnp.where(m, lax.rsqrt(ss * (1.0 / group) + NORM_EPS), inv)
    return inv


def _inproj_kernel(x_ref, sc_ref, sh_ref, nw_ref, w_ref, wga_ref, qw_ref, kw_ref, o_ref, ga_ref):
    x = x_ref[...]
    ms = jnp.mean(x * x, axis=-1, keepdims=True)
    h = x * lax.rsqrt(ms + NORM_EPS) * nw_ref[...]
    h = h * (1.0 + sc_ref[...]) + sh_ref[...]
    hb = h.astype(BF16)
    cw = 256
    for j in range(D_MAIN // cw):
        acc = jnp.dot(hb, w_ref[:, j * cw:(j + 1) * cw], preferred_element_type=F32)
        if j < 2 * MOBA_WIDTH // cw:
            nw = qw_ref if j < MOBA_WIDTH // cw else kw_ref
            acc = acc * _group_rms_inv(acc, MOBA_HEAD_DIM) * nw[...]
        o_ref[:, j * cw:(j + 1) * cw] = acc.astype(BF16)
    ga_ref[...] = jnp.dot(hb, wga_ref[...], preferred_element_type=F32)


def _inproj(x2, sc, sh, nw, w_main, w_ga, qw, kw, seq):
    t = x2.shape[0]
    tpb = seq // ROW_TILE
    vec = lambda: pl.BlockSpec((None, 1, D_MODEL), lambda i: (i // tpb, 0, 0))
    full = lambda a: pl.BlockSpec(a.shape, lambda i: (0,) * a.ndim)
    return pl.pallas_call(
        _inproj_kernel,
        grid=(t // ROW_TILE,),
        in_specs=[pl.BlockSpec((ROW_TILE, D_MODEL), lambda i: (i, 0)), vec(), vec(),
                  full(nw), full(w_main), full(w_ga), full(qw), full(kw)],
        out_specs=[pl.BlockSpec((ROW_TILE, D_MAIN), lambda i: (i, 0)),
                   pl.BlockSpec((ROW_TILE, LANES), lambda i: (i, 0))],
        out_shape=[jax.ShapeDtypeStruct((t, D_MAIN), BF16),
                   jax.ShapeDtypeStruct((t, LANES), F32)],
        compiler_params=_params(1),
        name="inproj",
    )(x2, sc, sh, nw, w_main, w_ga, qw, kw)


def _t5_bucket_np(rel):
    max_exact = REL_BUCKETS // 2
    relf = np.maximum(rel, 1).astype(np.float64)
    large = max_exact + (np.log(relf / max_exact) / math.log(REL_MAX_DIST / max_exact)
                         * (REL_BUCKETS - max_exact)).astype(np.int32)
    large = np.minimum(large, REL_BUCKETS - 1)
    return np.where(rel < max_exact, rel, large)


def _bias_kernel(rb_ref, idx_ref, o_ref):
    h = pl.program_id(0)
    idx = idx_ref[...]
    tab = jnp.full(idx.shape, -jnp.inf, F32)
    for bk in range(REL_BUCKETS):
        tab = jnp.where(idx == bk, rb_ref[bk * MOBA_HEADS + h], tab)
    o_ref[...] = tab


def _moba_bias_tables(rel_bias):
    j = np.arange(MOBA_BLOCK)[:, None]
    i = np.arange(MOBA_BLOCK)[None, :]
    own_idx = np.where(j <= i, _t5_bucket_np(np.maximum(i - j, 0)), -1)
    prev_idx = _t5_bucket_np(MOBA_BLOCK + i - j)
    idx = jnp.asarray(np.concatenate([prev_idx, own_idx], axis=0).astype(np.int32))
    assert int(_t5_bucket_np(np.array([MOBA_BLOCK + 1]))[0]) == REL_BUCKETS - 1
    rb = rel_bias.astype(F32)
    near = pl.pallas_call(
        _bias_kernel,
        grid=(MOBA_HEADS,),
        in_specs=[pl.BlockSpec(memory_space=pltpu.SMEM),
                  pl.BlockSpec(idx.shape, lambda h: (0, 0))],
        out_specs=pl.BlockSpec((None,) + idx.shape, lambda h: (h, 0, 0)),
        out_shape=jax.ShapeDtypeStruct((MOBA_HEADS,) + idx.shape, F32),
        compiler_params=_params(1),
        name="bias",
    )(rb.reshape(-1), idx)
    return near, rb[REL_BUCKETS - 1]


FAR_GROUP = 4


def _moba_kernel(far_ref, q_ref, k_ref, v_ref, near_ref, ow_ref, o_ref,
                 kmean_ref, vt_ref, vtg_ref, acc_ref, m_ref, sel_ref):
    hp = pl.program_id(1)
    i = pl.program_id(2)
    nblk = k_ref.shape[0] // MOBA_BLOCK
    hd = MOBA_HEAD_DIM
    bs = MOBA_BLOCK

    @pl.when(i == 0)
    def _prepare():
        row = lax.broadcasted_iota(jnp.int32, (LANES, MOBA_BLOCK), 0)
        for n in range(nblk):
            kb = k_ref[n * MOBA_BLOCK:(n + 1) * MOBA_BLOCK, :].astype(F32)
            kmean_ref[n:n + 1, :] = jnp.mean(kb, axis=0, keepdims=True)
            vt = v_ref[n * MOBA_BLOCK:(n + 1) * MOBA_BLOCK, :].astype(F32).T
            vt0 = jnp.where(row < hd, vt, 1.0).astype(BF16)
            vt1 = jnp.where(row < hd, 1.0, vt).astype(BF16)
            vt_ref[0, n] = vt0
            vt_ref[1, n] = vt1
            gcols = slice((n % FAR_GROUP) * bs, (n % FAR_GROUP + 1) * bs)
            vtg_ref[0, n // FAR_GROUP, :, gcols] = vt0
            vtg_ref[1, n // FAR_GROUP, :, gcols] = vt1

    q = q_ref[...]
    lane = lax.broadcasted_iota(jnp.int32, q.shape, 1)
    zero = jnp.zeros_like(q)
    qh = (jnp.where(lane < hd, q, zero), jnp.where(lane < hd, zero, q))

    blk = lax.broadcasted_iota(jnp.int32, (nblk, MOBA_BLOCK), 0)
    for h in range(2):
        gt = lax.dot_general(kmean_ref[...], qh[h].astype(F32), NT_DIMS, preferred_element_type=F32)
        gt = jnp.where(blk < i, gt, -jnp.inf)
        cnt = jnp.zeros(gt.shape, jnp.int32)
        for m in range(nblk):
            gm = gt[m:m + 1, :]
            cnt = cnt + jnp.where((gm > gt) | ((gm == gt) & (blk > m)), 1, 0)
        keep = (blk < i) & (cnt < MOBA_TOPK)
        sel_ref[0, h] = jnp.where(keep, 1.0, 0.0)
        sel_ref[1, h] = jnp.where(keep & (blk < i - 1), 1.0, 0.0)

    @pl.when(i == 0)
    def _own_block_only():
        kb = k_ref[0:bs, :]
        for h in range(2):
            s = lax.dot_general(kb, qh[h], NT_DIMS, preferred_element_type=F32) + near_ref[h, bs:2 * bs, :]
            m_new = jnp.max(s, axis=0, keepdims=True)
            p = jnp.exp(s - m_new).astype(BF16)
            acc_ref[h] = jnp.dot(vt_ref[h, 0], p, preferred_element_type=F32)
            m_ref[h] = m_new

    @pl.when(i >= 1)
    def _previous_and_own_block():
        k_prev = k_ref[pl.ds(pl.multiple_of((i - 1) * bs, bs), bs), :]
        k_own = k_ref[pl.ds(pl.multiple_of(i * bs, bs), bs), :]
        for h in range(2):
            s_own = lax.dot_general(k_own, qh[h], NT_DIMS, preferred_element_type=F32) + near_ref[h, bs:2 * bs, :]
            m_own = jnp.max(s_own, axis=0, keepdims=True)
            pv_own = jnp.dot(vt_ref[h, i], jnp.exp(s_own - m_own).astype(BF16), preferred_element_type=F32)
            s_prev = lax.dot_general(k_prev, qh[h], NT_DIMS, preferred_element_type=F32) + near_ref[h, 0:bs, :]
            keep = sel_ref[0, h, pl.ds(i - 1, 1), :] > 0.5
            mx = jnp.max(s_prev, axis=0, keepdims=True)
            p_prev = jnp.exp(s_prev - jnp.where(keep, mx, jnp.inf)).astype(BF16)
            pv_prev = jnp.dot(vt_ref[h, i - 1], p_prev, preferred_element_type=F32)
            m_new = jnp.maximum(m_own, jnp.where(keep, mx, -jnp.inf))
            acc_ref[h] = (pv_own * jnp.exp(m_own - m_new)
                          + pv_prev * jnp.exp(jnp.where(keep, mx, -jnp.inf) - m_new))
            m_ref[h] = m_new

    def far_body(g, carry):
        gk = FAR_GROUP * bs
        kb = k_ref[pl.ds(pl.multiple_of(g * gk, gk), gk), :]
        ss = [lax.dot_general(kb, qh[h], NT_DIMS, preferred_element_type=F32) for h in range(2)]
        for h in range(2):
            s = ss[h]
            fb = far_ref[2 * hp + h]
            m_old = m_ref[h]
            m_new = m_old
            keeps = []
            for j in range(FAR_GROUP):
                keep = sel_ref[1, h, pl.ds(g * FAR_GROUP + j, 1), :] > 0.5
                mx = jnp.max(s[j * bs:(j + 1) * bs], axis=0, keepdims=True) + fb
                m_new = jnp.maximum(m_new, jnp.where(keep, mx, -jnp.inf))
                keeps.append(keep)
            p = jnp.concatenate(
                [jnp.exp(s[j * bs:(j + 1) * bs] - jnp.where(keeps[j], m_new - fb, jnp.inf)).astype(BF16)
                 for j in range(FAR_GROUP)], axis=0)
            pv = jnp.dot(vtg_ref[h, g], p, preferred_element_type=F32)
            acc_ref[h] = acc_ref[h] * jnp.exp(m_old - m_new) + pv
            m_ref[h] = m_new
        return carry

    lax.fori_loop(0, (i + FAR_GROUP - 2) // FAR_GROUP, far_body, 0)

    a0 = acc_ref[0]
    a1 = acc_ref[1]
    row = lax.broadcasted_iota(jnp.int32, a0.shape, 0)
    ot = jnp.where(row < hd, a0 / a0[hd:hd + 1, :], a1 / a1[0:1, :])
    o2 = ot * ot
    ss0 = jnp.sum(jnp.where(row < hd, o2, 0.0), axis=0, keepdims=True)
    ss1 = jnp.sum(jnp.where(row < hd, 0.0, o2), axis=0, keepdims=True)
    inv = jnp.where(row < hd, lax.rsqrt(ss0 * (1.0 / hd) + NORM_EPS), lax.rsqrt(ss1 * (1.0 / hd) + NORM_EPS))
    o_ref[...] = ((ot * inv).T * ow_ref[...]).astype(o_ref.dtype)


def _moba(proj3, near, far, ow):
    b, s, _ = proj3.shape
    nblk = s // MOBA_BLOCK
    assert nblk % FAR_GROUP == 0
    npair = MOBA_HEADS // 2
    kcol = MOBA_WIDTH // LANES
    return pl.pallas_call(
        _moba_kernel,
        grid=(b, npair, nblk),
        in_specs=[pl.BlockSpec(memory_space=pltpu.SMEM),
                  pl.BlockSpec((None, MOBA_BLOCK, LANES), lambda bb, hp, i: (bb, i, hp)),
                  pl.BlockSpec((None, s, LANES), lambda bb, hp, i: (bb, 0, kcol + hp)),
                  pl.BlockSpec((None, s, LANES), lambda bb, hp, i: (bb, 0, 2 * kcol + hp)),
                  pl.BlockSpec((2, 2 * MOBA_BLOCK, MOBA_BLOCK), lambda bb, hp, i: (hp, 0, 0)),
                  pl.BlockSpec((1, LANES), lambda bb, hp, i: (0, 0))],
        out_specs=pl.BlockSpec((None, MOBA_BLOCK, LANES), lambda bb, hp, i: (bb, i, hp)),
        out_shape=jax.ShapeDtypeStruct((b, s, MOBA_WIDTH), BF16),
        scratch_shapes=[pltpu.VMEM((nblk, LANES), F32),
                        pltpu.VMEM((2, nblk, LANES, MOBA_BLOCK), BF16),
                        pltpu.VMEM((2, nblk // FAR_GROUP, LANES, FAR_GROUP * MOBA_BLOCK), BF16),
                        pltpu.VMEM((2, LANES, MOBA_BLOCK), F32),
                        pltpu.VMEM((2, 1, MOBA_BLOCK), F32),
                        pltpu.VMEM((2, 2, nblk, MOBA_BLOCK), F32)],
        compiler_params=_params(3),
        name="moba",
    )(far, proj3, proj3, proj3, near, ow)


def _split3(v):
    hi = v.astype(BF16)
    r1 = v - hi.astype(F32)
    mid = r1.astype(BF16)
    lo = (r1 - mid.astype(F32)).astype(BF16)
    return hi, mid, lo


def _gla_kernel(q_ref, k_ref, v_ref, g_ref, ga_ref, wal_ref, bal_ref, gw_ref, o_ref, b_ref, st_ref):
    seq = q_ref.shape[0]
    c = GLA_CHUNK
    pc = 256

    rr = lax.broadcasted_iota(jnp.int32, (pc, pc), 0)
    cc = lax.broadcasted_iota(jnp.int32, (pc, pc), 1)
    tri = jnp.where((rr >= cc) & (rr // c == cc // c), 1.0, 0.0).astype(BF16)

    def decay_body(j, carry):
        r0 = pl.multiple_of(j * pc, pc)
        xg = jnp.dot(ga_ref[pl.ds(r0, pc), :], wal_ref[...], preferred_element_type=F32) + bal_ref[...]
        la = (jnp.minimum(xg, 0.0) - jnp.log(1.0 + jnp.exp(-jnp.abs(xg)))) * (1.0 / GLA_GATE_TAU)
        hi, mid, lo = _split3(la)
        b_ref[pl.ds(r0, pc), :] = (jnp.dot(tri, hi, preferred_element_type=F32)
                                   + jnp.dot(tri, mid, preferred_element_type=F32)
                                   + jnp.dot(tri, lo, preferred_element_type=F32))
        return carry

    lax.fori_loop(0, seq // pc, decay_body, 0)

    st_ref[...] = jnp.zeros_like(st_ref)
    lane = lax.broadcasted_iota(jnp.int32, (c, LANES), 1)
    head_mask = (lane < GLA_DK, lane >= GLA_DK)
    causal = lax.broadcasted_iota(jnp.int32, (c, c), 0) >= lax.broadcasted_iota(jnp.int32, (c, c), 1)

    def chunk_body(ci, carry):
        r0 = pl.multiple_of(ci * c, c)
        b = b_ref[pl.ds(r0, c), :]
        ref_row = b[c // 2 - 1:c // 2, :]
        last = b[c - 1:c, :]
        q = q_ref[pl.ds(r0, c), :].astype(F32) * (GLA_DK ** -0.5)
        k = k_ref[pl.ds(r0, c), :].astype(F32)
        qt = q * jnp.exp(b - ref_row)
        kt = (k * jnp.exp(ref_row - b)).astype(BF16)
        qs = q * jnp.exp(b)
        ke = (k * jnp.exp(last - b)).astype(BF16)
        e_last = jnp.exp(last)
        for h in range(2):
            cols = slice(h * GLA_DV, (h + 1) * GLA_DV)
            a = lax.dot_general(jnp.where(head_mask[h], qt, 0.0).astype(BF16), kt, NT_DIMS,
                                preferred_element_type=F32)
            a = jnp.where(causal, a, 0.0).astype(BF16)
            v = v_ref[pl.ds(r0, c), cols]
            st = st_ref[h]
            o = jnp.dot(a, v, preferred_element_type=F32)
            o = o + lax.dot_general(jnp.where(head_mask[h], qs, 0.0).astype(BF16), st.astype(BF16), NT_DIMS,
                                    preferred_element_type=F32)
            st_ref[h] = st * e_last + lax.dot_general(v, ke, TN_DIMS, preferred_element_type=F32)
            ms = jnp.mean(o * o, axis=-1, keepdims=True)
            on = o * lax.rsqrt(ms + NORM_EPS) * gw_ref[...]
            g = g_ref[pl.ds(r0, c), cols].astype(F32)
            o_ref[pl.ds(r0, c), cols] = (on * _silu(g)).astype(o_ref.dtype)
        return carry

    lax.fori_loop(0, seq // c, chunk_body, 0)


def _gla(proj3, ga3, wal, bal, gw):
    b, s, _ = proj3.shape
    npair = GLA_HEADS // 2
    qcol = 3 * MOBA_WIDTH // LANES
    kcol = qcol + GLA_KEY_WIDTH // LANES
    vcol = (3 * MOBA_WIDTH + 2 * GLA_KEY_WIDTH) // (2 * GLA_DV)
    gcol = vcol + npair
    return pl.pallas_call(
        _gla_kernel,
        grid=(b, npair),
        in_specs=[pl.BlockSpec((None, s, LANES), lambda bb, hp: (bb, 0, qcol + hp)),
                  pl.BlockSpec((None, s, LANES), lambda bb, hp: (bb, 0, kcol + hp)),
                  pl.BlockSpec((None, s, 2 * GLA_DV), lambda bb, hp: (bb, 0, vcol + hp)),
                  pl.BlockSpec((None, s, 2 * GLA_DV), lambda bb, hp: (bb, 0, gcol + hp)),
                  pl.BlockSpec((None, s, LANES), lambda bb, hp: (bb, 0, 0)),
                  pl.BlockSpec((LANES, LANES), lambda bb, hp: (0, hp)),
                  pl.BlockSpec((1, LANES), lambda bb, hp: (0, hp)),
                  pl.BlockSpec((1, GLA_DV), lambda bb, hp: (0, 0))],
        out_specs=pl.BlockSpec((None, s, 2 * GLA_DV), lambda bb, hp: (bb, 0, hp)),
        out_shape=jax.ShapeDtypeStruct((b, s, GLA_WIDTH), BF16),
        scratch_shapes=[pltpu.VMEM((s, LANES), F32),
                        pltpu.VMEM((2, GLA_DV, LANES), F32)],
        compiler_params=_params(2),
        name="gla",
    )(proj3, proj3, proj3, proj3, ga3, wal, bal, gw)


HALF = D_MODEL // 2


def _pack_rows(v):
    def bf16_bits(a):
        u = lax.bitcast_convert_type(a, jnp.uint32)
        return (u + (jnp.uint32(0x7FFF) + ((u >> 16) & jnp.uint32(1)))) >> 16
    return bf16_bits(v[:, :HALF]) | (bf16_bits(v[:, HALF:]) << 16)


def _unpack_rows(w):
    return (lax.bitcast_convert_type(w << 16, F32),
            lax.bitcast_convert_type(w & jnp.uint32(0xFFFF0000), F32))


def _outproj_kernel(oa_ref, ob_ref, x_ref, g1_ref, sc_ref, sh_ref, g2_ref, nw_ref, wo_ref,
                    ws1_ref, ws3_ref, ws2_ref, wrt_ref, base_ref, h_ref, st_ref):
    mix = (jnp.dot(oa_ref[...], wo_ref[:MOBA_WIDTH, :], preferred_element_type=F32)
           + jnp.dot(ob_ref[...], wo_ref[MOBA_WIDTH:, :], preferred_element_type=F32))
    x1 = x_ref[...] + g1_ref[...] * mix
    ms = jnp.mean(x1 * x1, axis=-1, keepdims=True)
    h = x1 * lax.rsqrt(ms + NORM_EPS) * nw_ref[...]
    h = h * (1.0 + sc_ref[...]) + sh_ref[...]
    h_ref[...] = _pack_rows(h)
    hb = h.astype(BF16)
    a = jnp.dot(hb, ws1_ref[...], preferred_element_type=F32)
    u = jnp.dot(hb, ws3_ref[...], preferred_element_type=F32)
    shared = jnp.dot((_silu(a) * u).astype(BF16), ws2_ref[...], preferred_element_type=F32)
    base_ref[...] = x1 + g2_ref[...] * shared
    logits_t = lax.dot_general(wrt_ref[...], hb, NT_DIMS, preferred_element_type=F32)
    st_ref[...] = jax.nn.sigmoid(logits_t)


def _outproj(oa, ob, x2, g1, sc, sh, g2, nw, wo, ws1, ws3, ws2, wrt, seq):
    t = x2.shape[0]
    tpb = seq // ROW_TILE
    vec = lambda: pl.BlockSpec((None, 1, D_MODEL), lambda i: (i // tpb, 0, 0))
    full = lambda a: pl.BlockSpec(a.shape, lambda i: (0,) * a.ndim)
    rows = lambda w: pl.BlockSpec((ROW_TILE, w), lambda i: (i, 0))
    return pl.pallas_call(
        _outproj_kernel,
        grid=(t // ROW_TILE,),
        in_specs=[rows(MOBA_WIDTH), rows(GLA_WIDTH), rows(D_MODEL), vec(), vec(), vec(), vec(),
                  full(nw), full(wo), full(ws1), full(ws3), full(ws2), full(wrt)],
        out_specs=[rows(D_MODEL), rows(HALF), pl.BlockSpec((N_EXPERTS, ROW_TILE), lambda i: (0, i))],
        out_shape=[jax.ShapeDtypeStruct((t, D_MODEL), F32),
                   jax.ShapeDtypeStruct((t, HALF), jnp.uint32),
                   jax.ShapeDtypeStruct((N_EXPERTS, t), F32)],
        compiler_params=_params(1),
        name="outproj",
    )(oa, ob, x2, g1, sc, sh, g2, nw, wo, ws1, ws3, ws2, wrt)


SLOT_CODE_SHIFT = 16
SLOT_CODE_BASE = 1 << SLOT_CODE_SHIFT


def _route_kernel(s_ref, eb_ref, code_ref, w_ref, cnt_ref, carry_ref):
    i = pl.program_id(0)
    ne, nt = s_ref.shape

    @pl.when(i == 0)
    def _init():
        carry_ref[...] = jnp.zeros_like(carry_ref)

    s = s_ref[...]
    choice = s + eb_ref[...]
    gio = lax.broadcasted_iota(jnp.int32, (GROUP_SIZE, nt), 0)
    gscore = []
    for g in range(N_GROUPS):
        cg = choice[g * GROUP_SIZE:(g + 1) * GROUP_SIZE, :]
        top1 = jnp.max(cg, axis=0, keepdims=True)
        first = jnp.min(jnp.where(cg == top1, gio, GROUP_SIZE), axis=0, keepdims=True)
        top2 = jnp.max(jnp.where(gio == first, -jnp.inf, cg), axis=0, keepdims=True)
        gscore.append(top1 + top2)
    gs = jnp.concatenate(gscore, axis=0)
    gidx = lax.broadcasted_iota(jnp.int32, gs.shape, 0)
    beaten = jnp.zeros(gs.shape, jnp.int32)
    for m in range(N_GROUPS):
        gm = gs[m:m + 1, :]
        beaten = beaten + jnp.where((gm > gs) | ((gm == gs) & (gidx > m)), 1, 0)
    gkeep = beaten < TOPK_GROUPS
    masked = jnp.concatenate(
        [jnp.where(gkeep[g:g + 1, :], choice[g * GROUP_SIZE:(g + 1) * GROUP_SIZE, :], -jnp.inf)
         for g in range(N_GROUPS)], axis=0)

    eio = lax.broadcasted_iota(jnp.int32, (ne, nt), 0)
    picked = jnp.zeros((ne, nt), F32)
    idx_rows, w_rows, hits = [], [], []
    for _ in range(TOP_K):
        mx = jnp.max(masked, axis=0, keepdims=True)
        idx = jnp.min(jnp.where(masked == mx, eio, ne), axis=0, keepdims=True)
        hit = eio == idx
        w_rows.append(jnp.sum(jnp.where(hit, s, 0.0), axis=0, keepdims=True))
        idx_rows.append(idx)
        hits.append(hit)
        masked = jnp.where(hit, -jnp.inf, masked)
        picked = jnp.where(hit, 1.0, picked)
    wk = jnp.concatenate(w_rows, axis=0)
    w_ref[...] = wk / jnp.sum(wk, axis=0, keepdims=True) * ROUTED_SCALE

    tr = lax.broadcasted_iota(jnp.int32, (nt, nt), 0)
    tc = lax.broadcasted_iota(jnp.int32, (nt, nt), 1)
    before = jnp.where(tr < tc, 1.0, 0.0).astype(BF16)
    pb = picked.astype(BF16)
    pos = carry_ref[...] + jnp.dot(pb, before, preferred_element_type=F32)
    rank = jnp.concatenate(
        [jnp.sum(jnp.where(hit, pos, 0.0), axis=0, keepdims=True) for hit in hits], axis=0).astype(jnp.int32)
    code_ref[...] = jnp.concatenate(idx_rows, axis=0) * SLOT_CODE_BASE + rank
    total = carry_ref[...] + jnp.dot(pb, jnp.ones((nt, nt), BF16), preferred_element_type=F32)
    carry_ref[...] = total
    cnt_ref[...] = total


def _route(scores_t, eb):
    ne, t = scores_t.shape
    assert t <= SLOT_CODE_BASE
    nt = ROUTE_TILE
    tok = lambda dt: jax.ShapeDtypeStruct((TOP_K, t), dt)
    return pl.pallas_call(
        _route_kernel,
        grid=(t // nt,),
        in_specs=[pl.BlockSpec((ne, nt), lambda i: (0, i)),
                  pl.BlockSpec((ne, nt), lambda i: (0, 0))],
        out_specs=[pl.BlockSpec((TOP_K, nt), lambda i: (0, i)),
                   pl.BlockSpec((TOP_K, nt), lambda i: (0, i)),
                   pl.BlockSpec((ne, nt), lambda i: (0, 0))],
        out_shape=[tok(jnp.int32), tok(F32), jax.ShapeDtypeStruct((ne, nt), F32)],
        scratch_shapes=[pltpu.VMEM((ne, nt), F32)],
        compiler_params=_params(1),
        name="route",
    )(scores_t, eb)


SLOT_TILE = 2048


def _slots_kernel(pstart_ref, code_ref, o_ref):
    code = code_ref[...]
    expert = lax.shift_right_logical(code, SLOT_CODE_SHIFT)

    def body(e, acc):
        return jnp.where(expert == e, pstart_ref[e], acc)

    start = lax.fori_loop(0, N_EXPERTS, body, jnp.zeros_like(code), unroll=8)
    o_ref[...] = start + (code & (SLOT_CODE_BASE - 1))


def _slots(pstart, code_t):
    k, t = code_t.shape
    return pl.pallas_call(
        _slots_kernel,
        grid_spec=pltpu.PrefetchScalarGridSpec(
            num_scalar_prefetch=1,
            grid=(t // SLOT_TILE,),
            in_specs=[pl.BlockSpec((k, SLOT_TILE), lambda i, p: (0, i))],
            out_specs=pl.BlockSpec((k, SLOT_TILE), lambda i, p: (0, i)),
        ),
        out_shape=jax.ShapeDtypeStruct((k, t), jnp.int32),
        compiler_params=_params(1),
        name="slots",
    )(pstart, code_t)


SC_WINDOW = 128


def _sc_gather_rows(table, idx_flat):
    from jax.experimental.pallas import tpu_sc as plsc
    info = plsc.get_sparse_core_info()
    nw = info.num_cores * info.num_subcores
    n = idx_flat.shape[0]
    width = table.shape[1]
    per_worker = n // nw
    assert per_worker * nw == n and per_worker % SC_WINDOW == 0
    mesh = plsc.VectorSubcoreMesh(core_axis_name="c", subcore_axis_name="s")

    def body(table_hbm, idx_hbm, out_hbm, idx_v, rows_v, sem):
        wid = lax.axis_index("s") * info.num_cores + lax.axis_index("c")
        base = wid * per_worker

        @pl.loop(0, per_worker // SC_WINDOW)
        def _(w):
            off = pl.multiple_of(base + w * SC_WINDOW, SC_WINDOW)
            pltpu.sync_copy(idx_hbm.at[pl.ds(off, SC_WINDOW)], idx_v)
            pltpu.async_copy(table_hbm.at[idx_v], rows_v, sem).wait()
            pltpu.sync_copy(rows_v, out_hbm.at[pl.ds(off, SC_WINDOW)])

    return pl.kernel(
        body,
        out_type=jax.ShapeDtypeStruct((n, width), table.dtype),
        mesh=mesh,
        scratch_types=[pltpu.VMEM((SC_WINDOW,), jnp.int32),
                       pltpu.VMEM((SC_WINDOW, width), table.dtype),
                       pltpu.SemaphoreType.DMA],
        name="sc_gather",
    )(table, idx_flat)


def _sc_scatter_rows(rows, idx_kt, n_out):
    from jax.experimental.pallas import tpu_sc as plsc
    info = plsc.get_sparse_core_info()
    nw = info.num_cores * info.num_subcores
    t, width = rows.shape
    nk = idx_kt.shape[0]
    per_worker = t // nw
    assert per_worker * nw == t and per_worker % SC_WINDOW == 0
    mesh = plsc.VectorSubcoreMesh(core_axis_name="c", subcore_axis_name="s")

    def body(rows_hbm, idx_hbm, out_hbm, idx_v, rows_v, sem):
        wid = lax.axis_index("s") * info.num_cores + lax.axis_index("c")
        base = wid * per_worker

        @pl.loop(0, per_worker // SC_WINDOW)
        def _(w):
            off = pl.multiple_of(base + w * SC_WINDOW, SC_WINDOW)
            pltpu.sync_copy(rows_hbm.at[pl.ds(off, SC_WINDOW)], rows_v)
            pltpu.sync_copy(idx_hbm.at[:, pl.ds(off, SC_WINDOW)], idx_v)
            copies = [pltpu.async_copy(rows_v, out_hbm.at[idx_v.at[k]], sem) for k in range(nk)]
            for cp in copies:
                cp.wait()

    return pl.kernel(
        body,
        out_type=jax.ShapeDtypeStruct((n_out, width), rows.dtype),
        mesh=mesh,
        scratch_types=[pltpu.VMEM((nk, SC_WINDOW), jnp.int32),
                       pltpu.VMEM((SC_WINDOW, width), rows.dtype),
                       pltpu.SemaphoreType.DMA],
        name="sc_scatter",
    )(rows, idx_kt)


def _expert_kernel(first_ref, count_ref, used_ref, w1_ref, w3_ref, w2_ref, xs_ref, ys_ref,
                   xbuf, ybuf, sem_in, sem_out, w1b, w3b, w2b):
    e = pl.program_id(0)
    r = xbuf.shape[1]
    n_used = used_ref[0]

    def x_copy(g, slot):
        return pltpu.make_async_copy(xs_ref.at[pl.ds(pl.multiple_of(g * r, r), r), :], xbuf.at[slot], sem_in.at[slot])

    def y_copy(g, slot):
        return pltpu.make_async_copy(ybuf.at[slot], ys_ref.at[pl.ds(pl.multiple_of(g * r, r), r), :], sem_out.at[slot])

    @pl.when(e == 0)
    def _first_read():
        x_copy(0, 0).start()

    n = count_ref[e]

    @pl.when(n > 0)
    def _cast_weights():
        w1b[...] = w1_ref[...].astype(BF16)
        w3b[...] = w3_ref[...].astype(BF16)
        w2b[...] = w2_ref[...].astype(BF16)

    def tile_body(g, carry):
        slot = g & 1
        x_copy(g, slot).wait()

        @pl.when(g + 1 < n_used)
        def _read_next():
            x_copy(g + 1, 1 - slot).start()

        @pl.when(g >= 2)
        def _free_out_slot():
            y_copy(g - 2, slot).wait()

        lo, hi = _unpack_rows(xbuf[slot])
        lo, hi = lo.astype(BF16), hi.astype(BF16)
        a = (jnp.dot(lo, w1b[:HALF, :], preferred_element_type=F32)
             + jnp.dot(hi, w1b[HALF:, :], preferred_element_type=F32))
        u = (jnp.dot(lo, w3b[:HALF, :], preferred_element_type=F32)
             + jnp.dot(hi, w3b[HALF:, :], preferred_element_type=F32))
        y = jnp.dot((_silu(a) * u).astype(BF16), w2b[...], preferred_element_type=F32)
        ybuf[slot] = _pack_rows(y)
        y_copy(g, slot).start()
        return carry

    g0 = first_ref[e]
    lax.fori_loop(g0, g0 + n, tile_body, 0)

    @pl.when(e == pl.num_programs(0) - 1)
    def _drain_writes():
        @pl.when(n_used >= 2)
        def _():
            y_copy(n_used - 2, n_used & 1).wait()
        y_copy(n_used - 1, (n_used - 1) & 1).wait()


def _experts(tile_first, tile_count, n_used, xs, w1, w3, w2, layer):
    n_rows = xs.shape[0]
    r = EXPERT_TILE
    w_map = lambda e, f, c, u: (layer, e, 0, 0)
    return pl.pallas_call(
        _expert_kernel,
        grid_spec=pltpu.PrefetchScalarGridSpec(
            num_scalar_prefetch=3,
            grid=(N_EXPERTS,),
            in_specs=[pl.BlockSpec((None, None, D_MODEL, EXPERT_FF), w_map),
                      pl.BlockSpec((None, None, D_MODEL, EXPERT_FF), w_map),
                      pl.BlockSpec((None, None, EXPERT_FF, D_MODEL), w_map),
                      pl.BlockSpec(memory_space=pl.ANY)],
            out_specs=pl.BlockSpec(memory_space=pl.ANY),
            scratch_shapes=[pltpu.VMEM((2, r, HALF), jnp.uint32),
                            pltpu.VMEM((2, r, HALF), jnp.uint32),
                            pltpu.SemaphoreType.DMA((2,)),
                            pltpu.SemaphoreType.DMA((2,)),
                            pltpu.VMEM((D_MODEL, EXPERT_FF), BF16),
                            pltpu.VMEM((D_MODEL, EXPERT_FF), BF16),
                            pltpu.VMEM((EXPERT_FF, D_MODEL), BF16)],
        ),
        out_shape=jax.ShapeDtypeStruct((n_rows, HALF), jnp.uint32),
        compiler_params=_params(1),
        name="experts",
    )(tile_first, tile_count, n_used, w1, w3, w2, xs)


def _combine_dense_kernel(base_ref, g2_ref, w_ref, yg_ref, o_ref):
    acc_lo = acc_hi = None
    for k in range(TOP_K):
        lo, hi = _unpack_rows(yg_ref[k])
        wk = w_ref[:, k:k + 1]
        acc_lo = wk * lo if acc_lo is None else acc_lo + wk * lo
        acc_hi = wk * hi if acc_hi is None else acc_hi + wk * hi
    o_ref[:, :HALF] = base_ref[:, :HALF] + g2_ref[:, :HALF] * acc_lo
    o_ref[:, HALF:] = base_ref[:, HALF:] + g2_ref[:, HALF:] * acc_hi


def _combine_dense(base, g2, w_tok, yg, seq):
    t = base.shape[0]
    nt = ROUTE_TILE
    tpb = seq // nt
    return pl.pallas_call(
        _combine_dense_kernel,
        grid=(t // nt,),
        in_specs=[pl.BlockSpec((nt, D_MODEL), lambda i: (i, 0)),
                  pl.BlockSpec((None, 1, D_MODEL), lambda i: (i // tpb, 0, 0)),
                  pl.BlockSpec((nt, TOP_K), lambda i: (i, 0)),
                  pl.BlockSpec((TOP_K, nt, HALF), lambda i: (0, i, 0))],
        out_specs=pl.BlockSpec((nt, D_MODEL), lambda i: (i, 0)),
        out_shape=jax.ShapeDtypeStruct((t, D_MODEL), F32),
        compiler_params=_params(1),
        name="combine_dense",
    )(base, g2, w_tok, yg)


def _layer(layer, x, c, w_ada, b_ada, norm1_w, norm2_w, w_in, q_norm_w, k_norm_w, rel_bias, w_alpha, b_alpha,
           moba_out_w, gla_out_w, w_out, w_router, e_bias, w1, w3, w2, ws1, ws3, ws2):
    b, s, d = x.shape
    t = b * s
    x2 = x.reshape(t, d)

    mod = _mod(c, w_ada, b_ada)
    sh1, sc1, g1, sh2, sc2, g2 = [mod[:, j * d:(j + 1) * d].reshape(b, 1, d) for j in range(6)]

    w_main = w_in[:, :D_MAIN].astype(BF16)
    w_ga = jnp.zeros((d, LANES), BF16).at[:, :GLA_GATE_RANK].set(w_in[:, D_MAIN:].astype(BF16))
    per_chunk = 256 // MOBA_HEAD_DIM
    qw = jnp.tile(q_norm_w.astype(F32), per_chunk).reshape(1, 256) * (MOBA_HEAD_DIM ** -0.5)
    kw = jnp.tile(k_norm_w.astype(F32), per_chunk).reshape(1, 256)
    proj, ga = _inproj(x2, sc1, sh1, norm1_w.reshape(1, d), w_main, w_ga, qw, kw, s)
    proj3 = proj.reshape(b, s, D_MAIN)

    near, far = _moba_bias_tables(rel_bias)
    ow = jnp.tile(moba_out_w.astype(F32), 2).reshape(1, LANES)
    o_a = _moba(proj3, near, far, ow)

    wal = jnp.zeros((LANES, GLA_KEY_WIDTH), F32).at[:GLA_GATE_RANK].set(w_alpha)
    o_b = _gla(proj3, ga.reshape(b, s, LANES), wal, b_alpha.reshape(1, GLA_KEY_WIDTH),
               gla_out_w.reshape(1, GLA_DV))

    base, h2, scores_t = _outproj(
        o_a.reshape(t, MOBA_WIDTH), o_b.reshape(t, GLA_WIDTH), x2, g1, sc2, sh2, g2,
        norm2_w.reshape(1, d), w_out.astype(BF16), ws1.astype(BF16), ws3.astype(BF16), ws2.astype(BF16),
        w_router.T.astype(BF16), s)

    eb = jnp.broadcast_to(e_bias.astype(F32)[:, None], (N_EXPERTS, ROUTE_TILE))
    code_t, w_t, counts = _route(scores_t, eb)

    r = EXPERT_TILE
    n_tiles = (t * TOP_K + N_EXPERTS * (r - 1) + r - 1) // r
    n_rows = n_tiles * r
    cnt = counts[:, 0].astype(jnp.int32)
    padded = (cnt + r - 1) // r * r
    pend = jnp.cumsum(padded)
    pstart = pend - padded
    n_used = (pend[-1:] // r).astype(jnp.int32)
    dest_t = _slots(pstart, code_t)

    xs = _sc_scatter_rows(h2, dest_t, n_rows)
    ys = _experts(pstart // r, padded // r, n_used, xs, w1, w3, w2, layer)
    yg = _sc_gather_rows(ys, dest_t.reshape(TOP_K * t)).reshape(TOP_K, t, HALF)
    out = _combine_dense(base, g2, w_t.T, yg, s)
    return out.reshape(b, s, d)


def kernel(x, c, w_ada, b_ada, norm1_w, norm2_w, w_in, q_norm_w, k_norm_w, rel_bias, w_alpha, b_alpha,
           moba_out_w, gla_out_w, w_out, w_router, e_bias, w1, w3, w2, ws1, ws3, ws2):
    for l in range(w_ada.shape[0]):
        x = _layer(l, x, c, w_ada[l], b_ada[l], norm1_w[l], norm2_w[l], w_in[l], q_norm_w[l], k_norm_w[l],
                   rel_bias, w_alpha[l], b_alpha[l], moba_out_w[l], gla_out_w[l], w_out[l], w_router[l],
                   e_bias[l], w1, w3, w2, ws1[l], ws3[l], ws2[l])
    return x
```

```python
import functools
import math

import numpy as np
import jax
import jax.numpy as jnp
from jax import lax
from jax.experimental import pallas as pl
from jax.experimental.pallas import tpu as pltpu

D_MODEL = 1024
MOBA_HEADS = 8
MOBA_HEAD_DIM = 64
MOBA_WIDTH = MOBA_HEADS * MOBA_HEAD_DIM
MOBA_BLOCK = 256
MOBA_TOPK = 3
GLA_HEADS = 4
GLA_DK = 64
GLA_DV = 128
GLA_KEY_WIDTH = GLA_HEADS * GLA_DK
GLA_WIDTH = GLA_HEADS * GLA_DV
GLA_GATE_RANK = 16
GLA_GATE_TAU = 16.0
GLA_CHUNK = 64
REL_BUCKETS = 32
REL_MAX_DIST = 128
N_EXPERTS = 256
TOP_K = 8
N_GROUPS = 8
TOPK_GROUPS = 4
GROUP_SIZE = N_EXPERTS // N_GROUPS
EXPERT_FF = 256
SHARED_FF = 256
ROUTED_SCALE = 2.5
NORM_EPS = 1e-6

D_MAIN = 3 * MOBA_WIDTH + 2 * GLA_KEY_WIDTH + 2 * GLA_WIDTH
LANES = 128
VMEM_LIMIT = 56 * 1024 * 1024

ROW_TILE = 512
ROUTE_TILE = 256
EXPERT_TILE = 256

F32 = jnp.float32
BF16 = jnp.bfloat16
NT_DIMS = (((1,), (1,)), ((), ()))
TN_DIMS = (((0,), (0,)), ((), ()))


def _params(n_axes):
    return pltpu.CompilerParams(dimension_semantics=("arbitrary",) * n_axes,
                                vmem_limit_bytes=VMEM_LIMIT)


def _silu(v):
    return v * jax.nn.sigmoid(v)


def _mod_kernel(c_ref, w_ref, b_ref, o_ref):
    o_ref[...] = jnp.dot(_silu(c_ref[...]), w_ref[...], preferred_element_type=F32) + b_ref[...]


def _mod(c, w, b):
    rows = 8
    cp = jnp.zeros((rows, D_MODEL), F32).at[:c.shape[0]].set(c)
    n = w.shape[1]
    tn = 1024
    out = pl.pallas_call(
        _mod_kernel,
        grid=(n // tn,),
        in_specs=[pl.BlockSpec((rows, D_MODEL), lambda j: (0, 0)),
                  pl.BlockSpec((D_MODEL, tn), lambda j: (0, j)),
                  pl.BlockSpec((1, tn), lambda j: (0, j))],
        out_specs=pl.BlockSpec((rows, tn), lambda j: (0, j)),
        out_shape=jax.ShapeDtypeStruct((rows, n), F32),
        compiler_params=_params(1),
        name="mod",
    )(cp, w, b.reshape(1, n))
    return out[:c.shape[0]]


def _group_rms_inv(a, group):
    lane = lax.broadcasted_iota(jnp.int32, (1, a.shape[1]), 1)
    a2 = a * a
    inv = jnp.zeros_like(a)
    for g in range(a.shape[1] // group):
        m = (lane >= g * group) & (lane < (g + 1) * group)
        ss = jnp.sum(jnp.where(m, a2, 0.0), axis=-1, keepdims=True)
        inv = jnp.where(m, lax.rsqrt(ss * (1.0 / group) + NORM_EPS), inv)
    return inv


def _inproj_kernel(x_ref, sc_ref, sh_ref, nw_ref, w_ref, wga_ref, qw_ref, kw_ref, o_ref, ga_ref):
    x = x_ref[...]
    ms = jnp.mean(x * x, axis=-1, keepdims=True)
    h = x * lax.rsqrt(ms + NORM_EPS) * nw_ref[...]
    h = h * (1.0 + sc_ref[...]) + sh_ref[...]
    hb = h.astype(BF16)
    cw = 256
    for j in range(D_MAIN // cw):
        acc = jnp.dot(hb, w_ref[:, j * cw:(j + 1) * cw], preferred_element_type=F32)
        if j < 2 * MOBA_WIDTH // cw:
            nw = qw_ref if j < MOBA_WIDTH // cw else kw_ref
            acc = acc * _group_rms_inv(acc, MOBA_HEAD_DIM) * nw[...]
        o_ref[:, j * cw:(j + 1) * cw] = acc.astype(BF16)
    ga_ref[...] = jnp.dot(hb, wga_ref[...], preferred_element_type=F32)


def _inproj(x2, sc, sh, nw, w_main, w_ga, qw, kw, seq):
    t = x2.shape[0]
    tpb = seq // ROW_TILE
    vec = lambda: pl.BlockSpec((None, 1, D_MODEL), lambda i: (i // tpb, 0, 0))
    full = lambda a: pl.BlockSpec(a.shape, lambda i: (0,) * a.ndim)
    return pl.pallas_call(
        _inproj_kernel,
        grid=(t // ROW_TILE,),
        in_specs=[pl.BlockSpec((ROW_TILE, D_MODEL), lambda i: (i, 0)), vec(), vec(),
                  full(nw), full(w_main), full(w_ga), full(qw), full(kw)],
        out_specs=[pl.BlockSpec((ROW_TILE, D_MAIN), lambda i: (i, 0)),
                   pl.BlockSpec((ROW_TILE, LANES), lambda i: (i, 0))],
        out_shape=[jax.ShapeDtypeStruct((t, D_MAIN), BF16),
                   jax.ShapeDtypeStruct((t, LANES), F32)],
        compiler_params=_params(1),
        name="inproj",
    )(x2, sc, sh, nw, w_main, w_ga, qw, kw)


def _t5_bucket_np(rel):
    max_exact = REL_BUCKETS // 2
    relf = np.maximum(rel, 1).astype(np.float64)
    large = max_exact + (np.log(relf / max_exact) / math.log(REL_MAX_DIST / max_exact)
                         * (REL_BUCKETS - max_exact)).astype(np.int32)
    large = np.minimum(large, REL_BUCKETS - 1)
    return np.where(rel < max_exact, rel, large)


def _bias_kernel(rb_ref, idx_ref, o_ref):
    h = pl.program_id(0)
    idx = idx_ref[...]
    tab = jnp.full(idx.shape, -jnp.inf, F32)
    for bk in range(REL_BUCKETS):
        tab = jnp.where(idx == bk, rb_ref[bk * MOBA_HEADS + h], tab)
    o_ref[...] = tab


def _moba_bias_tables(rel_bias):
    j = np.arange(MOBA_BLOCK)[:, None]
    i = np.arange(MOBA_BLOCK)[None, :]
    own_idx = np.where(j <= i, _t5_bucket_np(np.maximum(i - j, 0)), -1)
    prev_idx = _t5_bucket_np(MOBA_BLOCK + i - j)
    idx = jnp.asarray(np.concatenate([prev_idx, own_idx], axis=0).astype(np.int32))
    assert int(_t5_bucket_np(np.array([MOBA_BLOCK + 1]))[0]) == REL_BUCKETS - 1
    rb = rel_bias.astype(F32)
    near = pl.pallas_call(
        _bias_kernel,
        grid=(MOBA_HEADS,),
        in_specs=[pl.BlockSpec(memory_space=pltpu.SMEM),
                  pl.BlockSpec(idx.shape, lambda h: (0, 0))],
        out_specs=pl.BlockSpec((None,) + idx.shape, lambda h: (h, 0, 0)),
        out_shape=jax.ShapeDtypeStruct((MOBA_HEADS,) + idx.shape, F32),
        compiler_params=_params(1),
        name="bias",
    )(rb.reshape(-1), idx)
    return near, rb[REL_BUCKETS - 1]


FAR_GROUP = 4


def _moba_kernel(far_ref, q_ref, k_ref, v_ref, near_ref, ow_ref, o_ref,
                 vt_ref, vtg_ref, acc_ref, m_ref, sel_ref, s_ref):
    hp = pl.program_id(1)
    i = pl.program_id(2)
    nblk = k_ref.shape[0] // MOBA_BLOCK
    ngrp = nblk // FAR_GROUP
    hd = MOBA_HEAD_DIM
    bs = MOBA_BLOCK
    lane = lax.broadcasted_iota(jnp.int32, (bs, LANES), 1)

    def split_heads(qb):
        zero = jnp.zeros_like(qb)
        return jnp.where(lane < hd, qb, zero), jnp.where(lane < hd, zero, qb)

    @pl.when(i == 0)
    def _prepare():
        row = lax.broadcasted_iota(jnp.int32, (LANES, bs), 0)
        kmeans = []
        for n in range(nblk):
            kb = k_ref[n * bs:(n + 1) * bs, :].astype(F32)
            kmeans.append(jnp.mean(kb, axis=0, keepdims=True))
            vt = v_ref[n * bs:(n + 1) * bs, :].astype(F32).T
            vt0 = jnp.where(row < hd, vt, 1.0).astype(BF16)
            vt1 = jnp.where(row < hd, 1.0, vt).astype(BF16)
            vt_ref[0, n] = vt0
            vt_ref[1, n] = vt1
            gcols = slice((n % FAR_GROUP) * bs, (n % FAR_GROUP + 1) * bs)
            vtg_ref[0, n // FAR_GROUP, :, gcols] = vt0
            vtg_ref[1, n // FAR_GROUP, :, gcols] = vt1
        kmean = jnp.concatenate(kmeans, axis=0)
        km_hi = kmean.astype(BF16)
        km_lo = (kmean - km_hi.astype(F32)).astype(BF16)
        blk = lax.broadcasted_iota(jnp.int32, (nblk, bs), 0)
        for ib in range(nblk):
            qparts = split_heads(q_ref[ib * bs:(ib + 1) * bs, :])
            for h in range(2):
                gt = (lax.dot_general(km_hi, qparts[h], NT_DIMS, preferred_element_type=F32)
                      + lax.dot_general(km_lo, qparts[h], NT_DIMS, preferred_element_type=F32))
                gt = jnp.where(blk < ib, gt, -jnp.inf)
                cnt = jnp.zeros(gt.shape, jnp.int32)
                for m in range(ib):
                    gm = gt[m:m + 1, :]
                    cnt = cnt + jnp.where((gm > gt) | ((gm == gt) & (blk > m)), 1, 0)
                keep = (blk < ib) & (cnt < MOBA_TOPK)
                sel_ref[0, h, ib] = jnp.where(keep, 1.0, 0.0)
                sel_ref[1, h, ib] = jnp.where(keep & (blk < ib - 1), 1.0, 0.0)

    qh = split_heads(q_ref[pl.ds(pl.multiple_of(i * bs, bs), bs), :])

    @pl.when(i == 0)
    def _own_block_only():
        kb = k_ref[0:bs, :]
        for h in range(2):
            s = lax.dot_general(kb, qh[h], NT_DIMS, preferred_element_type=F32) + near_ref[h, bs:2 * bs, :]
            m_new = jnp.max(s, axis=0, keepdims=True)
            p = jnp.exp(s - m_new).astype(BF16)
            acc_ref[h] = jnp.dot(vt_ref[h, 0], p, preferred_element_type=F32)
            m_ref[h] = m_new

    @pl.when(i >= 1)
    def _previous_and_own_block():
        kbs = (k_ref[pl.ds(pl.multiple_of((i - 1) * bs, bs), bs), :],
               k_ref[pl.ds(pl.multiple_of(i * bs, bs), bs), :])
        ss = [[lax.dot_general(kbs[w], qh[h], NT_DIMS, preferred_element_type=F32)
               + near_ref[h, w * bs:(w + 1) * bs, :] for w in range(2)] for h in range(2)]
        ps, ms = [], []
        for h in range(2):
            s_prev, s_own = ss[h]
            keep = sel_ref[0, h, i, pl.ds(i - 1, 1), :] > 0.5
            mx = jnp.where(keep, jnp.max(s_prev, axis=0, keepdims=True), -jnp.inf)
            m_new = jnp.maximum(jnp.max(s_own, axis=0, keepdims=True), mx)
            ps.append((jnp.exp(s_prev - jnp.where(keep, m_new, jnp.inf)).astype(BF16),
                       jnp.exp(s_own - m_new).astype(BF16)))
            ms.append(m_new)
        for h in range(2):
            acc_ref[h] = (jnp.dot(vt_ref[h, i - 1], ps[h][0], preferred_element_type=F32)
                          + jnp.dot(vt_ref[h, i], ps[h][1], preferred_element_type=F32))
            m_ref[h] = ms[h]

    n_far = (i + FAR_GROUP - 2) // FAR_GROUP
    gk = FAR_GROUP * bs

    def far_scores(g):
        kb = k_ref[g * gk:(g + 1) * gk, :]
        for h in range(2):
            s_ref[g % 2, h] = lax.dot_general(kb, qh[h], NT_DIMS, preferred_element_type=F32)

    @pl.when(n_far > 0)
    def _first_far_scores():
        far_scores(0)

    for g in range(ngrp):
        @pl.when(g < n_far)
        def _far_group(g=g):
            if g + 1 < ngrp:
                far_scores(g + 1)
            for h in range(2):
                fb = far_ref[2 * hp + h]
                m_old = m_ref[h]
                m_new = m_old
                keeps = []
                for j in range(FAR_GROUP):
                    keep = sel_ref[1, h, i, pl.ds(g * FAR_GROUP + j, 1), :] > 0.5
                    mx = jnp.max(s_ref[g % 2, h, j * bs:(j + 1) * bs, :], axis=0, keepdims=True) + fb
                    m_new = jnp.maximum(m_new, jnp.where(keep, mx, -jnp.inf))
                    keeps.append(keep)
                p = jnp.concatenate(
                    [jnp.exp(s_ref[g % 2, h, j * bs:(j + 1) * bs, :]
                             - jnp.where(keeps[j], m_new - fb, jnp.inf)).astype(BF16)
                     for j in range(FAR_GROUP)], axis=0)
                pv = jnp.dot(vtg_ref[h, g], p, preferred_element_type=F32)
                acc_ref[h] = acc_ref[h] * jnp.exp(m_old - m_new) + pv
                m_ref[h] = m_new

    a0 = acc_ref[0]
    a1 = acc_ref[1]
    row = lax.broadcasted_iota(jnp.int32, a0.shape, 0)
    ot = jnp.where(row < hd, a0 / a0[hd:hd + 1, :], a1 / a1[0:1, :])
    o2 = ot * ot
    ss0 = jnp.sum(jnp.where(row < hd, o2, 0.0), axis=0, keepdims=True)
    ss1 = jnp.sum(jnp.where(row < hd, 0.0, o2), axis=0, keepdims=True)
    inv = jnp.where(row < hd, lax.rsqrt(ss0 * (1.0 / hd) + NORM_EPS), lax.rsqrt(ss1 * (1.0 / hd) + NORM_EPS))
    o_ref[...] = ((ot * inv).T * ow_ref[...]).astype(o_ref.dtype)


def _moba(proj3, near, far, ow):
    b, s, _ = proj3.shape
    nblk = s // MOBA_BLOCK
    assert nblk % FAR_GROUP == 0
    npair = MOBA_HEADS // 2
    kcol = MOBA_WIDTH // LANES
    return pl.pallas_call(
        _moba_kernel,
        grid=(b, npair, nblk),
        in_specs=[pl.BlockSpec(memory_space=pltpu.SMEM),
                  pl.BlockSpec((None, s, LANES), lambda bb, hp, i: (bb, 0, hp)),
                  pl.BlockSpec((None, s, LANES), lambda bb, hp, i: (bb, 0, kcol + hp)),
                  pl.BlockSpec((None, s, LANES), lambda bb, hp, i: (bb, 0, 2 * kcol + hp)),
                  pl.BlockSpec((2, 2 * MOBA_BLOCK, MOBA_BLOCK), lambda bb, hp, i: (hp, 0, 0)),
                  pl.BlockSpec((1, LANES), lambda bb, hp, i: (0, 0))],
        out_specs=pl.BlockSpec((None, MOBA_BLOCK, LANES), lambda bb, hp, i: (bb, i, hp)),
        out_shape=jax.ShapeDtypeStruct((b, s, MOBA_WIDTH), BF16),
        scratch_shapes=[pltpu.VMEM((2, nblk, LANES, MOBA_BLOCK), BF16),
                        pltpu.VMEM((2, nblk // FAR_GROUP, LANES, FAR_GROUP * MOBA_BLOCK), BF16),
                        pltpu.VMEM((2, LANES, MOBA_BLOCK), F32),
                        pltpu.VMEM((2, 1, MOBA_BLOCK), F32),
                        pltpu.VMEM((2, 2, nblk, nblk, MOBA_BLOCK), F32),
                        pltpu.VMEM((2, 2, FAR_GROUP * MOBA_BLOCK, MOBA_BLOCK), F32)],
        compiler_params=_params(3),
        name="moba",
    )(far, proj3, proj3, proj3, near, ow)


def _split3(v):
    hi = v.astype(BF16)
    r1 = v - hi.astype(F32)
    mid = r1.astype(BF16)
    lo = (r1 - mid.astype(F32)).astype(BF16)
    return hi, mid, lo


def _gla_kernel(q_ref, k_ref, v_ref, g_ref, ga_ref, wal_ref, bal_ref, gw_ref, o_ref, b_ref, st_ref):
    seq = q_ref.shape[0]
    c = GLA_CHUNK
    pc = 256

    rr = lax.broadcasted_iota(jnp.int32, (pc, pc), 0)
    cc = lax.broadcasted_iota(jnp.int32, (pc, pc), 1)
    tri = jnp.where((rr >= cc) & (rr // c == cc // c), 1.0, 0.0).astype(BF16)

    def decay_body(j, carry):
        r0 = pl.multiple_of(j * pc, pc)
        xg = jnp.dot(ga_ref[pl.ds(r0, pc), :], wal_ref[...], preferred_element_type=F32) + bal_ref[...]
        la = (jnp.minimum(xg, 0.0) - jnp.log(1.0 + jnp.exp(-jnp.abs(xg)))) * (1.0 / GLA_GATE_TAU)
        hi, mid, lo = _split3(la)
        b_ref[pl.ds(r0, pc), :] = (jnp.dot(tri, hi, preferred_element_type=F32)
                                   + jnp.dot(tri, mid, preferred_element_type=F32)
                                   + jnp.dot(tri, lo, preferred_element_type=F32))
        return carry

    lax.fori_loop(0, seq // pc, decay_body, 0)

    st_ref[...] = jnp.zeros_like(st_ref)
    lane = lax.broadcasted_iota(jnp.int32, (c, LANES), 1)
    head_mask = (lane < GLA_DK, lane >= GLA_DK)
    causal = lax.broadcasted_iota(jnp.int32, (c, c), 0) >= lax.broadcasted_iota(jnp.int32, (c, c), 1)

    def chunk_body(ci, carry):
        r0 = pl.multiple_of(ci * c, c)
        b = b_ref[pl.ds(r0, c), :]
        ref_row = b[c // 2 - 1:c // 2, :]
        last = b[c - 1:c, :]
        q = q_ref[pl.ds(r0, c), :].astype(F32) * (GLA_DK ** -0.5)
        k = k_ref[pl.ds(r0, c), :].astype(F32)
        qt = q * jnp.exp(b - ref_row)
        kt = (k * jnp.exp(ref_row - b)).astype(BF16)
        qs = q * jnp.exp(b)
        ke = (k * jnp.exp(last - b)).astype(BF16)
        e_last = jnp.exp(last)
        for h in range(2):
            cols = slice(h * GLA_DV, (h + 1) * GLA_DV)
            a = lax.dot_general(jnp.where(head_mask[h], qt, 0.0).astype(BF16), kt, NT_DIMS,
                                preferred_element_type=F32)
            a = jnp.where(causal, a, 0.0).astype(BF16)
            v = v_ref[pl.ds(r0, c), cols]
            st = st_ref[h]
            o = jnp.dot(a, v, preferred_element_type=F32)
            o = o + lax.dot_general(jnp.where(head_mask[h], qs, 0.0).astype(BF16), st.astype(BF16), NT_DIMS,
                                    preferred_element_type=F32)
            st_ref[h] = st * e_last + lax.dot_general(v, ke, TN_DIMS, preferred_element_type=F32)
            ms = jnp.mean(o * o, axis=-1, keepdims=True)
            on = o * lax.rsqrt(ms + NORM_EPS) * gw_ref[...]
            g = g_ref[pl.ds(r0, c), cols].astype(F32)
            o_ref[pl.ds(r0, c), cols] = (on * _silu(g)).astype(o_ref.dtype)
        return carry

    lax.fori_loop(0, seq // c, chunk_body, 0)


def _gla(proj3, ga3, wal, bal, gw):
    b, s, _ = proj3.shape
    npair = GLA_HEADS // 2
    qcol = 3 * MOBA_WIDTH // LANES
    kcol = qcol + GLA_KEY_WIDTH // LANES
    vcol = (3 * MOBA_WIDTH + 2 * GLA_KEY_WIDTH) // (2 * GLA_DV)
    gcol = vcol + npair
    return pl.pallas_call(
        _gla_kernel,
        grid=(b, npair),
        in_specs=[pl.BlockSpec((None, s, LANES), lambda bb, hp: (bb, 0, qcol + hp)),
                  pl.BlockSpec((None, s, LANES), lambda bb, hp: (bb, 0, kcol + hp)),
                  pl.BlockSpec((None, s, 2 * GLA_DV), lambda bb, hp: (bb, 0, vcol + hp)),
                  pl.BlockSpec((None, s, 2 * GLA_DV), lambda bb, hp: (bb, 0, gcol + hp)),
                  pl.BlockSpec((None, s, LANES), lambda bb, hp: (bb, 0, 0)),
                  pl.BlockSpec((LANES, LANES), lambda bb, hp: (0, hp)),
                  pl.BlockSpec((1, LANES), lambda bb, hp: (0, hp)),
                  pl.BlockSpec((1, GLA_DV), lambda bb, hp: (0, 0))],
        out_specs=pl.BlockSpec((None, s, 2 * GLA_DV), lambda bb, hp: (bb, 0, hp)),
        out_shape=jax.ShapeDtypeStruct((b, s, GLA_WIDTH), BF16),
        scratch_shapes=[pltpu.VMEM((s, LANES), F32),
                        pltpu.VMEM((2, GLA_DV, LANES), F32)],
        compiler_params=_params(2),
        name="gla",
    )(proj3, proj3, proj3, proj3, ga3, wal, bal, gw)


HALF = D_MODEL // 2


def _pack_rows(v):
    def bf16_bits(a):
        u = lax.bitcast_convert_type(a, jnp.uint32)
        return (u + (jnp.uint32(0x7FFF) + ((u >> 16) & jnp.uint32(1)))) >> 16
    return bf16_bits(v[:, :HALF]) | (bf16_bits(v[:, HALF:]) << 16)


def _unpack_rows(w):
    return (lax.bitcast_convert_type(w << 16, F32),
            lax.bitcast_convert_type(w & jnp.uint32(0xFFFF0000), F32))


def _outproj_kernel(oa_ref, ob_ref, x_ref, g1_ref, sc_ref, sh_ref, g2_ref, nw_ref, wo_ref,
                    ws1_ref, ws3_ref, ws2_ref, wrt_ref, base_ref, h_ref, st_ref):
    mix = (jnp.dot(oa_ref[...], wo_ref[:MOBA_WIDTH, :], preferred_element_type=F32)
           + jnp.dot(ob_ref[...], wo_ref[MOBA_WIDTH:, :], preferred_element_type=F32))
    x1 = x_ref[...] + g1_ref[...] * mix
    ms = jnp.mean(x1 * x1, axis=-1, keepdims=True)
    h = x1 * lax.rsqrt(ms + NORM_EPS) * nw_ref[...]
    h = h * (1.0 + sc_ref[...]) + sh_ref[...]
    h_ref[...] = _pack_rows(h)
    hb = h.astype(BF16)
    a = jnp.dot(hb, ws1_ref[...], preferred_element_type=F32)
    u = jnp.dot(hb, ws3_ref[...], preferred_element_type=F32)
    shared = jnp.dot((_silu(a) * u).astype(BF16), ws2_ref[...], preferred_element_type=F32)
    base_ref[...] = x1 + g2_ref[...] * shared
    logits_t = lax.dot_general(wrt_ref[...], hb, NT_DIMS, preferred_element_type=F32)
    st_ref[...] = jax.nn.sigmoid(logits_t)


def _outproj(oa, ob, x2, g1, sc, sh, g2, nw, wo, ws1, ws3, ws2, wrt, seq):
    t = x2.shape[0]
    tpb = seq // ROW_TILE
    vec = lambda: pl.BlockSpec((None, 1, D_MODEL), lambda i: (i // tpb, 0, 0))
    full = lambda a: pl.BlockSpec(a.shape, lambda i: (0,) * a.ndim)
    rows = lambda w: pl.BlockSpec((ROW_TILE, w), lambda i: (i, 0))
    return pl.pallas_call(
        _outproj_kernel,
        grid=(t // ROW_TILE,),
        in_specs=[rows(MOBA_WIDTH), rows(GLA_WIDTH), rows(D_MODEL), vec(), vec(), vec(), vec(),
                  full(nw), full(wo), full(ws1), full(ws3), full(ws2), full(wrt)],
        out_specs=[rows(D_MODEL), rows(HALF), pl.BlockSpec((N_EXPERTS, ROW_TILE), lambda i: (0, i))],
        out_shape=[jax.ShapeDtypeStruct((t, D_MODEL), F32),
                   jax.ShapeDtypeStruct((t, HALF), jnp.uint32),
                   jax.ShapeDtypeStruct((N_EXPERTS, t), F32)],
        compiler_params=_params(1),
        name="outproj",
    )(oa, ob, x2, g1, sc, sh, g2, nw, wo, ws1, ws3, ws2, wrt)


SLOT_CODE_SHIFT = 16
SLOT_CODE_BASE = 1 << SLOT_CODE_SHIFT


def _route_kernel(s_ref, eb_ref, code_ref, w_ref, cnt_ref, carry_ref):
    i = pl.program_id(0)
    ne, nt = s_ref.shape

    @pl.when(i == 0)
    def _init():
        carry_ref[...] = jnp.zeros_like(carry_ref)

    s = s_ref[...]
    choice = s + eb_ref[...]
    gio = lax.broadcasted_iota(jnp.int32, (GROUP_SIZE, nt), 0)
    gscore = []
    for g in range(N_GROUPS):
        cg = choice[g * GROUP_SIZE:(g + 1) * GROUP_SIZE, :]
        top1 = jnp.max(cg, axis=0, keepdims=True)
        first = jnp.min(jnp.where(cg == top1, gio, GROUP_SIZE), axis=0, keepdims=True)
        top2 = jnp.max(jnp.where(gio == first, -jnp.inf, cg), axis=0, keepdims=True)
        gscore.append(top1 + top2)
    gs = jnp.concatenate(gscore, axis=0)
    gidx = lax.broadcasted_iota(jnp.int32, gs.shape, 0)
    beaten = jnp.zeros(gs.shape, jnp.int32)
    for m in range(N_GROUPS):
        gm = gs[m:m + 1, :]
        beaten = beaten + jnp.where((gm > gs) | ((gm == gs) & (gidx > m)), 1, 0)
    gkeep = beaten < TOPK_GROUPS
    masked = jnp.concatenate(
        [jnp.where(gkeep[g:g + 1, :], choice[g * GROUP_SIZE:(g + 1) * GROUP_SIZE, :], -jnp.inf)
         for g in range(N_GROUPS)], axis=0)

    eio = lax.broadcasted_iota(jnp.int32, (ne, nt), 0)
    picked = jnp.zeros((ne, nt), F32)
    idx_rows, w_rows, hits = [], [], []
    for _ in range(TOP_K):
        mx = jnp.max(masked, axis=0, keepdims=True)
        idx = jnp.min(jnp.where(masked == mx, eio, ne), axis=0, keepdims=True)
        hit = eio == idx
        w_rows.append(jnp.sum(jnp.where(hit, s, 0.0), axis=0, keepdims=True))
        idx_rows.append(idx)
        hits.append(hit)
        masked = jnp.where(hit, -jnp.inf, masked)
        picked = jnp.where(hit, 1.0, picked)
    wk = jnp.concatenate(w_rows, axis=0)
    w_ref[...] = wk / jnp.sum(wk, axis=0, keepdims=True) * ROUTED_SCALE

    tr = lax.broadcasted_iota(jnp.int32, (nt, nt), 0)
    tc = lax.broadcasted_iota(jnp.int32, (nt, nt), 1)
    before = jnp.where(tr < tc, 1.0, 0.0).astype(BF16)
    pb = picked.astype(BF16)
    pos = carry_ref[...] + jnp.dot(pb, before, preferred_element_type=F32)
    rank = jnp.concatenate(
        [jnp.sum(jnp.where(hit, pos, 0.0), axis=0, keepdims=True) for hit in hits], axis=0).astype(jnp.int32)
    code_ref[...] = jnp.concatenate(idx_rows, axis=0) * SLOT_CODE_BASE + rank
    total = carry_ref[...] + jnp.dot(pb, jnp.ones((nt, nt), BF16), preferred_element_type=F32)
    carry_ref[...] = total
    cnt_ref[...] = total


def _route(scores_t, eb):
    ne, t = scores_t.shape
    assert t <= SLOT_CODE_BASE
    nt = ROUTE_TILE
    tok = lambda dt: jax.ShapeDtypeStruct((TOP_K, t), dt)
    return pl.pallas_call(
        _route_kernel,
        grid=(t // nt,),
        in_specs=[pl.BlockSpec((ne, nt), lambda i: (0, i)),
                  pl.BlockSpec((ne, nt), lambda i: (0, 0))],
        out_specs=[pl.BlockSpec((TOP_K, nt), lambda i: (0, i)),
                   pl.BlockSpec((TOP_K, nt), lambda i: (0, i)),
                   pl.BlockSpec((ne, nt), lambda i: (0, 0))],
        out_shape=[tok(jnp.int32), tok(F32), jax.ShapeDtypeStruct((ne, nt), F32)],
        scratch_shapes=[pltpu.VMEM((ne, nt), F32)],
        compiler_params=_params(1),
        name="route",
    )(scores_t, eb)


SLOT_TILE = 2048


def _slots_kernel(pstart_ref, code_ref, o_ref):
    code = code_ref[...]
    expert = lax.shift_right_logical(code, SLOT_CODE_SHIFT)

    def body(e, acc):
        return jnp.where(expert == e, pstart_ref[e], acc)

    start = lax.fori_loop(0, N_EXPERTS, body, jnp.zeros_like(code), unroll=8)
    o_ref[...] = start + (code & (SLOT_CODE_BASE - 1))


def _slots(pstart, code_t):
    k, t = code_t.shape
    return pl.pallas_call(
        _slots_kernel,
        grid_spec=pltpu.PrefetchScalarGridSpec(
            num_scalar_prefetch=1,
            grid=(t // SLOT_TILE,),
            in_specs=[pl.BlockSpec((k, SLOT_TILE), lambda i, p: (0, i))],
            out_specs=pl.BlockSpec((k, SLOT_TILE), lambda i, p: (0, i)),
        ),
        out_shape=jax.ShapeDtypeStruct((k, t), jnp.int32),
        compiler_params=_params(1),
        name="slots",
    )(pstart, code_t)


SC_WINDOW = 128


def _sc_gather_rows(table, idx_flat):
    from jax.experimental.pallas import tpu_sc as plsc
    info = plsc.get_sparse_core_info()
    nw = info.num_cores * info.num_subcores
    n = idx_flat.shape[0]
    width = table.shape[1]
    per_worker = n // nw
    assert per_worker * nw == n and per_worker % SC_WINDOW == 0
    mesh = plsc.VectorSubcoreMesh(core_axis_name="c", subcore_axis_name="s")

    def body(table_hbm, idx_hbm, out_hbm, idx_v, rows_v, sem):
        wid = lax.axis_index("s") * info.num_cores + lax.axis_index("c")
        base = wid * per_worker

        @pl.loop(0, per_worker // SC_WINDOW)
        def _(w):
            off = pl.multiple_of(base + w * SC_WINDOW, SC_WINDOW)
            pltpu.sync_copy(idx_hbm.at[pl.ds(off, SC_WINDOW)], idx_v)
            pltpu.async_copy(table_hbm.at[idx_v], rows_v, sem).wait()
            pltpu.sync_copy(rows_v, out_hbm.at[pl.ds(off, SC_WINDOW)])

    return pl.kernel(
        body,
        out_type=jax.ShapeDtypeStruct((n, width), table.dtype),
        mesh=mesh,
        scratch_types=[pltpu.VMEM((SC_WINDOW,), jnp.int32),
                       pltpu.VMEM((SC_WINDOW, width), table.dtype),
                       pltpu.SemaphoreType.DMA],
        name="sc_gather",
    )(table, idx_flat)


def _sc_scatter_rows(rows, idx_kt, n_out):
    from jax.experimental.pallas import tpu_sc as plsc
    info = plsc.get_sparse_core_info()
    nw = info.num_cores * info.num_subcores
    t, width = rows.shape
    nk = idx_kt.shape[0]
    per_worker = t // nw
    assert per_worker * nw == t and per_worker % SC_WINDOW == 0
    mesh = plsc.VectorSubcoreMesh(core_axis_name="c", subcore_axis_name="s")

    def body(rows_hbm, idx_hbm, out_hbm, idx_v, rows_v, sem):
        wid = lax.axis_index("s") * info.num_cores + lax.axis_index("c")
        base = wid * per_worker

        @pl.loop(0, per_worker // SC_WINDOW)
        def _(w):
            off = pl.multiple_of(base + w * SC_WINDOW, SC_WINDOW)
            pltpu.sync_copy(rows_hbm.at[pl.ds(off, SC_WINDOW)], rows_v)
            pltpu.sync_copy(idx_hbm.at[:, pl.ds(off, SC_WINDOW)], idx_v)
            copies = [pltpu.async_copy(rows_v, out_hbm.at[idx_v.at[k]], sem) for k in range(nk)]
            for cp in copies:
                cp.wait()

    return pl.kernel(
        body,
        out_type=jax.ShapeDtypeStruct((n_out, width), rows.dtype),
        mesh=mesh,
        scratch_types=[pltpu.VMEM((nk, SC_WINDOW), jnp.int32),
                       pltpu.VMEM((SC_WINDOW, width), rows.dtype),
                       pltpu.SemaphoreType.DMA],
        name="sc_scatter",
    )(rows, idx_kt)


def _expert_kernel(first_ref, count_ref, used_ref, w1_ref, w3_ref, w2_ref, xs_ref, ys_ref,
                   xbuf, ybuf, sem_in, sem_out, w1b, w3b, w2b):
    e = pl.program_id(0)
    r = xbuf.shape[1]
    n_used = used_ref[0]

    def x_copy(g, slot):
        return pltpu.make_async_copy(xs_ref.at[pl.ds(pl.multiple_of(g * r, r), r), :], xbuf.at[slot], sem_in.at[slot])

    def y_copy(g, slot):
        return pltpu.make_async_copy(ybuf.at[slot], ys_ref.at[pl.ds(pl.multiple_of(g * r, r), r), :], sem_out.at[slot])

    @pl.when(e == 0)
    def _first_read():
        x_copy(0, 0).start()

    n = count_ref[e]

    @pl.when(n > 0)
    def _cast_weights():
        w1b[...] = w1_ref[...].astype(BF16)
        w3b[...] = w3_ref[...].astype(BF16)
        w2b[...] = w2_ref[...].astype(BF16)

    def tile_body(g, carry):
        slot = g & 1
        x_copy(g, slot).wait()

        @pl.when(g + 1 < n_used)
        def _read_next():
            x_copy(g + 1, 1 - slot).start()

        @pl.when(g >= 2)
        def _free_out_slot():
            y_copy(g - 2, slot).wait()

        lo, hi = _unpack_rows(xbuf[slot])
        lo, hi = lo.astype(BF16), hi.astype(BF16)
        a = (jnp.dot(lo, w1b[:HALF, :], preferred_element_type=F32)
             + jnp.dot(hi, w1b[HALF:, :], preferred_element_type=F32))
        u = (jnp.dot(lo, w3b[:HALF, :], preferred_element_type=F32)
             + jnp.dot(hi, w3b[HALF:, :], preferred_element_type=F32))
        y = jnp.dot((_silu(a) * u).astype(BF16), w2b[...], preferred_element_type=F32)
        ybuf[slot] = _pack_rows(y)
        y_copy(g, slot).start()
        return carry

    g0 = first_ref[e]
    lax.fori_loop(g0, g0 + n, tile_body, 0)

    @pl.when(e == pl.num_programs(0) - 1)
    def _drain_writes():
        @pl.when(n_used >= 2)
        def _():
            y_copy(n_used - 2, n_used & 1).wait()
        y_copy(n_used - 1, (n_used - 1) & 1).wait()


def _experts(tile_first, tile_count, n_used, xs, w1, w3, w2, layer):
    n_rows = xs.shape[0]
    r = EXPERT_TILE
    w_map = lambda e, f, c, u: (layer, e, 0, 0)
    return pl.pallas_call(
        _expert_kernel,
        grid_spec=pltpu.PrefetchScalarGridSpec(
            num_scalar_prefetch=3,
            grid=(N_EXPERTS,),
            in_specs=[pl.BlockSpec((None, None, D_MODEL, EXPERT_FF), w_map),
                      pl.BlockSpec((None, None, D_MODEL, EXPERT_FF), w_map),
                      pl.BlockSpec((None, None, EXPERT_FF, D_MODEL), w_map),
                      pl.BlockSpec(memory_space=pl.ANY)],
            out_specs=pl.BlockSpec(memory_space=pl.ANY),
            scratch_shapes=[pltpu.VMEM((2, r, HALF), jnp.uint32),
                            pltpu.VMEM((2, r, HALF), jnp.uint32),
                            pltpu.SemaphoreType.DMA((2,)),
                            pltpu.SemaphoreType.DMA((2,)),
                            pltpu.VMEM((D_MODEL, EXPERT_FF), BF16),
                            pltpu.VMEM((D_MODEL, EXPERT_FF), BF16),
                            pltpu.VMEM((EXPERT_FF, D_MODEL), BF16)],
        ),
        out_shape=jax.ShapeDtypeStruct((n_rows, HALF), jnp.uint32),
        compiler_params=_params(1),
        name="experts",
    )(tile_first, tile_count, n_used, w1, w3, w2, xs)


def _combine_dense_kernel(base_ref, g2_ref, w_ref, yg_ref, o_ref):
    acc_lo = acc_hi = None
    for k in range(TOP_K):
        lo, hi = _unpack_rows(yg_ref[k])
        wk = w_ref[:, k:k + 1]
        acc_lo = wk * lo if acc_lo is None else acc_lo + wk * lo
        acc_hi = wk * hi if acc_hi is None else acc_hi + wk * hi
    o_ref[:, :HALF] = base_ref[:, :HALF] + g2_ref[:, :HALF] * acc_lo
    o_ref[:, HALF:] = base_ref[:, HALF:] + g2_ref[:, HALF:] * acc_hi


def _combine_dense(base, g2, w_tok, yg, seq):
    t = base.shape[0]
    nt = ROUTE_TILE
    tpb = seq // nt
    return pl.pallas_call(
        _combine_dense_kernel,
        grid=(t // nt,),
        in_specs=[pl.BlockSpec((nt, D_MODEL), lambda i: (i, 0)),
                  pl.BlockSpec((None, 1, D_MODEL), lambda i: (i // tpb, 0, 0)),
                  pl.BlockSpec((nt, TOP_K), lambda i: (i, 0)),
                  pl.BlockSpec((TOP_K, nt, HALF), lambda i: (0, i, 0))],
        out_specs=pl.BlockSpec((nt, D_MODEL), lambda i: (i, 0)),
        out_shape=jax.ShapeDtypeStruct((t, D_MODEL), F32),
        compiler_params=_params(1),
        name="combine_dense",
    )(base, g2, w_tok, yg)


def _layer(layer, x, c, w_ada, b_ada, norm1_w, norm2_w, w_in, q_norm_w, k_norm_w, rel_bias, w_alpha, b_alpha,
           moba_out_w, gla_out_w, w_out, w_router, e_bias, w1, w3, w2, ws1, ws3, ws2):
    b, s, d = x.shape
    t = b * s
    x2 = x.reshape(t, d)

    mod = _mod(c, w_ada, b_ada)
    sh1, sc1, g1, sh2, sc2, g2 = [mod[:, j * d:(j + 1) * d].reshape(b, 1, d) for j in range(6)]

    w_main = w_in[:, :D_MAIN].astype(BF16)
    w_ga = jnp.zeros((d, LANES), BF16).at[:, :GLA_GATE_RANK].set(w_in[:, D_MAIN:].astype(BF16))
    per_chunk = 256 // MOBA_HEAD_DIM
    qw = jnp.tile(q_norm_w.astype(F32), per_chunk).reshape(1, 256) * (MOBA_HEAD_DIM ** -0.5)
    kw = jnp.tile(k_norm_w.astype(F32), per_chunk).reshape(1, 256)
    proj, ga = _inproj(x2, sc1, sh1, norm1_w.reshape(1, d), w_main, w_ga, qw, kw, s)
    proj3 = proj.reshape(b, s, D_MAIN)

    near, far = _moba_bias_tables(rel_bias)
    ow = jnp.tile(moba_out_w.astype(F32), 2).reshape(1, LANES)
    o_a = _moba(proj3, near, far, ow)

    wal = jnp.zeros((LANES, GLA_KEY_WIDTH), F32).at[:GLA_GATE_RANK].set(w_alpha)
    o_b = _gla(proj3, ga.reshape(b, s, LANES), wal, b_alpha.reshape(1, GLA_KEY_WIDTH),
               gla_out_w.reshape(1, GLA_DV))

    base, h2, scores_t = _outproj(
        o_a.reshape(t, MOBA_WIDTH), o_b.reshape(t, GLA_WIDTH), x2, g1, sc2, sh2, g2,
        norm2_w.reshape(1, d), w_out.astype(BF16), ws1.astype(BF16), ws3.astype(BF16), ws2.astype(BF16),
        w_router.T.astype(BF16), s)

    eb = jnp.broadcast_to(e_bias.astype(F32)[:, None], (N_EXPERTS, ROUTE_TILE))
    code_t, w_t, counts = _route(scores_t, eb)

    r = EXPERT_TILE
    n_tiles = (t * TOP_K + N_EXPERTS * (r - 1) + r - 1) // r
    n_rows = n_tiles * r
    cnt = counts[:, 0].astype(jnp.int32)
    padded = (cnt + r - 1) // r * r
    pend = jnp.cumsum(padded)
    pstart = pend - padded
    n_used = (pend[-1:] // r).astype(jnp.int32)
    dest_t = _slots(pstart, code_t)

    xs = _sc_scatter_rows(h2, dest_t, n_rows)
    ys = _experts(pstart // r, padded // r, n_used, xs, w1, w3, w2, layer)
    yg = _sc_gather_rows(ys, dest_t.reshape(TOP_K * t)).reshape(TOP_K, t, HALF)
    out = _combine_dense(base, g2, w_t.T, yg, s)
    return out.reshape(b, s, d)


def kernel(x, c, w_ada, b_ada, norm1_w, norm2_w, w_in, q_norm_w, k_norm_w, rel_bias, w_alpha, b_alpha,
           moba_out_w, gla_out_w, w_out, w_router, e_bias, w1, w3, w2, ws1, ws3, ws2):
    for l in range(w_ada.shape[0]):
        x = _layer(l, x, c, w_ada[l], b_ada[l], norm1_w[l], norm2_w[l], w_in[l], q_norm_w[l], k_norm_w[l],
                   rel_bias, w_alpha[l], b_alpha[l], moba_out_w[l], gla_out_w[l], w_out[l], w_router[l],
                   e_bias[l], w1, w3, w2, ws1[l], ws3[l], ws2[l])
    return x
```

```python
import functools
import math

import numpy as np
import jax
import jax.numpy as jnp
from jax import lax
from jax.experimental import pallas as pl
from jax.experimental.pallas import tpu as pltpu

D_MODEL = 1024
MOBA_HEADS = 8
MOBA_HEAD_DIM = 64
MOBA_WIDTH = MOBA_HEADS * MOBA_HEAD_DIM
MOBA_BLOCK = 256
MOBA_TOPK = 3
GLA_HEADS = 4
GLA_DK = 64
GLA_DV = 128
GLA_KEY_WIDTH = GLA_HEADS * GLA_DK
GLA_WIDTH = GLA_HEADS * GLA_DV
GLA_GATE_RANK = 16
GLA_GATE_TAU = 16.0
GLA_CHUNK = 64
REL_BUCKETS = 32
REL_MAX_DIST = 128
N_EXPERTS = 256
TOP_K = 8
N_GROUPS = 8
TOPK_GROUPS = 4
GROUP_SIZE = N_EXPERTS // N_GROUPS
EXPERT_FF = 256
SHARED_FF = 256
ROUTED_SCALE = 2.5
NORM_EPS = 1e-6

D_MAIN = 3 * MOBA_WIDTH + 2 * GLA_KEY_WIDTH + 2 * GLA_WIDTH
LANES = 128
VMEM_LIMIT = 56 * 1024 * 1024

ROW_TILE = 512
ROUTE_TILE = 256
EXPERT_TILE = 256

F32 = jnp.float32
BF16 = jnp.bfloat16
NT_DIMS = (((1,), (1,)), ((), ()))
TN_DIMS = (((0,), (0,)), ((), ()))


def _params(n_axes):
    return pltpu.CompilerParams(dimension_semantics=("arbitrary",) * n_axes,
                                vmem_limit_bytes=VMEM_LIMIT)


def _silu(v):
    return v * jax.nn.sigmoid(v)


def _mod_kernel(c_ref, w_ref, b_ref, o_ref):
    o_ref[...] = jnp.dot(_silu(c_ref[...]), w_ref[...], preferred_element_type=F32) + b_ref[...]


def _mod(c, w, b):
    rows = 8
    cp = jnp.zeros((rows, D_MODEL), F32).at[:c.shape[0]].set(c)
    n = w.shape[1]
    tn = 1024
    out = pl.pallas_call(
        _mod_kernel,
        grid=(n // tn,),
        in_specs=[pl.BlockSpec((rows, D_MODEL), lambda j: (0, 0)),
                  pl.BlockSpec((D_MODEL, tn), lambda j: (0, j)),
                  pl.BlockSpec((1, tn), lambda j: (0, j))],
        out_specs=pl.BlockSpec((rows, tn), lambda j: (0, j)),
        out_shape=jax.ShapeDtypeStruct((rows, n), F32),
        compiler_params=_params(1),
        name="mod",
    )(cp, w, b.reshape(1, n))
    return out[:c.shape[0]]


def _group_rms_inv(a, group):
    lane = lax.broadcasted_iota(jnp.int32, (1, a.shape[1]), 1)
    a2 = a * a
    inv = jnp.zeros_like(a)
    for g in range(a.shape[1] // group):
        m = (lane >= g * group) & (lane < (g + 1) * group)
        ss = jnp.sum(jnp.where(m, a2, 0.0), axis=-1, keepdims=True)
        inv = jnp.where(m, lax.rsqrt(ss * (1.0 / group) + NORM_EPS), inv)
    return inv


def _inproj_kernel(x_ref, sc_ref, sh_ref, nw_ref, w_ref, wga_ref, qw_ref, kw_ref, o_ref, ga_ref):
    x = x_ref[...]
    ms = jnp.mean(x * x, axis=-1, keepdims=True)
    h = x * lax.rsqrt(ms + NORM_EPS) * nw_ref[...]
    h = h * (1.0 + sc_ref[...]) + sh_ref[...]
    hb = h.astype(BF16)
    cw = 256
    for j in range(D_MAIN // cw):
        acc = jnp.dot(hb, w_ref[:, j * cw:(j + 1) * cw], preferred_element_type=F32)
        if j < 2 * MOBA_WIDTH // cw:
            nw = qw_ref if j < MOBA_WIDTH // cw else kw_ref
            acc = acc * _group_rms_inv(acc, MOBA_HEAD_DIM) * nw[...]
        o_ref[:, j * cw:(j + 1) * cw] = acc.astype(BF16)
    ga_ref[...] = jnp.dot(hb, wga_ref[...], preferred_element_type=F32)


def _inproj(x2, sc, sh, nw, w_main, w_ga, qw, kw, seq):
    t = x2.shape[0]
    tpb = seq // ROW_TILE
    vec = lambda: pl.BlockSpec((None, 1, D_MODEL), lambda i: (i // tpb, 0, 0))
    full = lambda a: pl.BlockSpec(a.shape, lambda i: (0,) * a.ndim)
    return pl.pallas_call(
        _inproj_kernel,
        grid=(t // ROW_TILE,),
        in_specs=[pl.BlockSpec((ROW_TILE, D_MODEL), lambda i: (i, 0)), vec(), vec(),
                  full(nw), full(w_main), full(w_ga), full(qw), full(kw)],
        out_specs=[pl.BlockSpec((ROW_TILE, D_MAIN), lambda i: (i, 0)),
                   pl.BlockSpec((ROW_TILE, LANES), lambda i: (i, 0))],
        out_shape=[jax.ShapeDtypeStruct((t, D_MAIN), BF16),
                   jax.ShapeDtypeStruct((t, LANES), F32)],
        compiler_params=_params(1),
        name="inproj",
    )(x2, sc, sh, nw, w_main, w_ga, qw, kw)


def _t5_bucket_np(rel):
    max_exact = REL_BUCKETS // 2
    relf = np.maximum(rel, 1).astype(np.float64)
    large = max_exact + (np.log(relf / max_exact) / math.log(REL_MAX_DIST / max_exact)
                         * (REL_BUCKETS - max_exact)).astype(np.int32)
    large = np.minimum(large, REL_BUCKETS - 1)
    return np.where(rel < max_exact, rel, large)


def _bias_kernel(rb_ref, idx_ref, o_ref):
    h = pl.program_id(0)
    idx = idx_ref[...]
    tab = jnp.full(idx.shape, -jnp.inf, F32)
    for bk in range(REL_BUCKETS):
        tab = jnp.where(idx == bk, rb_ref[bk * MOBA_HEADS + h], tab)
    o_ref[...] = tab


def _moba_bias_tables(rel_bias):
    j = np.arange(MOBA_BLOCK)[:, None]
    i = np.arange(MOBA_BLOCK)[None, :]
    own_idx = np.where(j <= i, _t5_bucket_np(np.maximum(i - j, 0)), -1)
    prev_idx = _t5_bucket_np(MOBA_BLOCK + i - j)
    idx = jnp.asarray(np.concatenate([prev_idx, own_idx], axis=0).astype(np.int32))
    assert int(_t5_bucket_np(np.array([MOBA_BLOCK + 1]))[0]) == REL_BUCKETS - 1
    rb = rel_bias.astype(F32)
    near = pl.pallas_call(
        _bias_kernel,
        grid=(MOBA_HEADS,),
        in_specs=[pl.BlockSpec(memory_space=pltpu.SMEM),
                  pl.BlockSpec(idx.shape, lambda h: (0, 0))],
        out_specs=pl.BlockSpec((None,) + idx.shape, lambda h: (h, 0, 0)),
        out_shape=jax.ShapeDtypeStruct((MOBA_HEADS,) + idx.shape, F32),
        compiler_params=_params(1),
        name="bias",
    )(rb.reshape(-1), idx)
    return near, rb[REL_BUCKETS - 1]


FAR_GROUP = 4


def _moba_kernel(far_ref, q_ref, k_ref, v_ref, near_ref, ow_ref, o_ref,
                 vt_ref, vtg_ref, acc_ref, m_ref, sel_ref, s_ref):
    hp = pl.program_id(1)
    i = pl.program_id(2)
    nblk = k_ref.shape[0] // MOBA_BLOCK
    ngrp = nblk // FAR_GROUP
    hd = MOBA_HEAD_DIM
    bs = MOBA_BLOCK
    lane = lax.broadcasted_iota(jnp.int32, (bs, LANES), 1)

    def split_heads(qb):
        zero = jnp.zeros_like(qb)
        return jnp.where(lane < hd, qb, zero), jnp.where(lane < hd, zero, qb)

    @pl.when(i == 0)
    def _prepare():
        row = lax.broadcasted_iota(jnp.int32, (LANES, bs), 0)
        kmeans = []
        for n in range(nblk):
            kb = k_ref[n * bs:(n + 1) * bs, :].astype(F32)
            kmeans.append(jnp.mean(kb, axis=0, keepdims=True))
            vt = v_ref[n * bs:(n + 1) * bs, :].astype(F32).T
            vt0 = jnp.where(row < hd, vt, 1.0).astype(BF16)
            vt1 = jnp.where(row < hd, 1.0, vt).astype(BF16)
            vt_ref[0, n] = vt0
            vt_ref[1, n] = vt1
            gcols = slice((n % FAR_GROUP) * bs, (n % FAR_GROUP + 1) * bs)
            vtg_ref[0, n // FAR_GROUP, :, gcols] = vt0
            vtg_ref[1, n // FAR_GROUP, :, gcols] = vt1
        kmean = jnp.concatenate(kmeans, axis=0)
        km_hi = kmean.astype(BF16)
        km_lo = (kmean - km_hi.astype(F32)).astype(BF16)
        blk = lax.broadcasted_iota(jnp.int32, (nblk, bs), 0)
        for ib in range(nblk):
            qparts = split_heads(q_ref[ib * bs:(ib + 1) * bs, :])
            for h in range(2):
                gt = (lax.dot_general(km_hi, qparts[h], NT_DIMS, preferred_element_type=F32)
                      + lax.dot_general(km_lo, qparts[h], NT_DIMS, preferred_element_type=F32))
                gt = jnp.where(blk < ib, gt, -jnp.inf)
                cnt = jnp.zeros(gt.shape, jnp.int32)
                for m in range(ib):
                    gm = gt[m:m + 1, :]
                    cnt = cnt + jnp.where((gm > gt) | ((gm == gt) & (blk > m)), 1, 0)
                keep = (blk < ib) & (cnt < MOBA_TOPK)
                sel_ref[0, h, ib] = jnp.where(keep, 1.0, 0.0)
                sel_ref[1, h, ib] = jnp.where(keep & (blk < ib - 1), 1.0, 0.0)

    qh = split_heads(q_ref[pl.ds(pl.multiple_of(i * bs, bs), bs), :])

    @pl.when(i == 0)
    def _own_block_only():
        kb = k_ref[0:bs, :]
        for h in range(2):
            s = lax.dot_general(kb, qh[h], NT_DIMS, preferred_element_type=F32) + near_ref[h, bs:2 * bs, :]
            m_new = jnp.max(s, axis=0, keepdims=True)
            p = jnp.exp(s - m_new).astype(BF16)
            acc_ref[h] = jnp.dot(vt_ref[h, 0], p, preferred_element_type=F32)
            m_ref[h] = m_new

    @pl.when(i >= 1)
    def _previous_and_own_block():
        kbs = (k_ref[pl.ds(pl.multiple_of((i - 1) * bs, bs), bs), :],
               k_ref[pl.ds(pl.multiple_of(i * bs, bs), bs), :])
        ss = [[lax.dot_general(kbs[w], qh[h], NT_DIMS, preferred_element_type=F32)
               + near_ref[h, w * bs:(w + 1) * bs, :] for w in range(2)] for h in range(2)]
        ps, ms = [], []
        for h in range(2):
            s_prev, s_own = ss[h]
            keep = sel_ref[0, h, i, pl.ds(i - 1, 1), :] > 0.5
            mx = jnp.where(keep, jnp.max(s_prev, axis=0, keepdims=True), -jnp.inf)
            m_new = jnp.maximum(jnp.max(s_own, axis=0, keepdims=True), mx)
            ps.append((jnp.exp(s_prev - jnp.where(keep, m_new, jnp.inf)).astype(BF16),
                       jnp.exp(s_own - m_new).astype(BF16)))
            ms.append(m_new)
        for h in range(2):
            acc_ref[h] = (jnp.dot(vt_ref[h, i - 1], ps[h][0], preferred_element_type=F32)
                          + jnp.dot(vt_ref[h, i], ps[h][1], preferred_element_type=F32))
            m_ref[h] = ms[h]

    n_far = (i + FAR_GROUP - 2) // FAR_GROUP
    gk = FAR_GROUP * bs

    def far_scores(g):
        kb = k_ref[g * gk:(g + 1) * gk, :]
        for h in range(2):
            s_ref[g % 2, h] = lax.dot_general(kb, qh[h], NT_DIMS, preferred_element_type=F32)

    @pl.when(n_far > 0)
    def _first_far_scores():
        far_scores(0)

    for g in range(ngrp):
        @pl.when(g < n_far)
        def _far_group(g=g):
            if g + 1 < ngrp:
                far_scores(g + 1)
            for h in range(2):
                fb = far_ref[2 * hp + h]
                m_old = m_ref[h]
                m_new = m_old
                keeps = []
                for j in range(FAR_GROUP):
                    keep = sel_ref[1, h, i, pl.ds(g * FAR_GROUP + j, 1), :] > 0.5
                    mx = jnp.max(s_ref[g % 2, h, j * bs:(j + 1) * bs, :], axis=0, keepdims=True) + fb
                    m_new = jnp.maximum(m_new, jnp.where(keep, mx, -jnp.inf))
                    keeps.append(keep)
                p = jnp.concatenate(
                    [jnp.exp(s_ref[g % 2, h, j * bs:(j + 1) * bs, :]
                             - jnp.where(keeps[j], m_new - fb, jnp.inf)).astype(BF16)
                     for j in range(FAR_GROUP)], axis=0)
                pv = jnp.dot(vtg_ref[h, g], p, preferred_element_type=F32)
                acc_ref[h] = acc_ref[h] * jnp.exp(m_old - m_new) + pv
                m_ref[h] = m_new

    a0 = acc_ref[0]
    a1 = acc_ref[1]
    row = lax.broadcasted_iota(jnp.int32, a0.shape, 0)
    ot = jnp.where(row < hd, a0 / a0[hd:hd + 1, :], a1 / a1[0:1, :])
    o2 = ot * ot
    ss0 = jnp.sum(jnp.where(row < hd, o2, 0.0), axis=0, keepdims=True)
    ss1 = jnp.sum(jnp.where(row < hd, 0.0, o2), axis=0, keepdims=True)
    inv = jnp.where(row < hd, lax.rsqrt(ss0 * (1.0 / hd) + NORM_EPS), lax.rsqrt(ss1 * (1.0 / hd) + NORM_EPS))
    o_ref[...] = ((ot * inv).T * ow_ref[...]).astype(o_ref.dtype)


def _moba(proj3, near, far, ow):
    b, s, _ = proj3.shape
    nblk = s // MOBA_BLOCK
    assert nblk % FAR_GROUP == 0
    npair = MOBA_HEADS // 2
    kcol = MOBA_WIDTH // LANES
    return pl.pallas_call(
        _moba_kernel,
        grid=(b, npair, nblk),
        in_specs=[pl.BlockSpec(memory_space=pltpu.SMEM),
                  pl.BlockSpec((None, s, LANES), lambda bb, hp, i: (bb, 0, hp)),
                  pl.BlockSpec((None, s, LANES), lambda bb, hp, i: (bb, 0, kcol + hp)),
                  pl.BlockSpec((None, s, LANES), lambda bb, hp, i: (bb, 0, 2 * kcol + hp)),
                  pl.BlockSpec((2, 2 * MOBA_BLOCK, MOBA_BLOCK), lambda bb, hp, i: (hp, 0, 0)),
                  pl.BlockSpec((1, LANES), lambda bb, hp, i: (0, 0))],
        out_specs=pl.BlockSpec((None, MOBA_BLOCK, LANES), lambda bb, hp, i: (bb, i, hp)),
        out_shape=jax.ShapeDtypeStruct((b, s, MOBA_WIDTH), BF16),
        scratch_shapes=[pltpu.VMEM((2, nblk, LANES, MOBA_BLOCK), BF16),
                        pltpu.VMEM((2, nblk // FAR_GROUP, LANES, FAR_GROUP * MOBA_BLOCK), BF16),
                        pltpu.VMEM((2, LANES, MOBA_BLOCK), F32),
                        pltpu.VMEM((2, 1, MOBA_BLOCK), F32),
                        pltpu.VMEM((2, 2, nblk, nblk, MOBA_BLOCK), F32),
                        pltpu.VMEM((2, 2, FAR_GROUP * MOBA_BLOCK, MOBA_BLOCK), F32)],
        compiler_params=_params(3),
        name="moba",
    )(far, proj3, proj3, proj3, near, ow)


def _split3(v):
    hi = v.astype(BF16)
    r1 = v - hi.astype(F32)
    mid = r1.astype(BF16)
    lo = (r1 - mid.astype(F32)).astype(BF16)
    return hi, mid, lo


def _gla_kernel(q_ref, k_ref, v_ref, g_ref, ga_ref, wal_ref, bal_ref, gw_ref, o_ref, b_ref, st_ref):
    seq = q_ref.shape[0]
    c = GLA_CHUNK
    pc = 256

    rr = lax.broadcasted_iota(jnp.int32, (pc, pc), 0)
    cc = lax.broadcasted_iota(jnp.int32, (pc, pc), 1)
    tri = jnp.where((rr >= cc) & (rr // c == cc // c), 1.0, 0.0).astype(BF16)

    def decay_body(j, carry):
        r0 = pl.multiple_of(j * pc, pc)
        xg = jnp.dot(ga_ref[pl.ds(r0, pc), :], wal_ref[...], preferred_element_type=F32) + bal_ref[...]
        la = (jnp.minimum(xg, 0.0) - jnp.log(1.0 + jnp.exp(-jnp.abs(xg)))) * (1.0 / GLA_GATE_TAU)
        hi, mid, lo = _split3(la)
        b_ref[pl.ds(r0, pc), :] = (jnp.dot(tri, hi, preferred_element_type=F32)
                                   + jnp.dot(tri, mid, preferred_element_type=F32)
                                   + jnp.dot(tri, lo, preferred_element_type=F32))
        return carry

    lax.fori_loop(0, seq // pc, decay_body, 0)

    st_ref[...] = jnp.zeros_like(st_ref)
    lane = lax.broadcasted_iota(jnp.int32, (c, LANES), 1)
    head_mask = (lane < GLA_DK, lane >= GLA_DK)
    causal = lax.broadcasted_iota(jnp.int32, (c, c), 0) >= lax.broadcasted_iota(jnp.int32, (c, c), 1)

    def chunk_body(ci, carry):
        r0 = pl.multiple_of(ci * c, c)
        b = b_ref[pl.ds(r0, c), :]
        ref_row = b[c // 2 - 1:c // 2, :]
        last = b[c - 1:c, :]
        q = q_ref[pl.ds(r0, c), :].astype(F32) * (GLA_DK ** -0.5)
        k = k_ref[pl.ds(r0, c), :].astype(F32)
        qt = q * jnp.exp(b - ref_row)
        kt = (k * jnp.exp(ref_row - b)).astype(BF16)
        qs = q * jnp.exp(b)
        ke = (k * jnp.exp(last - b)).astype(BF16)
        e_last = jnp.exp(last)
        for h in range(2):
            cols = slice(h * GLA_DV, (h + 1) * GLA_DV)
            a = lax.dot_general(jnp.where(head_mask[h], qt, 0.0).astype(BF16), kt, NT_DIMS,
                                preferred_element_type=F32)
            a = jnp.where(causal, a, 0.0).astype(BF16)
            v = v_ref[pl.ds(r0, c), cols]
            st = st_ref[h]
            o = jnp.dot(a, v, preferred_element_type=F32)
            o = o + lax.dot_general(jnp.where(head_mask[h], qs, 0.0).astype(BF16), st.astype(BF16), NT_DIMS,
                                    preferred_element_type=F32)
            st_ref[h] = st * e_last + lax.dot_general(v, ke, TN_DIMS, preferred_element_type=F32)
            ms = jnp.mean(o * o, axis=-1, keepdims=True)
            on = o * lax.rsqrt(ms + NORM_EPS) * gw_ref[...]
            g = g_ref[pl.ds(r0, c), cols].astype(F32)
            o_ref[pl.ds(r0, c), cols] = (on * _silu(g)).astype(o_ref.dtype)
        return carry

    lax.fori_loop(0, seq // c, chunk_body, 0)


def _gla(proj3, ga3, wal, bal, gw):
    b, s, _ = proj3.shape
    npair = GLA_HEADS // 2
    qcol = 3 * MOBA_WIDTH // LANES
    kcol = qcol + GLA_KEY_WIDTH // LANES
    vcol = (3 * MOBA_WIDTH + 2 * GLA_KEY_WIDTH) // (2 * GLA_DV)
    gcol = vcol + npair
    return pl.pallas_call(
        _gla_kernel,
        grid=(b, npair),
        in_specs=[pl.BlockSpec((None, s, LANES), lambda bb, hp: (bb, 0, qcol + hp)),
                  pl.BlockSpec((None, s, LANES), lambda bb, hp: (bb, 0, kcol + hp)),
                  pl.BlockSpec((None, s, 2 * GLA_DV), lambda bb, hp: (bb, 0, vcol + hp)),
                  pl.BlockSpec((None, s, 2 * GLA_DV), lambda bb, hp: (bb, 0, gcol + hp)),
                  pl.BlockSpec((None, s, LANES), lambda bb, hp: (bb, 0, 0)),
                  pl.BlockSpec((LANES, LANES), lambda bb, hp: (0, hp)),
                  pl.BlockSpec((1, LANES), lambda bb, hp: (0, hp)),
                  pl.BlockSpec((1, GLA_DV), lambda bb, hp: (0, 0))],
        out_specs=pl.BlockSpec((None, s, 2 * GLA_DV), lambda bb, hp: (bb, 0, hp)),
        out_shape=jax.ShapeDtypeStruct((b, s, GLA_WIDTH), BF16),
        scratch_shapes=[pltpu.VMEM((s, LANES), F32),
                        pltpu.VMEM((2, GLA_DV, LANES), F32)],
        compiler_params=_params(2),
        name="gla",
    )(proj3, proj3, proj3, proj3, ga3, wal, bal, gw)


HALF = D_MODEL // 2


def _pack_rows(v):
    def bf16_bits(a):
        u = lax.bitcast_convert_type(a, jnp.uint32)
        return (u + (jnp.uint32(0x7FFF) + ((u >> 16) & jnp.uint32(1)))) >> 16
    return bf16_bits(v[:, :HALF]) | (bf16_bits(v[:, HALF:]) << 16)


def _unpack_rows(w):
    return (lax.bitcast_convert_type(w << 16, F32),
            lax.bitcast_convert_type(w & jnp.uint32(0xFFFF0000), F32))


def _outproj_kernel(oa_ref, ob_ref, x_ref, g1_ref, sc_ref, sh_ref, g2_ref, nw_ref, wo_ref,
                    ws1_ref, ws3_ref, ws2_ref, wrt_ref, base_ref, h_ref, st_ref):
    mix = (jnp.dot(oa_ref[...], wo_ref[:MOBA_WIDTH, :], preferred_element_type=F32)
           + jnp.dot(ob_ref[...], wo_ref[MOBA_WIDTH:, :], preferred_element_type=F32))
    x1 = x_ref[...] + g1_ref[...] * mix
    ms = jnp.mean(x1 * x1, axis=-1, keepdims=True)
    h = x1 * lax.rsqrt(ms + NORM_EPS) * nw_ref[...]
    h = h * (1.0 + sc_ref[...]) + sh_ref[...]
    h_ref[...] = _pack_rows(h)
    hb = h.astype(BF16)
    a = jnp.dot(hb, ws1_ref[...], preferred_element_type=F32)
    u = jnp.dot(hb, ws3_ref[...], preferred_element_type=F32)
    shared = jnp.dot((_silu(a) * u).astype(BF16), ws2_ref[...], preferred_element_type=F32)
    base_ref[...] = x1 + g2_ref[...] * shared
    logits_t = lax.dot_general(wrt_ref[...], hb, NT_DIMS, preferred_element_type=F32)
    st_ref[...] = jax.nn.sigmoid(logits_t)


def _outproj(oa, ob, x2, g1, sc, sh, g2, nw, wo, ws1, ws3, ws2, wrt, seq):
    t = x2.shape[0]
    tpb = seq // ROW_TILE
    vec = lambda: pl.BlockSpec((None, 1, D_MODEL), lambda i: (i // tpb, 0, 0))
    full = lambda a: pl.BlockSpec(a.shape, lambda i: (0,) * a.ndim)
    rows = lambda w: pl.BlockSpec((ROW_TILE, w), lambda i: (i, 0))
    return pl.pallas_call(
        _outproj_kernel,
        grid=(t // ROW_TILE,),
        in_specs=[rows(MOBA_WIDTH), rows(GLA_WIDTH), rows(D_MODEL), vec(), vec(), vec(), vec(),
                  full(nw), full(wo), full(ws1), full(ws3), full(ws2), full(wrt)],
        out_specs=[rows(D_MODEL), rows(HALF), pl.BlockSpec((N_EXPERTS, ROW_TILE), lambda i: (0, i))],
        out_shape=[jax.ShapeDtypeStruct((t, D_MODEL), F32),
                   jax.ShapeDtypeStruct((t, HALF), jnp.uint32),
                   jax.ShapeDtypeStruct((N_EXPERTS, t), F32)],
        compiler_params=_params(1),
        name="outproj",
    )(oa, ob, x2, g1, sc, sh, g2, nw, wo, ws1, ws3, ws2, wrt)


SLOT_CODE_SHIFT = 16
SLOT_CODE_BASE = 1 << SLOT_CODE_SHIFT


def _route_kernel(s_ref, eb_ref, code_ref, w_ref, cnt_ref, carry_ref):
    i = pl.program_id(0)
    ne, nt = s_ref.shape

    @pl.when(i == 0)
    def _init():
        carry_ref[...] = jnp.zeros_like(carry_ref)

    s = s_ref[...]
    choice = s + eb_ref[...]
    gio = lax.broadcasted_iota(jnp.int32, (GROUP_SIZE, nt), 0)
    gscore = []
    for g in range(N_GROUPS):
        cg = choice[g * GROUP_SIZE:(g + 1) * GROUP_SIZE, :]
        top1 = jnp.max(cg, axis=0, keepdims=True)
        first = jnp.min(jnp.where(cg == top1, gio, GROUP_SIZE), axis=0, keepdims=True)
        top2 = jnp.max(jnp.where(gio == first, -jnp.inf, cg), axis=0, keepdims=True)
        gscore.append(top1 + top2)
    gs = jnp.concatenate(gscore, axis=0)
    gidx = lax.broadcasted_iota(jnp.int32, gs.shape, 0)
    beaten = jnp.zeros(gs.shape, jnp.int32)
    for m in range(N_GROUPS):
        gm = gs[m:m + 1, :]
        beaten = beaten + jnp.where((gm > gs) | ((gm == gs) & (gidx > m)), 1, 0)
    gkeep = beaten < TOPK_GROUPS
    masked = jnp.concatenate(
        [jnp.where(gkeep[g:g + 1, :], choice[g * GROUP_SIZE:(g + 1) * GROUP_SIZE, :], -jnp.inf)
         for g in range(N_GROUPS)], axis=0)

    eio = lax.broadcasted_iota(jnp.int32, (ne, nt), 0)
    picked = jnp.zeros((ne, nt), F32)
    idx_rows, w_rows, hits = [], [], []
    for _ in range(TOP_K):
        mx = jnp.max(masked, axis=0, keepdims=True)
        idx = jnp.min(jnp.where(masked == mx, eio, ne), axis=0, keepdims=True)
        hit = eio == idx
        w_rows.append(jnp.sum(jnp.where(hit, s, 0.0), axis=0, keepdims=True))
        idx_rows.append(idx)
        hits.append(hit)
        masked = jnp.where(hit, -jnp.inf, masked)
        picked = jnp.where(hit, 1.0, picked)
    wk = jnp.concatenate(w_rows, axis=0)
    w_ref[...] = wk / jnp.sum(wk, axis=0, keepdims=True) * ROUTED_SCALE

    tr = lax.broadcasted_iota(jnp.int32, (nt, nt), 0)
    tc = lax.broadcasted_iota(jnp.int32, (nt, nt), 1)
    before = jnp.where(tr < tc, 1.0, 0.0).astype(BF16)
    pb = picked.astype(BF16)
    pos = carry_ref[...] + jnp.dot(pb, before, preferred_element_type=F32)
    rank = jnp.concatenate(
        [jnp.sum(jnp.where(hit, pos, 0.0), axis=0, keepdims=True) for hit in hits], axis=0).astype(jnp.int32)
    code_ref[...] = jnp.concatenate(idx_rows, axis=0) * SLOT_CODE_BASE + rank
    total = carry_ref[...] + jnp.dot(pb, jnp.ones((nt, nt), BF16), preferred_element_type=F32)
    carry_ref[...] = total
    cnt_ref[...] = total


def _route(scores_t, eb):
    ne, t = scores_t.shape
    assert t <= SLOT_CODE_BASE
    nt = ROUTE_TILE
    tok = lambda dt: jax.ShapeDtypeStruct((TOP_K, t), dt)
    return pl.pallas_call(
        _route_kernel,
        grid=(t // nt,),
        in_specs=[pl.BlockSpec((ne, nt), lambda i: (0, i)),
                  pl.BlockSpec((ne, nt), lambda i: (0, 0))],
        out_specs=[pl.BlockSpec((TOP_K, nt), lambda i: (0, i)),
                   pl.BlockSpec((TOP_K, nt), lambda i: (0, i)),
                   pl.BlockSpec((ne, nt), lambda i: (0, 0))],
        out_shape=[tok(jnp.int32), tok(F32), jax.ShapeDtypeStruct((ne, nt), F32)],
        scratch_shapes=[pltpu.VMEM((ne, nt), F32)],
        compiler_params=_params(1),
        name="route",
    )(scores_t, eb)


SLOT_TILE = 2048


def _slots_kernel(pstart_ref, code_ref, o_ref):
    code = code_ref[...]
    expert = lax.shift_right_logical(code, SLOT_CODE_SHIFT)

    def body(e, acc):
        return jnp.where(expert == e, pstart_ref[e], acc)

    start = lax.fori_loop(0, N_EXPERTS, body, jnp.zeros_like(code), unroll=8)
    o_ref[...] = start + (code & (SLOT_CODE_BASE - 1))


def _slots(pstart, code_t):
    k, t = code_t.shape
    return pl.pallas_call(
        _slots_kernel,
        grid_spec=pltpu.PrefetchScalarGridSpec(
            num_scalar_prefetch=1,
            grid=(t // SLOT_TILE,),
            in_specs=[pl.BlockSpec((k, SLOT_TILE), lambda i, p: (0, i))],
            out_specs=pl.BlockSpec((k, SLOT_TILE), lambda i, p: (0, i)),
        ),
        out_shape=jax.ShapeDtypeStruct((k, t), jnp.int32),
        compiler_params=_params(1),
        name="slots",
    )(pstart, code_t)


SC_WINDOW = 128


def _sc_gather_rows(table, idx_flat):
    from jax.experimental.pallas import tpu_sc as plsc
    info = plsc.get_sparse_core_info()
    nw = info.num_cores * info.num_subcores
    n = idx_flat.shape[0]
    width = table.shape[1]
    per_worker = n // nw
    assert per_worker * nw == n and per_worker % SC_WINDOW == 0
    mesh = plsc.VectorSubcoreMesh(core_axis_name="c", subcore_axis_name="s")

    def body(table_hbm, idx_hbm, out_hbm, idx_v, rows_v, sem):
        wid = lax.axis_index("s") * info.num_cores + lax.axis_index("c")
        base = wid * per_worker

        @pl.loop(0, per_worker // SC_WINDOW)
        def _(w):
            off = pl.multiple_of(base + w * SC_WINDOW, SC_WINDOW)
            pltpu.sync_copy(idx_hbm.at[pl.ds(off, SC_WINDOW)], idx_v)
            pltpu.async_copy(table_hbm.at[idx_v], rows_v, sem).wait()
            pltpu.sync_copy(rows_v, out_hbm.at[pl.ds(off, SC_WINDOW)])

    return pl.kernel(
        body,
        out_type=jax.ShapeDtypeStruct((n, width), table.dtype),
        mesh=mesh,
        scratch_types=[pltpu.VMEM((SC_WINDOW,), jnp.int32),
                       pltpu.VMEM((SC_WINDOW, width), table.dtype),
                       pltpu.SemaphoreType.DMA],
        name="sc_gather",
    )(table, idx_flat)


def _sc_scatter_rows(rows, idx_kt, n_out):
    from jax.experimental.pallas import tpu_sc as plsc
    info = plsc.get_sparse_core_info()
    nw = info.num_cores * info.num_subcores
    t, width = rows.shape
    nk = idx_kt.shape[0]
    per_worker = t // nw
    assert per_worker * nw == t and per_worker % SC_WINDOW == 0
    mesh = plsc.VectorSubcoreMesh(core_axis_name="c", subcore_axis_name="s")

    def body(rows_hbm, idx_hbm, out_hbm, idx_v, rows_v, sem):
        wid = lax.axis_index("s") * info.num_cores + lax.axis_index("c")
        base = wid * per_worker

        @pl.loop(0, per_worker // SC_WINDOW)
        def _(w):
            off = pl.multiple_of(base + w * SC_WINDOW, SC_WINDOW)
            pltpu.sync_copy(rows_hbm.at[pl.ds(off, SC_WINDOW)], rows_v)
            pltpu.sync_copy(idx_hbm.at[:, pl.ds(off, SC_WINDOW)], idx_v)
            copies = [pltpu.async_copy(rows_v, out_hbm.at[idx_v.at[k]], sem) for k in range(nk)]
            for cp in copies:
                cp.wait()

    return pl.kernel(
        body,
        out_type=jax.ShapeDtypeStruct((n_out, width), rows.dtype),
        mesh=mesh,
        scratch_types=[pltpu.VMEM((nk, SC_WINDOW), jnp.int32),
                       pltpu.VMEM((SC_WINDOW, width), rows.dtype),
                       pltpu.SemaphoreType.DMA],
        name="sc_scatter",
    )(rows, idx_kt)


def _expert_kernel(first_ref, count_ref, used_ref, w1_ref, w3_ref, w2_ref, xs_ref, ys_ref,
                   xbuf, ybuf, sem_in, sem_out, w1f, w3f, w2f, sem_w, w1b, w3b, w2b, *, layer):
    e = pl.program_id(0)
    ne = pl.num_programs(0)
    r = xbuf.shape[1]
    n_used = used_ref[0]

    def x_copy(g, slot):
        return pltpu.make_async_copy(xs_ref.at[pl.ds(pl.multiple_of(g * r, r), r), :], xbuf.at[slot], sem_in.at[slot])

    def y_copy(g, slot):
        return pltpu.make_async_copy(ybuf.at[slot], ys_ref.at[pl.ds(pl.multiple_of(g * r, r), r), :], sem_out.at[slot])

    def w_copies(ex, slot):
        return [pltpu.make_async_copy(src.at[layer, ex], dst.at[slot], sem_w.at[slot])
                for src, dst in ((w1_ref, w1f), (w3_ref, w3f), (w2_ref, w2f))]

    @pl.when(e == 0)
    def _first_reads():
        x_copy(0, 0).start()
        for cp in w_copies(0, 0):
            cp.start(priority=1)

    @pl.when(e + 1 < ne)
    def _next_weights():
        for cp in w_copies(e + 1, (e + 1) & 1):
            cp.start(priority=1)

    for cp in w_copies(e, e & 1):
        cp.wait()

    n = count_ref[e]

    @pl.when(n > 0)
    def _cast_weights():
        w1b[...] = w1f[e & 1].astype(BF16)
        w3b[...] = w3f[e & 1].astype(BF16)
        w2b[...] = w2f[e & 1].astype(BF16)

    def tile_body(g, carry):
        slot = g & 1
        x_copy(g, slot).wait()

        @pl.when(g + 1 < n_used)
        def _read_next():
            x_copy(g + 1, 1 - slot).start()

        @pl.when(g >= 2)
        def _free_out_slot():
            y_copy(g - 2, slot).wait()

        lo, hi = _unpack_rows(xbuf[slot])
        lo, hi = lo.astype(BF16), hi.astype(BF16)
        a = (jnp.dot(lo, w1b[:HALF, :], preferred_element_type=F32)
             + jnp.dot(hi, w1b[HALF:, :], preferred_element_type=F32))
        u = (jnp.dot(lo, w3b[:HALF, :], preferred_element_type=F32)
             + jnp.dot(hi, w3b[HALF:, :], preferred_element_type=F32))
        y = jnp.dot((_silu(a) * u).astype(BF16), w2b[...], preferred_element_type=F32)
        ybuf[slot] = _pack_rows(y)
        y_copy(g, slot).start()
        return carry

    g0 = first_ref[e]
    lax.fori_loop(g0, g0 + n, tile_body, 0)

    @pl.when(e == pl.num_programs(0) - 1)
    def _drain_writes():
        @pl.when(n_used >= 2)
        def _():
            y_copy(n_used - 2, n_used & 1).wait()
        y_copy(n_used - 1, (n_used - 1) & 1).wait()


def _experts(tile_first, tile_count, n_used, xs, w1, w3, w2, layer):
    n_rows = xs.shape[0]
    r = EXPERT_TILE
    any_spec = pl.BlockSpec(memory_space=pl.ANY)
    return pl.pallas_call(
        functools.partial(_expert_kernel, layer=layer),
        grid_spec=pltpu.PrefetchScalarGridSpec(
            num_scalar_prefetch=3,
            grid=(N_EXPERTS,),
            in_specs=[any_spec, any_spec, any_spec, any_spec],
            out_specs=any_spec,
            scratch_shapes=[pltpu.VMEM((2, r, HALF), jnp.uint32),
                            pltpu.VMEM((2, r, HALF), jnp.uint32),
                            pltpu.SemaphoreType.DMA((2,)),
                            pltpu.SemaphoreType.DMA((2,)),
                            pltpu.VMEM((2, D_MODEL, EXPERT_FF), F32),
                            pltpu.VMEM((2, D_MODEL, EXPERT_FF), F32),
                            pltpu.VMEM((2, EXPERT_FF, D_MODEL), F32),
                            pltpu.SemaphoreType.DMA((2,)),
                            pltpu.VMEM((D_MODEL, EXPERT_FF), BF16),
                            pltpu.VMEM((D_MODEL, EXPERT_FF), BF16),
                            pltpu.VMEM((EXPERT_FF, D_MODEL), BF16)],
        ),
        out_shape=jax.ShapeDtypeStruct((n_rows, HALF), jnp.uint32),
        compiler_params=_params(1),
        name="experts",
    )(tile_first, tile_count, n_used, w1, w3, w2, xs)


def _combine_dense_kernel(base_ref, g2_ref, w_ref, yg_ref, o_ref):
    acc_lo = acc_hi = None
    for k in range(TOP_K):
        lo, hi = _unpack_rows(yg_ref[k])
        wk = w_ref[:, k:k + 1]
        acc_lo = wk * lo if acc_lo is None else acc_lo + wk * lo
        acc_hi = wk * hi if acc_hi is None else acc_hi + wk * hi
    o_ref[:, :HALF] = base_ref[:, :HALF] + g2_ref[:, :HALF] * acc_lo
    o_ref[:, HALF:] = base_ref[:, HALF:] + g2_ref[:, HALF:] * acc_hi


def _combine_dense(base, g2, w_tok, yg, seq):
    t = base.shape[0]
    nt = ROUTE_TILE
    tpb = seq // nt
    return pl.pallas_call(
        _combine_dense_kernel,
        grid=(t // nt,),
        in_specs=[pl.BlockSpec((nt, D_MODEL), lambda i: (i, 0)),
                  pl.BlockSpec((None, 1, D_MODEL), lambda i: (i // tpb, 0, 0)),
                  pl.BlockSpec((nt, TOP_K), lambda i: (i, 0)),
                  pl.BlockSpec((TOP_K, nt, HALF), lambda i: (0, i, 0))],
        out_specs=pl.BlockSpec((nt, D_MODEL), lambda i: (i, 0)),
        out_shape=jax.ShapeDtypeStruct((t, D_MODEL), F32),
        compiler_params=_params(1),
        name="combine_dense",
    )(base, g2, w_tok, yg)


def _layer(layer, x, c, w_ada, b_ada, norm1_w, norm2_w, w_in, q_norm_w, k_norm_w, rel_bias, w_alpha, b_alpha,
           moba_out_w, gla_out_w, w_out, w_router, e_bias, w1, w3, w2, ws1, ws3, ws2):
    b, s, d = x.shape
    t = b * s
    x2 = x.reshape(t, d)

    mod = _mod(c, w_ada, b_ada)
    sh1, sc1, g1, sh2, sc2, g2 = [mod[:, j * d:(j + 1) * d].reshape(b, 1, d) for j in range(6)]

    w_main = w_in[:, :D_MAIN].astype(BF16)
    w_ga = jnp.zeros((d, LANES), BF16).at[:, :GLA_GATE_RANK].set(w_in[:, D_MAIN:].astype(BF16))
    per_chunk = 256 // MOBA_HEAD_DIM
    qw = jnp.tile(q_norm_w.astype(F32), per_chunk).reshape(1, 256) * (MOBA_HEAD_DIM ** -0.5)
    kw = jnp.tile(k_norm_w.astype(F32), per_chunk).reshape(1, 256)
    proj, ga = _inproj(x2, sc1, sh1, norm1_w.reshape(1, d), w_main, w_ga, qw, kw, s)
    proj3 = proj.reshape(b, s, D_MAIN)

    near, far = _moba_bias_tables(rel_bias)
    ow = jnp.tile(moba_out_w.astype(F32), 2).reshape(1, LANES)
    o_a = _moba(proj3, near, far, ow)

    wal = jnp.zeros((LANES, GLA_KEY_WIDTH), F32).at[:GLA_GATE_RANK].set(w_alpha)
    o_b = _gla(proj3, ga.reshape(b, s, LANES), wal, b_alpha.reshape(1, GLA_KEY_WIDTH),
               gla_out_w.reshape(1, GLA_DV))

    base, h2, scores_t = _outproj(
        o_a.reshape(t, MOBA_WIDTH), o_b.reshape(t, GLA_WIDTH), x2, g1, sc2, sh2, g2,
        norm2_w.reshape(1, d), w_out.astype(BF16), ws1.astype(BF16), ws3.astype(BF16), ws2.astype(BF16),
        w_router.T.astype(BF16), s)

    eb = jnp.broadcast_to(e_bias.astype(F32)[:, None], (N_EXPERTS, ROUTE_TILE))
    code_t, w_t, counts = _route(scores_t, eb)

    r = EXPERT_TILE
    n_tiles = (t * TOP_K + N_EXPERTS * (r - 1) + r - 1) // r
    n_rows = n_tiles * r
    cnt = counts[:, 0].astype(jnp.int32)
    padded = (cnt + r - 1) // r * r
    pend = jnp.cumsum(padded)
    pstart = pend - padded
    n_used = (pend[-1:] // r).astype(jnp.int32)
    dest_t = _slots(pstart, code_t)

    xs = _sc_scatter_rows(h2, dest_t, n_rows)
    ys = _experts(pstart // r, padded // r, n_used, xs, w1, w3, w2, layer)
    yg = _sc_gather_rows(ys, dest_t.reshape(TOP_K * t)).reshape(TOP_K, t, HALF)
    out = _combine_dense(base, g2, w_t.T, yg, s)
    return out.reshape(b, s, d)


def kernel(x, c, w_ada, b_ada, norm1_w, norm2_w, w_in, q_norm_w, k_norm_w, rel_bias, w_alpha, b_alpha,
           moba_out_w, gla_out_w, w_out, w_router, e_bias, w1, w3, w2, ws1, ws3, ws2):
    for l in range(w_ada.shape[0]):
        x = _layer(l, x, c, w_ada[l], b_ada[l], norm1_w[l], norm2_w[l], w_in[l], q_norm_w[l], k_norm_w[l],
                   rel_bias, w_alpha[l], b_alpha[l], moba_out_w[l], gla_out_w[l], w_out[l], w_router[l],
                   e_bias[l], w1, w3, w2, ws1[l], ws3[l], ws2[l])
    return x
```

```python
import functools
import math

import numpy as np
import jax
import jax.numpy as jnp
from jax import lax
from jax.experimental import pallas as pl
from jax.experimental.pallas import tpu as pltpu

D_MODEL = 1024
MOBA_HEADS = 8
MOBA_HEAD_DIM = 64
MOBA_WIDTH = MOBA_HEADS * MOBA_HEAD_DIM
MOBA_BLOCK = 256
MOBA_TOPK = 3
GLA_HEADS = 4
GLA_DK = 64
GLA_DV = 128
GLA_KEY_WIDTH = GLA_HEADS * GLA_DK
GLA_WIDTH = GLA_HEADS * GLA_DV
GLA_GATE_RANK = 16
GLA_GATE_TAU = 16.0
GLA_CHUNK = 64
REL_BUCKETS = 32
REL_MAX_DIST = 128
N_EXPERTS = 256
TOP_K = 8
N_GROUPS = 8
TOPK_GROUPS = 4
GROUP_SIZE = N_EXPERTS // N_GROUPS
EXPERT_FF = 256
SHARED_FF = 256
ROUTED_SCALE = 2.5
NORM_EPS = 1e-6

D_MAIN = 3 * MOBA_WIDTH + 2 * GLA_KEY_WIDTH + 2 * GLA_WIDTH
LANES = 128
VMEM_LIMIT = 56 * 1024 * 1024

ROW_TILE = 512
ROUTE_TILE = 256
EXPERT_TILE = 256

F32 = jnp.float32
BF16 = jnp.bfloat16
NT_DIMS = (((1,), (1,)), ((), ()))
TN_DIMS = (((0,), (0,)), ((), ()))


def _params(n_axes):
    return pltpu.CompilerParams(dimension_semantics=("arbitrary",) * n_axes,
                                vmem_limit_bytes=VMEM_LIMIT)


def _silu(v):
    return v * jax.nn.sigmoid(v)


def _mod_kernel(c_ref, w_ref, b_ref, o_ref):
    o_ref[...] = jnp.dot(_silu(c_ref[...]), w_ref[...], preferred_element_type=F32) + b_ref[...]


def _mod(c, w, b):
    rows = 8
    cp = jnp.zeros((rows, D_MODEL), F32).at[:c.shape[0]].set(c)
    n = w.shape[1]
    tn = 1024
    out = pl.pallas_call(
        _mod_kernel,
        grid=(n // tn,),
        in_specs=[pl.BlockSpec((rows, D_MODEL), lambda j: (0, 0)),
                  pl.BlockSpec((D_MODEL, tn), lambda j: (0, j)),
                  pl.BlockSpec((1, tn), lambda j: (0, j))],
        out_specs=pl.BlockSpec((rows, tn), lambda j: (0, j)),
        out_shape=jax.ShapeDtypeStruct((rows, n), F32),
        compiler_params=_params(1),
        name="mod",
    )(cp, w, b.reshape(1, n))
    return out[:c.shape[0]]


def _group_rms_inv(a, group):
    lane = lax.broadcasted_iota(jnp.int32, (1, a.shape[1]), 1)
    a2 = a * a
    inv = jnp.zeros_like(a)
    for g in range(a.shape[1] // group):
        m = (lane >= g * group) & (lane < (g + 1) * group)
        ss = jnp.sum(jnp.where(m, a2, 0.0), axis=-1, keepdims=True)
        inv = jnp.where(m, lax.rsqrt(ss * (1.0 / group) + NORM_EPS), inv)
    return inv


def _inproj_kernel(x_ref, sc_ref, sh_ref, nw_ref, w_ref, wga_ref, qw_ref, kw_ref, o_ref, ga_ref):
    x = x_ref[...]
    ms = jnp.mean(x * x, axis=-1, keepdims=True)
    h = x * lax.rsqrt(ms + NORM_EPS) * nw_ref[...]
    h = h * (1.0 + sc_ref[...]) + sh_ref[...]
    hb = h.astype(BF16)
    cw = 256
    for j in range(D_MAIN // cw):
        acc = jnp.dot(hb, w_ref[:, j * cw:(j + 1) * cw], preferred_element_type=F32)
        if j < 2 * MOBA_WIDTH // cw:
            nw = qw_ref if j < MOBA_WIDTH // cw else kw_ref
            acc = acc * _group_rms_inv(acc, MOBA_HEAD_DIM) * nw[...]
        o_ref[:, j * cw:(j + 1) * cw] = acc.astype(BF16)
    ga_ref[...] = jnp.dot(hb, wga_ref[...], preferred_element_type=F32)


def _inproj(x2, sc, sh, nw, w_main, w_ga, qw, kw, seq):
    t = x2.shape[0]
    tpb = seq // ROW_TILE
    vec = lambda: pl.BlockSpec((None, 1, D_MODEL), lambda i: (i // tpb, 0, 0))
    full = lambda a: pl.BlockSpec(a.shape, lambda i: (0,) * a.ndim)
    return pl.pallas_call(
        _inproj_kernel,
        grid=(t // ROW_TILE,),
        in_specs=[pl.BlockSpec((ROW_TILE, D_MODEL), lambda i: (i, 0)), vec(), vec(),
                  full(nw), full(w_main), full(w_ga), full(qw), full(kw)],
        out_specs=[pl.BlockSpec((ROW_TILE, D_MAIN), lambda i: (i, 0)),
                   pl.BlockSpec((ROW_TILE, LANES), lambda i: (i, 0))],
        out_shape=[jax.ShapeDtypeStruct((t, D_MAIN), BF16),
                   jax.ShapeDtypeStruct((t, LANES), F32)],
        compiler_params=_params(1),
        name="inproj",
    )(x2, sc, sh, nw, w_main, w_ga, qw, kw)


def _t5_bucket_np(rel):
    max_exact = REL_BUCKETS // 2
    relf = np.maximum(rel, 1).astype(np.float64)
    large = max_exact + (np.log(relf / max_exact) / math.log(REL_MAX_DIST / max_exact)
                         * (REL_BUCKETS - max_exact)).astype(np.int32)
    large = np.minimum(large, REL_BUCKETS - 1)
    return np.where(rel < max_exact, rel, large)


def _bias_kernel(rb_ref, idx_ref, o_ref):
    h = pl.program_id(0)
    idx = idx_ref[...]
    tab = jnp.full(idx.shape, -jnp.inf, F32)
    for bk in range(REL_BUCKETS):
        tab = jnp.where(idx == bk, rb_ref[bk * MOBA_HEADS + h], tab)
    o_ref[...] = tab


def _moba_bias_tables(rel_bias):
    j = np.arange(MOBA_BLOCK)[:, None]
    i = np.arange(MOBA_BLOCK)[None, :]
    own_idx = np.where(j <= i, _t5_bucket_np(np.maximum(i - j, 0)), -1)
    prev_idx = _t5_bucket_np(MOBA_BLOCK + i - j)
    idx = jnp.asarray(np.concatenate([prev_idx, own_idx], axis=0).astype(np.int32))
    assert int(_t5_bucket_np(np.array([MOBA_BLOCK + 1]))[0]) == REL_BUCKETS - 1
    rb = rel_bias.astype(F32)
    near = pl.pallas_call(
        _bias_kernel,
        grid=(MOBA_HEADS,),
        in_specs=[pl.BlockSpec(memory_space=pltpu.SMEM),
                  pl.BlockSpec(idx.shape, lambda h: (0, 0))],
        out_specs=pl.BlockSpec((None,) + idx.shape, lambda h: (h, 0, 0)),
        out_shape=jax.ShapeDtypeStruct((MOBA_HEADS,) + idx.shape, F32),
        compiler_params=_params(1),
        name="bias",
    )(rb.reshape(-1), idx)
    return near, rb[REL_BUCKETS - 1]


FAR_GROUP = 4


def _moba_kernel(far_ref, q_ref, k_ref, v_ref, near_ref, ow_ref, o_ref,
                 vt_ref, vtg_ref, acc_ref, m_ref, sel_ref, s_ref):
    hp = pl.program_id(1)
    i = pl.program_id(2)
    nblk = k_ref.shape[0] // MOBA_BLOCK
    ngrp = nblk // FAR_GROUP
    hd = MOBA_HEAD_DIM
    bs = MOBA_BLOCK
    lane = lax.broadcasted_iota(jnp.int32, (bs, LANES), 1)

    def split_heads(qb):
        zero = jnp.zeros_like(qb)
        return jnp.where(lane < hd, qb, zero), jnp.where(lane < hd, zero, qb)

    @pl.when(i == 0)
    def _prepare():
        row = lax.broadcasted_iota(jnp.int32, (LANES, bs), 0)
        kmeans = []
        for n in range(nblk):
            kb = k_ref[n * bs:(n + 1) * bs, :].astype(F32)
            kmeans.append(jnp.mean(kb, axis=0, keepdims=True))
            vt = v_ref[n * bs:(n + 1) * bs, :].astype(F32).T
            vt0 = jnp.where(row < hd, vt, 1.0).astype(BF16)
            vt1 = jnp.where(row < hd, 1.0, vt).astype(BF16)
            vt_ref[0, n] = vt0
            vt_ref[1, n] = vt1
            gcols = slice((n % FAR_GROUP) * bs, (n % FAR_GROUP + 1) * bs)
            vtg_ref[0, n // FAR_GROUP, :, gcols] = vt0
            vtg_ref[1, n // FAR_GROUP, :, gcols] = vt1
        kmean = jnp.concatenate(kmeans, axis=0)
        km_hi = kmean.astype(BF16)
        km_lo = (kmean - km_hi.astype(F32)).astype(BF16)
        blk = lax.broadcasted_iota(jnp.int32, (nblk, bs), 0)
        for ib in range(nblk):
            qparts = split_heads(q_ref[ib * bs:(ib + 1) * bs, :])
            for h in range(2):
                gt = (lax.dot_general(km_hi, qparts[h], NT_DIMS, preferred_element_type=F32)
                      + lax.dot_general(km_lo, qparts[h], NT_DIMS, preferred_element_type=F32))
                gt = jnp.where(blk < ib, gt, -jnp.inf)
                cnt = jnp.zeros(gt.shape, jnp.int32)
                for m in range(ib):
                    gm = gt[m:m + 1, :]
                    cnt = cnt + jnp.where((gm > gt) | ((gm == gt) & (blk > m)), 1, 0)
                keep = (blk < ib) & (cnt < MOBA_TOPK)
                sel_ref[0, h, ib] = jnp.where(keep, 1.0, 0.0)
                sel_ref[1, h, ib] = jnp.where(keep & (blk < ib - 1), 1.0, 0.0)

    qh = split_heads(q_ref[pl.ds(pl.multiple_of(i * bs, bs), bs), :])

    @pl.when(i == 0)
    def _own_block_only():
        kb = k_ref[0:bs, :]
        for h in range(2):
            s = lax.dot_general(kb, qh[h], NT_DIMS, preferred_element_type=F32) + near_ref[h, bs:2 * bs, :]
            m_new = jnp.max(s, axis=0, keepdims=True)
            p = jnp.exp(s - m_new).astype(BF16)
            acc_ref[h] = jnp.dot(vt_ref[h, 0], p, preferred_element_type=F32)
            m_ref[h] = m_new

    @pl.when(i >= 1)
    def _previous_and_own_block():
        kbs = (k_ref[pl.ds(pl.multiple_of((i - 1) * bs, bs), bs), :],
               k_ref[pl.ds(pl.multiple_of(i * bs, bs), bs), :])
        ss = [[lax.dot_general(kbs[w], qh[h], NT_DIMS, preferred_element_type=F32)
               + near_ref[h, w * bs:(w + 1) * bs, :] for w in range(2)] for h in range(2)]
        ps, ms = [], []
        for h in range(2):
            s_prev, s_own = ss[h]
            keep = sel_ref[0, h, i, pl.ds(i - 1, 1), :] > 0.5
            mx = jnp.where(keep, jnp.max(s_prev, axis=0, keepdims=True), -jnp.inf)
            m_new = jnp.maximum(jnp.max(s_own, axis=0, keepdims=True), mx)
            ps.append((jnp.exp(s_prev - jnp.where(keep, m_new, jnp.inf)).astype(BF16),
                       jnp.exp(s_own - m_new).astype(BF16)))
            ms.append(m_new)
        for h in range(2):
            acc_ref[h] = (jnp.dot(vt_ref[h, i - 1], ps[h][0], preferred_element_type=F32)
                          + jnp.dot(vt_ref[h, i], ps[h][1], preferred_element_type=F32))
            m_ref[h] = ms[h]

    n_far = (i + FAR_GROUP - 2) // FAR_GROUP
    gk = FAR_GROUP * bs

    def far_scores(g):
        kb = k_ref[g * gk:(g + 1) * gk, :]
        for h in range(2):
            s_ref[g % 2, h] = lax.dot_general(kb, qh[h], NT_DIMS, preferred_element_type=F32)

    @pl.when(n_far > 0)
    def _first_far_scores():
        far_scores(0)

    for g in range(ngrp):
        @pl.when(g < n_far)
        def _far_group(g=g):
            if g + 1 < ngrp:
                far_scores(g + 1)
            for h in range(2):
                fb = far_ref[2 * hp + h]
                m_old = m_ref[h]
                m_new = m_old
                keeps = []
                for j in range(FAR_GROUP):
                    keep = sel_ref[1, h, i, pl.ds(g * FAR_GROUP + j, 1), :] > 0.5
                    mx = jnp.max(s_ref[g % 2, h, j * bs:(j + 1) * bs, :], axis=0, keepdims=True) + fb
                    m_new = jnp.maximum(m_new, jnp.where(keep, mx, -jnp.inf))
                    keeps.append(keep)
                p = jnp.concatenate(
                    [jnp.exp(s_ref[g % 2, h, j * bs:(j + 1) * bs, :]
                             - jnp.where(keeps[j], m_new - fb, jnp.inf)).astype(BF16)
                     for j in range(FAR_GROUP)], axis=0)
                pv = jnp.dot(vtg_ref[h, g], p, preferred_element_type=F32)
                acc_ref[h] = acc_ref[h] * jnp.exp(m_old - m_new) + pv
                m_ref[h] = m_new

    a0 = acc_ref[0]
    a1 = acc_ref[1]
    row = lax.broadcasted_iota(jnp.int32, a0.shape, 0)
    ot = jnp.where(row < hd, a0 / a0[hd:hd + 1, :], a1 / a1[0:1, :])
    o2 = ot * ot
    ss0 = jnp.sum(jnp.where(row < hd, o2, 0.0), axis=0, keepdims=True)
    ss1 = jnp.sum(jnp.where(row < hd, 0.0, o2), axis=0, keepdims=True)
    inv = jnp.where(row < hd, lax.rsqrt(ss0 * (1.0 / hd) + NORM_EPS), lax.rsqrt(ss1 * (1.0 / hd) + NORM_EPS))
    o_ref[...] = ((ot * inv).T * ow_ref[...]).astype(o_ref.dtype)


def _moba(proj3, near, far, ow):
    b, s, _ = proj3.shape
    nblk = s // MOBA_BLOCK
    assert nblk % FAR_GROUP == 0
    npair = MOBA_HEADS // 2
    kcol = MOBA_WIDTH // LANES
    return pl.pallas_call(
        _moba_kernel,
        grid=(b, npair, nblk),
        in_specs=[pl.BlockSpec(memory_space=pltpu.SMEM),
                  pl.BlockSpec((None, s, LANES), lambda bb, hp, i: (bb, 0, hp)),
                  pl.BlockSpec((None, s, LANES), lambda bb, hp, i: (bb, 0, kcol + hp)),
                  pl.BlockSpec((None, s, LANES), lambda bb, hp, i: (bb, 0, 2 * kcol + hp)),
                  pl.BlockSpec((2, 2 * MOBA_BLOCK, MOBA_BLOCK), lambda bb, hp, i: (hp, 0, 0)),
                  pl.BlockSpec((1, LANES), lambda bb, hp, i: (0, 0))],
        out_specs=pl.BlockSpec((None, MOBA_BLOCK, LANES), lambda bb, hp, i: (bb, i, hp)),
        out_shape=jax.ShapeDtypeStruct((b, s, MOBA_WIDTH), BF16),
        scratch_shapes=[pltpu.VMEM((2, nblk, LANES, MOBA_BLOCK), BF16),
                        pltpu.VMEM((2, nblk // FAR_GROUP, LANES, FAR_GROUP * MOBA_BLOCK), BF16),
                        pltpu.VMEM((2, LANES, MOBA_BLOCK), F32),
                        pltpu.VMEM((2, 1, MOBA_BLOCK), F32),
                        pltpu.VMEM((2, 2, nblk, nblk, MOBA_BLOCK), F32),
                        pltpu.VMEM((2, 2, FAR_GROUP * MOBA_BLOCK, MOBA_BLOCK), F32)],
        compiler_params=_params(3),
        name="moba",
    )(far, proj3, proj3, proj3, near, ow)


def _split3(v):
    hi = v.astype(BF16)
    r1 = v - hi.astype(F32)
    mid = r1.astype(BF16)
    lo = (r1 - mid.astype(F32)).astype(BF16)
    return hi, mid, lo


GLA_UNROLL = 4


def _gla_kernel(q_ref, k_ref, v_ref, g_ref, ga_ref, wal_ref, bal_ref, gw_ref, o_ref, b_ref, st_ref):
    seq = q_ref.shape[0]
    c = GLA_CHUNK
    pc = 256

    rr = lax.broadcasted_iota(jnp.int32, (pc, pc), 0)
    cc = lax.broadcasted_iota(jnp.int32, (pc, pc), 1)
    tri = jnp.where((rr >= cc) & (rr // c == cc // c), 1.0, 0.0).astype(BF16)

    def decay_body(j, carry):
        rows = [pl.ds(pl.multiple_of((j * GLA_UNROLL + u) * pc, pc), pc) for u in range(GLA_UNROLL)]
        xg = [jnp.dot(ga_ref[r, :], wal_ref[...], preferred_element_type=F32) + bal_ref[...] for r in rows]
        parts = [_split3((jnp.minimum(x, 0.0) - jnp.log(1.0 + jnp.exp(-jnp.abs(x)))) * (1.0 / GLA_GATE_TAU))
                 for x in xg]
        sums = [[jnp.dot(tri, term, preferred_element_type=F32) for term in p] for p in parts]
        for r, (hi, mid, lo) in zip(rows, sums):
            b_ref[r, :] = hi + mid + lo
        return carry

    lax.fori_loop(0, seq // (pc * GLA_UNROLL), decay_body, 0)

    st_ref[...] = jnp.zeros_like(st_ref)
    lane = lax.broadcasted_iota(jnp.int32, (c, LANES), 1)
    head_mask = (lane < GLA_DK, lane >= GLA_DK)
    causal = lax.broadcasted_iota(jnp.int32, (c, c), 0) >= lax.broadcasted_iota(jnp.int32, (c, c), 1)

    units = [(u, h) for u in range(GLA_UNROLL) for h in range(2)]

    def chunk_body(ci, carry):
        rows = [pl.ds(pl.multiple_of((ci * GLA_UNROLL + u) * c, c), c) for u in range(GLA_UNROLL)]
        qt, kt, qs, ke, e_last = [], [], [], [], []
        for u in range(GLA_UNROLL):
            b = b_ref[rows[u], :]
            ref_row = b[c // 2 - 1:c // 2, :]
            last = b[c - 1:c, :]
            q = q_ref[rows[u], :].astype(F32) * (GLA_DK ** -0.5)
            k = k_ref[rows[u], :].astype(F32)
            qt.append(q * jnp.exp(b - ref_row))
            kt.append((k * jnp.exp(ref_row - b)).astype(BF16))
            qs.append(q * jnp.exp(b))
            ke.append((k * jnp.exp(last - b)).astype(BF16))
            e_last.append(jnp.exp(last))
        vs = {(u, h): v_ref[rows[u], h * GLA_DV:(h + 1) * GLA_DV] for u, h in units}
        a = {(u, h): lax.dot_general(jnp.where(head_mask[h], qt[u], 0.0).astype(BF16), kt[u], NT_DIMS,
                                     preferred_element_type=F32) for u, h in units}
        inc = {(u, h): lax.dot_general(vs[u, h], ke[u], TN_DIMS, preferred_element_type=F32) for u, h in units}
        o = {(u, h): jnp.dot(jnp.where(causal, a[u, h], 0.0).astype(BF16), vs[u, h], preferred_element_type=F32)
             for u, h in units}
        states = {}
        for h in range(2):
            st = st_ref[h]
            for u in range(GLA_UNROLL):
                states[u, h] = st
                st = st * e_last[u] + inc[u, h]
            st_ref[h] = st
        for u, h in units:
            cols = slice(h * GLA_DV, (h + 1) * GLA_DV)
            ou = o[u, h] + lax.dot_general(jnp.where(head_mask[h], qs[u], 0.0).astype(BF16),
                                           states[u, h].astype(BF16), NT_DIMS, preferred_element_type=F32)
            ms = jnp.mean(ou * ou, axis=-1, keepdims=True)
            on = ou * lax.rsqrt(ms + NORM_EPS) * gw_ref[...]
            g = g_ref[rows[u], cols].astype(F32)
            o_ref[rows[u], cols] = (on * _silu(g)).astype(o_ref.dtype)
        return carry

    lax.fori_loop(0, seq // (c * GLA_UNROLL), chunk_body, 0)


def _gla(proj3, ga3, wal, bal, gw):
    b, s, _ = proj3.shape
    npair = GLA_HEADS // 2
    qcol = 3 * MOBA_WIDTH // LANES
    kcol = qcol + GLA_KEY_WIDTH // LANES
    vcol = (3 * MOBA_WIDTH + 2 * GLA_KEY_WIDTH) // (2 * GLA_DV)
    gcol = vcol + npair
    return pl.pallas_call(
        _gla_kernel,
        grid=(b, npair),
        in_specs=[pl.BlockSpec((None, s, LANES), lambda bb, hp: (bb, 0, qcol + hp)),
                  pl.BlockSpec((None, s, LANES), lambda bb, hp: (bb, 0, kcol + hp)),
                  pl.BlockSpec((None, s, 2 * GLA_DV), lambda bb, hp: (bb, 0, vcol + hp)),
                  pl.BlockSpec((None, s, 2 * GLA_DV), lambda bb, hp: (bb, 0, gcol + hp)),
                  pl.BlockSpec((None, s, LANES), lambda bb, hp: (bb, 0, 0)),
                  pl.BlockSpec((LANES, LANES), lambda bb, hp: (0, hp)),
                  pl.BlockSpec((1, LANES), lambda bb, hp: (0, hp)),
                  pl.BlockSpec((1, GLA_DV), lambda bb, hp: (0, 0))],
        out_specs=pl.BlockSpec((None, s, 2 * GLA_DV), lambda bb, hp: (bb, 0, hp)),
        out_shape=jax.ShapeDtypeStruct((b, s, GLA_WIDTH), BF16),
        scratch_shapes=[pltpu.VMEM((s, LANES), F32),
                        pltpu.VMEM((2, GLA_DV, LANES), F32)],
        compiler_params=_params(2),
        name="gla",
    )(proj3, proj3, proj3, proj3, ga3, wal, bal, gw)


HALF = D_MODEL // 2


def _pack_rows(v):
    return pltpu.pack_elementwise([v[:, :HALF], v[:, HALF:]], packed_dtype=BF16)


def _unpack_rows(w):
    return (pltpu.unpack_elementwise(w, index=0, packed_dtype=BF16, unpacked_dtype=F32),
            pltpu.unpack_elementwise(w, index=1, packed_dtype=BF16, unpacked_dtype=F32))


def _outproj_kernel(oa_ref, ob_ref, x_ref, g1_ref, sc_ref, sh_ref, g2_ref, nw_ref, wo_ref,
                    ws1_ref, ws3_ref, ws2_ref, wrt_ref, base_ref, h_ref, st_ref):
    mix = (jnp.dot(oa_ref[...], wo_ref[:MOBA_WIDTH, :], preferred_element_type=F32)
           + jnp.dot(ob_ref[...], wo_ref[MOBA_WIDTH:, :], preferred_element_type=F32))
    x1 = x_ref[...] + g1_ref[...] * mix
    ms = jnp.mean(x1 * x1, axis=-1, keepdims=True)
    h = x1 * lax.rsqrt(ms + NORM_EPS) * nw_ref[...]
    h = h * (1.0 + sc_ref[...]) + sh_ref[...]
    h_ref[...] = _pack_rows(h)
    hb = h.astype(BF16)
    a = jnp.dot(hb, ws1_ref[...], preferred_element_type=F32)
    u = jnp.dot(hb, ws3_ref[...], preferred_element_type=F32)
    shared = jnp.dot((_silu(a) * u).astype(BF16), ws2_ref[...], preferred_element_type=F32)
    base_ref[...] = x1 + g2_ref[...] * shared
    logits_t = lax.dot_general(wrt_ref[...], hb, NT_DIMS, preferred_element_type=F32)
    st_ref[...] = jax.nn.sigmoid(logits_t)


def _outproj(oa, ob, x2, g1, sc, sh, g2, nw, wo, ws1, ws3, ws2, wrt, seq):
    t = x2.shape[0]
    tpb = seq // ROW_TILE
    vec = lambda: pl.BlockSpec((None, 1, D_MODEL), lambda i: (i // tpb, 0, 0))
    full = lambda a: pl.BlockSpec(a.shape, lambda i: (0,) * a.ndim)
    rows = lambda w: pl.BlockSpec((ROW_TILE, w), lambda i: (i, 0))
    return pl.pallas_call(
        _outproj_kernel,
        grid=(t // ROW_TILE,),
        in_specs=[rows(MOBA_WIDTH), rows(GLA_WIDTH), rows(D_MODEL), vec(), vec(), vec(), vec(),
                  full(nw), full(wo), full(ws1), full(ws3), full(ws2), full(wrt)],
        out_specs=[rows(D_MODEL), rows(HALF), pl.BlockSpec((N_EXPERTS, ROW_TILE), lambda i: (0, i))],
        out_shape=[jax.ShapeDtypeStruct((t, D_MODEL), F32),
                   jax.ShapeDtypeStruct((t, HALF), jnp.uint32),
                   jax.ShapeDtypeStruct((N_EXPERTS, t), F32)],
        compiler_params=_params(1),
        name="outproj",
    )(oa, ob, x2, g1, sc, sh, g2, nw, wo, ws1, ws3, ws2, wrt)


SLOT_CODE_SHIFT = 16
SLOT_CODE_BASE = 1 << SLOT_CODE_SHIFT


def _route_kernel(s_ref, eb_ref, code_ref, w_ref, cnt_ref, carry_ref):
    i = pl.program_id(0)
    ne, nt = s_ref.shape

    @pl.when(i == 0)
    def _init():
        carry_ref[...] = jnp.zeros_like(carry_ref)

    s = s_ref[...]
    choice = s + eb_ref[...]
    gio = lax.broadcasted_iota(jnp.int32, (GROUP_SIZE, nt), 0)
    gscore = []
    for g in range(N_GROUPS):
        cg = choice[g * GROUP_SIZE:(g + 1) * GROUP_SIZE, :]
        top1 = jnp.max(cg, axis=0, keepdims=True)
        first = jnp.min(jnp.where(cg == top1, gio, GROUP_SIZE), axis=0, keepdims=True)
        top2 = jnp.max(jnp.where(gio == first, -jnp.inf, cg), axis=0, keepdims=True)
        gscore.append(top1 + top2)
    gs = jnp.concatenate(gscore, axis=0)
    gidx = lax.broadcasted_iota(jnp.int32, gs.shape, 0)
    beaten = jnp.zeros(gs.shape, jnp.int32)
    for m in range(N_GROUPS):
        gm = gs[m:m + 1, :]
        beaten = beaten + jnp.where((gm > gs) | ((gm == gs) & (gidx > m)), 1, 0)
    gkeep = beaten < TOPK_GROUPS
    masked = jnp.concatenate(
        [jnp.where(gkeep[g:g + 1, :], choice[g * GROUP_SIZE:(g + 1) * GROUP_SIZE, :], -jnp.inf)
         for g in range(N_GROUPS)], axis=0)

    eio = lax.broadcasted_iota(jnp.int32, (ne, nt), 0)
    picked = jnp.zeros((ne, nt), F32)
    idx_rows, w_rows, hits = [], [], []
    for _ in range(TOP_K):
        mx = jnp.max(masked, axis=0, keepdims=True)
        idx = jnp.min(jnp.where(masked == mx, eio, ne), axis=0, keepdims=True)
        hit = eio == idx
        w_rows.append(jnp.sum(jnp.where(hit, s, 0.0), axis=0, keepdims=True))
        idx_rows.append(idx)
        hits.append(hit)
        masked = jnp.where(hit, -jnp.inf, masked)
        picked = jnp.where(hit, 1.0, picked)
    wk = jnp.concatenate(w_rows, axis=0)
    w_ref[...] = wk / jnp.sum(wk, axis=0, keepdims=True) * ROUTED_SCALE

    tr = lax.broadcasted_iota(jnp.int32, (nt, nt), 0)
    tc = lax.broadcasted_iota(jnp.int32, (nt, nt), 1)
    before = jnp.where(tr < tc, 1.0, 0.0).astype(BF16)
    pb = picked.astype(BF16)
    pos = carry_ref[...] + jnp.dot(pb, before, preferred_element_type=F32)
    rank = jnp.concatenate(
        [jnp.sum(jnp.where(hit, pos, 0.0), axis=0, keepdims=True) for hit in hits], axis=0).astype(jnp.int32)
    code_ref[...] = jnp.concatenate(idx_rows, axis=0) * SLOT_CODE_BASE + rank
    total = carry_ref[...] + jnp.dot(pb, jnp.ones((nt, nt), BF16), preferred_element_type=F32)
    carry_ref[...] = total
    cnt_ref[...] = total


def _route(scores_t, eb):
    ne, t = scores_t.shape
    assert t <= SLOT_CODE_BASE
    nt = ROUTE_TILE
    tok = lambda dt: jax.ShapeDtypeStruct((TOP_K, t), dt)
    return pl.pallas_call(
        _route_kernel,
        grid=(t // nt,),
        in_specs=[pl.BlockSpec((ne, nt), lambda i: (0, i)),
                  pl.BlockSpec((ne, nt), lambda i: (0, 0))],
        out_specs=[pl.BlockSpec((TOP_K, nt), lambda i: (0, i)),
                   pl.BlockSpec((TOP_K, nt), lambda i: (0, i)),
                   pl.BlockSpec((ne, nt), lambda i: (0, 0))],
        out_shape=[tok(jnp.int32), tok(F32), jax.ShapeDtypeStruct((ne, nt), F32)],
        scratch_shapes=[pltpu.VMEM((ne, nt), F32)],
        compiler_params=_params(1),
        name="route",
    )(scores_t, eb)


SLOT_TILE = 2048


def _slots_kernel(pstart_ref, code_ref, o_ref):
    code = code_ref[...]
    expert = lax.shift_right_logical(code, SLOT_CODE_SHIFT)

    def body(e, acc):
        return jnp.where(expert == e, pstart_ref[e], acc)

    start = lax.fori_loop(0, N_EXPERTS, body, jnp.zeros_like(code), unroll=8)
    o_ref[...] = start + (code & (SLOT_CODE_BASE - 1))


def _slots(pstart, code_t):
    k, t = code_t.shape
    return pl.pallas_call(
        _slots_kernel,
        grid_spec=pltpu.PrefetchScalarGridSpec(
            num_scalar_prefetch=1,
            grid=(t // SLOT_TILE,),
            in_specs=[pl.BlockSpec((k, SLOT_TILE), lambda i, p: (0, i))],
            out_specs=pl.BlockSpec((k, SLOT_TILE), lambda i, p: (0, i)),
        ),
        out_shape=jax.ShapeDtypeStruct((k, t), jnp.int32),
        compiler_params=_params(1),
        name="slots",
    )(pstart, code_t)


SC_WINDOW = 128


def _sc_gather_rows(table, idx_flat):
    from jax.experimental.pallas import tpu_sc as plsc
    info = plsc.get_sparse_core_info()
    nw = info.num_cores * info.num_subcores
    n = idx_flat.shape[0]
    width = table.shape[1]
    per_worker = n // nw
    assert per_worker * nw == n and per_worker % SC_WINDOW == 0
    mesh = plsc.VectorSubcoreMesh(core_axis_name="c", subcore_axis_name="s")

    def body(table_hbm, idx_hbm, out_hbm, idx_v, rows_v, sem):
        wid = lax.axis_index("s") * info.num_cores + lax.axis_index("c")
        base = wid * per_worker

        @pl.loop(0, per_worker // SC_WINDOW)
        def _(w):
            off = pl.multiple_of(base + w * SC_WINDOW, SC_WINDOW)
            pltpu.sync_copy(idx_hbm.at[pl.ds(off, SC_WINDOW)], idx_v)
            pltpu.async_copy(table_hbm.at[idx_v], rows_v, sem).wait()
            pltpu.sync_copy(rows_v, out_hbm.at[pl.ds(off, SC_WINDOW)])

    return pl.kernel(
        body,
        out_type=jax.ShapeDtypeStruct((n, width), table.dtype),
        mesh=mesh,
        scratch_types=[pltpu.VMEM((SC_WINDOW,), jnp.int32),
                       pltpu.VMEM((SC_WINDOW, width), table.dtype),
                       pltpu.SemaphoreType.DMA],
        name="sc_gather",
    )(table, idx_flat)


def _sc_scatter_rows(rows, idx_kt, n_out):
    from jax.experimental.pallas import tpu_sc as plsc
    info = plsc.get_sparse_core_info()
    nw = info.num_cores * info.num_subcores
    t, width = rows.shape
    nk = idx_kt.shape[0]
    per_worker = t // nw
    assert per_worker * nw == t and per_worker % SC_WINDOW == 0
    mesh = plsc.VectorSubcoreMesh(core_axis_name="c", subcore_axis_name="s")

    def body(rows_hbm, idx_hbm, out_hbm, idx_v, rows_v, sem):
        wid = lax.axis_index("s") * info.num_cores + lax.axis_index("c")
        base = wid * per_worker

        @pl.loop(0, per_worker // SC_WINDOW)
        def _(w):
            off = pl.multiple_of(base + w * SC_WINDOW, SC_WINDOW)
            pltpu.sync_copy(rows_hbm.at[pl.ds(off, SC_WINDOW)], rows_v)
            pltpu.sync_copy(idx_hbm.at[:, pl.ds(off, SC_WINDOW)], idx_v)
            copies = [pltpu.async_copy(rows_v, out_hbm.at[idx_v.at[k]], sem) for k in range(nk)]
            for cp in copies:
                cp.wait()

    return pl.kernel(
        body,
        out_type=jax.ShapeDtypeStruct((n_out, width), rows.dtype),
        mesh=mesh,
        scratch_types=[pltpu.VMEM((nk, SC_WINDOW), jnp.int32),
                       pltpu.VMEM((SC_WINDOW, width), rows.dtype),
                       pltpu.SemaphoreType.DMA],
        name="sc_scatter",
    )(rows, idx_kt)


def _expert_kernel(first_ref, count_ref, used_ref, w1_ref, w3_ref, w2_ref, xs_ref, ys_ref,
                   xbuf, ybuf, sem_in, sem_out, w1f, w3f, w2f, sem_w, w1b, w3b, w2b, *, layer):
    e = pl.program_id(0)
    ne = pl.num_programs(0)
    r = xbuf.shape[1]
    n_used = used_ref[0]

    def x_copy(g, slot):
        return pltpu.make_async_copy(xs_ref.at[pl.ds(pl.multiple_of(g * r, r), r), :], xbuf.at[slot], sem_in.at[slot])

    def y_copy(g, slot):
        return pltpu.make_async_copy(ybuf.at[slot], ys_ref.at[pl.ds(pl.multiple_of(g * r, r), r), :], sem_out.at[slot])

    def w_copies(ex, slot):
        return [pltpu.make_async_copy(src.at[layer, ex], dst.at[slot], sem_w.at[slot])
                for src, dst in ((w1_ref, w1f), (w3_ref, w3f), (w2_ref, w2f))]

    @pl.when(e == 0)
    def _first_reads():
        x_copy(0, 0).start()
        for cp in w_copies(0, 0):
            cp.start(priority=1)

    @pl.when(e + 1 < ne)
    def _next_weights():
        for cp in w_copies(e + 1, (e + 1) & 1):
            cp.start(priority=1)

    for cp in w_copies(e, e & 1):
        cp.wait()

    n = count_ref[e]

    @pl.when(n > 0)
    def _cast_weights():
        w1b[...] = w1f[e & 1].astype(BF16)
        w3b[...] = w3f[e & 1].astype(BF16)
        w2b[...] = w2f[e & 1].astype(BF16)

    def tile_body(g, carry):
        slot = g & 1
        x_copy(g, slot).wait()

        @pl.when(g + 1 < n_used)
        def _read_next():
            x_copy(g + 1, 1 - slot).start()

        @pl.when(g >= 2)
        def _free_out_slot():
            y_copy(g - 2, slot).wait()

        lo, hi = _unpack_rows(xbuf[slot])
        lo, hi = lo.astype(BF16), hi.astype(BF16)
        a = (jnp.dot(lo, w1b[:HALF, :], preferred_element_type=F32)
             + jnp.dot(hi, w1b[HALF:, :], preferred_element_type=F32))
        u = (jnp.dot(lo, w3b[:HALF, :], preferred_element_type=F32)
             + jnp.dot(hi, w3b[HALF:, :], preferred_element_type=F32))
        y = jnp.dot((_silu(a) * u).astype(BF16), w2b[...], preferred_element_type=F32)
        ybuf[slot] = _pack_rows(y)
        y_copy(g, slot).start()
        return carry

    g0 = first_ref[e]
    lax.fori_loop(g0, g0 + n, tile_body, 0)

    @pl.when(e == pl.num_programs(0) - 1)
    def _drain_writes():
        @pl.when(n_used >= 2)
        def _():
            y_copy(n_used - 2, n_used & 1).wait()
        y_copy(n_used - 1, (n_used - 1) & 1).wait()


def _experts(tile_first, tile_count, n_used, xs, w1, w3, w2, layer):
    n_rows = xs.shape[0]
    r = EXPERT_TILE
    any_spec = pl.BlockSpec(memory_space=pl.ANY)
    return pl.pallas_call(
        functools.partial(_expert_kernel, layer=layer),
        grid_spec=pltpu.PrefetchScalarGridSpec(
            num_scalar_prefetch=3,
            grid=(N_EXPERTS,),
            in_specs=[any_spec, any_spec, any_spec, any_spec],
            out_specs=any_spec,
            scratch_shapes=[pltpu.VMEM((2, r, HALF), jnp.uint32),
                            pltpu.VMEM((2, r, HALF), jnp.uint32),
                            pltpu.SemaphoreType.DMA((2,)),
                            pltpu.SemaphoreType.DMA((2,)),
                            pltpu.VMEM((2, D_MODEL, EXPERT_FF), F32),
                            pltpu.VMEM((2, D_MODEL, EXPERT_FF), F32),
                            pltpu.VMEM((2, EXPERT_FF, D_MODEL), F32),
                            pltpu.SemaphoreType.DMA((2,)),
                            pltpu.VMEM((D_MODEL, EXPERT_FF), BF16),
                            pltpu.VMEM((D_MODEL, EXPERT_FF), BF16),
                            pltpu.VMEM((EXPERT_FF, D_MODEL), BF16)],
        ),
        out_shape=jax.ShapeDtypeStruct((n_rows, HALF), jnp.uint32),
        compiler_params=_params(1),
        name="experts",
    )(tile_first, tile_count, n_used, w1, w3, w2, xs)


def _combine_dense_kernel(base_ref, g2_ref, w_ref, yg_ref, o_ref):
    acc_lo = acc_hi = None
    for k in range(TOP_K):
        lo, hi = _unpack_rows(yg_ref[k])
        wk = w_ref[:, k:k + 1]
        acc_lo = wk * lo if acc_lo is None else acc_lo + wk * lo
        acc_hi = wk * hi if acc_hi is None else acc_hi + wk * hi
    o_ref[:, :HALF] = base_ref[:, :HALF] + g2_ref[:, :HALF] * acc_lo
    o_ref[:, HALF:] = base_ref[:, HALF:] + g2_ref[:, HALF:] * acc_hi


def _combine_dense(base, g2, w_tok, yg, seq):
    t = base.shape[0]
    nt = ROUTE_TILE
    tpb = seq // nt
    return pl.pallas_call(
        _combine_dense_kernel,
        grid=(t // nt,),
        in_specs=[pl.BlockSpec((nt, D_MODEL), lambda i: (i, 0)),
                  pl.BlockSpec((None, 1, D_MODEL), lambda i: (i // tpb, 0, 0)),
                  pl.BlockSpec((nt, TOP_K), lambda i: (i, 0)),
                  pl.BlockSpec((TOP_K, nt, HALF), lambda i: (0, i, 0))],
        out_specs=pl.BlockSpec((nt, D_MODEL), lambda i: (i, 0)),
        out_shape=jax.ShapeDtypeStruct((t, D_MODEL), F32),
        compiler_params=_params(1),
        name="combine_dense",
    )(base, g2, w_tok, yg)


def _layer(layer, x, c, w_ada, b_ada, norm1_w, norm2_w, w_in, q_norm_w, k_norm_w, rel_bias, w_alpha, b_alpha,
           moba_out_w, gla_out_w, w_out, w_router, e_bias, w1, w3, w2, ws1, ws3, ws2):
    b, s, d = x.shape
    t = b * s
    x2 = x.reshape(t, d)

    mod = _mod(c, w_ada, b_ada)
    sh1, sc1, g1, sh2, sc2, g2 = [mod[:, j * d:(j + 1) * d].reshape(b, 1, d) for j in range(6)]

    w_main = w_in[:, :D_MAIN].astype(BF16)
    w_ga = jnp.zeros((d, LANES), BF16).at[:, :GLA_GATE_RANK].set(w_in[:, D_MAIN:].astype(BF16))
    per_chunk = 256 // MOBA_HEAD_DIM
    qw = jnp.tile(q_norm_w.astype(F32), per_chunk).reshape(1, 256) * (MOBA_HEAD_DIM ** -0.5)
    kw = jnp.tile(k_norm_w.astype(F32), per_chunk).reshape(1, 256)
    proj, ga = _inproj(x2, sc1, sh1, norm1_w.reshape(1, d), w_main, w_ga, qw, kw, s)
    proj3 = proj.reshape(b, s, D_MAIN)

    near, far = _moba_bias_tables(rel_bias)
    ow = jnp.tile(moba_out_w.astype(F32), 2).reshape(1, LANES)
    o_a = _moba(proj3, near, far, ow)

    wal = jnp.zeros((LANES, GLA_KEY_WIDTH), F32).at[:GLA_GATE_RANK].set(w_alpha)
    o_b = _gla(proj3, ga.reshape(b, s, LANES), wal, b_alpha.reshape(1, GLA_KEY_WIDTH),
               gla_out_w.reshape(1, GLA_DV))

    base, h2, scores_t = _outproj(
        o_a.reshape(t, MOBA_WIDTH), o_b.reshape(t, GLA_WIDTH), x2, g1, sc2, sh2, g2,
        norm2_w.reshape(1, d), w_out.astype(BF16), ws1.astype(BF16), ws3.astype(BF16), ws2.astype(BF16),
        w_router.T.astype(BF16), s)

    eb = jnp.broadcast_to(e_bias.astype(F32)[:, None], (N_EXPERTS, ROUTE_TILE))
    code_t, w_t, counts = _route(scores_t, eb)

    r = EXPERT_TILE
    n_tiles = (t * TOP_K + N_EXPERTS * (r - 1) + r - 1) // r
    n_rows = n_tiles * r
    cnt = counts[:, 0].astype(jnp.int32)
    padded = (cnt + r - 1) // r * r
    pend = jnp.cumsum(padded)
    pstart = pend - padded
    n_used = (pend[-1:] // r).astype(jnp.int32)
    dest_t = _slots(pstart, code_t)

    xs = _sc_scatter_rows(h2, dest_t, n_rows)
    ys = _experts(pstart // r, padded // r, n_used, xs, w1, w3, w2, layer)
    yg = _sc_gather_rows(ys, dest_t.reshape(TOP_K * t)).reshape(TOP_K, t, HALF)
    out = _combine_dense(base, g2, w_t.T, yg, s)
    return out.reshape(b, s, d)


def kernel(x, c, w_ada, b_ada, norm1_w, norm2_w, w_in, q_norm_w, k_norm_w, rel_bias, w_alpha, b_alpha,
           moba_out_w, gla_out_w, w_out, w_router, e_bias, w1, w3, w2, ws1, ws3, ws2):
    for l in range(w_ada.shape[0]):
        x = _layer(l, x, c, w_ada[l], b_ada[l], norm1_w[l], norm2_w[l], w_in[l], q_norm_w[l], k_norm_w[l],
                   rel_bias, w_alpha[l], b_alpha[l], moba_out_w[l], gla_out_w[l], w_out[l], w_router[l],
                   e_bias[l], w1, w3, w2, ws1[l], ws3[l], ws2[l])
    return x
```

```python
import functools
import math

import numpy as np
import jax
import jax.numpy as jnp
from jax import lax
from jax.experimental import pallas as pl
from jax.experimental.pallas import tpu as pltpu

D_MODEL = 1024
MOBA_HEADS = 8
MOBA_HEAD_DIM = 64
MOBA_WIDTH = MOBA_HEADS * MOBA_HEAD_DIM
MOBA_BLOCK = 256
MOBA_TOPK = 3
GLA_HEADS = 4
GLA_DK = 64
GLA_DV = 128
GLA_KEY_WIDTH = GLA_HEADS * GLA_DK
GLA_WIDTH = GLA_HEADS * GLA_DV
GLA_GATE_RANK = 16
GLA_GATE_TAU = 16.0
GLA_CHUNK = 64
REL_BUCKETS = 32
REL_MAX_DIST = 128
N_EXPERTS = 256
TOP_K = 8
N_GROUPS = 8
TOPK_GROUPS = 4
GROUP_SIZE = N_EXPERTS // N_GROUPS
EXPERT_FF = 256
SHARED_FF = 256
ROUTED_SCALE = 2.5
NORM_EPS = 1e-6

D_MAIN = 3 * MOBA_WIDTH + 2 * GLA_KEY_WIDTH + 2 * GLA_WIDTH
LANES = 128
VMEM_LIMIT = 56 * 1024 * 1024

ROW_TILE = 512
ROUTE_TILE = 256
EXPERT_TILE = 256
EXPERT_IN_RING = 4
EXPERT_OUT_RING = 3

F32 = jnp.float32
BF16 = jnp.bfloat16
NT_DIMS = (((1,), (1,)), ((), ()))
TN_DIMS = (((0,), (0,)), ((), ()))


def _params(n_axes):
    return pltpu.CompilerParams(dimension_semantics=("arbitrary",) * n_axes,
                                vmem_limit_bytes=VMEM_LIMIT)


def _silu(v):
    return v * jax.nn.sigmoid(v)


def _mod_kernel(c_ref, w_ref, b_ref, o_ref):
    o_ref[...] = jnp.dot(_silu(c_ref[...]), w_ref[...], preferred_element_type=F32) + b_ref[...]


def _mod(c, w, b):
    rows = 8
    cp = jnp.zeros((rows, D_MODEL), F32).at[:c.shape[0]].set(c)
    n = w.shape[1]
    tn = 1024
    out = pl.pallas_call(
        _mod_kernel,
        grid=(n // tn,),
        in_specs=[pl.BlockSpec((rows, D_MODEL), lambda j: (0, 0)),
                  pl.BlockSpec((D_MODEL, tn), lambda j: (0, j)),
                  pl.BlockSpec((1, tn), lambda j: (0, j))],
        out_specs=pl.BlockSpec((rows, tn), lambda j: (0, j)),
        out_shape=jax.ShapeDtypeStruct((rows, n), F32),
        compiler_params=_params(1),
        name="mod",
    )(cp, w, b.reshape(1, n))
    return out[:c.shape[0]]


def _group_rms_inv(a, group):
    lane = lax.broadcasted_iota(jnp.int32, (1, a.shape[1]), 1)
    a2 = a * a
    inv = jnp.zeros_like(a)
    for g in range(a.shape[1] // group):
        m = (lane >= g * group) & (lane < (g + 1) * group)
        ss = jnp.sum(jnp.where(m, a2, 0.0), axis=-1, keepdims=True)
        inv = jnp.where(m, lax.rsqrt(ss * (1.0 / group) + NORM_EPS), inv)
    return inv


def _inproj_kernel(x_ref, sc_ref, sh_ref, nw_ref, w_ref, wga_ref, qw_ref, kw_ref, o_ref, ga_ref):
    x = x_ref[...]
    ms = jnp.mean(x * x, axis=-1, keepdims=True)
    h = x * lax.rsqrt(ms + NORM_EPS) * nw_ref[...]
    h = h * (1.0 + sc_ref[...]) + sh_ref[...]
    hb = h.astype(BF16)
    cw = 256
    for j in range(D_MAIN // cw):
        acc = jnp.dot(hb, w_ref[:, j * cw:(j + 1) * cw], preferred_element_type=F32)
        if j < 2 * MOBA_WIDTH // cw:
            nw = qw_ref if j < MOBA_WIDTH // cw else kw_ref
            acc = acc * _group_rms_inv(acc, MOBA_HEAD_DIM) * nw[...]
        o_ref[:, j * cw:(j + 1) * cw] = acc.astype(BF16)
    ga_ref[...] = jnp.dot(hb, wga_ref[...], preferred_element_type=F32)


def _inproj(x2, sc, sh, nw, w_main, w_ga, qw, kw, seq):
    t = x2.shape[0]
    tpb = seq // ROW_TILE
    vec = lambda: pl.BlockSpec((None, 1, D_MODEL), lambda i: (i // tpb, 0, 0))
    full = lambda a: pl.BlockSpec(a.shape, lambda i: (0,) * a.ndim)
    return pl.pallas_call(
        _inproj_kernel,
        grid=(t // ROW_TILE,),
        in_specs=[pl.BlockSpec((ROW_TILE, D_MODEL), lambda i: (i, 0)), vec(), vec(),
                  full(nw), full(w_main), full(w_ga), full(qw), full(kw)],
        out_specs=[pl.BlockSpec((ROW_TILE, D_MAIN), lambda i: (i, 0)),
                   pl.BlockSpec((ROW_TILE, LANES), lambda i: (i, 0))],
        out_shape=[jax.ShapeDtypeStruct((t, D_MAIN), BF16),
                   jax.ShapeDtypeStruct((t, LANES), F32)],
        compiler_params=_params(1),
        name="inproj",
    )(x2, sc, sh, nw, w_main, w_ga, qw, kw)


def _t5_bucket_np(rel):
    max_exact = REL_BUCKETS // 2
    relf = np.maximum(rel, 1).astype(np.float64)
    large = max_exact + (np.log(relf / max_exact) / math.log(REL_MAX_DIST / max_exact)
                         * (REL_BUCKETS - max_exact)).astype(np.int32)
    large = np.minimum(large, REL_BUCKETS - 1)
    return np.where(rel < max_exact, rel, large)


def _bias_kernel(rb_ref, idx_ref, o_ref):
    h = pl.program_id(0)
    idx = idx_ref[...]
    tab = jnp.full(idx.shape, -jnp.inf, F32)
    for bk in range(REL_BUCKETS):
        tab = jnp.where(idx == bk, rb_ref[bk * MOBA_HEADS + h], tab)
    o_ref[...] = tab


def _moba_bias_tables(rel_bias):
    j = np.arange(MOBA_BLOCK)[:, None]
    i = np.arange(MOBA_BLOCK)[None, :]
    own_idx = np.where(j <= i, _t5_bucket_np(np.maximum(i - j, 0)), -1)
    prev_idx = _t5_bucket_np(MOBA_BLOCK + i - j)
    idx = jnp.asarray(np.concatenate([prev_idx, own_idx], axis=0).astype(np.int32))
    assert int(_t5_bucket_np(np.array([MOBA_BLOCK + 1]))[0]) == REL_BUCKETS - 1
    rb = rel_bias.astype(F32)
    near = pl.pallas_call(
        _bias_kernel,
        grid=(MOBA_HEADS,),
        in_specs=[pl.BlockSpec(memory_space=pltpu.SMEM),
                  pl.BlockSpec(idx.shape, lambda h: (0, 0))],
        out_specs=pl.BlockSpec((None,) + idx.shape, lambda h: (h, 0, 0)),
        out_shape=jax.ShapeDtypeStruct((MOBA_HEADS,) + idx.shape, F32),
        compiler_params=_params(1),
        name="bias",
    )(rb.reshape(-1), idx)
    return near, rb[REL_BUCKETS - 1]


FAR_GROUP = 4


def _moba_kernel(far_ref, q_ref, k_ref, v_ref, near_ref, ow_ref, o_ref,
                 vt_ref, vtg_ref, acc_ref, m_ref, sel_ref, s_ref):
    hp = pl.program_id(1)
    i = pl.program_id(2)
    nblk = k_ref.shape[0] // MOBA_BLOCK
    ngrp = nblk // FAR_GROUP
    hd = MOBA_HEAD_DIM
    bs = MOBA_BLOCK
    lane = lax.broadcasted_iota(jnp.int32, (bs, LANES), 1)

    def split_heads(qb):
        zero = jnp.zeros_like(qb)
        return jnp.where(lane < hd, qb, zero), jnp.where(lane < hd, zero, qb)

    @pl.when(i == 0)
    def _prepare():
        row = lax.broadcasted_iota(jnp.int32, (LANES, bs), 0)
        kmeans = []
        for n in range(nblk):
            kb = k_ref[n * bs:(n + 1) * bs, :].astype(F32)
            kmeans.append(jnp.mean(kb, axis=0, keepdims=True))
            vt = v_ref[n * bs:(n + 1) * bs, :].astype(F32).T
            vt0 = jnp.where(row < hd, vt, 1.0).astype(BF16)
            vt1 = jnp.where(row < hd, 1.0, vt).astype(BF16)
            vt_ref[0, n] = vt0
            vt_ref[1, n] = vt1
            gcols = slice((n % FAR_GROUP) * bs, (n % FAR_GROUP + 1) * bs)
            vtg_ref[0, n // FAR_GROUP, :, gcols] = vt0
            vtg_ref[1, n // FAR_GROUP, :, gcols] = vt1
        kmean = jnp.concatenate(kmeans, axis=0)
        km_hi = kmean.astype(BF16)
        km_lo = (kmean - km_hi.astype(F32)).astype(BF16)
        blk = lax.broadcasted_iota(jnp.int32, (nblk, bs), 0)
        for ib in range(nblk):
            qparts = split_heads(q_ref[ib * bs:(ib + 1) * bs, :])
            for h in range(2):
                gt = (lax.dot_general(km_hi, qparts[h], NT_DIMS, preferred_element_type=F32)
                      + lax.dot_general(km_lo, qparts[h], NT_DIMS, preferred_element_type=F32))
                gt = jnp.where(blk < ib, gt, -jnp.inf)
                cnt = jnp.zeros(gt.shape, jnp.int32)
                for m in range(ib):
                    gm = gt[m:m + 1, :]
                    cnt = cnt + jnp.where((gm > gt) | ((gm == gt) & (blk > m)), 1, 0)
                keep = (blk < ib) & (cnt < MOBA_TOPK)
                sel_ref[0, h, ib] = jnp.where(keep, 1.0, 0.0)
                sel_ref[1, h, ib] = jnp.where(keep & (blk < ib - 1), 1.0, 0.0)

    qh = split_heads(q_ref[pl.ds(pl.multiple_of(i * bs, bs), bs), :])

    @pl.when(i == 0)
    def _own_block_only():
        kb = k_ref[0:bs, :]
        for h in range(2):
            s = lax.dot_general(kb, qh[h], NT_DIMS, preferred_element_type=F32) + near_ref[h, bs:2 * bs, :]
            m_new = jnp.max(s, axis=0, keepdims=True)
            p = jnp.exp(s - m_new).astype(BF16)
            acc_ref[h] = jnp.dot(vt_ref[h, 0], p, preferred_element_type=F32)
            m_ref[h] = m_new

    @pl.when(i >= 1)
    def _previous_and_own_block():
        kbs = (k_ref[pl.ds(pl.multiple_of((i - 1) * bs, bs), bs), :],
               k_ref[pl.ds(pl.multiple_of(i * bs, bs), bs), :])
        ss = [[lax.dot_general(kbs[w], qh[h], NT_DIMS, preferred_element_type=F32)
               + near_ref[h, w * bs:(w + 1) * bs, :] for w in range(2)] for h in range(2)]
        ps, ms = [], []
        for h in range(2):
            s_prev, s_own = ss[h]
            keep = sel_ref[0, h, i, pl.ds(i - 1, 1), :] > 0.5
            mx = jnp.where(keep, jnp.max(s_prev, axis=0, keepdims=True), -jnp.inf)
            m_new = jnp.maximum(jnp.max(s_own, axis=0, keepdims=True), mx)
            ps.append((jnp.exp(s_prev - jnp.where(keep, m_new, jnp.inf)).astype(BF16),
                       jnp.exp(s_own - m_new).astype(BF16)))
            ms.append(m_new)
        for h in range(2):
            acc_ref[h] = (jnp.dot(vt_ref[h, i - 1], ps[h][0], preferred_element_type=F32)
                          + jnp.dot(vt_ref[h, i], ps[h][1], preferred_element_type=F32))
            m_ref[h] = ms[h]

    n_far = (i + FAR_GROUP - 2) // FAR_GROUP
    gk = FAR_GROUP * bs

    def far_scores(g):
        kb = k_ref[g * gk:(g + 1) * gk, :]
        for h in range(2):
            s_ref[g % 2, h] = lax.dot_general(kb, qh[h], NT_DIMS, preferred_element_type=F32)

    @pl.when(n_far > 0)
    def _first_far_scores():
        far_scores(0)

    for g in range(ngrp):
        @pl.when(g < n_far)
        def _far_group(g=g):
            if g + 1 < ngrp:
                far_scores(g + 1)
            for h in range(2):
                fb = far_ref[2 * hp + h]
                m_old = m_ref[h]
                m_new = m_old
                keeps = []
                for j in range(FAR_GROUP):
                    keep = sel_ref[1, h, i, pl.ds(g * FAR_GROUP + j, 1), :] > 0.5
                    mx = jnp.max(s_ref[g % 2, h, j * bs:(j + 1) * bs, :], axis=0, keepdims=True) + fb
                    m_new = jnp.maximum(m_new, jnp.where(keep, mx, -jnp.inf))
                    keeps.append(keep)
                p = jnp.concatenate(
                    [jnp.exp(s_ref[g % 2, h, j * bs:(j + 1) * bs, :]
                             - jnp.where(keeps[j], m_new - fb, jnp.inf)).astype(BF16)
                     for j in range(FAR_GROUP)], axis=0)
                pv = jnp.dot(vtg_ref[h, g], p, preferred_element_type=F32)
                acc_ref[h] = acc_ref[h] * jnp.exp(m_old - m_new) + pv
                m_ref[h] = m_new

    a0 = acc_ref[0]
    a1 = acc_ref[1]
    row = lax.broadcasted_iota(jnp.int32, a0.shape, 0)
    ot = jnp.where(row < hd, a0 / a0[hd:hd + 1, :], a1 / a1[0:1, :])
    o2 = ot * ot
    ss0 = jnp.sum(jnp.where(row < hd, o2, 0.0), axis=0, keepdims=True)
    ss1 = jnp.sum(jnp.where(row < hd, 0.0, o2), axis=0, keepdims=True)
    inv = jnp.where(row < hd, lax.rsqrt(ss0 * (1.0 / hd) + NORM_EPS), lax.rsqrt(ss1 * (1.0 / hd) + NORM_EPS))
    o_ref[...] = ((ot * inv).T * ow_ref[...]).astype(o_ref.dtype)


def _moba(proj3, near, far, ow):
    b, s, _ = proj3.shape
    nblk = s // MOBA_BLOCK
    assert nblk % FAR_GROUP == 0
    npair = MOBA_HEADS // 2
    kcol = MOBA_WIDTH // LANES
    return pl.pallas_call(
        _moba_kernel,
        grid=(b, npair, nblk),
        in_specs=[pl.BlockSpec(memory_space=pltpu.SMEM),
                  pl.BlockSpec((None, s, LANES), lambda bb, hp, i: (bb, 0, hp)),
                  pl.BlockSpec((None, s, LANES), lambda bb, hp, i: (bb, 0, kcol + hp)),
                  pl.BlockSpec((None, s, LANES), lambda bb, hp, i: (bb, 0, 2 * kcol + hp)),
                  pl.BlockSpec((2, 2 * MOBA_BLOCK, MOBA_BLOCK), lambda bb, hp, i: (hp, 0, 0)),
                  pl.BlockSpec((1, LANES), lambda bb, hp, i: (0, 0))],
        out_specs=pl.BlockSpec((None, MOBA_BLOCK, LANES), lambda bb, hp, i: (bb, i, hp)),
        out_shape=jax.ShapeDtypeStruct((b, s, MOBA_WIDTH), BF16),
        scratch_shapes=[pltpu.VMEM((2, nblk, LANES, MOBA_BLOCK), BF16),
                        pltpu.VMEM((2, nblk // FAR_GROUP, LANES, FAR_GROUP * MOBA_BLOCK), BF16),
                        pltpu.VMEM((2, LANES, MOBA_BLOCK), F32),
                        pltpu.VMEM((2, 1, MOBA_BLOCK), F32),
                        pltpu.VMEM((2, 2, nblk, nblk, MOBA_BLOCK), F32),
                        pltpu.VMEM((2, 2, FAR_GROUP * MOBA_BLOCK, MOBA_BLOCK), F32)],
        compiler_params=_params(3),
        name="moba",
    )(far, proj3, proj3, proj3, near, ow)


def _split3(v):
    hi = v.astype(BF16)
    r1 = v - hi.astype(F32)
    mid = r1.astype(BF16)
    lo = (r1 - mid.astype(F32)).astype(BF16)
    return hi, mid, lo


GLA_UNROLL = 4


def _gla_kernel(q_ref, k_ref, v_ref, g_ref, ga_ref, wal_ref, bal_ref, gw_ref, o_ref, b_ref, st_ref):
    seq = q_ref.shape[0]
    c = GLA_CHUNK
    pc = 256

    rr = lax.broadcasted_iota(jnp.int32, (pc, pc), 0)
    cc = lax.broadcasted_iota(jnp.int32, (pc, pc), 1)
    tri = jnp.where((rr >= cc) & (rr // c == cc // c), 1.0, 0.0).astype(BF16)

    def decay_body(j, carry):
        rows = [pl.ds(pl.multiple_of((j * GLA_UNROLL + u) * pc, pc), pc) for u in range(GLA_UNROLL)]
        xg = [jnp.dot(ga_ref[r, :], wal_ref[...], preferred_element_type=F32) + bal_ref[...] for r in rows]
        parts = [_split3((jnp.minimum(x, 0.0) - jnp.log(1.0 + jnp.exp(-jnp.abs(x)))) * (1.0 / GLA_GATE_TAU))
                 for x in xg]
        sums = [[jnp.dot(tri, term, preferred_element_type=F32) for term in p] for p in parts]
        for r, (hi, mid, lo) in zip(rows, sums):
            b_ref[r, :] = hi + mid + lo
        return carry

    lax.fori_loop(0, seq // (pc * GLA_UNROLL), decay_body, 0)

    st_ref[...] = jnp.zeros_like(st_ref)
    lane = lax.broadcasted_iota(jnp.int32, (c, LANES), 1)
    head_mask = (lane < GLA_DK, lane >= GLA_DK)
    causal = lax.broadcasted_iota(jnp.int32, (c, c), 0) >= lax.broadcasted_iota(jnp.int32, (c, c), 1)

    units = [(u, h) for u in range(GLA_UNROLL) for h in range(2)]

    def chunk_body(ci, carry):
        rows = [pl.ds(pl.multiple_of((ci * GLA_UNROLL + u) * c, c), c) for u in range(GLA_UNROLL)]
        qt, kt, qs, ke, e_last = [], [], [], [], []
        for u in range(GLA_UNROLL):
            b = b_ref[rows[u], :]
            ref_row = b[c // 2 - 1:c // 2, :]
            last = b[c - 1:c, :]
            q = q_ref[rows[u], :].astype(F32) * (GLA_DK ** -0.5)
            k = k_ref[rows[u], :].astype(F32)
            qt.append(q * jnp.exp(b - ref_row))
            kt.append((k * jnp.exp(ref_row - b)).astype(BF16))
            qs.append(q * jnp.exp(b))
            ke.append((k * jnp.exp(last - b)).astype(BF16))
            e_last.append(jnp.exp(last))
        vs = {(u, h): v_ref[rows[u], h * GLA_DV:(h + 1) * GLA_DV] for u, h in units}
        a = {(u, h): lax.dot_general(jnp.where(head_mask[h], qt[u], 0.0).astype(BF16), kt[u], NT_DIMS,
                                     preferred_element_type=F32) for u, h in units}
        inc = {(u, h): lax.dot_general(vs[u, h], ke[u], TN_DIMS, preferred_element_type=F32) for u, h in units}
        o = {(u, h): jnp.dot(jnp.where(causal, a[u, h], 0.0).astype(BF16), vs[u, h], preferred_element_type=F32)
             for u, h in units}
        states = {}
        for h in range(2):
            st = st_ref[h]
            for u in range(GLA_UNROLL):
                states[u, h] = st
                st = st * e_last[u] + inc[u, h]
            st_ref[h] = st
        for u, h in units:
            cols = slice(h * GLA_DV, (h + 1) * GLA_DV)
            ou = o[u, h] + lax.dot_general(jnp.where(head_mask[h], qs[u], 0.0).astype(BF16),
                                           states[u, h].astype(BF16), NT_DIMS, preferred_element_type=F32)
            ms = jnp.mean(ou * ou, axis=-1, keepdims=True)
            on = ou * lax.rsqrt(ms + NORM_EPS) * gw_ref[...]
            g = g_ref[rows[u], cols].astype(F32)
            o_ref[rows[u], cols] = (on * _silu(g)).astype(o_ref.dtype)
        return carry

    lax.fori_loop(0, seq // (c * GLA_UNROLL), chunk_body, 0)


def _gla(proj3, ga3, wal, bal, gw):
    b, s, _ = proj3.shape
    npair = GLA_HEADS // 2
    qcol = 3 * MOBA_WIDTH // LANES
    kcol = qcol + GLA_KEY_WIDTH // LANES
    vcol = (3 * MOBA_WIDTH + 2 * GLA_KEY_WIDTH) // (2 * GLA_DV)
    gcol = vcol + npair
    return pl.pallas_call(
        _gla_kernel,
        grid=(b, npair),
        in_specs=[pl.BlockSpec((None, s, LANES), lambda bb, hp: (bb, 0, qcol + hp)),
                  pl.BlockSpec((None, s, LANES), lambda bb, hp: (bb, 0, kcol + hp)),
                  pl.BlockSpec((None, s, 2 * GLA_DV), lambda bb, hp: (bb, 0, vcol + hp)),
                  pl.BlockSpec((None, s, 2 * GLA_DV), lambda bb, hp: (bb, 0, gcol + hp)),
                  pl.BlockSpec((None, s, LANES), lambda bb, hp: (bb, 0, 0)),
                  pl.BlockSpec((LANES, LANES), lambda bb, hp: (0, hp)),
                  pl.BlockSpec((1, LANES), lambda bb, hp: (0, hp)),
                  pl.BlockSpec((1, GLA_DV), lambda bb, hp: (0, 0))],
        out_specs=pl.BlockSpec((None, s, 2 * GLA_DV), lambda bb, hp: (bb, 0, hp)),
        out_shape=jax.ShapeDtypeStruct((b, s, GLA_WIDTH), BF16),
        scratch_shapes=[pltpu.VMEM((s, LANES), F32),
                        pltpu.VMEM((2, GLA_DV, LANES), F32)],
        compiler_params=_params(2),
        name="gla",
    )(proj3, proj3, proj3, proj3, ga3, wal, bal, gw)


HALF = D_MODEL // 2


def _pack_rows(v):
    return pltpu.pack_elementwise([v[:, :HALF], v[:, HALF:]], packed_dtype=BF16)


def _unpack_rows(w):
    return (pltpu.unpack_elementwise(w, index=0, packed_dtype=BF16, unpacked_dtype=F32),
            pltpu.unpack_elementwise(w, index=1, packed_dtype=BF16, unpacked_dtype=F32))


def _outproj_kernel(oa_ref, ob_ref, x_ref, g1_ref, sc_ref, sh_ref, g2_ref, nw_ref, wo_ref,
                    ws1_ref, ws3_ref, ws2_ref, wrt_ref, base_ref, h_ref, st_ref):
    mix = (jnp.dot(oa_ref[...], wo_ref[:MOBA_WIDTH, :], preferred_element_type=F32)
           + jnp.dot(ob_ref[...], wo_ref[MOBA_WIDTH:, :], preferred_element_type=F32))
    x1 = x_ref[...] + g1_ref[...] * mix
    ms = jnp.mean(x1 * x1, axis=-1, keepdims=True)
    h = x1 * lax.rsqrt(ms + NORM_EPS) * nw_ref[...]
    h = h * (1.0 + sc_ref[...]) + sh_ref[...]
    h_ref[...] = _pack_rows(h)
    hb = h.astype(BF16)
    a = jnp.dot(hb, ws1_ref[...], preferred_element_type=F32)
    u = jnp.dot(hb, ws3_ref[...], preferred_element_type=F32)
    shared = jnp.dot((_silu(a) * u).astype(BF16), ws2_ref[...], preferred_element_type=F32)
    base_ref[...] = x1 + g2_ref[...] * shared
    logits_t = lax.dot_general(wrt_ref[...], hb, NT_DIMS, preferred_element_type=F32)
    st_ref[...] = jax.nn.sigmoid(logits_t)


def _outproj(oa, ob, x2, g1, sc, sh, g2, nw, wo, ws1, ws3, ws2, wrt, seq):
    t = x2.shape[0]
    tpb = seq // ROW_TILE
    vec = lambda: pl.BlockSpec((None, 1, D_MODEL), lambda i: (i // tpb, 0, 0))
    full = lambda a: pl.BlockSpec(a.shape, lambda i: (0,) * a.ndim)
    rows = lambda w: pl.BlockSpec((ROW_TILE, w), lambda i: (i, 0))
    return pl.pallas_call(
        _outproj_kernel,
        grid=(t // ROW_TILE,),
        in_specs=[rows(MOBA_WIDTH), rows(GLA_WIDTH), rows(D_MODEL), vec(), vec(), vec(), vec(),
                  full(nw), full(wo), full(ws1), full(ws3), full(ws2), full(wrt)],
        out_specs=[rows(D_MODEL), rows(HALF), pl.BlockSpec((N_EXPERTS, ROW_TILE), lambda i: (0, i))],
        out_shape=[jax.ShapeDtypeStruct((t, D_MODEL), F32),
                   jax.ShapeDtypeStruct((t, HALF), jnp.uint32),
                   jax.ShapeDtypeStruct((N_EXPERTS, t), F32)],
        compiler_params=_params(1),
        name="outproj",
    )(oa, ob, x2, g1, sc, sh, g2, nw, wo, ws1, ws3, ws2, wrt)


SLOT_CODE_SHIFT = 16
SLOT_CODE_BASE = 1 << SLOT_CODE_SHIFT


def _route_kernel(s_ref, eb_ref, code_ref, w_ref, cnt_ref, carry_ref):
    i = pl.program_id(0)
    ne, nt = s_ref.shape

    @pl.when(i == 0)
    def _init():
        carry_ref[...] = jnp.zeros_like(carry_ref)

    s = s_ref[...]
    choice = s + eb_ref[...]
    gio = lax.broadcasted_iota(jnp.int32, (GROUP_SIZE, nt), 0)
    gscore = []
    for g in range(N_GROUPS):
        cg = choice[g * GROUP_SIZE:(g + 1) * GROUP_SIZE, :]
        top1 = jnp.max(cg, axis=0, keepdims=True)
        first = jnp.min(jnp.where(cg == top1, gio, GROUP_SIZE), axis=0, keepdims=True)
        top2 = jnp.max(jnp.where(gio == first, -jnp.inf, cg), axis=0, keepdims=True)
        gscore.append(top1 + top2)
    gs = jnp.concatenate(gscore, axis=0)
    gidx = lax.broadcasted_iota(jnp.int32, gs.shape, 0)
    beaten = jnp.zeros(gs.shape, jnp.int32)
    for m in range(N_GROUPS):
        gm = gs[m:m + 1, :]
        beaten = beaten + jnp.where((gm > gs) | ((gm == gs) & (gidx > m)), 1, 0)
    gkeep = beaten < TOPK_GROUPS
    masked = jnp.concatenate(
        [jnp.where(gkeep[g:g + 1, :], choice[g * GROUP_SIZE:(g + 1) * GROUP_SIZE, :], -jnp.inf)
         for g in range(N_GROUPS)], axis=0)

    eio = lax.broadcasted_iota(jnp.int32, (ne, nt), 0)
    picked = jnp.zeros((ne, nt), F32)
    idx_rows, w_rows, hits = [], [], []
    for _ in range(TOP_K):
        mx = jnp.max(masked, axis=0, keepdims=True)
        idx = jnp.min(jnp.where(masked == mx, eio, ne), axis=0, keepdims=True)
        hit = eio == idx
        w_rows.append(jnp.sum(jnp.where(hit, s, 0.0), axis=0, keepdims=True))
        idx_rows.append(idx)
        hits.append(hit)
        masked = jnp.where(hit, -jnp.inf, masked)
        picked = jnp.where(hit, 1.0, picked)
    wk = jnp.concatenate(w_rows, axis=0)
    w_ref[...] = wk / jnp.sum(wk, axis=0, keepdims=True) * ROUTED_SCALE

    tr = lax.broadcasted_iota(jnp.int32, (nt, nt), 0)
    tc = lax.broadcasted_iota(jnp.int32, (nt, nt), 1)
    before = jnp.where(tr < tc, 1.0, 0.0).astype(BF16)
    pb = picked.astype(BF16)
    pos = carry_ref[...] + jnp.dot(pb, before, preferred_element_type=F32)
    rank = jnp.concatenate(
        [jnp.sum(jnp.where(hit, pos, 0.0), axis=0, keepdims=True) for hit in hits], axis=0).astype(jnp.int32)
    code_ref[...] = jnp.concatenate(idx_rows, axis=0) * SLOT_CODE_BASE + rank
    total = carry_ref[...] + jnp.dot(pb, jnp.ones((nt, nt), BF16), preferred_element_type=F32)
    carry_ref[...] = total
    cnt_ref[...] = total


def _route(scores_t, eb):
    ne, t = scores_t.shape
    assert t <= SLOT_CODE_BASE
    nt = ROUTE_TILE
    tok = lambda dt: jax.ShapeDtypeStruct((TOP_K, t), dt)
    return pl.pallas_call(
        _route_kernel,
        grid=(t // nt,),
        in_specs=[pl.BlockSpec((ne, nt), lambda i: (0, i)),
                  pl.BlockSpec((ne, nt), lambda i: (0, 0))],
        out_specs=[pl.BlockSpec((TOP_K, nt), lambda i: (0, i)),
                   pl.BlockSpec((TOP_K, nt), lambda i: (0, i)),
                   pl.BlockSpec((ne, nt), lambda i: (0, 0))],
        out_shape=[tok(jnp.int32), tok(F32), jax.ShapeDtypeStruct((ne, nt), F32)],
        scratch_shapes=[pltpu.VMEM((ne, nt), F32)],
        compiler_params=_params(1),
        name="route",
    )(scores_t, eb)


SLOT_TILE = 2048


def _slots_kernel(pstart_ref, code_ref, o_ref):
    code = code_ref[...]
    expert = lax.shift_right_logical(code, SLOT_CODE_SHIFT)

    def body(e, acc):
        return jnp.where(expert == e, pstart_ref[e], acc)

    start = lax.fori_loop(0, N_EXPERTS, body, jnp.zeros_like(code), unroll=8)
    o_ref[...] = start + (code & (SLOT_CODE_BASE - 1))


def _slots(pstart, code_t):
    k, t = code_t.shape
    return pl.pallas_call(
        _slots_kernel,
        grid_spec=pltpu.PrefetchScalarGridSpec(
            num_scalar_prefetch=1,
            grid=(t // SLOT_TILE,),
            in_specs=[pl.BlockSpec((k, SLOT_TILE), lambda i, p: (0, i))],
            out_specs=pl.BlockSpec((k, SLOT_TILE), lambda i, p: (0, i)),
        ),
        out_shape=jax.ShapeDtypeStruct((k, t), jnp.int32),
        compiler_params=_params(1),
        name="slots",
    )(pstart, code_t)


SC_WINDOW = 128


def _sc_gather_rows(table, idx_flat):
    from jax.experimental.pallas import tpu_sc as plsc
    info = plsc.get_sparse_core_info()
    nw = info.num_cores * info.num_subcores
    n = idx_flat.shape[0]
    width = table.shape[1]
    per_worker = n // nw
    assert per_worker * nw == n and per_worker % SC_WINDOW == 0
    mesh = plsc.VectorSubcoreMesh(core_axis_name="c", subcore_axis_name="s")

    def body(table_hbm, idx_hbm, out_hbm, idx_v, rows_v, sem):
        wid = lax.axis_index("s") * info.num_cores + lax.axis_index("c")
        base = wid * per_worker

        @pl.loop(0, per_worker // SC_WINDOW)
        def _(w):
            off = pl.multiple_of(base + w * SC_WINDOW, SC_WINDOW)
            pltpu.sync_copy(idx_hbm.at[pl.ds(off, SC_WINDOW)], idx_v)
            pltpu.async_copy(table_hbm.at[idx_v], rows_v, sem).wait()
            pltpu.sync_copy(rows_v, out_hbm.at[pl.ds(off, SC_WINDOW)])

    return pl.kernel(
        body,
        out_type=jax.ShapeDtypeStruct((n, width), table.dtype),
        mesh=mesh,
        scratch_types=[pltpu.VMEM((SC_WINDOW,), jnp.int32),
                       pltpu.VMEM((SC_WINDOW, width), table.dtype),
                       pltpu.SemaphoreType.DMA],
        name="sc_gather",
    )(table, idx_flat)


def _sc_scatter_rows(rows, idx_kt, n_out):
    from jax.experimental.pallas import tpu_sc as plsc
    info = plsc.get_sparse_core_info()
    nw = info.num_cores * info.num_subcores
    t, width = rows.shape
    nk = idx_kt.shape[0]
    per_worker = t // nw
    assert per_worker * nw == t and per_worker % SC_WINDOW == 0
    mesh = plsc.VectorSubcoreMesh(core_axis_name="c", subcore_axis_name="s")

    def body(rows_hbm, idx_hbm, out_hbm, idx_v, rows_v, sem):
        wid = lax.axis_index("s") * info.num_cores + lax.axis_index("c")
        base = wid * per_worker

        @pl.loop(0, per_worker // SC_WINDOW)
        def _(w):
            off = pl.multiple_of(base + w * SC_WINDOW, SC_WINDOW)
            pltpu.sync_copy(rows_hbm.at[pl.ds(off, SC_WINDOW)], rows_v)
            pltpu.sync_copy(idx_hbm.at[:, pl.ds(off, SC_WINDOW)], idx_v)
            copies = [pltpu.async_copy(rows_v, out_hbm.at[idx_v.at[k]], sem) for k in range(nk)]
            for cp in copies:
                cp.wait()

    return pl.kernel(
        body,
        out_type=jax.ShapeDtypeStruct((n_out, width), rows.dtype),
        mesh=mesh,
        scratch_types=[pltpu.VMEM((nk, SC_WINDOW), jnp.int32),
                       pltpu.VMEM((SC_WINDOW, width), rows.dtype),
                       pltpu.SemaphoreType.DMA],
        name="sc_scatter",
    )(rows, idx_kt)


def _expert_kernel(first_ref, count_ref, used_ref, w1_ref, w3_ref, w2_ref, xs_ref, ys_ref,
                   xbuf, ybuf, sem_in, sem_out, w1f, w3f, w2f, sem_w, w1b, w3b, w2b, *, layer):
    e = pl.program_id(0)
    ne = pl.num_programs(0)
    r = xbuf.shape[1]
    n_used = used_ref[0]

    def x_copy(g, slot):
        return pltpu.make_async_copy(xs_ref.at[pl.ds(pl.multiple_of(g * r, r), r), :], xbuf.at[slot], sem_in.at[slot])

    def y_copy(g, slot):
        return pltpu.make_async_copy(ybuf.at[slot], ys_ref.at[pl.ds(pl.multiple_of(g * r, r), r), :], sem_out.at[slot])

    def w_copies(ex, slot):
        return [pltpu.make_async_copy(src.at[layer, ex], dst.at[slot], sem_w.at[slot])
                for src, dst in ((w1_ref, w1f), (w3_ref, w3f), (w2_ref, w2f))]

    nin = xbuf.shape[0]
    nout = ybuf.shape[0]

    @pl.when(e == 0)
    def _first_reads():
        for g in range(nin - 1):
            @pl.when(g < n_used)
            def _(g=g):
                x_copy(g, g).start()
        for cp in w_copies(0, 0):
            cp.start(priority=1)

    @pl.when(e + 1 < ne)
    def _next_weights():
        for cp in w_copies(e + 1, (e + 1) & 1):
            cp.start(priority=1)

    for cp in w_copies(e, e & 1):
        cp.wait()

    n = count_ref[e]

    @pl.when(n > 0)
    def _cast_weights():
        w1b[...] = w1f[e & 1].astype(BF16)
        w3b[...] = w3f[e & 1].astype(BF16)
        w2b[...] = w2f[e & 1].astype(BF16)

    def tile_body(g, carry):
        slot = g % nin
        oslot = g % nout
        x_copy(g, slot).wait()

        @pl.when(g + nin - 1 < n_used)
        def _read_ahead():
            x_copy(g + nin - 1, (g + nin - 1) % nin).start()

        @pl.when(g >= nout)
        def _free_out_slot():
            y_copy(g - nout, oslot).wait()

        lo, hi = _unpack_rows(xbuf[slot])
        lo, hi = lo.astype(BF16), hi.astype(BF16)
        a = (jnp.dot(lo, w1b[:HALF, :], preferred_element_type=F32)
             + jnp.dot(hi, w1b[HALF:, :], preferred_element_type=F32))
        u = (jnp.dot(lo, w3b[:HALF, :], preferred_element_type=F32)
             + jnp.dot(hi, w3b[HALF:, :], preferred_element_type=F32))
        y = jnp.dot((_silu(a) * u).astype(BF16), w2b[...], preferred_element_type=F32)
        ybuf[oslot] = _pack_rows(y)
        y_copy(g, oslot).start()
        return carry

    g0 = first_ref[e]
    lax.fori_loop(g0, g0 + n, tile_body, 0)

    @pl.when(e == ne - 1)
    def _drain_writes():
        for back in range(nout, 0, -1):
            @pl.when(n_used >= back)
            def _(back=back):
                y_copy(n_used - back, (n_used - back) % nout).wait()


def _experts(tile_first, tile_count, n_used, xs, w1, w3, w2, layer):
    n_rows = xs.shape[0]
    r = EXPERT_TILE
    any_spec = pl.BlockSpec(memory_space=pl.ANY)
    return pl.pallas_call(
        functools.partial(_expert_kernel, layer=layer),
        grid_spec=pltpu.PrefetchScalarGridSpec(
            num_scalar_prefetch=3,
            grid=(N_EXPERTS,),
            in_specs=[any_spec, any_spec, any_spec, any_spec],
            out_specs=any_spec,
            scratch_shapes=[pltpu.VMEM((EXPERT_IN_RING, r, HALF), jnp.uint32),
                            pltpu.VMEM((EXPERT_OUT_RING, r, HALF), jnp.uint32),
                            pltpu.SemaphoreType.DMA((EXPERT_IN_RING,)),
                            pltpu.SemaphoreType.DMA((EXPERT_OUT_RING,)),
                            pltpu.VMEM((2, D_MODEL, EXPERT_FF), F32),
                            pltpu.VMEM((2, D_MODEL, EXPERT_FF), F32),
                            pltpu.VMEM((2, EXPERT_FF, D_MODEL), F32),
                            pltpu.SemaphoreType.DMA((2,)),
                            pltpu.VMEM((D_MODEL, EXPERT_FF), BF16),
                            pltpu.VMEM((D_MODEL, EXPERT_FF), BF16),
                            pltpu.VMEM((EXPERT_FF, D_MODEL), BF16)],
        ),
        out_shape=jax.ShapeDtypeStruct((n_rows, HALF), jnp.uint32),
        compiler_params=_params(1),
        name="experts",
    )(tile_first, tile_count, n_used, w1, w3, w2, xs)


def _combine_dense_kernel(base_ref, g2_ref, w_ref, yg_ref, o_ref):
    acc_lo = acc_hi = None
    for k in range(TOP_K):
        lo, hi = _unpack_rows(yg_ref[k])
        wk = w_ref[:, k:k + 1]
        acc_lo = wk * lo if acc_lo is None else acc_lo + wk * lo
        acc_hi = wk * hi if acc_hi is None else acc_hi + wk * hi
    o_ref[:, :HALF] = base_ref[:, :HALF] + g2_ref[:, :HALF] * acc_lo
    o_ref[:, HALF:] = base_ref[:, HALF:] + g2_ref[:, HALF:] * acc_hi


def _combine_dense(base, g2, w_tok, yg, seq):
    t = base.shape[0]
    nt = ROUTE_TILE
    tpb = seq // nt
    return pl.pallas_call(
        _combine_dense_kernel,
        grid=(t // nt,),
        in_specs=[pl.BlockSpec((nt, D_MODEL), lambda i: (i, 0)),
                  pl.BlockSpec((None, 1, D_MODEL), lambda i: (i // tpb, 0, 0)),
                  pl.BlockSpec((nt, TOP_K), lambda i: (i, 0)),
                  pl.BlockSpec((TOP_K, nt, HALF), lambda i: (0, i, 0))],
        out_specs=pl.BlockSpec((nt, D_MODEL), lambda i: (i, 0)),
        out_shape=jax.ShapeDtypeStruct((t, D_MODEL), F32),
        compiler_params=_params(1),
        name="combine_dense",
    )(base, g2, w_tok, yg)


def _layer(layer, x, c, w_ada, b_ada, norm1_w, norm2_w, w_in, q_norm_w, k_norm_w, rel_bias, w_alpha, b_alpha,
           moba_out_w, gla_out_w, w_out, w_router, e_bias, w1, w3, w2, ws1, ws3, ws2):
    b, s, d = x.shape
    t = b * s
    x2 = x.reshape(t, d)

    mod = _mod(c, w_ada, b_ada)
    sh1, sc1, g1, sh2, sc2, g2 = [mod[:, j * d:(j + 1) * d].reshape(b, 1, d) for j in range(6)]

    w_main = w_in[:, :D_MAIN].astype(BF16)
    w_ga = jnp.zeros((d, LANES), BF16).at[:, :GLA_GATE_RANK].set(w_in[:, D_MAIN:].astype(BF16))
    per_chunk = 256 // MOBA_HEAD_DIM
    qw = jnp.tile(q_norm_w.astype(F32), per_chunk).reshape(1, 256) * (MOBA_HEAD_DIM ** -0.5)
    kw = jnp.tile(k_norm_w.astype(F32), per_chunk).reshape(1, 256)
    proj, ga = _inproj(x2, sc1, sh1, norm1_w.reshape(1, d), w_main, w_ga, qw, kw, s)
    proj3 = proj.reshape(b, s, D_MAIN)

    near, far = _moba_bias_tables(rel_bias)
    ow = jnp.tile(moba_out_w.astype(F32), 2).reshape(1, LANES)
    o_a = _moba(proj3, near, far, ow)

    wal = jnp.zeros((LANES, GLA_KEY_WIDTH), F32).at[:GLA_GATE_RANK].set(w_alpha)
    o_b = _gla(proj3, ga.reshape(b, s, LANES), wal, b_alpha.reshape(1, GLA_KEY_WIDTH),
               gla_out_w.reshape(1, GLA_DV))

    base, h2, scores_t = _outproj(
        o_a.reshape(t, MOBA_WIDTH), o_b.reshape(t, GLA_WIDTH), x2, g1, sc2, sh2, g2,
        norm2_w.reshape(1, d), w_out.astype(BF16), ws1.astype(BF16), ws3.astype(BF16), ws2.astype(BF16),
        w_router.T.astype(BF16), s)

    eb = jnp.broadcast_to(e_bias.astype(F32)[:, None], (N_EXPERTS, ROUTE_TILE))
    code_t, w_t, counts = _route(scores_t, eb)

    r = EXPERT_TILE
    n_tiles = (t * TOP_K + N_EXPERTS * (r - 1) + r - 1) // r
    n_rows = n_tiles * r
    cnt = counts[:, 0].astype(jnp.int32)
    padded = (cnt + r - 1) // r * r
    pend = jnp.cumsum(padded)
    pstart = pend - padded
    n_used = (pend[-1:] // r).astype(jnp.int32)
    dest_t = _slots(pstart, code_t)

    xs = _sc_scatter_rows(h2, dest_t, n_rows)
    ys = _experts(pstart // r, padded // r, n_used, xs, w1, w3, w2, layer)
    yg = _sc_gather_rows(ys, dest_t.reshape(TOP_K * t)).reshape(TOP_K, t, HALF)
    out = _combine_dense(base, g2, w_t.T, yg, s)
    return out.reshape(b, s, d)


def kernel(x, c, w_ada, b_ada, norm1_w, norm2_w, w_in, q_norm_w, k_norm_w, rel_bias, w_alpha, b_alpha,
           moba_out_w, gla_out_w, w_out, w_router, e_bias, w1, w3, w2, ws1, ws3, ws2):
    for l in range(w_ada.shape[0]):
        x = _layer(l, x, c, w_ada[l], b_ada[l], norm1_w[l], norm2_w[l], w_in[l], q_norm_w[l], k_norm_w[l],
                   rel_bias, w_alpha[l], b_alpha[l], moba_out_w[l], gla_out_w[l], w_out[l], w_router[l],
                   e_bias[l], w1, w3, w2, ws1[l], ws3[l], ws2[l])
    return x
```

```python
import functools
import math

import numpy as np
import jax
import jax.numpy as jnp
from jax import lax
from jax.experimental import pallas as pl
from jax.experimental.pallas import tpu as pltpu

D_MODEL = 1024
MOBA_HEADS = 8
MOBA_HEAD_DIM = 64
MOBA_WIDTH = MOBA_HEADS * MOBA_HEAD_DIM
MOBA_BLOCK = 256
MOBA_TOPK = 3
GLA_HEADS = 4
GLA_DK = 64
GLA_DV = 128
GLA_KEY_WIDTH = GLA_HEADS * GLA_DK
GLA_WIDTH = GLA_HEADS * GLA_DV
GLA_GATE_RANK = 16
GLA_GATE_TAU = 16.0
GLA_CHUNK = 64
REL_BUCKETS = 32
REL_MAX_DIST = 128
N_EXPERTS = 256
TOP_K = 8
N_GROUPS = 8
TOPK_GROUPS = 4
GROUP_SIZE = N_EXPERTS // N_GROUPS
EXPERT_FF = 256
SHARED_FF = 256
ROUTED_SCALE = 2.5
NORM_EPS = 1e-6

D_MAIN = 3 * MOBA_WIDTH + 2 * GLA_KEY_WIDTH + 2 * GLA_WIDTH
LANES = 128
VMEM_LIMIT = 56 * 1024 * 1024

ROW_TILE = 512
ROUTE_TILE = 256
EXPERT_TILE = 256
EXPERT_IN_RING = 4
EXPERT_OUT_RING = 3

F32 = jnp.float32
BF16 = jnp.bfloat16
NT_DIMS = (((1,), (1,)), ((), ()))
TN_DIMS = (((0,), (0,)), ((), ()))


def _params(n_axes):
    return pltpu.CompilerParams(dimension_semantics=("arbitrary",) * n_axes,
                                vmem_limit_bytes=VMEM_LIMIT)


def _silu(v):
    return v * jax.nn.sigmoid(v)


def _mod_kernel(c_ref, w_ref, b_ref, o_ref):
    o_ref[...] = jnp.dot(_silu(c_ref[...]), w_ref[...], preferred_element_type=F32) + b_ref[...]


def _mod(c, w, b):
    rows = 8
    cp = jnp.zeros((rows, D_MODEL), F32).at[:c.shape[0]].set(c)
    n = w.shape[1]
    tn = 1024
    out = pl.pallas_call(
        _mod_kernel,
        grid=(n // tn,),
        in_specs=[pl.BlockSpec((rows, D_MODEL), lambda j: (0, 0)),
                  pl.BlockSpec((D_MODEL, tn), lambda j: (0, j)),
                  pl.BlockSpec((1, tn), lambda j: (0, j))],
        out_specs=pl.BlockSpec((rows, tn), lambda j: (0, j)),
        out_shape=jax.ShapeDtypeStruct((rows, n), F32),
        compiler_params=_params(1),
        name="mod",
    )(cp, w, b.reshape(1, n))
    return out[:c.shape[0]]


def _group_rms_inv(a, group):
    lane = lax.broadcasted_iota(jnp.int32, (1, a.shape[1]), 1)
    a2 = a * a
    inv = jnp.zeros_like(a)
    for g in range(a.shape[1] // group):
        m = (lane >= g * group) & (lane < (g + 1) * group)
        ss = jnp.sum(jnp.where(m, a2, 0.0), axis=-1, keepdims=True)
        inv = jnp.where(m, lax.rsqrt(ss * (1.0 / group) + NORM_EPS), inv)
    return inv


def _inproj_kernel(x_ref, sc_ref, sh_ref, nw_ref, w_ref, wga_ref, qw_ref, kw_ref, o_ref, ga_ref):
    x = x_ref[...]
    ms = jnp.mean(x * x, axis=-1, keepdims=True)
    h = x * lax.rsqrt(ms + NORM_EPS) * nw_ref[...]
    h = h * (1.0 + sc_ref[...]) + sh_ref[...]
    hb = h.astype(BF16)
    cw = 256
    for j in range(D_MAIN // cw):
        acc = jnp.dot(hb, w_ref[:, j * cw:(j + 1) * cw], preferred_element_type=F32)
        if j < 2 * MOBA_WIDTH // cw:
            nw = qw_ref if j < MOBA_WIDTH // cw else kw_ref
            acc = acc * _group_rms_inv(acc, MOBA_HEAD_DIM) * nw[...]
        o_ref[:, j * cw:(j + 1) * cw] = acc.astype(BF16)
    ga_ref[...] = jnp.dot(hb, wga_ref[...], preferred_element_type=F32)


def _inproj(x2, sc, sh, nw, w_main, w_ga, qw, kw, seq):
    t = x2.shape[0]
    tpb = seq // ROW_TILE
    vec = lambda: pl.BlockSpec((None, 1, D_MODEL), lambda i: (i // tpb, 0, 0))
    full = lambda a: pl.BlockSpec(a.shape, lambda i: (0,) * a.ndim)
    return pl.pallas_call(
        _inproj_kernel,
        grid=(t // ROW_TILE,),
        in_specs=[pl.BlockSpec((ROW_TILE, D_MODEL), lambda i: (i, 0)), vec(), vec(),
                  full(nw), full(w_main), full(w_ga), full(qw), full(kw)],
        out_specs=[pl.BlockSpec((ROW_TILE, D_MAIN), lambda i: (i, 0)),
                   pl.BlockSpec((ROW_TILE, LANES), lambda i: (i, 0))],
        out_shape=[jax.ShapeDtypeStruct((t, D_MAIN), BF16),
                   jax.ShapeDtypeStruct((t, LANES), F32)],
        compiler_params=_params(1),
        name="inproj",
    )(x2, sc, sh, nw, w_main, w_ga, qw, kw)


def _t5_bucket_np(rel):
    max_exact = REL_BUCKETS // 2
    relf = np.maximum(rel, 1).astype(np.float64)
    large = max_exact + (np.log(relf / max_exact) / math.log(REL_MAX_DIST / max_exact)
                         * (REL_BUCKETS - max_exact)).astype(np.int32)
    large = np.minimum(large, REL_BUCKETS - 1)
    return np.where(rel < max_exact, rel, large)


def _bias_kernel(rb_ref, idx_ref, o_ref):
    h = pl.program_id(0)
    idx = idx_ref[...]
    tab = jnp.full(idx.shape, -jnp.inf, F32)
    for bk in range(REL_BUCKETS):
        tab = jnp.where(idx == bk, rb_ref[bk * MOBA_HEADS + h], tab)
    o_ref[...] = tab


def _moba_bias_tables(rel_bias):
    j = np.arange(MOBA_BLOCK)[:, None]
    i = np.arange(MOBA_BLOCK)[None, :]
    own_idx = np.where(j <= i, _t5_bucket_np(np.maximum(i - j, 0)), -1)
    prev_idx = _t5_bucket_np(MOBA_BLOCK + i - j)
    idx = jnp.asarray(np.concatenate([prev_idx, own_idx], axis=0).astype(np.int32))
    assert int(_t5_bucket_np(np.array([MOBA_BLOCK + 1]))[0]) == REL_BUCKETS - 1
    rb = rel_bias.astype(F32)
    near = pl.pallas_call(
        _bias_kernel,
        grid=(MOBA_HEADS,),
        in_specs=[pl.BlockSpec(memory_space=pltpu.SMEM),
                  pl.BlockSpec(idx.shape, lambda h: (0, 0))],
        out_specs=pl.BlockSpec((None,) + idx.shape, lambda h: (h, 0, 0)),
        out_shape=jax.ShapeDtypeStruct((MOBA_HEADS,) + idx.shape, F32),
        compiler_params=_params(1),
        name="bias",
    )(rb.reshape(-1), idx)
    return near, rb[REL_BUCKETS - 1]


FAR_GROUP = 4


def _moba_kernel(far_ref, q_ref, k_ref, v_ref, near_ref, ow_ref, o_ref,
                 vt_ref, vtg_ref, acc_ref, m_ref, sel_ref, s_ref):
    hp = pl.program_id(1)
    i = pl.program_id(2)
    nblk = k_ref.shape[0] // MOBA_BLOCK
    ngrp = nblk // FAR_GROUP
    hd = MOBA_HEAD_DIM
    bs = MOBA_BLOCK
    lane = lax.broadcasted_iota(jnp.int32, (bs, LANES), 1)

    def split_heads(qb):
        zero = jnp.zeros_like(qb)
        return jnp.where(lane < hd, qb, zero), jnp.where(lane < hd, zero, qb)

    @pl.when(i == 0)
    def _prepare():
        row = lax.broadcasted_iota(jnp.int32, (LANES, bs), 0)
        kmeans = []
        for n in range(nblk):
            kb = k_ref[n * bs:(n + 1) * bs, :].astype(F32)
            kmeans.append(jnp.mean(kb, axis=0, keepdims=True))
            vt = v_ref[n * bs:(n + 1) * bs, :].astype(F32).T
            vt0 = jnp.where(row < hd, vt, 1.0).astype(BF16)
            vt1 = jnp.where(row < hd, 1.0, vt).astype(BF16)
            vt_ref[0, n] = vt0
            vt_ref[1, n] = vt1
            gcols = slice((n % FAR_GROUP) * bs, (n % FAR_GROUP + 1) * bs)
            vtg_ref[0, n // FAR_GROUP, :, gcols] = vt0
            vtg_ref[1, n // FAR_GROUP, :, gcols] = vt1
        kmean = jnp.concatenate(kmeans, axis=0)
        km_hi = kmean.astype(BF16)
        km_lo = (kmean - km_hi.astype(F32)).astype(BF16)
        blk = lax.broadcasted_iota(jnp.int32, (nblk, bs), 0)
        for ib in range(nblk):
            qparts = split_heads(q_ref[ib * bs:(ib + 1) * bs, :])
            for h in range(2):
                gt = (lax.dot_general(km_hi, qparts[h], NT_DIMS, preferred_element_type=F32)
                      + lax.dot_general(km_lo, qparts[h], NT_DIMS, preferred_element_type=F32))
                gt = jnp.where(blk < ib, gt, -jnp.inf)
                cnt = jnp.zeros(gt.shape, jnp.int32)
                for m in range(ib):
                    gm = gt[m:m + 1, :]
                    cnt = cnt + jnp.where((gm > gt) | ((gm == gt) & (blk > m)), 1, 0)
                keep = (blk < ib) & (cnt < MOBA_TOPK)
                sel_ref[0, h, ib] = jnp.where(keep, 1.0, 0.0)
                sel_ref[1, h, ib] = jnp.where(keep & (blk < ib - 1), 1.0, 0.0)

    qh = split_heads(q_ref[pl.ds(pl.multiple_of(i * bs, bs), bs), :])

    @pl.when(i == 0)
    def _own_block_only():
        kb = k_ref[0:bs, :]
        for h in range(2):
            s = lax.dot_general(kb, qh[h], NT_DIMS, preferred_element_type=F32) + near_ref[h, bs:2 * bs, :]
            m_new = jnp.max(s, axis=0, keepdims=True)
            p = jnp.exp(s - m_new).astype(BF16)
            acc_ref[h] = jnp.dot(vt_ref[h, 0], p, preferred_element_type=F32)
            m_ref[h] = m_new

    n_far = (i + FAR_GROUP - 2) // FAR_GROUP
    gk = FAR_GROUP * bs

    def far_scores(g):
        kb = k_ref[g * gk:(g + 1) * gk, :]
        for h in range(2):
            s_ref[g % 2, h] = lax.dot_general(kb, qh[h], NT_DIMS, preferred_element_type=F32)

    @pl.when(i >= 1)
    def _previous_and_own_block():
        kbs = (k_ref[pl.ds(pl.multiple_of((i - 1) * bs, bs), bs), :],
               k_ref[pl.ds(pl.multiple_of(i * bs, bs), bs), :])
        ss = [[lax.dot_general(kbs[w], qh[h], NT_DIMS, preferred_element_type=F32)
               + near_ref[h, w * bs:(w + 1) * bs, :] for w in range(2)] for h in range(2)]
        far_scores(0)
        ps, ms = [], []
        for h in range(2):
            s_prev, s_own = ss[h]
            keep = sel_ref[0, h, i, pl.ds(i - 1, 1), :] > 0.5
            mx = jnp.where(keep, jnp.max(s_prev, axis=0, keepdims=True), -jnp.inf)
            m_new = jnp.maximum(jnp.max(s_own, axis=0, keepdims=True), mx)
            ps.append((jnp.exp(s_prev - jnp.where(keep, m_new, jnp.inf)).astype(BF16),
                       jnp.exp(s_own - m_new).astype(BF16)))
            ms.append(m_new)
        for h in range(2):
            acc_ref[h] = (jnp.dot(vt_ref[h, i - 1], ps[h][0], preferred_element_type=F32)
                          + jnp.dot(vt_ref[h, i], ps[h][1], preferred_element_type=F32))
            m_ref[h] = ms[h]

    def far_group(g, with_next):
        if with_next:
            far_scores(g + 1)
        for h in range(2):
            fb = far_ref[2 * hp + h]
            m_old = m_ref[h]
            m_new = m_old
            keeps = []
            for j in range(FAR_GROUP):
                keep = sel_ref[1, h, i, pl.ds(g * FAR_GROUP + j, 1), :] > 0.5
                mx = jnp.max(s_ref[g % 2, h, j * bs:(j + 1) * bs, :], axis=0, keepdims=True) + fb
                m_new = jnp.maximum(m_new, jnp.where(keep, mx, -jnp.inf))
                keeps.append(keep)
            p = jnp.concatenate(
                [jnp.exp(s_ref[g % 2, h, j * bs:(j + 1) * bs, :]
                         - jnp.where(keeps[j], m_new - fb, jnp.inf)).astype(BF16)
                 for j in range(FAR_GROUP)], axis=0)
            pv = jnp.dot(vtg_ref[h, g], p, preferred_element_type=F32)
            acc_ref[h] = acc_ref[h] * jnp.exp(m_old - m_new) + pv
            m_ref[h] = m_new

    for g in range(ngrp):
        if g + 1 < ngrp:
            pl.when(g + 1 < n_far)(functools.partial(far_group, g, True))
        pl.when(g + 1 == n_far)(functools.partial(far_group, g, False))

    a0 = acc_ref[0]
    a1 = acc_ref[1]
    row = lax.broadcasted_iota(jnp.int32, a0.shape, 0)
    ot = jnp.where(row < hd, a0 / a0[hd:hd + 1, :], a1 / a1[0:1, :])
    o2 = ot * ot
    ss0 = jnp.sum(jnp.where(row < hd, o2, 0.0), axis=0, keepdims=True)
    ss1 = jnp.sum(jnp.where(row < hd, 0.0, o2), axis=0, keepdims=True)
    inv = jnp.where(row < hd, lax.rsqrt(ss0 * (1.0 / hd) + NORM_EPS), lax.rsqrt(ss1 * (1.0 / hd) + NORM_EPS))
    o_ref[...] = ((ot * inv).T * ow_ref[...]).astype(o_ref.dtype)


def _moba(proj3, near, far, ow):
    b, s, _ = proj3.shape
    nblk = s // MOBA_BLOCK
    assert nblk % FAR_GROUP == 0
    npair = MOBA_HEADS // 2
    kcol = MOBA_WIDTH // LANES
    return pl.pallas_call(
        _moba_kernel,
        grid=(b, npair, nblk),
        in_specs=[pl.BlockSpec(memory_space=pltpu.SMEM),
                  pl.BlockSpec((None, s, LANES), lambda bb, hp, i: (bb, 0, hp)),
                  pl.BlockSpec((None, s, LANES), lambda bb, hp, i: (bb, 0, kcol + hp)),
                  pl.BlockSpec((None, s, LANES), lambda bb, hp, i: (bb, 0, 2 * kcol + hp)),
                  pl.BlockSpec((2, 2 * MOBA_BLOCK, MOBA_BLOCK), lambda bb, hp, i: (hp, 0, 0)),
                  pl.BlockSpec((1, LANES), lambda bb, hp, i: (0, 0))],
        out_specs=pl.BlockSpec((None, MOBA_BLOCK, LANES), lambda bb, hp, i: (bb, i, hp)),
        out_shape=jax.ShapeDtypeStruct((b, s, MOBA_WIDTH), BF16),
        scratch_shapes=[pltpu.VMEM((2, nblk, LANES, MOBA_BLOCK), BF16),
                        pltpu.VMEM((2, nblk // FAR_GROUP, LANES, FAR_GROUP * MOBA_BLOCK), BF16),
                        pltpu.VMEM((2, LANES, MOBA_BLOCK), F32),
                        pltpu.VMEM((2, 1, MOBA_BLOCK), F32),
                        pltpu.VMEM((2, 2, nblk, nblk, MOBA_BLOCK), F32),
                        pltpu.VMEM((2, 2, FAR_GROUP * MOBA_BLOCK, MOBA_BLOCK), F32)],
        compiler_params=_params(3),
        name="moba",
    )(far, proj3, proj3, proj3, near, ow)


def _split3(v):
    hi = v.astype(BF16)
    r1 = v - hi.astype(F32)
    mid = r1.astype(BF16)
    lo = (r1 - mid.astype(F32)).astype(BF16)
    return hi, mid, lo


GLA_UNROLL = 4


def _gla_kernel(q_ref, k_ref, v_ref, g_ref, ga_ref, wal_ref, bal_ref, gw_ref, o_ref, b_ref, st_ref):
    seq = q_ref.shape[0]
    c = GLA_CHUNK
    pc = 256

    rr = lax.broadcasted_iota(jnp.int32, (pc, pc), 0)
    cc = lax.broadcasted_iota(jnp.int32, (pc, pc), 1)
    tri = jnp.where((rr >= cc) & (rr // c == cc // c), 1.0, 0.0).astype(BF16)

    def decay_body(j, carry):
        rows = [pl.ds(pl.multiple_of((j * GLA_UNROLL + u) * pc, pc), pc) for u in range(GLA_UNROLL)]
        xg = [jnp.dot(ga_ref[r, :], wal_ref[...], preferred_element_type=F32) + bal_ref[...] for r in rows]
        parts = [_split3((jnp.minimum(x, 0.0) - jnp.log(1.0 + jnp.exp(-jnp.abs(x)))) * (1.0 / GLA_GATE_TAU))
                 for x in xg]
        sums = [[jnp.dot(tri, term, preferred_element_type=F32) for term in p] for p in parts]
        for r, (hi, mid, lo) in zip(rows, sums):
            b_ref[r, :] = hi + mid + lo
        return carry

    lax.fori_loop(0, seq // (pc * GLA_UNROLL), decay_body, 0)

    st_ref[...] = jnp.zeros_like(st_ref)
    lane = lax.broadcasted_iota(jnp.int32, (c, LANES), 1)
    head_mask = (lane < GLA_DK, lane >= GLA_DK)
    causal = lax.broadcasted_iota(jnp.int32, (c, c), 0) >= lax.broadcasted_iota(jnp.int32, (c, c), 1)

    units = [(u, h) for u in range(GLA_UNROLL) for h in range(2)]

    def chunk_body(ci, carry):
        rows = [pl.ds(pl.multiple_of((ci * GLA_UNROLL + u) * c, c), c) for u in range(GLA_UNROLL)]
        qt, kt, qs, ke, e_last = [], [], [], [], []
        for u in range(GLA_UNROLL):
            b = b_ref[rows[u], :]
            ref_row = b[c // 2 - 1:c // 2, :]
            last = b[c - 1:c, :]
            q = q_ref[rows[u], :].astype(F32) * (GLA_DK ** -0.5)
            k = k_ref[rows[u], :].astype(F32)
            qt.append(q * jnp.exp(b - ref_row))
            kt.append((k * jnp.exp(ref_row - b)).astype(BF16))
            qs.append(q * jnp.exp(b))
            ke.append((k * jnp.exp(last - b)).astype(BF16))
            e_last.append(jnp.exp(last))
        vs = {(u, h): v_ref[rows[u], h * GLA_DV:(h + 1) * GLA_DV] for u, h in units}
        a = {(u, h): lax.dot_general(jnp.where(head_mask[h], qt[u], 0.0).astype(BF16), kt[u], NT_DIMS,
                                     preferred_element_type=F32) for u, h in units}
        inc = {(u, h): lax.dot_general(vs[u, h], ke[u], TN_DIMS, preferred_element_type=F32) for u, h in units}
        o = {(u, h): jnp.dot(jnp.where(causal, a[u, h], 0.0).astype(BF16), vs[u, h], preferred_element_type=F32)
             for u, h in units}
        states = {}
        for h in range(2):
            st = st_ref[h]
            for u in range(GLA_UNROLL):
                states[u, h] = st
                st = st * e_last[u] + inc[u, h]
            st_ref[h] = st
        for u, h in units:
            cols = slice(h * GLA_DV, (h + 1) * GLA_DV)
            ou = o[u, h] + lax.dot_general(jnp.where(head_mask[h], qs[u], 0.0).astype(BF16),
                                           states[u, h].astype(BF16), NT_DIMS, preferred_element_type=F32)
            ms = jnp.mean(ou * ou, axis=-1, keepdims=True)
            on = ou * lax.rsqrt(ms + NORM_EPS) * gw_ref[...]
            g = g_ref[rows[u], cols].astype(F32)
            o_ref[rows[u], cols] = (on * _silu(g)).astype(o_ref.dtype)
        return carry

    lax.fori_loop(0, seq // (c * GLA_UNROLL), chunk_body, 0)


def _gla(proj3, ga3, wal, bal, gw):
    b, s, _ = proj3.shape
    npair = GLA_HEADS // 2
    qcol = 3 * MOBA_WIDTH // LANES
    kcol = qcol + GLA_KEY_WIDTH // LANES
    vcol = (3 * MOBA_WIDTH + 2 * GLA_KEY_WIDTH) // (2 * GLA_DV)
    gcol = vcol + npair
    return pl.pallas_call(
        _gla_kernel,
        grid=(b, npair),
        in_specs=[pl.BlockSpec((None, s, LANES), lambda bb, hp: (bb, 0, qcol + hp)),
                  pl.BlockSpec((None, s, LANES), lambda bb, hp: (bb, 0, kcol + hp)),
                  pl.BlockSpec((None, s, 2 * GLA_DV), lambda bb, hp: (bb, 0, vcol + hp)),
                  pl.BlockSpec((None, s, 2 * GLA_DV), lambda bb, hp: (bb, 0, gcol + hp)),
                  pl.BlockSpec((None, s, LANES), lambda bb, hp: (bb, 0, 0)),
                  pl.BlockSpec((LANES, LANES), lambda bb, hp: (0, hp)),
                  pl.BlockSpec((1, LANES), lambda bb, hp: (0, hp)),
                  pl.BlockSpec((1, GLA_DV), lambda bb, hp: (0, 0))],
        out_specs=pl.BlockSpec((None, s, 2 * GLA_DV), lambda bb, hp: (bb, 0, hp)),
        out_shape=jax.ShapeDtypeStruct((b, s, GLA_WIDTH), BF16),
        scratch_shapes=[pltpu.VMEM((s, LANES), F32),
                        pltpu.VMEM((2, GLA_DV, LANES), F32)],
        compiler_params=_params(2),
        name="gla",
    )(proj3, proj3, proj3, proj3, ga3, wal, bal, gw)


HALF = D_MODEL // 2


def _pack_rows(v):
    return pltpu.pack_elementwise([v[:, :HALF], v[:, HALF:]], packed_dtype=BF16)


def _unpack_rows(w):
    return (pltpu.unpack_elementwise(w, index=0, packed_dtype=BF16, unpacked_dtype=F32),
            pltpu.unpack_elementwise(w, index=1, packed_dtype=BF16, unpacked_dtype=F32))


def _outproj_kernel(oa_ref, ob_ref, x_ref, g1_ref, sc_ref, sh_ref, g2_ref, nw_ref, wo_ref,
                    ws1_ref, ws3_ref, ws2_ref, wrt_ref, base_ref, h_ref, st_ref):
    mix = (jnp.dot(oa_ref[...], wo_ref[:MOBA_WIDTH, :], preferred_element_type=F32)
           + jnp.dot(ob_ref[...], wo_ref[MOBA_WIDTH:, :], preferred_element_type=F32))
    x1 = x_ref[...] + g1_ref[...] * mix
    ms = jnp.mean(x1 * x1, axis=-1, keepdims=True)
    h = x1 * lax.rsqrt(ms + NORM_EPS) * nw_ref[...]
    h = h * (1.0 + sc_ref[...]) + sh_ref[...]
    h_ref[...] = _pack_rows(h)
    hb = h.astype(BF16)
    a = jnp.dot(hb, ws1_ref[...], preferred_element_type=F32)
    u = jnp.dot(hb, ws3_ref[...], preferred_element_type=F32)
    shared = jnp.dot((_silu(a) * u).astype(BF16), ws2_ref[...], preferred_element_type=F32)
    base_ref[...] = x1 + g2_ref[...] * shared
    logits_t = lax.dot_general(wrt_ref[...], hb, NT_DIMS, preferred_element_type=F32)
    st_ref[...] = jax.nn.sigmoid(logits_t)


def _outproj(oa, ob, x2, g1, sc, sh, g2, nw, wo, ws1, ws3, ws2, wrt, seq):
    t = x2.shape[0]
    tpb = seq // ROW_TILE
    vec = lambda: pl.BlockSpec((None, 1, D_MODEL), lambda i: (i // tpb, 0, 0))
    full = lambda a: pl.BlockSpec(a.shape, lambda i: (0,) * a.ndim)
    rows = lambda w: pl.BlockSpec((ROW_TILE, w), lambda i: (i, 0))
    return pl.pallas_call(
        _outproj_kernel,
        grid=(t // ROW_TILE,),
        in_specs=[rows(MOBA_WIDTH), rows(GLA_WIDTH), rows(D_MODEL), vec(), vec(), vec(), vec(),
                  full(nw), full(wo), full(ws1), full(ws3), full(ws2), full(wrt)],
        out_specs=[rows(D_MODEL), rows(HALF), pl.BlockSpec((N_EXPERTS, ROW_TILE), lambda i: (0, i))],
        out_shape=[jax.ShapeDtypeStruct((t, D_MODEL), F32),
                   jax.ShapeDtypeStruct((t, HALF), jnp.uint32),
                   jax.ShapeDtypeStruct((N_EXPERTS, t), F32)],
        compiler_params=_params(1),
        name="outproj",
    )(oa, ob, x2, g1, sc, sh, g2, nw, wo, ws1, ws3, ws2, wrt)


SLOT_CODE_SHIFT = 16
SLOT_CODE_BASE = 1 << SLOT_CODE_SHIFT


def _route_kernel(s_ref, eb_ref, code_ref, w_ref, cnt_ref, carry_ref):
    i = pl.program_id(0)
    ne, nt = s_ref.shape

    @pl.when(i == 0)
    def _init():
        carry_ref[...] = jnp.zeros_like(carry_ref)

    s = s_ref[...]
    choice = s + eb_ref[...]
    gio = lax.broadcasted_iota(jnp.int32, (GROUP_SIZE, nt), 0)
    gscore = []
    for g in range(N_GROUPS):
        cg = choice[g * GROUP_SIZE:(g + 1) * GROUP_SIZE, :]
        top1 = jnp.max(cg, axis=0, keepdims=True)
        first = jnp.min(jnp.where(cg == top1, gio, GROUP_SIZE), axis=0, keepdims=True)
        top2 = jnp.max(jnp.where(gio == first, -jnp.inf, cg), axis=0, keepdims=True)
        gscore.append(top1 + top2)
    gs = jnp.concatenate(gscore, axis=0)
    gidx = lax.broadcasted_iota(jnp.int32, gs.shape, 0)
    beaten = jnp.zeros(gs.shape, jnp.int32)
    for m in range(N_GROUPS):
        gm = gs[m:m + 1, :]
        beaten = beaten + jnp.where((gm > gs) | ((gm == gs) & (gidx > m)), 1, 0)
    gkeep = beaten < TOPK_GROUPS
    masked = jnp.concatenate(
        [jnp.where(gkeep[g:g + 1, :], choice[g * GROUP_SIZE:(g + 1) * GROUP_SIZE, :], -jnp.inf)
         for g in range(N_GROUPS)], axis=0)

    eio = lax.broadcasted_iota(jnp.int32, (ne, nt), 0)
    picked = jnp.zeros((ne, nt), F32)
    idx_rows, w_rows, hits = [], [], []
    for _ in range(TOP_K):
        mx = jnp.max(masked, axis=0, keepdims=True)
        idx = jnp.min(jnp.where(masked == mx, eio, ne), axis=0, keepdims=True)
        hit = eio == idx
        w_rows.append(jnp.sum(jnp.where(hit, s, 0.0), axis=0, keepdims=True))
        idx_rows.append(idx)
        hits.append(hit)
        masked = jnp.where(hit, -jnp.inf, masked)
        picked = jnp.where(hit, 1.0, picked)
    wk = jnp.concatenate(w_rows, axis=0)
    w_ref[...] = wk / jnp.sum(wk, axis=0, keepdims=True) * ROUTED_SCALE

    tr = lax.broadcasted_iota(jnp.int32, (nt, nt), 0)
    tc = lax.broadcasted_iota(jnp.int32, (nt, nt), 1)
    before = jnp.where(tr < tc, 1.0, 0.0).astype(BF16)
    pb = picked.astype(BF16)
    pos = carry_ref[...] + jnp.dot(pb, before, preferred_element_type=F32)
    rank = jnp.concatenate(
        [jnp.sum(jnp.where(hit, pos, 0.0), axis=0, keepdims=True) for hit in hits], axis=0).astype(jnp.int32)
    code_ref[...] = jnp.concatenate(idx_rows, axis=0) * SLOT_CODE_BASE + rank
    total = carry_ref[...] + jnp.dot(pb, jnp.ones((nt, nt), BF16), preferred_element_type=F32)
    carry_ref[...] = total
    cnt_ref[...] = total


def _route(scores_t, eb):
    ne, t = scores_t.shape
    assert t <= SLOT_CODE_BASE
    nt = ROUTE_TILE
    tok = lambda dt: jax.ShapeDtypeStruct((TOP_K, t), dt)
    return pl.pallas_call(
        _route_kernel,
        grid=(t // nt,),
        in_specs=[pl.BlockSpec((ne, nt), lambda i: (0, i)),
                  pl.BlockSpec((ne, nt), lambda i: (0, 0))],
        out_specs=[pl.BlockSpec((TOP_K, nt), lambda i: (0, i)),
                   pl.BlockSpec((TOP_K, nt), lambda i: (0, i)),
                   pl.BlockSpec((ne, nt), lambda i: (0, 0))],
        out_shape=[tok(jnp.int32), tok(F32), jax.ShapeDtypeStruct((ne, nt), F32)],
        scratch_shapes=[pltpu.VMEM((ne, nt), F32)],
        compiler_params=_params(1),
        name="route",
    )(scores_t, eb)


SLOT_TILE = 2048


def _slots_kernel(pstart_ref, code_ref, o_ref):
    code = code_ref[...]
    expert = lax.shift_right_logical(code, SLOT_CODE_SHIFT)

    def body(e, acc):
        return jnp.where(expert == e, pstart_ref[e], acc)

    start = lax.fori_loop(0, N_EXPERTS, body, jnp.zeros_like(code), unroll=8)
    o_ref[...] = start + (code & (SLOT_CODE_BASE - 1))


def _slots(pstart, code_t):
    k, t = code_t.shape
    return pl.pallas_call(
        _slots_kernel,
        grid_spec=pltpu.PrefetchScalarGridSpec(
            num_scalar_prefetch=1,
            grid=(t // SLOT_TILE,),
            in_specs=[pl.BlockSpec((k, SLOT_TILE), lambda i, p: (0, i))],
            out_specs=pl.BlockSpec((k, SLOT_TILE), lambda i, p: (0, i)),
        ),
        out_shape=jax.ShapeDtypeStruct((k, t), jnp.int32),
        compiler_params=_params(1),
        name="slots",
    )(pstart, code_t)


SC_WINDOW = 128


def _sc_gather_rows(table, idx_flat):
    from jax.experimental.pallas import tpu_sc as plsc
    info = plsc.get_sparse_core_info()
    nw = info.num_cores * info.num_subcores
    n = idx_flat.shape[0]
    width = table.shape[1]
    per_worker = n // nw
    assert per_worker * nw == n and per_worker % SC_WINDOW == 0
    mesh = plsc.VectorSubcoreMesh(core_axis_name="c", subcore_axis_name="s")

    def body(table_hbm, idx_hbm, out_hbm, idx_v, rows_v, sem):
        wid = lax.axis_index("s") * info.num_cores + lax.axis_index("c")
        base = wid * per_worker

        @pl.loop(0, per_worker // SC_WINDOW)
        def _(w):
            off = pl.multiple_of(base + w * SC_WINDOW, SC_WINDOW)
            pltpu.sync_copy(idx_hbm.at[pl.ds(off, SC_WINDOW)], idx_v)
            pltpu.async_copy(table_hbm.at[idx_v], rows_v, sem).wait()
            pltpu.sync_copy(rows_v, out_hbm.at[pl.ds(off, SC_WINDOW)])

    return pl.kernel(
        body,
        out_type=jax.ShapeDtypeStruct((n, width), table.dtype),
        mesh=mesh,
        scratch_types=[pltpu.VMEM((SC_WINDOW,), jnp.int32),
                       pltpu.VMEM((SC_WINDOW, width), table.dtype),
                       pltpu.SemaphoreType.DMA],
        name="sc_gather",
    )(table, idx_flat)


def _sc_scatter_rows(rows, idx_kt, n_out):
    from jax.experimental.pallas import tpu_sc as plsc
    info = plsc.get_sparse_core_info()
    nw = info.num_cores * info.num_subcores
    t, width = rows.shape
    nk = idx_kt.shape[0]
    per_worker = t // nw
    assert per_worker * nw == t and per_worker % SC_WINDOW == 0
    mesh = plsc.VectorSubcoreMesh(core_axis_name="c", subcore_axis_name="s")

    def body(rows_hbm, idx_hbm, out_hbm, idx_v, rows_v, sem):
        wid = lax.axis_index("s") * info.num_cores + lax.axis_index("c")
        base = wid * per_worker

        @pl.loop(0, per_worker // SC_WINDOW)
        def _(w):
            off = pl.multiple_of(base + w * SC_WINDOW, SC_WINDOW)
            pltpu.sync_copy(rows_hbm.at[pl.ds(off, SC_WINDOW)], rows_v)
            pltpu.sync_copy(idx_hbm.at[:, pl.ds(off, SC_WINDOW)], idx_v)
            copies = [pltpu.async_copy(rows_v, out_hbm.at[idx_v.at[k]], sem) for k in range(nk)]
            for cp in copies:
                cp.wait()

    return pl.kernel(
        body,
        out_type=jax.ShapeDtypeStruct((n_out, width), rows.dtype),
        mesh=mesh,
        scratch_types=[pltpu.VMEM((nk, SC_WINDOW), jnp.int32),
                       pltpu.VMEM((SC_WINDOW, width), rows.dtype),
                       pltpu.SemaphoreType.DMA],
        name="sc_scatter",
    )(rows, idx_kt)


def _expert_kernel(first_ref, count_ref, used_ref, w1_ref, w3_ref, w2_ref, xs_ref, ys_ref,
                   xbuf, ybuf, sem_in, sem_out, w1f, w3f, w2f, sem_w, w1b, w3b, w2b, *, layer):
    e = pl.program_id(0)
    ne = pl.num_programs(0)
    r = xbuf.shape[1]
    n_used = used_ref[0]

    def x_copy(g, slot):
        return pltpu.make_async_copy(xs_ref.at[pl.ds(pl.multiple_of(g * r, r), r), :], xbuf.at[slot], sem_in.at[slot])

    def y_copy(g, slot):
        return pltpu.make_async_copy(ybuf.at[slot], ys_ref.at[pl.ds(pl.multiple_of(g * r, r), r), :], sem_out.at[slot])

    def w_copies(ex, slot):
        return [pltpu.make_async_copy(src.at[layer, ex], dst.at[slot], sem_w.at[slot])
                for src, dst in ((w1_ref, w1f), (w3_ref, w3f), (w2_ref, w2f))]

    nin = xbuf.shape[0]
    nout = ybuf.shape[0]

    @pl.when(e == 0)
    def _first_reads():
        for g in range(nin - 1):
            @pl.when(g < n_used)
            def _(g=g):
                x_copy(g, g).start()
        for cp in w_copies(0, 0):
            cp.start(priority=1)

    @pl.when(e + 1 < ne)
    def _next_weights():
        for cp in w_copies(e + 1, (e + 1) & 1):
            cp.start(priority=1)

    for cp in w_copies(e, e & 1):
        cp.wait()

    n = count_ref[e]

    @pl.when(n > 0)
    def _cast_weights():
        w1b[...] = w1f[e & 1].astype(BF16)
        w3b[...] = w3f[e & 1].astype(BF16)
        w2b[...] = w2f[e & 1].astype(BF16)

    def tile_body(g, carry):
        slot = g % nin
        oslot = g % nout
        x_copy(g, slot).wait()

        @pl.when(g + nin - 1 < n_used)
        def _read_ahead():
            x_copy(g + nin - 1, (g + nin - 1) % nin).start()

        @pl.when(g >= nout)
        def _free_out_slot():
            y_copy(g - nout, oslot).wait()

        lo, hi = _unpack_rows(xbuf[slot])
        lo, hi = lo.astype(BF16), hi.astype(BF16)
        a = (jnp.dot(lo, w1b[:HALF, :], preferred_element_type=F32)
             + jnp.dot(hi, w1b[HALF:, :], preferred_element_type=F32))
        u = (jnp.dot(lo, w3b[:HALF, :], preferred_element_type=F32)
             + jnp.dot(hi, w3b[HALF:, :], preferred_element_type=F32))
        y = jnp.dot((_silu(a) * u).astype(BF16), w2b[...], preferred_element_type=F32)
        ybuf[oslot] = _pack_rows(y)
        y_copy(g, oslot).start()
        return carry

    g0 = first_ref[e]
    lax.fori_loop(g0, g0 + n, tile_body, 0)

    @pl.when(e == ne - 1)
    def _drain_writes():
        for back in range(nout, 0, -1):
            @pl.when(n_used >= back)
            def _(back=back):
                y_copy(n_used - back, (n_used - back) % nout).wait()


def _experts(tile_first, tile_count, n_used, xs, w1, w3, w2, layer):
    n_rows = xs.shape[0]
    r = EXPERT_TILE
    any_spec = pl.BlockSpec(memory_space=pl.ANY)
    return pl.pallas_call(
        functools.partial(_expert_kernel, layer=layer),
        grid_spec=pltpu.PrefetchScalarGridSpec(
            num_scalar_prefetch=3,
            grid=(N_EXPERTS,),
            in_specs=[any_spec, any_spec, any_spec, any_spec],
            out_specs=any_spec,
            scratch_shapes=[pltpu.VMEM((EXPERT_IN_RING, r, HALF), jnp.uint32),
                            pltpu.VMEM((EXPERT_OUT_RING, r, HALF), jnp.uint32),
                            pltpu.SemaphoreType.DMA((EXPERT_IN_RING,)),
                            pltpu.SemaphoreType.DMA((EXPERT_OUT_RING,)),
                            pltpu.VMEM((2, D_MODEL, EXPERT_FF), F32),
                            pltpu.VMEM((2, D_MODEL, EXPERT_FF), F32),
                            pltpu.VMEM((2, EXPERT_FF, D_MODEL), F32),
                            pltpu.SemaphoreType.DMA((2,)),
                            pltpu.VMEM((D_MODEL, EXPERT_FF), BF16),
                            pltpu.VMEM((D_MODEL, EXPERT_FF), BF16),
                            pltpu.VMEM((EXPERT_FF, D_MODEL), BF16)],
        ),
        out_shape=jax.ShapeDtypeStruct((n_rows, HALF), jnp.uint32),
        compiler_params=_params(1),
        name="experts",
    )(tile_first, tile_count, n_used, w1, w3, w2, xs)


def _combine_dense_kernel(base_ref, g2_ref, w_ref, yg_ref, o_ref):
    acc_lo = acc_hi = None
    for k in range(TOP_K):
        lo, hi = _unpack_rows(yg_ref[k])
        wk = w_ref[:, k:k + 1]
        acc_lo = wk * lo if acc_lo is None else acc_lo + wk * lo
        acc_hi = wk * hi if acc_hi is None else acc_hi + wk * hi
    o_ref[:, :HALF] = base_ref[:, :HALF] + g2_ref[:, :HALF] * acc_lo
    o_ref[:, HALF:] = base_ref[:, HALF:] + g2_ref[:, HALF:] * acc_hi


def _combine_dense(base, g2, w_tok, yg, seq):
    t = base.shape[0]
    nt = ROUTE_TILE
    tpb = seq // nt
    return pl.pallas_call(
        _combine_dense_kernel,
        grid=(t // nt,),
        in_specs=[pl.BlockSpec((nt, D_MODEL), lambda i: (i, 0)),
                  pl.BlockSpec((None, 1, D_MODEL), lambda i: (i // tpb, 0, 0)),
                  pl.BlockSpec((nt, TOP_K), lambda i: (i, 0)),
                  pl.BlockSpec((TOP_K, nt, HALF), lambda i: (0, i, 0))],
        out_specs=pl.BlockSpec((nt, D_MODEL), lambda i: (i, 0)),
        out_shape=jax.ShapeDtypeStruct((t, D_MODEL), F32),
        compiler_params=_params(1),
        name="combine_dense",
    )(base, g2, w_tok, yg)


def _layer(layer, x, c, w_ada, b_ada, norm1_w, norm2_w, w_in, q_norm_w, k_norm_w, rel_bias, w_alpha, b_alpha,
           moba_out_w, gla_out_w, w_out, w_router, e_bias, w1, w3, w2, ws1, ws3, ws2):
    b, s, d = x.shape
    t = b * s
    x2 = x.reshape(t, d)

    mod = _mod(c, w_ada, b_ada)
    sh1, sc1, g1, sh2, sc2, g2 = [mod[:, j * d:(j + 1) * d].reshape(b, 1, d) for j in range(6)]

    w_main = w_in[:, :D_MAIN].astype(BF16)
    w_ga = jnp.zeros((d, LANES), BF16).at[:, :GLA_GATE_RANK].set(w_in[:, D_MAIN:].astype(BF16))
    per_chunk = 256 // MOBA_HEAD_DIM
    qw = jnp.tile(q_norm_w.astype(F32), per_chunk).reshape(1, 256) * (MOBA_HEAD_DIM ** -0.5)
    kw = jnp.tile(k_norm_w.astype(F32), per_chunk).reshape(1, 256)
    proj, ga = _inproj(x2, sc1, sh1, norm1_w.reshape(1, d), w_main, w_ga, qw, kw, s)
    proj3 = proj.reshape(b, s, D_MAIN)

    near, far = _moba_bias_tables(rel_bias)
    ow = jnp.tile(moba_out_w.astype(F32), 2).reshape(1, LANES)
    o_a = _moba(proj3, near, far, ow)

    wal = jnp.zeros((LANES, GLA_KEY_WIDTH), F32).at[:GLA_GATE_RANK].set(w_alpha)
    o_b = _gla(proj3, ga.reshape(b, s, LANES), wal, b_alpha.reshape(1, GLA_KEY_WIDTH),
               gla_out_w.reshape(1, GLA_DV))

    base, h2, scores_t = _outproj(
        o_a.reshape(t, MOBA_WIDTH), o_b.reshape(t, GLA_WIDTH), x2, g1, sc2, sh2, g2,
        norm2_w.reshape(1, d), w_out.astype(BF16), ws1.astype(BF16), ws3.astype(BF16), ws2.astype(BF16),
        w_router.T.astype(BF16), s)

    eb = jnp.broadcast_to(e_bias.astype(F32)[:, None], (N_EXPERTS, ROUTE_TILE))
    code_t, w_t, counts = _route(scores_t, eb)

    r = EXPERT_TILE
    n_tiles = (t * TOP_K + N_EXPERTS * (r - 1) + r - 1) // r
    n_rows = n_tiles * r
    cnt = counts[:, 0].astype(jnp.int32)
    padded = (cnt + r - 1) // r * r
    pend = jnp.cumsum(padded)
    pstart = pend - padded
    n_used = (pend[-1:] // r).astype(jnp.int32)
    dest_t = _slots(pstart, code_t)

    xs = _sc_scatter_rows(h2, dest_t, n_rows)
    ys = _experts(pstart // r, padded // r, n_used, xs, w1, w3, w2, layer)
    yg = _sc_gather_rows(ys, dest_t.reshape(TOP_K * t)).reshape(TOP_K, t, HALF)
    out = _combine_dense(base, g2, w_t.T, yg, s)
    return out.reshape(b, s, d)


def kernel(x, c, w_ada, b_ada, norm1_w, norm2_w, w_in, q_norm_w, k_norm_w, rel_bias, w_alpha, b_alpha,
           moba_out_w, gla_out_w, w_out, w_router, e_bias, w1, w3, w2, ws1, ws3, ws2):
    for l in range(w_ada.shape[0]):
        x = _layer(l, x, c, w_ada[l], b_ada[l], norm1_w[l], norm2_w[l], w_in[l], q_norm_w[l], k_norm_w[l],
                   rel_bias, w_alpha[l], b_alpha[l], moba_out_w[l], gla_out_w[l], w_out[l], w_router[l],
                   e_bias[l], w1, w3, w2, ws1[l], ws3[l], ws2[l])
    return x
```

```python
import functools
import math

import numpy as np
import jax
import jax.numpy as jnp
from jax import lax
from jax.experimental import pallas as pl
from jax.experimental.pallas import tpu as pltpu

D_MODEL = 1024
MOBA_HEADS = 8
MOBA_HEAD_DIM = 64
MOBA_WIDTH = MOBA_HEADS * MOBA_HEAD_DIM
MOBA_BLOCK = 256
MOBA_TOPK = 3
GLA_HEADS = 4
GLA_DK = 64
GLA_DV = 128
GLA_KEY_WIDTH = GLA_HEADS * GLA_DK
GLA_WIDTH = GLA_HEADS * GLA_DV
GLA_GATE_RANK = 16
GLA_GATE_TAU = 16.0
GLA_CHUNK = 64
REL_BUCKETS = 32
REL_MAX_DIST = 128
N_EXPERTS = 256
TOP_K = 8
N_GROUPS = 8
TOPK_GROUPS = 4
GROUP_SIZE = N_EXPERTS // N_GROUPS
EXPERT_FF = 256
SHARED_FF = 256
ROUTED_SCALE = 2.5
NORM_EPS = 1e-6

D_MAIN = 3 * MOBA_WIDTH + 2 * GLA_KEY_WIDTH + 2 * GLA_WIDTH
LANES = 128
VMEM_LIMIT = 56 * 1024 * 1024

ROW_TILE = 512
ROUTE_TILE = 256
EXPERT_TILE = 256
EXPERT_IN_RING = 4
EXPERT_OUT_RING = 3

F32 = jnp.float32
BF16 = jnp.bfloat16
NT_DIMS = (((1,), (1,)), ((), ()))
TN_DIMS = (((0,), (0,)), ((), ()))


def _params(n_axes):
    return pltpu.CompilerParams(dimension_semantics=("arbitrary",) * n_axes,
                                vmem_limit_bytes=VMEM_LIMIT)


def _silu(v):
    return v * jax.nn.sigmoid(v)


def _mod_kernel(c_ref, w_ref, b_ref, o_ref):
    o_ref[...] = jnp.dot(_silu(c_ref[...]), w_ref[...], preferred_element_type=F32) + b_ref[...]


def _mod(c, w, b):
    rows = 8
    cp = jnp.zeros((rows, D_MODEL), F32).at[:c.shape[0]].set(c)
    n = w.shape[1]
    tn = 1024
    out = pl.pallas_call(
        _mod_kernel,
        grid=(n // tn,),
        in_specs=[pl.BlockSpec((rows, D_MODEL), lambda j: (0, 0)),
                  pl.BlockSpec((D_MODEL, tn), lambda j: (0, j)),
                  pl.BlockSpec((1, tn), lambda j: (0, j))],
        out_specs=pl.BlockSpec((rows, tn), lambda j: (0, j)),
        out_shape=jax.ShapeDtypeStruct((rows, n), F32),
        compiler_params=_params(1),
        name="mod",
    )(cp, w, b.reshape(1, n))
    return out[:c.shape[0]]


def _group_rms_inv(a, group):
    lane = lax.broadcasted_iota(jnp.int32, (1, a.shape[1]), 1)
    a2 = a * a
    inv = jnp.zeros_like(a)
    for g in range(a.shape[1] // group):
        m = (lane >= g * group) & (lane < (g + 1) * group)
        ss = jnp.sum(jnp.where(m, a2, 0.0), axis=-1, keepdims=True)
        inv = jnp.where(m, lax.rsqrt(ss * (1.0 / group) + NORM_EPS), inv)
    return inv


def _inproj_kernel(x_ref, sc_ref, sh_ref, nw_ref, w_ref, wga_ref, qw_ref, kw_ref, o_ref, ga_ref):
    x = x_ref[...]
    ms = jnp.mean(x * x, axis=-1, keepdims=True)
    h = x * lax.rsqrt(ms + NORM_EPS) * nw_ref[...]
    h = h * (1.0 + sc_ref[...]) + sh_ref[...]
    hb = h.astype(BF16)
    cw = 256
    for j in range(D_MAIN // cw):
        acc = jnp.dot(hb, w_ref[:, j * cw:(j + 1) * cw], preferred_element_type=F32)
        if j < 2 * MOBA_WIDTH // cw:
            nw = qw_ref if j < MOBA_WIDTH // cw else kw_ref
            acc = acc * _group_rms_inv(acc, MOBA_HEAD_DIM) * nw[...]
        o_ref[:, j * cw:(j + 1) * cw] = acc.astype(BF16)
    ga_ref[...] = jnp.dot(hb, wga_ref[...], preferred_element_type=F32)


def _inproj(x2, sc, sh, nw, w_main, w_ga, qw, kw, seq):
    t = x2.shape[0]
    tpb = seq // ROW_TILE
    vec = lambda: pl.BlockSpec((None, 1, D_MODEL), lambda i: (i // tpb, 0, 0))
    full = lambda a: pl.BlockSpec(a.shape, lambda i: (0,) * a.ndim)
    return pl.pallas_call(
        _inproj_kernel,
        grid=(t // ROW_TILE,),
        in_specs=[pl.BlockSpec((ROW_TILE, D_MODEL), lambda i: (i, 0)), vec(), vec(),
                  full(nw), full(w_main), full(w_ga), full(qw), full(kw)],
        out_specs=[pl.BlockSpec((ROW_TILE, D_MAIN), lambda i: (i, 0)),
                   pl.BlockSpec((ROW_TILE, LANES), lambda i: (i, 0))],
        out_shape=[jax.ShapeDtypeStruct((t, D_MAIN), BF16),
                   jax.ShapeDtypeStruct((t, LANES), F32)],
        compiler_params=_params(1),
        name="inproj",
    )(x2, sc, sh, nw, w_main, w_ga, qw, kw)


def _t5_bucket_np(rel):
    max_exact = REL_BUCKETS // 2
    relf = np.maximum(rel, 1).astype(np.float64)
    large = max_exact + (np.log(relf / max_exact) / math.log(REL_MAX_DIST / max_exact)
                         * (REL_BUCKETS - max_exact)).astype(np.int32)
    large = np.minimum(large, REL_BUCKETS - 1)
    return np.where(rel < max_exact, rel, large)


def _bias_kernel(rb_ref, idx_ref, o_ref):
    h = pl.program_id(0)
    idx = idx_ref[...]
    tab = jnp.full(idx.shape, -jnp.inf, F32)
    for bk in range(REL_BUCKETS):
        tab = jnp.where(idx == bk, rb_ref[bk * MOBA_HEADS + h], tab)
    o_ref[...] = tab


def _moba_bias_tables(rel_bias):
    j = np.arange(MOBA_BLOCK)[:, None]
    i = np.arange(MOBA_BLOCK)[None, :]
    own_idx = np.where(j <= i, _t5_bucket_np(np.maximum(i - j, 0)), -1)
    prev_idx = _t5_bucket_np(MOBA_BLOCK + i - j)
    idx = jnp.asarray(np.concatenate([prev_idx, own_idx], axis=0).astype(np.int32))
    assert int(_t5_bucket_np(np.array([MOBA_BLOCK + 1]))[0]) == REL_BUCKETS - 1
    rb = rel_bias.astype(F32)
    near = pl.pallas_call(
        _bias_kernel,
        grid=(MOBA_HEADS,),
        in_specs=[pl.BlockSpec(memory_space=pltpu.SMEM),
                  pl.BlockSpec(idx.shape, lambda h: (0, 0))],
        out_specs=pl.BlockSpec((None,) + idx.shape, lambda h: (h, 0, 0)),
        out_shape=jax.ShapeDtypeStruct((MOBA_HEADS,) + idx.shape, F32),
        compiler_params=_params(1),
        name="bias",
    )(rb.reshape(-1), idx)
    return near, rb[REL_BUCKETS - 1]


FAR_GROUP = 4


def _moba_kernel(far_ref, q_ref, k_ref, v_ref, near_ref, ow_ref, o_ref,
                 vt_ref, vtg_ref, acc_ref, m_ref, sel_ref, s_ref, mx_ref):
    hp = pl.program_id(1)
    i = pl.program_id(2)
    nblk = k_ref.shape[0] // MOBA_BLOCK
    ngrp = nblk // FAR_GROUP
    hd = MOBA_HEAD_DIM
    bs = MOBA_BLOCK
    lane = lax.broadcasted_iota(jnp.int32, (bs, LANES), 1)

    def split_heads(qb):
        zero = jnp.zeros_like(qb)
        return jnp.where(lane < hd, qb, zero), jnp.where(lane < hd, zero, qb)

    @pl.when(i == 0)
    def _prepare():
        row = lax.broadcasted_iota(jnp.int32, (LANES, bs), 0)
        kmeans = []
        for n in range(nblk):
            kb = k_ref[n * bs:(n + 1) * bs, :].astype(F32)
            kmeans.append(jnp.mean(kb, axis=0, keepdims=True))
            vt = v_ref[n * bs:(n + 1) * bs, :].astype(F32).T
            vt0 = jnp.where(row < hd, vt, 1.0).astype(BF16)
            vt1 = jnp.where(row < hd, 1.0, vt).astype(BF16)
            vt_ref[0, n] = vt0
            vt_ref[1, n] = vt1
            gcols = slice((n % FAR_GROUP) * bs, (n % FAR_GROUP + 1) * bs)
            vtg_ref[0, n // FAR_GROUP, :, gcols] = vt0
            vtg_ref[1, n // FAR_GROUP, :, gcols] = vt1
        kmean = jnp.concatenate(kmeans, axis=0)
        km_hi = kmean.astype(BF16)
        km_lo = (kmean - km_hi.astype(F32)).astype(BF16)
        blk = lax.broadcasted_iota(jnp.int32, (nblk, bs), 0)
        for ib in range(nblk):
            qparts = split_heads(q_ref[ib * bs:(ib + 1) * bs, :])
            for h in range(2):
                gt = (lax.dot_general(km_hi, qparts[h], NT_DIMS, preferred_element_type=F32)
                      + lax.dot_general(km_lo, qparts[h], NT_DIMS, preferred_element_type=F32))
                gt = jnp.where(blk < ib, gt, -jnp.inf)
                cnt = jnp.zeros(gt.shape, jnp.int32)
                for m in range(ib):
                    gm = gt[m:m + 1, :]
                    cnt = cnt + jnp.where((gm > gt) | ((gm == gt) & (blk > m)), 1, 0)
                keep = (blk < ib) & (cnt < MOBA_TOPK)
                sel_ref[0, h, ib] = jnp.where(keep, 1.0, 0.0)
                sel_ref[1, h, ib] = jnp.where(keep & (blk < ib - 1), 1.0, 0.0)

    qh = split_heads(q_ref[pl.ds(pl.multiple_of(i * bs, bs), bs), :])

    @pl.when(i == 0)
    def _own_block_only():
        kb = k_ref[0:bs, :]
        for h in range(2):
            s = lax.dot_general(kb, qh[h], NT_DIMS, preferred_element_type=F32) + near_ref[h, bs:2 * bs, :]
            m_new = jnp.max(s, axis=0, keepdims=True)
            p = jnp.exp(s - m_new).astype(BF16)
            acc_ref[h] = jnp.dot(vt_ref[h, 0], p, preferred_element_type=F32)
            m_ref[h] = m_new

    n_far = (i + FAR_GROUP - 2) // FAR_GROUP
    gk = FAR_GROUP * bs

    def far_scores(g):
        kb = k_ref[g * gk:(g + 1) * gk, :]
        for h in range(2):
            s = lax.dot_general(kb, qh[h], NT_DIMS, preferred_element_type=F32)
            s_ref[g % 2, h] = s
            for j in range(FAR_GROUP):
                mx_ref[g % 2, h, j] = jnp.max(s[j * bs:(j + 1) * bs], axis=0, keepdims=True)

    def near_blocks(with_far):
        kbs = (k_ref[pl.ds(pl.multiple_of((i - 1) * bs, bs), bs), :],
               k_ref[pl.ds(pl.multiple_of(i * bs, bs), bs), :])
        ss = [[lax.dot_general(kbs[w], qh[h], NT_DIMS, preferred_element_type=F32)
               + near_ref[h, w * bs:(w + 1) * bs, :] for w in range(2)] for h in range(2)]
        if with_far:
            far_scores(0)
        ps, ms = [], []
        for h in range(2):
            s_prev, s_own = ss[h]
            keep = sel_ref[0, h, i, pl.ds(i - 1, 1), :] > 0.5
            mx = jnp.where(keep, jnp.max(s_prev, axis=0, keepdims=True), -jnp.inf)
            m_new = jnp.maximum(jnp.max(s_own, axis=0, keepdims=True), mx)
            ps.append((jnp.exp(s_prev - jnp.where(keep, m_new, jnp.inf)).astype(BF16),
                       jnp.exp(s_own - m_new).astype(BF16)))
            ms.append(m_new)
        for h in range(2):
            acc_ref[h] = (jnp.dot(vt_ref[h, i - 1], ps[h][0], preferred_element_type=F32)
                          + jnp.dot(vt_ref[h, i], ps[h][1], preferred_element_type=F32))
            m_ref[h] = ms[h]

    def far_group(g, with_next):
        if with_next:
            far_scores(g + 1)
        for h in range(2):
            fb = far_ref[2 * hp + h]
            m_old = m_ref[h]
            m_new = m_old
            keeps = []
            for j in range(FAR_GROUP):
                keep = sel_ref[1, h, i, pl.ds(g * FAR_GROUP + j, 1), :] > 0.5
                m_new = jnp.maximum(m_new, jnp.where(keep, mx_ref[g % 2, h, j] + fb, -jnp.inf))
                keeps.append(keep)
            p = jnp.concatenate(
                [jnp.exp(s_ref[g % 2, h, j * bs:(j + 1) * bs, :]
                         - jnp.where(keeps[j], m_new - fb, jnp.inf)).astype(BF16)
                 for j in range(FAR_GROUP)], axis=0)
            pv = jnp.dot(vtg_ref[h, g], p, preferred_element_type=F32)
            acc_ref[h] = acc_ref[h] * jnp.exp(m_old - m_new) + pv
            m_ref[h] = m_new

    def step_body(nf):
        near_blocks(nf > 0)
        for g in range(nf):
            far_group(g, g + 1 < nf)

    for nf in range(ngrp + 1):
        pl.when((i >= 1) & (n_far == nf))(functools.partial(step_body, nf))

    a0 = acc_ref[0]
    a1 = acc_ref[1]
    row = lax.broadcasted_iota(jnp.int32, a0.shape, 0)
    ot = jnp.where(row < hd, a0 / a0[hd:hd + 1, :], a1 / a1[0:1, :])
    o2 = ot * ot
    ss0 = jnp.sum(jnp.where(row < hd, o2, 0.0), axis=0, keepdims=True)
    ss1 = jnp.sum(jnp.where(row < hd, 0.0, o2), axis=0, keepdims=True)
    inv = jnp.where(row < hd, lax.rsqrt(ss0 * (1.0 / hd) + NORM_EPS), lax.rsqrt(ss1 * (1.0 / hd) + NORM_EPS))
    o_ref[...] = ((ot * inv).T * ow_ref[...]).astype(o_ref.dtype)


def _moba(proj3, near, far, ow):
    b, s, _ = proj3.shape
    nblk = s // MOBA_BLOCK
    assert nblk % FAR_GROUP == 0
    npair = MOBA_HEADS // 2
    kcol = MOBA_WIDTH // LANES
    return pl.pallas_call(
        _moba_kernel,
        grid=(b, npair, nblk),
        in_specs=[pl.BlockSpec(memory_space=pltpu.SMEM),
                  pl.BlockSpec((None, s, LANES), lambda bb, hp, i: (bb, 0, hp)),
                  pl.BlockSpec((None, s, LANES), lambda bb, hp, i: (bb, 0, kcol + hp)),
                  pl.BlockSpec((None, s, LANES), lambda bb, hp, i: (bb, 0, 2 * kcol + hp)),
                  pl.BlockSpec((2, 2 * MOBA_BLOCK, MOBA_BLOCK), lambda bb, hp, i: (hp, 0, 0)),
                  pl.BlockSpec((1, LANES), lambda bb, hp, i: (0, 0))],
        out_specs=pl.BlockSpec((None, MOBA_BLOCK, LANES), lambda bb, hp, i: (bb, i, hp)),
        out_shape=jax.ShapeDtypeStruct((b, s, MOBA_WIDTH), BF16),
        scratch_shapes=[pltpu.VMEM((2, nblk, LANES, MOBA_BLOCK), BF16),
                        pltpu.VMEM((2, nblk // FAR_GROUP, LANES, FAR_GROUP * MOBA_BLOCK), BF16),
                        pltpu.VMEM((2, LANES, MOBA_BLOCK), F32),
                        pltpu.VMEM((2, 1, MOBA_BLOCK), F32),
                        pltpu.VMEM((2, 2, nblk, nblk, MOBA_BLOCK), F32),
                        pltpu.VMEM((2, 2, FAR_GROUP * MOBA_BLOCK, MOBA_BLOCK), F32),
                        pltpu.VMEM((2, 2, FAR_GROUP, 1, MOBA_BLOCK), F32)],
        compiler_params=_params(3),
        name="moba",
    )(far, proj3, proj3, proj3, near, ow)


def _split3(v):
    hi = v.astype(BF16)
    r1 = v - hi.astype(F32)
    mid = r1.astype(BF16)
    lo = (r1 - mid.astype(F32)).astype(BF16)
    return hi, mid, lo


GLA_UNROLL = 4


def _gla_kernel(q_ref, k_ref, v_ref, g_ref, ga_ref, wal_ref, bal_ref, gw_ref, o_ref, b_ref, st_ref):
    seq = q_ref.shape[0]
    c = GLA_CHUNK
    pc = 256

    rr = lax.broadcasted_iota(jnp.int32, (pc, pc), 0)
    cc = lax.broadcasted_iota(jnp.int32, (pc, pc), 1)
    tri = jnp.where((rr >= cc) & (rr // c == cc // c), 1.0, 0.0).astype(BF16)

    def decay_body(j, carry):
        rows = [pl.ds(pl.multiple_of((j * GLA_UNROLL + u) * pc, pc), pc) for u in range(GLA_UNROLL)]
        xg = [jnp.dot(ga_ref[r, :], wal_ref[...], preferred_element_type=F32) + bal_ref[...] for r in rows]
        parts = [_split3((jnp.minimum(x, 0.0) - jnp.log(1.0 + jnp.exp(-jnp.abs(x)))) * (1.0 / GLA_GATE_TAU))
                 for x in xg]
        sums = [[jnp.dot(tri, term, preferred_element_type=F32) for term in p] for p in parts]
        for r, (hi, mid, lo) in zip(rows, sums):
            b_ref[r, :] = hi + mid + lo
        return carry

    lax.fori_loop(0, seq // (pc * GLA_UNROLL), decay_body, 0)

    st_ref[...] = jnp.zeros_like(st_ref)
    lane = lax.broadcasted_iota(jnp.int32, (c, LANES), 1)
    head_mask = (lane < GLA_DK, lane >= GLA_DK)
    causal = lax.broadcasted_iota(jnp.int32, (c, c), 0) >= lax.broadcasted_iota(jnp.int32, (c, c), 1)

    units = [(u, h) for u in range(GLA_UNROLL) for h in range(2)]

    def chunk_body(ci, carry):
        rows = [pl.ds(pl.multiple_of((ci * GLA_UNROLL + u) * c, c), c) for u in range(GLA_UNROLL)]
        qt, kt, qs, ke, e_last = [], [], [], [], []
        for u in range(GLA_UNROLL):
            b = b_ref[rows[u], :]
            ref_row = b[c // 2 - 1:c // 2, :]
            last = b[c - 1:c, :]
            q = q_ref[rows[u], :].astype(F32) * (GLA_DK ** -0.5)
            k = k_ref[rows[u], :].astype(F32)
            qt.append(q * jnp.exp(b - ref_row))
            kt.append((k * jnp.exp(ref_row - b)).astype(BF16))
            qs.append(q * jnp.exp(b))
            ke.append((k * jnp.exp(last - b)).astype(BF16))
            e_last.append(jnp.exp(last))
        vs = {(u, h): v_ref[rows[u], h * GLA_DV:(h + 1) * GLA_DV] for u, h in units}
        a = {(u, h): lax.dot_general(jnp.where(head_mask[h], qt[u], 0.0).astype(BF16), kt[u], NT_DIMS,
                                     preferred_element_type=F32) for u, h in units}
        inc = {(u, h): lax.dot_general(vs[u, h], ke[u], TN_DIMS, preferred_element_type=F32) for u, h in units}
        o = {(u, h): jnp.dot(jnp.where(causal, a[u, h], 0.0).astype(BF16), vs[u, h], preferred_element_type=F32)
             for u, h in units}
        states = {}
        for h in range(2):
            st = st_ref[h]
            for u in range(GLA_UNROLL):
                states[u, h] = st
                st = st * e_last[u] + inc[u, h]
            st_ref[h] = st
        for u, h in units:
            cols = slice(h * GLA_DV, (h + 1) * GLA_DV)
            ou = o[u, h] + lax.dot_general(jnp.where(head_mask[h], qs[u], 0.0).astype(BF16),
                                           states[u, h].astype(BF16), NT_DIMS, preferred_element_type=F32)
            ms = jnp.mean(ou * ou, axis=-1, keepdims=True)
            on = ou * lax.rsqrt(ms + NORM_EPS) * gw_ref[...]
            g = g_ref[rows[u], cols].astype(F32)
            o_ref[rows[u], cols] = (on * _silu(g)).astype(o_ref.dtype)
        return carry

    lax.fori_loop(0, seq // (c * GLA_UNROLL), chunk_body, 0)


def _gla(proj3, ga3, wal, bal, gw):
    b, s, _ = proj3.shape
    npair = GLA_HEADS // 2
    qcol = 3 * MOBA_WIDTH // LANES
    kcol = qcol + GLA_KEY_WIDTH // LANES
    vcol = (3 * MOBA_WIDTH + 2 * GLA_KEY_WIDTH) // (2 * GLA_DV)
    gcol = vcol + npair
    return pl.pallas_call(
        _gla_kernel,
        grid=(b, npair),
        in_specs=[pl.BlockSpec((None, s, LANES), lambda bb, hp: (bb, 0, qcol + hp)),
                  pl.BlockSpec((None, s, LANES), lambda bb, hp: (bb, 0, kcol + hp)),
                  pl.BlockSpec((None, s, 2 * GLA_DV), lambda bb, hp: (bb, 0, vcol + hp)),
                  pl.BlockSpec((None, s, 2 * GLA_DV), lambda bb, hp: (bb, 0, gcol + hp)),
                  pl.BlockSpec((None, s, LANES), lambda bb, hp: (bb, 0, 0)),
                  pl.BlockSpec((LANES, LANES), lambda bb, hp: (0, hp)),
                  pl.BlockSpec((1, LANES), lambda bb, hp: (0, hp)),
                  pl.BlockSpec((1, GLA_DV), lambda bb, hp: (0, 0))],
        out_specs=pl.BlockSpec((None, s, 2 * GLA_DV), lambda bb, hp: (bb, 0, hp)),
        out_shape=jax.ShapeDtypeStruct((b, s, GLA_WIDTH), BF16),
        scratch_shapes=[pltpu.VMEM((s, LANES), F32),
                        pltpu.VMEM((2, GLA_DV, LANES), F32)],
        compiler_params=_params(2),
        name="gla",
    )(proj3, proj3, proj3, proj3, ga3, wal, bal, gw)


HALF = D_MODEL // 2


def _pack_rows(v):
    return pltpu.pack_elementwise([v[:, :HALF], v[:, HALF:]], packed_dtype=BF16)


def _unpack_rows(w):
    return (pltpu.unpack_elementwise(w, index=0, packed_dtype=BF16, unpacked_dtype=F32),
            pltpu.unpack_elementwise(w, index=1, packed_dtype=BF16, unpacked_dtype=F32))


def _outproj_kernel(oa_ref, ob_ref, x_ref, g1_ref, sc_ref, sh_ref, g2_ref, nw_ref, wo_ref,
                    ws1_ref, ws3_ref, ws2_ref, wrt_ref, base_ref, h_ref, st_ref):
    mix = (jnp.dot(oa_ref[...], wo_ref[:MOBA_WIDTH, :], preferred_element_type=F32)
           + jnp.dot(ob_ref[...], wo_ref[MOBA_WIDTH:, :], preferred_element_type=F32))
    x1 = x_ref[...] + g1_ref[...] * mix
    ms = jnp.mean(x1 * x1, axis=-1, keepdims=True)
    h = x1 * lax.rsqrt(ms + NORM_EPS) * nw_ref[...]
    h = h * (1.0 + sc_ref[...]) + sh_ref[...]
    h_ref[...] = _pack_rows(h)
    hb = h.astype(BF16)
    a = jnp.dot(hb, ws1_ref[...], preferred_element_type=F32)
    u = jnp.dot(hb, ws3_ref[...], preferred_element_type=F32)
    shared = jnp.dot((_silu(a) * u).astype(BF16), ws2_ref[...], preferred_element_type=F32)
    base_ref[...] = x1 + g2_ref[...] * shared
    logits_t = lax.dot_general(wrt_ref[...], hb, NT_DIMS, preferred_element_type=F32)
    st_ref[...] = jax.nn.sigmoid(logits_t)


def _outproj(oa, ob, x2, g1, sc, sh, g2, nw, wo, ws1, ws3, ws2, wrt, seq):
    t = x2.shape[0]
    tpb = seq // ROW_TILE
    vec = lambda: pl.BlockSpec((None, 1, D_MODEL), lambda i: (i // tpb, 0, 0))
    full = lambda a: pl.BlockSpec(a.shape, lambda i: (0,) * a.ndim)
    rows = lambda w: pl.BlockSpec((ROW_TILE, w), lambda i: (i, 0))
    return pl.pallas_call(
        _outproj_kernel,
        grid=(t // ROW_TILE,),
        in_specs=[rows(MOBA_WIDTH), rows(GLA_WIDTH), rows(D_MODEL), vec(), vec(), vec(), vec(),
                  full(nw), full(wo), full(ws1), full(ws3), full(ws2), full(wrt)],
        out_specs=[rows(D_MODEL), rows(HALF), pl.BlockSpec((N_EXPERTS, ROW_TILE), lambda i: (0, i))],
        out_shape=[jax.ShapeDtypeStruct((t, D_MODEL), F32),
                   jax.ShapeDtypeStruct((t, HALF), jnp.uint32),
                   jax.ShapeDtypeStruct((N_EXPERTS, t), F32)],
        compiler_params=_params(1),
        name="outproj",
    )(oa, ob, x2, g1, sc, sh, g2, nw, wo, ws1, ws3, ws2, wrt)


SLOT_CODE_SHIFT = 16
SLOT_CODE_BASE = 1 << SLOT_CODE_SHIFT


def _route_kernel(s_ref, eb_ref, code_ref, w_ref, cnt_ref, carry_ref):
    i = pl.program_id(0)
    ne, nt = s_ref.shape

    @pl.when(i == 0)
    def _init():
        carry_ref[...] = jnp.zeros_like(carry_ref)

    s = s_ref[...]
    choice = s + eb_ref[...]
    gio = lax.broadcasted_iota(jnp.int32, (GROUP_SIZE, nt), 0)
    gscore = []
    for g in range(N_GROUPS):
        cg = choice[g * GROUP_SIZE:(g + 1) * GROUP_SIZE, :]
        top1 = jnp.max(cg, axis=0, keepdims=True)
        first = jnp.min(jnp.where(cg == top1, gio, GROUP_SIZE), axis=0, keepdims=True)
        top2 = jnp.max(jnp.where(gio == first, -jnp.inf, cg), axis=0, keepdims=True)
        gscore.append(top1 + top2)
    gs = jnp.concatenate(gscore, axis=0)
    gidx = lax.broadcasted_iota(jnp.int32, gs.shape, 0)
    beaten = jnp.zeros(gs.shape, jnp.int32)
    for m in range(N_GROUPS):
        gm = gs[m:m + 1, :]
        beaten = beaten + jnp.where((gm > gs) | ((gm == gs) & (gidx > m)), 1, 0)
    gkeep = beaten < TOPK_GROUPS
    masked = jnp.concatenate(
        [jnp.where(gkeep[g:g + 1, :], choice[g * GROUP_SIZE:(g + 1) * GROUP_SIZE, :], -jnp.inf)
         for g in range(N_GROUPS)], axis=0)

    eio = lax.broadcasted_iota(jnp.int32, (ne, nt), 0)
    picked = jnp.zeros((ne, nt), F32)
    idx_rows, w_rows, hits = [], [], []
    for _ in range(TOP_K):
        mx = jnp.max(masked, axis=0, keepdims=True)
        idx = jnp.min(jnp.where(masked == mx, eio, ne), axis=0, keepdims=True)
        hit = eio == idx
        w_rows.append(jnp.sum(jnp.where(hit, s, 0.0), axis=0, keepdims=True))
        idx_rows.append(idx)
        hits.append(hit)
        masked = jnp.where(hit, -jnp.inf, masked)
        picked = jnp.where(hit, 1.0, picked)
    wk = jnp.concatenate(w_rows, axis=0)
    w_ref[...] = wk / jnp.sum(wk, axis=0, keepdims=True) * ROUTED_SCALE

    tr = lax.broadcasted_iota(jnp.int32, (nt, nt), 0)
    tc = lax.broadcasted_iota(jnp.int32, (nt, nt), 1)
    before = jnp.where(tr < tc, 1.0, 0.0).astype(BF16)
    pb = picked.astype(BF16)
    pos = carry_ref[...] + jnp.dot(pb, before, preferred_element_type=F32)
    rank = jnp.concatenate(
        [jnp.sum(jnp.where(hit, pos, 0.0), axis=0, keepdims=True) for hit in hits], axis=0).astype(jnp.int32)
    code_ref[...] = jnp.concatenate(idx_rows, axis=0) * SLOT_CODE_BASE + rank
    total = carry_ref[...] + jnp.dot(pb, jnp.ones((nt, nt), BF16), preferred_element_type=F32)
    carry_ref[...] = total
    cnt_ref[...] = total


def _route(scores_t, eb):
    ne, t = scores_t.shape
    assert t <= SLOT_CODE_BASE
    nt = ROUTE_TILE
    tok = lambda dt: jax.ShapeDtypeStruct((TOP_K, t), dt)
    return pl.pallas_call(
        _route_kernel,
        grid=(t // nt,),
        in_specs=[pl.BlockSpec((ne, nt), lambda i: (0, i)),
                  pl.BlockSpec((ne, nt), lambda i: (0, 0))],
        out_specs=[pl.BlockSpec((TOP_K, nt), lambda i: (0, i)),
                   pl.BlockSpec((TOP_K, nt), lambda i: (0, i)),
                   pl.BlockSpec((ne, nt), lambda i: (0, 0))],
        out_shape=[tok(jnp.int32), tok(F32), jax.ShapeDtypeStruct((ne, nt), F32)],
        scratch_shapes=[pltpu.VMEM((ne, nt), F32)],
        compiler_params=_params(1),
        name="route",
    )(scores_t, eb)


SLOT_TILE = 2048


def _slots_kernel(pstart_ref, code_ref, o_ref):
    code = code_ref[...]
    expert = lax.shift_right_logical(code, SLOT_CODE_SHIFT)

    def body(e, acc):
        return jnp.where(expert == e, pstart_ref[e], acc)

    start = lax.fori_loop(0, N_EXPERTS, body, jnp.zeros_like(code), unroll=8)
    o_ref[...] = start + (code & (SLOT_CODE_BASE - 1))


def _slots(pstart, code_t):
    k, t = code_t.shape
    return pl.pallas_call(
        _slots_kernel,
        grid_spec=pltpu.PrefetchScalarGridSpec(
            num_scalar_prefetch=1,
            grid=(t // SLOT_TILE,),
            in_specs=[pl.BlockSpec((k, SLOT_TILE), lambda i, p: (0, i))],
            out_specs=pl.BlockSpec((k, SLOT_TILE), lambda i, p: (0, i)),
        ),
        out_shape=jax.ShapeDtypeStruct((k, t), jnp.int32),
        compiler_params=_params(1),
        name="slots",
    )(pstart, code_t)


SC_WINDOW = 128


def _sc_gather_rows(table, idx_flat):
    from jax.experimental.pallas import tpu_sc as plsc
    info = plsc.get_sparse_core_info()
    nw = info.num_cores * info.num_subcores
    n = idx_flat.shape[0]
    width = table.shape[1]
    per_worker = n // nw
    assert per_worker * nw == n and per_worker % SC_WINDOW == 0
    mesh = plsc.VectorSubcoreMesh(core_axis_name="c", subcore_axis_name="s")

    def body(table_hbm, idx_hbm, out_hbm, idx_v, rows_v, sem):
        wid = lax.axis_index("s") * info.num_cores + lax.axis_index("c")
        base = wid * per_worker

        @pl.loop(0, per_worker // SC_WINDOW)
        def _(w):
            off = pl.multiple_of(base + w * SC_WINDOW, SC_WINDOW)
            pltpu.sync_copy(idx_hbm.at[pl.ds(off, SC_WINDOW)], idx_v)
            pltpu.async_copy(table_hbm.at[idx_v], rows_v, sem).wait()
            pltpu.sync_copy(rows_v, out_hbm.at[pl.ds(off, SC_WINDOW)])

    return pl.kernel(
        body,
        out_type=jax.ShapeDtypeStruct((n, width), table.dtype),
        mesh=mesh,
        scratch_types=[pltpu.VMEM((SC_WINDOW,), jnp.int32),
                       pltpu.VMEM((SC_WINDOW, width), table.dtype),
                       pltpu.SemaphoreType.DMA],
        name="sc_gather",
    )(table, idx_flat)


def _sc_scatter_rows(rows, idx_kt, n_out):
    from jax.experimental.pallas import tpu_sc as plsc
    info = plsc.get_sparse_core_info()
    nw = info.num_cores * info.num_subcores
    t, width = rows.shape
    nk = idx_kt.shape[0]
    per_worker = t // nw
    assert per_worker * nw == t and per_worker % SC_WINDOW == 0
    mesh = plsc.VectorSubcoreMesh(core_axis_name="c", subcore_axis_name="s")

    def body(rows_hbm, idx_hbm, out_hbm, idx_v, rows_v, sem):
        wid = lax.axis_index("s") * info.num_cores + lax.axis_index("c")
        base = wid * per_worker

        @pl.loop(0, per_worker // SC_WINDOW)
        def _(w):
            off = pl.multiple_of(base + w * SC_WINDOW, SC_WINDOW)
            pltpu.sync_copy(rows_hbm.at[pl.ds(off, SC_WINDOW)], rows_v)
            pltpu.sync_copy(idx_hbm.at[:, pl.ds(off, SC_WINDOW)], idx_v)
            copies = [pltpu.async_copy(rows_v, out_hbm.at[idx_v.at[k]], sem) for k in range(nk)]
            for cp in copies:
                cp.wait()

    return pl.kernel(
        body,
        out_type=jax.ShapeDtypeStruct((n_out, width), rows.dtype),
        mesh=mesh,
        scratch_types=[pltpu.VMEM((nk, SC_WINDOW), jnp.int32),
                       pltpu.VMEM((SC_WINDOW, width), rows.dtype),
                       pltpu.SemaphoreType.DMA],
        name="sc_scatter",
    )(rows, idx_kt)


def _expert_kernel(first_ref, count_ref, used_ref, w1_ref, w3_ref, w2_ref, xs_ref, ys_ref,
                   xbuf, ybuf, sem_in, sem_out, w1f, w3f, w2f, sem_w, w1b, w3b, w2b, *, layer):
    e = pl.program_id(0)
    ne = pl.num_programs(0)
    r = xbuf.shape[1]
    n_used = used_ref[0]

    def x_copy(g, slot):
        return pltpu.make_async_copy(xs_ref.at[pl.ds(pl.multiple_of(g * r, r), r), :], xbuf.at[slot], sem_in.at[slot])

    def y_copy(g, slot):
        return pltpu.make_async_copy(ybuf.at[slot], ys_ref.at[pl.ds(pl.multiple_of(g * r, r), r), :], sem_out.at[slot])

    def w_copies(ex, slot):
        return [pltpu.make_async_copy(src.at[layer, ex], dst.at[slot], sem_w.at[slot])
                for src, dst in ((w1_ref, w1f), (w3_ref, w3f), (w2_ref, w2f))]

    nin = xbuf.shape[0]
    nout = ybuf.shape[0]

    @pl.when(e == 0)
    def _first_reads():
        for g in range(nin - 1):
            @pl.when(g < n_used)
            def _(g=g):
                x_copy(g, g).start()
        for cp in w_copies(0, 0):
            cp.start(priority=1)

    @pl.when(e + 1 < ne)
    def _next_weights():
        for cp in w_copies(e + 1, (e + 1) & 1):
            cp.start(priority=1)

    for cp in w_copies(e, e & 1):
        cp.wait()

    n = count_ref[e]

    @pl.when(n > 0)
    def _cast_weights():
        w1b[...] = w1f[e & 1].astype(BF16)
        w3b[...] = w3f[e & 1].astype(BF16)
        w2b[...] = w2f[e & 1].astype(BF16)

    def tile_body(g, carry):
        slot = g % nin
        oslot = g % nout
        x_copy(g, slot).wait()

        @pl.when(g + nin - 1 < n_used)
        def _read_ahead():
            x_copy(g + nin - 1, (g + nin - 1) % nin).start()

        @pl.when(g >= nout)
        def _free_out_slot():
            y_copy(g - nout, oslot).wait()

        lo, hi = _unpack_rows(xbuf[slot])
        lo, hi = lo.astype(BF16), hi.astype(BF16)
        a = (jnp.dot(lo, w1b[:HALF, :], preferred_element_type=F32)
             + jnp.dot(hi, w1b[HALF:, :], preferred_element_type=F32))
        u = (jnp.dot(lo, w3b[:HALF, :], preferred_element_type=F32)
             + jnp.dot(hi, w3b[HALF:, :], preferred_element_type=F32))
        y = jnp.dot((_silu(a) * u).astype(BF16), w2b[...], preferred_element_type=F32)
        ybuf[oslot] = _pack_rows(y)
        y_copy(g, oslot).start()
        return carry

    g0 = first_ref[e]
    lax.fori_loop(g0, g0 + n, tile_body, 0)

    @pl.when(e == ne - 1)
    def _drain_writes():
        for back in range(nout, 0, -1):
            @pl.when(n_used >= back)
            def _(back=back):
                y_copy(n_used - back, (n_used - back) % nout).wait()


def _experts(tile_first, tile_count, n_used, xs, w1, w3, w2, layer):
    n_rows = xs.shape[0]
    r = EXPERT_TILE
    any_spec = pl.BlockSpec(memory_space=pl.ANY)
    return pl.pallas_call(
        functools.partial(_expert_kernel, layer=layer),
        grid_spec=pltpu.PrefetchScalarGridSpec(
            num_scalar_prefetch=3,
            grid=(N_EXPERTS,),
            in_specs=[any_spec, any_spec, any_spec, any_spec],
            out_specs=any_spec,
            scratch_shapes=[pltpu.VMEM((EXPERT_IN_RING, r, HALF), jnp.uint32),
                            pltpu.VMEM((EXPERT_OUT_RING, r, HALF), jnp.uint32),
                            pltpu.SemaphoreType.DMA((EXPERT_IN_RING,)),
                            pltpu.SemaphoreType.DMA((EXPERT_OUT_RING,)),
                            pltpu.VMEM((2, D_MODEL, EXPERT_FF), F32),
                            pltpu.VMEM((2, D_MODEL, EXPERT_FF), F32),
                            pltpu.VMEM((2, EXPERT_FF, D_MODEL), F32),
                            pltpu.SemaphoreType.DMA((2,)),
                            pltpu.VMEM((D_MODEL, EXPERT_FF), BF16),
                            pltpu.VMEM((D_MODEL, EXPERT_FF), BF16),
                            pltpu.VMEM((EXPERT_FF, D_MODEL), BF16)],
        ),
        out_shape=jax.ShapeDtypeStruct((n_rows, HALF), jnp.uint32),
        compiler_params=_params(1),
        name="experts",
    )(tile_first, tile_count, n_used, w1, w3, w2, xs)


def _combine_dense_kernel(base_ref, g2_ref, w_ref, yg_ref, o_ref):
    acc_lo = acc_hi = None
    for k in range(TOP_K):
        lo, hi = _unpack_rows(yg_ref[k])
        wk = w_ref[:, k:k + 1]
        acc_lo = wk * lo if acc_lo is None else acc_lo + wk * lo
        acc_hi = wk * hi if acc_hi is None else acc_hi + wk * hi
    o_ref[:, :HALF] = base_ref[:, :HALF] + g2_ref[:, :HALF] * acc_lo
    o_ref[:, HALF:] = base_ref[:, HALF:] + g2_ref[:, HALF:] * acc_hi


def _combine_dense(base, g2, w_tok, yg, seq):
    t = base.shape[0]
    nt = ROUTE_TILE
    tpb = seq // nt
    return pl.pallas_call(
        _combine_dense_kernel,
        grid=(t // nt,),
        in_specs=[pl.BlockSpec((nt, D_MODEL), lambda i: (i, 0)),
                  pl.BlockSpec((None, 1, D_MODEL), lambda i: (i // tpb, 0, 0)),
                  pl.BlockSpec((nt, TOP_K), lambda i: (i, 0)),
                  pl.BlockSpec((TOP_K, nt, HALF), lambda i: (0, i, 0))],
        out_specs=pl.BlockSpec((nt, D_MODEL), lambda i: (i, 0)),
        out_shape=jax.ShapeDtypeStruct((t, D_MODEL), F32),
        compiler_params=_params(1),
        name="combine_dense",
    )(base, g2, w_tok, yg)


def _layer(layer, x, c, w_ada, b_ada, norm1_w, norm2_w, w_in, q_norm_w, k_norm_w, rel_bias, w_alpha, b_alpha,
           moba_out_w, gla_out_w, w_out, w_router, e_bias, w1, w3, w2, ws1, ws3, ws2):
    b, s, d = x.shape
    t = b * s
    x2 = x.reshape(t, d)

    mod = _mod(c, w_ada, b_ada)
    sh1, sc1, g1, sh2, sc2, g2 = [mod[:, j * d:(j + 1) * d].reshape(b, 1, d) for j in range(6)]

    w_main = w_in[:, :D_MAIN].astype(BF16)
    w_ga = jnp.zeros((d, LANES), BF16).at[:, :GLA_GATE_RANK].set(w_in[:, D_MAIN:].astype(BF16))
    per_chunk = 256 // MOBA_HEAD_DIM
    qw = jnp.tile(q_norm_w.astype(F32), per_chunk).reshape(1, 256) * (MOBA_HEAD_DIM ** -0.5)
    kw = jnp.tile(k_norm_w.astype(F32), per_chunk).reshape(1, 256)
    proj, ga = _inproj(x2, sc1, sh1, norm1_w.reshape(1, d), w_main, w_ga, qw, kw, s)
    proj3 = proj.reshape(b, s, D_MAIN)

    near, far = _moba_bias_tables(rel_bias)
    ow = jnp.tile(moba_out_w.astype(F32), 2).reshape(1, LANES)
    o_a = _moba(proj3, near, far, ow)

    wal = jnp.zeros((LANES, GLA_KEY_WIDTH), F32).at[:GLA_GATE_RANK].set(w_alpha)
    o_b = _gla(proj3, ga.reshape(b, s, LANES), wal, b_alpha.reshape(1, GLA_KEY_WIDTH),
               gla_out_w.reshape(1, GLA_DV))

    base, h2, scores_t = _outproj(
        o_a.reshape(t, MOBA_WIDTH), o_b.reshape(t, GLA_WIDTH), x2, g1, sc2, sh2, g2,
        norm2_w.reshape(1, d), w_out.astype(BF16), ws1.astype(BF16), ws3.astype(BF16), ws2.astype(BF16),
        w_router.T.astype(BF16), s)

    eb = jnp.broadcast_to(e_bias.astype(F32)[:, None], (N_EXPERTS, ROUTE_TILE))
    code_t, w_t, counts = _route(scores_t, eb)

    r = EXPERT_TILE
    n_tiles = (t * TOP_K + N_EXPERTS * (r - 1) + r - 1) // r
    n_rows = n_tiles * r
    cnt = counts[:, 0].astype(jnp.int32)
    padded = (cnt + r - 1) // r * r
    pend = jnp.cumsum(padded)
    pstart = pend - padded
    n_used = (pend[-1:] // r).astype(jnp.int32)
    dest_t = _slots(pstart, code_t)

    xs = _sc_scatter_rows(h2, dest_t, n_rows)
    ys = _experts(pstart // r, padded // r, n_used, xs, w1, w3, w2, layer)
    yg = _sc_gather_rows(ys, dest_t.reshape(TOP_K * t)).reshape(TOP_K, t, HALF)
    out = _combine_dense(base, g2, w_t.T, yg, s)
    return out.reshape(b, s, d)


def kernel(x, c, w_ada, b_ada, norm1_w, norm2_w, w_in, q_norm_w, k_norm_w, rel_bias, w_alpha, b_alpha,
           moba_out_w, gla_out_w, w_out, w_router, e_bias, w1, w3, w2, ws1, ws3, ws2):
    for l in range(w_ada.shape[0]):
        x = _layer(l, x, c, w_ada[l], b_ada[l], norm1_w[l], norm2_w[l], w_in[l], q_norm_w[l], k_norm_w[l],
                   rel_bias, w_alpha[l], b_alpha[l], moba_out_w[l], gla_out_w[l], w_out[l], w_router[l],
                   e_bias[l], w1, w3, w2, ws1[l], ws3[l], ws2[l])
    return x
```

```python
import functools
import math

import numpy as np
import jax
import jax.numpy as jnp
from jax import lax
from jax.experimental import pallas as pl
from jax.experimental.pallas import tpu as pltpu

D_MODEL = 1024
MOBA_HEADS = 8
MOBA_HEAD_DIM = 64
MOBA_WIDTH = MOBA_HEADS * MOBA_HEAD_DIM
MOBA_BLOCK = 256
MOBA_TOPK = 3
GLA_HEADS = 4
GLA_DK = 64
GLA_DV = 128
GLA_KEY_WIDTH = GLA_HEADS * GLA_DK
GLA_WIDTH = GLA_HEADS * GLA_DV
GLA_GATE_RANK = 16
GLA_GATE_TAU = 16.0
GLA_CHUNK = 64
REL_BUCKETS = 32
REL_MAX_DIST = 128
N_EXPERTS = 256
TOP_K = 8
N_GROUPS = 8
TOPK_GROUPS = 4
GROUP_SIZE = N_EXPERTS // N_GROUPS
EXPERT_FF = 256
SHARED_FF = 256
ROUTED_SCALE = 2.5
NORM_EPS = 1e-6

D_MAIN = 3 * MOBA_WIDTH + 2 * GLA_KEY_WIDTH + 2 * GLA_WIDTH
LANES = 128
VMEM_LIMIT = 56 * 1024 * 1024

ROW_TILE = 512
ROUTE_TILE = 256
EXPERT_TILE = 256
EXPERT_IN_RING = 4
EXPERT_OUT_RING = 3

F32 = jnp.float32
BF16 = jnp.bfloat16
NT_DIMS = (((1,), (1,)), ((), ()))
TN_DIMS = (((0,), (0,)), ((), ()))


def _params(n_axes):
    return pltpu.CompilerParams(dimension_semantics=("arbitrary",) * n_axes,
                                vmem_limit_bytes=VMEM_LIMIT)


def _silu(v):
    return v * jax.nn.sigmoid(v)


def _mod_kernel(c_ref, w_ref, b_ref, o_ref):
    o_ref[...] = jnp.dot(_silu(c_ref[...]), w_ref[...], preferred_element_type=F32) + b_ref[...]


def _mod(c, w, b):
    rows = 8
    cp = jnp.zeros((rows, D_MODEL), F32).at[:c.shape[0]].set(c)
    n = w.shape[1]
    tn = 1024
    out = pl.pallas_call(
        _mod_kernel,
        grid=(n // tn,),
        in_specs=[pl.BlockSpec((rows, D_MODEL), lambda j: (0, 0)),
                  pl.BlockSpec((D_MODEL, tn), lambda j: (0, j)),
                  pl.BlockSpec((1, tn), lambda j: (0, j))],
        out_specs=pl.BlockSpec((rows, tn), lambda j: (0, j)),
        out_shape=jax.ShapeDtypeStruct((rows, n), F32),
        compiler_params=_params(1),
        name="mod",
    )(cp, w, b.reshape(1, n))
    return out[:c.shape[0]]


def _group_rms_inv(a, group):
    lane = lax.broadcasted_iota(jnp.int32, (1, a.shape[1]), 1)
    a2 = a * a
    inv = jnp.zeros_like(a)
    for g in range(a.shape[1] // group):
        m = (lane >= g * group) & (lane < (g + 1) * group)
        ss = jnp.sum(jnp.where(m, a2, 0.0), axis=-1, keepdims=True)
        inv = jnp.where(m, lax.rsqrt(ss * (1.0 / group) + NORM_EPS), inv)
    return inv


def _inproj_kernel(x_ref, sc_ref, sh_ref, nw_ref, w_ref, wga_ref, qw_ref, kw_ref, o_ref, ga_ref):
    x = x_ref[...]
    ms = jnp.mean(x * x, axis=-1, keepdims=True)
    h = x * lax.rsqrt(ms + NORM_EPS) * nw_ref[...]
    h = h * (1.0 + sc_ref[...]) + sh_ref[...]
    hb = h.astype(BF16)
    cw = 256
    for j in range(D_MAIN // cw):
        acc = jnp.dot(hb, w_ref[:, j * cw:(j + 1) * cw], preferred_element_type=F32)
        if j < 2 * MOBA_WIDTH // cw:
            nw = qw_ref if j < MOBA_WIDTH // cw else kw_ref
            acc = acc * _group_rms_inv(acc, MOBA_HEAD_DIM) * nw[...]
        o_ref[:, j * cw:(j + 1) * cw] = acc.astype(BF16)
    ga_ref[...] = jnp.dot(hb, wga_ref[...], preferred_element_type=F32)


def _inproj(x2, sc, sh, nw, w_main, w_ga, qw, kw, seq):
    t = x2.shape[0]
    tpb = seq // ROW_TILE
    vec = lambda: pl.BlockSpec((None, 1, D_MODEL), lambda i: (i // tpb, 0, 0))
    full = lambda a: pl.BlockSpec(a.shape, lambda i: (0,) * a.ndim)
    return pl.pallas_call(
        _inproj_kernel,
        grid=(t // ROW_TILE,),
        in_specs=[pl.BlockSpec((ROW_TILE, D_MODEL), lambda i: (i, 0)), vec(), vec(),
                  full(nw), full(w_main), full(w_ga), full(qw), full(kw)],
        out_specs=[pl.BlockSpec((ROW_TILE, D_MAIN), lambda i: (i, 0)),
                   pl.BlockSpec((ROW_TILE, LANES), lambda i: (i, 0))],
        out_shape=[jax.ShapeDtypeStruct((t, D_MAIN), BF16),
                   jax.ShapeDtypeStruct((t, LANES), F32)],
        compiler_params=_params(1),
        name="inproj",
    )(x2, sc, sh, nw, w_main, w_ga, qw, kw)


def _t5_bucket_np(rel):
    max_exact = REL_BUCKETS // 2
    relf = np.maximum(rel, 1).astype(np.float64)
    large = max_exact + (np.log(relf / max_exact) / math.log(REL_MAX_DIST / max_exact)
                         * (REL_BUCKETS - max_exact)).astype(np.int32)
    large = np.minimum(large, REL_BUCKETS - 1)
    return np.where(rel < max_exact, rel, large)


def _bias_kernel(rb_ref, idx_ref, o_ref):
    h = pl.program_id(0)
    idx = idx_ref[...]
    tab = jnp.full(idx.shape, -jnp.inf, F32)
    for bk in range(REL_BUCKETS):
        tab = jnp.where(idx == bk, rb_ref[bk * MOBA_HEADS + h], tab)
    o_ref[...] = tab


def _moba_bias_tables(rel_bias):
    j = np.arange(MOBA_BLOCK)[:, None]
    i = np.arange(MOBA_BLOCK)[None, :]
    own_idx = np.where(j <= i, _t5_bucket_np(np.maximum(i - j, 0)), -1)
    prev_idx = _t5_bucket_np(MOBA_BLOCK + i - j)
    idx = jnp.asarray(np.concatenate([prev_idx, own_idx], axis=0).astype(np.int32))
    assert int(_t5_bucket_np(np.array([MOBA_BLOCK + 1]))[0]) == REL_BUCKETS - 1
    rb = rel_bias.astype(F32)
    near = pl.pallas_call(
        _bias_kernel,
        grid=(MOBA_HEADS,),
        in_specs=[pl.BlockSpec(memory_space=pltpu.SMEM),
                  pl.BlockSpec(idx.shape, lambda h: (0, 0))],
        out_specs=pl.BlockSpec((None,) + idx.shape, lambda h: (h, 0, 0)),
        out_shape=jax.ShapeDtypeStruct((MOBA_HEADS,) + idx.shape, F32),
        compiler_params=_params(1),
        name="bias",
    )(rb.reshape(-1), idx)
    return near, rb[REL_BUCKETS - 1]


FAR_GROUP = 4


def _moba_kernel(far_ref, q_ref, k_ref, v_ref, near_ref, ow_ref, o_ref,
                 vt_ref, vtg_ref, acc_ref, m_ref, sel_ref, s_ref, mx_ref):
    hp = pl.program_id(1)
    i = pl.program_id(2)
    nblk = k_ref.shape[0] // MOBA_BLOCK
    ngrp = nblk // FAR_GROUP
    hd = MOBA_HEAD_DIM
    bs = MOBA_BLOCK
    lane = lax.broadcasted_iota(jnp.int32, (bs, LANES), 1)

    def split_heads(qb):
        zero = jnp.zeros_like(qb)
        return jnp.where(lane < hd, qb, zero), jnp.where(lane < hd, zero, qb)

    @pl.when(i == 0)
    def _prepare():
        row = lax.broadcasted_iota(jnp.int32, (LANES, bs), 0)
        kmeans = []
        for n in range(nblk):
            kb = k_ref[n * bs:(n + 1) * bs, :].astype(F32)
            kmeans.append(jnp.mean(kb, axis=0, keepdims=True))
            vt = v_ref[n * bs:(n + 1) * bs, :].astype(F32).T
            vt0 = jnp.where(row < hd, vt, 1.0).astype(BF16)
            vt1 = jnp.where(row < hd, 1.0, vt).astype(BF16)
            vt_ref[0, n] = vt0
            vt_ref[1, n] = vt1
            gcols = slice((n % FAR_GROUP) * bs, (n % FAR_GROUP + 1) * bs)
            vtg_ref[0, n // FAR_GROUP, :, gcols] = vt0
            vtg_ref[1, n // FAR_GROUP, :, gcols] = vt1
        kmean = jnp.concatenate(kmeans, axis=0)
        km_hi = kmean.astype(BF16)
        km_lo = (kmean - km_hi.astype(F32)).astype(BF16)
        blk = lax.broadcasted_iota(jnp.int32, (nblk, bs), 0)
        for ib in range(nblk):
            qparts = split_heads(q_ref[ib * bs:(ib + 1) * bs, :])
            for h in range(2):
                gt = (lax.dot_general(km_hi, qparts[h], NT_DIMS, preferred_element_type=F32)
                      + lax.dot_general(km_lo, qparts[h], NT_DIMS, preferred_element_type=F32))
                gt = jnp.where(blk < ib, gt, -jnp.inf)
                cnt = jnp.zeros(gt.shape, jnp.int32)
                for m in range(ib):
                    gm = gt[m:m + 1, :]
                    cnt = cnt + jnp.where((gm > gt) | ((gm == gt) & (blk > m)), 1, 0)
                keep = (blk < ib) & (cnt < MOBA_TOPK)
                sel_ref[0, h, ib] = jnp.where(keep, 1.0, 0.0)
                sel_ref[1, h, ib] = jnp.where(keep & (blk < ib - 1), 1.0, 0.0)

    qh = split_heads(q_ref[pl.ds(pl.multiple_of(i * bs, bs), bs), :])

    def finish():
        a0 = acc_ref[0]
        a1 = acc_ref[1]
        row = lax.broadcasted_iota(jnp.int32, a0.shape, 0)
        ot = jnp.where(row < hd, a0 / a0[hd:hd + 1, :], a1 / a1[0:1, :])
        o2 = ot * ot
        ss0 = jnp.sum(jnp.where(row < hd, o2, 0.0), axis=0, keepdims=True)
        ss1 = jnp.sum(jnp.where(row < hd, 0.0, o2), axis=0, keepdims=True)
        inv = jnp.where(row < hd, lax.rsqrt(ss0 * (1.0 / hd) + NORM_EPS), lax.rsqrt(ss1 * (1.0 / hd) + NORM_EPS))
        o_ref[...] = ((ot * inv).T * ow_ref[...]).astype(o_ref.dtype)

    @pl.when(i == 0)
    def _own_block_only():
        kb = k_ref[0:bs, :]
        for h in range(2):
            s = lax.dot_general(kb, qh[h], NT_DIMS, preferred_element_type=F32) + near_ref[h, bs:2 * bs, :]
            m_new = jnp.max(s, axis=0, keepdims=True)
            p = jnp.exp(s - m_new).astype(BF16)
            acc_ref[h] = jnp.dot(vt_ref[h, 0], p, preferred_element_type=F32)
        finish()

    n_far = (i + FAR_GROUP - 2) // FAR_GROUP
    gk = FAR_GROUP * bs

    def far_scores(g):
        kb = k_ref[g * gk:(g + 1) * gk, :]
        for h in range(2):
            s = lax.dot_general(kb, qh[h], NT_DIMS, preferred_element_type=F32)
            s_ref[g % 2, h] = s
            for j in range(FAR_GROUP):
                mx_ref[g % 2, h, j] = jnp.max(s[j * bs:(j + 1) * bs], axis=0, keepdims=True)

    def near_blocks(with_far):
        kbs = (k_ref[pl.ds(pl.multiple_of((i - 1) * bs, bs), bs), :],
               k_ref[pl.ds(pl.multiple_of(i * bs, bs), bs), :])
        ss = [[lax.dot_general(kbs[w], qh[h], NT_DIMS, preferred_element_type=F32)
               + near_ref[h, w * bs:(w + 1) * bs, :] for w in range(2)] for h in range(2)]
        if with_far:
            far_scores(0)
        ps, ms = [], []
        for h in range(2):
            s_prev, s_own = ss[h]
            keep = sel_ref[0, h, i, pl.ds(i - 1, 1), :] > 0.5
            mx = jnp.where(keep, jnp.max(s_prev, axis=0, keepdims=True), -jnp.inf)
            m_new = jnp.maximum(jnp.max(s_own, axis=0, keepdims=True), mx)
            ps.append((jnp.exp(s_prev - jnp.where(keep, m_new, jnp.inf)).astype(BF16),
                       jnp.exp(s_own - m_new).astype(BF16)))
            ms.append(m_new)
        for h in range(2):
            acc_ref[h] = (jnp.dot(vt_ref[h, i - 1], ps[h][0], preferred_element_type=F32)
                          + jnp.dot(vt_ref[h, i], ps[h][1], preferred_element_type=F32))
            m_ref[h] = ms[h]

    def far_group(g, with_next):
        if with_next:
            far_scores(g + 1)
        for h in range(2):
            fb = far_ref[2 * hp + h]
            m_old = m_ref[h]
            m_new = m_old
            keeps = []
            for j in range(FAR_GROUP):
                keep = sel_ref[1, h, i, pl.ds(g * FAR_GROUP + j, 1), :] > 0.5
                m_new = jnp.maximum(m_new, jnp.where(keep, mx_ref[g % 2, h, j] + fb, -jnp.inf))
                keeps.append(keep)
            p = jnp.concatenate(
                [jnp.exp(s_ref[g % 2, h, j * bs:(j + 1) * bs, :]
                         - jnp.where(keeps[j], m_new - fb, jnp.inf)).astype(BF16)
                 for j in range(FAR_GROUP)], axis=0)
            pv = jnp.dot(vtg_ref[h, g], p, preferred_element_type=F32)
            acc_ref[h] = acc_ref[h] * jnp.exp(m_old - m_new) + pv
            m_ref[h] = m_new

    def step_body(nf):
        near_blocks(nf > 0)
        for g in range(nf):
            far_group(g, g + 1 < nf)
        finish()

    for nf in range(ngrp + 1):
        pl.when((i >= 1) & (n_far == nf))(functools.partial(step_body, nf))


def _moba(proj3, near, far, ow):
    b, s, _ = proj3.shape
    nblk = s // MOBA_BLOCK
    assert nblk % FAR_GROUP == 0
    npair = MOBA_HEADS // 2
    kcol = MOBA_WIDTH // LANES
    return pl.pallas_call(
        _moba_kernel,
        grid=(b, npair, nblk),
        in_specs=[pl.BlockSpec(memory_space=pltpu.SMEM),
                  pl.BlockSpec((None, s, LANES), lambda bb, hp, i: (bb, 0, hp)),
                  pl.BlockSpec((None, s, LANES), lambda bb, hp, i: (bb, 0, kcol + hp)),
                  pl.BlockSpec((None, s, LANES), lambda bb, hp, i: (bb, 0, 2 * kcol + hp)),
                  pl.BlockSpec((2, 2 * MOBA_BLOCK, MOBA_BLOCK), lambda bb, hp, i: (hp, 0, 0)),
                  pl.BlockSpec((1, LANES), lambda bb, hp, i: (0, 0))],
        out_specs=pl.BlockSpec((None, MOBA_BLOCK, LANES), lambda bb, hp, i: (bb, i, hp)),
        out_shape=jax.ShapeDtypeStruct((b, s, MOBA_WIDTH), BF16),
        scratch_shapes=[pltpu.VMEM((2, nblk, LANES, MOBA_BLOCK), BF16),
                        pltpu.VMEM((2, nblk // FAR_GROUP, LANES, FAR_GROUP * MOBA_BLOCK), BF16),
                        pltpu.VMEM((2, LANES, MOBA_BLOCK), F32),
                        pltpu.VMEM((2, 1, MOBA_BLOCK), F32),
                        pltpu.VMEM((2, 2, nblk, nblk, MOBA_BLOCK), F32),
                        pltpu.VMEM((2, 2, FAR_GROUP * MOBA_BLOCK, MOBA_BLOCK), F32),
                        pltpu.VMEM((2, 2, FAR_GROUP, 1, MOBA_BLOCK), F32)],
        compiler_params=_params(3),
        name="moba",
    )(far, proj3, proj3, proj3, near, ow)


def _split3(v):
    hi = v.astype(BF16)
    r1 = v - hi.astype(F32)
    mid = r1.astype(BF16)
    lo = (r1 - mid.astype(F32)).astype(BF16)
    return hi, mid, lo


GLA_UNROLL = 4


def _gla_kernel(q_ref, k_ref, v_ref, g_ref, ga_ref, wal_ref, bal_ref, gw_ref, o_ref, b_ref, st_ref):
    seq = q_ref.shape[0]
    c = GLA_CHUNK
    pc = 256

    rr = lax.broadcasted_iota(jnp.int32, (pc, pc), 0)
    cc = lax.broadcasted_iota(jnp.int32, (pc, pc), 1)
    tri = jnp.where((rr >= cc) & (rr // c == cc // c), 1.0, 0.0).astype(BF16)

    def decay_body(j, carry):
        rows = [pl.ds(pl.multiple_of((j * GLA_UNROLL + u) * pc, pc), pc) for u in range(GLA_UNROLL)]
        xg = [jnp.dot(ga_ref[r, :], wal_ref[...], preferred_element_type=F32) + bal_ref[...] for r in rows]
        parts = [_split3((jnp.minimum(x, 0.0) - jnp.log(1.0 + jnp.exp(-jnp.abs(x)))) * (1.0 / GLA_GATE_TAU))
                 for x in xg]
        sums = [[jnp.dot(tri, term, preferred_element_type=F32) for term in p] for p in parts]
        for r, (hi, mid, lo) in zip(rows, sums):
            b_ref[r, :] = hi + mid + lo
        return carry

    lax.fori_loop(0, seq // (pc * GLA_UNROLL), decay_body, 0)

    st_ref[...] = jnp.zeros_like(st_ref)
    lane = lax.broadcasted_iota(jnp.int32, (c, LANES), 1)
    head_mask = (lane < GLA_DK, lane >= GLA_DK)
    causal = lax.broadcasted_iota(jnp.int32, (c, c), 0) >= lax.broadcasted_iota(jnp.int32, (c, c), 1)

    units = [(u, h) for u in range(GLA_UNROLL) for h in range(2)]

    def chunk_body(ci, carry):
        rows = [pl.ds(pl.multiple_of((ci * GLA_UNROLL + u) * c, c), c) for u in range(GLA_UNROLL)]
        qt, kt, qs, ke, e_last = [], [], [], [], []
        for u in range(GLA_UNROLL):
            b = b_ref[rows[u], :]
            ref_row = b[c // 2 - 1:c // 2, :]
            last = b[c - 1:c, :]
            q = q_ref[rows[u], :].astype(F32) * (GLA_DK ** -0.5)
            k = k_ref[rows[u], :].astype(F32)
            qt.append(q * jnp.exp(b - ref_row))
            kt.append((k * jnp.exp(ref_row - b)).astype(BF16))
            qs.append(q * jnp.exp(b))
            ke.append((k * jnp.exp(last - b)).astype(BF16))
            e_last.append(jnp.exp(last))
        vs = {(u, h): v_ref[rows[u], h * GLA_DV:(h + 1) * GLA_DV] for u, h in units}
        a = {(u, h): lax.dot_general(jnp.where(head_mask[h], qt[u], 0.0).astype(BF16), kt[u], NT_DIMS,
                                     preferred_element_type=F32) for u, h in units}
        inc = {(u, h): lax.dot_general(vs[u, h], ke[u], TN_DIMS, preferred_element_type=F32) for u, h in units}
        o = {(u, h): jnp.dot(jnp.where(causal, a[u, h], 0.0).astype(BF16), vs[u, h], preferred_element_type=F32)
             for u, h in units}
        states = {}
        for h in range(2):
            st = st_ref[h]
            for u in range(GLA_UNROLL):
                states[u, h] = st
                st = st * e_last[u] + inc[u, h]
            st_ref[h] = st
        for u, h in units:
            cols = slice(h * GLA_DV, (h + 1) * GLA_DV)
            ou = o[u, h] + lax.dot_general(jnp.where(head_mask[h], qs[u], 0.0).astype(BF16),
                                           states[u, h].astype(BF16), NT_DIMS, preferred_element_type=F32)
            ms = jnp.mean(ou * ou, axis=-1, keepdims=True)
            on = ou * lax.rsqrt(ms + NORM_EPS) * gw_ref[...]
            g = g_ref[rows[u], cols].astype(F32)
            o_ref[rows[u], cols] = (on * _silu(g)).astype(o_ref.dtype)
        return carry

    lax.fori_loop(0, seq // (c * GLA_UNROLL), chunk_body, 0)


def _gla(proj3, ga3, wal, bal, gw):
    b, s, _ = proj3.shape
    npair = GLA_HEADS // 2
    qcol = 3 * MOBA_WIDTH // LANES
    kcol = qcol + GLA_KEY_WIDTH // LANES
    vcol = (3 * MOBA_WIDTH + 2 * GLA_KEY_WIDTH) // (2 * GLA_DV)
    gcol = vcol + npair
    return pl.pallas_call(
        _gla_kernel,
        grid=(b, npair),
        in_specs=[pl.BlockSpec((None, s, LANES), lambda bb, hp: (bb, 0, qcol + hp)),
                  pl.BlockSpec((None, s, LANES), lambda bb, hp: (bb, 0, kcol + hp)),
                  pl.BlockSpec((None, s, 2 * GLA_DV), lambda bb, hp: (bb, 0, vcol + hp)),
                  pl.BlockSpec((None, s, 2 * GLA_DV), lambda bb, hp: (bb, 0, gcol + hp)),
                  pl.BlockSpec((None, s, LANES), lambda bb, hp: (bb, 0, 0)),
                  pl.BlockSpec((LANES, LANES), lambda bb, hp: (0, hp)),
                  pl.BlockSpec((1, LANES), lambda bb, hp: (0, hp)),
                  pl.BlockSpec((1, GLA_DV), lambda bb, hp: (0, 0))],
        out_specs=pl.BlockSpec((None, s, 2 * GLA_DV), lambda bb, hp: (bb, 0, hp)),
        out_shape=jax.ShapeDtypeStruct((b, s, GLA_WIDTH), BF16),
        scratch_shapes=[pltpu.VMEM((s, LANES), F32),
                        pltpu.VMEM((2, GLA_DV, LANES), F32)],
        compiler_params=_params(2),
        name="gla",
    )(proj3, proj3, proj3, proj3, ga3, wal, bal, gw)


HALF = D_MODEL // 2


def _pack_rows(v):
    return pltpu.pack_elementwise([v[:, :HALF], v[:, HALF:]], packed_dtype=BF16)


def _unpack_rows(w):
    return (pltpu.unpack_elementwise(w, index=0, packed_dtype=BF16, unpacked_dtype=F32),
            pltpu.unpack_elementwise(w, index=1, packed_dtype=BF16, unpacked_dtype=F32))


def _outproj_kernel(oa_ref, ob_ref, x_ref, g1_ref, sc_ref, sh_ref, g2_ref, nw_ref, wo_ref,
                    ws1_ref, ws3_ref, ws2_ref, wrt_ref, base_ref, h_ref, st_ref):
    mix = (jnp.dot(oa_ref[...], wo_ref[:MOBA_WIDTH, :], preferred_element_type=F32)
           + jnp.dot(ob_ref[...], wo_ref[MOBA_WIDTH:, :], preferred_element_type=F32))
    x1 = x_ref[...] + g1_ref[...] * mix
    ms = jnp.mean(x1 * x1, axis=-1, keepdims=True)
    h = x1 * lax.rsqrt(ms + NORM_EPS) * nw_ref[...]
    h = h * (1.0 + sc_ref[...]) + sh_ref[...]
    h_ref[...] = _pack_rows(h)
    hb = h.astype(BF16)
    a = jnp.dot(hb, ws1_ref[...], preferred_element_type=F32)
    u = jnp.dot(hb, ws3_ref[...], preferred_element_type=F32)
    shared = jnp.dot((_silu(a) * u).astype(BF16), ws2_ref[...], preferred_element_type=F32)
    base_ref[...] = x1 + g2_ref[...] * shared
    logits_t = lax.dot_general(wrt_ref[...], hb, NT_DIMS, preferred_element_type=F32)
    st_ref[...] = jax.nn.sigmoid(logits_t)


def _outproj(oa, ob, x2, g1, sc, sh, g2, nw, wo, ws1, ws3, ws2, wrt, seq):
    t = x2.shape[0]
    tpb = seq // ROW_TILE
    vec = lambda: pl.BlockSpec((None, 1, D_MODEL), lambda i: (i // tpb, 0, 0))
    full = lambda a: pl.BlockSpec(a.shape, lambda i: (0,) * a.ndim)
    rows = lambda w: pl.BlockSpec((ROW_TILE, w), lambda i: (i, 0))
    return pl.pallas_call(
        _outproj_kernel,
        grid=(t // ROW_TILE,),
        in_specs=[rows(MOBA_WIDTH), rows(GLA_WIDTH), rows(D_MODEL), vec(), vec(), vec(), vec(),
                  full(nw), full(wo), full(ws1), full(ws3), full(ws2), full(wrt)],
        out_specs=[rows(D_MODEL), rows(HALF), pl.BlockSpec((N_EXPERTS, ROW_TILE), lambda i: (0, i))],
        out_shape=[jax.ShapeDtypeStruct((t, D_MODEL), F32),
                   jax.ShapeDtypeStruct((t, HALF), jnp.uint32),
                   jax.ShapeDtypeStruct((N_EXPERTS, t), F32)],
        compiler_params=_params(1),
        name="outproj",
    )(oa, ob, x2, g1, sc, sh, g2, nw, wo, ws1, ws3, ws2, wrt)


SLOT_CODE_SHIFT = 16
SLOT_CODE_BASE = 1 << SLOT_CODE_SHIFT


def _route_kernel(s_ref, eb_ref, code_ref, w_ref, cnt_ref, carry_ref):
    i = pl.program_id(0)
    ne, nt = s_ref.shape

    @pl.when(i == 0)
    def _init():
        carry_ref[...] = jnp.zeros_like(carry_ref)

    s = s_ref[...]
    choice = s + eb_ref[...]
    gio = lax.broadcasted_iota(jnp.int32, (GROUP_SIZE, nt), 0)
    gscore = []
    for g in range(N_GROUPS):
        cg = choice[g * GROUP_SIZE:(g + 1) * GROUP_SIZE, :]
        top1 = jnp.max(cg, axis=0, keepdims=True)
        first = jnp.min(jnp.where(cg == top1, gio, GROUP_SIZE), axis=0, keepdims=True)
        top2 = jnp.max(jnp.where(gio == first, -jnp.inf, cg), axis=0, keepdims=True)
        gscore.append(top1 + top2)
    gs = jnp.concatenate(gscore, axis=0)
    gidx = lax.broadcasted_iota(jnp.int32, gs.shape, 0)
    beaten = jnp.zeros(gs.shape, jnp.int32)
    for m in range(N_GROUPS):
        gm = gs[m:m + 1, :]
        beaten = beaten + jnp.where((gm > gs) | ((gm == gs) & (gidx > m)), 1, 0)
    gkeep = beaten < TOPK_GROUPS
    masked = jnp.concatenate(
        [jnp.where(gkeep[g:g + 1, :], choice[g * GROUP_SIZE:(g + 1) * GROUP_SIZE, :], -jnp.inf)
         for g in range(N_GROUPS)], axis=0)

    eio = lax.broadcasted_iota(jnp.int32, (ne, nt), 0)
    picked = jnp.zeros((ne, nt), F32)
    idx_rows, w_rows, hits = [], [], []
    for _ in range(TOP_K):
        mx = jnp.max(masked, axis=0, keepdims=True)
        idx = jnp.min(jnp.where(masked == mx, eio, ne), axis=0, keepdims=True)
        hit = eio == idx
        w_rows.append(jnp.sum(jnp.where(hit, s, 0.0), axis=0, keepdims=True))
        idx_rows.append(idx)
        hits.append(hit)
        masked = jnp.where(hit, -jnp.inf, masked)
        picked = jnp.where(hit, 1.0, picked)
    wk = jnp.concatenate(w_rows, axis=0)
    w_ref[...] = wk / jnp.sum(wk, axis=0, keepdims=True) * ROUTED_SCALE

    tr = lax.broadcasted_iota(jnp.int32, (nt, nt), 0)
    tc = lax.broadcasted_iota(jnp.int32, (nt, nt), 1)
    before = jnp.where(tr < tc, 1.0, 0.0).astype(BF16)
    pb = picked.astype(BF16)
    pos = carry_ref[...] + jnp.dot(pb, before, preferred_element_type=F32)
    rank = jnp.concatenate(
        [jnp.sum(jnp.where(hit, pos, 0.0), axis=0, keepdims=True) for hit in hits], axis=0).astype(jnp.int32)
    code_ref[...] = jnp.concatenate(idx_rows, axis=0) * SLOT_CODE_BASE + rank
    total = carry_ref[...] + jnp.dot(pb, jnp.ones((nt, nt), BF16), preferred_element_type=F32)
    carry_ref[...] = total
    cnt_ref[...] = total


def _route(scores_t, eb):
    ne, t = scores_t.shape
    assert t <= SLOT_CODE_BASE
    nt = ROUTE_TILE
    tok = lambda dt: jax.ShapeDtypeStruct((TOP_K, t), dt)
    return pl.pallas_call(
        _route_kernel,
        grid=(t // nt,),
        in_specs=[pl.BlockSpec((ne, nt), lambda i: (0, i)),
                  pl.BlockSpec((ne, nt), lambda i: (0, 0))],
        out_specs=[pl.BlockSpec((TOP_K, nt), lambda i: (0, i)),
                   pl.BlockSpec((TOP_K, nt), lambda i: (0, i)),
                   pl.BlockSpec((ne, nt), lambda i: (0, 0))],
        out_shape=[tok(jnp.int32), tok(F32), jax.ShapeDtypeStruct((ne, nt), F32)],
        scratch_shapes=[pltpu.VMEM((ne, nt), F32)],
        compiler_params=_params(1),
        name="route",
    )(scores_t, eb)


SLOT_TILE = 2048


def _slots_kernel(pstart_ref, code_ref, o_ref):
    code = code_ref[...]
    expert = lax.shift_right_logical(code, SLOT_CODE_SHIFT)

    def body(e, acc):
        return jnp.where(expert == e, pstart_ref[e], acc)

    start = lax.fori_loop(0, N_EXPERTS, body, jnp.zeros_like(code), unroll=8)
    o_ref[...] = start + (code & (SLOT_CODE_BASE - 1))


def _slots(pstart, code_t):
    k, t = code_t.shape
    return pl.pallas_call(
        _slots_kernel,
        grid_spec=pltpu.PrefetchScalarGridSpec(
            num_scalar_prefetch=1,
            grid=(t // SLOT_TILE,),
            in_specs=[pl.BlockSpec((k, SLOT_TILE), lambda i, p: (0, i))],
            out_specs=pl.BlockSpec((k, SLOT_TILE), lambda i, p: (0, i)),
        ),
        out_shape=jax.ShapeDtypeStruct((k, t), jnp.int32),
        compiler_params=_params(1),
        name="slots",
    )(pstart, code_t)


SC_WINDOW = 128


def _sc_gather_rows(table, idx_flat):
    from jax.experimental.pallas import tpu_sc as plsc
    info = plsc.get_sparse_core_info()
    nw = info.num_cores * info.num_subcores
    n = idx_flat.shape[0]
    width = table.shape[1]
    per_worker = n // nw
    assert per_worker * nw == n and per_worker % SC_WINDOW == 0
    mesh = plsc.VectorSubcoreMesh(core_axis_name="c", subcore_axis_name="s")

    def body(table_hbm, idx_hbm, out_hbm, idx_v, rows_v, sem):
        wid = lax.axis_index("s") * info.num_cores + lax.axis_index("c")
        base = wid * per_worker

        @pl.loop(0, per_worker // SC_WINDOW)
        def _(w):
            off = pl.multiple_of(base + w * SC_WINDOW, SC_WINDOW)
            pltpu.sync_copy(idx_hbm.at[pl.ds(off, SC_WINDOW)], idx_v)
            pltpu.async_copy(table_hbm.at[idx_v], rows_v, sem).wait()
            pltpu.sync_copy(rows_v, out_hbm.at[pl.ds(off, SC_WINDOW)])

    return pl.kernel(
        body,
        out_type=jax.ShapeDtypeStruct((n, width), table.dtype),
        mesh=mesh,
        scratch_types=[pltpu.VMEM((SC_WINDOW,), jnp.int32),
                       pltpu.VMEM((SC_WINDOW, width), table.dtype),
                       pltpu.SemaphoreType.DMA],
        name="sc_gather",
    )(table, idx_flat)


def _sc_scatter_rows(rows, idx_kt, n_out):
    from jax.experimental.pallas import tpu_sc as plsc
    info = plsc.get_sparse_core_info()
    nw = info.num_cores * info.num_subcores
    t, width = rows.shape
    nk = idx_kt.shape[0]
    per_worker = t // nw
    assert per_worker * nw == t and per_worker % SC_WINDOW == 0
    mesh = plsc.VectorSubcoreMesh(core_axis_name="c", subcore_axis_name="s")

    def body(rows_hbm, idx_hbm, out_hbm, idx_v, rows_v, sem):
        wid = lax.axis_index("s") * info.num_cores + lax.axis_index("c")
        base = wid * per_worker

        @pl.loop(0, per_worker // SC_WINDOW)
        def _(w):
            off = pl.multiple_of(base + w * SC_WINDOW, SC_WINDOW)
            pltpu.sync_copy(rows_hbm.at[pl.ds(off, SC_WINDOW)], rows_v)
            pltpu.sync_copy(idx_hbm.at[:, pl.ds(off, SC_WINDOW)], idx_v)
            copies = [pltpu.async_copy(rows_v, out_hbm.at[idx_v.at[k]], sem) for k in range(nk)]
            for cp in copies:
                cp.wait()

    return pl.kernel(
        body,
        out_type=jax.ShapeDtypeStruct((n_out, width), rows.dtype),
        mesh=mesh,
        scratch_types=[pltpu.VMEM((nk, SC_WINDOW), jnp.int32),
                       pltpu.VMEM((SC_WINDOW, width), rows.dtype),
                       pltpu.SemaphoreType.DMA],
        name="sc_scatter",
    )(rows, idx_kt)


def _expert_kernel(first_ref, count_ref, used_ref, w1_ref, w3_ref, w2_ref, xs_ref, ys_ref,
                   xbuf, ybuf, sem_in, sem_out, w1f, w3f, w2f, sem_w, w1b, w3b, w2b, *, layer):
    e = pl.program_id(0)
    ne = pl.num_programs(0)
    r = xbuf.shape[1]
    n_used = used_ref[0]

    def x_copy(g, slot):
        return pltpu.make_async_copy(xs_ref.at[pl.ds(pl.multiple_of(g * r, r), r), :], xbuf.at[slot], sem_in.at[slot])

    def y_copy(g, slot):
        return pltpu.make_async_copy(ybuf.at[slot], ys_ref.at[pl.ds(pl.multiple_of(g * r, r), r), :], sem_out.at[slot])

    def w_copies(ex, slot):
        return [pltpu.make_async_copy(src.at[layer, ex], dst.at[slot], sem_w.at[slot])
                for src, dst in ((w1_ref, w1f), (w3_ref, w3f), (w2_ref, w2f))]

    nin = xbuf.shape[0]
    nout = ybuf.shape[0]

    @pl.when(e == 0)
    def _first_reads():
        for g in range(nin - 1):
            @pl.when(g < n_used)
            def _(g=g):
                x_copy(g, g).start()
        for cp in w_copies(0, 0):
            cp.start(priority=1)

    @pl.when(e + 1 < ne)
    def _next_weights():
        for cp in w_copies(e + 1, (e + 1) & 1):
            cp.start(priority=1)

    for cp in w_copies(e, e & 1):
        cp.wait()

    n = count_ref[e]

    @pl.when(n > 0)
    def _cast_weights():
        w1b[...] = w1f[e & 1].astype(BF16)
        w3b[...] = w3f[e & 1].astype(BF16)
        w2b[...] = w2f[e & 1].astype(BF16)

    def tile_body(g, carry):
        slot = g % nin
        oslot = g % nout
        x_copy(g, slot).wait()

        @pl.when(g + nin - 1 < n_used)
        def _read_ahead():
            x_copy(g + nin - 1, (g + nin - 1) % nin).start()

        @pl.when(g >= nout)
        def _free_out_slot():
            y_copy(g - nout, oslot).wait()

        lo, hi = _unpack_rows(xbuf[slot])
        lo, hi = lo.astype(BF16), hi.astype(BF16)
        a = (jnp.dot(lo, w1b[:HALF, :], preferred_element_type=F32)
             + jnp.dot(hi, w1b[HALF:, :], preferred_element_type=F32))
        u = (jnp.dot(lo, w3b[:HALF, :], preferred_element_type=F32)
             + jnp.dot(hi, w3b[HALF:, :], preferred_element_type=F32))
        y = jnp.dot((_silu(a) * u).astype(BF16), w2b[...], preferred_element_type=F32)
        ybuf[oslot] = _pack_rows(y)
        y_copy(g, oslot).start()
        return carry

    g0 = first_ref[e]
    lax.fori_loop(g0, g0 + n, tile_body, 0)

    @pl.when(e == ne - 1)
    def _drain_writes():
        for back in range(nout, 0, -1):
            @pl.when(n_used >= back)
            def _(back=back):
                y_copy(n_used - back, (n_used - back) % nout).wait()


def _experts(tile_first, tile_count, n_used, xs, w1, w3, w2, layer):
    n_rows = xs.shape[0]
    r = EXPERT_TILE
    any_spec = pl.BlockSpec(memory_space=pl.ANY)
    return pl.pallas_call(
        functools.partial(_expert_kernel, layer=layer),
        grid_spec=pltpu.PrefetchScalarGridSpec(
            num_scalar_prefetch=3,
            grid=(N_EXPERTS,),
            in_specs=[any_spec, any_spec, any_spec, any_spec],
            out_specs=any_spec,
            scratch_shapes=[pltpu.VMEM((EXPERT_IN_RING, r, HALF), jnp.uint32),
                            pltpu.VMEM((EXPERT_OUT_RING, r, HALF), jnp.uint32),
                            pltpu.SemaphoreType.DMA((EXPERT_IN_RING,)),
                            pltpu.SemaphoreType.DMA((EXPERT_OUT_RING,)),
                            pltpu.VMEM((2, D_MODEL, EXPERT_FF), F32),
                            pltpu.VMEM((2, D_MODEL, EXPERT_FF), F32),
                            pltpu.VMEM((2, EXPERT_FF, D_MODEL), F32),
                            pltpu.SemaphoreType.DMA((2,)),
                            pltpu.VMEM((D_MODEL, EXPERT_FF), BF16),
                            pltpu.VMEM((D_MODEL, EXPERT_FF), BF16),
                            pltpu.VMEM((EXPERT_FF, D_MODEL), BF16)],
        ),
        out_shape=jax.ShapeDtypeStruct((n_rows, HALF), jnp.uint32),
        compiler_params=_params(1),
        name="experts",
    )(tile_first, tile_count, n_used, w1, w3, w2, xs)


def _combine_dense_kernel(base_ref, g2_ref, w_ref, yg_ref, o_ref):
    acc_lo = acc_hi = None
    for k in range(TOP_K):
        lo, hi = _unpack_rows(yg_ref[k])
        wk = w_ref[:, k:k + 1]
        acc_lo = wk * lo if acc_lo is None else acc_lo + wk * lo
        acc_hi = wk * hi if acc_hi is None else acc_hi + wk * hi
    o_ref[:, :HALF] = base_ref[:, :HALF] + g2_ref[:, :HALF] * acc_lo
    o_ref[:, HALF:] = base_ref[:, HALF:] + g2_ref[:, HALF:] * acc_hi


def _combine_dense(base, g2, w_tok, yg, batch):
    t = base.shape[0]
    seq = yg.shape[1]
    nt = ROUTE_TILE
    tpb = seq // nt
    rows = lambda i: (batch * tpb + i, 0)
    return pl.pallas_call(
        _combine_dense_kernel,
        grid=(tpb,),
        in_specs=[pl.BlockSpec((nt, D_MODEL), rows),
                  pl.BlockSpec((None, 1, D_MODEL), lambda i: (batch, 0, 0)),
                  pl.BlockSpec((nt, TOP_K), rows),
                  pl.BlockSpec((TOP_K, nt, HALF), lambda i: (0, i, 0))],
        out_specs=pl.BlockSpec((nt, D_MODEL), rows),
        out_shape=jax.ShapeDtypeStruct((t, D_MODEL), F32),
        input_output_aliases={0: 0},
        compiler_params=_params(1),
        name="combine_dense",
    )(base, g2, w_tok, yg)


def _layer(layer, x, c, w_ada, b_ada, norm1_w, norm2_w, w_in, q_norm_w, k_norm_w, rel_bias, w_alpha, b_alpha,
           moba_out_w, gla_out_w, w_out, w_router, e_bias, w1, w3, w2, ws1, ws3, ws2):
    b, s, d = x.shape
    t = b * s
    x2 = x.reshape(t, d)

    mod = _mod(c, w_ada, b_ada)
    sh1, sc1, g1, sh2, sc2, g2 = [mod[:, j * d:(j + 1) * d].reshape(b, 1, d) for j in range(6)]

    w_main = w_in[:, :D_MAIN].astype(BF16)
    w_ga = jnp.zeros((d, LANES), BF16).at[:, :GLA_GATE_RANK].set(w_in[:, D_MAIN:].astype(BF16))
    per_chunk = 256 // MOBA_HEAD_DIM
    qw = jnp.tile(q_norm_w.astype(F32), per_chunk).reshape(1, 256) * (MOBA_HEAD_DIM ** -0.5)
    kw = jnp.tile(k_norm_w.astype(F32), per_chunk).reshape(1, 256)
    proj, ga = _inproj(x2, sc1, sh1, norm1_w.reshape(1, d), w_main, w_ga, qw, kw, s)
    proj3 = proj.reshape(b, s, D_MAIN)

    near, far = _moba_bias_tables(rel_bias)
    ow = jnp.tile(moba_out_w.astype(F32), 2).reshape(1, LANES)
    o_a = _moba(proj3, near, far, ow)

    wal = jnp.zeros((LANES, GLA_KEY_WIDTH), F32).at[:GLA_GATE_RANK].set(w_alpha)
    o_b = _gla(proj3, ga.reshape(b, s, LANES), wal, b_alpha.reshape(1, GLA_KEY_WIDTH),
               gla_out_w.reshape(1, GLA_DV))

    base, h2, scores_t = _outproj(
        o_a.reshape(t, MOBA_WIDTH), o_b.reshape(t, GLA_WIDTH), x2, g1, sc2, sh2, g2,
        norm2_w.reshape(1, d), w_out.astype(BF16), ws1.astype(BF16), ws3.astype(BF16), ws2.astype(BF16),
        w_router.T.astype(BF16), s)

    eb = jnp.broadcast_to(e_bias.astype(F32)[:, None], (N_EXPERTS, ROUTE_TILE))
    code_t, w_t, counts = _route(scores_t, eb)

    r = EXPERT_TILE
    n_tiles = (t * TOP_K + N_EXPERTS * (r - 1) + r - 1) // r
    n_rows = n_tiles * r
    cnt = counts[:, 0].astype(jnp.int32)
    padded = (cnt + r - 1) // r * r
    pend = jnp.cumsum(padded)
    pstart = pend - padded
    n_used = (pend[-1:] // r).astype(jnp.int32)
    dest_t = _slots(pstart, code_t)

    xs = _sc_scatter_rows(h2, dest_t, n_rows)
    ys = _experts(pstart // r, padded // r, n_used, xs, w1, w3, w2, layer)
    w_tok = w_t.T
    out = base
    for bi in range(b):
        idx = dest_t[:, bi * s:(bi + 1) * s].reshape(TOP_K * s)
        yg = _sc_gather_rows(ys, idx).reshape(TOP_K, s, HALF)
        out = _combine_dense(out, g2, w_tok, yg, bi)
    return out.reshape(b, s, d)


def kernel(x, c, w_ada, b_ada, norm1_w, norm2_w, w_in, q_norm_w, k_norm_w, rel_bias, w_alpha, b_alpha,
           moba_out_w, gla_out_w, w_out, w_router, e_bias, w1, w3, w2, ws1, ws3, ws2):
    for l in range(w_ada.shape[0]):
        x = _layer(l, x, c, w_ada[l], b_ada[l], norm1_w[l], norm2_w[l], w_in[l], q_norm_w[l], k_norm_w[l],
                   rel_bias, w_alpha[l], b_alpha[l], moba_out_w[l], gla_out_w[l], w_out[l], w_router[l],
                   e_bias[l], w1, w3, w2, ws1[l], ws3[l], ws2[l])
    return x
```

```python
import functools
import math

import numpy as np
import jax
import jax.numpy as jnp
from jax import lax
from jax.experimental import pallas as pl
from jax.experimental.pallas import tpu as pltpu

D_MODEL = 1024
MOBA_HEADS = 8
MOBA_HEAD_DIM = 64
MOBA_WIDTH = MOBA_HEADS * MOBA_HEAD_DIM
MOBA_BLOCK = 256
MOBA_TOPK = 3
GLA_HEADS = 4
GLA_DK = 64
GLA_DV = 128
GLA_KEY_WIDTH = GLA_HEADS * GLA_DK
GLA_WIDTH = GLA_HEADS * GLA_DV
GLA_GATE_RANK = 16
GLA_GATE_TAU = 16.0
GLA_CHUNK = 64
REL_BUCKETS = 32
REL_MAX_DIST = 128
N_EXPERTS = 256
TOP_K = 8
N_GROUPS = 8
TOPK_GROUPS = 4
GROUP_SIZE = N_EXPERTS // N_GROUPS
EXPERT_FF = 256
SHARED_FF = 256
ROUTED_SCALE = 2.5
NORM_EPS = 1e-6

D_MAIN = 3 * MOBA_WIDTH + 2 * GLA_KEY_WIDTH + 2 * GLA_WIDTH
LANES = 128
VMEM_LIMIT = 56 * 1024 * 1024

ROW_TILE = 512
ROUTE_TILE = 256
EXPERT_TILE = 256
EXPERT_TILES_PER_MATMUL = 2
EXPERT_IN_RING = 6
EXPERT_OUT_RING = 4

F32 = jnp.float32
BF16 = jnp.bfloat16
NT_DIMS = (((1,), (1,)), ((), ()))
TN_DIMS = (((0,), (0,)), ((), ()))


def _params(n_axes):
    return pltpu.CompilerParams(dimension_semantics=("arbitrary",) * n_axes,
                                vmem_limit_bytes=VMEM_LIMIT)


def _silu(v):
    return v * jax.nn.sigmoid(v)


def _mod_kernel(c_ref, w_ref, b_ref, o_ref):
    o_ref[...] = jnp.dot(_silu(c_ref[...]), w_ref[...], preferred_element_type=F32) + b_ref[...]


def _mod(c, w, b):
    rows = 8
    cp = jnp.zeros((rows, D_MODEL), F32).at[:c.shape[0]].set(c)
    n = w.shape[1]
    tn = 1024
    out = pl.pallas_call(
        _mod_kernel,
        grid=(n // tn,),
        in_specs=[pl.BlockSpec((rows, D_MODEL), lambda j: (0, 0)),
                  pl.BlockSpec((D_MODEL, tn), lambda j: (0, j)),
                  pl.BlockSpec((1, tn), lambda j: (0, j))],
        out_specs=pl.BlockSpec((rows, tn), lambda j: (0, j)),
        out_shape=jax.ShapeDtypeStruct((rows, n), F32),
        compiler_params=_params(1),
        name="mod",
    )(cp, w, b.reshape(1, n))
    return out[:c.shape[0]]


def _group_rms_inv(a, group):
    lane = lax.broadcasted_iota(jnp.int32, (1, a.shape[1]), 1)
    a2 = a * a
    inv = jnp.zeros_like(a)
    for g in range(a.shape[1] // group):
        m = (lane >= g * group) & (lane < (g + 1) * group)
        ss = jnp.sum(jnp.where(m, a2, 0.0), axis=-1, keepdims=True)
        inv = jnp.where(m, lax.rsqrt(ss * (1.0 / group) + NORM_EPS), inv)
    return inv


def _inproj_kernel(x_ref, sc_ref, sh_ref, nw_ref, w_ref, wga_ref, qw_ref, kw_ref, o_ref, ga_ref):
    x = x_ref[...]
    ms = jnp.mean(x * x, axis=-1, keepdims=True)
    h = x * lax.rsqrt(ms + NORM_EPS) * nw_ref[...]
    h = h * (1.0 + sc_ref[...]) + sh_ref[...]
    hb = h.astype(BF16)
    cw = 256
    for j in range(D_MAIN // cw):
        acc = jnp.dot(hb, w_ref[:, j * cw:(j + 1) * cw], preferred_element_type=F32)
        if j < 2 * MOBA_WIDTH // cw:
            nw = qw_ref if j < MOBA_WIDTH // cw else kw_ref
            acc = acc * _group_rms_inv(acc, MOBA_HEAD_DIM) * nw[...]
        o_ref[:, j * cw:(j + 1) * cw] = acc.astype(BF16)
    ga_ref[...] = jnp.dot(hb, wga_ref[...], preferred_element_type=F32)


def _inproj(x2, sc, sh, nw, w_main, w_ga, qw, kw, seq):
    t = x2.shape[0]
    tpb = seq // ROW_TILE
    vec = lambda: pl.BlockSpec((None, 1, D_MODEL), lambda i: (i // tpb, 0, 0))
    full = lambda a: pl.BlockSpec(a.shape, lambda i: (0,) * a.ndim)
    return pl.pallas_call(
        _inproj_kernel,
        grid=(t // ROW_TILE,),
        in_specs=[pl.BlockSpec((ROW_TILE, D_MODEL), lambda i: (i, 0)), vec(), vec(),
                  full(nw), full(w_main), full(w_ga), full(qw), full(kw)],
        out_specs=[pl.BlockSpec((ROW_TILE, D_MAIN), lambda i: (i, 0)),
                   pl.BlockSpec((ROW_TILE, LANES), lambda i: (i, 0))],
        out_shape=[jax.ShapeDtypeStruct((t, D_MAIN), BF16),
                   jax.ShapeDtypeStruct((t, LANES), F32)],
        compiler_params=_params(1),
        name="inproj",
    )(x2, sc, sh, nw, w_main, w_ga, qw, kw)


def _t5_bucket_np(rel):
    max_exact = REL_BUCKETS // 2
    relf = np.maximum(rel, 1).astype(np.float64)
    large = max_exact + (np.log(relf / max_exact) / math.log(REL_MAX_DIST / max_exact)
                         * (REL_BUCKETS - max_exact)).astype(np.int32)
    large = np.minimum(large, REL_BUCKETS - 1)
    return np.where(rel < max_exact, rel, large)


def _bias_kernel(rb_ref, idx_ref, o_ref):
    h = pl.program_id(0)
    idx = idx_ref[...]
    tab = jnp.full(idx.shape, -jnp.inf, F32)
    for bk in range(REL_BUCKETS):
        tab = jnp.where(idx == bk, rb_ref[bk * MOBA_HEADS + h], tab)
    o_ref[...] = tab


def _moba_bias_tables(rel_bias):
    j = np.arange(MOBA_BLOCK)[:, None]
    i = np.arange(MOBA_BLOCK)[None, :]
    own_idx = np.where(j <= i, _t5_bucket_np(np.maximum(i - j, 0)), -1)
    prev_idx = _t5_bucket_np(MOBA_BLOCK + i - j)
    idx = jnp.asarray(np.concatenate([prev_idx, own_idx], axis=0).astype(np.int32))
    assert int(_t5_bucket_np(np.array([MOBA_BLOCK + 1]))[0]) == REL_BUCKETS - 1
    rb = rel_bias.astype(F32)
    near = pl.pallas_call(
        _bias_kernel,
        grid=(MOBA_HEADS,),
        in_specs=[pl.BlockSpec(memory_space=pltpu.SMEM),
                  pl.BlockSpec(idx.shape, lambda h: (0, 0))],
        out_specs=pl.BlockSpec((None,) + idx.shape, lambda h: (h, 0, 0)),
        out_shape=jax.ShapeDtypeStruct((MOBA_HEADS,) + idx.shape, F32),
        compiler_params=_params(1),
        name="bias",
    )(rb.reshape(-1), idx)
    return near, rb[REL_BUCKETS - 1]


FAR_GROUP = 4


def _moba_kernel(far_ref, q_ref, k_ref, v_ref, near_ref, ow_ref, o_ref,
                 vt_ref, vtg_ref, acc_ref, m_ref, sel_ref, s_ref, mx_ref):
    hp = pl.program_id(1)
    i = pl.program_id(2)
    nblk = k_ref.shape[0] // MOBA_BLOCK
    ngrp = nblk // FAR_GROUP
    hd = MOBA_HEAD_DIM
    bs = MOBA_BLOCK
    lane = lax.broadcasted_iota(jnp.int32, (bs, LANES), 1)

    def split_heads(qb):
        zero = jnp.zeros_like(qb)
        return jnp.where(lane < hd, qb, zero), jnp.where(lane < hd, zero, qb)

    @pl.when(i == 0)
    def _prepare():
        row = lax.broadcasted_iota(jnp.int32, (LANES, bs), 0)
        kmeans = []
        for n in range(nblk):
            kb = k_ref[n * bs:(n + 1) * bs, :].astype(F32)
            kmeans.append(jnp.mean(kb, axis=0, keepdims=True))
            vt = v_ref[n * bs:(n + 1) * bs, :].astype(F32).T
            vt0 = jnp.where(row < hd, vt, 1.0).astype(BF16)
            vt1 = jnp.where(row < hd, 1.0, vt).astype(BF16)
            vt_ref[0, n] = vt0
            vt_ref[1, n] = vt1
            gcols = slice((n % FAR_GROUP) * bs, (n % FAR_GROUP + 1) * bs)
            vtg_ref[0, n // FAR_GROUP, :, gcols] = vt0
            vtg_ref[1, n // FAR_GROUP, :, gcols] = vt1
        kmean = jnp.concatenate(kmeans, axis=0)
        km_hi = kmean.astype(BF16)
        km_lo = (kmean - km_hi.astype(F32)).astype(BF16)
        blk = lax.broadcasted_iota(jnp.int32, (nblk, bs), 0)
        for ib in range(nblk):
            qparts = split_heads(q_ref[ib * bs:(ib + 1) * bs, :])
            for h in range(2):
                gt = (lax.dot_general(km_hi, qparts[h], NT_DIMS, preferred_element_type=F32)
                      + lax.dot_general(km_lo, qparts[h], NT_DIMS, preferred_element_type=F32))
                gt = jnp.where(blk < ib, gt, -jnp.inf)
                cnt = jnp.zeros(gt.shape, jnp.int32)
                for m in range(ib):
                    gm = gt[m:m + 1, :]
                    cnt = cnt + jnp.where((gm > gt) | ((gm == gt) & (blk > m)), 1, 0)
                keep = (blk < ib) & (cnt < MOBA_TOPK)
                sel_ref[0, h, ib] = jnp.where(keep, 1.0, 0.0)
                sel_ref[1, h, ib] = jnp.where(keep & (blk < ib - 1), 1.0, 0.0)

    qh = split_heads(q_ref[pl.ds(pl.multiple_of(i * bs, bs), bs), :])

    def finish():
        a0 = acc_ref[0]
        a1 = acc_ref[1]
        row = lax.broadcasted_iota(jnp.int32, a0.shape, 0)
        ot = jnp.where(row < hd, a0 / a0[hd:hd + 1, :], a1 / a1[0:1, :])
        o2 = ot * ot
        ss0 = jnp.sum(jnp.where(row < hd, o2, 0.0), axis=0, keepdims=True)
        ss1 = jnp.sum(jnp.where(row < hd, 0.0, o2), axis=0, keepdims=True)
        inv = jnp.where(row < hd, lax.rsqrt(ss0 * (1.0 / hd) + NORM_EPS), lax.rsqrt(ss1 * (1.0 / hd) + NORM_EPS))
        o_ref[...] = ((ot * inv).T * ow_ref[...]).astype(o_ref.dtype)

    @pl.when(i == 0)
    def _own_block_only():
        kb = k_ref[0:bs, :]
        for h in range(2):
            s = lax.dot_general(kb, qh[h], NT_DIMS, preferred_element_type=F32) + near_ref[h, bs:2 * bs, :]
            m_new = jnp.max(s, axis=0, keepdims=True)
            p = jnp.exp(s - m_new).astype(BF16)
            acc_ref[h] = jnp.dot(vt_ref[h, 0], p, preferred_element_type=F32)
        finish()

    n_far = (i + FAR_GROUP - 2) // FAR_GROUP
    gk = FAR_GROUP * bs

    def far_scores(g):
        kb = k_ref[g * gk:(g + 1) * gk, :]
        for h in range(2):
            s = lax.dot_general(kb, qh[h], NT_DIMS, preferred_element_type=F32)
            s_ref[g % 2, h] = s
            for j in range(FAR_GROUP):
                mx_ref[g % 2, h, j] = jnp.max(s[j * bs:(j + 1) * bs], axis=0, keepdims=True)

    def near_blocks(with_far):
        kbs = (k_ref[pl.ds(pl.multiple_of((i - 1) * bs, bs), bs), :],
               k_ref[pl.ds(pl.multiple_of(i * bs, bs), bs), :])
        ss = [[lax.dot_general(kbs[w], qh[h], NT_DIMS, preferred_element_type=F32)
               + near_ref[h, w * bs:(w + 1) * bs, :] for w in range(2)] for h in range(2)]
        if with_far:
            far_scores(0)
        ps, ms = [], []
        for h in range(2):
            s_prev, s_own = ss[h]
            keep = sel_ref[0, h, i, pl.ds(i - 1, 1), :] > 0.5
            mx = jnp.where(keep, jnp.max(s_prev, axis=0, keepdims=True), -jnp.inf)
            m_new = jnp.maximum(jnp.max(s_own, axis=0, keepdims=True), mx)
            ps.append((jnp.exp(s_prev - jnp.where(keep, m_new, jnp.inf)).astype(BF16),
                       jnp.exp(s_own - m_new).astype(BF16)))
            ms.append(m_new)
        for h in range(2):
            acc_ref[h] = (jnp.dot(vt_ref[h, i - 1], ps[h][0], preferred_element_type=F32)
                          + jnp.dot(vt_ref[h, i], ps[h][1], preferred_element_type=F32))
            m_ref[h] = ms[h]

    def far_group(g, with_next):
        if with_next:
            far_scores(g + 1)
        for h in range(2):
            fb = far_ref[2 * hp + h]
            m_old = m_ref[h]
            m_new = m_old
            keeps = []
            for j in range(FAR_GROUP):
                keep = sel_ref[1, h, i, pl.ds(g * FAR_GROUP + j, 1), :] > 0.5
                m_new = jnp.maximum(m_new, jnp.where(keep, mx_ref[g % 2, h, j] + fb, -jnp.inf))
                keeps.append(keep)
            p = jnp.concatenate(
                [jnp.exp(s_ref[g % 2, h, j * bs:(j + 1) * bs, :]
                         - jnp.where(keeps[j], m_new - fb, jnp.inf)).astype(BF16)
                 for j in range(FAR_GROUP)], axis=0)
            pv = jnp.dot(vtg_ref[h, g], p, preferred_element_type=F32)
            acc_ref[h] = acc_ref[h] * jnp.exp(m_old - m_new) + pv
            m_ref[h] = m_new

    def step_body(nf):
        near_blocks(nf > 0)
        for g in range(nf):
            far_group(g, g + 1 < nf)
        finish()

    for nf in range(ngrp + 1):
        pl.when((i >= 1) & (n_far == nf))(functools.partial(step_body, nf))


def _moba(proj3, near, far, ow):
    b, s, _ = proj3.shape
    nblk = s // MOBA_BLOCK
    assert nblk % FAR_GROUP == 0
    npair = MOBA_HEADS // 2
    kcol = MOBA_WIDTH // LANES
    return pl.pallas_call(
        _moba_kernel,
        grid=(b, npair, nblk),
        in_specs=[pl.BlockSpec(memory_space=pltpu.SMEM),
                  pl.BlockSpec((None, s, LANES), lambda bb, hp, i: (bb, 0, hp)),
                  pl.BlockSpec((None, s, LANES), lambda bb, hp, i: (bb, 0, kcol + hp)),
                  pl.BlockSpec((None, s, LANES), lambda bb, hp, i: (bb, 0, 2 * kcol + hp)),
                  pl.BlockSpec((2, 2 * MOBA_BLOCK, MOBA_BLOCK), lambda bb, hp, i: (hp, 0, 0)),
                  pl.BlockSpec((1, LANES), lambda bb, hp, i: (0, 0))],
        out_specs=pl.BlockSpec((None, MOBA_BLOCK, LANES), lambda bb, hp, i: (bb, i, hp)),
        out_shape=jax.ShapeDtypeStruct((b, s, MOBA_WIDTH), BF16),
        scratch_shapes=[pltpu.VMEM((2, nblk, LANES, MOBA_BLOCK), BF16),
                        pltpu.VMEM((2, nblk // FAR_GROUP, LANES, FAR_GROUP * MOBA_BLOCK), BF16),
                        pltpu.VMEM((2, LANES, MOBA_BLOCK), F32),
                        pltpu.VMEM((2, 1, MOBA_BLOCK), F32),
                        pltpu.VMEM((2, 2, nblk, nblk, MOBA_BLOCK), F32),
                        pltpu.VMEM((2, 2, FAR_GROUP * MOBA_BLOCK, MOBA_BLOCK), F32),
                        pltpu.VMEM((2, 2, FAR_GROUP, 1, MOBA_BLOCK), F32)],
        compiler_params=_params(3),
        name="moba",
    )(far, proj3, proj3, proj3, near, ow)


def _split3(v):
    hi = v.astype(BF16)
    r1 = v - hi.astype(F32)
    mid = r1.astype(BF16)
    lo = (r1 - mid.astype(F32)).astype(BF16)
    return hi, mid, lo


GLA_UNROLL = 8


def _gla_kernel(q_ref, k_ref, v_ref, g_ref, ga_ref, wal_ref, bal_ref, gw_ref, o_ref, b_ref, st_ref):
    seq = q_ref.shape[0]
    c = GLA_CHUNK
    pc = 256

    rr = lax.broadcasted_iota(jnp.int32, (pc, pc), 0)
    cc = lax.broadcasted_iota(jnp.int32, (pc, pc), 1)
    tri = jnp.where((rr >= cc) & (rr // c == cc // c), 1.0, 0.0).astype(BF16)

    def decay_body(j, carry):
        rows = [pl.ds(pl.multiple_of((j * GLA_UNROLL + u) * pc, pc), pc) for u in range(GLA_UNROLL)]
        xg = [jnp.dot(ga_ref[r, :], wal_ref[...], preferred_element_type=F32) + bal_ref[...] for r in rows]
        parts = [_split3((jnp.minimum(x, 0.0) - jnp.log(1.0 + jnp.exp(-jnp.abs(x)))) * (1.0 / GLA_GATE_TAU))
                 for x in xg]
        sums = [[jnp.dot(tri, term, preferred_element_type=F32) for term in p] for p in parts]
        for r, (hi, mid, lo) in zip(rows, sums):
            b_ref[r, :] = hi + mid + lo
        return carry

    lax.fori_loop(0, seq // (pc * GLA_UNROLL), decay_body, 0)

    st_ref[...] = jnp.zeros_like(st_ref)
    lane = lax.broadcasted_iota(jnp.int32, (c, LANES), 1)
    head_mask = (lane < GLA_DK, lane >= GLA_DK)
    causal = lax.broadcasted_iota(jnp.int32, (c, c), 0) >= lax.broadcasted_iota(jnp.int32, (c, c), 1)

    units = [(u, h) for u in range(GLA_UNROLL) for h in range(2)]

    def chunk_body(ci, carry):
        rows = [pl.ds(pl.multiple_of((ci * GLA_UNROLL + u) * c, c), c) for u in range(GLA_UNROLL)]
        qt, kt, qs, ke, e_last = [], [], [], [], []
        for u in range(GLA_UNROLL):
            b = b_ref[rows[u], :]
            ref_row = b[c // 2 - 1:c // 2, :]
            last = b[c - 1:c, :]
            q = q_ref[rows[u], :].astype(F32) * (GLA_DK ** -0.5)
            k = k_ref[rows[u], :].astype(F32)
            qt.append(q * jnp.exp(b - ref_row))
            kt.append((k * jnp.exp(ref_row - b)).astype(BF16))
            qs.append(q * jnp.exp(b))
            ke.append((k * jnp.exp(last - b)).astype(BF16))
            e_last.append(jnp.exp(last))
        vs = {(u, h): v_ref[rows[u], h * GLA_DV:(h + 1) * GLA_DV] for u, h in units}
        a = {(u, h): lax.dot_general(jnp.where(head_mask[h], qt[u], 0.0).astype(BF16), kt[u], NT_DIMS,
                                     preferred_element_type=F32) for u, h in units}
        inc = {(u, h): lax.dot_general(vs[u, h], ke[u], TN_DIMS, preferred_element_type=F32) for u, h in units}
        o = {(u, h): jnp.dot(jnp.where(causal, a[u, h], 0.0).astype(BF16), vs[u, h], preferred_element_type=F32)
             for u, h in units}
        states = {}
        for h in range(2):
            st = st_ref[h]
            for u in range(GLA_UNROLL):
                states[u, h] = st
                st = st * e_last[u] + inc[u, h]
            st_ref[h] = st
        for u, h in units:
            cols = slice(h * GLA_DV, (h + 1) * GLA_DV)
            ou = o[u, h] + lax.dot_general(jnp.where(head_mask[h], qs[u], 0.0).astype(BF16),
                                           states[u, h].astype(BF16), NT_DIMS, preferred_element_type=F32)
            ms = jnp.mean(ou * ou, axis=-1, keepdims=True)
            on = ou * lax.rsqrt(ms + NORM_EPS) * gw_ref[...]
            g = g_ref[rows[u], cols].astype(F32)
            o_ref[rows[u], cols] = (on * _silu(g)).astype(o_ref.dtype)
        return carry

    lax.fori_loop(0, seq // (c * GLA_UNROLL), chunk_body, 0)


def _gla(proj3, ga3, wal, bal, gw):
    b, s, _ = proj3.shape
    npair = GLA_HEADS // 2
    qcol = 3 * MOBA_WIDTH // LANES
    kcol = qcol + GLA_KEY_WIDTH // LANES
    vcol = (3 * MOBA_WIDTH + 2 * GLA_KEY_WIDTH) // (2 * GLA_DV)
    gcol = vcol + npair
    return pl.pallas_call(
        _gla_kernel,
        grid=(b, npair),
        in_specs=[pl.BlockSpec((None, s, LANES), lambda bb, hp: (bb, 0, qcol + hp)),
                  pl.BlockSpec((None, s, LANES), lambda bb, hp: (bb, 0, kcol + hp)),
                  pl.BlockSpec((None, s, 2 * GLA_DV), lambda bb, hp: (bb, 0, vcol + hp)),
                  pl.BlockSpec((None, s, 2 * GLA_DV), lambda bb, hp: (bb, 0, gcol + hp)),
                  pl.BlockSpec((None, s, LANES), lambda bb, hp: (bb, 0, 0)),
                  pl.BlockSpec((LANES, LANES), lambda bb, hp: (0, hp)),
                  pl.BlockSpec((1, LANES), lambda bb, hp: (0, hp)),
                  pl.BlockSpec((1, GLA_DV), lambda bb, hp: (0, 0))],
        out_specs=pl.BlockSpec((None, s, 2 * GLA_DV), lambda bb, hp: (bb, 0, hp)),
        out_shape=jax.ShapeDtypeStruct((b, s, GLA_WIDTH), BF16),
        scratch_shapes=[pltpu.VMEM((s, LANES), F32),
                        pltpu.VMEM((2, GLA_DV, LANES), F32)],
        compiler_params=_params(2),
        name="gla",
    )(proj3, proj3, proj3, proj3, ga3, wal, bal, gw)


HALF = D_MODEL // 2


def _pack_rows(v):
    return pltpu.pack_elementwise([v[:, :HALF], v[:, HALF:]], packed_dtype=BF16)


def _unpack_rows(w):
    return (pltpu.unpack_elementwise(w, index=0, packed_dtype=BF16, unpacked_dtype=F32),
            pltpu.unpack_elementwise(w, index=1, packed_dtype=BF16, unpacked_dtype=F32))


def _outproj_kernel(oa_ref, ob_ref, x_ref, g1_ref, sc_ref, sh_ref, g2_ref, nw_ref, wo_ref,
                    ws1_ref, ws3_ref, ws2_ref, wrt_ref, base_ref, h_ref, st_ref):
    mix = (jnp.dot(oa_ref[...], wo_ref[:MOBA_WIDTH, :], preferred_element_type=F32)
           + jnp.dot(ob_ref[...], wo_ref[MOBA_WIDTH:, :], preferred_element_type=F32))
    x1 = x_ref[...] + g1_ref[...] * mix
    ms = jnp.mean(x1 * x1, axis=-1, keepdims=True)
    h = x1 * lax.rsqrt(ms + NORM_EPS) * nw_ref[...]
    h = h * (1.0 + sc_ref[...]) + sh_ref[...]
    h_ref[...] = _pack_rows(h)
    hb = h.astype(BF16)
    a = jnp.dot(hb, ws1_ref[...], preferred_element_type=F32)
    u = jnp.dot(hb, ws3_ref[...], preferred_element_type=F32)
    shared = jnp.dot((_silu(a) * u).astype(BF16), ws2_ref[...], preferred_element_type=F32)
    base_ref[...] = x1 + g2_ref[...] * shared
    logits_t = lax.dot_general(wrt_ref[...], hb, NT_DIMS, preferred_element_type=F32)
    st_ref[...] = jax.nn.sigmoid(logits_t)


def _outproj(oa, ob, x2, g1, sc, sh, g2, nw, wo, ws1, ws3, ws2, wrt, seq):
    t = x2.shape[0]
    tpb = seq // ROW_TILE
    vec = lambda: pl.BlockSpec((None, 1, D_MODEL), lambda i: (i // tpb, 0, 0))
    full = lambda a: pl.BlockSpec(a.shape, lambda i: (0,) * a.ndim)
    rows = lambda w: pl.BlockSpec((ROW_TILE, w), lambda i: (i, 0))
    return pl.pallas_call(
        _outproj_kernel,
        grid=(t // ROW_TILE,),
        in_specs=[rows(MOBA_WIDTH), rows(GLA_WIDTH), rows(D_MODEL), vec(), vec(), vec(), vec(),
                  full(nw), full(wo), full(ws1), full(ws3), full(ws2), full(wrt)],
        out_specs=[rows(D_MODEL), rows(HALF), pl.BlockSpec((N_EXPERTS, ROW_TILE), lambda i: (0, i))],
        out_shape=[jax.ShapeDtypeStruct((t, D_MODEL), F32),
                   jax.ShapeDtypeStruct((t, HALF), jnp.uint32),
                   jax.ShapeDtypeStruct((N_EXPERTS, t), F32)],
        compiler_params=_params(1),
        name="outproj",
    )(oa, ob, x2, g1, sc, sh, g2, nw, wo, ws1, ws3, ws2, wrt)


SLOT_CODE_SHIFT = 16
SLOT_CODE_BASE = 1 << SLOT_CODE_SHIFT


def _route_kernel(s_ref, eb_ref, code_ref, w_ref, cnt_ref, carry_ref):
    i = pl.program_id(0)
    ne, nt = s_ref.shape

    @pl.when(i == 0)
    def _init():
        carry_ref[...] = jnp.zeros_like(carry_ref)

    s = s_ref[...]
    choice = s + eb_ref[...]
    gio = lax.broadcasted_iota(jnp.int32, (GROUP_SIZE, nt), 0)
    gscore = []
    for g in range(N_GROUPS):
        cg = choice[g * GROUP_SIZE:(g + 1) * GROUP_SIZE, :]
        top1 = jnp.max(cg, axis=0, keepdims=True)
        first = jnp.min(jnp.where(cg == top1, gio, GROUP_SIZE), axis=0, keepdims=True)
        top2 = jnp.max(jnp.where(gio == first, -jnp.inf, cg), axis=0, keepdims=True)
        gscore.append(top1 + top2)
    gs = jnp.concatenate(gscore, axis=0)
    gidx = lax.broadcasted_iota(jnp.int32, gs.shape, 0)
    beaten = jnp.zeros(gs.shape, jnp.int32)
    for m in range(N_GROUPS):
        gm = gs[m:m + 1, :]
        beaten = beaten + jnp.where((gm > gs) | ((gm == gs) & (gidx > m)), 1, 0)
    gkeep = beaten < TOPK_GROUPS
    masked = jnp.concatenate(
        [jnp.where(gkeep[g:g + 1, :], choice[g * GROUP_SIZE:(g + 1) * GROUP_SIZE, :], -jnp.inf)
         for g in range(N_GROUPS)], axis=0)

    eio = lax.broadcasted_iota(jnp.int32, (ne, nt), 0)
    picked = jnp.zeros((ne, nt), F32)
    idx_rows, w_rows, hits = [], [], []
    for _ in range(TOP_K):
        mx = jnp.max(masked, axis=0, keepdims=True)
        idx = jnp.min(jnp.where(masked == mx, eio, ne), axis=0, keepdims=True)
        hit = eio == idx
        w_rows.append(jnp.sum(jnp.where(hit, s, 0.0), axis=0, keepdims=True))
        idx_rows.append(idx)
        hits.append(hit)
        masked = jnp.where(hit, -jnp.inf, masked)
        picked = jnp.where(hit, 1.0, picked)
    wk = jnp.concatenate(w_rows, axis=0)
    w_ref[...] = wk / jnp.sum(wk, axis=0, keepdims=True) * ROUTED_SCALE

    tr = lax.broadcasted_iota(jnp.int32, (nt, nt), 0)
    tc = lax.broadcasted_iota(jnp.int32, (nt, nt), 1)
    before = jnp.where(tr < tc, 1.0, 0.0).astype(BF16)
    pb = picked.astype(BF16)
    pos = carry_ref[...] + jnp.dot(pb, before, preferred_element_type=F32)
    rank = jnp.concatenate(
        [jnp.sum(jnp.where(hit, pos, 0.0), axis=0, keepdims=True) for hit in hits], axis=0).astype(jnp.int32)
    code_ref[...] = jnp.concatenate(idx_rows, axis=0) * SLOT_CODE_BASE + rank
    total = carry_ref[...] + jnp.dot(pb, jnp.ones((nt, nt), BF16), preferred_element_type=F32)
    carry_ref[...] = total
    cnt_ref[...] = total


def _route(scores_t, eb):
    ne, t = scores_t.shape
    assert t <= SLOT_CODE_BASE
    nt = ROUTE_TILE
    tok = lambda dt: jax.ShapeDtypeStruct((TOP_K, t), dt)
    return pl.pallas_call(
        _route_kernel,
        grid=(t // nt,),
        in_specs=[pl.BlockSpec((ne, nt), lambda i: (0, i)),
                  pl.BlockSpec((ne, nt), lambda i: (0, 0))],
        out_specs=[pl.BlockSpec((TOP_K, nt), lambda i: (0, i)),
                   pl.BlockSpec((TOP_K, nt), lambda i: (0, i)),
                   pl.BlockSpec((ne, nt), lambda i: (0, 0))],
        out_shape=[tok(jnp.int32), tok(F32), jax.ShapeDtypeStruct((ne, nt), F32)],
        scratch_shapes=[pltpu.VMEM((ne, nt), F32)],
        compiler_params=_params(1),
        name="route",
    )(scores_t, eb)


SLOT_TILE = 2048


def _slots_kernel(pstart_ref, code_ref, o_ref):
    code = code_ref[...]
    expert = lax.shift_right_logical(code, SLOT_CODE_SHIFT)

    def body(e, acc):
        return jnp.where(expert == e, pstart_ref[e], acc)

    start = lax.fori_loop(0, N_EXPERTS, body, jnp.zeros_like(code), unroll=8)
    o_ref[...] = start + (code & (SLOT_CODE_BASE - 1))


def _slots(pstart, code_t):
    k, t = code_t.shape
    return pl.pallas_call(
        _slots_kernel,
        grid_spec=pltpu.PrefetchScalarGridSpec(
            num_scalar_prefetch=1,
            grid=(t // SLOT_TILE,),
            in_specs=[pl.BlockSpec((k, SLOT_TILE), lambda i, p: (0, i))],
            out_specs=pl.BlockSpec((k, SLOT_TILE), lambda i, p: (0, i)),
        ),
        out_shape=jax.ShapeDtypeStruct((k, t), jnp.int32),
        compiler_params=_params(1),
        name="slots",
    )(pstart, code_t)


SC_WINDOW = 128


def _sc_gather_rows(table, idx_flat):
    from jax.experimental.pallas import tpu_sc as plsc
    info = plsc.get_sparse_core_info()
    nw = info.num_cores * info.num_subcores
    n = idx_flat.shape[0]
    width = table.shape[1]
    per_worker = n // nw
    assert per_worker * nw == n and per_worker % SC_WINDOW == 0
    mesh = plsc.VectorSubcoreMesh(core_axis_name="c", subcore_axis_name="s")

    def body(table_hbm, idx_hbm, out_hbm, idx_v, rows_v, sem):
        wid = lax.axis_index("s") * info.num_cores + lax.axis_index("c")
        base = wid * per_worker

        @pl.loop(0, per_worker // SC_WINDOW)
        def _(w):
            off = pl.multiple_of(base + w * SC_WINDOW, SC_WINDOW)
            pltpu.sync_copy(idx_hbm.at[pl.ds(off, SC_WINDOW)], idx_v)
            pltpu.async_copy(table_hbm.at[idx_v], rows_v, sem).wait()
            pltpu.sync_copy(rows_v, out_hbm.at[pl.ds(off, SC_WINDOW)])

    return pl.kernel(
        body,
        out_type=jax.ShapeDtypeStruct((n, width), table.dtype),
        mesh=mesh,
        scratch_types=[pltpu.VMEM((SC_WINDOW,), jnp.int32),
                       pltpu.VMEM((SC_WINDOW, width), table.dtype),
                       pltpu.SemaphoreType.DMA],
        name="sc_gather",
    )(table, idx_flat)


def _sc_scatter_rows(rows, idx_kt, n_out):
    from jax.experimental.pallas import tpu_sc as plsc
    info = plsc.get_sparse_core_info()
    nw = info.num_cores * info.num_subcores
    t, width = rows.shape
    nk = idx_kt.shape[0]
    per_worker = t // nw
    assert per_worker * nw == t and per_worker % SC_WINDOW == 0
    mesh = plsc.VectorSubcoreMesh(core_axis_name="c", subcore_axis_name="s")

    def body(rows_hbm, idx_hbm, out_hbm, idx_v, rows_v, sem):
        wid = lax.axis_index("s") * info.num_cores + lax.axis_index("c")
        base = wid * per_worker

        @pl.loop(0, per_worker // SC_WINDOW)
        def _(w):
            off = pl.multiple_of(base + w * SC_WINDOW, SC_WINDOW)
            pltpu.sync_copy(rows_hbm.at[pl.ds(off, SC_WINDOW)], rows_v)
            pltpu.sync_copy(idx_hbm.at[:, pl.ds(off, SC_WINDOW)], idx_v)
            copies = [pltpu.async_copy(rows_v, out_hbm.at[idx_v.at[k]], sem) for k in range(nk)]
            for cp in copies:
                cp.wait()

    return pl.kernel(
        body,
        out_type=jax.ShapeDtypeStruct((n_out, width), rows.dtype),
        mesh=mesh,
        scratch_types=[pltpu.VMEM((nk, SC_WINDOW), jnp.int32),
                       pltpu.VMEM((SC_WINDOW, width), rows.dtype),
                       pltpu.SemaphoreType.DMA],
        name="sc_scatter",
    )(rows, idx_kt)


def _expert_kernel(first_ref, count_ref, used_ref, w1_ref, w3_ref, w2_ref, xs_ref, ys_ref,
                   xbuf, ybuf, sem_in, sem_out, w1f, w3f, w2f, sem_w, w1b, w3b, w2b, *, layer):
    e = pl.program_id(0)
    ne = pl.num_programs(0)
    r = xbuf.shape[1]
    n_used = used_ref[0]

    def x_copy(g, slot):
        return pltpu.make_async_copy(xs_ref.at[pl.ds(pl.multiple_of(g * r, r), r), :], xbuf.at[slot], sem_in.at[slot])

    def y_copy(g, slot):
        return pltpu.make_async_copy(ybuf.at[slot], ys_ref.at[pl.ds(pl.multiple_of(g * r, r), r), :], sem_out.at[slot])

    def w_copies(ex, slot):
        return [pltpu.make_async_copy(src.at[layer, ex], dst.at[slot], sem_w.at[slot])
                for src, dst in ((w1_ref, w1f), (w3_ref, w3f), (w2_ref, w2f))]

    nin = xbuf.shape[0]
    nout = ybuf.shape[0]

    @pl.when(e == 0)
    def _first_reads():
        for g in range(nin):
            @pl.when(g < n_used)
            def _(g=g):
                x_copy(g, g).start()
        for cp in w_copies(0, 0):
            cp.start(priority=1)

    @pl.when(e + 1 < ne)
    def _next_weights():
        for cp in w_copies(e + 1, (e + 1) & 1):
            cp.start(priority=1)

    for cp in w_copies(e, e & 1):
        cp.wait()

    n = count_ref[e]

    @pl.when(n > 0)
    def _cast_weights():
        w1b[...] = w1f[e & 1].astype(BF16)
        w3b[...] = w3f[e & 1].astype(BF16)
        w2b[...] = w2f[e & 1].astype(BF16)

    def run_tiles(g, count):
        tiles = [g + j for j in range(count)]
        for gj in tiles:
            x_copy(gj, gj % nin).wait()
        x = jnp.concatenate([xbuf[gj % nin] for gj in tiles], axis=0)
        for gj in tiles:
            @pl.when(gj + nin < n_used)
            def _(gj=gj):
                x_copy(gj + nin, gj % nin).start()
        lo, hi = _unpack_rows(x)
        lo, hi = lo.astype(BF16), hi.astype(BF16)
        a = (jnp.dot(lo, w1b[:HALF, :], preferred_element_type=F32)
             + jnp.dot(hi, w1b[HALF:, :], preferred_element_type=F32))
        u = (jnp.dot(lo, w3b[:HALF, :], preferred_element_type=F32)
             + jnp.dot(hi, w3b[HALF:, :], preferred_element_type=F32))
        y = _pack_rows(jnp.dot((_silu(a) * u).astype(BF16), w2b[...], preferred_element_type=F32))
        for j, gj in enumerate(tiles):
            @pl.when(gj >= nout)
            def _(gj=gj):
                y_copy(gj - nout, gj % nout).wait()
            ybuf[gj % nout] = y[j * r:(j + 1) * r]
            y_copy(gj, gj % nout).start()

    g0 = first_ref[e]
    pair = EXPERT_TILES_PER_MATMUL

    def pair_body(p, carry):
        run_tiles(g0 + p * pair, pair)
        return carry

    lax.fori_loop(0, n // pair, pair_body, 0)
    for left in range(1, pair):
        pl.when(n % pair == left)(functools.partial(run_tiles, g0 + n - left, left))

    @pl.when(e == ne - 1)
    def _drain_writes():
        for back in range(nout, 0, -1):
            @pl.when(n_used >= back)
            def _(back=back):
                y_copy(n_used - back, (n_used - back) % nout).wait()


def _experts(tile_first, tile_count, n_used, xs, w1, w3, w2, layer):
    n_rows = xs.shape[0]
    r = EXPERT_TILE
    any_spec = pl.BlockSpec(memory_space=pl.ANY)
    return pl.pallas_call(
        functools.partial(_expert_kernel, layer=layer),
        grid_spec=pltpu.PrefetchScalarGridSpec(
            num_scalar_prefetch=3,
            grid=(N_EXPERTS,),
            in_specs=[any_spec, any_spec, any_spec, any_spec],
            out_specs=any_spec,
            scratch_shapes=[pltpu.VMEM((EXPERT_IN_RING, r, HALF), jnp.uint32),
                            pltpu.VMEM((EXPERT_OUT_RING, r, HALF), jnp.uint32),
                            pltpu.SemaphoreType.DMA((EXPERT_IN_RING,)),
                            pltpu.SemaphoreType.DMA((EXPERT_OUT_RING,)),
                            pltpu.VMEM((2, D_MODEL, EXPERT_FF), F32),
                            pltpu.VMEM((2, D_MODEL, EXPERT_FF), F32),
                            pltpu.VMEM((2, EXPERT_FF, D_MODEL), F32),
                            pltpu.SemaphoreType.DMA((2,)),
                            pltpu.VMEM((D_MODEL, EXPERT_FF), BF16),
                            pltpu.VMEM((D_MODEL, EXPERT_FF), BF16),
                            pltpu.VMEM((EXPERT_FF, D_MODEL), BF16)],
        ),
        out_shape=jax.ShapeDtypeStruct((n_rows, HALF), jnp.uint32),
        compiler_params=_params(1),
        name="experts",
    )(tile_first, tile_count, n_used, w1, w3, w2, xs)


def _combine_dense_kernel(base_ref, g2_ref, w_ref, yg_ref, o_ref):
    acc_lo = acc_hi = None
    for k in range(TOP_K):
        lo, hi = _unpack_rows(yg_ref[k])
        wk = w_ref[:, k:k + 1]
        acc_lo = wk * lo if acc_lo is None else acc_lo + wk * lo
        acc_hi = wk * hi if acc_hi is None else acc_hi + wk * hi
    o_ref[:, :HALF] = base_ref[:, :HALF] + g2_ref[:, :HALF] * acc_lo
    o_ref[:, HALF:] = base_ref[:, HALF:] + g2_ref[:, HALF:] * acc_hi


def _combine_dense(base, g2, w_tok, yg, batch):
    t = base.shape[0]
    seq = yg.shape[1]
    nt = ROUTE_TILE
    tpb = seq // nt
    rows = lambda i: (batch * tpb + i, 0)
    return pl.pallas_call(
        _combine_dense_kernel,
        grid=(tpb,),
        in_specs=[pl.BlockSpec((nt, D_MODEL), rows),
                  pl.BlockSpec((None, 1, D_MODEL), lambda i: (batch, 0, 0)),
                  pl.BlockSpec((nt, TOP_K), rows),
                  pl.BlockSpec((TOP_K, nt, HALF), lambda i: (0, i, 0))],
        out_specs=pl.BlockSpec((nt, D_MODEL), rows),
        out_shape=jax.ShapeDtypeStruct((t, D_MODEL), F32),
        input_output_aliases={0: 0},
        compiler_params=_params(1),
        name="combine_dense",
    )(base, g2, w_tok, yg)


def _layer(layer, x, c, w_ada, b_ada, norm1_w, norm2_w, w_in, q_norm_w, k_norm_w, rel_bias, w_alpha, b_alpha,
           moba_out_w, gla_out_w, w_out, w_router, e_bias, w1, w3, w2, ws1, ws3, ws2):
    b, s, d = x.shape
    t = b * s
    x2 = x.reshape(t, d)

    mod = _mod(c, w_ada, b_ada)
    sh1, sc1, g1, sh2, sc2, g2 = [mod[:, j * d:(j + 1) * d].reshape(b, 1, d) for j in range(6)]

    w_main = w_in[:, :D_MAIN].astype(BF16)
    w_ga = jnp.zeros((d, LANES), BF16).at[:, :GLA_GATE_RANK].set(w_in[:, D_MAIN:].astype(BF16))
    per_chunk = 256 // MOBA_HEAD_DIM
    qw = jnp.tile(q_norm_w.astype(F32), per_chunk).reshape(1, 256) * (MOBA_HEAD_DIM ** -0.5)
    kw = jnp.tile(k_norm_w.astype(F32), per_chunk).reshape(1, 256)
    proj, ga = _inproj(x2, sc1, sh1, norm1_w.reshape(1, d), w_main, w_ga, qw, kw, s)
    proj3 = proj.reshape(b, s, D_MAIN)

    near, far = _moba_bias_tables(rel_bias)
    ow = jnp.tile(moba_out_w.astype(F32), 2).reshape(1, LANES)
    o_a = _moba(proj3, near, far, ow)

    wal = jnp.zeros((LANES, GLA_KEY_WIDTH), F32).at[:GLA_GATE_RANK].set(w_alpha)
    o_b = _gla(proj3, ga.reshape(b, s, LANES), wal, b_alpha.reshape(1, GLA_KEY_WIDTH),
               gla_out_w.reshape(1, GLA_DV))

    base, h2, scores_t = _outproj(
        o_a.reshape(t, MOBA_WIDTH), o_b.reshape(t, GLA_WIDTH), x2, g1, sc2, sh2, g2,
        norm2_w.reshape(1, d), w_out.astype(BF16), ws1.astype(BF16), ws3.astype(BF16), ws2.astype(BF16),
        w_router.T.astype(BF16), s)

    eb = jnp.broadcast_to(e_bias.astype(F32)[:, None], (N_EXPERTS, ROUTE_TILE))
    code_t, w_t, counts = _route(scores_t, eb)

    r = EXPERT_TILE
    n_tiles = (t * TOP_K + N_EXPERTS * (r - 1) + r - 1) // r
    n_rows = n_tiles * r
    cnt = counts[:, 0].astype(jnp.int32)
    padded = (cnt + r - 1) // r * r
    pend = jnp.cumsum(padded)
    pstart = pend - padded
    n_used = (pend[-1:] // r).astype(jnp.int32)
    dest_t = _slots(pstart, code_t)

    xs = _sc_scatter_rows(h2, dest_t, n_rows)
    ys = _experts(pstart // r, padded // r, n_used, xs, w1, w3, w2, layer)
    w_tok = w_t.T
    out = base
    for bi in range(b):
        idx = dest_t[:, bi * s:(bi + 1) * s].reshape(TOP_K * s)
        yg = _sc_gather_rows(ys, idx).reshape(TOP_K, s, HALF)
        out = _combine_dense(out, g2, w_tok, yg, bi)
    return out.reshape(b, s, d)


def kernel(x, c, w_ada, b_ada, norm1_w, norm2_w, w_in, q_norm_w, k_norm_w, rel_bias, w_alpha, b_alpha,
           moba_out_w, gla_out_w, w_out, w_router, e_bias, w1, w3, w2, ws1, ws3, ws2):
    for l in range(w_ada.shape[0]):
        x = _layer(l, x, c, w_ada[l], b_ada[l], norm1_w[l], norm2_w[l], w_in[l], q_norm_w[l], k_norm_w[l],
                   rel_bias, w_alpha[l], b_alpha[l], moba_out_w[l], gla_out_w[l], w_out[l], w_router[l],
                   e_bias[l], w1, w3, w2, ws1[l], ws3[l], ws2[l])
    return x
```

```python
import functools
import math

import numpy as np
import jax
import jax.numpy as jnp
from jax import lax
from jax.experimental import pallas as pl
from jax.experimental.pallas import tpu as pltpu

D_MODEL = 1024
MOBA_HEADS = 8
MOBA_HEAD_DIM = 64
MOBA_WIDTH = MOBA_HEADS * MOBA_HEAD_DIM
MOBA_BLOCK = 256
MOBA_TOPK = 3
GLA_HEADS = 4
GLA_DK = 64
GLA_DV = 128
GLA_KEY_WIDTH = GLA_HEADS * GLA_DK
GLA_WIDTH = GLA_HEADS * GLA_DV
GLA_GATE_RANK = 16
GLA_GATE_TAU = 16.0
GLA_CHUNK = 64
REL_BUCKETS = 32
REL_MAX_DIST = 128
N_EXPERTS = 256
TOP_K = 8
N_GROUPS = 8
TOPK_GROUPS = 4
GROUP_SIZE = N_EXPERTS // N_GROUPS
EXPERT_FF = 256
SHARED_FF = 256
ROUTED_SCALE = 2.5
NORM_EPS = 1e-6

D_MAIN = 3 * MOBA_WIDTH + 2 * GLA_KEY_WIDTH + 2 * GLA_WIDTH
LANES = 128
VMEM_LIMIT = 56 * 1024 * 1024

ROW_TILE = 512
ROUTE_TILE = 256
EXPERT_TILE = 256
EXPERT_TILES_PER_MATMUL = 2
EXPERT_IN_RING = 6
EXPERT_OUT_RING = 4

F32 = jnp.float32
BF16 = jnp.bfloat16
NT_DIMS = (((1,), (1,)), ((), ()))
TN_DIMS = (((0,), (0,)), ((), ()))


def _params(n_axes):
    return pltpu.CompilerParams(dimension_semantics=("arbitrary",) * n_axes,
                                vmem_limit_bytes=VMEM_LIMIT)


def _silu(v):
    return v * jax.nn.sigmoid(v)


def _mod_kernel(c_ref, w_ref, b_ref, o_ref):
    o_ref[...] = jnp.dot(_silu(c_ref[...]), w_ref[...], preferred_element_type=F32) + b_ref[...]


def _mod(c, w, b):
    rows = 8
    cp = jnp.zeros((rows, D_MODEL), F32).at[:c.shape[0]].set(c)
    n = w.shape[1]
    tn = 1024
    out = pl.pallas_call(
        _mod_kernel,
        grid=(n // tn,),
        in_specs=[pl.BlockSpec((rows, D_MODEL), lambda j: (0, 0)),
                  pl.BlockSpec((D_MODEL, tn), lambda j: (0, j)),
                  pl.BlockSpec((1, tn), lambda j: (0, j))],
        out_specs=pl.BlockSpec((rows, tn), lambda j: (0, j)),
        out_shape=jax.ShapeDtypeStruct((rows, n), F32),
        compiler_params=_params(1),
        name="mod",
    )(cp, w, b.reshape(1, n))
    return out[:c.shape[0]]


def _group_rms_inv(a, group):
    lane = lax.broadcasted_iota(jnp.int32, (1, a.shape[1]), 1)
    a2 = a * a
    inv = jnp.zeros_like(a)
    for g in range(a.shape[1] // group):
        m = (lane >= g * group) & (lane < (g + 1) * group)
        ss = jnp.sum(jnp.where(m, a2, 0.0), axis=-1, keepdims=True)
        inv = jnp.where(m, lax.rsqrt(ss * (1.0 / group) + NORM_EPS), inv)
    return inv


def _inproj_kernel(x_ref, sc_ref, sh_ref, nw_ref, w_ref, wga_ref, qw_ref, kw_ref, o_ref, ga_ref):
    x = x_ref[...]
    ms = jnp.mean(x * x, axis=-1, keepdims=True)
    h = x * lax.rsqrt(ms + NORM_EPS) * nw_ref[...]
    h = h * (1.0 + sc_ref[...]) + sh_ref[...]
    hb = h.astype(BF16)
    cw = 256
    for j in range(D_MAIN // cw):
        acc = jnp.dot(hb, w_ref[:, j * cw:(j + 1) * cw], preferred_element_type=F32)
        if j < 2 * MOBA_WIDTH // cw:
            nw = qw_ref if j < MOBA_WIDTH // cw else kw_ref
            acc = acc * _group_rms_inv(acc, MOBA_HEAD_DIM) * nw[...]
        o_ref[:, j * cw:(j + 1) * cw] = acc.astype(BF16)
    ga_ref[...] = jnp.dot(hb, wga_ref[...], preferred_element_type=F32)


def _inproj(x2, sc, sh, nw, w_main, w_ga, qw, kw, seq):
    t = x2.shape[0]
    tpb = seq // ROW_TILE
    vec = lambda: pl.BlockSpec((None, 1, D_MODEL), lambda i: (i // tpb, 0, 0))
    full = lambda a: pl.BlockSpec(a.shape, lambda i: (0,) * a.ndim)
    return pl.pallas_call(
        _inproj_kernel,
        grid=(t // ROW_TILE,),
        in_specs=[pl.BlockSpec((ROW_TILE, D_MODEL), lambda i: (i, 0)), vec(), vec(),
                  full(nw), full(w_main), full(w_ga), full(qw), full(kw)],
        out_specs=[pl.BlockSpec((ROW_TILE, D_MAIN), lambda i: (i, 0)),
                   pl.BlockSpec((ROW_TILE, LANES), lambda i: (i, 0))],
        out_shape=[jax.ShapeDtypeStruct((t, D_MAIN), BF16),
                   jax.ShapeDtypeStruct((t, LANES), F32)],
        compiler_params=_params(1),
        name="inproj",
    )(x2, sc, sh, nw, w_main, w_ga, qw, kw)


def _t5_bucket_np(rel):
    max_exact = REL_BUCKETS // 2
    relf = np.maximum(rel, 1).astype(np.float64)
    large = max_exact + (np.log(relf / max_exact) / math.log(REL_MAX_DIST / max_exact)
                         * (REL_BUCKETS - max_exact)).astype(np.int32)
    large = np.minimum(large, REL_BUCKETS - 1)
    return np.where(rel < max_exact, rel, large)


def _bias_kernel(rb_ref, idx_ref, o_ref):
    h = pl.program_id(0)
    idx = idx_ref[...]
    tab = jnp.full(idx.shape, -jnp.inf, F32)
    for bk in range(REL_BUCKETS):
        tab = jnp.where(idx == bk, rb_ref[bk * MOBA_HEADS + h], tab)
    o_ref[...] = tab


def _moba_bias_tables(rel_bias):
    j = np.arange(MOBA_BLOCK)[:, None]
    i = np.arange(MOBA_BLOCK)[None, :]
    own_idx = np.where(j <= i, _t5_bucket_np(np.maximum(i - j, 0)), -1)
    prev_idx = _t5_bucket_np(MOBA_BLOCK + i - j)
    idx = jnp.asarray(np.concatenate([prev_idx, own_idx], axis=0).astype(np.int32))
    assert int(_t5_bucket_np(np.array([MOBA_BLOCK + 1]))[0]) == REL_BUCKETS - 1
    rb = rel_bias.astype(F32)
    near = pl.pallas_call(
        _bias_kernel,
        grid=(MOBA_HEADS,),
        in_specs=[pl.BlockSpec(memory_space=pltpu.SMEM),
                  pl.BlockSpec(idx.shape, lambda h: (0, 0))],
        out_specs=pl.BlockSpec((None,) + idx.shape, lambda h: (h, 0, 0)),
        out_shape=jax.ShapeDtypeStruct((MOBA_HEADS,) + idx.shape, F32),
        compiler_params=_params(1),
        name="bias",
    )(rb.reshape(-1), idx)
    return near, rb[REL_BUCKETS - 1]


FAR_GROUP = 4


def _moba_kernel(*refs):
    hp = pl.program_id(1)
    _moba_body(None, hp, *refs, prepare=True)

    def query_block(i, carry):
        _moba_body(i, hp, *refs, prepare=False)
        return carry

    lax.fori_loop(0, refs[2].shape[0] // MOBA_BLOCK, query_block, 0)


def _moba_body(i, hp, far_ref, q_ref, k_ref, v_ref, near_ref, ow_ref, o_ref,
               vt_ref, vtg_ref, acc_ref, m_ref, sel_ref, s_ref, mx_ref, *, prepare):
    nblk = k_ref.shape[0] // MOBA_BLOCK
    ngrp = nblk // FAR_GROUP
    hd = MOBA_HEAD_DIM
    bs = MOBA_BLOCK
    lane = lax.broadcasted_iota(jnp.int32, (bs, LANES), 1)

    def split_heads(qb):
        zero = jnp.zeros_like(qb)
        return jnp.where(lane < hd, qb, zero), jnp.where(lane < hd, zero, qb)

    def _prepare():
        row = lax.broadcasted_iota(jnp.int32, (LANES, bs), 0)
        kmeans = []
        for n in range(nblk):
            kb = k_ref[n * bs:(n + 1) * bs, :].astype(F32)
            kmeans.append(jnp.mean(kb, axis=0, keepdims=True))
            vt = v_ref[n * bs:(n + 1) * bs, :].astype(F32).T
            vt0 = jnp.where(row < hd, vt, 1.0).astype(BF16)
            vt1 = jnp.where(row < hd, 1.0, vt).astype(BF16)
            vt_ref[0, n] = vt0
            vt_ref[1, n] = vt1
            gcols = slice((n % FAR_GROUP) * bs, (n % FAR_GROUP + 1) * bs)
            vtg_ref[0, n // FAR_GROUP, :, gcols] = vt0
            vtg_ref[1, n // FAR_GROUP, :, gcols] = vt1
        kmean = jnp.concatenate(kmeans, axis=0)
        km_hi = kmean.astype(BF16)
        km_lo = (kmean - km_hi.astype(F32)).astype(BF16)
        blk = lax.broadcasted_iota(jnp.int32, (nblk, bs), 0)
        for ib in range(nblk):
            qparts = split_heads(q_ref[ib * bs:(ib + 1) * bs, :])
            for h in range(2):
                gt = (lax.dot_general(km_hi, qparts[h], NT_DIMS, preferred_element_type=F32)
                      + lax.dot_general(km_lo, qparts[h], NT_DIMS, preferred_element_type=F32))
                gt = jnp.where(blk < ib, gt, -jnp.inf)
                cnt = jnp.zeros(gt.shape, jnp.int32)
                for m in range(ib):
                    gm = gt[m:m + 1, :]
                    cnt = cnt + jnp.where((gm > gt) | ((gm == gt) & (blk > m)), 1, 0)
                keep = (blk < ib) & (cnt < MOBA_TOPK)
                sel_ref[0, h, ib] = jnp.where(keep, 1.0, 0.0)
                sel_ref[1, h, ib] = jnp.where(keep & (blk < ib - 1), 1.0, 0.0)

    if prepare:
        _prepare()
        return

    q_rows = pl.ds(pl.multiple_of(i * bs, bs), bs)
    qh = split_heads(q_ref[q_rows, :])

    def finish():
        a0 = acc_ref[0]
        a1 = acc_ref[1]
        row = lax.broadcasted_iota(jnp.int32, a0.shape, 0)
        ot = jnp.where(row < hd, a0 / a0[hd:hd + 1, :], a1 / a1[0:1, :])
        o2 = ot * ot
        ss0 = jnp.sum(jnp.where(row < hd, o2, 0.0), axis=0, keepdims=True)
        ss1 = jnp.sum(jnp.where(row < hd, 0.0, o2), axis=0, keepdims=True)
        inv = jnp.where(row < hd, lax.rsqrt(ss0 * (1.0 / hd) + NORM_EPS), lax.rsqrt(ss1 * (1.0 / hd) + NORM_EPS))
        o_ref[q_rows, :] = ((ot * inv).T * ow_ref[...]).astype(o_ref.dtype)

    @pl.when(i == 0)
    def _own_block_only():
        kb = k_ref[0:bs, :]
        for h in range(2):
            s = lax.dot_general(kb, qh[h], NT_DIMS, preferred_element_type=F32) + near_ref[h, bs:2 * bs, :]
            m_new = jnp.max(s, axis=0, keepdims=True)
            p = jnp.exp(s - m_new).astype(BF16)
            acc_ref[h] = jnp.dot(vt_ref[h, 0], p, preferred_element_type=F32)
        finish()

    n_far = (i + FAR_GROUP - 2) // FAR_GROUP
    gk = FAR_GROUP * bs

    def far_scores(g):
        kb = k_ref[g * gk:(g + 1) * gk, :]
        for h in range(2):
            s = lax.dot_general(kb, qh[h], NT_DIMS, preferred_element_type=F32)
            s_ref[g % 2, h] = s
            for j in range(FAR_GROUP):
                mx_ref[g % 2, h, j] = jnp.max(s[j * bs:(j + 1) * bs], axis=0, keepdims=True)

    def near_blocks(with_far):
        kbs = (k_ref[pl.ds(pl.multiple_of((i - 1) * bs, bs), bs), :],
               k_ref[pl.ds(pl.multiple_of(i * bs, bs), bs), :])
        ss = [[lax.dot_general(kbs[w], qh[h], NT_DIMS, preferred_element_type=F32)
               + near_ref[h, w * bs:(w + 1) * bs, :] for w in range(2)] for h in range(2)]
        if with_far:
            far_scores(0)
        ps, ms = [], []
        for h in range(2):
            s_prev, s_own = ss[h]
            keep = sel_ref[0, h, i, pl.ds(i - 1, 1), :] > 0.5
            mx = jnp.where(keep, jnp.max(s_prev, axis=0, keepdims=True), -jnp.inf)
            m_new = jnp.maximum(jnp.max(s_own, axis=0, keepdims=True), mx)
            ps.append((jnp.exp(s_prev - jnp.where(keep, m_new, jnp.inf)).astype(BF16),
                       jnp.exp(s_own - m_new).astype(BF16)))
            ms.append(m_new)
        for h in range(2):
            acc_ref[h] = (jnp.dot(vt_ref[h, i - 1], ps[h][0], preferred_element_type=F32)
                          + jnp.dot(vt_ref[h, i], ps[h][1], preferred_element_type=F32))
            m_ref[h] = ms[h]

    def far_group(g, with_next):
        if with_next:
            far_scores(g + 1)
        for h in range(2):
            fb = far_ref[2 * hp + h]
            m_old = m_ref[h]
            m_new = m_old
            keeps = []
            for j in range(FAR_GROUP):
                keep = sel_ref[1, h, i, pl.ds(g * FAR_GROUP + j, 1), :] > 0.5
                m_new = jnp.maximum(m_new, jnp.where(keep, mx_ref[g % 2, h, j] + fb, -jnp.inf))
                keeps.append(keep)
            p = jnp.concatenate(
                [jnp.exp(s_ref[g % 2, h, j * bs:(j + 1) * bs, :]
                         - jnp.where(keeps[j], m_new - fb, jnp.inf)).astype(BF16)
                 for j in range(FAR_GROUP)], axis=0)
            pv = jnp.dot(vtg_ref[h, g], p, preferred_element_type=F32)
            acc_ref[h] = acc_ref[h] * jnp.exp(m_old - m_new) + pv
            m_ref[h] = m_new

    def step_body(nf):
        near_blocks(nf > 0)
        for g in range(nf):
            far_group(g, g + 1 < nf)
        finish()

    for nf in range(ngrp + 1):
        pl.when((i >= 1) & (n_far == nf))(functools.partial(step_body, nf))


def _moba(proj3, near, far, ow):
    b, s, _ = proj3.shape
    nblk = s // MOBA_BLOCK
    assert nblk % FAR_GROUP == 0
    npair = MOBA_HEADS // 2
    kcol = MOBA_WIDTH // LANES
    return pl.pallas_call(
        _moba_kernel,
        grid=(b, npair),
        in_specs=[pl.BlockSpec(memory_space=pltpu.SMEM),
                  pl.BlockSpec((None, s, LANES), lambda bb, hp: (bb, 0, hp)),
                  pl.BlockSpec((None, s, LANES), lambda bb, hp: (bb, 0, kcol + hp)),
                  pl.BlockSpec((None, s, LANES), lambda bb, hp: (bb, 0, 2 * kcol + hp)),
                  pl.BlockSpec((2, 2 * MOBA_BLOCK, MOBA_BLOCK), lambda bb, hp: (hp, 0, 0)),
                  pl.BlockSpec((1, LANES), lambda bb, hp: (0, 0))],
        out_specs=pl.BlockSpec((None, s, LANES), lambda bb, hp: (bb, 0, hp)),
        out_shape=jax.ShapeDtypeStruct((b, s, MOBA_WIDTH), BF16),
        scratch_shapes=[pltpu.VMEM((2, nblk, LANES, MOBA_BLOCK), BF16),
                        pltpu.VMEM((2, nblk // FAR_GROUP, LANES, FAR_GROUP * MOBA_BLOCK), BF16),
                        pltpu.VMEM((2, LANES, MOBA_BLOCK), F32),
                        pltpu.VMEM((2, 1, MOBA_BLOCK), F32),
                        pltpu.VMEM((2, 2, nblk, nblk, MOBA_BLOCK), F32),
                        pltpu.VMEM((2, 2, FAR_GROUP * MOBA_BLOCK, MOBA_BLOCK), F32),
                        pltpu.VMEM((2, 2, FAR_GROUP, 1, MOBA_BLOCK), F32)],
        compiler_params=_params(2),
        name="moba",
    )(far, proj3, proj3, proj3, near, ow)


def _split3(v):
    hi = v.astype(BF16)
    r1 = v - hi.astype(F32)
    mid = r1.astype(BF16)
    lo = (r1 - mid.astype(F32)).astype(BF16)
    return hi, mid, lo


GLA_UNROLL = 8


def _gla_kernel(q_ref, k_ref, v_ref, g_ref, ga_ref, wal_ref, bal_ref, gw_ref, o_ref, b_ref, st_ref):
    seq = q_ref.shape[0]
    c = GLA_CHUNK
    pc = 256

    rr = lax.broadcasted_iota(jnp.int32, (pc, pc), 0)
    cc = lax.broadcasted_iota(jnp.int32, (pc, pc), 1)
    tri = jnp.where((rr >= cc) & (rr // c == cc // c), 1.0, 0.0).astype(BF16)

    def decay_body(j, carry):
        rows = [pl.ds(pl.multiple_of((j * GLA_UNROLL + u) * pc, pc), pc) for u in range(GLA_UNROLL)]
        xg = [jnp.dot(ga_ref[r, :], wal_ref[...], preferred_element_type=F32) + bal_ref[...] for r in rows]
        parts = [_split3((jnp.minimum(x, 0.0) - jnp.log(1.0 + jnp.exp(-jnp.abs(x)))) * (1.0 / GLA_GATE_TAU))
                 for x in xg]
        sums = [[jnp.dot(tri, term, preferred_element_type=F32) for term in p] for p in parts]
        for r, (hi, mid, lo) in zip(rows, sums):
            b_ref[r, :] = hi + mid + lo
        return carry

    lax.fori_loop(0, seq // (pc * GLA_UNROLL), decay_body, 0)

    st_ref[...] = jnp.zeros_like(st_ref)
    lane = lax.broadcasted_iota(jnp.int32, (c, LANES), 1)
    head_mask = (lane < GLA_DK, lane >= GLA_DK)
    causal = lax.broadcasted_iota(jnp.int32, (c, c), 0) >= lax.broadcasted_iota(jnp.int32, (c, c), 1)

    units = [(u, h) for u in range(GLA_UNROLL) for h in range(2)]

    def chunk_body(ci, carry):
        rows = [pl.ds(pl.multiple_of((ci * GLA_UNROLL + u) * c, c), c) for u in range(GLA_UNROLL)]
        qt, kt, qs, ke, e_last = [], [], [], [], []
        for u in range(GLA_UNROLL):
            b = b_ref[rows[u], :]
            ref_row = b[c // 2 - 1:c // 2, :]
            last = b[c - 1:c, :]
            q = q_ref[rows[u], :].astype(F32) * (GLA_DK ** -0.5)
            k = k_ref[rows[u], :].astype(F32)
            qt.append(q * jnp.exp(b - ref_row))
            kt.append((k * jnp.exp(ref_row - b)).astype(BF16))
            qs.append(q * jnp.exp(b))
            ke.append((k * jnp.exp(last - b)).astype(BF16))
            e_last.append(jnp.exp(last))
        vs = {(u, h): v_ref[rows[u], h * GLA_DV:(h + 1) * GLA_DV] for u, h in units}
        a = {(u, h): lax.dot_general(jnp.where(head_mask[h], qt[u], 0.0).astype(BF16), kt[u], NT_DIMS,
                                     preferred_element_type=F32) for u, h in units}
        inc = {(u, h): lax.dot_general(vs[u, h], ke[u], TN_DIMS, preferred_element_type=F32) for u, h in units}
        o = {(u, h): jnp.dot(jnp.where(causal, a[u, h], 0.0).astype(BF16), vs[u, h], preferred_element_type=F32)
             for u, h in units}
        states = {}
        for h in range(2):
            st = st_ref[h]
            for u in range(GLA_UNROLL):
                states[u, h] = st
                st = st * e_last[u] + inc[u, h]
            st_ref[h] = st
        for u, h in units:
            cols = slice(h * GLA_DV, (h + 1) * GLA_DV)
            ou = o[u, h] + lax.dot_general(jnp.where(head_mask[h], qs[u], 0.0).astype(BF16),
                                           states[u, h].astype(BF16), NT_DIMS, preferred_element_type=F32)
            ms = jnp.mean(ou * ou, axis=-1, keepdims=True)
            on = ou * lax.rsqrt(ms + NORM_EPS) * gw_ref[...]
            g = g_ref[rows[u], cols].astype(F32)
            o_ref[rows[u], cols] = (on * _silu(g)).astype(o_ref.dtype)
        return carry

    lax.fori_loop(0, seq // (c * GLA_UNROLL), chunk_body, 0)


def _gla(proj3, ga3, wal, bal, gw):
    b, s, _ = proj3.shape
    npair = GLA_HEADS // 2
    qcol = 3 * MOBA_WIDTH // LANES
    kcol = qcol + GLA_KEY_WIDTH // LANES
    vcol = (3 * MOBA_WIDTH + 2 * GLA_KEY_WIDTH) // (2 * GLA_DV)
    gcol = vcol + npair
    return pl.pallas_call(
        _gla_kernel,
        grid=(b, npair),
        in_specs=[pl.BlockSpec((None, s, LANES), lambda bb, hp: (bb, 0, qcol + hp)),
                  pl.BlockSpec((None, s, LANES), lambda bb, hp: (bb, 0, kcol + hp)),
                  pl.BlockSpec((None, s, 2 * GLA_DV), lambda bb, hp: (bb, 0, vcol + hp)),
                  pl.BlockSpec((None, s, 2 * GLA_DV), lambda bb, hp: (bb, 0, gcol + hp)),
                  pl.BlockSpec((None, s, LANES), lambda bb, hp: (bb, 0, 0)),
                  pl.BlockSpec((LANES, LANES), lambda bb, hp: (0, hp)),
                  pl.BlockSpec((1, LANES), lambda bb, hp: (0, hp)),
                  pl.BlockSpec((1, GLA_DV), lambda bb, hp: (0, 0))],
        out_specs=pl.BlockSpec((None, s, 2 * GLA_DV), lambda bb, hp: (bb, 0, hp)),
        out_shape=jax.ShapeDtypeStruct((b, s, GLA_WIDTH), BF16),
        scratch_shapes=[pltpu.VMEM((s, LANES), F32),
                        pltpu.VMEM((2, GLA_DV, LANES), F32)],
        compiler_params=_params(2),
        name="gla",
    )(proj3, proj3, proj3, proj3, ga3, wal, bal, gw)


HALF = D_MODEL // 2


def _pack_rows(v):
    return pltpu.pack_elementwise([v[:, :HALF], v[:, HALF:]], packed_dtype=BF16)


def _unpack_rows(w):
    return (pltpu.unpack_elementwise(w, index=0, packed_dtype=BF16, unpacked_dtype=F32),
            pltpu.unpack_elementwise(w, index=1, packed_dtype=BF16, unpacked_dtype=F32))


def _outproj_kernel(oa_ref, ob_ref, x_ref, g1_ref, sc_ref, sh_ref, g2_ref, nw_ref, wo_ref,
                    ws1_ref, ws3_ref, ws2_ref, wrt_ref, base_ref, h_ref, st_ref):
    mix = (jnp.dot(oa_ref[...], wo_ref[:MOBA_WIDTH, :], preferred_element_type=F32)
           + jnp.dot(ob_ref[...], wo_ref[MOBA_WIDTH:, :], preferred_element_type=F32))
    x1 = x_ref[...] + g1_ref[...] * mix
    ms = jnp.mean(x1 * x1, axis=-1, keepdims=True)
    h = x1 * lax.rsqrt(ms + NORM_EPS) * nw_ref[...]
    h = h * (1.0 + sc_ref[...]) + sh_ref[...]
    h_ref[...] = _pack_rows(h)
    hb = h.astype(BF16)
    a = jnp.dot(hb, ws1_ref[...], preferred_element_type=F32)
    u = jnp.dot(hb, ws3_ref[...], preferred_element_type=F32)
    shared = jnp.dot((_silu(a) * u).astype(BF16), ws2_ref[...], preferred_element_type=F32)
    base_ref[...] = x1 + g2_ref[...] * shared
    logits_t = lax.dot_general(wrt_ref[...], hb, NT_DIMS, preferred_element_type=F32)
    st_ref[...] = jax.nn.sigmoid(logits_t)


def _outproj(oa, ob, x2, g1, sc, sh, g2, nw, wo, ws1, ws3, ws2, wrt, seq):
    t = x2.shape[0]
    tpb = seq // ROW_TILE
    vec = lambda: pl.BlockSpec((None, 1, D_MODEL), lambda i: (i // tpb, 0, 0))
    full = lambda a: pl.BlockSpec(a.shape, lambda i: (0,) * a.ndim)
    rows = lambda w: pl.BlockSpec((ROW_TILE, w), lambda i: (i, 0))
    return pl.pallas_call(
        _outproj_kernel,
        grid=(t // ROW_TILE,),
        in_specs=[rows(MOBA_WIDTH), rows(GLA_WIDTH), rows(D_MODEL), vec(), vec(), vec(), vec(),
                  full(nw), full(wo), full(ws1), full(ws3), full(ws2), full(wrt)],
        out_specs=[rows(D_MODEL), rows(HALF), pl.BlockSpec((N_EXPERTS, ROW_TILE), lambda i: (0, i))],
        out_shape=[jax.ShapeDtypeStruct((t, D_MODEL), F32),
                   jax.ShapeDtypeStruct((t, HALF), jnp.uint32),
                   jax.ShapeDtypeStruct((N_EXPERTS, t), F32)],
        compiler_params=_params(1),
        name="outproj",
    )(oa, ob, x2, g1, sc, sh, g2, nw, wo, ws1, ws3, ws2, wrt)


SLOT_CODE_SHIFT = 16
SLOT_CODE_BASE = 1 << SLOT_CODE_SHIFT


def _route_kernel(s_ref, eb_ref, code_ref, w_ref, cnt_ref, carry_ref):
    i = pl.program_id(0)
    ne, nt = s_ref.shape

    @pl.when(i == 0)
    def _init():
        carry_ref[...] = jnp.zeros_like(carry_ref)

    s = s_ref[...]
    choice = s + eb_ref[...]
    gio = lax.broadcasted_iota(jnp.int32, (GROUP_SIZE, nt), 0)
    gscore = []
    for g in range(N_GROUPS):
        cg = choice[g * GROUP_SIZE:(g + 1) * GROUP_SIZE, :]
        top1 = jnp.max(cg, axis=0, keepdims=True)
        first = jnp.min(jnp.where(cg == top1, gio, GROUP_SIZE), axis=0, keepdims=True)
        top2 = jnp.max(jnp.where(gio == first, -jnp.inf, cg), axis=0, keepdims=True)
        gscore.append(top1 + top2)
    gs = jnp.concatenate(gscore, axis=0)
    gidx = lax.broadcasted_iota(jnp.int32, gs.shape, 0)
    beaten = jnp.zeros(gs.shape, jnp.int32)
    for m in range(N_GROUPS):
        gm = gs[m:m + 1, :]
        beaten = beaten + jnp.where((gm > gs) | ((gm == gs) & (gidx > m)), 1, 0)
    gkeep = beaten < TOPK_GROUPS
    masked = jnp.concatenate(
        [jnp.where(gkeep[g:g + 1, :], choice[g * GROUP_SIZE:(g + 1) * GROUP_SIZE, :], -jnp.inf)
         for g in range(N_GROUPS)], axis=0)

    eio = lax.broadcasted_iota(jnp.int32, (ne, nt), 0)
    picked = jnp.zeros((ne, nt), F32)
    idx_rows, w_rows, hits = [], [], []
    for _ in range(TOP_K):
        mx = jnp.max(masked, axis=0, keepdims=True)
        idx = jnp.min(jnp.where(masked == mx, eio, ne), axis=0, keepdims=True)
        hit = eio == idx
        w_rows.append(jnp.sum(jnp.where(hit, s, 0.0), axis=0, keepdims=True))
        idx_rows.append(idx)
        hits.append(hit)
        masked = jnp.where(hit, -jnp.inf, masked)
        picked = jnp.where(hit, 1.0, picked)
    wk = jnp.concatenate(w_rows, axis=0)
    w_ref[...] = wk / jnp.sum(wk, axis=0, keepdims=True) * ROUTED_SCALE

    tr = lax.broadcasted_iota(jnp.int32, (nt, nt), 0)
    tc = lax.broadcasted_iota(jnp.int32, (nt, nt), 1)
    before = jnp.where(tr < tc, 1.0, 0.0).astype(BF16)
    pb = picked.astype(BF16)
    pos = carry_ref[...] + jnp.dot(pb, before, preferred_element_type=F32)
    rank = jnp.concatenate(
        [jnp.sum(jnp.where(hit, pos, 0.0), axis=0, keepdims=True) for hit in hits], axis=0).astype(jnp.int32)
    code_ref[...] = jnp.concatenate(idx_rows, axis=0) * SLOT_CODE_BASE + rank
    total = carry_ref[...] + jnp.dot(pb, jnp.ones((nt, nt), BF16), preferred_element_type=F32)
    carry_ref[...] = total
    cnt_ref[...] = total


def _route(scores_t, eb):
    ne, t = scores_t.shape
    assert t <= SLOT_CODE_BASE
    nt = ROUTE_TILE
    tok = lambda dt: jax.ShapeDtypeStruct((TOP_K, t), dt)
    return pl.pallas_call(
        _route_kernel,
        grid=(t // nt,),
        in_specs=[pl.BlockSpec((ne, nt), lambda i: (0, i)),
                  pl.BlockSpec((ne, nt), lambda i: (0, 0))],
        out_specs=[pl.BlockSpec((TOP_K, nt), lambda i: (0, i)),
                   pl.BlockSpec((TOP_K, nt), lambda i: (0, i)),
                   pl.BlockSpec((ne, nt), lambda i: (0, 0))],
        out_shape=[tok(jnp.int32), tok(F32), jax.ShapeDtypeStruct((ne, nt), F32)],
        scratch_shapes=[pltpu.VMEM((ne, nt), F32)],
        compiler_params=_params(1),
        name="route",
    )(scores_t, eb)


SLOT_TILE = 2048


def _slots_kernel(pstart_ref, code_ref, o_ref):
    code = code_ref[...]
    expert = lax.shift_right_logical(code, SLOT_CODE_SHIFT)

    def body(e, acc):
        return jnp.where(expert == e, pstart_ref[e], acc)

    start = lax.fori_loop(0, N_EXPERTS, body, jnp.zeros_like(code), unroll=8)
    o_ref[...] = start + (code & (SLOT_CODE_BASE - 1))


def _slots(pstart, code_t):
    k, t = code_t.shape
    return pl.pallas_call(
        _slots_kernel,
        grid_spec=pltpu.PrefetchScalarGridSpec(
            num_scalar_prefetch=1,
            grid=(t // SLOT_TILE,),
            in_specs=[pl.BlockSpec((k, SLOT_TILE), lambda i, p: (0, i))],
            out_specs=pl.BlockSpec((k, SLOT_TILE), lambda i, p: (0, i)),
        ),
        out_shape=jax.ShapeDtypeStruct((k, t), jnp.int32),
        compiler_params=_params(1),
        name="slots",
    )(pstart, code_t)


SC_WINDOW = 128


def _sc_gather_rows(table, idx_flat):
    from jax.experimental.pallas import tpu_sc as plsc
    info = plsc.get_sparse_core_info()
    nw = info.num_cores * info.num_subcores
    n = idx_flat.shape[0]
    width = table.shape[1]
    per_worker = n // nw
    assert per_worker * nw == n and per_worker % SC_WINDOW == 0
    mesh = plsc.VectorSubcoreMesh(core_axis_name="c", subcore_axis_name="s")

    def body(table_hbm, idx_hbm, out_hbm, idx_v, rows_v, sem):
        wid = lax.axis_index("s") * info.num_cores + lax.axis_index("c")
        base = wid * per_worker

        @pl.loop(0, per_worker // SC_WINDOW)
        def _(w):
            off = pl.multiple_of(base + w * SC_WINDOW, SC_WINDOW)
            pltpu.sync_copy(idx_hbm.at[pl.ds(off, SC_WINDOW)], idx_v)
            pltpu.async_copy(table_hbm.at[idx_v], rows_v, sem).wait()
            pltpu.sync_copy(rows_v, out_hbm.at[pl.ds(off, SC_WINDOW)])

    return pl.kernel(
        body,
        out_type=jax.ShapeDtypeStruct((n, width), table.dtype),
        mesh=mesh,
        scratch_types=[pltpu.VMEM((SC_WINDOW,), jnp.int32),
                       pltpu.VMEM((SC_WINDOW, width), table.dtype),
                       pltpu.SemaphoreType.DMA],
        name="sc_gather",
    )(table, idx_flat)


def _sc_scatter_rows(rows, idx_kt, n_out):
    from jax.experimental.pallas import tpu_sc as plsc
    info = plsc.get_sparse_core_info()
    nw = info.num_cores * info.num_subcores
    t, width = rows.shape
    nk = idx_kt.shape[0]
    per_worker = t // nw
    assert per_worker * nw == t and per_worker % SC_WINDOW == 0
    mesh = plsc.VectorSubcoreMesh(core_axis_name="c", subcore_axis_name="s")

    def body(rows_hbm, idx_hbm, out_hbm, idx_v, rows_v, sem):
        wid = lax.axis_index("s") * info.num_cores + lax.axis_index("c")
        base = wid * per_worker

        @pl.loop(0, per_worker // SC_WINDOW)
        def _(w):
            off = pl.multiple_of(base + w * SC_WINDOW, SC_WINDOW)
            pltpu.sync_copy(rows_hbm.at[pl.ds(off, SC_WINDOW)], rows_v)
            pltpu.sync_copy(idx_hbm.at[:, pl.ds(off, SC_WINDOW)], idx_v)
            copies = [pltpu.async_copy(rows_v, out_hbm.at[idx_v.at[k]], sem) for k in range(nk)]
            for cp in copies:
                cp.wait()

    return pl.kernel(
        body,
        out_type=jax.ShapeDtypeStruct((n_out, width), rows.dtype),
        mesh=mesh,
        scratch_types=[pltpu.VMEM((nk, SC_WINDOW), jnp.int32),
                       pltpu.VMEM((SC_WINDOW, width), rows.dtype),
                       pltpu.SemaphoreType.DMA],
        name="sc_scatter",
    )(rows, idx_kt)


def _expert_kernel(first_ref, count_ref, used_ref, w1_ref, w3_ref, w2_ref, xs_ref, ys_ref,
                   xbuf, ybuf, sem_in, sem_out, w1f, w3f, w2f, sem_w, w1b, w3b, w2b, *, layer):
    e = pl.program_id(0)
    ne = pl.num_programs(0)
    r = xbuf.shape[1]
    n_used = used_ref[0]

    def x_copy(g, slot):
        return pltpu.make_async_copy(xs_ref.at[pl.ds(pl.multiple_of(g * r, r), r), :], xbuf.at[slot], sem_in.at[slot])

    def y_copy(g, slot):
        return pltpu.make_async_copy(ybuf.at[slot], ys_ref.at[pl.ds(pl.multiple_of(g * r, r), r), :], sem_out.at[slot])

    def w_copies(ex, slot):
        return [pltpu.make_async_copy(src.at[layer, ex], dst.at[slot], sem_w.at[slot])
                for src, dst in ((w1_ref, w1f), (w3_ref, w3f), (w2_ref, w2f))]

    nin = xbuf.shape[0]
    nout = ybuf.shape[0]

    @pl.when(e == 0)
    def _first_reads():
        for g in range(nin):
            @pl.when(g < n_used)
            def _(g=g):
                x_copy(g, g).start()
        for cp in w_copies(0, 0):
            cp.start(priority=1)

    @pl.when(e + 1 < ne)
    def _next_weights():
        for cp in w_copies(e + 1, (e + 1) & 1):
            cp.start(priority=1)

    for cp in w_copies(e, e & 1):
        cp.wait()

    n = count_ref[e]

    @pl.when(n > 0)
    def _cast_weights():
        w1b[...] = w1f[e & 1].astype(BF16)
        w3b[...] = w3f[e & 1].astype(BF16)
        w2b[...] = w2f[e & 1].astype(BF16)

    def run_tiles(g, count):
        tiles = [g + j for j in range(count)]
        for gj in tiles:
            x_copy(gj, gj % nin).wait()
        x = jnp.concatenate([xbuf[gj % nin] for gj in tiles], axis=0)
        for gj in tiles:
            @pl.when(gj + nin < n_used)
            def _(gj=gj):
                x_copy(gj + nin, gj % nin).start()
        lo, hi = _unpack_rows(x)
        lo, hi = lo.astype(BF16), hi.astype(BF16)
        a = (jnp.dot(lo, w1b[:HALF, :], preferred_element_type=F32)
             + jnp.dot(hi, w1b[HALF:, :], preferred_element_type=F32))
        u = (jnp.dot(lo, w3b[:HALF, :], preferred_element_type=F32)
             + jnp.dot(hi, w3b[HALF:, :], preferred_element_type=F32))
        y = _pack_rows(jnp.dot((_silu(a) * u).astype(BF16), w2b[...], preferred_element_type=F32))
        for j, gj in enumerate(tiles):
            @pl.when(gj >= nout)
            def _(gj=gj):
                y_copy(gj - nout, gj % nout).wait()
            ybuf[gj % nout] = y[j * r:(j + 1) * r]
            y_copy(gj, gj % nout).start()

    g0 = first_ref[e]
    pair = EXPERT_TILES_PER_MATMUL

    def pair_body(p, carry):
        run_tiles(g0 + p * pair, pair)
        return carry

    lax.fori_loop(0, n // pair, pair_body, 0)
    for left in range(1, pair):
        pl.when(n % pair == left)(functools.partial(run_tiles, g0 + n - left, left))

    @pl.when(e == ne - 1)
    def _drain_writes():
        for back in range(nout, 0, -1):
            @pl.when(n_used >= back)
            def _(back=back):
                y_copy(n_used - back, (n_used - back) % nout).wait()


def _experts(tile_first, tile_count, n_used, xs, w1, w3, w2, layer):
    n_rows = xs.shape[0]
    r = EXPERT_TILE
    any_spec = pl.BlockSpec(memory_space=pl.ANY)
    return pl.pallas_call(
        functools.partial(_expert_kernel, layer=layer),
        grid_spec=pltpu.PrefetchScalarGridSpec(
            num_scalar_prefetch=3,
            grid=(N_EXPERTS,),
            in_specs=[any_spec, any_spec, any_spec, any_spec],
            out_specs=any_spec,
            scratch_shapes=[pltpu.VMEM((EXPERT_IN_RING, r, HALF), jnp.uint32),
                            pltpu.VMEM((EXPERT_OUT_RING, r, HALF), jnp.uint32),
                            pltpu.SemaphoreType.DMA((EXPERT_IN_RING,)),
                            pltpu.SemaphoreType.DMA((EXPERT_OUT_RING,)),
                            pltpu.VMEM((2, D_MODEL, EXPERT_FF), F32),
                            pltpu.VMEM((2, D_MODEL, EXPERT_FF), F32),
                            pltpu.VMEM((2, EXPERT_FF, D_MODEL), F32),
                            pltpu.SemaphoreType.DMA((2,)),
                            pltpu.VMEM((D_MODEL, EXPERT_FF), BF16),
                            pltpu.VMEM((D_MODEL, EXPERT_FF), BF16),
                            pltpu.VMEM((EXPERT_FF, D_MODEL), BF16)],
        ),
        out_shape=jax.ShapeDtypeStruct((n_rows, HALF), jnp.uint32),
        compiler_params=_params(1),
        name="experts",
    )(tile_first, tile_count, n_used, w1, w3, w2, xs)


def _combine_dense_kernel(base_ref, g2_ref, w_ref, yg_ref, o_ref):
    acc_lo = acc_hi = None
    for k in range(TOP_K):
        lo, hi = _unpack_rows(yg_ref[k])
        wk = w_ref[:, k:k + 1]
        acc_lo = wk * lo if acc_lo is None else acc_lo + wk * lo
        acc_hi = wk * hi if acc_hi is None else acc_hi + wk * hi
    o_ref[:, :HALF] = base_ref[:, :HALF] + g2_ref[:, :HALF] * acc_lo
    o_ref[:, HALF:] = base_ref[:, HALF:] + g2_ref[:, HALF:] * acc_hi


def _combine_dense(base, g2, w_tok, yg, batch):
    t = base.shape[0]
    seq = yg.shape[1]
    nt = ROUTE_TILE
    tpb = seq // nt
    rows = lambda i: (batch * tpb + i, 0)
    return pl.pallas_call(
        _combine_dense_kernel,
        grid=(tpb,),
        in_specs=[pl.BlockSpec((nt, D_MODEL), rows),
                  pl.BlockSpec((None, 1, D_MODEL), lambda i: (batch, 0, 0)),
                  pl.BlockSpec((nt, TOP_K), rows),
                  pl.BlockSpec((TOP_K, nt, HALF), lambda i: (0, i, 0))],
        out_specs=pl.BlockSpec((nt, D_MODEL), rows),
        out_shape=jax.ShapeDtypeStruct((t, D_MODEL), F32),
        input_output_aliases={0: 0},
        compiler_params=_params(1),
        name="combine_dense",
    )(base, g2, w_tok, yg)


def _layer(layer, x, c, w_ada, b_ada, norm1_w, norm2_w, w_in, q_norm_w, k_norm_w, rel_bias, w_alpha, b_alpha,
           moba_out_w, gla_out_w, w_out, w_router, e_bias, w1, w3, w2, ws1, ws3, ws2):
    b, s, d = x.shape
    t = b * s
    x2 = x.reshape(t, d)

    mod = _mod(c, w_ada, b_ada)
    sh1, sc1, g1, sh2, sc2, g2 = [mod[:, j * d:(j + 1) * d].reshape(b, 1, d) for j in range(6)]

    w_main = w_in[:, :D_MAIN].astype(BF16)
    w_ga = jnp.zeros((d, LANES), BF16).at[:, :GLA_GATE_RANK].set(w_in[:, D_MAIN:].astype(BF16))
    per_chunk = 256 // MOBA_HEAD_DIM
    qw = jnp.tile(q_norm_w.astype(F32), per_chunk).reshape(1, 256) * (MOBA_HEAD_DIM ** -0.5)
    kw = jnp.tile(k_norm_w.astype(F32), per_chunk).reshape(1, 256)
    proj, ga = _inproj(x2, sc1, sh1, norm1_w.reshape(1, d), w_main, w_ga, qw, kw, s)
    proj3 = proj.reshape(b, s, D_MAIN)

    near, far = _moba_bias_tables(rel_bias)
    ow = jnp.tile(moba_out_w.astype(F32), 2).reshape(1, LANES)
    o_a = _moba(proj3, near, far, ow)

    wal = jnp.zeros((LANES, GLA_KEY_WIDTH), F32).at[:GLA_GATE_RANK].set(w_alpha)
    o_b = _gla(proj3, ga.reshape(b, s, LANES), wal, b_alpha.reshape(1, GLA_KEY_WIDTH),
               gla_out_w.reshape(1, GLA_DV))

    base, h2, scores_t = _outproj(
        o_a.reshape(t, MOBA_WIDTH), o_b.reshape(t, GLA_WIDTH), x2, g1, sc2, sh2, g2,
        norm2_w.reshape(1, d), w_out.astype(BF16), ws1.astype(BF16), ws3.astype(BF16), ws2.astype(BF16),
        w_router.T.astype(BF16), s)

    eb = jnp.broadcast_to(e_bias.astype(F32)[:, None], (N_EXPERTS, ROUTE_TILE))
    code_t, w_t, counts = _route(scores_t, eb)

    r = EXPERT_TILE
    n_tiles = (t * TOP_K + N_EXPERTS * (r - 1) + r - 1) // r
    n_rows = n_tiles * r
    cnt = counts[:, 0].astype(jnp.int32)
    padded = (cnt + r - 1) // r * r
    pend = jnp.cumsum(padded)
    pstart = pend - padded
    n_used = (pend[-1:] // r).astype(jnp.int32)
    dest_t = _slots(pstart, code_t)

    xs = _sc_scatter_rows(h2, dest_t, n_rows)
    ys = _experts(pstart // r, padded // r, n_used, xs, w1, w3, w2, layer)
    w_tok = w_t.T
    out = base
    for bi in range(b):
        idx = dest_t[:, bi * s:(bi + 1) * s].reshape(TOP_K * s)
        yg = _sc_gather_rows(ys, idx).reshape(TOP_K, s, HALF)
        out = _combine_dense(out, g2, w_tok, yg, bi)
    return out.reshape(b, s, d)


def kernel(x, c, w_ada, b_ada, norm1_w, norm2_w, w_in, q_norm_w, k_norm_w, rel_bias, w_alpha, b_alpha,
           moba_out_w, gla_out_w, w_out, w_router, e_bias, w1, w3, w2, ws1, ws3, ws2):
    for l in range(w_ada.shape[0]):
        x = _layer(l, x, c, w_ada[l], b_ada[l], norm1_w[l], norm2_w[l], w_in[l], q_norm_w[l], k_norm_w[l],
                   rel_bias, w_alpha[l], b_alpha[l], moba_out_w[l], gla_out_w[l], w_out[l], w_router[l],
                   e_bias[l], w1, w3, w2, ws1[l], ws3[l], ws2[l])
    return x
```

```python
import functools
import math

import numpy as np
import jax
import jax.numpy as jnp
from jax import lax
from jax.experimental import pallas as pl
from jax.experimental.pallas import tpu as pltpu

D_MODEL = 1024
MOBA_HEADS = 8
MOBA_HEAD_DIM = 64
MOBA_WIDTH = MOBA_HEADS * MOBA_HEAD_DIM
MOBA_BLOCK = 256
MOBA_TOPK = 3
GLA_HEADS = 4
GLA_DK = 64
GLA_DV = 128
GLA_KEY_WIDTH = GLA_HEADS * GLA_DK
GLA_WIDTH = GLA_HEADS * GLA_DV
GLA_GATE_RANK = 16
GLA_GATE_TAU = 16.0
GLA_CHUNK = 64
REL_BUCKETS = 32
REL_MAX_DIST = 128
N_EXPERTS = 256
TOP_K = 8
N_GROUPS = 8
TOPK_GROUPS = 4
GROUP_SIZE = N_EXPERTS // N_GROUPS
EXPERT_FF = 256
SHARED_FF = 256
ROUTED_SCALE = 2.5
NORM_EPS = 1e-6

D_MAIN = 3 * MOBA_WIDTH + 2 * GLA_KEY_WIDTH + 2 * GLA_WIDTH
LANES = 128
VMEM_LIMIT = 56 * 1024 * 1024

ROW_TILE = 512
ROUTE_TILE = 256
EXPERT_TILE = 256
EXPERT_TILES_PER_MATMUL = 3
EXPERT_IN_RING = 8
EXPERT_OUT_RING = 6

F32 = jnp.float32
BF16 = jnp.bfloat16
NT_DIMS = (((1,), (1,)), ((), ()))
TN_DIMS = (((0,), (0,)), ((), ()))


def _params(n_axes):
    return pltpu.CompilerParams(dimension_semantics=("arbitrary",) * n_axes,
                                vmem_limit_bytes=VMEM_LIMIT)


def _silu(v):
    return v * jax.nn.sigmoid(v)


def _mod_kernel(c_ref, w_ref, b_ref, o_ref):
    o_ref[...] = jnp.dot(_silu(c_ref[...]), w_ref[...], preferred_element_type=F32) + b_ref[...]


def _mod(c, w, b):
    rows = 8
    cp = jnp.zeros((rows, D_MODEL), F32).at[:c.shape[0]].set(c)
    n = w.shape[1]
    tn = 1024
    out = pl.pallas_call(
        _mod_kernel,
        grid=(n // tn,),
        in_specs=[pl.BlockSpec((rows, D_MODEL), lambda j: (0, 0)),
                  pl.BlockSpec((D_MODEL, tn), lambda j: (0, j)),
                  pl.BlockSpec((1, tn), lambda j: (0, j))],
        out_specs=pl.BlockSpec((rows, tn), lambda j: (0, j)),
        out_shape=jax.ShapeDtypeStruct((rows, n), F32),
        compiler_params=_params(1),
        name="mod",
    )(cp, w, b.reshape(1, n))
    return out[:c.shape[0]]


def _group_rms_inv(a, group):
    lane = lax.broadcasted_iota(jnp.int32, (1, a.shape[1]), 1)
    a2 = a * a
    inv = jnp.zeros_like(a)
    for g in range(a.shape[1] // group):
        m = (lane >= g * group) & (lane < (g + 1) * group)
        ss = jnp.sum(jnp.where(m, a2, 0.0), axis=-1, keepdims=True)
        inv = jnp.where(m, lax.rsqrt(ss * (1.0 / group) + NORM_EPS), inv)
    return inv


def _inproj_kernel(x_ref, sc_ref, sh_ref, nw_ref, w_ref, wga_ref, qw_ref, kw_ref, o_ref, ga_ref):
    x = x_ref[...]
    ms = jnp.mean(x * x, axis=-1, keepdims=True)
    h = x * lax.rsqrt(ms + NORM_EPS) * nw_ref[...]
    h = h * (1.0 + sc_ref[...]) + sh_ref[...]
    hb = h.astype(BF16)
    cw = 256
    for j in range(D_MAIN // cw):
        acc = jnp.dot(hb, w_ref[:, j * cw:(j + 1) * cw], preferred_element_type=F32)
        if j < 2 * MOBA_WIDTH // cw:
            nw = qw_ref if j < MOBA_WIDTH // cw else kw_ref
            acc = acc * _group_rms_inv(acc, MOBA_HEAD_DIM) * nw[...]
        o_ref[:, j * cw:(j + 1) * cw] = acc.astype(BF16)
    ga_ref[...] = jnp.dot(hb, wga_ref[...], preferred_element_type=F32)


def _inproj(x2, sc, sh, nw, w_main, w_ga, qw, kw, seq):
    t = x2.shape[0]
    tpb = seq // ROW_TILE
    vec = lambda: pl.BlockSpec((None, 1, D_MODEL), lambda i: (i // tpb, 0, 0))
    full = lambda a: pl.BlockSpec(a.shape, lambda i: (0,) * a.ndim)
    return pl.pallas_call(
        _inproj_kernel,
        grid=(t // ROW_TILE,),
        in_specs=[pl.BlockSpec((ROW_TILE, D_MODEL), lambda i: (i, 0)), vec(), vec(),
                  full(nw), full(w_main), full(w_ga), full(qw), full(kw)],
        out_specs=[pl.BlockSpec((ROW_TILE, D_MAIN), lambda i: (i, 0)),
                   pl.BlockSpec((ROW_TILE, LANES), lambda i: (i, 0))],
        out_shape=[jax.ShapeDtypeStruct((t, D_MAIN), BF16),
                   jax.ShapeDtypeStruct((t, LANES), F32)],
        compiler_params=_params(1),
        name="inproj",
    )(x2, sc, sh, nw, w_main, w_ga, qw, kw)


def _t5_bucket_np(rel):
    max_exact = REL_BUCKETS // 2
    relf = np.maximum(rel, 1).astype(np.float64)
    large = max_exact + (np.log(relf / max_exact) / math.log(REL_MAX_DIST / max_exact)
                         * (REL_BUCKETS - max_exact)).astype(np.int32)
    large = np.minimum(large, REL_BUCKETS - 1)
    return np.where(rel < max_exact, rel, large)


def _bias_kernel(rb_ref, idx_ref, o_ref):
    h = pl.program_id(0)
    idx = idx_ref[...]
    tab = jnp.full(idx.shape, -jnp.inf, F32)
    for bk in range(REL_BUCKETS):
        tab = jnp.where(idx == bk, rb_ref[bk * MOBA_HEADS + h], tab)
    o_ref[...] = tab


def _moba_bias_tables(rel_bias):
    j = np.arange(MOBA_BLOCK)[:, None]
    i = np.arange(MOBA_BLOCK)[None, :]
    own_idx = np.where(j <= i, _t5_bucket_np(np.maximum(i - j, 0)), -1)
    prev_idx = _t5_bucket_np(MOBA_BLOCK + i - j)
    idx = jnp.asarray(np.concatenate([prev_idx, own_idx], axis=0).astype(np.int32))
    assert int(_t5_bucket_np(np.array([MOBA_BLOCK + 1]))[0]) == REL_BUCKETS - 1
    rb = rel_bias.astype(F32)
    near = pl.pallas_call(
        _bias_kernel,
        grid=(MOBA_HEADS,),
        in_specs=[pl.BlockSpec(memory_space=pltpu.SMEM),
                  pl.BlockSpec(idx.shape, lambda h: (0, 0))],
        out_specs=pl.BlockSpec((None,) + idx.shape, lambda h: (h, 0, 0)),
        out_shape=jax.ShapeDtypeStruct((MOBA_HEADS,) + idx.shape, F32),
        compiler_params=_params(1),
        name="bias",
    )(rb.reshape(-1), idx)
    return near, rb[REL_BUCKETS - 1]


FAR_GROUP = 4


def _moba_kernel(*refs):
    hp = pl.program_id(1)
    _moba_body(None, hp, *refs, prepare=True)

    def query_block(i, carry):
        _moba_body(i, hp, *refs, prepare=False)
        return carry

    lax.fori_loop(0, refs[2].shape[0] // MOBA_BLOCK, query_block, 0)


def _moba_body(i, hp, far_ref, q_ref, k_ref, v_ref, near_ref, ow_ref, o_ref,
               vt_ref, vtg_ref, acc_ref, m_ref, sel_ref, s_ref, mx_ref, *, prepare):
    nblk = k_ref.shape[0] // MOBA_BLOCK
    ngrp = nblk // FAR_GROUP
    hd = MOBA_HEAD_DIM
    bs = MOBA_BLOCK
    lane = lax.broadcasted_iota(jnp.int32, (bs, LANES), 1)

    def split_heads(qb):
        zero = jnp.zeros_like(qb)
        return jnp.where(lane < hd, qb, zero), jnp.where(lane < hd, zero, qb)

    def _prepare():
        row = lax.broadcasted_iota(jnp.int32, (LANES, bs), 0)
        kmeans = []
        for n in range(nblk):
            kb = k_ref[n * bs:(n + 1) * bs, :].astype(F32)
            kmeans.append(jnp.mean(kb, axis=0, keepdims=True))
            vt = v_ref[n * bs:(n + 1) * bs, :].astype(F32).T
            vt0 = jnp.where(row < hd, vt, 1.0).astype(BF16)
            vt1 = jnp.where(row < hd, 1.0, vt).astype(BF16)
            vt_ref[0, n] = vt0
            vt_ref[1, n] = vt1
            gcols = slice((n % FAR_GROUP) * bs, (n % FAR_GROUP + 1) * bs)
            vtg_ref[0, n // FAR_GROUP, :, gcols] = vt0
            vtg_ref[1, n // FAR_GROUP, :, gcols] = vt1
        kmean = jnp.concatenate(kmeans, axis=0)
        km_hi = kmean.astype(BF16)
        km_lo = (kmean - km_hi.astype(F32)).astype(BF16)
        blk = lax.broadcasted_iota(jnp.int32, (nblk, bs), 0)
        for ib in range(nblk):
            qparts = split_heads(q_ref[ib * bs:(ib + 1) * bs, :])
            for h in range(2):
                gt = (lax.dot_general(km_hi, qparts[h], NT_DIMS, preferred_element_type=F32)
                      + lax.dot_general(km_lo, qparts[h], NT_DIMS, preferred_element_type=F32))
                gt = jnp.where(blk < ib, gt, -jnp.inf)
                cnt = jnp.zeros(gt.shape, jnp.int32)
                for m in range(ib):
                    gm = gt[m:m + 1, :]
                    cnt = cnt + jnp.where((gm > gt) | ((gm == gt) & (blk > m)), 1, 0)
                keep = (blk < ib) & (cnt < MOBA_TOPK)
                sel_ref[0, h, ib] = jnp.where(keep, 1.0, 0.0)
                sel_ref[1, h, ib] = jnp.where(keep & (blk < ib - 1), 1.0, 0.0)

    if prepare:
        _prepare()
        return

    q_rows = pl.ds(pl.multiple_of(i * bs, bs), bs)
    qh = split_heads(q_ref[q_rows, :])

    def finish():
        a0 = acc_ref[0]
        a1 = acc_ref[1]
        row = lax.broadcasted_iota(jnp.int32, a0.shape, 0)
        ot = jnp.where(row < hd, a0 / a0[hd:hd + 1, :], a1 / a1[0:1, :])
        o2 = ot * ot
        ss0 = jnp.sum(jnp.where(row < hd, o2, 0.0), axis=0, keepdims=True)
        ss1 = jnp.sum(jnp.where(row < hd, 0.0, o2), axis=0, keepdims=True)
        inv = jnp.where(row < hd, lax.rsqrt(ss0 * (1.0 / hd) + NORM_EPS), lax.rsqrt(ss1 * (1.0 / hd) + NORM_EPS))
        o_ref[q_rows, :] = ((ot * inv).T * ow_ref[...]).astype(o_ref.dtype)

    @pl.when(i == 0)
    def _own_block_only():
        kb = k_ref[0:bs, :]
        for h in range(2):
            s = lax.dot_general(kb, qh[h], NT_DIMS, preferred_element_type=F32) + near_ref[h, bs:2 * bs, :]
            m_new = jnp.max(s, axis=0, keepdims=True)
            p = jnp.exp(s - m_new).astype(BF16)
            acc_ref[h] = jnp.dot(vt_ref[h, 0], p, preferred_element_type=F32)
        finish()

    n_far = (i + FAR_GROUP - 2) // FAR_GROUP
    gk = FAR_GROUP * bs

    def far_scores(g):
        kb = k_ref[g * gk:(g + 1) * gk, :]
        for h in range(2):
            s = lax.dot_general(kb, qh[h], NT_DIMS, preferred_element_type=F32)
            s_ref[g % 2, h] = s
            for j in range(FAR_GROUP):
                mx_ref[g % 2, h, j] = jnp.max(s[j * bs:(j + 1) * bs], axis=0, keepdims=True)

    def near_blocks(with_far):
        kbs = (k_ref[pl.ds(pl.multiple_of((i - 1) * bs, bs), bs), :],
               k_ref[pl.ds(pl.multiple_of(i * bs, bs), bs), :])
        ss = [[lax.dot_general(kbs[w], qh[h], NT_DIMS, preferred_element_type=F32)
               + near_ref[h, w * bs:(w + 1) * bs, :] for w in range(2)] for h in range(2)]
        if with_far:
            far_scores(0)
        ps, ms = [], []
        for h in range(2):
            s_prev, s_own = ss[h]
            keep = sel_ref[0, h, i, pl.ds(i - 1, 1), :] > 0.5
            mx = jnp.where(keep, jnp.max(s_prev, axis=0, keepdims=True), -jnp.inf)
            m_new = jnp.maximum(jnp.max(s_own, axis=0, keepdims=True), mx)
            ps.append((jnp.exp(s_prev - jnp.where(keep, m_new, jnp.inf)).astype(BF16),
                       jnp.exp(s_own - m_new).astype(BF16)))
            ms.append(m_new)
        for h in range(2):
            acc_ref[h] = (jnp.dot(vt_ref[h, i - 1], ps[h][0], preferred_element_type=F32)
                          + jnp.dot(vt_ref[h, i], ps[h][1], preferred_element_type=F32))
            m_ref[h] = ms[h]

    def far_group(g, with_next):
        if with_next:
            far_scores(g + 1)
        for h in range(2):
            fb = far_ref[2 * hp + h]
            m_old = m_ref[h]
            m_new = m_old
            keeps = []
            for j in range(FAR_GROUP):
                keep = sel_ref[1, h, i, pl.ds(g * FAR_GROUP + j, 1), :] > 0.5
                m_new = jnp.maximum(m_new, jnp.where(keep, mx_ref[g % 2, h, j] + fb, -jnp.inf))
                keeps.append(keep)
            p = jnp.concatenate(
                [jnp.exp(s_ref[g % 2, h, j * bs:(j + 1) * bs, :]
                         - jnp.where(keeps[j], m_new - fb, jnp.inf)).astype(BF16)
                 for j in range(FAR_GROUP)], axis=0)
            pv = jnp.dot(vtg_ref[h, g], p, preferred_element_type=F32)
            acc_ref[h] = acc_ref[h] * jnp.exp(m_old - m_new) + pv
            m_ref[h] = m_new

    def step_body(nf):
        near_blocks(nf > 0)
        for g in range(nf):
            far_group(g, g + 1 < nf)
        finish()

    for nf in range(ngrp + 1):
        pl.when((i >= 1) & (n_far == nf))(functools.partial(step_body, nf))


def _moba(proj3, near, far, ow):
    b, s, _ = proj3.shape
    nblk = s // MOBA_BLOCK
    assert nblk % FAR_GROUP == 0
    npair = MOBA_HEADS // 2
    kcol = MOBA_WIDTH // LANES
    return pl.pallas_call(
        _moba_kernel,
        grid=(b, npair),
        in_specs=[pl.BlockSpec(memory_space=pltpu.SMEM),
                  pl.BlockSpec((None, s, LANES), lambda bb, hp: (bb, 0, hp)),
                  pl.BlockSpec((None, s, LANES), lambda bb, hp: (bb, 0, kcol + hp)),
                  pl.BlockSpec((None, s, LANES), lambda bb, hp: (bb, 0, 2 * kcol + hp)),
                  pl.BlockSpec((2, 2 * MOBA_BLOCK, MOBA_BLOCK), lambda bb, hp: (hp, 0, 0)),
                  pl.BlockSpec((1, LANES), lambda bb, hp: (0, 0))],
        out_specs=pl.BlockSpec((None, s, LANES), lambda bb, hp: (bb, 0, hp)),
        out_shape=jax.ShapeDtypeStruct((b, s, MOBA_WIDTH), BF16),
        scratch_shapes=[pltpu.VMEM((2, nblk, LANES, MOBA_BLOCK), BF16),
                        pltpu.VMEM((2, nblk // FAR_GROUP, LANES, FAR_GROUP * MOBA_BLOCK), BF16),
                        pltpu.VMEM((2, LANES, MOBA_BLOCK), F32),
                        pltpu.VMEM((2, 1, MOBA_BLOCK), F32),
                        pltpu.VMEM((2, 2, nblk, nblk, MOBA_BLOCK), F32),
                        pltpu.VMEM((2, 2, FAR_GROUP * MOBA_BLOCK, MOBA_BLOCK), F32),
                        pltpu.VMEM((2, 2, FAR_GROUP, 1, MOBA_BLOCK), F32)],
        compiler_params=_params(2),
        name="moba",
    )(far, proj3, proj3, proj3, near, ow)


def _split3(v):
    hi = v.astype(BF16)
    r1 = v - hi.astype(F32)
    mid = r1.astype(BF16)
    lo = (r1 - mid.astype(F32)).astype(BF16)
    return hi, mid, lo


GLA_UNROLL = 8


def _gla_kernel(q_ref, k_ref, v_ref, g_ref, ga_ref, wal_ref, bal_ref, gw_ref, o_ref, b_ref, st_ref):
    seq = q_ref.shape[0]
    c = GLA_CHUNK
    pc = 256

    rr = lax.broadcasted_iota(jnp.int32, (pc, pc), 0)
    cc = lax.broadcasted_iota(jnp.int32, (pc, pc), 1)
    tri = jnp.where((rr >= cc) & (rr // c == cc // c), 1.0, 0.0).astype(BF16)

    def decay_body(j, carry):
        rows = [pl.ds(pl.multiple_of((j * GLA_UNROLL + u) * pc, pc), pc) for u in range(GLA_UNROLL)]
        xg = [jnp.dot(ga_ref[r, :], wal_ref[...], preferred_element_type=F32) + bal_ref[...] for r in rows]
        parts = [_split3((jnp.minimum(x, 0.0) - jnp.log(1.0 + jnp.exp(-jnp.abs(x)))) * (1.0 / GLA_GATE_TAU))
                 for x in xg]
        sums = [[jnp.dot(tri, term, preferred_element_type=F32) for term in p] for p in parts]
        for r, (hi, mid, lo) in zip(rows, sums):
            b_ref[r, :] = hi + mid + lo
        return carry

    lax.fori_loop(0, seq // (pc * GLA_UNROLL), decay_body, 0)

    st_ref[...] = jnp.zeros_like(st_ref)
    lane = lax.broadcasted_iota(jnp.int32, (c, LANES), 1)
    head_mask = (lane < GLA_DK, lane >= GLA_DK)
    causal = lax.broadcasted_iota(jnp.int32, (c, c), 0) >= lax.broadcasted_iota(jnp.int32, (c, c), 1)

    units = [(u, h) for u in range(GLA_UNROLL) for h in range(2)]

    def chunk_body(ci, carry):
        rows = [pl.ds(pl.multiple_of((ci * GLA_UNROLL + u) * c, c), c) for u in range(GLA_UNROLL)]
        qt, kt, qs, ke, e_last = [], [], [], [], []
        for u in range(GLA_UNROLL):
            b = b_ref[rows[u], :]
            ref_row = b[c // 2 - 1:c // 2, :]
            last = b[c - 1:c, :]
            q = q_ref[rows[u], :].astype(F32) * (GLA_DK ** -0.5)
            k = k_ref[rows[u], :].astype(F32)
            qt.append(q * jnp.exp(b - ref_row))
            kt.append((k * jnp.exp(ref_row - b)).astype(BF16))
            qs.append(q * jnp.exp(b))
            ke.append((k * jnp.exp(last - b)).astype(BF16))
            e_last.append(jnp.exp(last))
        vs = {(u, h): v_ref[rows[u], h * GLA_DV:(h + 1) * GLA_DV] for u, h in units}
        a = {(u, h): lax.dot_general(jnp.where(head_mask[h], qt[u], 0.0).astype(BF16), kt[u], NT_DIMS,
                                     preferred_element_type=F32) for u, h in units}
        inc = {(u, h): lax.dot_general(vs[u, h], ke[u], TN_DIMS, preferred_element_type=F32) for u, h in units}
        o = {(u, h): jnp.dot(jnp.where(causal, a[u, h], 0.0).astype(BF16), vs[u, h], preferred_element_type=F32)
             for u, h in units}
        states = {}
        for h in range(2):
            st = st_ref[h]
            for u in range(GLA_UNROLL):
                states[u, h] = st
                st = st * e_last[u] + inc[u, h]
            st_ref[h] = st
        for u, h in units:
            cols = slice(h * GLA_DV, (h + 1) * GLA_DV)
            ou = o[u, h] + lax.dot_general(jnp.where(head_mask[h], qs[u], 0.0).astype(BF16),
                                           states[u, h].astype(BF16), NT_DIMS, preferred_element_type=F32)
            ms = jnp.mean(ou * ou, axis=-1, keepdims=True)
            on = ou * lax.rsqrt(ms + NORM_EPS) * gw_ref[...]
            g = g_ref[rows[u], cols].astype(F32)
            o_ref[rows[u], cols] = (on * _silu(g)).astype(o_ref.dtype)
        return carry

    lax.fori_loop(0, seq // (c * GLA_UNROLL), chunk_body, 0)


def _gla(proj3, ga3, wal, bal, gw):
    b, s, _ = proj3.shape
    npair = GLA_HEADS // 2
    qcol = 3 * MOBA_WIDTH // LANES
    kcol = qcol + GLA_KEY_WIDTH // LANES
    vcol = (3 * MOBA_WIDTH + 2 * GLA_KEY_WIDTH) // (2 * GLA_DV)
    gcol = vcol + npair
    return pl.pallas_call(
        _gla_kernel,
        grid=(b, npair),
        in_specs=[pl.BlockSpec((None, s, LANES), lambda bb, hp: (bb, 0, qcol + hp)),
                  pl.BlockSpec((None, s, LANES), lambda bb, hp: (bb, 0, kcol + hp)),
                  pl.BlockSpec((None, s, 2 * GLA_DV), lambda bb, hp: (bb, 0, vcol + hp)),
                  pl.BlockSpec((None, s, 2 * GLA_DV), lambda bb, hp: (bb, 0, gcol + hp)),
                  pl.BlockSpec((None, s, LANES), lambda bb, hp: (bb, 0, 0)),
                  pl.BlockSpec((LANES, LANES), lambda bb, hp: (0, hp)),
                  pl.BlockSpec((1, LANES), lambda bb, hp: (0, hp)),
                  pl.BlockSpec((1, GLA_DV), lambda bb, hp: (0, 0))],
        out_specs=pl.BlockSpec((None, s, 2 * GLA_DV), lambda bb, hp: (bb, 0, hp)),
        out_shape=jax.ShapeDtypeStruct((b, s, GLA_WIDTH), BF16),
        scratch_shapes=[pltpu.VMEM((s, LANES), F32),
                        pltpu.VMEM((2, GLA_DV, LANES), F32)],
        compiler_params=_params(2),
        name="gla",
    )(proj3, proj3, proj3, proj3, ga3, wal, bal, gw)


HALF = D_MODEL // 2


def _pack_rows(v):
    return pltpu.pack_elementwise([v[:, :HALF], v[:, HALF:]], packed_dtype=BF16)


def _unpack_rows(w):
    return (pltpu.unpack_elementwise(w, index=0, packed_dtype=BF16, unpacked_dtype=F32),
            pltpu.unpack_elementwise(w, index=1, packed_dtype=BF16, unpacked_dtype=F32))


def _outproj_kernel(oa_ref, ob_ref, x_ref, g1_ref, sc_ref, sh_ref, g2_ref, nw_ref, wo_ref,
                    ws1_ref, ws3_ref, ws2_ref, wrt_ref, base_ref, h_ref, st_ref):
    mix = (jnp.dot(oa_ref[...], wo_ref[:MOBA_WIDTH, :], preferred_element_type=F32)
           + jnp.dot(ob_ref[...], wo_ref[MOBA_WIDTH:, :], preferred_element_type=F32))
    x1 = x_ref[...] + g1_ref[...] * mix
    ms = jnp.mean(x1 * x1, axis=-1, keepdims=True)
    h = x1 * lax.rsqrt(ms + NORM_EPS) * nw_ref[...]
    h = h * (1.0 + sc_ref[...]) + sh_ref[...]
    h_ref[...] = _pack_rows(h)
    hb = h.astype(BF16)
    a = jnp.dot(hb, ws1_ref[...], preferred_element_type=F32)
    u = jnp.dot(hb, ws3_ref[...], preferred_element_type=F32)
    shared = jnp.dot((_silu(a) * u).astype(BF16), ws2_ref[...], preferred_element_type=F32)
    base_ref[...] = x1 + g2_ref[...] * shared
    logits_t = lax.dot_general(wrt_ref[...], hb, NT_DIMS, preferred_element_type=F32)
    st_ref[...] = jax.nn.sigmoid(logits_t)


def _outproj(oa, ob, x2, g1, sc, sh, g2, nw, wo, ws1, ws3, ws2, wrt, seq):
    t = x2.shape[0]
    tpb = seq // ROW_TILE
    vec = lambda: pl.BlockSpec((None, 1, D_MODEL), lambda i: (i // tpb, 0, 0))
    full = lambda a: pl.BlockSpec(a.shape, lambda i: (0,) * a.ndim)
    rows = lambda w: pl.BlockSpec((ROW_TILE, w), lambda i: (i, 0))
    return pl.pallas_call(
        _outproj_kernel,
        grid=(t // ROW_TILE,),
        in_specs=[rows(MOBA_WIDTH), rows(GLA_WIDTH), rows(D_MODEL), vec(), vec(), vec(), vec(),
                  full(nw), full(wo), full(ws1), full(ws3), full(ws2), full(wrt)],
        out_specs=[rows(D_MODEL), rows(HALF), pl.BlockSpec((N_EXPERTS, ROW_TILE), lambda i: (0, i))],
        out_shape=[jax.ShapeDtypeStruct((t, D_MODEL), F32),
                   jax.ShapeDtypeStruct((t, HALF), jnp.uint32),
                   jax.ShapeDtypeStruct((N_EXPERTS, t), F32)],
        compiler_params=_params(1),
        name="outproj",
    )(oa, ob, x2, g1, sc, sh, g2, nw, wo, ws1, ws3, ws2, wrt)


SLOT_CODE_SHIFT = 16
SLOT_CODE_BASE = 1 << SLOT_CODE_SHIFT


def _route_kernel(s_ref, eb_ref, code_ref, w_ref, cnt_ref, carry_ref):
    i = pl.program_id(0)
    ne, nt = s_ref.shape

    @pl.when(i == 0)
    def _init():
        carry_ref[...] = jnp.zeros_like(carry_ref)

    s = s_ref[...]
    choice = s + eb_ref[...]
    gio = lax.broadcasted_iota(jnp.int32, (GROUP_SIZE, nt), 0)
    gscore = []
    for g in range(N_GROUPS):
        cg = choice[g * GROUP_SIZE:(g + 1) * GROUP_SIZE, :]
        top1 = jnp.max(cg, axis=0, keepdims=True)
        first = jnp.min(jnp.where(cg == top1, gio, GROUP_SIZE), axis=0, keepdims=True)
        top2 = jnp.max(jnp.where(gio == first, -jnp.inf, cg), axis=0, keepdims=True)
        gscore.append(top1 + top2)
    gs = jnp.concatenate(gscore, axis=0)
    gidx = lax.broadcasted_iota(jnp.int32, gs.shape, 0)
    beaten = jnp.zeros(gs.shape, jnp.int32)
    for m in range(N_GROUPS):
        gm = gs[m:m + 1, :]
        beaten = beaten + jnp.where((gm > gs) | ((gm == gs) & (gidx > m)), 1, 0)
    gkeep = beaten < TOPK_GROUPS
    masked = jnp.concatenate(
        [jnp.where(gkeep[g:g + 1, :], choice[g * GROUP_SIZE:(g + 1) * GROUP_SIZE, :], -jnp.inf)
         for g in range(N_GROUPS)], axis=0)

    eio = lax.broadcasted_iota(jnp.int32, (ne, nt), 0)
    picked = jnp.zeros((ne, nt), F32)
    idx_rows, w_rows, hits = [], [], []
    for _ in range(TOP_K):
        mx = jnp.max(masked, axis=0, keepdims=True)
        idx = jnp.min(jnp.where(masked == mx, eio, ne), axis=0, keepdims=True)
        hit = eio == idx
        w_rows.append(jnp.sum(jnp.where(hit, s, 0.0), axis=0, keepdims=True))
        idx_rows.append(idx)
        hits.append(hit)
        masked = jnp.where(hit, -jnp.inf, masked)
        picked = jnp.where(hit, 1.0, picked)
    wk = jnp.concatenate(w_rows, axis=0)
    w_ref[...] = wk / jnp.sum(wk, axis=0, keepdims=True) * ROUTED_SCALE

    tr = lax.broadcasted_iota(jnp.int32, (nt, nt), 0)
    tc = lax.broadcasted_iota(jnp.int32, (nt, nt), 1)
    before = jnp.where(tr < tc, 1.0, 0.0).astype(BF16)
    pb = picked.astype(BF16)
    pos = carry_ref[...] + jnp.dot(pb, before, preferred_element_type=F32)
    rank = jnp.concatenate(
        [jnp.sum(jnp.where(hit, pos, 0.0), axis=0, keepdims=True) for hit in hits], axis=0).astype(jnp.int32)
    code_ref[...] = jnp.concatenate(idx_rows, axis=0) * SLOT_CODE_BASE + rank
    total = carry_ref[...] + jnp.dot(pb, jnp.ones((nt, nt), BF16), preferred_element_type=F32)
    carry_ref[...] = total
    cnt_ref[...] = total


def _route(scores_t, eb):
    ne, t = scores_t.shape
    assert t <= SLOT_CODE_BASE
    nt = ROUTE_TILE
    tok = lambda dt: jax.ShapeDtypeStruct((TOP_K, t), dt)
    return pl.pallas_call(
        _route_kernel,
        grid=(t // nt,),
        in_specs=[pl.BlockSpec((ne, nt), lambda i: (0, i)),
                  pl.BlockSpec((ne, nt), lambda i: (0, 0))],
        out_specs=[pl.BlockSpec((TOP_K, nt), lambda i: (0, i)),
                   pl.BlockSpec((TOP_K, nt), lambda i: (0, i)),
                   pl.BlockSpec((ne, nt), lambda i: (0, 0))],
        out_shape=[tok(jnp.int32), tok(F32), jax.ShapeDtypeStruct((ne, nt), F32)],
        scratch_shapes=[pltpu.VMEM((ne, nt), F32)],
        compiler_params=_params(1),
        name="route",
    )(scores_t, eb)


SLOT_TILE = 2048


def _slots_kernel(pstart_ref, code_ref, o_ref):
    code = code_ref[...]
    expert = lax.shift_right_logical(code, SLOT_CODE_SHIFT)

    def body(e, acc):
        return jnp.where(expert == e, pstart_ref[e], acc)

    start = lax.fori_loop(0, N_EXPERTS, body, jnp.zeros_like(code), unroll=8)
    o_ref[...] = start + (code & (SLOT_CODE_BASE - 1))


def _slots(pstart, code_t):
    k, t = code_t.shape
    return pl.pallas_call(
        _slots_kernel,
        grid_spec=pltpu.PrefetchScalarGridSpec(
            num_scalar_prefetch=1,
            grid=(t // SLOT_TILE,),
            in_specs=[pl.BlockSpec((k, SLOT_TILE), lambda i, p: (0, i))],
            out_specs=pl.BlockSpec((k, SLOT_TILE), lambda i, p: (0, i)),
        ),
        out_shape=jax.ShapeDtypeStruct((k, t), jnp.int32),
        compiler_params=_params(1),
        name="slots",
    )(pstart, code_t)


SC_WINDOW = 128


def _sc_gather_rows(table, idx_flat):
    from jax.experimental.pallas import tpu_sc as plsc
    info = plsc.get_sparse_core_info()
    nw = info.num_cores * info.num_subcores
    n = idx_flat.shape[0]
    width = table.shape[1]
    per_worker = n // nw
    assert per_worker * nw == n and per_worker % SC_WINDOW == 0
    mesh = plsc.VectorSubcoreMesh(core_axis_name="c", subcore_axis_name="s")

    def body(table_hbm, idx_hbm, out_hbm, idx_v, rows_v, sem):
        wid = lax.axis_index("s") * info.num_cores + lax.axis_index("c")
        base = wid * per_worker

        @pl.loop(0, per_worker // SC_WINDOW)
        def _(w):
            off = pl.multiple_of(base + w * SC_WINDOW, SC_WINDOW)
            pltpu.sync_copy(idx_hbm.at[pl.ds(off, SC_WINDOW)], idx_v)
            pltpu.async_copy(table_hbm.at[idx_v], rows_v, sem).wait()
            pltpu.sync_copy(rows_v, out_hbm.at[pl.ds(off, SC_WINDOW)])

    return pl.kernel(
        body,
        out_type=jax.ShapeDtypeStruct((n, width), table.dtype),
        mesh=mesh,
        scratch_types=[pltpu.VMEM((SC_WINDOW,), jnp.int32),
                       pltpu.VMEM((SC_WINDOW, width), table.dtype),
                       pltpu.SemaphoreType.DMA],
        name="sc_gather",
    )(table, idx_flat)


def _sc_scatter_rows(rows, idx_kt, n_out):
    from jax.experimental.pallas import tpu_sc as plsc
    info = plsc.get_sparse_core_info()
    nw = info.num_cores * info.num_subcores
    t, width = rows.shape
    nk = idx_kt.shape[0]
    per_worker = t // nw
    assert per_worker * nw == t and per_worker % SC_WINDOW == 0
    mesh = plsc.VectorSubcoreMesh(core_axis_name="c", subcore_axis_name="s")

    def body(rows_hbm, idx_hbm, out_hbm, idx_v, rows_v, sem):
        wid = lax.axis_index("s") * info.num_cores + lax.axis_index("c")
        base = wid * per_worker

        @pl.loop(0, per_worker // SC_WINDOW)
        def _(w):
            off = pl.multiple_of(base + w * SC_WINDOW, SC_WINDOW)
            pltpu.sync_copy(rows_hbm.at[pl.ds(off, SC_WINDOW)], rows_v)
            pltpu.sync_copy(idx_hbm.at[:, pl.ds(off, SC_WINDOW)], idx_v)
            copies = [pltpu.async_copy(rows_v, out_hbm.at[idx_v.at[k]], sem) for k in range(nk)]
            for cp in copies:
                cp.wait()

    return pl.kernel(
        body,
        out_type=jax.ShapeDtypeStruct((n_out, width), rows.dtype),
        mesh=mesh,
        scratch_types=[pltpu.VMEM((nk, SC_WINDOW), jnp.int32),
                       pltpu.VMEM((SC_WINDOW, width), rows.dtype),
                       pltpu.SemaphoreType.DMA],
        name="sc_scatter",
    )(rows, idx_kt)


def _expert_kernel(first_ref, count_ref, used_ref, w1_ref, w3_ref, w2_ref, xs_ref, ys_ref,
                   xbuf, ybuf, sem_in, sem_out, w1f, w3f, w2f, sem_w, w1b, w3b, w2b, *, layer):
    e = pl.program_id(0)
    ne = pl.num_programs(0)
    r = xbuf.shape[1]
    n_used = used_ref[0]

    def x_copy(g, slot):
        return pltpu.make_async_copy(xs_ref.at[pl.ds(pl.multiple_of(g * r, r), r), :], xbuf.at[slot], sem_in.at[slot])

    def y_copy(g, slot):
        return pltpu.make_async_copy(ybuf.at[slot], ys_ref.at[pl.ds(pl.multiple_of(g * r, r), r), :], sem_out.at[slot])

    def w_copies(ex, slot):
        return [pltpu.make_async_copy(src.at[layer, ex], dst.at[slot], sem_w.at[slot])
                for src, dst in ((w1_ref, w1f), (w3_ref, w3f), (w2_ref, w2f))]

    nin = xbuf.shape[0]
    nout = ybuf.shape[0]

    @pl.when(e == 0)
    def _first_reads():
        for g in range(nin):
            @pl.when(g < n_used)
            def _(g=g):
                x_copy(g, g).start()
        for cp in w_copies(0, 0):
            cp.start(priority=1)

    @pl.when(e + 1 < ne)
    def _next_weights():
        for cp in w_copies(e + 1, (e + 1) & 1):
            cp.start(priority=1)

    for cp in w_copies(e, e & 1):
        cp.wait()

    n = count_ref[e]

    @pl.when(n > 0)
    def _cast_weights():
        w1b[...] = w1f[e & 1].astype(BF16)
        w3b[...] = w3f[e & 1].astype(BF16)
        w2b[...] = w2f[e & 1].astype(BF16)

    def run_tiles(g, count):
        tiles = [g + j for j in range(count)]
        for gj in tiles:
            x_copy(gj, gj % nin).wait()
        x = jnp.concatenate([xbuf[gj % nin] for gj in tiles], axis=0)
        for gj in tiles:
            @pl.when(gj + nin < n_used)
            def _(gj=gj):
                x_copy(gj + nin, gj % nin).start()
        lo, hi = _unpack_rows(x)
        lo, hi = lo.astype(BF16), hi.astype(BF16)
        a = (jnp.dot(lo, w1b[:HALF, :], preferred_element_type=F32)
             + jnp.dot(hi, w1b[HALF:, :], preferred_element_type=F32))
        u = (jnp.dot(lo, w3b[:HALF, :], preferred_element_type=F32)
             + jnp.dot(hi, w3b[HALF:, :], preferred_element_type=F32))
        y = _pack_rows(jnp.dot((_silu(a) * u).astype(BF16), w2b[...], preferred_element_type=F32))
        for j, gj in enumerate(tiles):
            @pl.when(gj >= nout)
            def _(gj=gj):
                y_copy(gj - nout, gj % nout).wait()
            ybuf[gj % nout] = y[j * r:(j + 1) * r]
            y_copy(gj, gj % nout).start()

    g0 = first_ref[e]
    pair = EXPERT_TILES_PER_MATMUL

    def pair_body(p, carry):
        run_tiles(g0 + p * pair, pair)
        return carry

    lax.fori_loop(0, n // pair, pair_body, 0)
    for left in range(1, pair):
        pl.when(n % pair == left)(functools.partial(run_tiles, g0 + n - left, left))

    @pl.when(e == ne - 1)
    def _drain_writes():
        for back in range(nout, 0, -1):
            @pl.when(n_used >= back)
            def _(back=back):
                y_copy(n_used - back, (n_used - back) % nout).wait()


def _experts(tile_first, tile_count, n_used, xs, w1, w3, w2, layer):
    n_rows = xs.shape[0]
    r = EXPERT_TILE
    any_spec = pl.BlockSpec(memory_space=pl.ANY)
    return pl.pallas_call(
        functools.partial(_expert_kernel, layer=layer),
        grid_spec=pltpu.PrefetchScalarGridSpec(
            num_scalar_prefetch=3,
            grid=(N_EXPERTS,),
            in_specs=[any_spec, any_spec, any_spec, any_spec],
            out_specs=any_spec,
            scratch_shapes=[pltpu.VMEM((EXPERT_IN_RING, r, HALF), jnp.uint32),
                            pltpu.VMEM((EXPERT_OUT_RING, r, HALF), jnp.uint32),
                            pltpu.SemaphoreType.DMA((EXPERT_IN_RING,)),
                            pltpu.SemaphoreType.DMA((EXPERT_OUT_RING,)),
                            pltpu.VMEM((2, D_MODEL, EXPERT_FF), F32),
                            pltpu.VMEM((2, D_MODEL, EXPERT_FF), F32),
                            pltpu.VMEM((2, EXPERT_FF, D_MODEL), F32),
                            pltpu.SemaphoreType.DMA((2,)),
                            pltpu.VMEM((D_MODEL, EXPERT_FF), BF16),
                            pltpu.VMEM((D_MODEL, EXPERT_FF), BF16),
                            pltpu.VMEM((EXPERT_FF, D_MODEL), BF16)],
        ),
        out_shape=jax.ShapeDtypeStruct((n_rows, HALF), jnp.uint32),
        compiler_params=_params(1),
        name="experts",
    )(tile_first, tile_count, n_used, w1, w3, w2, xs)


def _combine_dense_kernel(base_ref, g2_ref, w_ref, yg_ref, o_ref):
    acc_lo = acc_hi = None
    for k in range(TOP_K):
        lo, hi = _unpack_rows(yg_ref[k])
        wk = w_ref[:, k:k + 1]
        acc_lo = wk * lo if acc_lo is None else acc_lo + wk * lo
        acc_hi = wk * hi if acc_hi is None else acc_hi + wk * hi
    o_ref[:, :HALF] = base_ref[:, :HALF] + g2_ref[:, :HALF] * acc_lo
    o_ref[:, HALF:] = base_ref[:, HALF:] + g2_ref[:, HALF:] * acc_hi


def _combine_dense(base, g2, w_tok, yg, batch):
    t = base.shape[0]
    seq = yg.shape[1]
    nt = ROUTE_TILE
    tpb = seq // nt
    rows = lambda i: (batch * tpb + i, 0)
    return pl.pallas_call(
        _combine_dense_kernel,
        grid=(tpb,),
        in_specs=[pl.BlockSpec((nt, D_MODEL), rows),
                  pl.BlockSpec((None, 1, D_MODEL), lambda i: (batch, 0, 0)),
                  pl.BlockSpec((nt, TOP_K), rows),
                  pl.BlockSpec((TOP_K, nt, HALF), lambda i: (0, i, 0))],
        out_specs=pl.BlockSpec((nt, D_MODEL), rows),
        out_shape=jax.ShapeDtypeStruct((t, D_MODEL), F32),
        input_output_aliases={0: 0},
        compiler_params=_params(1),
        name="combine_dense",
    )(base, g2, w_tok, yg)


def _layer(layer, x, c, w_ada, b_ada, norm1_w, norm2_w, w_in, q_norm_w, k_norm_w, rel_bias, w_alpha, b_alpha,
           moba_out_w, gla_out_w, w_out, w_router, e_bias, w1, w3, w2, ws1, ws3, ws2):
    b, s, d = x.shape
    t = b * s
    x2 = x.reshape(t, d)

    mod = _mod(c, w_ada, b_ada)
    sh1, sc1, g1, sh2, sc2, g2 = [mod[:, j * d:(j + 1) * d].reshape(b, 1, d) for j in range(6)]

    w_main = w_in[:, :D_MAIN].astype(BF16)
    w_ga = jnp.zeros((d, LANES), BF16).at[:, :GLA_GATE_RANK].set(w_in[:, D_MAIN:].astype(BF16))
    per_chunk = 256 // MOBA_HEAD_DIM
    qw = jnp.tile(q_norm_w.astype(F32), per_chunk).reshape(1, 256) * (MOBA_HEAD_DIM ** -0.5)
    kw = jnp.tile(k_norm_w.astype(F32), per_chunk).reshape(1, 256)
    proj, ga = _inproj(x2, sc1, sh1, norm1_w.reshape(1, d), w_main, w_ga, qw, kw, s)
    proj3 = proj.reshape(b, s, D_MAIN)

    near, far = _moba_bias_tables(rel_bias)
    ow = jnp.tile(moba_out_w.astype(F32), 2).reshape(1, LANES)
    o_a = _moba(proj3, near, far, ow)

    wal = jnp.zeros((LANES, GLA_KEY_WIDTH), F32).at[:GLA_GATE_RANK].set(w_alpha)
    o_b = _gla(proj3, ga.reshape(b, s, LANES), wal, b_alpha.reshape(1, GLA_KEY_WIDTH),
               gla_out_w.reshape(1, GLA_DV))

    base, h2, scores_t = _outproj(
        o_a.reshape(t, MOBA_WIDTH), o_b.reshape(t, GLA_WIDTH), x2, g1, sc2, sh2, g2,
        norm2_w.reshape(1, d), w_out.astype(BF16), ws1.astype(BF16), ws3.astype(BF16), ws2.astype(BF16),
        w_router.T.astype(BF16), s)

    eb = jnp.broadcast_to(e_bias.astype(F32)[:, None], (N_EXPERTS, ROUTE_TILE))
    code_t, w_t, counts = _route(scores_t, eb)

    r = EXPERT_TILE
    n_tiles = (t * TOP_K + N_EXPERTS * (r - 1) + r - 1) // r
    n_rows = n_tiles * r
    cnt = counts[:, 0].astype(jnp.int32)
    padded = (cnt + r - 1) // r * r
    pend = jnp.cumsum(padded)
    pstart = pend - padded
    n_used = (pend[-1:] // r).astype(jnp.int32)
    dest_t = _slots(pstart, code_t)

    xs = _sc_scatter_rows(h2, dest_t, n_rows)
    ys = _experts(pstart // r, padded // r, n_used, xs, w1, w3, w2, layer)
    w_tok = w_t.T
    out = base
    for bi in range(b):
        idx = dest_t[:, bi * s:(bi + 1) * s].reshape(TOP_K * s)
        yg = _sc_gather_rows(ys, idx).reshape(TOP_K, s, HALF)
        out = _combine_dense(out, g2, w_tok, yg, bi)
    return out.reshape(b, s, d)


def kernel(x, c, w_ada, b_ada, norm1_w, norm2_w, w_in, q_norm_w, k_norm_w, rel_bias, w_alpha, b_alpha,
           moba_out_w, gla_out_w, w_out, w_router, e_bias, w1, w3, w2, ws1, ws3, ws2):
    for l in range(w_ada.shape[0]):
        x = _layer(l, x, c, w_ada[l], b_ada[l], norm1_w[l], norm2_w[l], w_in[l], q_norm_w[l], k_norm_w[l],
                   rel_bias, w_alpha[l], b_alpha[l], moba_out_w[l], gla_out_w[l], w_out[l], w_router[l],
                   e_bias[l], w1, w3, w2, ws1[l], ws3[l], ws2[l])
    return x
```

```python
import functools
import math

import numpy as np
import jax
import jax.numpy as jnp
from jax import lax
from jax.experimental import pallas as pl
from jax.experimental.pallas import tpu as pltpu

D_MODEL = 1024
MOBA_HEADS = 8
MOBA_HEAD_DIM = 64
MOBA_WIDTH = MOBA_HEADS * MOBA_HEAD_DIM
MOBA_BLOCK = 256
MOBA_TOPK = 3
GLA_HEADS = 4
GLA_DK = 64
GLA_DV = 128
GLA_KEY_WIDTH = GLA_HEADS * GLA_DK
GLA_WIDTH = GLA_HEADS * GLA_DV
GLA_GATE_RANK = 16
GLA_GATE_TAU = 16.0
GLA_CHUNK = 64
REL_BUCKETS = 32
REL_MAX_DIST = 128
N_EXPERTS = 256
TOP_K = 8
N_GROUPS = 8
TOPK_GROUPS = 4
GROUP_SIZE = N_EXPERTS // N_GROUPS
EXPERT_FF = 256
SHARED_FF = 256
ROUTED_SCALE = 2.5
NORM_EPS = 1e-6
LOG2E = math.log2(math.e)

D_MAIN = 3 * MOBA_WIDTH + 2 * GLA_KEY_WIDTH + 2 * GLA_WIDTH
LANES = 128
VMEM_LIMIT = 56 * 1024 * 1024

ROW_TILE = 512
ROUTE_TILE = 256
EXPERT_TILE = 256
EXPERT_TILES_PER_MATMUL = 3
EXPERT_IN_RING = 8
EXPERT_OUT_RING = 6

F32 = jnp.float32
BF16 = jnp.bfloat16
NT_DIMS = (((1,), (1,)), ((), ()))
TN_DIMS = (((0,), (0,)), ((), ()))


def _params(n_axes):
    return pltpu.CompilerParams(dimension_semantics=("arbitrary",) * n_axes,
                                vmem_limit_bytes=VMEM_LIMIT)


def _silu(v):
    return v * jax.nn.sigmoid(v)


def _mod_kernel(c_ref, w_ref, b_ref, o_ref):
    o_ref[...] = jnp.dot(_silu(c_ref[...]), w_ref[...], preferred_element_type=F32) + b_ref[...]


def _mod(c, w, b):
    rows = 8
    cp = jnp.zeros((rows, D_MODEL), F32).at[:c.shape[0]].set(c)
    n = w.shape[1]
    tn = 1024
    out = pl.pallas_call(
        _mod_kernel,
        grid=(n // tn,),
        in_specs=[pl.BlockSpec((rows, D_MODEL), lambda j: (0, 0)),
                  pl.BlockSpec((D_MODEL, tn), lambda j: (0, j)),
                  pl.BlockSpec((1, tn), lambda j: (0, j))],
        out_specs=pl.BlockSpec((rows, tn), lambda j: (0, j)),
        out_shape=jax.ShapeDtypeStruct((rows, n), F32),
        compiler_params=_params(1),
        name="mod",
    )(cp, w, b.reshape(1, n))
    return out[:c.shape[0]]


def _group_rms_inv(a, group):
    lane = lax.broadcasted_iota(jnp.int32, (1, a.shape[1]), 1)
    a2 = a * a
    inv = jnp.zeros_like(a)
    for g in range(a.shape[1] // group):
        m = (lane >= g * group) & (lane < (g + 1) * group)
        ss = jnp.sum(jnp.where(m, a2, 0.0), axis=-1, keepdims=True)
        inv = jnp.where(m, lax.rsqrt(ss * (1.0 / group) + NORM_EPS), inv)
    return inv


def _inproj_kernel(x_ref, sc_ref, sh_ref, nw_ref, w_ref, wga_ref, qw_ref, kw_ref, o_ref, ga_ref):
    x = x_ref[...]
    ms = jnp.mean(x * x, axis=-1, keepdims=True)
    h = x * lax.rsqrt(ms + NORM_EPS) * nw_ref[...]
    h = h * (1.0 + sc_ref[...]) + sh_ref[...]
    hb = h.astype(BF16)
    cw = 256
    for j in range(D_MAIN // cw):
        acc = jnp.dot(hb, w_ref[:, j * cw:(j + 1) * cw], preferred_element_type=F32)
        if j < 2 * MOBA_WIDTH // cw:
            nw = qw_ref if j < MOBA_WIDTH // cw else kw_ref
            acc = acc * _group_rms_inv(acc, MOBA_HEAD_DIM) * nw[...]
        o_ref[:, j * cw:(j + 1) * cw] = acc.astype(BF16)
    ga_ref[...] = jnp.dot(hb, wga_ref[...], preferred_element_type=F32)


def _inproj(x2, sc, sh, nw, w_main, w_ga, qw, kw, seq):
    t = x2.shape[0]
    tpb = seq // ROW_TILE
    vec = lambda: pl.BlockSpec((None, 1, D_MODEL), lambda i: (i // tpb, 0, 0))
    full = lambda a: pl.BlockSpec(a.shape, lambda i: (0,) * a.ndim)
    return pl.pallas_call(
        _inproj_kernel,
        grid=(t // ROW_TILE,),
        in_specs=[pl.BlockSpec((ROW_TILE, D_MODEL), lambda i: (i, 0)), vec(), vec(),
                  full(nw), full(w_main), full(w_ga), full(qw), full(kw)],
        out_specs=[pl.BlockSpec((ROW_TILE, D_MAIN), lambda i: (i, 0)),
                   pl.BlockSpec((ROW_TILE, LANES), lambda i: (i, 0))],
        out_shape=[jax.ShapeDtypeStruct((t, D_MAIN), BF16),
                   jax.ShapeDtypeStruct((t, LANES), F32)],
        compiler_params=_params(1),
        name="inproj",
    )(x2, sc, sh, nw, w_main, w_ga, qw, kw)


def _t5_bucket_np(rel):
    max_exact = REL_BUCKETS // 2
    relf = np.maximum(rel, 1).astype(np.float64)
    large = max_exact + (np.log(relf / max_exact) / math.log(REL_MAX_DIST / max_exact)
                         * (REL_BUCKETS - max_exact)).astype(np.int32)
    large = np.minimum(large, REL_BUCKETS - 1)
    return np.where(rel < max_exact, rel, large)


def _bias_kernel(rb_ref, idx_ref, o_ref):
    h = pl.program_id(0)
    idx = idx_ref[...]
    tab = jnp.full(idx.shape, -jnp.inf, F32)
    for bk in range(REL_BUCKETS):
        tab = jnp.where(idx == bk, rb_ref[bk * MOBA_HEADS + h], tab)
    o_ref[...] = tab


def _moba_bias_tables(rel_bias):
    j = np.arange(MOBA_BLOCK)[:, None]
    i = np.arange(MOBA_BLOCK)[None, :]
    own_idx = np.where(j <= i, _t5_bucket_np(np.maximum(i - j, 0)), -1)
    prev_idx = _t5_bucket_np(MOBA_BLOCK + i - j)
    idx = jnp.asarray(np.concatenate([prev_idx, own_idx], axis=0).astype(np.int32))
    assert int(_t5_bucket_np(np.array([MOBA_BLOCK + 1]))[0]) == REL_BUCKETS - 1
    rb = rel_bias.astype(F32) * LOG2E
    near = pl.pallas_call(
        _bias_kernel,
        grid=(MOBA_HEADS,),
        in_specs=[pl.BlockSpec(memory_space=pltpu.SMEM),
                  pl.BlockSpec(idx.shape, lambda h: (0, 0))],
        out_specs=pl.BlockSpec((None,) + idx.shape, lambda h: (h, 0, 0)),
        out_shape=jax.ShapeDtypeStruct((MOBA_HEADS,) + idx.shape, F32),
        compiler_params=_params(1),
        name="bias",
    )(rb.reshape(-1), idx)
    return near, rb[REL_BUCKETS - 1]


FAR_GROUP = 4


def _moba_kernel(*refs):
    hp = pl.program_id(1)
    _moba_body(None, hp, *refs, prepare=True)

    def query_block(i, carry):
        _moba_body(i, hp, *refs, prepare=False)
        return carry

    lax.fori_loop(0, refs[2].shape[0] // MOBA_BLOCK, query_block, 0)


def _moba_body(i, hp, far_ref, q_ref, k_ref, v_ref, near_ref, ow_ref, o_ref,
               vt_ref, vtg_ref, acc_ref, m_ref, sel_ref, s_ref, mx_ref, *, prepare):
    nblk = k_ref.shape[0] // MOBA_BLOCK
    ngrp = nblk // FAR_GROUP
    hd = MOBA_HEAD_DIM
    bs = MOBA_BLOCK
    lane = lax.broadcasted_iota(jnp.int32, (bs, LANES), 1)

    def split_heads(qb):
        zero = jnp.zeros_like(qb)
        return jnp.where(lane < hd, qb, zero), jnp.where(lane < hd, zero, qb)

    def _prepare():
        row = lax.broadcasted_iota(jnp.int32, (LANES, bs), 0)
        kmeans = []
        for n in range(nblk):
            kb = k_ref[n * bs:(n + 1) * bs, :].astype(F32)
            kmeans.append(jnp.mean(kb, axis=0, keepdims=True))
            vt = v_ref[n * bs:(n + 1) * bs, :].astype(F32).T
            vt0 = jnp.where(row < hd, vt, 1.0).astype(BF16)
            vt1 = jnp.where(row < hd, 1.0, vt).astype(BF16)
            vt_ref[0, n] = vt0
            vt_ref[1, n] = vt1
            gcols = slice((n % FAR_GROUP) * bs, (n % FAR_GROUP + 1) * bs)
            vtg_ref[0, n // FAR_GROUP, :, gcols] = vt0
            vtg_ref[1, n // FAR_GROUP, :, gcols] = vt1
        kmean = jnp.concatenate(kmeans, axis=0)
        km_hi = kmean.astype(BF16)
        km_lo = (kmean - km_hi.astype(F32)).astype(BF16)
        blk = lax.broadcasted_iota(jnp.int32, (nblk, bs), 0)
        for ib in range(nblk):
            qparts = split_heads(q_ref[ib * bs:(ib + 1) * bs, :])
            for h in range(2):
                gt = (lax.dot_general(km_hi, qparts[h], NT_DIMS, preferred_element_type=F32)
                      + lax.dot_general(km_lo, qparts[h], NT_DIMS, preferred_element_type=F32))
                gt = jnp.where(blk < ib, gt, -jnp.inf)
                cnt = jnp.zeros(gt.shape, jnp.int32)
                for m in range(ib):
                    gm = gt[m:m + 1, :]
                    cnt = cnt + jnp.where((gm > gt) | ((gm == gt) & (blk > m)), 1, 0)
                keep = (blk < ib) & (cnt < MOBA_TOPK)
                sel_ref[0, h, ib] = jnp.where(keep, 1.0, 0.0)
                sel_ref[1, h, ib] = jnp.where(keep & (blk < ib - 1), 1.0, 0.0)

    if prepare:
        _prepare()
        return

    q_rows = pl.ds(pl.multiple_of(i * bs, bs), bs)
    qh = split_heads(q_ref[q_rows, :])

    def finish():
        a0 = acc_ref[0]
        a1 = acc_ref[1]
        row = lax.broadcasted_iota(jnp.int32, a0.shape, 0)
        ot = jnp.where(row < hd, a0 / a0[hd:hd + 1, :], a1 / a1[0:1, :])
        o2 = ot * ot
        ss0 = jnp.sum(jnp.where(row < hd, o2, 0.0), axis=0, keepdims=True)
        ss1 = jnp.sum(jnp.where(row < hd, 0.0, o2), axis=0, keepdims=True)
        inv = jnp.where(row < hd, lax.rsqrt(ss0 * (1.0 / hd) + NORM_EPS), lax.rsqrt(ss1 * (1.0 / hd) + NORM_EPS))
        o_ref[q_rows, :] = ((ot * inv).T * ow_ref[...]).astype(o_ref.dtype)

    @pl.when(i == 0)
    def _own_block_only():
        kb = k_ref[0:bs, :]
        for h in range(2):
            s = lax.dot_general(kb, qh[h], NT_DIMS, preferred_element_type=F32) + near_ref[h, bs:2 * bs, :]
            m_new = jnp.max(s, axis=0, keepdims=True)
            p = jnp.exp2(s - m_new).astype(BF16)
            acc_ref[h] = jnp.dot(vt_ref[h, 0], p, preferred_element_type=F32)
        finish()

    n_far = (i + FAR_GROUP - 2) // FAR_GROUP
    gk = FAR_GROUP * bs

    def far_scores(g):
        kb = k_ref[g * gk:(g + 1) * gk, :]
        for h in range(2):
            s = lax.dot_general(kb, qh[h], NT_DIMS, preferred_element_type=F32)
            s_ref[g % 2, h] = s
            for j in range(FAR_GROUP):
                mx_ref[g % 2, h, j] = jnp.max(s[j * bs:(j + 1) * bs], axis=0, keepdims=True)

    def near_blocks(with_far):
        kbs = (k_ref[pl.ds(pl.multiple_of((i - 1) * bs, bs), bs), :],
               k_ref[pl.ds(pl.multiple_of(i * bs, bs), bs), :])
        ss = [[lax.dot_general(kbs[w], qh[h], NT_DIMS, preferred_element_type=F32)
               + near_ref[h, w * bs:(w + 1) * bs, :] for w in range(2)] for h in range(2)]
        if with_far:
            far_scores(0)
        ps, ms = [], []
        for h in range(2):
            s_prev, s_own = ss[h]
            keep = sel_ref[0, h, i, pl.ds(i - 1, 1), :] > 0.5
            mx = jnp.where(keep, jnp.max(s_prev, axis=0, keepdims=True), -jnp.inf)
            m_new = jnp.maximum(jnp.max(s_own, axis=0, keepdims=True), mx)
            ps.append((jnp.exp2(s_prev - jnp.where(keep, m_new, jnp.inf)).astype(BF16),
                       jnp.exp2(s_own - m_new).astype(BF16)))
            ms.append(m_new)
        for h in range(2):
            acc_ref[h] = (jnp.dot(vt_ref[h, i - 1], ps[h][0], preferred_element_type=F32)
                          + jnp.dot(vt_ref[h, i], ps[h][1], preferred_element_type=F32))
            m_ref[h] = ms[h]

    def far_group(g, with_next):
        if with_next:
            far_scores(g + 1)
        for h in range(2):
            fb = far_ref[2 * hp + h]
            m_old = m_ref[h]
            m_new = m_old
            keeps = []
            for j in range(FAR_GROUP):
                keep = sel_ref[1, h, i, pl.ds(g * FAR_GROUP + j, 1), :] > 0.5
                m_new = jnp.maximum(m_new, jnp.where(keep, mx_ref[g % 2, h, j] + fb, -jnp.inf))
                keeps.append(keep)
            p = jnp.concatenate(
                [jnp.exp2(s_ref[g % 2, h, j * bs:(j + 1) * bs, :]
                         - jnp.where(keeps[j], m_new - fb, jnp.inf)).astype(BF16)
                 for j in range(FAR_GROUP)], axis=0)
            pv = jnp.dot(vtg_ref[h, g], p, preferred_element_type=F32)
            acc_ref[h] = acc_ref[h] * jnp.exp2(m_old - m_new) + pv
            m_ref[h] = m_new

    def step_body(nf):
        near_blocks(nf > 0)
        for g in range(nf):
            far_group(g, g + 1 < nf)
        finish()

    for nf in range(ngrp + 1):
        pl.when((i >= 1) & (n_far == nf))(functools.partial(step_body, nf))


def _moba(proj3, near, far, ow):
    b, s, _ = proj3.shape
    nblk = s // MOBA_BLOCK
    assert nblk % FAR_GROUP == 0
    npair = MOBA_HEADS // 2
    kcol = MOBA_WIDTH // LANES
    return pl.pallas_call(
        _moba_kernel,
        grid=(b, npair),
        in_specs=[pl.BlockSpec(memory_space=pltpu.SMEM),
                  pl.BlockSpec((None, s, LANES), lambda bb, hp: (bb, 0, hp)),
                  pl.BlockSpec((None, s, LANES), lambda bb, hp: (bb, 0, kcol + hp)),
                  pl.BlockSpec((None, s, LANES), lambda bb, hp: (bb, 0, 2 * kcol + hp)),
                  pl.BlockSpec((2, 2 * MOBA_BLOCK, MOBA_BLOCK), lambda bb, hp: (hp, 0, 0)),
                  pl.BlockSpec((1, LANES), lambda bb, hp: (0, 0))],
        out_specs=pl.BlockSpec((None, s, LANES), lambda bb, hp: (bb, 0, hp)),
        out_shape=jax.ShapeDtypeStruct((b, s, MOBA_WIDTH), BF16),
        scratch_shapes=[pltpu.VMEM((2, nblk, LANES, MOBA_BLOCK), BF16),
                        pltpu.VMEM((2, nblk // FAR_GROUP, LANES, FAR_GROUP * MOBA_BLOCK), BF16),
                        pltpu.VMEM((2, LANES, MOBA_BLOCK), F32),
                        pltpu.VMEM((2, 1, MOBA_BLOCK), F32),
                        pltpu.VMEM((2, 2, nblk, nblk, MOBA_BLOCK), F32),
                        pltpu.VMEM((2, 2, FAR_GROUP * MOBA_BLOCK, MOBA_BLOCK), F32),
                        pltpu.VMEM((2, 2, FAR_GROUP, 1, MOBA_BLOCK), F32)],
        compiler_params=_params(2),
        name="moba",
    )(far, proj3, proj3, proj3, near, ow)


def _split3(v):
    hi = v.astype(BF16)
    r1 = v - hi.astype(F32)
    mid = r1.astype(BF16)
    lo = (r1 - mid.astype(F32)).astype(BF16)
    return hi, mid, lo


GLA_UNROLL = 8


def _gla_kernel(q_ref, k_ref, v_ref, g_ref, ga_ref, wal_ref, bal_ref, gw_ref, o_ref, b_ref, st_ref):
    seq = q_ref.shape[0]
    c = GLA_CHUNK
    pc = 256

    rr = lax.broadcasted_iota(jnp.int32, (pc, pc), 0)
    cc = lax.broadcasted_iota(jnp.int32, (pc, pc), 1)
    tri = jnp.where((rr >= cc) & (rr // c == cc // c), 1.0, 0.0).astype(BF16)

    def decay_body(j, carry):
        rows = [pl.ds(pl.multiple_of((j * GLA_UNROLL + u) * pc, pc), pc) for u in range(GLA_UNROLL)]
        xg = [jnp.dot(ga_ref[r, :], wal_ref[...], preferred_element_type=F32) + bal_ref[...] for r in rows]
        parts = [_split3((jnp.minimum(x, 0.0) - jnp.log(1.0 + jnp.exp(-jnp.abs(x)))) * (1.0 / GLA_GATE_TAU))
                 for x in xg]
        sums = [[jnp.dot(tri, term, preferred_element_type=F32) for term in p] for p in parts]
        for r, (hi, mid, lo) in zip(rows, sums):
            b_ref[r, :] = hi + mid + lo
        return carry

    lax.fori_loop(0, seq // (pc * GLA_UNROLL), decay_body, 0)

    st_ref[...] = jnp.zeros_like(st_ref)
    lane = lax.broadcasted_iota(jnp.int32, (c, LANES), 1)
    head_mask = (lane < GLA_DK, lane >= GLA_DK)
    causal = lax.broadcasted_iota(jnp.int32, (c, c), 0) >= lax.broadcasted_iota(jnp.int32, (c, c), 1)

    units = [(u, h) for u in range(GLA_UNROLL) for h in range(2)]

    def chunk_body(ci, carry):
        rows = [pl.ds(pl.multiple_of((ci * GLA_UNROLL + u) * c, c), c) for u in range(GLA_UNROLL)]
        qt, kt, qs, ke, e_last = [], [], [], [], []
        for u in range(GLA_UNROLL):
            b = b_ref[rows[u], :]
            ref_row = b[c // 2 - 1:c // 2, :]
            last = b[c - 1:c, :]
            q = q_ref[rows[u], :].astype(F32) * (GLA_DK ** -0.5)
            k = k_ref[rows[u], :].astype(F32)
            qt.append(q * jnp.exp(b - ref_row))
            kt.append((k * jnp.exp(ref_row - b)).astype(BF16))
            qs.append(q * jnp.exp(b))
            ke.append((k * jnp.exp(last - b)).astype(BF16))
            e_last.append(jnp.exp(last))
        vs = {(u, h): v_ref[rows[u], h * GLA_DV:(h + 1) * GLA_DV] for u, h in units}
        a = {(u, h): lax.dot_general(jnp.where(head_mask[h], qt[u], 0.0).astype(BF16), kt[u], NT_DIMS,
                                     preferred_element_type=F32) for u, h in units}
        inc = {(u, h): lax.dot_general(vs[u, h], ke[u], TN_DIMS, preferred_element_type=F32) for u, h in units}
        o = {(u, h): jnp.dot(jnp.where(causal, a[u, h], 0.0).astype(BF16), vs[u, h], preferred_element_type=F32)
             for u, h in units}
        states = {}
        for h in range(2):
            st = st_ref[h]
            for u in range(GLA_UNROLL):
                states[u, h] = st
                st = st * e_last[u] + inc[u, h]
            st_ref[h] = st
        for u, h in units:
            cols = slice(h * GLA_DV, (h + 1) * GLA_DV)
            ou = o[u, h] + lax.dot_general(jnp.where(head_mask[h], qs[u], 0.0).astype(BF16),
                                           states[u, h].astype(BF16), NT_DIMS, preferred_element_type=F32)
            ms = jnp.mean(ou * ou, axis=-1, keepdims=True)
            on = ou * lax.rsqrt(ms + NORM_EPS) * gw_ref[...]
            g = g_ref[rows[u], cols].astype(F32)
            o_ref[rows[u], cols] = (on * _silu(g)).astype(o_ref.dtype)
        return carry

    lax.fori_loop(0, seq // (c * GLA_UNROLL), chunk_body, 0)


def _gla(proj3, ga3, wal, bal, gw):
    b, s, _ = proj3.shape
    npair = GLA_HEADS // 2
    qcol = 3 * MOBA_WIDTH // LANES
    kcol = qcol + GLA_KEY_WIDTH // LANES
    vcol = (3 * MOBA_WIDTH + 2 * GLA_KEY_WIDTH) // (2 * GLA_DV)
    gcol = vcol + npair
    return pl.pallas_call(
        _gla_kernel,
        grid=(b, npair),
        in_specs=[pl.BlockSpec((None, s, LANES), lambda bb, hp: (bb, 0, qcol + hp)),
                  pl.BlockSpec((None, s, LANES), lambda bb, hp: (bb, 0, kcol + hp)),
                  pl.BlockSpec((None, s, 2 * GLA_DV), lambda bb, hp: (bb, 0, vcol + hp)),
                  pl.BlockSpec((None, s, 2 * GLA_DV), lambda bb, hp: (bb, 0, gcol + hp)),
                  pl.BlockSpec((None, s, LANES), lambda bb, hp: (bb, 0, 0)),
                  pl.BlockSpec((LANES, LANES), lambda bb, hp: (0, hp)),
                  pl.BlockSpec((1, LANES), lambda bb, hp: (0, hp)),
                  pl.BlockSpec((1, GLA_DV), lambda bb, hp: (0, 0))],
        out_specs=pl.BlockSpec((None, s, 2 * GLA_DV), lambda bb, hp: (bb, 0, hp)),
        out_shape=jax.ShapeDtypeStruct((b, s, GLA_WIDTH), BF16),
        scratch_shapes=[pltpu.VMEM((s, LANES), F32),
                        pltpu.VMEM((2, GLA_DV, LANES), F32)],
        compiler_params=_params(2),
        name="gla",
    )(proj3, proj3, proj3, proj3, ga3, wal, bal, gw)


HALF = D_MODEL // 2


def _pack_rows(v):
    return pltpu.pack_elementwise([v[:, :HALF], v[:, HALF:]], packed_dtype=BF16)


def _unpack_rows(w):
    return (pltpu.unpack_elementwise(w, index=0, packed_dtype=BF16, unpacked_dtype=F32),
            pltpu.unpack_elementwise(w, index=1, packed_dtype=BF16, unpacked_dtype=F32))


def _outproj_kernel(oa_ref, ob_ref, x_ref, g1_ref, sc_ref, sh_ref, g2_ref, nw_ref, wo_ref,
                    ws1_ref, ws3_ref, ws2_ref, wrt_ref, base_ref, h_ref, st_ref):
    mix = (jnp.dot(oa_ref[...], wo_ref[:MOBA_WIDTH, :], preferred_element_type=F32)
           + jnp.dot(ob_ref[...], wo_ref[MOBA_WIDTH:, :], preferred_element_type=F32))
    x1 = x_ref[...] + g1_ref[...] * mix
    ms = jnp.mean(x1 * x1, axis=-1, keepdims=True)
    h = x1 * lax.rsqrt(ms + NORM_EPS) * nw_ref[...]
    h = h * (1.0 + sc_ref[...]) + sh_ref[...]
    h_ref[...] = _pack_rows(h)
    hb = h.astype(BF16)
    a = jnp.dot(hb, ws1_ref[...], preferred_element_type=F32)
    u = jnp.dot(hb, ws3_ref[...], preferred_element_type=F32)
    shared = jnp.dot((_silu(a) * u).astype(BF16), ws2_ref[...], preferred_element_type=F32)
    base_ref[...] = x1 + g2_ref[...] * shared
    logits_t = lax.dot_general(wrt_ref[...], hb, NT_DIMS, preferred_element_type=F32)
    st_ref[...] = jax.nn.sigmoid(logits_t)


def _outproj(oa, ob, x2, g1, sc, sh, g2, nw, wo, ws1, ws3, ws2, wrt, seq):
    t = x2.shape[0]
    tpb = seq // ROW_TILE
    vec = lambda: pl.BlockSpec((None, 1, D_MODEL), lambda i: (i // tpb, 0, 0))
    full = lambda a: pl.BlockSpec(a.shape, lambda i: (0,) * a.ndim)
    rows = lambda w: pl.BlockSpec((ROW_TILE, w), lambda i: (i, 0))
    return pl.pallas_call(
        _outproj_kernel,
        grid=(t // ROW_TILE,),
        in_specs=[rows(MOBA_WIDTH), rows(GLA_WIDTH), rows(D_MODEL), vec(), vec(), vec(), vec(),
                  full(nw), full(wo), full(ws1), full(ws3), full(ws2), full(wrt)],
        out_specs=[rows(D_MODEL), rows(HALF), pl.BlockSpec((N_EXPERTS, ROW_TILE), lambda i: (0, i))],
        out_shape=[jax.ShapeDtypeStruct((t, D_MODEL), F32),
                   jax.ShapeDtypeStruct((t, HALF), jnp.uint32),
                   jax.ShapeDtypeStruct((N_EXPERTS, t), F32)],
        compiler_params=_params(1),
        name="outproj",
    )(oa, ob, x2, g1, sc, sh, g2, nw, wo, ws1, ws3, ws2, wrt)


SLOT_CODE_SHIFT = 16
SLOT_CODE_BASE = 1 << SLOT_CODE_SHIFT


def _route_kernel(s_ref, eb_ref, code_ref, w_ref, cnt_ref, carry_ref):
    i = pl.program_id(0)
    ne, nt = s_ref.shape

    @pl.when(i == 0)
    def _init():
        carry_ref[...] = jnp.zeros_like(carry_ref)

    s = s_ref[...]
    choice = s + eb_ref[...]
    gio = lax.broadcasted_iota(jnp.int32, (GROUP_SIZE, nt), 0)
    gscore = []
    for g in range(N_GROUPS):
        cg = choice[g * GROUP_SIZE:(g + 1) * GROUP_SIZE, :]
        top1 = jnp.max(cg, axis=0, keepdims=True)
        first = jnp.min(jnp.where(cg == top1, gio, GROUP_SIZE), axis=0, keepdims=True)
        top2 = jnp.max(jnp.where(gio == first, -jnp.inf, cg), axis=0, keepdims=True)
        gscore.append(top1 + top2)
    gs = jnp.concatenate(gscore, axis=0)
    gidx = lax.broadcasted_iota(jnp.int32, gs.shape, 0)
    beaten = jnp.zeros(gs.shape, jnp.int32)
    for m in range(N_GROUPS):
        gm = gs[m:m + 1, :]
        beaten = beaten + jnp.where((gm > gs) | ((gm == gs) & (gidx > m)), 1, 0)
    gkeep = beaten < TOPK_GROUPS
    masked = jnp.concatenate(
        [jnp.where(gkeep[g:g + 1, :], choice[g * GROUP_SIZE:(g + 1) * GROUP_SIZE, :], -jnp.inf)
         for g in range(N_GROUPS)], axis=0)

    eio = lax.broadcasted_iota(jnp.int32, (ne, nt), 0)
    picked = jnp.zeros((ne, nt), F32)
    idx_rows, w_rows, hits = [], [], []
    for _ in range(TOP_K):
        mx = jnp.max(masked, axis=0, keepdims=True)
        idx = jnp.min(jnp.where(masked == mx, eio, ne), axis=0, keepdims=True)
        hit = eio == idx
        w_rows.append(jnp.sum(jnp.where(hit, s, 0.0), axis=0, keepdims=True))
        idx_rows.append(idx)
        hits.append(hit)
        masked = jnp.where(hit, -jnp.inf, masked)
        picked = jnp.where(hit, 1.0, picked)
    wk = jnp.concatenate(w_rows, axis=0)
    w_ref[...] = wk / jnp.sum(wk, axis=0, keepdims=True) * ROUTED_SCALE

    tr = lax.broadcasted_iota(jnp.int32, (nt, nt), 0)
    tc = lax.broadcasted_iota(jnp.int32, (nt, nt), 1)
    before = jnp.where(tr < tc, 1.0, 0.0).astype(BF16)
    pb = picked.astype(BF16)
    pos = carry_ref[...] + jnp.dot(pb, before, preferred_element_type=F32)
    rank = jnp.concatenate(
        [jnp.sum(jnp.where(hit, pos, 0.0), axis=0, keepdims=True) for hit in hits], axis=0).astype(jnp.int32)
    code_ref[...] = jnp.concatenate(idx_rows, axis=0) * SLOT_CODE_BASE + rank
    total = carry_ref[...] + jnp.dot(pb, jnp.ones((nt, nt), BF16), preferred_element_type=F32)
    carry_ref[...] = total
    cnt_ref[...] = total


def _route(scores_t, eb):
    ne, t = scores_t.shape
    assert t <= SLOT_CODE_BASE
    nt = ROUTE_TILE
    tok = lambda dt: jax.ShapeDtypeStruct((TOP_K, t), dt)
    return pl.pallas_call(
        _route_kernel,
        grid=(t // nt,),
        in_specs=[pl.BlockSpec((ne, nt), lambda i: (0, i)),
                  pl.BlockSpec((ne, nt), lambda i: (0, 0))],
        out_specs=[pl.BlockSpec((TOP_K, nt), lambda i: (0, i)),
                   pl.BlockSpec((TOP_K, nt), lambda i: (0, i)),
                   pl.BlockSpec((ne, nt), lambda i: (0, 0))],
        out_shape=[tok(jnp.int32), tok(F32), jax.ShapeDtypeStruct((ne, nt), F32)],
        scratch_shapes=[pltpu.VMEM((ne, nt), F32)],
        compiler_params=_params(1),
        name="route",
    )(scores_t, eb)


SLOT_TILE = 2048


def _slots_kernel(pstart_ref, code_ref, o_ref):
    code = code_ref[...]
    expert = lax.shift_right_logical(code, SLOT_CODE_SHIFT)

    def body(e, acc):
        return jnp.where(expert == e, pstart_ref[e], acc)

    start = lax.fori_loop(0, N_EXPERTS, body, jnp.zeros_like(code), unroll=8)
    o_ref[...] = start + (code & (SLOT_CODE_BASE - 1))


def _slots(pstart, code_t):
    k, t = code_t.shape
    return pl.pallas_call(
        _slots_kernel,
        grid_spec=pltpu.PrefetchScalarGridSpec(
            num_scalar_prefetch=1,
            grid=(t // SLOT_TILE,),
            in_specs=[pl.BlockSpec((k, SLOT_TILE), lambda i, p: (0, i))],
            out_specs=pl.BlockSpec((k, SLOT_TILE), lambda i, p: (0, i)),
        ),
        out_shape=jax.ShapeDtypeStruct((k, t), jnp.int32),
        compiler_params=_params(1),
        name="slots",
    )(pstart, code_t)


SC_WINDOW = 128


def _sc_gather_rows(table, idx_flat):
    from jax.experimental.pallas import tpu_sc as plsc
    info = plsc.get_sparse_core_info()
    nw = info.num_cores * info.num_subcores
    n = idx_flat.shape[0]
    width = table.shape[1]
    per_worker = n // nw
    assert per_worker * nw == n and per_worker % SC_WINDOW == 0
    mesh = plsc.VectorSubcoreMesh(core_axis_name="c", subcore_axis_name="s")

    def body(table_hbm, idx_hbm, out_hbm, idx_v, rows_v, sem):
        wid = lax.axis_index("s") * info.num_cores + lax.axis_index("c")
        base = wid * per_worker

        @pl.loop(0, per_worker // SC_WINDOW)
        def _(w):
            off = pl.multiple_of(base + w * SC_WINDOW, SC_WINDOW)
            pltpu.sync_copy(idx_hbm.at[pl.ds(off, SC_WINDOW)], idx_v)
            pltpu.async_copy(table_hbm.at[idx_v], rows_v, sem).wait()
            pltpu.sync_copy(rows_v, out_hbm.at[pl.ds(off, SC_WINDOW)])

    return pl.kernel(
        body,
        out_type=jax.ShapeDtypeStruct((n, width), table.dtype),
        mesh=mesh,
        scratch_types=[pltpu.VMEM((SC_WINDOW,), jnp.int32),
                       pltpu.VMEM((SC_WINDOW, width), table.dtype),
                       pltpu.SemaphoreType.DMA],
        name="sc_gather",
    )(table, idx_flat)


def _sc_scatter_rows(rows, idx_kt, n_out):
    from jax.experimental.pallas import tpu_sc as plsc
    info = plsc.get_sparse_core_info()
    nw = info.num_cores * info.num_subcores
    t, width = rows.shape
    nk = idx_kt.shape[0]
    per_worker = t // nw
    assert per_worker * nw == t and per_worker % SC_WINDOW == 0
    mesh = plsc.VectorSubcoreMesh(core_axis_name="c", subcore_axis_name="s")

    def body(rows_hbm, idx_hbm, out_hbm, idx_v, rows_v, sem):
        wid = lax.axis_index("s") * info.num_cores + lax.axis_index("c")
        base = wid * per_worker

        @pl.loop(0, per_worker // SC_WINDOW)
        def _(w):
            off = pl.multiple_of(base + w * SC_WINDOW, SC_WINDOW)
            pltpu.sync_copy(rows_hbm.at[pl.ds(off, SC_WINDOW)], rows_v)
            pltpu.sync_copy(idx_hbm.at[:, pl.ds(off, SC_WINDOW)], idx_v)
            copies = [pltpu.async_copy(rows_v, out_hbm.at[idx_v.at[k]], sem) for k in range(nk)]
            for cp in copies:
                cp.wait()

    return pl.kernel(
        body,
        out_type=jax.ShapeDtypeStruct((n_out, width), rows.dtype),
        mesh=mesh,
        scratch_types=[pltpu.VMEM((nk, SC_WINDOW), jnp.int32),
                       pltpu.VMEM((SC_WINDOW, width), rows.dtype),
                       pltpu.SemaphoreType.DMA],
        name="sc_scatter",
    )(rows, idx_kt)


def _expert_kernel(first_ref, count_ref, used_ref, w1_ref, w3_ref, w2_ref, xs_ref, ys_ref,
                   xbuf, ybuf, sem_in, sem_out, w1f, w3f, w2f, sem_w, w1b, w3b, w2b, *, layer):
    e = pl.program_id(0)
    ne = pl.num_programs(0)
    r = xbuf.shape[1]
    n_used = used_ref[0]

    def x_copy(g, slot):
        return pltpu.make_async_copy(xs_ref.at[pl.ds(pl.multiple_of(g * r, r), r), :], xbuf.at[slot], sem_in.at[slot])

    def y_copy(g, slot):
        return pltpu.make_async_copy(ybuf.at[slot], ys_ref.at[pl.ds(pl.multiple_of(g * r, r), r), :], sem_out.at[slot])

    def w_copies(ex, slot):
        return [pltpu.make_async_copy(src.at[layer, ex], dst.at[slot], sem_w.at[slot])
                for src, dst in ((w1_ref, w1f), (w3_ref, w3f), (w2_ref, w2f))]

    nin = xbuf.shape[0]
    nout = ybuf.shape[0]

    @pl.when(e == 0)
    def _first_reads():
        for g in range(nin):
            @pl.when(g < n_used)
            def _(g=g):
                x_copy(g, g).start()
        for cp in w_copies(0, 0):
            cp.start(priority=1)

    @pl.when(e + 1 < ne)
    def _next_weights():
        for cp in w_copies(e + 1, (e + 1) & 1):
            cp.start(priority=1)

    for cp in w_copies(e, e & 1):
        cp.wait()

    n = count_ref[e]

    @pl.when(n > 0)
    def _cast_weights():
        w1b[...] = w1f[e & 1].astype(BF16)
        w3b[...] = w3f[e & 1].astype(BF16)
        w2b[...] = w2f[e & 1].astype(BF16)

    def run_tiles(g, count):
        tiles = [g + j for j in range(count)]
        for gj in tiles:
            x_copy(gj, gj % nin).wait()
        x = jnp.concatenate([xbuf[gj % nin] for gj in tiles], axis=0)
        for gj in tiles:
            @pl.when(gj + nin < n_used)
            def _(gj=gj):
                x_copy(gj + nin, gj % nin).start()
        lo, hi = _unpack_rows(x)
        lo, hi = lo.astype(BF16), hi.astype(BF16)
        a = (jnp.dot(lo, w1b[:HALF, :], preferred_element_type=F32)
             + jnp.dot(hi, w1b[HALF:, :], preferred_element_type=F32))
        u = (jnp.dot(lo, w3b[:HALF, :], preferred_element_type=F32)
             + jnp.dot(hi, w3b[HALF:, :], preferred_element_type=F32))
        y = _pack_rows(jnp.dot((_silu(a) * u).astype(BF16), w2b[...], preferred_element_type=F32))
        for j, gj in enumerate(tiles):
            @pl.when(gj >= nout)
            def _(gj=gj):
                y_copy(gj - nout, gj % nout).wait()
            ybuf[gj % nout] = y[j * r:(j + 1) * r]
            y_copy(gj, gj % nout).start()

    g0 = first_ref[e]
    pair = EXPERT_TILES_PER_MATMUL

    def pair_body(p, carry):
        run_tiles(g0 + p * pair, pair)
        return carry

    lax.fori_loop(0, n // pair, pair_body, 0)
    for left in range(1, pair):
        pl.when(n % pair == left)(functools.partial(run_tiles, g0 + n - left, left))

    @pl.when(e == ne - 1)
    def _drain_writes():
        for back in range(nout, 0, -1):
            @pl.when(n_used >= back)
            def _(back=back):
                y_copy(n_used - back, (n_used - back) % nout).wait()


def _experts(tile_first, tile_count, n_used, xs, w1, w3, w2, layer):
    n_rows = xs.shape[0]
    r = EXPERT_TILE
    any_spec = pl.BlockSpec(memory_space=pl.ANY)
    return pl.pallas_call(
        functools.partial(_expert_kernel, layer=layer),
        grid_spec=pltpu.PrefetchScalarGridSpec(
            num_scalar_prefetch=3,
            grid=(N_EXPERTS,),
            in_specs=[any_spec, any_spec, any_spec, any_spec],
            out_specs=any_spec,
            scratch_shapes=[pltpu.VMEM((EXPERT_IN_RING, r, HALF), jnp.uint32),
                            pltpu.VMEM((EXPERT_OUT_RING, r, HALF), jnp.uint32),
                            pltpu.SemaphoreType.DMA((EXPERT_IN_RING,)),
                            pltpu.SemaphoreType.DMA((EXPERT_OUT_RING,)),
                            pltpu.VMEM((2, D_MODEL, EXPERT_FF), F32),
                            pltpu.VMEM((2, D_MODEL, EXPERT_FF), F32),
                            pltpu.VMEM((2, EXPERT_FF, D_MODEL), F32),
                            pltpu.SemaphoreType.DMA((2,)),
                            pltpu.VMEM((D_MODEL, EXPERT_FF), BF16),
                            pltpu.VMEM((D_MODEL, EXPERT_FF), BF16),
                            pltpu.VMEM((EXPERT_FF, D_MODEL), BF16)],
        ),
        out_shape=jax.ShapeDtypeStruct((n_rows, HALF), jnp.uint32),
        compiler_params=_params(1),
        name="experts",
    )(tile_first, tile_count, n_used, w1, w3, w2, xs)


def _combine_dense_kernel(base_ref, g2_ref, w_ref, yg_ref, o_ref):
    acc_lo = acc_hi = None
    for k in range(TOP_K):
        lo, hi = _unpack_rows(yg_ref[k])
        wk = w_ref[:, k:k + 1]
        acc_lo = wk * lo if acc_lo is None else acc_lo + wk * lo
        acc_hi = wk * hi if acc_hi is None else acc_hi + wk * hi
    o_ref[:, :HALF] = base_ref[:, :HALF] + g2_ref[:, :HALF] * acc_lo
    o_ref[:, HALF:] = base_ref[:, HALF:] + g2_ref[:, HALF:] * acc_hi


def _combine_dense(base, g2, w_tok, yg, batch):
    t = base.shape[0]
    seq = yg.shape[1]
    nt = ROUTE_TILE
    tpb = seq // nt
    rows = lambda i: (batch * tpb + i, 0)
    return pl.pallas_call(
        _combine_dense_kernel,
        grid=(tpb,),
        in_specs=[pl.BlockSpec((nt, D_MODEL), rows),
                  pl.BlockSpec((None, 1, D_MODEL), lambda i: (batch, 0, 0)),
                  pl.BlockSpec((nt, TOP_K), rows),
                  pl.BlockSpec((TOP_K, nt, HALF), lambda i: (0, i, 0))],
        out_specs=pl.BlockSpec((nt, D_MODEL), rows),
        out_shape=jax.ShapeDtypeStruct((t, D_MODEL), F32),
        input_output_aliases={0: 0},
        compiler_params=_params(1),
        name="combine_dense",
    )(base, g2, w_tok, yg)


def _layer(layer, x, c, w_ada, b_ada, norm1_w, norm2_w, w_in, q_norm_w, k_norm_w, rel_bias, w_alpha, b_alpha,
           moba_out_w, gla_out_w, w_out, w_router, e_bias, w1, w3, w2, ws1, ws3, ws2):
    b, s, d = x.shape
    t = b * s
    x2 = x.reshape(t, d)

    mod = _mod(c, w_ada, b_ada)
    sh1, sc1, g1, sh2, sc2, g2 = [mod[:, j * d:(j + 1) * d].reshape(b, 1, d) for j in range(6)]

    w_main = w_in[:, :D_MAIN].astype(BF16)
    w_ga = jnp.zeros((d, LANES), BF16).at[:, :GLA_GATE_RANK].set(w_in[:, D_MAIN:].astype(BF16))
    per_chunk = 256 // MOBA_HEAD_DIM
    qw = jnp.tile(q_norm_w.astype(F32), per_chunk).reshape(1, 256) * (MOBA_HEAD_DIM ** -0.5 * LOG2E)
    kw = jnp.tile(k_norm_w.astype(F32), per_chunk).reshape(1, 256)
    proj, ga = _inproj(x2, sc1, sh1, norm1_w.reshape(1, d), w_main, w_ga, qw, kw, s)
    proj3 = proj.reshape(b, s, D_MAIN)

    near, far = _moba_bias_tables(rel_bias)
    ow = jnp.tile(moba_out_w.astype(F32), 2).reshape(1, LANES)
    o_a = _moba(proj3, near, far, ow)

    wal = jnp.zeros((LANES, GLA_KEY_WIDTH), F32).at[:GLA_GATE_RANK].set(w_alpha)
    o_b = _gla(proj3, ga.reshape(b, s, LANES), wal, b_alpha.reshape(1, GLA_KEY_WIDTH),
               gla_out_w.reshape(1, GLA_DV))

    base, h2, scores_t = _outproj(
        o_a.reshape(t, MOBA_WIDTH), o_b.reshape(t, GLA_WIDTH), x2, g1, sc2, sh2, g2,
        norm2_w.reshape(1, d), w_out.astype(BF16), ws1.astype(BF16), ws3.astype(BF16), ws2.astype(BF16),
        w_router.T.astype(BF16), s)

    eb = jnp.broadcast_to(e_bias.astype(F32)[:, None], (N_EXPERTS, ROUTE_TILE))
    code_t, w_t, counts = _route(scores_t, eb)

    r = EXPERT_TILE
    n_tiles = (t * TOP_K + N_EXPERTS * (r - 1) + r - 1) // r
    n_rows = n_tiles * r
    cnt = counts[:, 0].astype(jnp.int32)
    padded = (cnt + r - 1) // r * r
    pend = jnp.cumsum(padded)
    pstart = pend - padded
    n_used = (pend[-1:] // r).astype(jnp.int32)
    dest_t = _slots(pstart, code_t)

    xs = _sc_scatter_rows(h2, dest_t, n_rows)
    ys = _experts(pstart // r, padded // r, n_used, xs, w1, w3, w2, layer)
    w_tok = w_t.T
    out = base
    for bi in range(b):
        idx = dest_t[:, bi * s:(bi + 1) * s].reshape(TOP_K * s)
        yg = _sc_gather_rows(ys, idx).reshape(TOP_K, s, HALF)
        out = _combine_dense(out, g2, w_tok, yg, bi)
    return out.reshape(b, s, d)


def kernel(x, c, w_ada, b_ada, norm1_w, norm2_w, w_in, q_norm_w, k_norm_w, rel_bias, w_alpha, b_alpha,
           moba_out_w, gla_out_w, w_out, w_router, e_bias, w1, w3, w2, ws1, ws3, ws2):
    for l in range(w_ada.shape[0]):
        x = _layer(l, x, c, w_ada[l], b_ada[l], norm1_w[l], norm2_w[l], w_in[l], q_norm_w[l], k_norm_w[l],
                   rel_bias, w_alpha[l], b_alpha[l], moba_out_w[l], gla_out_w[l], w_out[l], w_router[l],
                   e_bias[l], w1, w3, w2, ws1[l], ws3[l], ws2[l])
    return x
```

```python
import functools
import math

import numpy as np
import jax
import jax.numpy as jnp
from jax import lax
from jax.experimental import pallas as pl
from jax.experimental.pallas import tpu as pltpu

D_MODEL = 1024
MOBA_HEADS = 8
MOBA_HEAD_DIM = 64
MOBA_WIDTH = MOBA_HEADS * MOBA_HEAD_DIM
MOBA_BLOCK = 256
MOBA_TOPK = 3
GLA_HEADS = 4
GLA_DK = 64
GLA_DV = 128
GLA_KEY_WIDTH = GLA_HEADS * GLA_DK
GLA_WIDTH = GLA_HEADS * GLA_DV
GLA_GATE_RANK = 16
GLA_GATE_TAU = 16.0
GLA_CHUNK = 64
REL_BUCKETS = 32
REL_MAX_DIST = 128
N_EXPERTS = 256
TOP_K = 8
N_GROUPS = 8
TOPK_GROUPS = 4
GROUP_SIZE = N_EXPERTS // N_GROUPS
EXPERT_FF = 256
SHARED_FF = 256
ROUTED_SCALE = 2.5
NORM_EPS = 1e-6
LOG2E = math.log2(math.e)

D_MAIN = 3 * MOBA_WIDTH + 2 * GLA_KEY_WIDTH + 2 * GLA_WIDTH
LANES = 128
VMEM_LIMIT = 56 * 1024 * 1024

ROW_TILE = 512
ROUTE_TILE = 256
EXPERT_TILE = 256
EXPERT_TILES_PER_MATMUL = 3
EXPERT_IN_RING = 8
EXPERT_OUT_RING = 6

F32 = jnp.float32
BF16 = jnp.bfloat16
NT_DIMS = (((1,), (1,)), ((), ()))
TN_DIMS = (((0,), (0,)), ((), ()))


def _params(n_axes):
    return pltpu.CompilerParams(dimension_semantics=("arbitrary",) * n_axes,
                                vmem_limit_bytes=VMEM_LIMIT)


def _silu(v):
    return v * jax.nn.sigmoid(v)


def _mod_kernel(c_ref, w_ref, b_ref, o_ref):
    o_ref[...] = jnp.dot(_silu(c_ref[...]), w_ref[...], preferred_element_type=F32) + b_ref[...]


def _mod(c, w, b):
    rows = 8
    cp = jnp.zeros((rows, D_MODEL), F32).at[:c.shape[0]].set(c)
    n = w.shape[1]
    tn = 1024
    out = pl.pallas_call(
        _mod_kernel,
        grid=(n // tn,),
        in_specs=[pl.BlockSpec((rows, D_MODEL), lambda j: (0, 0)),
                  pl.BlockSpec((D_MODEL, tn), lambda j: (0, j)),
                  pl.BlockSpec((1, tn), lambda j: (0, j))],
        out_specs=pl.BlockSpec((rows, tn), lambda j: (0, j)),
        out_shape=jax.ShapeDtypeStruct((rows, n), F32),
        compiler_params=_params(1),
        name="mod",
    )(cp, w, b.reshape(1, n))
    return out[:c.shape[0]]


def _group_rms_inv(a, group):
    lane = lax.broadcasted_iota(jnp.int32, (1, a.shape[1]), 1)
    a2 = a * a
    inv = jnp.zeros_like(a)
    for g in range(a.shape[1] // group):
        m = (lane >= g * group) & (lane < (g + 1) * group)
        ss = jnp.sum(jnp.where(m, a2, 0.0), axis=-1, keepdims=True)
        inv = jnp.where(m, lax.rsqrt(ss * (1.0 / group) + NORM_EPS), inv)
    return inv


def _inproj_kernel(x_ref, sc_ref, sh_ref, nw_ref, w_ref, wga_ref, qw_ref, kw_ref, o_ref, ga_ref):
    x = x_ref[...]
    ms = jnp.mean(x * x, axis=-1, keepdims=True)
    h = x * lax.rsqrt(ms + NORM_EPS) * nw_ref[...]
    h = h * (1.0 + sc_ref[...]) + sh_ref[...]
    hb = h.astype(BF16)
    cw = 256
    for j in range(D_MAIN // cw):
        acc = jnp.dot(hb, w_ref[:, j * cw:(j + 1) * cw], preferred_element_type=F32)
        if j < 2 * MOBA_WIDTH // cw:
            nw = qw_ref if j < MOBA_WIDTH // cw else kw_ref
            acc = acc * _group_rms_inv(acc, MOBA_HEAD_DIM) * nw[...]
        o_ref[:, j * cw:(j + 1) * cw] = acc.astype(BF16)
    ga_ref[...] = jnp.dot(hb, wga_ref[...], preferred_element_type=F32)


def _inproj(x2, sc, sh, nw, w_main, w_ga, qw, kw, seq):
    t = x2.shape[0]
    tpb = seq // ROW_TILE
    vec = lambda: pl.BlockSpec((None, 1, D_MODEL), lambda i: (i // tpb, 0, 0))
    full = lambda a: pl.BlockSpec(a.shape, lambda i: (0,) * a.ndim)
    return pl.pallas_call(
        _inproj_kernel,
        grid=(t // ROW_TILE,),
        in_specs=[pl.BlockSpec((ROW_TILE, D_MODEL), lambda i: (i, 0)), vec(), vec(),
                  full(nw), full(w_main), full(w_ga), full(qw), full(kw)],
        out_specs=[pl.BlockSpec((ROW_TILE, D_MAIN), lambda i: (i, 0)),
                   pl.BlockSpec((ROW_TILE, LANES), lambda i: (i, 0))],
        out_shape=[jax.ShapeDtypeStruct((t, D_MAIN), BF16),
                   jax.ShapeDtypeStruct((t, LANES), F32)],
        compiler_params=_params(1),
        name="inproj",
    )(x2, sc, sh, nw, w_main, w_ga, qw, kw)


def _t5_bucket_np(rel):
    max_exact = REL_BUCKETS // 2
    relf = np.maximum(rel, 1).astype(np.float64)
    large = max_exact + (np.log(relf / max_exact) / math.log(REL_MAX_DIST / max_exact)
                         * (REL_BUCKETS - max_exact)).astype(np.int32)
    large = np.minimum(large, REL_BUCKETS - 1)
    return np.where(rel < max_exact, rel, large)


def _bias_kernel(rb_ref, idx_ref, o_ref):
    h = pl.program_id(0)
    idx = idx_ref[...]
    tab = jnp.full(idx.shape, -jnp.inf, F32)
    for bk in range(REL_BUCKETS):
        tab = jnp.where(idx == bk, rb_ref[bk * MOBA_HEADS + h], tab)
    o_ref[...] = tab


def _moba_bias_tables(rel_bias):
    j = np.arange(MOBA_BLOCK)[:, None]
    i = np.arange(MOBA_BLOCK)[None, :]
    own_idx = np.where(j <= i, _t5_bucket_np(np.maximum(i - j, 0)), -1)
    prev_idx = _t5_bucket_np(MOBA_BLOCK + i - j)
    idx = jnp.asarray(np.concatenate([prev_idx, own_idx], axis=0).astype(np.int32))
    assert int(_t5_bucket_np(np.array([MOBA_BLOCK + 1]))[0]) == REL_BUCKETS - 1
    rb = rel_bias.astype(F32) * LOG2E
    near = pl.pallas_call(
        _bias_kernel,
        grid=(MOBA_HEADS,),
        in_specs=[pl.BlockSpec(memory_space=pltpu.SMEM),
                  pl.BlockSpec(idx.shape, lambda h: (0, 0))],
        out_specs=pl.BlockSpec((None,) + idx.shape, lambda h: (h, 0, 0)),
        out_shape=jax.ShapeDtypeStruct((MOBA_HEADS,) + idx.shape, F32),
        compiler_params=_params(1),
        name="bias",
    )(rb.reshape(-1), idx)
    return near, rb[REL_BUCKETS - 1]


FAR_GROUP = 4


def _moba_kernel(*refs):
    hp = pl.program_id(1)
    _moba_body(None, hp, *refs, prepare=True)
    _moba_body(None, hp, *refs, prepare=False)

    def block_pair(j, carry):
        _moba_body(2 * j, hp, *refs, prepare=False)
        return carry

    lax.fori_loop(1, refs[2].shape[0] // (2 * MOBA_BLOCK), block_pair, 0)


def _moba_body(i, hp, far_ref, q_ref, k_ref, v_ref, near_ref, ow_ref, o_ref,
               vt_ref, vtg_ref, acc_ref, m_ref, sel_ref, s_ref, mx_ref, *, prepare):
    nblk = k_ref.shape[0] // MOBA_BLOCK
    ngrp = nblk // FAR_GROUP
    hd = MOBA_HEAD_DIM
    bs = MOBA_BLOCK
    lane = lax.broadcasted_iota(jnp.int32, (bs, LANES), 1)

    def split_heads(qb):
        zero = jnp.zeros_like(qb)
        return jnp.where(lane < hd, qb, zero), jnp.where(lane < hd, zero, qb)

    def _prepare():
        row = lax.broadcasted_iota(jnp.int32, (LANES, bs), 0)
        kmeans = []
        for n in range(nblk):
            kb = k_ref[n * bs:(n + 1) * bs, :].astype(F32)
            kmeans.append(jnp.mean(kb, axis=0, keepdims=True))
            vt = v_ref[n * bs:(n + 1) * bs, :].astype(F32).T
            vt0 = jnp.where(row < hd, vt, 1.0).astype(BF16)
            vt1 = jnp.where(row < hd, 1.0, vt).astype(BF16)
            vt_ref[0, n] = vt0
            vt_ref[1, n] = vt1
            gcols = slice((n % FAR_GROUP) * bs, (n % FAR_GROUP + 1) * bs)
            vtg_ref[0, n // FAR_GROUP, :, gcols] = vt0
            vtg_ref[1, n // FAR_GROUP, :, gcols] = vt1
        kmean = jnp.concatenate(kmeans, axis=0)
        km_hi = kmean.astype(BF16)
        km_lo = (kmean - km_hi.astype(F32)).astype(BF16)
        blk = lax.broadcasted_iota(jnp.int32, (nblk, bs), 0)
        for ib in range(nblk):
            qparts = split_heads(q_ref[ib * bs:(ib + 1) * bs, :])
            for h in range(2):
                gt = (lax.dot_general(km_hi, qparts[h], NT_DIMS, preferred_element_type=F32)
                      + lax.dot_general(km_lo, qparts[h], NT_DIMS, preferred_element_type=F32))
                gt = jnp.where(blk < ib, gt, -jnp.inf)
                cnt = jnp.zeros(gt.shape, jnp.int32)
                for m in range(ib):
                    gm = gt[m:m + 1, :]
                    cnt = cnt + jnp.where((gm > gt) | ((gm == gt) & (blk > m)), 1, 0)
                keep = (blk < ib) & (cnt < MOBA_TOPK)
                sel_ref[0, h, ib] = jnp.where(keep, 1.0, 0.0)
                sel_ref[1, h, ib] = jnp.where(keep & (blk < ib - 1), 1.0, 0.0)

    if prepare:
        _prepare()
        return

    gk = FAR_GROUP * bs

    def rows_of(ib):
        if isinstance(ib, int):
            return slice(ib * bs, (ib + 1) * bs)
        return pl.ds(pl.multiple_of(ib * bs, bs), bs)

    def finish(ib, slot):
        a0 = acc_ref[slot, 0]
        a1 = acc_ref[slot, 1]
        row = lax.broadcasted_iota(jnp.int32, a0.shape, 0)
        ot = jnp.where(row < hd, a0 / a0[hd:hd + 1, :], a1 / a1[0:1, :])
        o2 = ot * ot
        ss0 = jnp.sum(jnp.where(row < hd, o2, 0.0), axis=0, keepdims=True)
        ss1 = jnp.sum(jnp.where(row < hd, 0.0, o2), axis=0, keepdims=True)
        inv = jnp.where(row < hd, lax.rsqrt(ss0 * (1.0 / hd) + NORM_EPS), lax.rsqrt(ss1 * (1.0 / hd) + NORM_EPS))
        o_ref[rows_of(ib), :] = ((ot * inv).T * ow_ref[...]).astype(o_ref.dtype)

    def far_scores(g, slot, qh):
        kb = k_ref[g * gk:(g + 1) * gk, :]
        for h in range(2):
            s = lax.dot_general(kb, qh[h], NT_DIMS, preferred_element_type=F32)
            s_ref[slot, g % 2, h] = s
            for j in range(FAR_GROUP):
                mx_ref[slot, g % 2, h, j] = jnp.max(s[j * bs:(j + 1) * bs], axis=0, keepdims=True)

    def near_scores(ib, qh):
        kbs = (k_ref[rows_of(ib - 1), :], k_ref[rows_of(ib), :])
        return [[lax.dot_general(kbs[w], qh[h], NT_DIMS, preferred_element_type=F32)
                 + near_ref[h, w * bs:(w + 1) * bs, :] for w in range(2)] for h in range(2)]

    def near_values(ib, slot, ss):
        ps, ms = [], []
        for h in range(2):
            s_prev, s_own = ss[h]
            keep = sel_ref[0, h, ib, pl.ds(ib - 1, 1), :] > 0.5
            mx = jnp.where(keep, jnp.max(s_prev, axis=0, keepdims=True), -jnp.inf)
            m_new = jnp.maximum(jnp.max(s_own, axis=0, keepdims=True), mx)
            ps.append((jnp.exp2(s_prev - jnp.where(keep, m_new, jnp.inf)).astype(BF16),
                       jnp.exp2(s_own - m_new).astype(BF16)))
            ms.append(m_new)
        for h in range(2):
            acc_ref[slot, h] = (jnp.dot(vt_ref[h, ib - 1], ps[h][0], preferred_element_type=F32)
                                + jnp.dot(vt_ref[h, ib], ps[h][1], preferred_element_type=F32))
            m_ref[slot, h] = ms[h]

    def far_group(g, ib, slot):
        for h in range(2):
            fb = far_ref[2 * hp + h]
            m_old = m_ref[slot, h]
            m_new = m_old
            keeps = []
            for j in range(FAR_GROUP):
                keep = sel_ref[1, h, ib, pl.ds(g * FAR_GROUP + j, 1), :] > 0.5
                m_new = jnp.maximum(m_new, jnp.where(keep, mx_ref[slot, g % 2, h, j] + fb, -jnp.inf))
                keeps.append(keep)
            p = jnp.concatenate(
                [jnp.exp2(s_ref[slot, g % 2, h, j * bs:(j + 1) * bs, :]
                          - jnp.where(keeps[j], m_new - fb, jnp.inf)).astype(BF16)
                 for j in range(FAR_GROUP)], axis=0)
            pv = jnp.dot(vtg_ref[h, g], p, preferred_element_type=F32)
            acc_ref[slot, h] = acc_ref[slot, h] * jnp.exp2(m_old - m_new) + pv
            m_ref[slot, h] = m_new

    def step_body(nf, blocks):
        qhs = {slot: split_heads(q_ref[rows_of(ib), :]) for ib, slot in blocks}
        near = {slot: near_scores(ib, qhs[slot]) for ib, slot in blocks}
        if nf > 0:
            for ib, slot in blocks:
                far_scores(0, slot, qhs[slot])
        for ib, slot in blocks:
            near_values(ib, slot, near[slot])
        for g in range(nf):
            if g + 1 < nf:
                for ib, slot in blocks:
                    far_scores(g + 1, slot, qhs[slot])
            for ib, slot in blocks:
                far_group(g, ib, slot)
        for ib, slot in blocks:
            finish(ib, slot)

    if i is None:
        qh = split_heads(q_ref[0:bs, :])
        kb = k_ref[0:bs, :]
        for h in range(2):
            s = lax.dot_general(kb, qh[h], NT_DIMS, preferred_element_type=F32) + near_ref[h, bs:2 * bs, :]
            p = jnp.exp2(s - jnp.max(s, axis=0, keepdims=True)).astype(BF16)
            acc_ref[0, h] = jnp.dot(vt_ref[h, 0], p, preferred_element_type=F32)
        finish(0, 0)
        step_body(0, [(1, 1)])
        return

    n_far = (i + FAR_GROUP - 2) // FAR_GROUP
    for nf in range(1, ngrp + 1):
        pl.when(n_far == nf)(functools.partial(step_body, nf, [(i, 0), (i + 1, 1)]))


def _moba(proj3, near, far, ow):
    b, s, _ = proj3.shape
    nblk = s // MOBA_BLOCK
    assert nblk % FAR_GROUP == 0
    npair = MOBA_HEADS // 2
    kcol = MOBA_WIDTH // LANES
    return pl.pallas_call(
        _moba_kernel,
        grid=(b, npair),
        in_specs=[pl.BlockSpec(memory_space=pltpu.SMEM),
                  pl.BlockSpec((None, s, LANES), lambda bb, hp: (bb, 0, hp)),
                  pl.BlockSpec((None, s, LANES), lambda bb, hp: (bb, 0, kcol + hp)),
                  pl.BlockSpec((None, s, LANES), lambda bb, hp: (bb, 0, 2 * kcol + hp)),
                  pl.BlockSpec((2, 2 * MOBA_BLOCK, MOBA_BLOCK), lambda bb, hp: (hp, 0, 0)),
                  pl.BlockSpec((1, LANES), lambda bb, hp: (0, 0))],
        out_specs=pl.BlockSpec((None, s, LANES), lambda bb, hp: (bb, 0, hp)),
        out_shape=jax.ShapeDtypeStruct((b, s, MOBA_WIDTH), BF16),
        scratch_shapes=[pltpu.VMEM((2, nblk, LANES, MOBA_BLOCK), BF16),
                        pltpu.VMEM((2, nblk // FAR_GROUP, LANES, FAR_GROUP * MOBA_BLOCK), BF16),
                        pltpu.VMEM((2, 2, LANES, MOBA_BLOCK), F32),
                        pltpu.VMEM((2, 2, 1, MOBA_BLOCK), F32),
                        pltpu.VMEM((2, 2, nblk, nblk, MOBA_BLOCK), F32),
                        pltpu.VMEM((2, 2, 2, FAR_GROUP * MOBA_BLOCK, MOBA_BLOCK), F32),
                        pltpu.VMEM((2, 2, 2, FAR_GROUP, 1, MOBA_BLOCK), F32)],
        compiler_params=_params(2),
        name="moba",
    )(far, proj3, proj3, proj3, near, ow)


def _split3(v):
    hi = v.astype(BF16)
    r1 = v - hi.astype(F32)
    mid = r1.astype(BF16)
    lo = (r1 - mid.astype(F32)).astype(BF16)
    return hi, mid, lo


GLA_UNROLL = 8


def _gla_kernel(q_ref, k_ref, v_ref, g_ref, ga_ref, wal_ref, bal_ref, gw_ref, o_ref, b_ref, st_ref):
    seq = q_ref.shape[0]
    c = GLA_CHUNK
    pc = 256

    rr = lax.broadcasted_iota(jnp.int32, (pc, pc), 0)
    cc = lax.broadcasted_iota(jnp.int32, (pc, pc), 1)
    tri = jnp.where((rr >= cc) & (rr // c == cc // c), 1.0, 0.0).astype(BF16)

    def decay_body(j, carry):
        rows = [pl.ds(pl.multiple_of((j * GLA_UNROLL + u) * pc, pc), pc) for u in range(GLA_UNROLL)]
        xg = [jnp.dot(ga_ref[r, :], wal_ref[...], preferred_element_type=F32) + bal_ref[...] for r in rows]
        parts = [_split3((jnp.minimum(x, 0.0) - jnp.log(1.0 + jnp.exp(-jnp.abs(x)))) * (1.0 / GLA_GATE_TAU))
                 for x in xg]
        sums = [[jnp.dot(tri, term, preferred_element_type=F32) for term in p] for p in parts]
        for r, (hi, mid, lo) in zip(rows, sums):
            b_ref[r, :] = hi + mid + lo
        return carry

    lax.fori_loop(0, seq // (pc * GLA_UNROLL), decay_body, 0)

    st_ref[...] = jnp.zeros_like(st_ref)
    lane = lax.broadcasted_iota(jnp.int32, (c, LANES), 1)
    head_mask = (lane < GLA_DK, lane >= GLA_DK)
    causal = lax.broadcasted_iota(jnp.int32, (c, c), 0) >= lax.broadcasted_iota(jnp.int32, (c, c), 1)

    units = [(u, h) for u in range(GLA_UNROLL) for h in range(2)]

    def chunk_body(ci, carry):
        rows = [pl.ds(pl.multiple_of((ci * GLA_UNROLL + u) * c, c), c) for u in range(GLA_UNROLL)]
        qt, kt, qs, ke, e_last = [], [], [], [], []
        for u in range(GLA_UNROLL):
            b = b_ref[rows[u], :]
            ref_row = b[c // 2 - 1:c // 2, :]
            last = b[c - 1:c, :]
            q = q_ref[rows[u], :].astype(F32) * (GLA_DK ** -0.5)
            k = k_ref[rows[u], :].astype(F32)
            qt.append(q * jnp.exp(b - ref_row))
            kt.append((k * jnp.exp(ref_row - b)).astype(BF16))
            qs.append(q * jnp.exp(b))
            ke.append((k * jnp.exp(last - b)).astype(BF16))
            e_last.append(jnp.exp(last))
        vs = {(u, h): v_ref[rows[u], h * GLA_DV:(h + 1) * GLA_DV] for u, h in units}
        a = {(u, h): lax.dot_general(jnp.where(head_mask[h], qt[u], 0.0).astype(BF16), kt[u], NT_DIMS,
                                     preferred_element_type=F32) for u, h in units}
        inc = {(u, h): lax.dot_general(vs[u, h], ke[u], TN_DIMS, preferred_element_type=F32) for u, h in units}
        o = {(u, h): jnp.dot(jnp.where(causal, a[u, h], 0.0).astype(BF16), vs[u, h], preferred_element_type=F32)
             for u, h in units}
        states = {}
        for h in range(2):
            st = st_ref[h]
            for u in range(GLA_UNROLL):
                states[u, h] = st
                st = st * e_last[u] + inc[u, h]
            st_ref[h] = st
        for u, h in units:
            cols = slice(h * GLA_DV, (h + 1) * GLA_DV)
            ou = o[u, h] + lax.dot_general(jnp.where(head_mask[h], qs[u], 0.0).astype(BF16),
                                           states[u, h].astype(BF16), NT_DIMS, preferred_element_type=F32)
            ms = jnp.mean(ou * ou, axis=-1, keepdims=True)
            on = ou * lax.rsqrt(ms + NORM_EPS) * gw_ref[...]
            g = g_ref[rows[u], cols].astype(F32)
            o_ref[rows[u], cols] = (on * _silu(g)).astype(o_ref.dtype)
        return carry

    lax.fori_loop(0, seq // (c * GLA_UNROLL), chunk_body, 0)


def _gla(proj3, ga3, wal, bal, gw):
    b, s, _ = proj3.shape
    npair = GLA_HEADS // 2
    qcol = 3 * MOBA_WIDTH // LANES
    kcol = qcol + GLA_KEY_WIDTH // LANES
    vcol = (3 * MOBA_WIDTH + 2 * GLA_KEY_WIDTH) // (2 * GLA_DV)
    gcol = vcol + npair
    return pl.pallas_call(
        _gla_kernel,
        grid=(b, npair),
        in_specs=[pl.BlockSpec((None, s, LANES), lambda bb, hp: (bb, 0, qcol + hp)),
                  pl.BlockSpec((None, s, LANES), lambda bb, hp: (bb, 0, kcol + hp)),
                  pl.BlockSpec((None, s, 2 * GLA_DV), lambda bb, hp: (bb, 0, vcol + hp)),
                  pl.BlockSpec((None, s, 2 * GLA_DV), lambda bb, hp: (bb, 0, gcol + hp)),
                  pl.BlockSpec((None, s, LANES), lambda bb, hp: (bb, 0, 0)),
                  pl.BlockSpec((LANES, LANES), lambda bb, hp: (0, hp)),
                  pl.BlockSpec((1, LANES), lambda bb, hp: (0, hp)),
                  pl.BlockSpec((1, GLA_DV), lambda bb, hp: (0, 0))],
        out_specs=pl.BlockSpec((None, s, 2 * GLA_DV), lambda bb, hp: (bb, 0, hp)),
        out_shape=jax.ShapeDtypeStruct((b, s, GLA_WIDTH), BF16),
        scratch_shapes=[pltpu.VMEM((s, LANES), F32),
                        pltpu.VMEM((2, GLA_DV, LANES), F32)],
        compiler_params=_params(2),
        name="gla",
    )(proj3, proj3, proj3, proj3, ga3, wal, bal, gw)


HALF = D_MODEL // 2


def _pack_rows(v):
    return pltpu.pack_elementwise([v[:, :HALF], v[:, HALF:]], packed_dtype=BF16)


def _unpack_rows(w):
    return (pltpu.unpack_elementwise(w, index=0, packed_dtype=BF16, unpacked_dtype=F32),
            pltpu.unpack_elementwise(w, index=1, packed_dtype=BF16, unpacked_dtype=F32))


def _outproj_kernel(oa_ref, ob_ref, x_ref, g1_ref, sc_ref, sh_ref, g2_ref, nw_ref, wo_ref,
                    ws1_ref, ws3_ref, ws2_ref, wrt_ref, base_ref, h_ref, st_ref):
    mix = (jnp.dot(oa_ref[...], wo_ref[:MOBA_WIDTH, :], preferred_element_type=F32)
           + jnp.dot(ob_ref[...], wo_ref[MOBA_WIDTH:, :], preferred_element_type=F32))
    x1 = x_ref[...] + g1_ref[...] * mix
    ms = jnp.mean(x1 * x1, axis=-1, keepdims=True)
    h = x1 * lax.rsqrt(ms + NORM_EPS) * nw_ref[...]
    h = h * (1.0 + sc_ref[...]) + sh_ref[...]
    h_ref[...] = _pack_rows(h)
    hb = h.astype(BF16)
    a = jnp.dot(hb, ws1_ref[...], preferred_element_type=F32)
    u = jnp.dot(hb, ws3_ref[...], preferred_element_type=F32)
    shared = jnp.dot((_silu(a) * u).astype(BF16), ws2_ref[...], preferred_element_type=F32)
    base_ref[...] = x1 + g2_ref[...] * shared
    logits_t = lax.dot_general(wrt_ref[...], hb, NT_DIMS, preferred_element_type=F32)
    st_ref[...] = jax.nn.sigmoid(logits_t)


def _outproj(oa, ob, x2, g1, sc, sh, g2, nw, wo, ws1, ws3, ws2, wrt, seq):
    t = x2.shape[0]
    tpb = seq // ROW_TILE
    vec = lambda: pl.BlockSpec((None, 1, D_MODEL), lambda i: (i // tpb, 0, 0))
    full = lambda a: pl.BlockSpec(a.shape, lambda i: (0,) * a.ndim)
    rows = lambda w: pl.BlockSpec((ROW_TILE, w), lambda i: (i, 0))
    return pl.pallas_call(
        _outproj_kernel,
        grid=(t // ROW_TILE,),
        in_specs=[rows(MOBA_WIDTH), rows(GLA_WIDTH), rows(D_MODEL), vec(), vec(), vec(), vec(),
                  full(nw), full(wo), full(ws1), full(ws3), full(ws2), full(wrt)],
        out_specs=[rows(D_MODEL), rows(HALF), pl.BlockSpec((N_EXPERTS, ROW_TILE), lambda i: (0, i))],
        out_shape=[jax.ShapeDtypeStruct((t, D_MODEL), F32),
                   jax.ShapeDtypeStruct((t, HALF), jnp.uint32),
                   jax.ShapeDtypeStruct((N_EXPERTS, t), F32)],
        compiler_params=_params(1),
        name="outproj",
    )(oa, ob, x2, g1, sc, sh, g2, nw, wo, ws1, ws3, ws2, wrt)


SLOT_CODE_SHIFT = 16
SLOT_CODE_BASE = 1 << SLOT_CODE_SHIFT


def _route_kernel(s_ref, eb_ref, code_ref, w_ref, cnt_ref, carry_ref):
    i = pl.program_id(0)
    ne, nt = s_ref.shape

    @pl.when(i == 0)
    def _init():
        carry_ref[...] = jnp.zeros_like(carry_ref)

    s = s_ref[...]
    choice = s + eb_ref[...]
    gio = lax.broadcasted_iota(jnp.int32, (GROUP_SIZE, nt), 0)
    gscore = []
    for g in range(N_GROUPS):
        cg = choice[g * GROUP_SIZE:(g + 1) * GROUP_SIZE, :]
        top1 = jnp.max(cg, axis=0, keepdims=True)
        first = jnp.min(jnp.where(cg == top1, gio, GROUP_SIZE), axis=0, keepdims=True)
        top2 = jnp.max(jnp.where(gio == first, -jnp.inf, cg), axis=0, keepdims=True)
        gscore.append(top1 + top2)
    gs = jnp.concatenate(gscore, axis=0)
    gidx = lax.broadcasted_iota(jnp.int32, gs.shape, 0)
    beaten = jnp.zeros(gs.shape, jnp.int32)
    for m in range(N_GROUPS):
        gm = gs[m:m + 1, :]
        beaten = beaten + jnp.where((gm > gs) | ((gm == gs) & (gidx > m)), 1, 0)
    gkeep = beaten < TOPK_GROUPS
    masked = jnp.concatenate(
        [jnp.where(gkeep[g:g + 1, :], choice[g * GROUP_SIZE:(g + 1) * GROUP_SIZE, :], -jnp.inf)
         for g in range(N_GROUPS)], axis=0)

    eio = lax.broadcasted_iota(jnp.int32, (ne, nt), 0)
    picked = jnp.zeros((ne, nt), F32)
    idx_rows, w_rows, hits = [], [], []
    for _ in range(TOP_K):
        mx = jnp.max(masked, axis=0, keepdims=True)
        idx = jnp.min(jnp.where(masked == mx, eio, ne), axis=0, keepdims=True)
        hit = eio == idx
        w_rows.append(jnp.sum(jnp.where(hit, s, 0.0), axis=0, keepdims=True))
        idx_rows.append(idx)
        hits.append(hit)
        masked = jnp.where(hit, -jnp.inf, masked)
        picked = jnp.where(hit, 1.0, picked)
    wk = jnp.concatenate(w_rows, axis=0)
    w_ref[...] = wk / jnp.sum(wk, axis=0, keepdims=True) * ROUTED_SCALE

    tr = lax.broadcasted_iota(jnp.int32, (nt, nt), 0)
    tc = lax.broadcasted_iota(jnp.int32, (nt, nt), 1)
    before = jnp.where(tr < tc, 1.0, 0.0).astype(BF16)
    pb = picked.astype(BF16)
    pos = carry_ref[...] + jnp.dot(pb, before, preferred_element_type=F32)
    rank = jnp.concatenate(
        [jnp.sum(jnp.where(hit, pos, 0.0), axis=0, keepdims=True) for hit in hits], axis=0).astype(jnp.int32)
    code_ref[...] = jnp.concatenate(idx_rows, axis=0) * SLOT_CODE_BASE + rank
    total = carry_ref[...] + jnp.dot(pb, jnp.ones((nt, nt), BF16), preferred_element_type=F32)
    carry_ref[...] = total
    cnt_ref[...] = total


def _route(scores_t, eb):
    ne, t = scores_t.shape
    assert t <= SLOT_CODE_BASE
    nt = ROUTE_TILE
    tok = lambda dt: jax.ShapeDtypeStruct((TOP_K, t), dt)
    return pl.pallas_call(
        _route_kernel,
        grid=(t // nt,),
        in_specs=[pl.BlockSpec((ne, nt), lambda i: (0, i)),
                  pl.BlockSpec((ne, nt), lambda i: (0, 0))],
        out_specs=[pl.BlockSpec((TOP_K, nt), lambda i: (0, i)),
                   pl.BlockSpec((TOP_K, nt), lambda i: (0, i)),
                   pl.BlockSpec((ne, nt), lambda i: (0, 0))],
        out_shape=[tok(jnp.int32), tok(F32), jax.ShapeDtypeStruct((ne, nt), F32)],
        scratch_shapes=[pltpu.VMEM((ne, nt), F32)],
        compiler_params=_params(1),
        name="route",
    )(scores_t, eb)


SLOT_TILE = 2048


def _slots_kernel(pstart_ref, code_ref, o_ref):
    code = code_ref[...]
    expert = lax.shift_right_logical(code, SLOT_CODE_SHIFT)

    def body(e, acc):
        return jnp.where(expert == e, pstart_ref[e], acc)

    start = lax.fori_loop(0, N_EXPERTS, body, jnp.zeros_like(code), unroll=8)
    o_ref[...] = start + (code & (SLOT_CODE_BASE - 1))


def _slots(pstart, code_t):
    k, t = code_t.shape
    return pl.pallas_call(
        _slots_kernel,
        grid_spec=pltpu.PrefetchScalarGridSpec(
            num_scalar_prefetch=1,
            grid=(t // SLOT_TILE,),
            in_specs=[pl.BlockSpec((k, SLOT_TILE), lambda i, p: (0, i))],
            out_specs=pl.BlockSpec((k, SLOT_TILE), lambda i, p: (0, i)),
        ),
        out_shape=jax.ShapeDtypeStruct((k, t), jnp.int32),
        compiler_params=_params(1),
        name="slots",
    )(pstart, code_t)


SC_WINDOW = 128


def _sc_gather_rows(table, idx_flat):
    from jax.experimental.pallas import tpu_sc as plsc
    info = plsc.get_sparse_core_info()
    nw = info.num_cores * info.num_subcores
    n = idx_flat.shape[0]
    width = table.shape[1]
    per_worker = n // nw
    assert per_worker * nw == n and per_worker % SC_WINDOW == 0
    mesh = plsc.VectorSubcoreMesh(core_axis_name="c", subcore_axis_name="s")

    def body(table_hbm, idx_hbm, out_hbm, idx_v, rows_v, sem):
        wid = lax.axis_index("s") * info.num_cores + lax.axis_index("c")
        base = wid * per_worker

        @pl.loop(0, per_worker // SC_WINDOW)
        def _(w):
            off = pl.multiple_of(base + w * SC_WINDOW, SC_WINDOW)
            pltpu.sync_copy(idx_hbm.at[pl.ds(off, SC_WINDOW)], idx_v)
            pltpu.async_copy(table_hbm.at[idx_v], rows_v, sem).wait()
            pltpu.sync_copy(rows_v, out_hbm.at[pl.ds(off, SC_WINDOW)])

    return pl.kernel(
        body,
        out_type=jax.ShapeDtypeStruct((n, width), table.dtype),
        mesh=mesh,
        scratch_types=[pltpu.VMEM((SC_WINDOW,), jnp.int32),
                       pltpu.VMEM((SC_WINDOW, width), table.dtype),
                       pltpu.SemaphoreType.DMA],
        name="sc_gather",
    )(table, idx_flat)


def _sc_scatter_rows(rows, idx_kt, n_out):
    from jax.experimental.pallas import tpu_sc as plsc
    info = plsc.get_sparse_core_info()
    nw = info.num_cores * info.num_subcores
    t, width = rows.shape
    nk = idx_kt.shape[0]
    per_worker = t // nw
    assert per_worker * nw == t and per_worker % SC_WINDOW == 0
    mesh = plsc.VectorSubcoreMesh(core_axis_name="c", subcore_axis_name="s")

    def body(rows_hbm, idx_hbm, out_hbm, idx_v, rows_v, sem):
        wid = lax.axis_index("s") * info.num_cores + lax.axis_index("c")
        base = wid * per_worker

        @pl.loop(0, per_worker // SC_WINDOW)
        def _(w):
            off = pl.multiple_of(base + w * SC_WINDOW, SC_WINDOW)
            pltpu.sync_copy(rows_hbm.at[pl.ds(off, SC_WINDOW)], rows_v)
            pltpu.sync_copy(idx_hbm.at[:, pl.ds(off, SC_WINDOW)], idx_v)
            copies = [pltpu.async_copy(rows_v, out_hbm.at[idx_v.at[k]], sem) for k in range(nk)]
            for cp in copies:
                cp.wait()

    return pl.kernel(
        body,
        out_type=jax.ShapeDtypeStruct((n_out, width), rows.dtype),
        mesh=mesh,
        scratch_types=[pltpu.VMEM((nk, SC_WINDOW), jnp.int32),
                       pltpu.VMEM((SC_WINDOW, width), rows.dtype),
                       pltpu.SemaphoreType.DMA],
        name="sc_scatter",
    )(rows, idx_kt)


def _expert_kernel(first_ref, count_ref, used_ref, w1_ref, w3_ref, w2_ref, xs_ref, ys_ref,
                   xbuf, ybuf, sem_in, sem_out, w1f, w3f, w2f, sem_w, w1b, w3b, w2b, *, layer):
    e = pl.program_id(0)
    ne = pl.num_programs(0)
    r = xbuf.shape[1]
    n_used = used_ref[0]

    def x_copy(g, slot):
        return pltpu.make_async_copy(xs_ref.at[pl.ds(pl.multiple_of(g * r, r), r), :], xbuf.at[slot], sem_in.at[slot])

    def y_copy(g, slot):
        return pltpu.make_async_copy(ybuf.at[slot], ys_ref.at[pl.ds(pl.multiple_of(g * r, r), r), :], sem_out.at[slot])

    def w_copies(ex, slot):
        return [pltpu.make_async_copy(src.at[layer, ex], dst.at[slot], sem_w.at[slot])
                for src, dst in ((w1_ref, w1f), (w3_ref, w3f), (w2_ref, w2f))]

    nin = xbuf.shape[0]
    nout = ybuf.shape[0]

    @pl.when(e == 0)
    def _first_reads():
        for g in range(nin):
            @pl.when(g < n_used)
            def _(g=g):
                x_copy(g, g).start()
        for cp in w_copies(0, 0):
            cp.start(priority=1)

    @pl.when(e + 1 < ne)
    def _next_weights():
        for cp in w_copies(e + 1, (e + 1) & 1):
            cp.start(priority=1)

    for cp in w_copies(e, e & 1):
        cp.wait()

    n = count_ref[e]

    @pl.when(n > 0)
    def _cast_weights():
        w1b[...] = w1f[e & 1].astype(BF16)
        w3b[...] = w3f[e & 1].astype(BF16)
        w2b[...] = w2f[e & 1].astype(BF16)

    def run_tiles(g, count):
        tiles = [g + j for j in range(count)]
        for gj in tiles:
            x_copy(gj, gj % nin).wait()
        x = jnp.concatenate([xbuf[gj % nin] for gj in tiles], axis=0)
        for gj in tiles:
            @pl.when(gj + nin < n_used)
            def _(gj=gj):
                x_copy(gj + nin, gj % nin).start()
        lo, hi = _unpack_rows(x)
        lo, hi = lo.astype(BF16), hi.astype(BF16)
        a = (jnp.dot(lo, w1b[:HALF, :], preferred_element_type=F32)
             + jnp.dot(hi, w1b[HALF:, :], preferred_element_type=F32))
        u = (jnp.dot(lo, w3b[:HALF, :], preferred_element_type=F32)
             + jnp.dot(hi, w3b[HALF:, :], preferred_element_type=F32))
        y = _pack_rows(jnp.dot((_silu(a) * u).astype(BF16), w2b[...], preferred_element_type=F32))
        for j, gj in enumerate(tiles):
            @pl.when(gj >= nout)
            def _(gj=gj):
                y_copy(gj - nout, gj % nout).wait()
            ybuf[gj % nout] = y[j * r:(j + 1) * r]
            y_copy(gj, gj % nout).start()

    g0 = first_ref[e]
    pair = EXPERT_TILES_PER_MATMUL

    def pair_body(p, carry):
        run_tiles(g0 + p * pair, pair)
        return carry

    lax.fori_loop(0, n // pair, pair_body, 0)
    for left in range(1, pair):
        pl.when(n % pair == left)(functools.partial(run_tiles, g0 + n - left, left))

    @pl.when(e == ne - 1)
    def _drain_writes():
        for back in range(nout, 0, -1):
            @pl.when(n_used >= back)
            def _(back=back):
                y_copy(n_used - back, (n_used - back) % nout).wait()


def _experts(tile_first, tile_count, n_used, xs, w1, w3, w2, layer):
    n_rows = xs.shape[0]
    r = EXPERT_TILE
    any_spec = pl.BlockSpec(memory_space=pl.ANY)
    return pl.pallas_call(
        functools.partial(_expert_kernel, layer=layer),
        grid_spec=pltpu.PrefetchScalarGridSpec(
            num_scalar_prefetch=3,
            grid=(N_EXPERTS,),
            in_specs=[any_spec, any_spec, any_spec, any_spec],
            out_specs=any_spec,
            scratch_shapes=[pltpu.VMEM((EXPERT_IN_RING, r, HALF), jnp.uint32),
                            pltpu.VMEM((EXPERT_OUT_RING, r, HALF), jnp.uint32),
                            pltpu.SemaphoreType.DMA((EXPERT_IN_RING,)),
                            pltpu.SemaphoreType.DMA((EXPERT_OUT_RING,)),
                            pltpu.VMEM((2, D_MODEL, EXPERT_FF), F32),
                            pltpu.VMEM((2, D_MODEL, EXPERT_FF), F32),
                            pltpu.VMEM((2, EXPERT_FF, D_MODEL), F32),
                            pltpu.SemaphoreType.DMA((2,)),
                            pltpu.VMEM((D_MODEL, EXPERT_FF), BF16),
                            pltpu.VMEM((D_MODEL, EXPERT_FF), BF16),
                            pltpu.VMEM((EXPERT_FF, D_MODEL), BF16)],
        ),
        out_shape=jax.ShapeDtypeStruct((n_rows, HALF), jnp.uint32),
        compiler_params=_params(1),
        name="experts",
    )(tile_first, tile_count, n_used, w1, w3, w2, xs)


def _combine_dense_kernel(base_ref, g2_ref, w_ref, yg_ref, o_ref):
    acc_lo = acc_hi = None
    for k in range(TOP_K):
        lo, hi = _unpack_rows(yg_ref[k])
        wk = w_ref[:, k:k + 1]
        acc_lo = wk * lo if acc_lo is None else acc_lo + wk * lo
        acc_hi = wk * hi if acc_hi is None else acc_hi + wk * hi
    o_ref[:, :HALF] = base_ref[:, :HALF] + g2_ref[:, :HALF] * acc_lo
    o_ref[:, HALF:] = base_ref[:, HALF:] + g2_ref[:, HALF:] * acc_hi


def _combine_dense(base, g2, w_tok, yg, batch):
    t = base.shape[0]
    seq = yg.shape[1]
    nt = ROUTE_TILE
    tpb = seq // nt
    rows = lambda i: (batch * tpb + i, 0)
    return pl.pallas_call(
        _combine_dense_kernel,
        grid=(tpb,),
        in_specs=[pl.BlockSpec((nt, D_MODEL), rows),
                  pl.BlockSpec((None, 1, D_MODEL), lambda i: (batch, 0, 0)),
                  pl.BlockSpec((nt, TOP_K), rows),
                  pl.BlockSpec((TOP_K, nt, HALF), lambda i: (0, i, 0))],
        out_specs=pl.BlockSpec((nt, D_MODEL), rows),
        out_shape=jax.ShapeDtypeStruct((t, D_MODEL), F32),
        input_output_aliases={0: 0},
        compiler_params=_params(1),
        name="combine_dense",
    )(base, g2, w_tok, yg)


def _layer(layer, x, c, w_ada, b_ada, norm1_w, norm2_w, w_in, q_norm_w, k_norm_w, rel_bias, w_alpha, b_alpha,
           moba_out_w, gla_out_w, w_out, w_router, e_bias, w1, w3, w2, ws1, ws3, ws2):
    b, s, d = x.shape
    t = b * s
    x2 = x.reshape(t, d)

    mod = _mod(c, w_ada, b_ada)
    sh1, sc1, g1, sh2, sc2, g2 = [mod[:, j * d:(j + 1) * d].reshape(b, 1, d) for j in range(6)]

    w_main = w_in[:, :D_MAIN].astype(BF16)
    w_ga = jnp.zeros((d, LANES), BF16).at[:, :GLA_GATE_RANK].set(w_in[:, D_MAIN:].astype(BF16))
    per_chunk = 256 // MOBA_HEAD_DIM
    qw = jnp.tile(q_norm_w.astype(F32), per_chunk).reshape(1, 256) * (MOBA_HEAD_DIM ** -0.5 * LOG2E)
    kw = jnp.tile(k_norm_w.astype(F32), per_chunk).reshape(1, 256)
    proj, ga = _inproj(x2, sc1, sh1, norm1_w.reshape(1, d), w_main, w_ga, qw, kw, s)
    proj3 = proj.reshape(b, s, D_MAIN)

    near, far = _moba_bias_tables(rel_bias)
    ow = jnp.tile(moba_out_w.astype(F32), 2).reshape(1, LANES)
    o_a = _moba(proj3, near, far, ow)

    wal = jnp.zeros((LANES, GLA_KEY_WIDTH), F32).at[:GLA_GATE_RANK].set(w_alpha)
    o_b = _gla(proj3, ga.reshape(b, s, LANES), wal, b_alpha.reshape(1, GLA_KEY_WIDTH),
               gla_out_w.reshape(1, GLA_DV))

    base, h2, scores_t = _outproj(
        o_a.reshape(t, MOBA_WIDTH), o_b.reshape(t, GLA_WIDTH), x2, g1, sc2, sh2, g2,
        norm2_w.reshape(1, d), w_out.astype(BF16), ws1.astype(BF16), ws3.astype(BF16), ws2.astype(BF16),
        w_router.T.astype(BF16), s)

    eb = jnp.broadcast_to(e_bias.astype(F32)[:, None], (N_EXPERTS, ROUTE_TILE))
    code_t, w_t, counts = _route(scores_t, eb)

    r = EXPERT_TILE
    n_tiles = (t * TOP_K + N_EXPERTS * (r - 1) + r - 1) // r
    n_rows = n_tiles * r
    cnt = counts[:, 0].astype(jnp.int32)
    padded = (cnt + r - 1) // r * r
    pend = jnp.cumsum(padded)
    pstart = pend - padded
    n_used = (pend[-1:] // r).astype(jnp.int32)
    dest_t = _slots(pstart, code_t)

    xs = _sc_scatter_rows(h2, dest_t, n_rows)
    ys = _experts(pstart // r, padded // r, n_used, xs, w1, w3, w2, layer)
    w_tok = w_t.T
    out = base
    for bi in range(b):
        idx = dest_t[:, bi * s:(bi + 1) * s].reshape(TOP_K * s)
        yg = _sc_gather_rows(ys, idx).reshape(TOP_K, s, HALF)
        out = _combine_dense(out, g2, w_tok, yg, bi)
    return out.reshape(b, s, d)


def kernel(x, c, w_ada, b_ada, norm1_w, norm2_w, w_in, q_norm_w, k_norm_w, rel_bias, w_alpha, b_alpha,
           moba_out_w, gla_out_w, w_out, w_router, e_bias, w1, w3, w2, ws1, ws3, ws2):
    for l in range(w_ada.shape[0]):
        x = _layer(l, x, c, w_ada[l], b_ada[l], norm1_w[l], norm2_w[l], w_in[l], q_norm_w[l], k_norm_w[l],
                   rel_bias, w_alpha[l], b_alpha[l], moba_out_w[l], gla_out_w[l], w_out[l], w_router[l],
                   e_bias[l], w1, w3, w2, ws1[l], ws3[l], ws2[l])
    return x
```

```python
import functools
import math

import numpy as np
import jax
import jax.numpy as jnp
from jax import lax
from jax.experimental import pallas as pl
from jax.experimental.pallas import tpu as pltpu
from jax.experimental.pallas import tpu_sc as plsc

D_MODEL = 1024
MOBA_HEADS = 8
MOBA_HEAD_DIM = 64
MOBA_WIDTH = MOBA_HEADS * MOBA_HEAD_DIM
MOBA_BLOCK = 256
MOBA_TOPK = 3
GLA_HEADS = 4
GLA_DK = 64
GLA_DV = 128
GLA_KEY_WIDTH = GLA_HEADS * GLA_DK
GLA_WIDTH = GLA_HEADS * GLA_DV
GLA_GATE_RANK = 16
GLA_GATE_TAU = 16.0
GLA_CHUNK = 64
REL_BUCKETS = 32
REL_MAX_DIST = 128
N_EXPERTS = 256
TOP_K = 8
N_GROUPS = 8
TOPK_GROUPS = 4
GROUP_SIZE = N_EXPERTS // N_GROUPS
EXPERT_FF = 256
SHARED_FF = 256
ROUTED_SCALE = 2.5
NORM_EPS = 1e-6
LOG2E = math.log2(math.e)

D_MAIN = 3 * MOBA_WIDTH + 2 * GLA_KEY_WIDTH + 2 * GLA_WIDTH
LANES = 128
VMEM_LIMIT = 56 * 1024 * 1024

ROW_TILE = 512
ROUTE_TILE = 256
EXPERT_TILE = 128
EXPERT_TILES_PER_MATMUL = 6
EXPERT_IN_RING = 16
EXPERT_OUT_RING = 12

F32 = jnp.float32
BF16 = jnp.bfloat16
NT_DIMS = (((1,), (1,)), ((), ()))
TN_DIMS = (((0,), (0,)), ((), ()))


def _params(n_axes):
    return pltpu.CompilerParams(dimension_semantics=("arbitrary",) * n_axes,
                                vmem_limit_bytes=VMEM_LIMIT)


def _silu(v):
    return v * jax.nn.sigmoid(v)


def _mod_kernel(c_ref, w_ref, b_ref, o_ref):
    o_ref[...] = jnp.dot(_silu(c_ref[...]), w_ref[...], preferred_element_type=F32) + b_ref[...]


def _mod(c, w, b):
    rows = 8
    cp = jnp.zeros((rows, D_MODEL), F32).at[:c.shape[0]].set(c)
    n = w.shape[1]
    tn = 1024
    out = pl.pallas_call(
        _mod_kernel,
        grid=(n // tn,),
        in_specs=[pl.BlockSpec((rows, D_MODEL), lambda j: (0, 0)),
                  pl.BlockSpec((D_MODEL, tn), lambda j: (0, j)),
                  pl.BlockSpec((1, tn), lambda j: (0, j))],
        out_specs=pl.BlockSpec((rows, tn), lambda j: (0, j)),
        out_shape=jax.ShapeDtypeStruct((rows, n), F32),
        compiler_params=_params(1),
        name="mod",
    )(cp, w, b.reshape(1, n))
    return out[:c.shape[0]]


def _group_rms_inv(a, group):
    lane = lax.broadcasted_iota(jnp.int32, (1, a.shape[1]), 1)
    a2 = a * a
    inv = jnp.zeros_like(a)
    for g in range(a.shape[1] // group):
        m = (lane >= g * group) & (lane < (g + 1) * group)
        ss = jnp.sum(jnp.where(m, a2, 0.0), axis=-1, keepdims=True)
        inv = jnp.where(m, lax.rsqrt(ss * (1.0 / group) + NORM_EPS), inv)
    return inv


def _inproj_kernel(x_ref, sc_ref, sh_ref, nw_ref, w_ref, wga_ref, qw_ref, kw_ref, o_ref, ga_ref):
    x = x_ref[...]
    ms = jnp.mean(x * x, axis=-1, keepdims=True)
    h = x * lax.rsqrt(ms + NORM_EPS) * nw_ref[...]
    h = h * (1.0 + sc_ref[...]) + sh_ref[...]
    hb = h.astype(BF16)
    cw = 256
    for j in range(D_MAIN // cw):
        acc = jnp.dot(hb, w_ref[:, j * cw:(j + 1) * cw], preferred_element_type=F32)
        if j < 2 * MOBA_WIDTH // cw:
            nw = qw_ref if j < MOBA_WIDTH // cw else kw_ref
            acc = acc * _group_rms_inv(acc, MOBA_HEAD_DIM) * nw[...]
        o_ref[:, j * cw:(j + 1) * cw] = acc.astype(BF16)
    ga_ref[...] = jnp.dot(hb, wga_ref[...], preferred_element_type=F32)


def _inproj(x2, sc, sh, nw, w_main, w_ga, qw, kw, seq):
    t = x2.shape[0]
    tpb = seq // ROW_TILE
    vec = lambda: pl.BlockSpec((None, 1, D_MODEL), lambda i: (i // tpb, 0, 0))
    full = lambda a: pl.BlockSpec(a.shape, lambda i: (0,) * a.ndim)
    return pl.pallas_call(
        _inproj_kernel,
        grid=(t // ROW_TILE,),
        in_specs=[pl.BlockSpec((ROW_TILE, D_MODEL), lambda i: (i, 0)), vec(), vec(),
                  full(nw), full(w_main), full(w_ga), full(qw), full(kw)],
        out_specs=[pl.BlockSpec((ROW_TILE, D_MAIN), lambda i: (i, 0)),
                   pl.BlockSpec((ROW_TILE, LANES), lambda i: (i, 0))],
        out_shape=[jax.ShapeDtypeStruct((t, D_MAIN), BF16),
                   jax.ShapeDtypeStruct((t, LANES), F32)],
        compiler_params=_params(1),
        name="inproj",
    )(x2, sc, sh, nw, w_main, w_ga, qw, kw)


def _t5_bucket_np(rel):
    max_exact = REL_BUCKETS // 2
    relf = np.maximum(rel, 1).astype(np.float64)
    large = max_exact + (np.log(relf / max_exact) / math.log(REL_MAX_DIST / max_exact)
                         * (REL_BUCKETS - max_exact)).astype(np.int32)
    large = np.minimum(large, REL_BUCKETS - 1)
    return np.where(rel < max_exact, rel, large)


def _bias_kernel(rb_ref, idx_ref, o_ref):
    h = pl.program_id(0)
    idx = idx_ref[...]
    tab = jnp.full(idx.shape, -jnp.inf, F32)
    for bk in range(REL_BUCKETS):
        tab = jnp.where(idx == bk, rb_ref[bk * MOBA_HEADS + h], tab)
    o_ref[...] = tab


def _moba_bias_tables(rel_bias):
    j = np.arange(MOBA_BLOCK)[:, None]
    i = np.arange(MOBA_BLOCK)[None, :]
    own_idx = np.where(j <= i, _t5_bucket_np(np.maximum(i - j, 0)), -1)
    prev_idx = _t5_bucket_np(MOBA_BLOCK + i - j)
    idx = jnp.asarray(np.concatenate([prev_idx, own_idx], axis=0).astype(np.int32))
    assert int(_t5_bucket_np(np.array([MOBA_BLOCK + 1]))[0]) == REL_BUCKETS - 1
    rb = rel_bias.astype(F32) * LOG2E
    near = pl.pallas_call(
        _bias_kernel,
        grid=(MOBA_HEADS,),
        in_specs=[pl.BlockSpec(memory_space=pltpu.SMEM),
                  pl.BlockSpec(idx.shape, lambda h: (0, 0))],
        out_specs=pl.BlockSpec((None,) + idx.shape, lambda h: (h, 0, 0)),
        out_shape=jax.ShapeDtypeStruct((MOBA_HEADS,) + idx.shape, F32),
        compiler_params=_params(1),
        name="bias",
    )(rb.reshape(-1), idx)
    return near, rb[REL_BUCKETS - 1]


FAR_GROUP = 4


def _moba_kernel(*refs):
    hp = pl.program_id(1)
    _moba_body(None, hp, *refs, prepare=True)
    _moba_body(None, hp, *refs, prepare=False)

    def block_pair(j, carry):
        _moba_body(2 * j, hp, *refs, prepare=False)
        return carry

    lax.fori_loop(1, refs[2].shape[0] // (2 * MOBA_BLOCK), block_pair, 0)


def _moba_body(i, hp, far_ref, q_ref, k_ref, v_ref, near_ref, ow_ref, o_ref,
               vt_ref, vtg_ref, acc_ref, m_ref, sel_ref, s_ref, mx_ref, *, prepare):
    nblk = k_ref.shape[0] // MOBA_BLOCK
    ngrp = nblk // FAR_GROUP
    hd = MOBA_HEAD_DIM
    bs = MOBA_BLOCK
    lane = lax.broadcasted_iota(jnp.int32, (bs, LANES), 1)

    def split_heads(qb):
        zero = jnp.zeros_like(qb)
        return jnp.where(lane < hd, qb, zero), jnp.where(lane < hd, zero, qb)

    def _prepare():
        row = lax.broadcasted_iota(jnp.int32, (LANES, bs), 0)
        kmeans = []
        for n in range(nblk):
            kb = k_ref[n * bs:(n + 1) * bs, :].astype(F32)
            kmeans.append(jnp.mean(kb, axis=0, keepdims=True))
            vt = v_ref[n * bs:(n + 1) * bs, :].astype(F32).T
            vt0 = jnp.where(row < hd, vt, 1.0).astype(BF16)
            vt1 = jnp.where(row < hd, 1.0, vt).astype(BF16)
            vt_ref[0, n] = vt0
            vt_ref[1, n] = vt1
            gcols = slice((n % FAR_GROUP) * bs, (n % FAR_GROUP + 1) * bs)
            vtg_ref[0, n // FAR_GROUP, :, gcols] = vt0
            vtg_ref[1, n // FAR_GROUP, :, gcols] = vt1
        kmean = jnp.concatenate(kmeans, axis=0)
        km_hi = kmean.astype(BF16)
        km_lo = (kmean - km_hi.astype(F32)).astype(BF16)
        blk = lax.broadcasted_iota(jnp.int32, (nblk, bs), 0)
        for ib in range(nblk):
            qparts = split_heads(q_ref[ib * bs:(ib + 1) * bs, :])
            for h in range(2):
                gt = (lax.dot_general(km_hi, qparts[h], NT_DIMS, preferred_element_type=F32)
                      + lax.dot_general(km_lo, qparts[h], NT_DIMS, preferred_element_type=F32))
                gt = jnp.where(blk < ib, gt, -jnp.inf)
                cnt = jnp.zeros(gt.shape, jnp.int32)
                for m in range(ib):
                    gm = gt[m:m + 1, :]
                    cnt = cnt + jnp.where((gm > gt) | ((gm == gt) & (blk > m)), 1, 0)
                keep = (blk < ib) & (cnt < MOBA_TOPK)
                sel_ref[0, h, ib] = jnp.where(keep, 1.0, 0.0)
                sel_ref[1, h, ib] = jnp.where(keep & (blk < ib - 1), 1.0, 0.0)

    if prepare:
        _prepare()
        return

    gk = FAR_GROUP * bs

    def rows_of(ib):
        if isinstance(ib, int):
            return slice(ib * bs, (ib + 1) * bs)
        return pl.ds(pl.multiple_of(ib * bs, bs), bs)

    def finish(ib, slot):
        a0 = acc_ref[slot, 0]
        a1 = acc_ref[slot, 1]
        row = lax.broadcasted_iota(jnp.int32, a0.shape, 0)
        ot = jnp.where(row < hd, a0 / a0[hd:hd + 1, :], a1 / a1[0:1, :])
        o2 = ot * ot
        ss0 = jnp.sum(jnp.where(row < hd, o2, 0.0), axis=0, keepdims=True)
        ss1 = jnp.sum(jnp.where(row < hd, 0.0, o2), axis=0, keepdims=True)
        inv = jnp.where(row < hd, lax.rsqrt(ss0 * (1.0 / hd) + NORM_EPS), lax.rsqrt(ss1 * (1.0 / hd) + NORM_EPS))
        o_ref[rows_of(ib), :] = ((ot * inv).T * ow_ref[...]).astype(o_ref.dtype)

    def far_scores(g, slot, qh):
        kb = k_ref[g * gk:(g + 1) * gk, :]
        for h in range(2):
            s = lax.dot_general(kb, qh[h], NT_DIMS, preferred_element_type=F32)
            s_ref[slot, g % 2, h] = s
            for j in range(FAR_GROUP):
                mx_ref[slot, g % 2, h, j] = jnp.max(s[j * bs:(j + 1) * bs], axis=0, keepdims=True)

    def near_scores(ib, qh):
        kbs = (k_ref[rows_of(ib - 1), :], k_ref[rows_of(ib), :])
        return [[lax.dot_general(kbs[w], qh[h], NT_DIMS, preferred_element_type=F32)
                 + near_ref[h, w * bs:(w + 1) * bs, :] for w in range(2)] for h in range(2)]

    def near_values(ib, slot, ss):
        ps, ms = [], []
        for h in range(2):
            s_prev, s_own = ss[h]
            keep = sel_ref[0, h, ib, pl.ds(ib - 1, 1), :] > 0.5
            mx = jnp.where(keep, jnp.max(s_prev, axis=0, keepdims=True), -jnp.inf)
            m_new = jnp.maximum(jnp.max(s_own, axis=0, keepdims=True), mx)
            ps.append((jnp.exp2(s_prev - jnp.where(keep, m_new, jnp.inf)).astype(BF16),
                       jnp.exp2(s_own - m_new).astype(BF16)))
            ms.append(m_new)
        for h in range(2):
            acc_ref[slot, h] = (jnp.dot(vt_ref[h, ib - 1], ps[h][0], preferred_element_type=F32)
                                + jnp.dot(vt_ref[h, ib], ps[h][1], preferred_element_type=F32))
            m_ref[slot, h] = ms[h]

    def far_group(g, ib, slot):
        for h in range(2):
            fb = far_ref[2 * hp + h]
            m_old = m_ref[slot, h]
            m_new = m_old
            keeps = []
            for j in range(FAR_GROUP):
                keep = sel_ref[1, h, ib, pl.ds(g * FAR_GROUP + j, 1), :] > 0.5
                m_new = jnp.maximum(m_new, jnp.where(keep, mx_ref[slot, g % 2, h, j] + fb, -jnp.inf))
                keeps.append(keep)
            p = jnp.concatenate(
                [jnp.exp2(s_ref[slot, g % 2, h, j * bs:(j + 1) * bs, :]
                          - jnp.where(keeps[j], m_new - fb, jnp.inf)).astype(BF16)
                 for j in range(FAR_GROUP)], axis=0)
            pv = jnp.dot(vtg_ref[h, g], p, preferred_element_type=F32)
            acc_ref[slot, h] = acc_ref[slot, h] * jnp.exp2(m_old - m_new) + pv
            m_ref[slot, h] = m_new

    def step_body(nf, blocks):
        qhs = {slot: split_heads(q_ref[rows_of(ib), :]) for ib, slot in blocks}
        near = {slot: near_scores(ib, qhs[slot]) for ib, slot in blocks}
        if nf > 0:
            for ib, slot in blocks:
                far_scores(0, slot, qhs[slot])
        for ib, slot in blocks:
            near_values(ib, slot, near[slot])
        for g in range(nf):
            if g + 1 < nf:
                for ib, slot in blocks:
                    far_scores(g + 1, slot, qhs[slot])
            for ib, slot in blocks:
                far_group(g, ib, slot)
        for ib, slot in blocks:
            finish(ib, slot)

    if i is None:
        qh = split_heads(q_ref[0:bs, :])
        kb = k_ref[0:bs, :]
        for h in range(2):
            s = lax.dot_general(kb, qh[h], NT_DIMS, preferred_element_type=F32) + near_ref[h, bs:2 * bs, :]
            p = jnp.exp2(s - jnp.max(s, axis=0, keepdims=True)).astype(BF16)
            acc_ref[0, h] = jnp.dot(vt_ref[h, 0], p, preferred_element_type=F32)
        finish(0, 0)
        step_body(0, [(1, 1)])
        return

    n_far = (i + FAR_GROUP - 2) // FAR_GROUP
    for nf in range(1, ngrp + 1):
        pl.when(n_far == nf)(functools.partial(step_body, nf, [(i, 0), (i + 1, 1)]))


def _moba(proj3, near, far, ow):
    b, s, _ = proj3.shape
    nblk = s // MOBA_BLOCK
    assert nblk % FAR_GROUP == 0
    npair = MOBA_HEADS // 2
    kcol = MOBA_WIDTH // LANES
    return pl.pallas_call(
        _moba_kernel,
        grid=(b, npair),
        in_specs=[pl.BlockSpec(memory_space=pltpu.SMEM),
                  pl.BlockSpec((None, s, LANES), lambda bb, hp: (bb, 0, hp)),
                  pl.BlockSpec((None, s, LANES), lambda bb, hp: (bb, 0, kcol + hp)),
                  pl.BlockSpec((None, s, LANES), lambda bb, hp: (bb, 0, 2 * kcol + hp)),
                  pl.BlockSpec((2, 2 * MOBA_BLOCK, MOBA_BLOCK), lambda bb, hp: (hp, 0, 0)),
                  pl.BlockSpec((1, LANES), lambda bb, hp: (0, 0))],
        out_specs=pl.BlockSpec((None, s, LANES), lambda bb, hp: (bb, 0, hp)),
        out_shape=jax.ShapeDtypeStruct((b, s, MOBA_WIDTH), BF16),
        scratch_shapes=[pltpu.VMEM((2, nblk, LANES, MOBA_BLOCK), BF16),
                        pltpu.VMEM((2, nblk // FAR_GROUP, LANES, FAR_GROUP * MOBA_BLOCK), BF16),
                        pltpu.VMEM((2, 2, LANES, MOBA_BLOCK), F32),
                        pltpu.VMEM((2, 2, 1, MOBA_BLOCK), F32),
                        pltpu.VMEM((2, 2, nblk, nblk, MOBA_BLOCK), F32),
                        pltpu.VMEM((2, 2, 2, FAR_GROUP * MOBA_BLOCK, MOBA_BLOCK), F32),
                        pltpu.VMEM((2, 2, 2, FAR_GROUP, 1, MOBA_BLOCK), F32)],
        compiler_params=_params(2),
        name="moba",
    )(far, proj3, proj3, proj3, near, ow)


def _split3(v):
    hi = v.astype(BF16)
    r1 = v - hi.astype(F32)
    mid = r1.astype(BF16)
    lo = (r1 - mid.astype(F32)).astype(BF16)
    return hi, mid, lo


GLA_UNROLL = 8


def _gla_kernel(q_ref, k_ref, v_ref, g_ref, ga_ref, wal_ref, bal_ref, gw_ref, o_ref, b_ref, st_ref):
    seq = q_ref.shape[0]
    c = GLA_CHUNK
    pc = 256

    rr = lax.broadcasted_iota(jnp.int32, (pc, pc), 0)
    cc = lax.broadcasted_iota(jnp.int32, (pc, pc), 1)
    tri = jnp.where((rr >= cc) & (rr // c == cc // c), 1.0, 0.0).astype(BF16)

    def decay_body(j, carry):
        rows = [pl.ds(pl.multiple_of((j * GLA_UNROLL + u) * pc, pc), pc) for u in range(GLA_UNROLL)]
        xg = [jnp.dot(ga_ref[r, :], wal_ref[...], preferred_element_type=F32) + bal_ref[...] for r in rows]
        parts = [_split3((jnp.minimum(x, 0.0) - jnp.log(1.0 + jnp.exp(-jnp.abs(x)))) * (1.0 / GLA_GATE_TAU))
                 for x in xg]
        sums = [[jnp.dot(tri, term, preferred_element_type=F32) for term in p] for p in parts]
        for r, (hi, mid, lo) in zip(rows, sums):
            b_ref[r, :] = hi + mid + lo
        return carry

    lax.fori_loop(0, seq // (pc * GLA_UNROLL), decay_body, 0)

    st_ref[...] = jnp.zeros_like(st_ref)
    lane = lax.broadcasted_iota(jnp.int32, (c, LANES), 1)
    head_mask = (lane < GLA_DK, lane >= GLA_DK)
    causal = lax.broadcasted_iota(jnp.int32, (c, c), 0) >= lax.broadcasted_iota(jnp.int32, (c, c), 1)

    units = [(u, h) for u in range(GLA_UNROLL) for h in range(2)]

    def chunk_body(ci, carry):
        rows = [pl.ds(pl.multiple_of((ci * GLA_UNROLL + u) * c, c), c) for u in range(GLA_UNROLL)]
        qt, kt, qs, ke, e_last = [], [], [], [], []
        for u in range(GLA_UNROLL):
            b = b_ref[rows[u], :]
            ref_row = b[c // 2 - 1:c // 2, :]
            last = b[c - 1:c, :]
            q = q_ref[rows[u], :].astype(F32) * (GLA_DK ** -0.5)
            k = k_ref[rows[u], :].astype(F32)
            qt.append(q * jnp.exp(b - ref_row))
            kt.append((k * jnp.exp(ref_row - b)).astype(BF16))
            qs.append(q * jnp.exp(b))
            ke.append((k * jnp.exp(last - b)).astype(BF16))
            e_last.append(jnp.exp(last))
        vs = {(u, h): v_ref[rows[u], h * GLA_DV:(h + 1) * GLA_DV] for u, h in units}
        a = {(u, h): lax.dot_general(jnp.where(head_mask[h], qt[u], 0.0).astype(BF16), kt[u], NT_DIMS,
                                     preferred_element_type=F32) for u, h in units}
        inc = {(u, h): lax.dot_general(vs[u, h], ke[u], TN_DIMS, preferred_element_type=F32) for u, h in units}
        o = {(u, h): jnp.dot(jnp.where(causal, a[u, h], 0.0).astype(BF16), vs[u, h], preferred_element_type=F32)
             for u, h in units}
        states = {}
        for h in range(2):
            st = st_ref[h]
            for u in range(GLA_UNROLL):
                states[u, h] = st
                st = st * e_last[u] + inc[u, h]
            st_ref[h] = st
        for u, h in units:
            cols = slice(h * GLA_DV, (h + 1) * GLA_DV)
            ou = o[u, h] + lax.dot_general(jnp.where(head_mask[h], qs[u], 0.0).astype(BF16),
                                           states[u, h].astype(BF16), NT_DIMS, preferred_element_type=F32)
            ms = jnp.mean(ou * ou, axis=-1, keepdims=True)
            on = ou * lax.rsqrt(ms + NORM_EPS) * gw_ref[...]
            g = g_ref[rows[u], cols].astype(F32)
            o_ref[rows[u], cols] = (on * _silu(g)).astype(o_ref.dtype)
        return carry

    lax.fori_loop(0, seq // (c * GLA_UNROLL), chunk_body, 0)


def _gla(proj3, ga3, wal, bal, gw):
    b, s, _ = proj3.shape
    npair = GLA_HEADS // 2
    qcol = 3 * MOBA_WIDTH // LANES
    kcol = qcol + GLA_KEY_WIDTH // LANES
    vcol = (3 * MOBA_WIDTH + 2 * GLA_KEY_WIDTH) // (2 * GLA_DV)
    gcol = vcol + npair
    return pl.pallas_call(
        _gla_kernel,
        grid=(b, npair),
        in_specs=[pl.BlockSpec((None, s, LANES), lambda bb, hp: (bb, 0, qcol + hp)),
                  pl.BlockSpec((None, s, LANES), lambda bb, hp: (bb, 0, kcol + hp)),
                  pl.BlockSpec((None, s, 2 * GLA_DV), lambda bb, hp: (bb, 0, vcol + hp)),
                  pl.BlockSpec((None, s, 2 * GLA_DV), lambda bb, hp: (bb, 0, gcol + hp)),
                  pl.BlockSpec((None, s, LANES), lambda bb, hp: (bb, 0, 0)),
                  pl.BlockSpec((LANES, LANES), lambda bb, hp: (0, hp)),
                  pl.BlockSpec((1, LANES), lambda bb, hp: (0, hp)),
                  pl.BlockSpec((1, GLA_DV), lambda bb, hp: (0, 0))],
        out_specs=pl.BlockSpec((None, s, 2 * GLA_DV), lambda bb, hp: (bb, 0, hp)),
        out_shape=jax.ShapeDtypeStruct((b, s, GLA_WIDTH), BF16),
        scratch_shapes=[pltpu.VMEM((s, LANES), F32),
                        pltpu.VMEM((2, GLA_DV, LANES), F32)],
        compiler_params=_params(2),
        name="gla",
    )(proj3, proj3, proj3, proj3, ga3, wal, bal, gw)


HALF = D_MODEL // 2


def _pack_rows(v):
    return pltpu.pack_elementwise([v[:, :HALF], v[:, HALF:]], packed_dtype=BF16)


def _unpack_rows(w):
    return (pltpu.unpack_elementwise(w, index=0, packed_dtype=BF16, unpacked_dtype=F32),
            pltpu.unpack_elementwise(w, index=1, packed_dtype=BF16, unpacked_dtype=F32))


def _outproj_kernel(oa_ref, ob_ref, x_ref, g1_ref, sc_ref, sh_ref, g2_ref, nw_ref, wo_ref,
                    ws1_ref, ws3_ref, ws2_ref, wrt_ref, base_ref, h_ref, st_ref):
    mix = (jnp.dot(oa_ref[...], wo_ref[:MOBA_WIDTH, :], preferred_element_type=F32)
           + jnp.dot(ob_ref[...], wo_ref[MOBA_WIDTH:, :], preferred_element_type=F32))
    x1 = x_ref[...] + g1_ref[...] * mix
    ms = jnp.mean(x1 * x1, axis=-1, keepdims=True)
    h = x1 * lax.rsqrt(ms + NORM_EPS) * nw_ref[...]
    h = h * (1.0 + sc_ref[...]) + sh_ref[...]
    h_ref[...] = _pack_rows(h)
    hb = h.astype(BF16)
    a = jnp.dot(hb, ws1_ref[...], preferred_element_type=F32)
    u = jnp.dot(hb, ws3_ref[...], preferred_element_type=F32)
    shared = jnp.dot((_silu(a) * u).astype(BF16), ws2_ref[...], preferred_element_type=F32)
    base_ref[...] = x1 + g2_ref[...] * shared
    logits_t = lax.dot_general(wrt_ref[...], hb, NT_DIMS, preferred_element_type=F32)
    st_ref[...] = jax.nn.sigmoid(logits_t)


def _outproj(oa, ob, x2, g1, sc, sh, g2, nw, wo, ws1, ws3, ws2, wrt, seq):
    t = x2.shape[0]
    tpb = seq // ROW_TILE
    vec = lambda: pl.BlockSpec((None, 1, D_MODEL), lambda i: (i // tpb, 0, 0))
    full = lambda a: pl.BlockSpec(a.shape, lambda i: (0,) * a.ndim)
    rows = lambda w: pl.BlockSpec((ROW_TILE, w), lambda i: (i, 0))
    return pl.pallas_call(
        _outproj_kernel,
        grid=(t // ROW_TILE,),
        in_specs=[rows(MOBA_WIDTH), rows(GLA_WIDTH), rows(D_MODEL), vec(), vec(), vec(), vec(),
                  full(nw), full(wo), full(ws1), full(ws3), full(ws2), full(wrt)],
        out_specs=[rows(D_MODEL), rows(HALF), pl.BlockSpec((N_EXPERTS, ROW_TILE), lambda i: (0, i))],
        out_shape=[jax.ShapeDtypeStruct((t, D_MODEL), F32),
                   jax.ShapeDtypeStruct((t, HALF), jnp.uint32),
                   jax.ShapeDtypeStruct((N_EXPERTS, t), F32)],
        compiler_params=_params(1),
        name="outproj",
    )(oa, ob, x2, g1, sc, sh, g2, nw, wo, ws1, ws3, ws2, wrt)


SLOT_CODE_SHIFT = 16
SLOT_CODE_BASE = 1 << SLOT_CODE_SHIFT


def _route_kernel(s_ref, eb_ref, code_ref, w_ref, cnt_ref, carry_ref):
    i = pl.program_id(0)
    ne, nt = s_ref.shape

    @pl.when(i == 0)
    def _init():
        carry_ref[...] = jnp.zeros_like(carry_ref)

    s = s_ref[...]
    choice = s + eb_ref[...]
    gio = lax.broadcasted_iota(jnp.int32, (GROUP_SIZE, nt), 0)
    gscore = []
    for g in range(N_GROUPS):
        cg = choice[g * GROUP_SIZE:(g + 1) * GROUP_SIZE, :]
        top1 = jnp.max(cg, axis=0, keepdims=True)
        first = jnp.min(jnp.where(cg == top1, gio, GROUP_SIZE), axis=0, keepdims=True)
        top2 = jnp.max(jnp.where(gio == first, -jnp.inf, cg), axis=0, keepdims=True)
        gscore.append(top1 + top2)
    gs = jnp.concatenate(gscore, axis=0)
    gidx = lax.broadcasted_iota(jnp.int32, gs.shape, 0)
    beaten = jnp.zeros(gs.shape, jnp.int32)
    for m in range(N_GROUPS):
        gm = gs[m:m + 1, :]
        beaten = beaten + jnp.where((gm > gs) | ((gm == gs) & (gidx > m)), 1, 0)
    gkeep = beaten < TOPK_GROUPS
    masked = jnp.concatenate(
        [jnp.where(gkeep[g:g + 1, :], choice[g * GROUP_SIZE:(g + 1) * GROUP_SIZE, :], -jnp.inf)
         for g in range(N_GROUPS)], axis=0)

    eio = lax.broadcasted_iota(jnp.int32, (ne, nt), 0)
    picked = jnp.zeros((ne, nt), F32)
    idx_rows, w_rows, hits = [], [], []
    for _ in range(TOP_K):
        mx = jnp.max(masked, axis=0, keepdims=True)
        idx = jnp.min(jnp.where(masked == mx, eio, ne), axis=0, keepdims=True)
        hit = eio == idx
        w_rows.append(jnp.sum(jnp.where(hit, s, 0.0), axis=0, keepdims=True))
        idx_rows.append(idx)
        hits.append(hit)
        masked = jnp.where(hit, -jnp.inf, masked)
        picked = jnp.where(hit, 1.0, picked)
    wk = jnp.concatenate(w_rows, axis=0)
    w_ref[...] = wk / jnp.sum(wk, axis=0, keepdims=True) * ROUTED_SCALE

    tr = lax.broadcasted_iota(jnp.int32, (nt, nt), 0)
    tc = lax.broadcasted_iota(jnp.int32, (nt, nt), 1)
    before = jnp.where(tr < tc, 1.0, 0.0).astype(BF16)
    pb = picked.astype(BF16)
    pos = carry_ref[...] + jnp.dot(pb, before, preferred_element_type=F32)
    rank = jnp.concatenate(
        [jnp.sum(jnp.where(hit, pos, 0.0), axis=0, keepdims=True) for hit in hits], axis=0).astype(jnp.int32)
    code_ref[...] = jnp.concatenate(idx_rows, axis=0) * SLOT_CODE_BASE + rank
    total = carry_ref[...] + jnp.dot(pb, jnp.ones((nt, nt), BF16), preferred_element_type=F32)
    carry_ref[...] = total
    cnt_ref[...] = total


def _route(scores_t, eb):
    ne, t = scores_t.shape
    assert t <= SLOT_CODE_BASE
    nt = ROUTE_TILE
    tok = lambda dt: jax.ShapeDtypeStruct((TOP_K, t), dt)
    return pl.pallas_call(
        _route_kernel,
        grid=(t // nt,),
        in_specs=[pl.BlockSpec((ne, nt), lambda i: (0, i)),
                  pl.BlockSpec((ne, nt), lambda i: (0, 0))],
        out_specs=[pl.BlockSpec((TOP_K, nt), lambda i: (0, i)),
                   pl.BlockSpec((TOP_K, nt), lambda i: (0, i)),
                   pl.BlockSpec((ne, nt), lambda i: (0, 0))],
        out_shape=[tok(jnp.int32), tok(F32), jax.ShapeDtypeStruct((ne, nt), F32)],
        scratch_shapes=[pltpu.VMEM((ne, nt), F32)],
        compiler_params=_params(1),
        name="route",
    )(scores_t, eb)


SLOT_TILE = 2048


def _slots_kernel(pstart_ref, code_ref, o_ref):
    code = code_ref[...]
    expert = lax.shift_right_logical(code, SLOT_CODE_SHIFT)

    def body(e, acc):
        return jnp.where(expert == e, pstart_ref[e], acc)

    start = lax.fori_loop(0, N_EXPERTS, body, jnp.zeros_like(code), unroll=8)
    o_ref[...] = start + (code & (SLOT_CODE_BASE - 1))


def _slots(pstart, code_t):
    k, t = code_t.shape
    return pl.pallas_call(
        _slots_kernel,
        grid_spec=pltpu.PrefetchScalarGridSpec(
            num_scalar_prefetch=1,
            grid=(t // SLOT_TILE,),
            in_specs=[pl.BlockSpec((k, SLOT_TILE), lambda i, p: (0, i))],
            out_specs=pl.BlockSpec((k, SLOT_TILE), lambda i, p: (0, i)),
        ),
        out_shape=jax.ShapeDtypeStruct((k, t), jnp.int32),
        compiler_params=_params(1),
        name="slots",
    )(pstart, code_t)


SC_WINDOW = 128


def _sc_gather_rows(table, idx_flat):
    info = plsc.get_sparse_core_info()
    nw = info.num_cores * info.num_subcores
    n = idx_flat.shape[0]
    width = table.shape[1]
    per_worker = n // nw
    assert per_worker * nw == n and per_worker % SC_WINDOW == 0
    mesh = plsc.VectorSubcoreMesh(core_axis_name="c", subcore_axis_name="s")

    def body(table_hbm, idx_hbm, out_hbm, idx_v, rows_v, sem):
        wid = lax.axis_index("s") * info.num_cores + lax.axis_index("c")
        base = wid * per_worker

        @pl.loop(0, per_worker // SC_WINDOW)
        def _(w):
            off = pl.multiple_of(base + w * SC_WINDOW, SC_WINDOW)
            pltpu.sync_copy(idx_hbm.at[pl.ds(off, SC_WINDOW)], idx_v)
            pltpu.async_copy(table_hbm.at[idx_v], rows_v, sem).wait()
            pltpu.sync_copy(rows_v, out_hbm.at[pl.ds(off, SC_WINDOW)])

    return pl.kernel(
        body,
        out_type=jax.ShapeDtypeStruct((n, width), table.dtype),
        mesh=mesh,
        scratch_types=[pltpu.VMEM((SC_WINDOW,), jnp.int32),
                       pltpu.VMEM((SC_WINDOW, width), table.dtype),
                       pltpu.SemaphoreType.DMA],
        name="sc_gather",
    )(table, idx_flat)


def _sc_scatter_rows(rows, idx_kt, n_out):
    info = plsc.get_sparse_core_info()
    nw = info.num_cores * info.num_subcores
    t, width = rows.shape
    nk = idx_kt.shape[0]
    per_worker = t // nw
    assert per_worker * nw == t and per_worker % SC_WINDOW == 0
    mesh = plsc.VectorSubcoreMesh(core_axis_name="c", subcore_axis_name="s")

    def body(rows_hbm, idx_hbm, out_hbm, idx_v, rows_v, sem):
        wid = lax.axis_index("s") * info.num_cores + lax.axis_index("c")
        base = wid * per_worker

        @pl.loop(0, per_worker // SC_WINDOW)
        def _(w):
            off = pl.multiple_of(base + w * SC_WINDOW, SC_WINDOW)
            pltpu.sync_copy(rows_hbm.at[pl.ds(off, SC_WINDOW)], rows_v)
            pltpu.sync_copy(idx_hbm.at[:, pl.ds(off, SC_WINDOW)], idx_v)
            copies = [pltpu.async_copy(rows_v, out_hbm.at[idx_v.at[k]], sem) for k in range(nk)]
            for cp in copies:
                cp.wait()

    return pl.kernel(
        body,
        out_type=jax.ShapeDtypeStruct((n_out, width), rows.dtype),
        mesh=mesh,
        scratch_types=[pltpu.VMEM((nk, SC_WINDOW), jnp.int32),
                       pltpu.VMEM((SC_WINDOW, width), rows.dtype),
                       pltpu.SemaphoreType.DMA],
        name="sc_scatter",
    )(rows, idx_kt)


def _expert_kernel(first_ref, count_ref, used_ref, w1_ref, w3_ref, w2_ref, xs_ref, ys_ref,
                   xbuf, ybuf, sem_in, sem_out, w1f, w3f, w2f, sem_w, w1b, w3b, w2b, *, layer):
    e = pl.program_id(0)
    ne = pl.num_programs(0)
    r = xbuf.shape[1]
    n_used = used_ref[0]

    def x_copy(g, slot):
        return pltpu.make_async_copy(xs_ref.at[pl.ds(pl.multiple_of(g * r, r), r), :], xbuf.at[slot], sem_in.at[slot])

    def y_copy(g, slot):
        return pltpu.make_async_copy(ybuf.at[slot], ys_ref.at[pl.ds(pl.multiple_of(g * r, r), r), :], sem_out.at[slot])

    def w_copies(ex, slot):
        return [pltpu.make_async_copy(src.at[layer, ex], dst.at[slot], sem_w.at[slot])
                for src, dst in ((w1_ref, w1f), (w3_ref, w3f), (w2_ref, w2f))]

    nin = xbuf.shape[0]
    nout = ybuf.shape[0]

    @pl.when(e == 0)
    def _first_reads():
        for g in range(nin):
            @pl.when(g < n_used)
            def _(g=g):
                x_copy(g, g).start()
        for cp in w_copies(0, 0):
            cp.start(priority=1)

    @pl.when(e + 1 < ne)
    def _next_weights():
        for cp in w_copies(e + 1, (e + 1) & 1):
            cp.start(priority=1)

    for cp in w_copies(e, e & 1):
        cp.wait()

    n = count_ref[e]

    @pl.when(n > 0)
    def _cast_weights():
        w1b[...] = w1f[e & 1].astype(BF16)
        w3b[...] = w3f[e & 1].astype(BF16)
        w2b[...] = w2f[e & 1].astype(BF16)

    def run_tiles(g, count):
        tiles = [g + j for j in range(count)]
        for gj in tiles:
            x_copy(gj, gj % nin).wait()
        x = jnp.concatenate([xbuf[gj % nin] for gj in tiles], axis=0)
        for gj in tiles:
            @pl.when(gj + nin < n_used)
            def _(gj=gj):
                x_copy(gj + nin, gj % nin).start()
        lo, hi = _unpack_rows(x)
        lo, hi = lo.astype(BF16), hi.astype(BF16)
        a = (jnp.dot(lo, w1b[:HALF, :], preferred_element_type=F32)
             + jnp.dot(hi, w1b[HALF:, :], preferred_element_type=F32))
        u = (jnp.dot(lo, w3b[:HALF, :], preferred_element_type=F32)
             + jnp.dot(hi, w3b[HALF:, :], preferred_element_type=F32))
        y = _pack_rows(jnp.dot((_silu(a) * u).astype(BF16), w2b[...], preferred_element_type=F32))
        for j, gj in enumerate(tiles):
            @pl.when(gj >= nout)
            def _(gj=gj):
                y_copy(gj - nout, gj % nout).wait()
            ybuf[gj % nout] = y[j * r:(j + 1) * r]
            y_copy(gj, gj % nout).start()

    g0 = first_ref[e]
    pair = EXPERT_TILES_PER_MATMUL

    def pair_body(p, carry):
        run_tiles(g0 + p * pair, pair)
        return carry

    lax.fori_loop(0, n // pair, pair_body, 0)
    for left in range(1, pair):
        pl.when(n % pair == left)(functools.partial(run_tiles, g0 + n - left, left))

    @pl.when(e == ne - 1)
    def _drain_writes():
        for back in range(nout, 0, -1):
            @pl.when(n_used >= back)
            def _(back=back):
                y_copy(n_used - back, (n_used - back) % nout).wait()


def _experts(tile_first, tile_count, n_used, xs, w1, w3, w2, layer):
    n_rows = xs.shape[0]
    r = EXPERT_TILE
    any_spec = pl.BlockSpec(memory_space=pl.ANY)
    return pl.pallas_call(
        functools.partial(_expert_kernel, layer=layer),
        grid_spec=pltpu.PrefetchScalarGridSpec(
            num_scalar_prefetch=3,
            grid=(N_EXPERTS,),
            in_specs=[any_spec, any_spec, any_spec, any_spec],
            out_specs=any_spec,
            scratch_shapes=[pltpu.VMEM((EXPERT_IN_RING, r, HALF), jnp.uint32),
                            pltpu.VMEM((EXPERT_OUT_RING, r, HALF), jnp.uint32),
                            pltpu.SemaphoreType.DMA((EXPERT_IN_RING,)),
                            pltpu.SemaphoreType.DMA((EXPERT_OUT_RING,)),
                            pltpu.VMEM((2, D_MODEL, EXPERT_FF), F32),
                            pltpu.VMEM((2, D_MODEL, EXPERT_FF), F32),
                            pltpu.VMEM((2, EXPERT_FF, D_MODEL), F32),
                            pltpu.SemaphoreType.DMA((2,)),
                            pltpu.VMEM((D_MODEL, EXPERT_FF), BF16),
                            pltpu.VMEM((D_MODEL, EXPERT_FF), BF16),
                            pltpu.VMEM((EXPERT_FF, D_MODEL), BF16)],
        ),
        out_shape=jax.ShapeDtypeStruct((n_rows, HALF), jnp.uint32),
        compiler_params=_params(1),
        name="experts",
    )(tile_first, tile_count, n_used, w1, w3, w2, xs)


def _combine_dense_kernel(base_ref, g2_ref, w_ref, yg_ref, o_ref):
    acc_lo = acc_hi = None
    for k in range(TOP_K):
        lo, hi = _unpack_rows(yg_ref[k])
        wk = w_ref[:, k:k + 1]
        acc_lo = wk * lo if acc_lo is None else acc_lo + wk * lo
        acc_hi = wk * hi if acc_hi is None else acc_hi + wk * hi
    o_ref[:, :HALF] = base_ref[:, :HALF] + g2_ref[:, :HALF] * acc_lo
    o_ref[:, HALF:] = base_ref[:, HALF:] + g2_ref[:, HALF:] * acc_hi


def _combine_dense(base, g2, w_tok, yg, batch):
    t = base.shape[0]
    seq = yg.shape[1]
    nt = ROUTE_TILE
    tpb = seq // nt
    rows = lambda i: (batch * tpb + i, 0)
    return pl.pallas_call(
        _combine_dense_kernel,
        grid=(tpb,),
        in_specs=[pl.BlockSpec((nt, D_MODEL), rows),
                  pl.BlockSpec((None, 1, D_MODEL), lambda i: (batch, 0, 0)),
                  pl.BlockSpec((nt, TOP_K), rows),
                  pl.BlockSpec((TOP_K, nt, HALF), lambda i: (0, i, 0))],
        out_specs=pl.BlockSpec((nt, D_MODEL), rows),
        out_shape=jax.ShapeDtypeStruct((t, D_MODEL), F32),
        input_output_aliases={0: 0},
        compiler_params=_params(1),
        name="combine_dense",
    )(base, g2, w_tok, yg)


def _layer(layer, x, c, w_ada, b_ada, norm1_w, norm2_w, w_in, q_norm_w, k_norm_w, rel_bias, w_alpha, b_alpha,
           moba_out_w, gla_out_w, w_out, w_router, e_bias, w1, w3, w2, ws1, ws3, ws2):
    b, s, d = x.shape
    t = b * s
    x2 = x.reshape(t, d)

    mod = _mod(c, w_ada, b_ada)
    sh1, sc1, g1, sh2, sc2, g2 = [mod[:, j * d:(j + 1) * d].reshape(b, 1, d) for j in range(6)]

    w_main = w_in[:, :D_MAIN].astype(BF16)
    w_ga = jnp.zeros((d, LANES), BF16).at[:, :GLA_GATE_RANK].set(w_in[:, D_MAIN:].astype(BF16))
    per_chunk = 256 // MOBA_HEAD_DIM
    qw = jnp.tile(q_norm_w.astype(F32), per_chunk).reshape(1, 256) * (MOBA_HEAD_DIM ** -0.5 * LOG2E)
    kw = jnp.tile(k_norm_w.astype(F32), per_chunk).reshape(1, 256)
    proj, ga = _inproj(x2, sc1, sh1, norm1_w.reshape(1, d), w_main, w_ga, qw, kw, s)
    proj3 = proj.reshape(b, s, D_MAIN)

    near, far = _moba_bias_tables(rel_bias)
    ow = jnp.tile(moba_out_w.astype(F32), 2).reshape(1, LANES)
    o_a = _moba(proj3, near, far, ow)

    wal = jnp.zeros((LANES, GLA_KEY_WIDTH), F32).at[:GLA_GATE_RANK].set(w_alpha)
    o_b = _gla(proj3, ga.reshape(b, s, LANES), wal, b_alpha.reshape(1, GLA_KEY_WIDTH),
               gla_out_w.reshape(1, GLA_DV))

    base, h2, scores_t = _outproj(
        o_a.reshape(t, MOBA_WIDTH), o_b.reshape(t, GLA_WIDTH), x2, g1, sc2, sh2, g2,
        norm2_w.reshape(1, d), w_out.astype(BF16), ws1.astype(BF16), ws3.astype(BF16), ws2.astype(BF16),
        w_router.T.astype(BF16), s)

    eb = jnp.broadcast_to(e_bias.astype(F32)[:, None], (N_EXPERTS, ROUTE_TILE))
    code_t, w_t, counts = _route(scores_t, eb)

    r = EXPERT_TILE
    n_tiles = (t * TOP_K + N_EXPERTS * (r - 1) + r - 1) // r
    n_rows = n_tiles * r
    cnt = counts[:, 0].astype(jnp.int32)
    padded = (cnt + r - 1) // r * r
    pend = jnp.cumsum(padded)
    pstart = pend - padded
    n_used = (pend[-1:] // r).astype(jnp.int32)
    dest_t = _slots(pstart, code_t)

    xs = _sc_scatter_rows(h2, dest_t, n_rows)
    ys = _experts(pstart // r, padded // r, n_used, xs, w1, w3, w2, layer)
    w_tok = w_t.T
    out = base
    for bi in range(b):
        idx = dest_t[:, bi * s:(bi + 1) * s].reshape(TOP_K * s)
        yg = _sc_gather_rows(ys, idx).reshape(TOP_K, s, HALF)
        out = _combine_dense(out, g2, w_tok, yg, bi)
    return out.reshape(b, s, d)


def kernel(x, c, w_ada, b_ada, norm1_w, norm2_w, w_in, q_norm_w, k_norm_w, rel_bias, w_alpha, b_alpha,
           moba_out_w, gla_out_w, w_out, w_router, e_bias, w1, w3, w2, ws1, ws3, ws2):
    for l in range(w_ada.shape[0]):
        x = _layer(l, x, c, w_ada[l], b_ada[l], norm1_w[l], norm2_w[l], w_in[l], q_norm_w[l], k_norm_w[l],
                   rel_bias, w_alpha[l], b_alpha[l], moba_out_w[l], gla_out_w[l], w_out[l], w_router[l],
                   e_bias[l], w1, w3, w2, ws1[l], ws3[l], ws2[l])
    return x
```

```python
import functools
import math

import numpy as np
import jax
import jax.numpy as jnp
from jax import lax
from jax.experimental import pallas as pl
from jax.experimental.pallas import tpu as pltpu
from jax.experimental.pallas import tpu_sc as plsc

D_MODEL = 1024
MOBA_HEADS = 8
MOBA_HEAD_DIM = 64
MOBA_WIDTH = MOBA_HEADS * MOBA_HEAD_DIM
MOBA_BLOCK = 256
MOBA_TOPK = 3
GLA_HEADS = 4
GLA_DK = 64
GLA_DV = 128
GLA_KEY_WIDTH = GLA_HEADS * GLA_DK
GLA_WIDTH = GLA_HEADS * GLA_DV
GLA_GATE_RANK = 16
GLA_GATE_TAU = 16.0
GLA_CHUNK = 64
REL_BUCKETS = 32
REL_MAX_DIST = 128
N_EXPERTS = 256
TOP_K = 8
N_GROUPS = 8
TOPK_GROUPS = 4
GROUP_SIZE = N_EXPERTS // N_GROUPS
EXPERT_FF = 256
SHARED_FF = 256
ROUTED_SCALE = 2.5
NORM_EPS = 1e-6
LOG2E = math.log2(math.e)

D_MAIN = 3 * MOBA_WIDTH + 2 * GLA_KEY_WIDTH + 2 * GLA_WIDTH
LANES = 128
VMEM_LIMIT = 56 * 1024 * 1024

ROW_TILE = 512
ROUTE_TILE = 256
EXPERT_TILE = 128
EXPERT_TILES_PER_MATMUL = 6
EXPERT_IN_RING = 16
EXPERT_OUT_RING = 12
EXPERT_WEIGHT_RING = 3

F32 = jnp.float32
BF16 = jnp.bfloat16
NT_DIMS = (((1,), (1,)), ((), ()))
TN_DIMS = (((0,), (0,)), ((), ()))


def _params(n_axes):
    return pltpu.CompilerParams(dimension_semantics=("arbitrary",) * n_axes,
                                vmem_limit_bytes=VMEM_LIMIT)


def _silu(v):
    return v * jax.nn.sigmoid(v)


def _mod_kernel(c_ref, w_ref, b_ref, o_ref):
    o_ref[...] = jnp.dot(_silu(c_ref[...]), w_ref[...], preferred_element_type=F32) + b_ref[...]


def _mod(c, w, b):
    rows = 8
    cp = jnp.zeros((rows, D_MODEL), F32).at[:c.shape[0]].set(c)
    n = w.shape[1]
    tn = 1024
    out = pl.pallas_call(
        _mod_kernel,
        grid=(n // tn,),
        in_specs=[pl.BlockSpec((rows, D_MODEL), lambda j: (0, 0)),
                  pl.BlockSpec((D_MODEL, tn), lambda j: (0, j)),
                  pl.BlockSpec((1, tn), lambda j: (0, j))],
        out_specs=pl.BlockSpec((rows, tn), lambda j: (0, j)),
        out_shape=jax.ShapeDtypeStruct((rows, n), F32),
        compiler_params=_params(1),
        name="mod",
    )(cp, w, b.reshape(1, n))
    return out[:c.shape[0]]


def _group_rms_inv(a, group):
    lane = lax.broadcasted_iota(jnp.int32, (1, a.shape[1]), 1)
    a2 = a * a
    inv = jnp.zeros_like(a)
    for g in range(a.shape[1] // group):
        m = (lane >= g * group) & (lane < (g + 1) * group)
        ss = jnp.sum(jnp.where(m, a2, 0.0), axis=-1, keepdims=True)
        inv = jnp.where(m, lax.rsqrt(ss * (1.0 / group) + NORM_EPS), inv)
    return inv


def _inproj_kernel(x_ref, sc_ref, sh_ref, nw_ref, w_ref, wga_ref, qw_ref, kw_ref, o_ref, ga_ref):
    x = x_ref[...]
    ms = jnp.mean(x * x, axis=-1, keepdims=True)
    h = x * lax.rsqrt(ms + NORM_EPS) * nw_ref[...]
    h = h * (1.0 + sc_ref[...]) + sh_ref[...]
    hb = h.astype(BF16)
    cw = 256
    for j in range(D_MAIN // cw):
        acc = jnp.dot(hb, w_ref[:, j * cw:(j + 1) * cw], preferred_element_type=F32)
        if j < 2 * MOBA_WIDTH // cw:
            nw = qw_ref if j < MOBA_WIDTH // cw else kw_ref
            acc = acc * _group_rms_inv(acc, MOBA_HEAD_DIM) * nw[...]
        o_ref[:, j * cw:(j + 1) * cw] = acc.astype(BF16)
    ga_ref[...] = jnp.dot(hb, wga_ref[...], preferred_element_type=F32)


def _inproj(x2, sc, sh, nw, w_main, w_ga, qw, kw, seq):
    t = x2.shape[0]
    tpb = seq // ROW_TILE
    vec = lambda: pl.BlockSpec((None, 1, D_MODEL), lambda i: (i // tpb, 0, 0))
    full = lambda a: pl.BlockSpec(a.shape, lambda i: (0,) * a.ndim)
    return pl.pallas_call(
        _inproj_kernel,
        grid=(t // ROW_TILE,),
        in_specs=[pl.BlockSpec((ROW_TILE, D_MODEL), lambda i: (i, 0)), vec(), vec(),
                  full(nw), full(w_main), full(w_ga), full(qw), full(kw)],
        out_specs=[pl.BlockSpec((ROW_TILE, D_MAIN), lambda i: (i, 0)),
                   pl.BlockSpec((ROW_TILE, LANES), lambda i: (i, 0))],
        out_shape=[jax.ShapeDtypeStruct((t, D_MAIN), BF16),
                   jax.ShapeDtypeStruct((t, LANES), F32)],
        compiler_params=_params(1),
        name="inproj",
    )(x2, sc, sh, nw, w_main, w_ga, qw, kw)


def _t5_bucket_np(rel):
    max_exact = REL_BUCKETS // 2
    relf = np.maximum(rel, 1).astype(np.float64)
    large = max_exact + (np.log(relf / max_exact) / math.log(REL_MAX_DIST / max_exact)
                         * (REL_BUCKETS - max_exact)).astype(np.int32)
    large = np.minimum(large, REL_BUCKETS - 1)
    return np.where(rel < max_exact, rel, large)


def _bias_kernel(rb_ref, idx_ref, o_ref):
    h = pl.program_id(0)
    idx = idx_ref[...]
    tab = jnp.full(idx.shape, -jnp.inf, F32)
    for bk in range(REL_BUCKETS):
        tab = jnp.where(idx == bk, rb_ref[bk * MOBA_HEADS + h], tab)
    o_ref[...] = tab


def _moba_bias_tables(rel_bias):
    j = np.arange(MOBA_BLOCK)[:, None]
    i = np.arange(MOBA_BLOCK)[None, :]
    own_idx = np.where(j <= i, _t5_bucket_np(np.maximum(i - j, 0)), -1)
    prev_idx = _t5_bucket_np(MOBA_BLOCK + i - j)
    idx = jnp.asarray(np.concatenate([prev_idx, own_idx], axis=0).astype(np.int32))
    assert int(_t5_bucket_np(np.array([MOBA_BLOCK + 1]))[0]) == REL_BUCKETS - 1
    rb = rel_bias.astype(F32) * LOG2E
    near = pl.pallas_call(
        _bias_kernel,
        grid=(MOBA_HEADS,),
        in_specs=[pl.BlockSpec(memory_space=pltpu.SMEM),
                  pl.BlockSpec(idx.shape, lambda h: (0, 0))],
        out_specs=pl.BlockSpec((None,) + idx.shape, lambda h: (h, 0, 0)),
        out_shape=jax.ShapeDtypeStruct((MOBA_HEADS,) + idx.shape, F32),
        compiler_params=_params(1),
        name="bias",
    )(rb.reshape(-1), idx)
    return near, rb[REL_BUCKETS - 1]


FAR_GROUP = 4


def _moba_kernel(*refs):
    hp = pl.program_id(1)
    _moba_body(None, hp, *refs, prepare=True)
    _moba_body(None, hp, *refs, prepare=False)

    def block_pair(j, carry):
        _moba_body(2 * j, hp, *refs, prepare=False)
        return carry

    lax.fori_loop(1, refs[2].shape[0] // (2 * MOBA_BLOCK), block_pair, 0)


def _moba_body(i, hp, far_ref, q_ref, k_ref, v_ref, near_ref, ow_ref, o_ref,
               vt_ref, vtg_ref, acc_ref, m_ref, sel_ref, s_ref, mx_ref, *, prepare):
    nblk = k_ref.shape[0] // MOBA_BLOCK
    ngrp = nblk // FAR_GROUP
    hd = MOBA_HEAD_DIM
    bs = MOBA_BLOCK
    lane = lax.broadcasted_iota(jnp.int32, (bs, LANES), 1)

    def split_heads(qb):
        zero = jnp.zeros_like(qb)
        return jnp.where(lane < hd, qb, zero), jnp.where(lane < hd, zero, qb)

    def _prepare():
        row = lax.broadcasted_iota(jnp.int32, (LANES, bs), 0)
        kmeans = []
        for n in range(nblk):
            kb = k_ref[n * bs:(n + 1) * bs, :].astype(F32)
            kmeans.append(jnp.mean(kb, axis=0, keepdims=True))
            vt = v_ref[n * bs:(n + 1) * bs, :].astype(F32).T
            vt0 = jnp.where(row < hd, vt, 1.0).astype(BF16)
            vt1 = jnp.where(row < hd, 1.0, vt).astype(BF16)
            vt_ref[0, n] = vt0
            vt_ref[1, n] = vt1
            gcols = slice((n % FAR_GROUP) * bs, (n % FAR_GROUP + 1) * bs)
            vtg_ref[0, n // FAR_GROUP, :, gcols] = vt0
            vtg_ref[1, n // FAR_GROUP, :, gcols] = vt1
        kmean = jnp.concatenate(kmeans, axis=0)
        km_hi = kmean.astype(BF16)
        km_lo = (kmean - km_hi.astype(F32)).astype(BF16)
        blk = lax.broadcasted_iota(jnp.int32, (nblk, bs), 0)
        for ib in range(nblk):
            qparts = split_heads(q_ref[ib * bs:(ib + 1) * bs, :])
            for h in range(2):
                gt = (lax.dot_general(km_hi, qparts[h], NT_DIMS, preferred_element_type=F32)
                      + lax.dot_general(km_lo, qparts[h], NT_DIMS, preferred_element_type=F32))
                gt = jnp.where(blk < ib, gt, -jnp.inf)
                cnt = jnp.zeros(gt.shape, jnp.int32)
                for m in range(ib):
                    gm = gt[m:m + 1, :]
                    cnt = cnt + jnp.where((gm > gt) | ((gm == gt) & (blk > m)), 1, 0)
                keep = (blk < ib) & (cnt < MOBA_TOPK)
                sel_ref[0, h, ib] = jnp.where(keep, 1.0, 0.0)
                sel_ref[1, h, ib] = jnp.where(keep & (blk < ib - 1), 1.0, 0.0)

    if prepare:
        _prepare()
        return

    gk = FAR_GROUP * bs

    def rows_of(ib):
        if isinstance(ib, int):
            return slice(ib * bs, (ib + 1) * bs)
        return pl.ds(pl.multiple_of(ib * bs, bs), bs)

    def finish(ib, slot):
        a0 = acc_ref[slot, 0]
        a1 = acc_ref[slot, 1]
        row = lax.broadcasted_iota(jnp.int32, a0.shape, 0)
        ot = jnp.where(row < hd, a0 / a0[hd:hd + 1, :], a1 / a1[0:1, :])
        o2 = ot * ot
        ss0 = jnp.sum(jnp.where(row < hd, o2, 0.0), axis=0, keepdims=True)
        ss1 = jnp.sum(jnp.where(row < hd, 0.0, o2), axis=0, keepdims=True)
        inv = jnp.where(row < hd, lax.rsqrt(ss0 * (1.0 / hd) + NORM_EPS), lax.rsqrt(ss1 * (1.0 / hd) + NORM_EPS))
        o_ref[rows_of(ib), :] = ((ot * inv).T * ow_ref[...]).astype(o_ref.dtype)

    def far_scores(g, slot, qh):
        kb = k_ref[g * gk:(g + 1) * gk, :]
        for h in range(2):
            s = lax.dot_general(kb, qh[h], NT_DIMS, preferred_element_type=F32)
            s_ref[slot, g % 2, h] = s
            for j in range(FAR_GROUP):
                mx_ref[slot, g % 2, h, j] = jnp.max(s[j * bs:(j + 1) * bs], axis=0, keepdims=True)

    def near_scores(ib, qh):
        kbs = (k_ref[rows_of(ib - 1), :], k_ref[rows_of(ib), :])
        return [[lax.dot_general(kbs[w], qh[h], NT_DIMS, preferred_element_type=F32)
                 + near_ref[h, w * bs:(w + 1) * bs, :] for w in range(2)] for h in range(2)]

    def near_values(ib, slot, ss):
        ps, ms = [], []
        for h in range(2):
            s_prev, s_own = ss[h]
            keep = sel_ref[0, h, ib, pl.ds(ib - 1, 1), :] > 0.5
            mx = jnp.where(keep, jnp.max(s_prev, axis=0, keepdims=True), -jnp.inf)
            m_new = jnp.maximum(jnp.max(s_own, axis=0, keepdims=True), mx)
            ps.append((jnp.exp2(s_prev - jnp.where(keep, m_new, jnp.inf)).astype(BF16),
                       jnp.exp2(s_own - m_new).astype(BF16)))
            ms.append(m_new)
        for h in range(2):
            acc_ref[slot, h] = (jnp.dot(vt_ref[h, ib - 1], ps[h][0], preferred_element_type=F32)
                                + jnp.dot(vt_ref[h, ib], ps[h][1], preferred_element_type=F32))
            m_ref[slot, h] = ms[h]

    def far_group(g, ib, slot):
        for h in range(2):
            fb = far_ref[2 * hp + h]
            m_old = m_ref[slot, h]
            m_new = m_old
            keeps = []
            for j in range(FAR_GROUP):
                keep = sel_ref[1, h, ib, pl.ds(g * FAR_GROUP + j, 1), :] > 0.5
                m_new = jnp.maximum(m_new, jnp.where(keep, mx_ref[slot, g % 2, h, j] + fb, -jnp.inf))
                keeps.append(keep)
            p = jnp.concatenate(
                [jnp.exp2(s_ref[slot, g % 2, h, j * bs:(j + 1) * bs, :]
                          - jnp.where(keeps[j], m_new - fb, jnp.inf)).astype(BF16)
                 for j in range(FAR_GROUP)], axis=0)
            pv = jnp.dot(vtg_ref[h, g], p, preferred_element_type=F32)
            acc_ref[slot, h] = acc_ref[slot, h] * jnp.exp2(m_old - m_new) + pv
            m_ref[slot, h] = m_new

    def step_body(nf, blocks):
        qhs = {slot: split_heads(q_ref[rows_of(ib), :]) for ib, slot in blocks}
        near = {slot: near_scores(ib, qhs[slot]) for ib, slot in blocks}
        if nf > 0:
            for ib, slot in blocks:
                far_scores(0, slot, qhs[slot])
        for ib, slot in blocks:
            near_values(ib, slot, near[slot])
        for g in range(nf):
            if g + 1 < nf:
                for ib, slot in blocks:
                    far_scores(g + 1, slot, qhs[slot])
            for ib, slot in blocks:
                far_group(g, ib, slot)
        for ib, slot in blocks:
            finish(ib, slot)

    if i is None:
        qh = split_heads(q_ref[0:bs, :])
        kb = k_ref[0:bs, :]
        for h in range(2):
            s = lax.dot_general(kb, qh[h], NT_DIMS, preferred_element_type=F32) + near_ref[h, bs:2 * bs, :]
            p = jnp.exp2(s - jnp.max(s, axis=0, keepdims=True)).astype(BF16)
            acc_ref[0, h] = jnp.dot(vt_ref[h, 0], p, preferred_element_type=F32)
        finish(0, 0)
        step_body(0, [(1, 1)])
        return

    n_far = (i + FAR_GROUP - 2) // FAR_GROUP
    for nf in range(1, ngrp + 1):
        pl.when(n_far == nf)(functools.partial(step_body, nf, [(i, 0), (i + 1, 1)]))


def _moba(proj3, near, far, ow):
    b, s, _ = proj3.shape
    nblk = s // MOBA_BLOCK
    assert nblk % FAR_GROUP == 0
    npair = MOBA_HEADS // 2
    kcol = MOBA_WIDTH // LANES
    return pl.pallas_call(
        _moba_kernel,
        grid=(b, npair),
        in_specs=[pl.BlockSpec(memory_space=pltpu.SMEM),
                  pl.BlockSpec((None, s, LANES), lambda bb, hp: (bb, 0, hp)),
                  pl.BlockSpec((None, s, LANES), lambda bb, hp: (bb, 0, kcol + hp)),
                  pl.BlockSpec((None, s, LANES), lambda bb, hp: (bb, 0, 2 * kcol + hp)),
                  pl.BlockSpec((2, 2 * MOBA_BLOCK, MOBA_BLOCK), lambda bb, hp: (hp, 0, 0)),
                  pl.BlockSpec((1, LANES), lambda bb, hp: (0, 0))],
        out_specs=pl.BlockSpec((None, s, LANES), lambda bb, hp: (bb, 0, hp)),
        out_shape=jax.ShapeDtypeStruct((b, s, MOBA_WIDTH), BF16),
        scratch_shapes=[pltpu.VMEM((2, nblk, LANES, MOBA_BLOCK), BF16),
                        pltpu.VMEM((2, nblk // FAR_GROUP, LANES, FAR_GROUP * MOBA_BLOCK), BF16),
                        pltpu.VMEM((2, 2, LANES, MOBA_BLOCK), F32),
                        pltpu.VMEM((2, 2, 1, MOBA_BLOCK), F32),
                        pltpu.VMEM((2, 2, nblk, nblk, MOBA_BLOCK), F32),
                        pltpu.VMEM((2, 2, 2, FAR_GROUP * MOBA_BLOCK, MOBA_BLOCK), F32),
                        pltpu.VMEM((2, 2, 2, FAR_GROUP, 1, MOBA_BLOCK), F32)],
        compiler_params=_params(2),
        name="moba",
    )(far, proj3, proj3, proj3, near, ow)


def _split3(v):
    hi = v.astype(BF16)
    r1 = v - hi.astype(F32)
    mid = r1.astype(BF16)
    lo = (r1 - mid.astype(F32)).astype(BF16)
    return hi, mid, lo


GLA_UNROLL = 8


def _gla_kernel(q_ref, k_ref, v_ref, g_ref, ga_ref, wal_ref, bal_ref, gw_ref, o_ref, b_ref, st_ref):
    seq = q_ref.shape[0]
    c = GLA_CHUNK
    pc = 256

    rr = lax.broadcasted_iota(jnp.int32, (pc, pc), 0)
    cc = lax.broadcasted_iota(jnp.int32, (pc, pc), 1)
    tri = jnp.where((rr >= cc) & (rr // c == cc // c), 1.0, 0.0).astype(BF16)

    def decay_body(j, carry):
        rows = [pl.ds(pl.multiple_of((j * GLA_UNROLL + u) * pc, pc), pc) for u in range(GLA_UNROLL)]
        xg = [jnp.dot(ga_ref[r, :], wal_ref[...], preferred_element_type=F32) + bal_ref[...] for r in rows]
        parts = [_split3((jnp.minimum(x, 0.0) - jnp.log(1.0 + jnp.exp(-jnp.abs(x)))) * (1.0 / GLA_GATE_TAU))
                 for x in xg]
        sums = [[jnp.dot(tri, term, preferred_element_type=F32) for term in p] for p in parts]
        for r, (hi, mid, lo) in zip(rows, sums):
            b_ref[r, :] = hi + mid + lo
        return carry

    lax.fori_loop(0, seq // (pc * GLA_UNROLL), decay_body, 0)

    st_ref[...] = jnp.zeros_like(st_ref)
    lane = lax.broadcasted_iota(jnp.int32, (c, LANES), 1)
    head_mask = (lane < GLA_DK, lane >= GLA_DK)
    causal = lax.broadcasted_iota(jnp.int32, (c, c), 0) >= lax.broadcasted_iota(jnp.int32, (c, c), 1)

    units = [(u, h) for u in range(GLA_UNROLL) for h in range(2)]

    def chunk_body(ci, carry):
        rows = [pl.ds(pl.multiple_of((ci * GLA_UNROLL + u) * c, c), c) for u in range(GLA_UNROLL)]
        qt, kt, qs, ke, e_last = [], [], [], [], []
        for u in range(GLA_UNROLL):
            b = b_ref[rows[u], :]
            ref_row = b[c // 2 - 1:c // 2, :]
            last = b[c - 1:c, :]
            q = q_ref[rows[u], :].astype(F32) * (GLA_DK ** -0.5)
            k = k_ref[rows[u], :].astype(F32)
            qt.append(q * jnp.exp(b - ref_row))
            kt.append((k * jnp.exp(ref_row - b)).astype(BF16))
            qs.append(q * jnp.exp(b))
            ke.append((k * jnp.exp(last - b)).astype(BF16))
            e_last.append(jnp.exp(last))
        vs = {(u, h): v_ref[rows[u], h * GLA_DV:(h + 1) * GLA_DV] for u, h in units}
        a = {(u, h): lax.dot_general(jnp.where(head_mask[h], qt[u], 0.0).astype(BF16), kt[u], NT_DIMS,
                                     preferred_element_type=F32) for u, h in units}
        inc = {(u, h): lax.dot_general(vs[u, h], ke[u], TN_DIMS, preferred_element_type=F32) for u, h in units}
        o = {(u, h): jnp.dot(jnp.where(causal, a[u, h], 0.0).astype(BF16), vs[u, h], preferred_element_type=F32)
             for u, h in units}
        states = {}
        for h in range(2):
            st = st_ref[h]
            for u in range(GLA_UNROLL):
                states[u, h] = st
                st = st * e_last[u] + inc[u, h]
            st_ref[h] = st
        for u, h in units:
            cols = slice(h * GLA_DV, (h + 1) * GLA_DV)
            ou = o[u, h] + lax.dot_general(jnp.where(head_mask[h], qs[u], 0.0).astype(BF16),
                                           states[u, h].astype(BF16), NT_DIMS, preferred_element_type=F32)
            ms = jnp.mean(ou * ou, axis=-1, keepdims=True)
            on = ou * lax.rsqrt(ms + NORM_EPS) * gw_ref[...]
            g = g_ref[rows[u], cols].astype(F32)
            o_ref[rows[u], cols] = (on * _silu(g)).astype(o_ref.dtype)
        return carry

    lax.fori_loop(0, seq // (c * GLA_UNROLL), chunk_body, 0)


def _gla(proj3, ga3, wal, bal, gw):
    b, s, _ = proj3.shape
    npair = GLA_HEADS // 2
    qcol = 3 * MOBA_WIDTH // LANES
    kcol = qcol + GLA_KEY_WIDTH // LANES
    vcol = (3 * MOBA_WIDTH + 2 * GLA_KEY_WIDTH) // (2 * GLA_DV)
    gcol = vcol + npair
    return pl.pallas_call(
        _gla_kernel,
        grid=(b, npair),
        in_specs=[pl.BlockSpec((None, s, LANES), lambda bb, hp: (bb, 0, qcol + hp)),
                  pl.BlockSpec((None, s, LANES), lambda bb, hp: (bb, 0, kcol + hp)),
                  pl.BlockSpec((None, s, 2 * GLA_DV), lambda bb, hp: (bb, 0, vcol + hp)),
                  pl.BlockSpec((None, s, 2 * GLA_DV), lambda bb, hp: (bb, 0, gcol + hp)),
                  pl.BlockSpec((None, s, LANES), lambda bb, hp: (bb, 0, 0)),
                  pl.BlockSpec((LANES, LANES), lambda bb, hp: (0, hp)),
                  pl.BlockSpec((1, LANES), lambda bb, hp: (0, hp)),
                  pl.BlockSpec((1, GLA_DV), lambda bb, hp: (0, 0))],
        out_specs=pl.BlockSpec((None, s, 2 * GLA_DV), lambda bb, hp: (bb, 0, hp)),
        out_shape=jax.ShapeDtypeStruct((b, s, GLA_WIDTH), BF16),
        scratch_shapes=[pltpu.VMEM((s, LANES), F32),
                        pltpu.VMEM((2, GLA_DV, LANES), F32)],
        compiler_params=_params(2),
        name="gla",
    )(proj3, proj3, proj3, proj3, ga3, wal, bal, gw)


HALF = D_MODEL // 2


def _pack_rows(v):
    return pltpu.pack_elementwise([v[:, :HALF], v[:, HALF:]], packed_dtype=BF16)


def _unpack_rows(w):
    return (pltpu.unpack_elementwise(w, index=0, packed_dtype=BF16, unpacked_dtype=F32),
            pltpu.unpack_elementwise(w, index=1, packed_dtype=BF16, unpacked_dtype=F32))


def _outproj_kernel(oa_ref, ob_ref, x_ref, g1_ref, sc_ref, sh_ref, g2_ref, nw_ref, wo_ref,
                    ws1_ref, ws3_ref, ws2_ref, wrt_ref, base_ref, h_ref, st_ref):
    mix = (jnp.dot(oa_ref[...], wo_ref[:MOBA_WIDTH, :], preferred_element_type=F32)
           + jnp.dot(ob_ref[...], wo_ref[MOBA_WIDTH:, :], preferred_element_type=F32))
    x1 = x_ref[...] + g1_ref[...] * mix
    ms = jnp.mean(x1 * x1, axis=-1, keepdims=True)
    h = x1 * lax.rsqrt(ms + NORM_EPS) * nw_ref[...]
    h = h * (1.0 + sc_ref[...]) + sh_ref[...]
    h_ref[...] = _pack_rows(h)
    hb = h.astype(BF16)
    a = jnp.dot(hb, ws1_ref[...], preferred_element_type=F32)
    u = jnp.dot(hb, ws3_ref[...], preferred_element_type=F32)
    shared = jnp.dot((_silu(a) * u).astype(BF16), ws2_ref[...], preferred_element_type=F32)
    base_ref[...] = x1 + g2_ref[...] * shared
    logits_t = lax.dot_general(wrt_ref[...], hb, NT_DIMS, preferred_element_type=F32)
    st_ref[...] = jax.nn.sigmoid(logits_t)


def _outproj(oa, ob, x2, g1, sc, sh, g2, nw, wo, ws1, ws3, ws2, wrt, seq):
    t = x2.shape[0]
    tpb = seq // ROW_TILE
    vec = lambda: pl.BlockSpec((None, 1, D_MODEL), lambda i: (i // tpb, 0, 0))
    full = lambda a: pl.BlockSpec(a.shape, lambda i: (0,) * a.ndim)
    rows = lambda w: pl.BlockSpec((ROW_TILE, w), lambda i: (i, 0))
    return pl.pallas_call(
        _outproj_kernel,
        grid=(t // ROW_TILE,),
        in_specs=[rows(MOBA_WIDTH), rows(GLA_WIDTH), rows(D_MODEL), vec(), vec(), vec(), vec(),
                  full(nw), full(wo), full(ws1), full(ws3), full(ws2), full(wrt)],
        out_specs=[rows(D_MODEL), rows(HALF), pl.BlockSpec((N_EXPERTS, ROW_TILE), lambda i: (0, i))],
        out_shape=[jax.ShapeDtypeStruct((t, D_MODEL), F32),
                   jax.ShapeDtypeStruct((t, HALF), jnp.uint32),
                   jax.ShapeDtypeStruct((N_EXPERTS, t), F32)],
        compiler_params=_params(1),
        name="outproj",
    )(oa, ob, x2, g1, sc, sh, g2, nw, wo, ws1, ws3, ws2, wrt)


SLOT_CODE_SHIFT = 16
SLOT_CODE_BASE = 1 << SLOT_CODE_SHIFT


def _route_kernel(s_ref, eb_ref, code_ref, w_ref, cnt_ref, carry_ref):
    i = pl.program_id(0)
    ne, nt = s_ref.shape

    @pl.when(i == 0)
    def _init():
        carry_ref[...] = jnp.zeros_like(carry_ref)

    s = s_ref[...]
    choice = s + eb_ref[...]
    gio = lax.broadcasted_iota(jnp.int32, (GROUP_SIZE, nt), 0)
    gscore = []
    for g in range(N_GROUPS):
        cg = choice[g * GROUP_SIZE:(g + 1) * GROUP_SIZE, :]
        top1 = jnp.max(cg, axis=0, keepdims=True)
        first = jnp.min(jnp.where(cg == top1, gio, GROUP_SIZE), axis=0, keepdims=True)
        top2 = jnp.max(jnp.where(gio == first, -jnp.inf, cg), axis=0, keepdims=True)
        gscore.append(top1 + top2)
    gs = jnp.concatenate(gscore, axis=0)
    gidx = lax.broadcasted_iota(jnp.int32, gs.shape, 0)
    beaten = jnp.zeros(gs.shape, jnp.int32)
    for m in range(N_GROUPS):
        gm = gs[m:m + 1, :]
        beaten = beaten + jnp.where((gm > gs) | ((gm == gs) & (gidx > m)), 1, 0)
    gkeep = beaten < TOPK_GROUPS
    masked = jnp.concatenate(
        [jnp.where(gkeep[g:g + 1, :], choice[g * GROUP_SIZE:(g + 1) * GROUP_SIZE, :], -jnp.inf)
         for g in range(N_GROUPS)], axis=0)

    eio = lax.broadcasted_iota(jnp.int32, (ne, nt), 0)
    picked = jnp.zeros((ne, nt), F32)
    idx_rows, w_rows, hits = [], [], []
    for _ in range(TOP_K):
        mx = jnp.max(masked, axis=0, keepdims=True)
        idx = jnp.min(jnp.where(masked == mx, eio, ne), axis=0, keepdims=True)
        hit = eio == idx
        w_rows.append(jnp.sum(jnp.where(hit, s, 0.0), axis=0, keepdims=True))
        idx_rows.append(idx)
        hits.append(hit)
        masked = jnp.where(hit, -jnp.inf, masked)
        picked = jnp.where(hit, 1.0, picked)
    wk = jnp.concatenate(w_rows, axis=0)
    w_ref[...] = wk / jnp.sum(wk, axis=0, keepdims=True) * ROUTED_SCALE

    tr = lax.broadcasted_iota(jnp.int32, (nt, nt), 0)
    tc = lax.broadcasted_iota(jnp.int32, (nt, nt), 1)
    before = jnp.where(tr < tc, 1.0, 0.0).astype(BF16)
    pb = picked.astype(BF16)
    pos = carry_ref[...] + jnp.dot(pb, before, preferred_element_type=F32)
    rank = jnp.concatenate(
        [jnp.sum(jnp.where(hit, pos, 0.0), axis=0, keepdims=True) for hit in hits], axis=0).astype(jnp.int32)
    code_ref[...] = jnp.concatenate(idx_rows, axis=0) * SLOT_CODE_BASE + rank
    total = carry_ref[...] + jnp.dot(pb, jnp.ones((nt, nt), BF16), preferred_element_type=F32)
    carry_ref[...] = total
    cnt_ref[...] = total


def _route(scores_t, eb):
    ne, t = scores_t.shape
    assert t <= SLOT_CODE_BASE
    nt = ROUTE_TILE
    tok = lambda dt: jax.ShapeDtypeStruct((TOP_K, t), dt)
    return pl.pallas_call(
        _route_kernel,
        grid=(t // nt,),
        in_specs=[pl.BlockSpec((ne, nt), lambda i: (0, i)),
                  pl.BlockSpec((ne, nt), lambda i: (0, 0))],
        out_specs=[pl.BlockSpec((TOP_K, nt), lambda i: (0, i)),
                   pl.BlockSpec((TOP_K, nt), lambda i: (0, i)),
                   pl.BlockSpec((ne, nt), lambda i: (0, 0))],
        out_shape=[tok(jnp.int32), tok(F32), jax.ShapeDtypeStruct((ne, nt), F32)],
        scratch_shapes=[pltpu.VMEM((ne, nt), F32)],
        compiler_params=_params(1),
        name="route",
    )(scores_t, eb)


SLOT_TILE = 2048


def _slots_kernel(pstart_ref, code_ref, o_ref):
    code = code_ref[...]
    expert = lax.shift_right_logical(code, SLOT_CODE_SHIFT)

    def body(e, acc):
        return jnp.where(expert == e, pstart_ref[e], acc)

    start = lax.fori_loop(0, N_EXPERTS, body, jnp.zeros_like(code), unroll=8)
    o_ref[...] = start + (code & (SLOT_CODE_BASE - 1))


def _slots(pstart, code_t):
    k, t = code_t.shape
    return pl.pallas_call(
        _slots_kernel,
        grid_spec=pltpu.PrefetchScalarGridSpec(
            num_scalar_prefetch=1,
            grid=(t // SLOT_TILE,),
            in_specs=[pl.BlockSpec((k, SLOT_TILE), lambda i, p: (0, i))],
            out_specs=pl.BlockSpec((k, SLOT_TILE), lambda i, p: (0, i)),
        ),
        out_shape=jax.ShapeDtypeStruct((k, t), jnp.int32),
        compiler_params=_params(1),
        name="slots",
    )(pstart, code_t)


SC_WINDOW = 128


def _sc_gather_rows(table, idx_flat):
    info = plsc.get_sparse_core_info()
    nw = info.num_cores * info.num_subcores
    n = idx_flat.shape[0]
    width = table.shape[1]
    per_worker = n // nw
    assert per_worker * nw == n and per_worker % SC_WINDOW == 0
    mesh = plsc.VectorSubcoreMesh(core_axis_name="c", subcore_axis_name="s")

    def body(table_hbm, idx_hbm, out_hbm, idx_v, rows_v, sem):
        wid = lax.axis_index("s") * info.num_cores + lax.axis_index("c")
        base = wid * per_worker

        @pl.loop(0, per_worker // SC_WINDOW)
        def _(w):
            off = pl.multiple_of(base + w * SC_WINDOW, SC_WINDOW)
            pltpu.sync_copy(idx_hbm.at[pl.ds(off, SC_WINDOW)], idx_v)
            pltpu.async_copy(table_hbm.at[idx_v], rows_v, sem).wait()
            pltpu.sync_copy(rows_v, out_hbm.at[pl.ds(off, SC_WINDOW)])

    return pl.kernel(
        body,
        out_type=jax.ShapeDtypeStruct((n, width), table.dtype),
        mesh=mesh,
        scratch_types=[pltpu.VMEM((SC_WINDOW,), jnp.int32),
                       pltpu.VMEM((SC_WINDOW, width), table.dtype),
                       pltpu.SemaphoreType.DMA],
        name="sc_gather",
    )(table, idx_flat)


def _sc_scatter_rows(rows, idx_kt, n_out):
    info = plsc.get_sparse_core_info()
    nw = info.num_cores * info.num_subcores
    t, width = rows.shape
    nk = idx_kt.shape[0]
    per_worker = t // nw
    assert per_worker * nw == t and per_worker % SC_WINDOW == 0
    mesh = plsc.VectorSubcoreMesh(core_axis_name="c", subcore_axis_name="s")

    def body(rows_hbm, idx_hbm, out_hbm, idx_v, rows_v, sem):
        wid = lax.axis_index("s") * info.num_cores + lax.axis_index("c")
        base = wid * per_worker

        @pl.loop(0, per_worker // SC_WINDOW)
        def _(w):
            off = pl.multiple_of(base + w * SC_WINDOW, SC_WINDOW)
            pltpu.sync_copy(rows_hbm.at[pl.ds(off, SC_WINDOW)], rows_v)
            pltpu.sync_copy(idx_hbm.at[:, pl.ds(off, SC_WINDOW)], idx_v)
            copies = [pltpu.async_copy(rows_v, out_hbm.at[idx_v.at[k]], sem) for k in range(nk)]
            for cp in copies:
                cp.wait()

    return pl.kernel(
        body,
        out_type=jax.ShapeDtypeStruct((n_out, width), rows.dtype),
        mesh=mesh,
        scratch_types=[pltpu.VMEM((nk, SC_WINDOW), jnp.int32),
                       pltpu.VMEM((SC_WINDOW, width), rows.dtype),
                       pltpu.SemaphoreType.DMA],
        name="sc_scatter",
    )(rows, idx_kt)


def _expert_kernel(first_ref, count_ref, used_ref, w1_ref, w3_ref, w2_ref, xs_ref, ys_ref,
                   xbuf, ybuf, sem_in, sem_out, w1f, w3f, w2f, sem_w, w1b, w3b, w2b, *, layer):
    e = pl.program_id(0)
    ne = pl.num_programs(0)
    r = xbuf.shape[1]
    n_used = used_ref[0]

    def x_copy(g, slot):
        return pltpu.make_async_copy(xs_ref.at[pl.ds(pl.multiple_of(g * r, r), r), :], xbuf.at[slot], sem_in.at[slot])

    def y_copy(g, slot):
        return pltpu.make_async_copy(ybuf.at[slot], ys_ref.at[pl.ds(pl.multiple_of(g * r, r), r), :], sem_out.at[slot])

    def w_copies(ex, slot):
        return [pltpu.make_async_copy(src.at[layer, ex], dst.at[slot], sem_w.at[slot])
                for src, dst in ((w1_ref, w1f), (w3_ref, w3f), (w2_ref, w2f))]

    nin = xbuf.shape[0]
    nout = ybuf.shape[0]

    nw = w1f.shape[0]

    @pl.when(e == 0)
    def _first_reads():
        for g in range(nin):
            @pl.when(g < n_used)
            def _(g=g):
                x_copy(g, g).start()
        for ex in range(nw - 1):
            for cp in w_copies(ex, ex):
                cp.start(priority=1)

    @pl.when(e + nw - 1 < ne)
    def _weights_ahead():
        for cp in w_copies(e + nw - 1, (e + nw - 1) % nw):
            cp.start(priority=1)

    wslot = e % nw
    for cp in w_copies(e, wslot):
        cp.wait()

    n = count_ref[e]

    @pl.when(n > 0)
    def _cast_weights():
        w1b[...] = w1f[wslot].astype(BF16)
        w3b[...] = w3f[wslot].astype(BF16)
        w2b[...] = w2f[wslot].astype(BF16)

    def run_tiles(g, count):
        tiles = [g + j for j in range(count)]
        for gj in tiles:
            x_copy(gj, gj % nin).wait()
        x = jnp.concatenate([xbuf[gj % nin] for gj in tiles], axis=0)
        for gj in tiles:
            @pl.when(gj + nin < n_used)
            def _(gj=gj):
                x_copy(gj + nin, gj % nin).start()
        lo, hi = _unpack_rows(x)
        lo, hi = lo.astype(BF16), hi.astype(BF16)
        a = (jnp.dot(lo, w1b[:HALF, :], preferred_element_type=F32)
             + jnp.dot(hi, w1b[HALF:, :], preferred_element_type=F32))
        u = (jnp.dot(lo, w3b[:HALF, :], preferred_element_type=F32)
             + jnp.dot(hi, w3b[HALF:, :], preferred_element_type=F32))
        y = _pack_rows(jnp.dot((_silu(a) * u).astype(BF16), w2b[...], preferred_element_type=F32))
        for j, gj in enumerate(tiles):
            @pl.when(gj >= nout)
            def _(gj=gj):
                y_copy(gj - nout, gj % nout).wait()
            ybuf[gj % nout] = y[j * r:(j + 1) * r]
            y_copy(gj, gj % nout).start()

    g0 = first_ref[e]
    pair = EXPERT_TILES_PER_MATMUL

    def pair_body(p, carry):
        run_tiles(g0 + p * pair, pair)
        return carry

    lax.fori_loop(0, n // pair, pair_body, 0)
    for left in range(1, pair):
        pl.when(n % pair == left)(functools.partial(run_tiles, g0 + n - left, left))

    @pl.when(e == ne - 1)
    def _drain_writes():
        for back in range(nout, 0, -1):
            @pl.when(n_used >= back)
            def _(back=back):
                y_copy(n_used - back, (n_used - back) % nout).wait()


def _experts(tile_first, tile_count, n_used, xs, w1, w3, w2, layer):
    n_rows = xs.shape[0]
    r = EXPERT_TILE
    any_spec = pl.BlockSpec(memory_space=pl.ANY)
    return pl.pallas_call(
        functools.partial(_expert_kernel, layer=layer),
        grid_spec=pltpu.PrefetchScalarGridSpec(
            num_scalar_prefetch=3,
            grid=(N_EXPERTS,),
            in_specs=[any_spec, any_spec, any_spec, any_spec],
            out_specs=any_spec,
            scratch_shapes=[pltpu.VMEM((EXPERT_IN_RING, r, HALF), jnp.uint32),
                            pltpu.VMEM((EXPERT_OUT_RING, r, HALF), jnp.uint32),
                            pltpu.SemaphoreType.DMA((EXPERT_IN_RING,)),
                            pltpu.SemaphoreType.DMA((EXPERT_OUT_RING,)),
                            pltpu.VMEM((EXPERT_WEIGHT_RING, D_MODEL, EXPERT_FF), F32),
                            pltpu.VMEM((EXPERT_WEIGHT_RING, D_MODEL, EXPERT_FF), F32),
                            pltpu.VMEM((EXPERT_WEIGHT_RING, EXPERT_FF, D_MODEL), F32),
                            pltpu.SemaphoreType.DMA((EXPERT_WEIGHT_RING,)),
                            pltpu.VMEM((D_MODEL, EXPERT_FF), BF16),
                            pltpu.VMEM((D_MODEL, EXPERT_FF), BF16),
                            pltpu.VMEM((EXPERT_FF, D_MODEL), BF16)],
        ),
        out_shape=jax.ShapeDtypeStruct((n_rows, HALF), jnp.uint32),
        compiler_params=_params(1),
        name="experts",
    )(tile_first, tile_count, n_used, w1, w3, w2, xs)


def _combine_dense_kernel(base_ref, g2_ref, w_ref, yg_ref, o_ref):
    acc_lo = acc_hi = None
    for k in range(TOP_K):
        lo, hi = _unpack_rows(yg_ref[k])
        wk = w_ref[:, k:k + 1]
        acc_lo = wk * lo if acc_lo is None else acc_lo + wk * lo
        acc_hi = wk * hi if acc_hi is None else acc_hi + wk * hi
    o_ref[:, :HALF] = base_ref[:, :HALF] + g2_ref[:, :HALF] * acc_lo
    o_ref[:, HALF:] = base_ref[:, HALF:] + g2_ref[:, HALF:] * acc_hi


def _combine_dense(base, g2, w_tok, yg, batch):
    t = base.shape[0]
    seq = yg.shape[1]
    nt = ROUTE_TILE
    tpb = seq // nt
    rows = lambda i: (batch * tpb + i, 0)
    return pl.pallas_call(
        _combine_dense_kernel,
        grid=(tpb,),
        in_specs=[pl.BlockSpec((nt, D_MODEL), rows),
                  pl.BlockSpec((None, 1, D_MODEL), lambda i: (batch, 0, 0)),
                  pl.BlockSpec((nt, TOP_K), rows),
                  pl.BlockSpec((TOP_K, nt, HALF), lambda i: (0, i, 0))],
        out_specs=pl.BlockSpec((nt, D_MODEL), rows),
        out_shape=jax.ShapeDtypeStruct((t, D_MODEL), F32),
        input_output_aliases={0: 0},
        compiler_params=_params(1),
        name="combine_dense",
    )(base, g2, w_tok, yg)


def _layer(layer, x, c, w_ada, b_ada, norm1_w, norm2_w, w_in, q_norm_w, k_norm_w, rel_bias, w_alpha, b_alpha,
           moba_out_w, gla_out_w, w_out, w_router, e_bias, w1, w3, w2, ws1, ws3, ws2):
    b, s, d = x.shape
    t = b * s
    x2 = x.reshape(t, d)

    mod = _mod(c, w_ada, b_ada)
    sh1, sc1, g1, sh2, sc2, g2 = [mod[:, j * d:(j + 1) * d].reshape(b, 1, d) for j in range(6)]

    w_main = w_in[:, :D_MAIN].astype(BF16)
    w_ga = jnp.zeros((d, LANES), BF16).at[:, :GLA_GATE_RANK].set(w_in[:, D_MAIN:].astype(BF16))
    per_chunk = 256 // MOBA_HEAD_DIM
    qw = jnp.tile(q_norm_w.astype(F32), per_chunk).reshape(1, 256) * (MOBA_HEAD_DIM ** -0.5 * LOG2E)
    kw = jnp.tile(k_norm_w.astype(F32), per_chunk).reshape(1, 256)
    proj, ga = _inproj(x2, sc1, sh1, norm1_w.reshape(1, d), w_main, w_ga, qw, kw, s)
    proj3 = proj.reshape(b, s, D_MAIN)

    near, far = _moba_bias_tables(rel_bias)
    ow = jnp.tile(moba_out_w.astype(F32), 2).reshape(1, LANES)
    o_a = _moba(proj3, near, far, ow)

    wal = jnp.zeros((LANES, GLA_KEY_WIDTH), F32).at[:GLA_GATE_RANK].set(w_alpha)
    o_b = _gla(proj3, ga.reshape(b, s, LANES), wal, b_alpha.reshape(1, GLA_KEY_WIDTH),
               gla_out_w.reshape(1, GLA_DV))

    base, h2, scores_t = _outproj(
        o_a.reshape(t, MOBA_WIDTH), o_b.reshape(t, GLA_WIDTH), x2, g1, sc2, sh2, g2,
        norm2_w.reshape(1, d), w_out.astype(BF16), ws1.astype(BF16), ws3.astype(BF16), ws2.astype(BF16),
        w_router.T.astype(BF16), s)

    eb = jnp.broadcast_to(e_bias.astype(F32)[:, None], (N_EXPERTS, ROUTE_TILE))
    code_t, w_t, counts = _route(scores_t, eb)

    r = EXPERT_TILE
    n_tiles = (t * TOP_K + N_EXPERTS * (r - 1) + r - 1) // r
    n_rows = n_tiles * r
    cnt = counts[:, 0].astype(jnp.int32)
    padded = (cnt + r - 1) // r * r
    pend = jnp.cumsum(padded)
    pstart = pend - padded
    n_used = (pend[-1:] // r).astype(jnp.int32)
    dest_t = _slots(pstart, code_t)

    xs = _sc_scatter_rows(h2, dest_t, n_rows)
    ys = _experts(pstart // r, padded // r, n_used, xs, w1, w3, w2, layer)
    w_tok = w_t.T
    out = base
    for bi in range(b):
        idx = dest_t[:, bi * s:(bi + 1) * s].reshape(TOP_K * s)
        yg = _sc_gather_rows(ys, idx).reshape(TOP_K, s, HALF)
        out = _combine_dense(out, g2, w_tok, yg, bi)
    return out.reshape(b, s, d)


def kernel(x, c, w_ada, b_ada, norm1_w, norm2_w, w_in, q_norm_w, k_norm_w, rel_bias, w_alpha, b_alpha,
           moba_out_w, gla_out_w, w_out, w_router, e_bias, w1, w3, w2, ws1, ws3, ws2):
    for l in range(w_ada.shape[0]):
        x = _layer(l, x, c, w_ada[l], b_ada[l], norm1_w[l], norm2_w[l], w_in[l], q_norm_w[l], k_norm_w[l],
                   rel_bias, w_alpha[l], b_alpha[l], moba_out_w[l], gla_out_w[l], w_out[l], w_router[l],
                   e_bias[l], w1, w3, w2, ws1[l], ws3[l], ws2[l])
    return x
```

```python
import functools
import math

import numpy as np
import jax
import jax.numpy as jnp
from jax import lax
from jax.experimental import pallas as pl
from jax.experimental.pallas import tpu as pltpu
from jax.experimental.pallas import tpu_sc as plsc

D_MODEL = 1024
MOBA_HEADS = 8
MOBA_HEAD_DIM = 64
MOBA_WIDTH = MOBA_HEADS * MOBA_HEAD_DIM
MOBA_BLOCK = 256
MOBA_TOPK = 3
GLA_HEADS = 4
GLA_DK = 64
GLA_DV = 128
GLA_KEY_WIDTH = GLA_HEADS * GLA_DK
GLA_WIDTH = GLA_HEADS * GLA_DV
GLA_GATE_RANK = 16
GLA_GATE_TAU = 16.0
GLA_CHUNK = 64
REL_BUCKETS = 32
REL_MAX_DIST = 128
N_EXPERTS = 256
TOP_K = 8
N_GROUPS = 8
TOPK_GROUPS = 4
GROUP_SIZE = N_EXPERTS // N_GROUPS
EXPERT_FF = 256
SHARED_FF = 256
ROUTED_SCALE = 2.5
NORM_EPS = 1e-6
LOG2E = math.log2(math.e)

D_MAIN = 3 * MOBA_WIDTH + 2 * GLA_KEY_WIDTH + 2 * GLA_WIDTH
LANES = 128
VMEM_LIMIT = 56 * 1024 * 1024

ROW_TILE = 512
ROUTE_TILE = 256
EXPERT_TILE = 128
EXPERT_TILES_PER_MATMUL = 6
EXPERT_IN_RING = 16
EXPERT_OUT_RING = 12
EXPERT_WEIGHT_RING = 3

F32 = jnp.float32
BF16 = jnp.bfloat16
NT_DIMS = (((1,), (1,)), ((), ()))
TN_DIMS = (((0,), (0,)), ((), ()))


def _params(n_axes):
    return pltpu.CompilerParams(dimension_semantics=("arbitrary",) * n_axes,
                                vmem_limit_bytes=VMEM_LIMIT)


def _silu(v):
    return v * jax.nn.sigmoid(v)


def _mod_kernel(c_ref, w_ref, b_ref, o_ref):
    o_ref[...] = jnp.dot(_silu(c_ref[...]), w_ref[...], preferred_element_type=F32) + b_ref[...]


def _mod(c, w, b):
    rows = 8
    cp = jnp.zeros((rows, D_MODEL), F32).at[:c.shape[0]].set(c)
    n = w.shape[1]
    tn = 1024
    out = pl.pallas_call(
        _mod_kernel,
        grid=(n // tn,),
        in_specs=[pl.BlockSpec((rows, D_MODEL), lambda j: (0, 0)),
                  pl.BlockSpec((D_MODEL, tn), lambda j: (0, j)),
                  pl.BlockSpec((1, tn), lambda j: (0, j))],
        out_specs=pl.BlockSpec((rows, tn), lambda j: (0, j)),
        out_shape=jax.ShapeDtypeStruct((rows, n), F32),
        compiler_params=_params(1),
        name="mod",
    )(cp, w, b.reshape(1, n))
    return out[:c.shape[0]]


def _group_rms_inv(a, group):
    lane = lax.broadcasted_iota(jnp.int32, (1, a.shape[1]), 1)
    a2 = a * a
    inv = jnp.zeros_like(a)
    for g in range(a.shape[1] // group):
        m = (lane >= g * group) & (lane < (g + 1) * group)
        ss = jnp.sum(jnp.where(m, a2, 0.0), axis=-1, keepdims=True)
        inv = jnp.where(m, lax.rsqrt(ss * (1.0 / group) + NORM_EPS), inv)
    return inv


def _inproj_kernel(x_ref, sc_ref, sh_ref, nw_ref, w_ref, wga_ref, qw_ref, kw_ref, o_ref, ga_ref):
    x = x_ref[...]
    ms = jnp.mean(x * x, axis=-1, keepdims=True)
    h = x * lax.rsqrt(ms + NORM_EPS) * nw_ref[...]
    h = h * (1.0 + sc_ref[...]) + sh_ref[...]
    hb = h.astype(BF16)
    cw = 256
    for j in range(D_MAIN // cw):
        acc = jnp.dot(hb, w_ref[:, j * cw:(j + 1) * cw], preferred_element_type=F32)
        if j < 2 * MOBA_WIDTH // cw:
            nw = qw_ref if j < MOBA_WIDTH // cw else kw_ref
            acc = acc * _group_rms_inv(acc, MOBA_HEAD_DIM) * nw[...]
        o_ref[:, j * cw:(j + 1) * cw] = acc.astype(BF16)
    ga_ref[...] = jnp.dot(hb, wga_ref[...], preferred_element_type=F32)


def _inproj(x2, sc, sh, nw, w_main, w_ga, qw, kw, seq):
    t = x2.shape[0]
    tpb = seq // ROW_TILE
    vec = lambda: pl.BlockSpec((None, 1, D_MODEL), lambda i: (i // tpb, 0, 0))
    full = lambda a: pl.BlockSpec(a.shape, lambda i: (0,) * a.ndim)
    return pl.pallas_call(
        _inproj_kernel,
        grid=(t // ROW_TILE,),
        in_specs=[pl.BlockSpec((ROW_TILE, D_MODEL), lambda i: (i, 0)), vec(), vec(),
                  full(nw), full(w_main), full(w_ga), full(qw), full(kw)],
        out_specs=[pl.BlockSpec((ROW_TILE, D_MAIN), lambda i: (i, 0)),
                   pl.BlockSpec((ROW_TILE, LANES), lambda i: (i, 0))],
        out_shape=[jax.ShapeDtypeStruct((t, D_MAIN), BF16),
                   jax.ShapeDtypeStruct((t, LANES), F32)],
        compiler_params=_params(1),
        name="inproj",
    )(x2, sc, sh, nw, w_main, w_ga, qw, kw)


def _t5_bucket_np(rel):
    max_exact = REL_BUCKETS // 2
    relf = np.maximum(rel, 1).astype(np.float64)
    large = max_exact + (np.log(relf / max_exact) / math.log(REL_MAX_DIST / max_exact)
                         * (REL_BUCKETS - max_exact)).astype(np.int32)
    large = np.minimum(large, REL_BUCKETS - 1)
    return np.where(rel < max_exact, rel, large)


def _bias_kernel(rb_ref, idx_ref, o_ref):
    h = pl.program_id(0)
    idx = idx_ref[...]
    tab = jnp.full(idx.shape, -jnp.inf, F32)
    for bk in range(REL_BUCKETS):
        tab = jnp.where(idx == bk, rb_ref[bk * MOBA_HEADS + h], tab)
    o_ref[...] = tab


def _moba_bias_tables(rel_bias):
    j = np.arange(MOBA_BLOCK)[:, None]
    i = np.arange(MOBA_BLOCK)[None, :]
    own_idx = np.where(j <= i, _t5_bucket_np(np.maximum(i - j, 0)), -1)
    prev_idx = _t5_bucket_np(MOBA_BLOCK + i - j)
    idx = jnp.asarray(np.concatenate([prev_idx, own_idx], axis=0).astype(np.int32))
    assert int(_t5_bucket_np(np.array([MOBA_BLOCK + 1]))[0]) == REL_BUCKETS - 1
    rb = rel_bias.astype(F32) * LOG2E
    near = pl.pallas_call(
        _bias_kernel,
        grid=(MOBA_HEADS,),
        in_specs=[pl.BlockSpec(memory_space=pltpu.SMEM),
                  pl.BlockSpec(idx.shape, lambda h: (0, 0))],
        out_specs=pl.BlockSpec((None,) + idx.shape, lambda h: (h, 0, 0)),
        out_shape=jax.ShapeDtypeStruct((MOBA_HEADS,) + idx.shape, F32),
        compiler_params=_params(1),
        name="bias",
    )(rb.reshape(-1), idx)
    return near, rb[REL_BUCKETS - 1]


FAR_GROUP = 4


def _moba_kernel(*refs):
    hp = pl.program_id(1)
    _moba_body(None, hp, *refs, prepare=True)
    _moba_body(None, hp, *refs, prepare=False)

    def block_pair(j, carry):
        _moba_body(2 * j, hp, *refs, prepare=False)
        return carry

    lax.fori_loop(1, refs[2].shape[0] // (2 * MOBA_BLOCK), block_pair, 0)


def _moba_body(i, hp, far_ref, q_ref, k_ref, v_ref, near_ref, ow_ref, o_ref,
               vt_ref, vtg_ref, acc_ref, m_ref, sel_ref, s_ref, mx_ref, *, prepare):
    nblk = k_ref.shape[0] // MOBA_BLOCK
    ngrp = nblk // FAR_GROUP
    hd = MOBA_HEAD_DIM
    bs = MOBA_BLOCK
    lane = lax.broadcasted_iota(jnp.int32, (bs, LANES), 1)

    def split_heads(qb):
        zero = jnp.zeros_like(qb)
        return jnp.where(lane < hd, qb, zero), jnp.where(lane < hd, zero, qb)

    def _prepare():
        row = lax.broadcasted_iota(jnp.int32, (LANES, bs), 0)
        kmeans = []
        for n in range(nblk):
            kb = k_ref[n * bs:(n + 1) * bs, :].astype(F32)
            kmeans.append(jnp.mean(kb, axis=0, keepdims=True))
            vt = v_ref[n * bs:(n + 1) * bs, :].astype(F32).T
            vt0 = jnp.where(row < hd, vt, 1.0).astype(BF16)
            vt1 = jnp.where(row < hd, 1.0, vt).astype(BF16)
            vt_ref[0, n] = vt0
            vt_ref[1, n] = vt1
            gcols = slice((n % FAR_GROUP) * bs, (n % FAR_GROUP + 1) * bs)
            vtg_ref[0, n // FAR_GROUP, :, gcols] = vt0
            vtg_ref[1, n // FAR_GROUP, :, gcols] = vt1
        kmean = jnp.concatenate(kmeans, axis=0)
        km_hi = kmean.astype(BF16)
        km_lo = (kmean - km_hi.astype(F32)).astype(BF16)
        blk = lax.broadcasted_iota(jnp.int32, (nblk, bs), 0)
        for ib in range(nblk):
            qparts = split_heads(q_ref[ib * bs:(ib + 1) * bs, :])
            for h in range(2):
                gt = (lax.dot_general(km_hi, qparts[h], NT_DIMS, preferred_element_type=F32)
                      + lax.dot_general(km_lo, qparts[h], NT_DIMS, preferred_element_type=F32))
                gt = jnp.where(blk < ib, gt, -jnp.inf)
                cnt = jnp.zeros(gt.shape, jnp.int32)
                for m in range(ib):
                    gm = gt[m:m + 1, :]
                    cnt = cnt + jnp.where((gm > gt) | ((gm == gt) & (blk > m)), 1, 0)
                keep = (blk < ib) & (cnt < MOBA_TOPK)
                sel_ref[0, h, ib] = jnp.where(keep, 1.0, 0.0)
                sel_ref[1, h, ib] = jnp.where(keep & (blk < ib - 1), 1.0, 0.0)

    if prepare:
        _prepare()
        return

    gk = FAR_GROUP * bs

    def rows_of(ib):
        if isinstance(ib, int):
            return slice(ib * bs, (ib + 1) * bs)
        return pl.ds(pl.multiple_of(ib * bs, bs), bs)

    def finish(ib, slot):
        a0 = acc_ref[slot, 0]
        a1 = acc_ref[slot, 1]
        row = lax.broadcasted_iota(jnp.int32, a0.shape, 0)
        ot = jnp.where(row < hd, a0 / a0[hd:hd + 1, :], a1 / a1[0:1, :])
        o2 = ot * ot
        ss0 = jnp.sum(jnp.where(row < hd, o2, 0.0), axis=0, keepdims=True)
        ss1 = jnp.sum(jnp.where(row < hd, 0.0, o2), axis=0, keepdims=True)
        inv = jnp.where(row < hd, lax.rsqrt(ss0 * (1.0 / hd) + NORM_EPS), lax.rsqrt(ss1 * (1.0 / hd) + NORM_EPS))
        o_ref[rows_of(ib), :] = ((ot * inv).T * ow_ref[...]).astype(o_ref.dtype)

    def far_scores(g, slot, qh):
        kb = k_ref[g * gk:(g + 1) * gk, :]
        for h in range(2):
            s = lax.dot_general(kb, qh[h], NT_DIMS, preferred_element_type=F32)
            s_ref[slot, g % 2, h] = s
            for j in range(FAR_GROUP):
                mx_ref[slot, g % 2, h, j] = jnp.max(s[j * bs:(j + 1) * bs], axis=0, keepdims=True)

    def near_scores(ib, qh):
        kbs = (k_ref[rows_of(ib - 1), :], k_ref[rows_of(ib), :])
        return [[lax.dot_general(kbs[w], qh[h], NT_DIMS, preferred_element_type=F32)
                 + near_ref[h, w * bs:(w + 1) * bs, :] for w in range(2)] for h in range(2)]

    def near_values(ib, slot, ss):
        ps, ms = [], []
        for h in range(2):
            s_prev, s_own = ss[h]
            keep = sel_ref[0, h, ib, pl.ds(ib - 1, 1), :] > 0.5
            mx = jnp.where(keep, jnp.max(s_prev, axis=0, keepdims=True), -jnp.inf)
            m_new = jnp.maximum(jnp.max(s_own, axis=0, keepdims=True), mx)
            ps.append((jnp.exp2(s_prev - jnp.where(keep, m_new, jnp.inf)).astype(BF16),
                       jnp.exp2(s_own - m_new).astype(BF16)))
            ms.append(m_new)
        for h in range(2):
            acc_ref[slot, h] = (jnp.dot(vt_ref[h, ib - 1], ps[h][0], preferred_element_type=F32)
                                + jnp.dot(vt_ref[h, ib], ps[h][1], preferred_element_type=F32))
            m_ref[slot, h] = ms[h]

    def far_group(g, ib, slot):
        for h in range(2):
            fb = far_ref[2 * hp + h]
            m_old = m_ref[slot, h]
            m_new = m_old
            keeps = []
            for j in range(FAR_GROUP):
                keep = sel_ref[1, h, ib, pl.ds(g * FAR_GROUP + j, 1), :] > 0.5
                m_new = jnp.maximum(m_new, jnp.where(keep, mx_ref[slot, g % 2, h, j] + fb, -jnp.inf))
                keeps.append(keep)
            p = jnp.concatenate(
                [jnp.exp2(s_ref[slot, g % 2, h, j * bs:(j + 1) * bs, :]
                          - jnp.where(keeps[j], m_new - fb, jnp.inf)).astype(BF16)
                 for j in range(FAR_GROUP)], axis=0)
            pv = jnp.dot(vtg_ref[h, g], p, preferred_element_type=F32)
            acc_ref[slot, h] = acc_ref[slot, h] * jnp.exp2(m_old - m_new) + pv
            m_ref[slot, h] = m_new

    def step_body(nf, blocks):
        qhs = {slot: split_heads(q_ref[rows_of(ib), :]) for ib, slot in blocks}
        near = {slot: near_scores(ib, qhs[slot]) for ib, slot in blocks}
        if nf > 0:
            for ib, slot in blocks:
                far_scores(0, slot, qhs[slot])
        for ib, slot in blocks:
            near_values(ib, slot, near[slot])
        for g in range(nf):
            if g + 1 < nf:
                for ib, slot in blocks:
                    far_scores(g + 1, slot, qhs[slot])
            for ib, slot in blocks:
                far_group(g, ib, slot)
        for ib, slot in blocks:
            finish(ib, slot)

    if i is None:
        qh = split_heads(q_ref[0:bs, :])
        kb = k_ref[0:bs, :]
        for h in range(2):
            s = lax.dot_general(kb, qh[h], NT_DIMS, preferred_element_type=F32) + near_ref[h, bs:2 * bs, :]
            p = jnp.exp2(s - jnp.max(s, axis=0, keepdims=True)).astype(BF16)
            acc_ref[0, h] = jnp.dot(vt_ref[h, 0], p, preferred_element_type=F32)
        finish(0, 0)
        step_body(0, [(1, 1)])
        return

    n_far = (i + FAR_GROUP - 2) // FAR_GROUP
    for nf in range(1, ngrp + 1):
        pl.when(n_far == nf)(functools.partial(step_body, nf, [(i, 0), (i + 1, 1)]))


def _moba(proj3, near, far, ow):
    b, s, _ = proj3.shape
    nblk = s // MOBA_BLOCK
    assert nblk % FAR_GROUP == 0
    npair = MOBA_HEADS // 2
    kcol = MOBA_WIDTH // LANES
    return pl.pallas_call(
        _moba_kernel,
        grid=(b, npair),
        in_specs=[pl.BlockSpec(memory_space=pltpu.SMEM),
                  pl.BlockSpec((None, s, LANES), lambda bb, hp: (bb, 0, hp)),
                  pl.BlockSpec((None, s, LANES), lambda bb, hp: (bb, 0, kcol + hp)),
                  pl.BlockSpec((None, s, LANES), lambda bb, hp: (bb, 0, 2 * kcol + hp)),
                  pl.BlockSpec((2, 2 * MOBA_BLOCK, MOBA_BLOCK), lambda bb, hp: (hp, 0, 0)),
                  pl.BlockSpec((1, LANES), lambda bb, hp: (0, 0))],
        out_specs=pl.BlockSpec((None, s, LANES), lambda bb, hp: (bb, 0, hp)),
        out_shape=jax.ShapeDtypeStruct((b, s, MOBA_WIDTH), BF16),
        scratch_shapes=[pltpu.VMEM((2, nblk, LANES, MOBA_BLOCK), BF16),
                        pltpu.VMEM((2, nblk // FAR_GROUP, LANES, FAR_GROUP * MOBA_BLOCK), BF16),
                        pltpu.VMEM((2, 2, LANES, MOBA_BLOCK), F32),
                        pltpu.VMEM((2, 2, 1, MOBA_BLOCK), F32),
                        pltpu.VMEM((2, 2, nblk, nblk, MOBA_BLOCK), F32),
                        pltpu.VMEM((2, 2, 2, FAR_GROUP * MOBA_BLOCK, MOBA_BLOCK), F32),
                        pltpu.VMEM((2, 2, 2, FAR_GROUP, 1, MOBA_BLOCK), F32)],
        compiler_params=_params(2),
        name="moba",
    )(far, proj3, proj3, proj3, near, ow)


def _split3(v):
    hi = v.astype(BF16)
    r1 = v - hi.astype(F32)
    mid = r1.astype(BF16)
    lo = (r1 - mid.astype(F32)).astype(BF16)
    return hi, mid, lo


GLA_UNROLL = 16


def _gla_kernel(q_ref, k_ref, v_ref, g_ref, ga_ref, wal_ref, bal_ref, gw_ref, o_ref, b_ref, st_ref):
    seq = q_ref.shape[0]
    c = GLA_CHUNK
    pc = 256

    rr = lax.broadcasted_iota(jnp.int32, (pc, pc), 0)
    cc = lax.broadcasted_iota(jnp.int32, (pc, pc), 1)
    tri = jnp.where((rr >= cc) & (rr // c == cc // c), 1.0, 0.0).astype(BF16)

    def decay_body(j, carry):
        rows = [pl.ds(pl.multiple_of((j * GLA_UNROLL + u) * pc, pc), pc) for u in range(GLA_UNROLL)]
        xg = [jnp.dot(ga_ref[r, :], wal_ref[...], preferred_element_type=F32) + bal_ref[...] for r in rows]
        parts = [_split3((jnp.minimum(x, 0.0) - jnp.log(1.0 + jnp.exp(-jnp.abs(x)))) * (1.0 / GLA_GATE_TAU))
                 for x in xg]
        sums = [[jnp.dot(tri, term, preferred_element_type=F32) for term in p] for p in parts]
        for r, (hi, mid, lo) in zip(rows, sums):
            b_ref[r, :] = hi + mid + lo
        return carry

    lax.fori_loop(0, seq // (pc * GLA_UNROLL), decay_body, 0)

    st_ref[...] = jnp.zeros_like(st_ref)
    lane = lax.broadcasted_iota(jnp.int32, (c, LANES), 1)
    head_mask = (lane < GLA_DK, lane >= GLA_DK)
    causal = lax.broadcasted_iota(jnp.int32, (c, c), 0) >= lax.broadcasted_iota(jnp.int32, (c, c), 1)

    units = [(u, h) for u in range(GLA_UNROLL) for h in range(2)]

    def chunk_body(ci, carry):
        rows = [pl.ds(pl.multiple_of((ci * GLA_UNROLL + u) * c, c), c) for u in range(GLA_UNROLL)]
        qt, kt, qs, ke, e_last = [], [], [], [], []
        for u in range(GLA_UNROLL):
            b = b_ref[rows[u], :]
            ref_row = b[c // 2 - 1:c // 2, :]
            last = b[c - 1:c, :]
            q = q_ref[rows[u], :].astype(F32) * (GLA_DK ** -0.5)
            k = k_ref[rows[u], :].astype(F32)
            qt.append(q * jnp.exp(b - ref_row))
            kt.append((k * jnp.exp(ref_row - b)).astype(BF16))
            qs.append(q * jnp.exp(b))
            ke.append((k * jnp.exp(last - b)).astype(BF16))
            e_last.append(jnp.exp(last))
        vs = {(u, h): v_ref[rows[u], h * GLA_DV:(h + 1) * GLA_DV] for u, h in units}
        a = {(u, h): lax.dot_general(jnp.where(head_mask[h], qt[u], 0.0).astype(BF16), kt[u], NT_DIMS,
                                     preferred_element_type=F32) for u, h in units}
        inc = {(u, h): lax.dot_general(vs[u, h], ke[u], TN_DIMS, preferred_element_type=F32) for u, h in units}
        o = {(u, h): jnp.dot(jnp.where(causal, a[u, h], 0.0).astype(BF16), vs[u, h], preferred_element_type=F32)
             for u, h in units}
        states = {}
        for h in range(2):
            st = st_ref[h]
            for u in range(GLA_UNROLL):
                states[u, h] = st
                st = st * e_last[u] + inc[u, h]
            st_ref[h] = st
        for u, h in units:
            cols = slice(h * GLA_DV, (h + 1) * GLA_DV)
            ou = o[u, h] + lax.dot_general(jnp.where(head_mask[h], qs[u], 0.0).astype(BF16),
                                           states[u, h].astype(BF16), NT_DIMS, preferred_element_type=F32)
            ms = jnp.mean(ou * ou, axis=-1, keepdims=True)
            on = ou * lax.rsqrt(ms + NORM_EPS) * gw_ref[...]
            g = g_ref[rows[u], cols].astype(F32)
            o_ref[rows[u], cols] = (on * _silu(g)).astype(o_ref.dtype)
        return carry

    lax.fori_loop(0, seq // (c * GLA_UNROLL), chunk_body, 0)


def _gla(proj3, ga3, wal, bal, gw):
    b, s, _ = proj3.shape
    npair = GLA_HEADS // 2
    qcol = 3 * MOBA_WIDTH // LANES
    kcol = qcol + GLA_KEY_WIDTH // LANES
    vcol = (3 * MOBA_WIDTH + 2 * GLA_KEY_WIDTH) // (2 * GLA_DV)
    gcol = vcol + npair
    return pl.pallas_call(
        _gla_kernel,
        grid=(b, npair),
        in_specs=[pl.BlockSpec((None, s, LANES), lambda bb, hp: (bb, 0, qcol + hp)),
                  pl.BlockSpec((None, s, LANES), lambda bb, hp: (bb, 0, kcol + hp)),
                  pl.BlockSpec((None, s, 2 * GLA_DV), lambda bb, hp: (bb, 0, vcol + hp)),
                  pl.BlockSpec((None, s, 2 * GLA_DV), lambda bb, hp: (bb, 0, gcol + hp)),
                  pl.BlockSpec((None, s, LANES), lambda bb, hp: (bb, 0, 0)),
                  pl.BlockSpec((LANES, LANES), lambda bb, hp: (0, hp)),
                  pl.BlockSpec((1, LANES), lambda bb, hp: (0, hp)),
                  pl.BlockSpec((1, GLA_DV), lambda bb, hp: (0, 0))],
        out_specs=pl.BlockSpec((None, s, 2 * GLA_DV), lambda bb, hp: (bb, 0, hp)),
        out_shape=jax.ShapeDtypeStruct((b, s, GLA_WIDTH), BF16),
        scratch_shapes=[pltpu.VMEM((s, LANES), F32),
                        pltpu.VMEM((2, GLA_DV, LANES), F32)],
        compiler_params=_params(2),
        name="gla",
    )(proj3, proj3, proj3, proj3, ga3, wal, bal, gw)


HALF = D_MODEL // 2


def _pack_rows(v):
    return pltpu.pack_elementwise([v[:, :HALF], v[:, HALF:]], packed_dtype=BF16)


def _unpack_rows(w):
    return (pltpu.unpack_elementwise(w, index=0, packed_dtype=BF16, unpacked_dtype=F32),
            pltpu.unpack_elementwise(w, index=1, packed_dtype=BF16, unpacked_dtype=F32))


def _outproj_kernel(oa_ref, ob_ref, x_ref, g1_ref, sc_ref, sh_ref, g2_ref, nw_ref, wo_ref,
                    ws1_ref, ws3_ref, ws2_ref, wrt_ref, base_ref, h_ref, st_ref):
    mix = (jnp.dot(oa_ref[...], wo_ref[:MOBA_WIDTH, :], preferred_element_type=F32)
           + jnp.dot(ob_ref[...], wo_ref[MOBA_WIDTH:, :], preferred_element_type=F32))
    x1 = x_ref[...] + g1_ref[...] * mix
    ms = jnp.mean(x1 * x1, axis=-1, keepdims=True)
    h = x1 * lax.rsqrt(ms + NORM_EPS) * nw_ref[...]
    h = h * (1.0 + sc_ref[...]) + sh_ref[...]
    h_ref[...] = _pack_rows(h)
    hb = h.astype(BF16)
    a = jnp.dot(hb, ws1_ref[...], preferred_element_type=F32)
    u = jnp.dot(hb, ws3_ref[...], preferred_element_type=F32)
    shared = jnp.dot((_silu(a) * u).astype(BF16), ws2_ref[...], preferred_element_type=F32)
    base_ref[...] = x1 + g2_ref[...] * shared
    logits_t = lax.dot_general(wrt_ref[...], hb, NT_DIMS, preferred_element_type=F32)
    st_ref[...] = jax.nn.sigmoid(logits_t)


def _outproj(oa, ob, x2, g1, sc, sh, g2, nw, wo, ws1, ws3, ws2, wrt, seq):
    t = x2.shape[0]
    tpb = seq // ROW_TILE
    vec = lambda: pl.BlockSpec((None, 1, D_MODEL), lambda i: (i // tpb, 0, 0))
    full = lambda a: pl.BlockSpec(a.shape, lambda i: (0,) * a.ndim)
    rows = lambda w: pl.BlockSpec((ROW_TILE, w), lambda i: (i, 0))
    return pl.pallas_call(
        _outproj_kernel,
        grid=(t // ROW_TILE,),
        in_specs=[rows(MOBA_WIDTH), rows(GLA_WIDTH), rows(D_MODEL), vec(), vec(), vec(), vec(),
                  full(nw), full(wo), full(ws1), full(ws3), full(ws2), full(wrt)],
        out_specs=[rows(D_MODEL), rows(HALF), pl.BlockSpec((N_EXPERTS, ROW_TILE), lambda i: (0, i))],
        out_shape=[jax.ShapeDtypeStruct((t, D_MODEL), F32),
                   jax.ShapeDtypeStruct((t, HALF), jnp.uint32),
                   jax.ShapeDtypeStruct((N_EXPERTS, t), F32)],
        compiler_params=_params(1),
        name="outproj",
    )(oa, ob, x2, g1, sc, sh, g2, nw, wo, ws1, ws3, ws2, wrt)


SLOT_CODE_SHIFT = 16
SLOT_CODE_BASE = 1 << SLOT_CODE_SHIFT


def _route_kernel(s_ref, eb_ref, code_ref, w_ref, cnt_ref, carry_ref):
    i = pl.program_id(0)
    ne, nt = s_ref.shape

    @pl.when(i == 0)
    def _init():
        carry_ref[...] = jnp.zeros_like(carry_ref)

    s = s_ref[...]
    choice = s + eb_ref[...]
    gio = lax.broadcasted_iota(jnp.int32, (GROUP_SIZE, nt), 0)
    gscore = []
    for g in range(N_GROUPS):
        cg = choice[g * GROUP_SIZE:(g + 1) * GROUP_SIZE, :]
        top1 = jnp.max(cg, axis=0, keepdims=True)
        first = jnp.min(jnp.where(cg == top1, gio, GROUP_SIZE), axis=0, keepdims=True)
        top2 = jnp.max(jnp.where(gio == first, -jnp.inf, cg), axis=0, keepdims=True)
        gscore.append(top1 + top2)
    gs = jnp.concatenate(gscore, axis=0)
    gidx = lax.broadcasted_iota(jnp.int32, gs.shape, 0)
    beaten = jnp.zeros(gs.shape, jnp.int32)
    for m in range(N_GROUPS):
        gm = gs[m:m + 1, :]
        beaten = beaten + jnp.where((gm > gs) | ((gm == gs) & (gidx > m)), 1, 0)
    gkeep = beaten < TOPK_GROUPS
    masked = jnp.concatenate(
        [jnp.where(gkeep[g:g + 1, :], choice[g * GROUP_SIZE:(g + 1) * GROUP_SIZE, :], -jnp.inf)
         for g in range(N_GROUPS)], axis=0)

    eio = lax.broadcasted_iota(jnp.int32, (ne, nt), 0)
    picked = jnp.zeros((ne, nt), F32)
    idx_rows, w_rows, hits = [], [], []
    for _ in range(TOP_K):
        mx = jnp.max(masked, axis=0, keepdims=True)
        idx = jnp.min(jnp.where(masked == mx, eio, ne), axis=0, keepdims=True)
        hit = eio == idx
        w_rows.append(jnp.sum(jnp.where(hit, s, 0.0), axis=0, keepdims=True))
        idx_rows.append(idx)
        hits.append(hit)
        masked = jnp.where(hit, -jnp.inf, masked)
        picked = jnp.where(hit, 1.0, picked)
    wk = jnp.concatenate(w_rows, axis=0)
    w_ref[...] = wk / jnp.sum(wk, axis=0, keepdims=True) * ROUTED_SCALE

    tr = lax.broadcasted_iota(jnp.int32, (nt, nt), 0)
    tc = lax.broadcasted_iota(jnp.int32, (nt, nt), 1)
    before = jnp.where(tr < tc, 1.0, 0.0).astype(BF16)
    pb = picked.astype(BF16)
    pos = carry_ref[...] + jnp.dot(pb, before, preferred_element_type=F32)
    rank = jnp.concatenate(
        [jnp.sum(jnp.where(hit, pos, 0.0), axis=0, keepdims=True) for hit in hits], axis=0).astype(jnp.int32)
    code_ref[...] = jnp.concatenate(idx_rows, axis=0) * SLOT_CODE_BASE + rank
    total = carry_ref[...] + jnp.dot(pb, jnp.ones((nt, nt), BF16), preferred_element_type=F32)
    carry_ref[...] = total
    cnt_ref[...] = total


def _route(scores_t, eb):
    ne, t = scores_t.shape
    assert t <= SLOT_CODE_BASE
    nt = ROUTE_TILE
    tok = lambda dt: jax.ShapeDtypeStruct((TOP_K, t), dt)
    return pl.pallas_call(
        _route_kernel,
        grid=(t // nt,),
        in_specs=[pl.BlockSpec((ne, nt), lambda i: (0, i)),
                  pl.BlockSpec((ne, nt), lambda i: (0, 0))],
        out_specs=[pl.BlockSpec((TOP_K, nt), lambda i: (0, i)),
                   pl.BlockSpec((TOP_K, nt), lambda i: (0, i)),
                   pl.BlockSpec((ne, nt), lambda i: (0, 0))],
        out_shape=[tok(jnp.int32), tok(F32), jax.ShapeDtypeStruct((ne, nt), F32)],
        scratch_shapes=[pltpu.VMEM((ne, nt), F32)],
        compiler_params=_params(1),
        name="route",
    )(scores_t, eb)


SLOT_TILE = 2048


def _slots_kernel(pstart_ref, code_ref, o_ref):
    code = code_ref[...]
    expert = lax.shift_right_logical(code, SLOT_CODE_SHIFT)

    def body(e, acc):
        return jnp.where(expert == e, pstart_ref[e], acc)

    start = lax.fori_loop(0, N_EXPERTS, body, jnp.zeros_like(code), unroll=8)
    o_ref[...] = start + (code & (SLOT_CODE_BASE - 1))


def _slots(pstart, code_t):
    k, t = code_t.shape
    return pl.pallas_call(
        _slots_kernel,
        grid_spec=pltpu.PrefetchScalarGridSpec(
            num_scalar_prefetch=1,
            grid=(t // SLOT_TILE,),
            in_specs=[pl.BlockSpec((k, SLOT_TILE), lambda i, p: (0, i))],
            out_specs=pl.BlockSpec((k, SLOT_TILE), lambda i, p: (0, i)),
        ),
        out_shape=jax.ShapeDtypeStruct((k, t), jnp.int32),
        compiler_params=_params(1),
        name="slots",
    )(pstart, code_t)


SC_WINDOW = 128


def _sc_gather_rows(table, idx_flat):
    info = plsc.get_sparse_core_info()
    nw = info.num_cores * info.num_subcores
    n = idx_flat.shape[0]
    width = table.shape[1]
    per_worker = n // nw
    assert per_worker * nw == n and per_worker % SC_WINDOW == 0
    mesh = plsc.VectorSubcoreMesh(core_axis_name="c", subcore_axis_name="s")

    def body(table_hbm, idx_hbm, out_hbm, idx_v, rows_v, sem):
        wid = lax.axis_index("s") * info.num_cores + lax.axis_index("c")
        base = wid * per_worker

        @pl.loop(0, per_worker // SC_WINDOW)
        def _(w):
            off = pl.multiple_of(base + w * SC_WINDOW, SC_WINDOW)
            pltpu.sync_copy(idx_hbm.at[pl.ds(off, SC_WINDOW)], idx_v)
            pltpu.async_copy(table_hbm.at[idx_v], rows_v, sem).wait()
            pltpu.sync_copy(rows_v, out_hbm.at[pl.ds(off, SC_WINDOW)])

    return pl.kernel(
        body,
        out_type=jax.ShapeDtypeStruct((n, width), table.dtype),
        mesh=mesh,
        scratch_types=[pltpu.VMEM((SC_WINDOW,), jnp.int32),
                       pltpu.VMEM((SC_WINDOW, width), table.dtype),
                       pltpu.SemaphoreType.DMA],
        name="sc_gather",
    )(table, idx_flat)


def _sc_scatter_rows(rows, idx_kt, n_out):
    info = plsc.get_sparse_core_info()
    nw = info.num_cores * info.num_subcores
    t, width = rows.shape
    nk = idx_kt.shape[0]
    per_worker = t // nw
    assert per_worker * nw == t and per_worker % SC_WINDOW == 0
    mesh = plsc.VectorSubcoreMesh(core_axis_name="c", subcore_axis_name="s")

    def body(rows_hbm, idx_hbm, out_hbm, idx_v, rows_v, sem):
        wid = lax.axis_index("s") * info.num_cores + lax.axis_index("c")
        base = wid * per_worker

        @pl.loop(0, per_worker // SC_WINDOW)
        def _(w):
            off = pl.multiple_of(base + w * SC_WINDOW, SC_WINDOW)
            pltpu.sync_copy(rows_hbm.at[pl.ds(off, SC_WINDOW)], rows_v)
            pltpu.sync_copy(idx_hbm.at[:, pl.ds(off, SC_WINDOW)], idx_v)
            copies = [pltpu.async_copy(rows_v, out_hbm.at[idx_v.at[k]], sem) for k in range(nk)]
            for cp in copies:
                cp.wait()

    return pl.kernel(
        body,
        out_type=jax.ShapeDtypeStruct((n_out, width), rows.dtype),
        mesh=mesh,
        scratch_types=[pltpu.VMEM((nk, SC_WINDOW), jnp.int32),
                       pltpu.VMEM((SC_WINDOW, width), rows.dtype),
                       pltpu.SemaphoreType.DMA],
        name="sc_scatter",
    )(rows, idx_kt)


def _expert_kernel(first_ref, count_ref, used_ref, w1_ref, w3_ref, w2_ref, xs_ref, ys_ref,
                   xbuf, ybuf, sem_in, sem_out, w1f, w3f, w2f, sem_w, w1b, w3b, w2b, *, layer):
    e = pl.program_id(0)
    ne = pl.num_programs(0)
    r = xbuf.shape[1]
    n_used = used_ref[0]

    def x_copy(g, slot):
        return pltpu.make_async_copy(xs_ref.at[pl.ds(pl.multiple_of(g * r, r), r), :], xbuf.at[slot], sem_in.at[slot])

    def y_copy(g, slot):
        return pltpu.make_async_copy(ybuf.at[slot], ys_ref.at[pl.ds(pl.multiple_of(g * r, r), r), :], sem_out.at[slot])

    def w_copies(ex, slot):
        return [pltpu.make_async_copy(src.at[layer, ex], dst.at[slot], sem_w.at[slot])
                for src, dst in ((w1_ref, w1f), (w3_ref, w3f), (w2_ref, w2f))]

    nin = xbuf.shape[0]
    nout = ybuf.shape[0]

    nw = w1f.shape[0]

    @pl.when(e == 0)
    def _first_reads():
        for g in range(nin):
            @pl.when(g < n_used)
            def _(g=g):
                x_copy(g, g).start()
        for ex in range(nw - 1):
            for cp in w_copies(ex, ex):
                cp.start(priority=1)

    @pl.when(e + nw - 1 < ne)
    def _weights_ahead():
        for cp in w_copies(e + nw - 1, (e + nw - 1) % nw):
            cp.start(priority=1)

    wslot = e % nw
    for cp in w_copies(e, wslot):
        cp.wait()

    n = count_ref[e]

    @pl.when(n > 0)
    def _cast_weights():
        w1b[...] = w1f[wslot].astype(BF16)
        w3b[...] = w3f[wslot].astype(BF16)
        w2b[...] = w2f[wslot].astype(BF16)

    def run_tiles(g, count):
        tiles = [g + j for j in range(count)]
        for gj in tiles:
            x_copy(gj, gj % nin).wait()
        x = jnp.concatenate([xbuf[gj % nin] for gj in tiles], axis=0)
        for gj in tiles:
            @pl.when(gj + nin < n_used)
            def _(gj=gj):
                x_copy(gj + nin, gj % nin).start()
        lo, hi = _unpack_rows(x)
        lo, hi = lo.astype(BF16), hi.astype(BF16)
        a = (jnp.dot(lo, w1b[:HALF, :], preferred_element_type=F32)
             + jnp.dot(hi, w1b[HALF:, :], preferred_element_type=F32))
        u = (jnp.dot(lo, w3b[:HALF, :], preferred_element_type=F32)
             + jnp.dot(hi, w3b[HALF:, :], preferred_element_type=F32))
        y = _pack_rows(jnp.dot((_silu(a) * u).astype(BF16), w2b[...], preferred_element_type=F32))
        for j, gj in enumerate(tiles):
            @pl.when(gj >= nout)
            def _(gj=gj):
                y_copy(gj - nout, gj % nout).wait()
            ybuf[gj % nout] = y[j * r:(j + 1) * r]
            y_copy(gj, gj % nout).start()

    g0 = first_ref[e]
    pair = EXPERT_TILES_PER_MATMUL

    def pair_body(p, carry):
        run_tiles(g0 + p * pair, pair)
        return carry

    lax.fori_loop(0, n // pair, pair_body, 0)
    for left in range(1, pair):
        pl.when(n % pair == left)(functools.partial(run_tiles, g0 + n - left, left))

    @pl.when(e == ne - 1)
    def _drain_writes():
        for back in range(nout, 0, -1):
            @pl.when(n_used >= back)
            def _(back=back):
                y_copy(n_used - back, (n_used - back) % nout).wait()


def _experts(tile_first, tile_count, n_used, xs, w1, w3, w2, layer):
    n_rows = xs.shape[0]
    r = EXPERT_TILE
    any_spec = pl.BlockSpec(memory_space=pl.ANY)
    return pl.pallas_call(
        functools.partial(_expert_kernel, layer=layer),
        grid_spec=pltpu.PrefetchScalarGridSpec(
            num_scalar_prefetch=3,
            grid=(N_EXPERTS,),
            in_specs=[any_spec, any_spec, any_spec, any_spec],
            out_specs=any_spec,
            scratch_shapes=[pltpu.VMEM((EXPERT_IN_RING, r, HALF), jnp.uint32),
                            pltpu.VMEM((EXPERT_OUT_RING, r, HALF), jnp.uint32),
                            pltpu.SemaphoreType.DMA((EXPERT_IN_RING,)),
                            pltpu.SemaphoreType.DMA((EXPERT_OUT_RING,)),
                            pltpu.VMEM((EXPERT_WEIGHT_RING, D_MODEL, EXPERT_FF), F32),
                            pltpu.VMEM((EXPERT_WEIGHT_RING, D_MODEL, EXPERT_FF), F32),
                            pltpu.VMEM((EXPERT_WEIGHT_RING, EXPERT_FF, D_MODEL), F32),
                            pltpu.SemaphoreType.DMA((EXPERT_WEIGHT_RING,)),
                            pltpu.VMEM((D_MODEL, EXPERT_FF), BF16),
                            pltpu.VMEM((D_MODEL, EXPERT_FF), BF16),
                            pltpu.VMEM((EXPERT_FF, D_MODEL), BF16)],
        ),
        out_shape=jax.ShapeDtypeStruct((n_rows, HALF), jnp.uint32),
        compiler_params=_params(1),
        name="experts",
    )(tile_first, tile_count, n_used, w1, w3, w2, xs)


def _combine_dense_kernel(base_ref, g2_ref, w_ref, yg_ref, o_ref):
    acc_lo = acc_hi = None
    for k in range(TOP_K):
        lo, hi = _unpack_rows(yg_ref[k])
        wk = w_ref[:, k:k + 1]
        acc_lo = wk * lo if acc_lo is None else acc_lo + wk * lo
        acc_hi = wk * hi if acc_hi is None else acc_hi + wk * hi
    o_ref[:, :HALF] = base_ref[:, :HALF] + g2_ref[:, :HALF] * acc_lo
    o_ref[:, HALF:] = base_ref[:, HALF:] + g2_ref[:, HALF:] * acc_hi


def _combine_dense(base, g2, w_tok, yg, batch):
    t = base.shape[0]
    seq = yg.shape[1]
    nt = ROUTE_TILE
    tpb = seq // nt
    rows = lambda i: (batch * tpb + i, 0)
    return pl.pallas_call(
        _combine_dense_kernel,
        grid=(tpb,),
        in_specs=[pl.BlockSpec((nt, D_MODEL), rows),
                  pl.BlockSpec((None, 1, D_MODEL), lambda i: (batch, 0, 0)),
                  pl.BlockSpec((nt, TOP_K), rows),
                  pl.BlockSpec((TOP_K, nt, HALF), lambda i: (0, i, 0))],
        out_specs=pl.BlockSpec((nt, D_MODEL), rows),
        out_shape=jax.ShapeDtypeStruct((t, D_MODEL), F32),
        input_output_aliases={0: 0},
        compiler_params=_params(1),
        name="combine_dense",
    )(base, g2, w_tok, yg)


def _layer(layer, x, c, w_ada, b_ada, norm1_w, norm2_w, w_in, q_norm_w, k_norm_w, rel_bias, w_alpha, b_alpha,
           moba_out_w, gla_out_w, w_out, w_router, e_bias, w1, w3, w2, ws1, ws3, ws2):
    b, s, d = x.shape
    t = b * s
    x2 = x.reshape(t, d)

    mod = _mod(c, w_ada, b_ada)
    sh1, sc1, g1, sh2, sc2, g2 = [mod[:, j * d:(j + 1) * d].reshape(b, 1, d) for j in range(6)]

    w_main = w_in[:, :D_MAIN].astype(BF16)
    w_ga = jnp.zeros((d, LANES), BF16).at[:, :GLA_GATE_RANK].set(w_in[:, D_MAIN:].astype(BF16))
    per_chunk = 256 // MOBA_HEAD_DIM
    qw = jnp.tile(q_norm_w.astype(F32), per_chunk).reshape(1, 256) * (MOBA_HEAD_DIM ** -0.5 * LOG2E)
    kw = jnp.tile(k_norm_w.astype(F32), per_chunk).reshape(1, 256)
    proj, ga = _inproj(x2, sc1, sh1, norm1_w.reshape(1, d), w_main, w_ga, qw, kw, s)
    proj3 = proj.reshape(b, s, D_MAIN)

    near, far = _moba_bias_tables(rel_bias)
    ow = jnp.tile(moba_out_w.astype(F32), 2).reshape(1, LANES)
    o_a = _moba(proj3, near, far, ow)

    wal = jnp.zeros((LANES, GLA_KEY_WIDTH), F32).at[:GLA_GATE_RANK].set(w_alpha)
    o_b = _gla(proj3, ga.reshape(b, s, LANES), wal, b_alpha.reshape(1, GLA_KEY_WIDTH),
               gla_out_w.reshape(1, GLA_DV))

    base, h2, scores_t = _outproj(
        o_a.reshape(t, MOBA_WIDTH), o_b.reshape(t, GLA_WIDTH), x2, g1, sc2, sh2, g2,
        norm2_w.reshape(1, d), w_out.astype(BF16), ws1.astype(BF16), ws3.astype(BF16), ws2.astype(BF16),
        w_router.T.astype(BF16), s)

    eb = jnp.broadcast_to(e_bias.astype(F32)[:, None], (N_EXPERTS, ROUTE_TILE))
    code_t, w_t, counts = _route(scores_t, eb)

    r = EXPERT_TILE
    n_tiles = (t * TOP_K + N_EXPERTS * (r - 1) + r - 1) // r
    n_rows = n_tiles * r
    cnt = counts[:, 0].astype(jnp.int32)
    padded = (cnt + r - 1) // r * r
    pend = jnp.cumsum(padded)
    pstart = pend - padded
    n_used = (pend[-1:] // r).astype(jnp.int32)
    dest_t = _slots(pstart, code_t)

    xs = _sc_scatter_rows(h2, dest_t, n_rows)
    ys = _experts(pstart // r, padded // r, n_used, xs, w1, w3, w2, layer)
    w_tok = w_t.T
    out = base
    for bi in range(b):
        idx = dest_t[:, bi * s:(bi + 1) * s].reshape(TOP_K * s)
        yg = _sc_gather_rows(ys, idx).reshape(TOP_K, s, HALF)
        out = _combine_dense(out, g2, w_tok, yg, bi)
    return out.reshape(b, s, d)


def kernel(x, c, w_ada, b_ada, norm1_w, norm2_w, w_in, q_norm_w, k_norm_w, rel_bias, w_alpha, b_alpha,
           moba_out_w, gla_out_w, w_out, w_router, e_bias, w1, w3, w2, ws1, ws3, ws2):
    for l in range(w_ada.shape[0]):
        x = _layer(l, x, c, w_ada[l], b_ada[l], norm1_w[l], norm2_w[l], w_in[l], q_norm_w[l], k_norm_w[l],
                   rel_bias, w_alpha[l], b_alpha[l], moba_out_w[l], gla_out_w[l], w_out[l], w_router[l],
                   e_bias[l], w1, w3, w2, ws1[l], ws3[l], ws2[l])
    return x
```

```python
import functools
import math

import numpy as np
import jax
import jax.numpy as jnp
from jax import lax
from jax.experimental import pallas as pl
from jax.experimental.pallas import tpu as pltpu
from jax.experimental.pallas import tpu_sc as plsc

D_MODEL = 1024
MOBA_HEADS = 8
MOBA_HEAD_DIM = 64
MOBA_WIDTH = MOBA_HEADS * MOBA_HEAD_DIM
MOBA_BLOCK = 256
MOBA_TOPK = 3
GLA_HEADS = 4
GLA_DK = 64
GLA_DV = 128
GLA_KEY_WIDTH = GLA_HEADS * GLA_DK
GLA_WIDTH = GLA_HEADS * GLA_DV
GLA_GATE_RANK = 16
GLA_GATE_TAU = 16.0
GLA_CHUNK = 64
REL_BUCKETS = 32
REL_MAX_DIST = 128
N_EXPERTS = 256
TOP_K = 8
N_GROUPS = 8
TOPK_GROUPS = 4
GROUP_SIZE = N_EXPERTS // N_GROUPS
EXPERT_FF = 256
SHARED_FF = 256
ROUTED_SCALE = 2.5
NORM_EPS = 1e-6
LOG2E = math.log2(math.e)

D_MAIN = 3 * MOBA_WIDTH + 2 * GLA_KEY_WIDTH + 2 * GLA_WIDTH
LANES = 128
VMEM_LIMIT = 56 * 1024 * 1024

ROW_TILE = 512
ROUTE_TILE = 256
EXPERT_TILE = 128
EXPERT_TILES_PER_MATMUL = 6
EXPERT_IN_RING = 16
EXPERT_OUT_RING = 12
EXPERT_WEIGHT_RING = 3

F32 = jnp.float32
BF16 = jnp.bfloat16
NT_DIMS = (((1,), (1,)), ((), ()))
TN_DIMS = (((0,), (0,)), ((), ()))


def _params(n_axes):
    return pltpu.CompilerParams(dimension_semantics=("arbitrary",) * n_axes,
                                vmem_limit_bytes=VMEM_LIMIT)


def _silu(v):
    return v * jax.nn.sigmoid(v)


def _mod_kernel(c_ref, w_ref, b_ref, o_ref):
    o_ref[...] = jnp.dot(_silu(c_ref[...]), w_ref[...], preferred_element_type=F32) + b_ref[...]


def _mod(c, w, b):
    rows = 8
    cp = jnp.zeros((rows, D_MODEL), F32).at[:c.shape[0]].set(c)
    n = w.shape[1]
    tn = 1024
    out = pl.pallas_call(
        _mod_kernel,
        grid=(n // tn,),
        in_specs=[pl.BlockSpec((rows, D_MODEL), lambda j: (0, 0)),
                  pl.BlockSpec((D_MODEL, tn), lambda j: (0, j)),
                  pl.BlockSpec((1, tn), lambda j: (0, j))],
        out_specs=pl.BlockSpec((rows, tn), lambda j: (0, j)),
        out_shape=jax.ShapeDtypeStruct((rows, n), F32),
        compiler_params=_params(1),
        name="mod",
    )(cp, w, b.reshape(1, n))
    return out[:c.shape[0]]


def _group_rms_inv(a, group):
    lane = lax.broadcasted_iota(jnp.int32, (1, a.shape[1]), 1)
    a2 = a * a
    inv = jnp.zeros_like(a)
    for g in range(a.shape[1] // group):
        m = (lane >= g * group) & (lane < (g + 1) * group)
        ss = jnp.sum(jnp.where(m, a2, 0.0), axis=-1, keepdims=True)
        inv = jnp.where(m, lax.rsqrt(ss * (1.0 / group) + NORM_EPS), inv)
    return inv


def _inproj_kernel(x_ref, sc_ref, sh_ref, nw_ref, w_ref, wga_ref, qw_ref, kw_ref, o_ref, ga_ref):
    x = x_ref[...]
    ms = jnp.mean(x * x, axis=-1, keepdims=True)
    h = x * lax.rsqrt(ms + NORM_EPS) * nw_ref[...]
    h = h * (1.0 + sc_ref[...]) + sh_ref[...]
    hb = h.astype(BF16)
    cw = 256
    for j in range(D_MAIN // cw):
        acc = jnp.dot(hb, w_ref[:, j * cw:(j + 1) * cw], preferred_element_type=F32)
        if j < 2 * MOBA_WIDTH // cw:
            nw = qw_ref if j < MOBA_WIDTH // cw else kw_ref
            acc = acc * _group_rms_inv(acc, MOBA_HEAD_DIM) * nw[...]
        o_ref[:, j * cw:(j + 1) * cw] = acc.astype(BF16)
    ga_ref[...] = jnp.dot(hb, wga_ref[...], preferred_element_type=F32)


def _inproj(x2, sc, sh, nw, w_main, w_ga, qw, kw, seq):
    t = x2.shape[0]
    tpb = seq // ROW_TILE
    vec = lambda: pl.BlockSpec((None, 1, D_MODEL), lambda i: (i // tpb, 0, 0))
    full = lambda a: pl.BlockSpec(a.shape, lambda i: (0,) * a.ndim)
    return pl.pallas_call(
        _inproj_kernel,
        grid=(t // ROW_TILE,),
        in_specs=[pl.BlockSpec((ROW_TILE, D_MODEL), lambda i: (i, 0)), vec(), vec(),
                  full(nw), full(w_main), full(w_ga), full(qw), full(kw)],
        out_specs=[pl.BlockSpec((ROW_TILE, D_MAIN), lambda i: (i, 0)),
                   pl.BlockSpec((ROW_TILE, LANES), lambda i: (i, 0))],
        out_shape=[jax.ShapeDtypeStruct((t, D_MAIN), BF16),
                   jax.ShapeDtypeStruct((t, LANES), F32)],
        compiler_params=_params(1),
        name="inproj",
    )(x2, sc, sh, nw, w_main, w_ga, qw, kw)


def _t5_bucket_np(rel):
    max_exact = REL_BUCKETS // 2
    relf = np.maximum(rel, 1).astype(np.float64)
    large = max_exact + (np.log(relf / max_exact) / math.log(REL_MAX_DIST / max_exact)
                         * (REL_BUCKETS - max_exact)).astype(np.int32)
    large = np.minimum(large, REL_BUCKETS - 1)
    return np.where(rel < max_exact, rel, large)


def _bias_kernel(rb_ref, idx_ref, o_ref):
    h = pl.program_id(0)
    idx = idx_ref[...]
    tab = jnp.full(idx.shape, -jnp.inf, F32)
    for bk in range(REL_BUCKETS):
        tab = jnp.where(idx == bk, rb_ref[bk * MOBA_HEADS + h], tab)
    o_ref[...] = tab


def _moba_bias_tables(rel_bias):
    j = np.arange(MOBA_BLOCK)[:, None]
    i = np.arange(MOBA_BLOCK)[None, :]
    own_idx = np.where(j <= i, _t5_bucket_np(np.maximum(i - j, 0)), -1)
    prev_idx = _t5_bucket_np(MOBA_BLOCK + i - j)
    idx = jnp.asarray(np.concatenate([prev_idx, own_idx], axis=0).astype(np.int32))
    assert int(_t5_bucket_np(np.array([MOBA_BLOCK + 1]))[0]) == REL_BUCKETS - 1
    rb = rel_bias.astype(F32) * LOG2E
    near = pl.pallas_call(
        _bias_kernel,
        grid=(MOBA_HEADS,),
        in_specs=[pl.BlockSpec(memory_space=pltpu.SMEM),
                  pl.BlockSpec(idx.shape, lambda h: (0, 0))],
        out_specs=pl.BlockSpec((None,) + idx.shape, lambda h: (h, 0, 0)),
        out_shape=jax.ShapeDtypeStruct((MOBA_HEADS,) + idx.shape, F32),
        compiler_params=_params(1),
        name="bias",
    )(rb.reshape(-1), idx)
    return near, rb[REL_BUCKETS - 1]


FAR_GROUP = 4


def _moba_kernel(*refs):
    hp = pl.program_id(1)
    npairs = refs[2].shape[0] // (2 * MOBA_BLOCK)
    _moba_body(None, hp, *refs, prepare=True)

    def block_pair(j, carry):
        for jj in range(npairs):
            pl.when(j == jj)(functools.partial(_moba_body, [2 * jj, 2 * jj + 1], hp, *refs, prepare=False))
        return carry

    lax.fori_loop(0, npairs, block_pair, 0)


def _moba_body(blocks, hp, far_ref, q_ref, k_ref, v_ref, near_ref, ow_ref, o_ref,
               vt_ref, vtg_ref, acc_ref, m_ref, sel_ref, s_ref, mx_ref, *, prepare):
    nblk = k_ref.shape[0] // MOBA_BLOCK
    ngrp = nblk // FAR_GROUP
    hd = MOBA_HEAD_DIM
    bs = MOBA_BLOCK
    lane = lax.broadcasted_iota(jnp.int32, (bs, LANES), 1)

    def split_heads(qb):
        zero = jnp.zeros_like(qb)
        return jnp.where(lane < hd, qb, zero), jnp.where(lane < hd, zero, qb)

    def _prepare():
        row = lax.broadcasted_iota(jnp.int32, (LANES, bs), 0)
        kmeans = []
        for n in range(nblk):
            kb = k_ref[n * bs:(n + 1) * bs, :].astype(F32)
            kmeans.append(jnp.mean(kb, axis=0, keepdims=True))
            vt = v_ref[n * bs:(n + 1) * bs, :].astype(F32).T
            vt0 = jnp.where(row < hd, vt, 1.0).astype(BF16)
            vt1 = jnp.where(row < hd, 1.0, vt).astype(BF16)
            vt_ref[0, n] = vt0
            vt_ref[1, n] = vt1
            gcols = slice((n % FAR_GROUP) * bs, (n % FAR_GROUP + 1) * bs)
            vtg_ref[0, n // FAR_GROUP, :, gcols] = vt0
            vtg_ref[1, n // FAR_GROUP, :, gcols] = vt1
        kmean = jnp.concatenate(kmeans, axis=0)
        km_hi = kmean.astype(BF16)
        km_lo = (kmean - km_hi.astype(F32)).astype(BF16)
        blk = lax.broadcasted_iota(jnp.int32, (nblk, bs), 0)
        for ib in range(nblk):
            qparts = split_heads(q_ref[ib * bs:(ib + 1) * bs, :])
            for h in range(2):
                gt = (lax.dot_general(km_hi, qparts[h], NT_DIMS, preferred_element_type=F32)
                      + lax.dot_general(km_lo, qparts[h], NT_DIMS, preferred_element_type=F32))
                gt = jnp.where(blk < ib, gt, -jnp.inf)
                cnt = jnp.zeros(gt.shape, jnp.int32)
                for m in range(ib):
                    gm = gt[m:m + 1, :]
                    cnt = cnt + jnp.where((gm > gt) | ((gm == gt) & (blk > m)), 1, 0)
                keep = (blk < ib) & (cnt < MOBA_TOPK)
                sel_ref[0, h, ib] = jnp.where(keep, 1.0, 0.0)
                sel_ref[1, h, ib] = jnp.where(keep & (blk < ib - 1), 1.0, 0.0)

    if prepare:
        _prepare()
        return

    gk = FAR_GROUP * bs

    def rows_of(ib):
        if isinstance(ib, int):
            return slice(ib * bs, (ib + 1) * bs)
        return pl.ds(pl.multiple_of(ib * bs, bs), bs)

    def finish(ib, slot):
        a0 = acc_ref[slot, 0]
        a1 = acc_ref[slot, 1]
        row = lax.broadcasted_iota(jnp.int32, a0.shape, 0)
        ot = jnp.where(row < hd, a0 / a0[hd:hd + 1, :], a1 / a1[0:1, :])
        o2 = ot * ot
        ss0 = jnp.sum(jnp.where(row < hd, o2, 0.0), axis=0, keepdims=True)
        ss1 = jnp.sum(jnp.where(row < hd, 0.0, o2), axis=0, keepdims=True)
        inv = jnp.where(row < hd, lax.rsqrt(ss0 * (1.0 / hd) + NORM_EPS), lax.rsqrt(ss1 * (1.0 / hd) + NORM_EPS))
        o_ref[rows_of(ib), :] = ((ot * inv).T * ow_ref[...]).astype(o_ref.dtype)

    def far_scores(g, slot, qh, nb):
        kb = k_ref[g * gk:g * gk + nb * bs, :]
        for h in range(2):
            s = lax.dot_general(kb, qh[h], NT_DIMS, preferred_element_type=F32)
            s_ref[slot, g % 2, h, 0:nb * bs, :] = s
            for j in range(nb):
                mx_ref[slot, g % 2, h, j] = jnp.max(s[j * bs:(j + 1) * bs], axis=0, keepdims=True)

    def near_scores(ib, qh):
        kbs = (k_ref[rows_of(ib - 1), :], k_ref[rows_of(ib), :])
        return [[lax.dot_general(kbs[w], qh[h], NT_DIMS, preferred_element_type=F32)
                 + near_ref[h, w * bs:(w + 1) * bs, :] for w in range(2)] for h in range(2)]

    def near_values(ib, slot, ss):
        ps, ms = [], []
        for h in range(2):
            s_prev, s_own = ss[h]
            keep = sel_ref[0, h, ib, pl.ds(ib - 1, 1), :] > 0.5
            mx = jnp.where(keep, jnp.max(s_prev, axis=0, keepdims=True), -jnp.inf)
            m_new = jnp.maximum(jnp.max(s_own, axis=0, keepdims=True), mx)
            ps.append((jnp.exp2(s_prev - jnp.where(keep, m_new, jnp.inf)).astype(BF16),
                       jnp.exp2(s_own - m_new).astype(BF16)))
            ms.append(m_new)
        for h in range(2):
            acc_ref[slot, h] = (jnp.dot(vt_ref[h, ib - 1], ps[h][0], preferred_element_type=F32)
                                + jnp.dot(vt_ref[h, ib], ps[h][1], preferred_element_type=F32))
            m_ref[slot, h] = ms[h]

    def far_group(g, ib, slot, nb):
        for h in range(2):
            fb = far_ref[2 * hp + h]
            m_old = m_ref[slot, h]
            m_new = m_old
            keeps = []
            for j in range(nb):
                keep = sel_ref[1, h, ib, pl.ds(g * FAR_GROUP + j, 1), :] > 0.5
                m_new = jnp.maximum(m_new, jnp.where(keep, mx_ref[slot, g % 2, h, j] + fb, -jnp.inf))
                keeps.append(keep)
            p = jnp.concatenate(
                [jnp.exp2(s_ref[slot, g % 2, h, j * bs:(j + 1) * bs, :]
                          - jnp.where(keeps[j], m_new - fb, jnp.inf)).astype(BF16)
                 for j in range(nb)], axis=0)
            pv = jnp.dot(vtg_ref[h, g, :, 0:nb * bs], p, preferred_element_type=F32)
            acc_ref[slot, h] = acc_ref[slot, h] * jnp.exp2(m_old - m_new) + pv
            m_ref[slot, h] = m_new

    def step_body(blocks):
        last_far = max(blocks) - 2
        nf = last_far // FAR_GROUP + 1 if last_far >= 0 else 0
        nbs = [min(FAR_GROUP, last_far + 1 - g * FAR_GROUP) for g in range(nf)]
        slots = list(enumerate(blocks))
        qhs = [split_heads(q_ref[rows_of(ib), :]) for ib in blocks]
        near = [near_scores(ib, qhs[slot]) for slot, ib in slots]
        if nf > 0:
            for slot, ib in slots:
                far_scores(0, slot, qhs[slot], nbs[0])
        for slot, ib in slots:
            near_values(ib, slot, near[slot])
        for g in range(nf):
            if g + 1 < nf:
                for slot, ib in slots:
                    far_scores(g + 1, slot, qhs[slot], nbs[g + 1])
            for slot, ib in slots:
                far_group(g, ib, slot, nbs[g])
        for slot, ib in slots:
            finish(ib, slot)

    if blocks[0] == 0:
        qh = split_heads(q_ref[0:bs, :])
        kb = k_ref[0:bs, :]
        for h in range(2):
            s = lax.dot_general(kb, qh[h], NT_DIMS, preferred_element_type=F32) + near_ref[h, bs:2 * bs, :]
            p = jnp.exp2(s - jnp.max(s, axis=0, keepdims=True)).astype(BF16)
            acc_ref[1, h] = jnp.dot(vt_ref[h, 0], p, preferred_element_type=F32)
        finish(0, 1)
        blocks = blocks[1:]
    step_body(blocks)


def _moba(proj3, near, far, ow):
    b, s, _ = proj3.shape
    nblk = s // MOBA_BLOCK
    assert nblk % FAR_GROUP == 0
    npair = MOBA_HEADS // 2
    kcol = MOBA_WIDTH // LANES
    return pl.pallas_call(
        _moba_kernel,
        grid=(b, npair),
        in_specs=[pl.BlockSpec(memory_space=pltpu.SMEM),
                  pl.BlockSpec((None, s, LANES), lambda bb, hp: (bb, 0, hp)),
                  pl.BlockSpec((None, s, LANES), lambda bb, hp: (bb, 0, kcol + hp)),
                  pl.BlockSpec((None, s, LANES), lambda bb, hp: (bb, 0, 2 * kcol + hp)),
                  pl.BlockSpec((2, 2 * MOBA_BLOCK, MOBA_BLOCK), lambda bb, hp: (hp, 0, 0)),
                  pl.BlockSpec((1, LANES), lambda bb, hp: (0, 0))],
        out_specs=pl.BlockSpec((None, s, LANES), lambda bb, hp: (bb, 0, hp)),
        out_shape=jax.ShapeDtypeStruct((b, s, MOBA_WIDTH), BF16),
        scratch_shapes=[pltpu.VMEM((2, nblk, LANES, MOBA_BLOCK), BF16),
                        pltpu.VMEM((2, nblk // FAR_GROUP, LANES, FAR_GROUP * MOBA_BLOCK), BF16),
                        pltpu.VMEM((2, 2, LANES, MOBA_BLOCK), F32),
                        pltpu.VMEM((2, 2, 1, MOBA_BLOCK), F32),
                        pltpu.VMEM((2, 2, nblk, nblk, MOBA_BLOCK), F32),
                        pltpu.VMEM((2, 2, 2, FAR_GROUP * MOBA_BLOCK, MOBA_BLOCK), F32),
                        pltpu.VMEM((2, 2, 2, FAR_GROUP, 1, MOBA_BLOCK), F32)],
        compiler_params=_params(2),
        name="moba",
    )(far, proj3, proj3, proj3, near, ow)


def _split3(v):
    hi = v.astype(BF16)
    r1 = v - hi.astype(F32)
    mid = r1.astype(BF16)
    lo = (r1 - mid.astype(F32)).astype(BF16)
    return hi, mid, lo


GLA_UNROLL = 16


def _gla_kernel(q_ref, k_ref, v_ref, g_ref, ga_ref, wal_ref, bal_ref, gw_ref, o_ref, b_ref, st_ref):
    seq = q_ref.shape[0]
    c = GLA_CHUNK
    pc = 256

    rr = lax.broadcasted_iota(jnp.int32, (pc, pc), 0)
    cc = lax.broadcasted_iota(jnp.int32, (pc, pc), 1)
    tri = jnp.where((rr >= cc) & (rr // c == cc // c), 1.0, 0.0).astype(BF16)

    def decay_body(j, carry):
        rows = [pl.ds(pl.multiple_of((j * GLA_UNROLL + u) * pc, pc), pc) for u in range(GLA_UNROLL)]
        xg = [jnp.dot(ga_ref[r, :], wal_ref[...], preferred_element_type=F32) + bal_ref[...] for r in rows]
        parts = [_split3((jnp.minimum(x, 0.0) - jnp.log(1.0 + jnp.exp(-jnp.abs(x)))) * (1.0 / GLA_GATE_TAU))
                 for x in xg]
        sums = [[jnp.dot(tri, term, preferred_element_type=F32) for term in p] for p in parts]
        for r, (hi, mid, lo) in zip(rows, sums):
            b_ref[r, :] = hi + mid + lo
        return carry

    lax.fori_loop(0, seq // (pc * GLA_UNROLL), decay_body, 0)

    st_ref[...] = jnp.zeros_like(st_ref)
    lane = lax.broadcasted_iota(jnp.int32, (c, LANES), 1)
    head_mask = (lane < GLA_DK, lane >= GLA_DK)
    causal = lax.broadcasted_iota(jnp.int32, (c, c), 0) >= lax.broadcasted_iota(jnp.int32, (c, c), 1)

    units = [(u, h) for u in range(GLA_UNROLL) for h in range(2)]

    def chunk_body(ci, carry):
        rows = [pl.ds(pl.multiple_of((ci * GLA_UNROLL + u) * c, c), c) for u in range(GLA_UNROLL)]
        qt, kt, qs, ke, e_last = [], [], [], [], []
        for u in range(GLA_UNROLL):
            b = b_ref[rows[u], :]
            ref_row = b[c // 2 - 1:c // 2, :]
            last = b[c - 1:c, :]
            q = q_ref[rows[u], :].astype(F32) * (GLA_DK ** -0.5)
            k = k_ref[rows[u], :].astype(F32)
            qt.append(q * jnp.exp(b - ref_row))
            kt.append((k * jnp.exp(ref_row - b)).astype(BF16))
            qs.append(q * jnp.exp(b))
            ke.append((k * jnp.exp(last - b)).astype(BF16))
            e_last.append(jnp.exp(last))
        vs = {(u, h): v_ref[rows[u], h * GLA_DV:(h + 1) * GLA_DV] for u, h in units}
        a = {(u, h): lax.dot_general(jnp.where(head_mask[h], qt[u], 0.0).astype(BF16), kt[u], NT_DIMS,
                                     preferred_element_type=F32) for u, h in units}
        inc = {(u, h): lax.dot_general(vs[u, h], ke[u], TN_DIMS, preferred_element_type=F32) for u, h in units}
        o = {(u, h): jnp.dot(jnp.where(causal, a[u, h], 0.0).astype(BF16), vs[u, h], preferred_element_type=F32)
             for u, h in units}
        states = {}
        for h in range(2):
            st = st_ref[h]
            for u in range(GLA_UNROLL):
                states[u, h] = st
                st = st * e_last[u] + inc[u, h]
            st_ref[h] = st
        for u, h in units:
            cols = slice(h * GLA_DV, (h + 1) * GLA_DV)
            ou = o[u, h] + lax.dot_general(jnp.where(head_mask[h], qs[u], 0.0).astype(BF16),
                                           states[u, h].astype(BF16), NT_DIMS, preferred_element_type=F32)
            ms = jnp.mean(ou * ou, axis=-1, keepdims=True)
            on = ou * lax.rsqrt(ms + NORM_EPS) * gw_ref[...]
            g = g_ref[rows[u], cols].astype(F32)
            o_ref[rows[u], cols] = (on * _silu(g)).astype(o_ref.dtype)
        return carry

    lax.fori_loop(0, seq // (c * GLA_UNROLL), chunk_body, 0)


def _gla(proj3, ga3, wal, bal, gw):
    b, s, _ = proj3.shape
    npair = GLA_HEADS // 2
    qcol = 3 * MOBA_WIDTH // LANES
    kcol = qcol + GLA_KEY_WIDTH // LANES
    vcol = (3 * MOBA_WIDTH + 2 * GLA_KEY_WIDTH) // (2 * GLA_DV)
    gcol = vcol + npair
    return pl.pallas_call(
        _gla_kernel,
        grid=(b, npair),
        in_specs=[pl.BlockSpec((None, s, LANES), lambda bb, hp: (bb, 0, qcol + hp)),
                  pl.BlockSpec((None, s, LANES), lambda bb, hp: (bb, 0, kcol + hp)),
                  pl.BlockSpec((None, s, 2 * GLA_DV), lambda bb, hp: (bb, 0, vcol + hp)),
                  pl.BlockSpec((None, s, 2 * GLA_DV), lambda bb, hp: (bb, 0, gcol + hp)),
                  pl.BlockSpec((None, s, LANES), lambda bb, hp: (bb, 0, 0)),
                  pl.BlockSpec((LANES, LANES), lambda bb, hp: (0, hp)),
                  pl.BlockSpec((1, LANES), lambda bb, hp: (0, hp)),
                  pl.BlockSpec((1, GLA_DV), lambda bb, hp: (0, 0))],
        out_specs=pl.BlockSpec((None, s, 2 * GLA_DV), lambda bb, hp: (bb, 0, hp)),
        out_shape=jax.ShapeDtypeStruct((b, s, GLA_WIDTH), BF16),
        scratch_shapes=[pltpu.VMEM((s, LANES), F32),
                        pltpu.VMEM((2, GLA_DV, LANES), F32)],
        compiler_params=_params(2),
        name="gla",
    )(proj3, proj3, proj3, proj3, ga3, wal, bal, gw)


HALF = D_MODEL // 2


def _pack_rows(v):
    return pltpu.pack_elementwise([v[:, :HALF], v[:, HALF:]], packed_dtype=BF16)


def _unpack_rows(w):
    return (pltpu.unpack_elementwise(w, index=0, packed_dtype=BF16, unpacked_dtype=F32),
            pltpu.unpack_elementwise(w, index=1, packed_dtype=BF16, unpacked_dtype=F32))


def _outproj_kernel(oa_ref, ob_ref, x_ref, g1_ref, sc_ref, sh_ref, g2_ref, nw_ref, wo_ref,
                    ws1_ref, ws3_ref, ws2_ref, wrt_ref, base_ref, h_ref, st_ref):
    mix = (jnp.dot(oa_ref[...], wo_ref[:MOBA_WIDTH, :], preferred_element_type=F32)
           + jnp.dot(ob_ref[...], wo_ref[MOBA_WIDTH:, :], preferred_element_type=F32))
    x1 = x_ref[...] + g1_ref[...] * mix
    ms = jnp.mean(x1 * x1, axis=-1, keepdims=True)
    h = x1 * lax.rsqrt(ms + NORM_EPS) * nw_ref[...]
    h = h * (1.0 + sc_ref[...]) + sh_ref[...]
    h_ref[...] = _pack_rows(h)
    hb = h.astype(BF16)
    a = jnp.dot(hb, ws1_ref[...], preferred_element_type=F32)
    u = jnp.dot(hb, ws3_ref[...], preferred_element_type=F32)
    shared = jnp.dot((_silu(a) * u).astype(BF16), ws2_ref[...], preferred_element_type=F32)
    base_ref[...] = x1 + g2_ref[...] * shared
    logits_t = lax.dot_general(wrt_ref[...], hb, NT_DIMS, preferred_element_type=F32)
    st_ref[...] = jax.nn.sigmoid(logits_t)


def _outproj(oa, ob, x2, g1, sc, sh, g2, nw, wo, ws1, ws3, ws2, wrt, seq):
    t = x2.shape[0]
    tpb = seq // ROW_TILE
    vec = lambda: pl.BlockSpec((None, 1, D_MODEL), lambda i: (i // tpb, 0, 0))
    full = lambda a: pl.BlockSpec(a.shape, lambda i: (0,) * a.ndim)
    rows = lambda w: pl.BlockSpec((ROW_TILE, w), lambda i: (i, 0))
    return pl.pallas_call(
        _outproj_kernel,
        grid=(t // ROW_TILE,),
        in_specs=[rows(MOBA_WIDTH), rows(GLA_WIDTH), rows(D_MODEL), vec(), vec(), vec(), vec(),
                  full(nw), full(wo), full(ws1), full(ws3), full(ws2), full(wrt)],
        out_specs=[rows(D_MODEL), rows(HALF), pl.BlockSpec((N_EXPERTS, ROW_TILE), lambda i: (0, i))],
        out_shape=[jax.ShapeDtypeStruct((t, D_MODEL), F32),
                   jax.ShapeDtypeStruct((t, HALF), jnp.uint32),
                   jax.ShapeDtypeStruct((N_EXPERTS, t), F32)],
        compiler_params=_params(1),
        name="outproj",
    )(oa, ob, x2, g1, sc, sh, g2, nw, wo, ws1, ws3, ws2, wrt)


SLOT_CODE_SHIFT = 16
SLOT_CODE_BASE = 1 << SLOT_CODE_SHIFT


def _route_kernel(s_ref, eb_ref, code_ref, w_ref, cnt_ref, carry_ref):
    i = pl.program_id(0)
    ne, nt = s_ref.shape

    @pl.when(i == 0)
    def _init():
        carry_ref[...] = jnp.zeros_like(carry_ref)

    s = s_ref[...]
    choice = s + eb_ref[...]
    gio = lax.broadcasted_iota(jnp.int32, (GROUP_SIZE, nt), 0)
    gscore = []
    for g in range(N_GROUPS):
        cg = choice[g * GROUP_SIZE:(g + 1) * GROUP_SIZE, :]
        top1 = jnp.max(cg, axis=0, keepdims=True)
        first = jnp.min(jnp.where(cg == top1, gio, GROUP_SIZE), axis=0, keepdims=True)
        top2 = jnp.max(jnp.where(gio == first, -jnp.inf, cg), axis=0, keepdims=True)
        gscore.append(top1 + top2)
    gs = jnp.concatenate(gscore, axis=0)
    gidx = lax.broadcasted_iota(jnp.int32, gs.shape, 0)
    beaten = jnp.zeros(gs.shape, jnp.int32)
    for m in range(N_GROUPS):
        gm = gs[m:m + 1, :]
        beaten = beaten + jnp.where((gm > gs) | ((gm == gs) & (gidx > m)), 1, 0)
    gkeep = beaten < TOPK_GROUPS
    masked = jnp.concatenate(
        [jnp.where(gkeep[g:g + 1, :], choice[g * GROUP_SIZE:(g + 1) * GROUP_SIZE, :], -jnp.inf)
         for g in range(N_GROUPS)], axis=0)

    eio = lax.broadcasted_iota(jnp.int32, (ne, nt), 0)
    picked = jnp.zeros((ne, nt), F32)
    idx_rows, w_rows, hits = [], [], []
    for _ in range(TOP_K):
        mx = jnp.max(masked, axis=0, keepdims=True)
        idx = jnp.min(jnp.where(masked == mx, eio, ne), axis=0, keepdims=True)
        hit = eio == idx
        w_rows.append(jnp.sum(jnp.where(hit, s, 0.0), axis=0, keepdims=True))
        idx_rows.append(idx)
        hits.append(hit)
        masked = jnp.where(hit, -jnp.inf, masked)
        picked = jnp.where(hit, 1.0, picked)
    wk = jnp.concatenate(w_rows, axis=0)
    w_ref[...] = wk / jnp.sum(wk, axis=0, keepdims=True) * ROUTED_SCALE

    tr = lax.broadcasted_iota(jnp.int32, (nt, nt), 0)
    tc = lax.broadcasted_iota(jnp.int32, (nt, nt), 1)
    before = jnp.where(tr < tc, 1.0, 0.0).astype(BF16)
    pb = picked.astype(BF16)
    pos = carry_ref[...] + jnp.dot(pb, before, preferred_element_type=F32)
    rank = jnp.concatenate(
        [jnp.sum(jnp.where(hit, pos, 0.0), axis=0, keepdims=True) for hit in hits], axis=0).astype(jnp.int32)
    code_ref[...] = jnp.concatenate(idx_rows, axis=0) * SLOT_CODE_BASE + rank
    total = carry_ref[...] + jnp.dot(pb, jnp.ones((nt, nt), BF16), preferred_element_type=F32)
    carry_ref[...] = total
    cnt_ref[...] = total


def _route(scores_t, eb):
    ne, t = scores_t.shape
    assert t <= SLOT_CODE_BASE
    nt = ROUTE_TILE
    tok = lambda dt: jax.ShapeDtypeStruct((TOP_K, t), dt)
    return pl.pallas_call(
        _route_kernel,
        grid=(t // nt,),
        in_specs=[pl.BlockSpec((ne, nt), lambda i: (0, i)),
                  pl.BlockSpec((ne, nt), lambda i: (0, 0))],
        out_specs=[pl.BlockSpec((TOP_K, nt), lambda i: (0, i)),
                   pl.BlockSpec((TOP_K, nt), lambda i: (0, i)),
                   pl.BlockSpec((ne, nt), lambda i: (0, 0))],
        out_shape=[tok(jnp.int32), tok(F32), jax.ShapeDtypeStruct((ne, nt), F32)],
        scratch_shapes=[pltpu.VMEM((ne, nt), F32)],
        compiler_params=_params(1),
        name="route",
    )(scores_t, eb)


SLOT_TILE = 2048


def _slots_kernel(pstart_ref, code_ref, o_ref):
    code = code_ref[...]
    expert = lax.shift_right_logical(code, SLOT_CODE_SHIFT)

    def body(e, acc):
        return jnp.where(expert == e, pstart_ref[e], acc)

    start = lax.fori_loop(0, N_EXPERTS, body, jnp.zeros_like(code), unroll=8)
    o_ref[...] = start + (code & (SLOT_CODE_BASE - 1))


def _slots(pstart, code_t):
    k, t = code_t.shape
    return pl.pallas_call(
        _slots_kernel,
        grid_spec=pltpu.PrefetchScalarGridSpec(
            num_scalar_prefetch=1,
            grid=(t // SLOT_TILE,),
            in_specs=[pl.BlockSpec((k, SLOT_TILE), lambda i, p: (0, i))],
            out_specs=pl.BlockSpec((k, SLOT_TILE), lambda i, p: (0, i)),
        ),
        out_shape=jax.ShapeDtypeStruct((k, t), jnp.int32),
        compiler_params=_params(1),
        name="slots",
    )(pstart, code_t)


SC_WINDOW = 128


def _sc_gather_rows(table, idx_flat):
    info = plsc.get_sparse_core_info()
    nw = info.num_cores * info.num_subcores
    n = idx_flat.shape[0]
    width = table.shape[1]
    per_worker = n // nw
    assert per_worker * nw == n and per_worker % SC_WINDOW == 0
    mesh = plsc.VectorSubcoreMesh(core_axis_name="c", subcore_axis_name="s")

    def body(table_hbm, idx_hbm, out_hbm, idx_v, rows_v, sem):
        wid = lax.axis_index("s") * info.num_cores + lax.axis_index("c")
        base = wid * per_worker

        @pl.loop(0, per_worker // SC_WINDOW)
        def _(w):
            off = pl.multiple_of(base + w * SC_WINDOW, SC_WINDOW)
            pltpu.sync_copy(idx_hbm.at[pl.ds(off, SC_WINDOW)], idx_v)
            pltpu.async_copy(table_hbm.at[idx_v], rows_v, sem).wait()
            pltpu.sync_copy(rows_v, out_hbm.at[pl.ds(off, SC_WINDOW)])

    return pl.kernel(
        body,
        out_type=jax.ShapeDtypeStruct((n, width), table.dtype),
        mesh=mesh,
        scratch_types=[pltpu.VMEM((SC_WINDOW,), jnp.int32),
                       pltpu.VMEM((SC_WINDOW, width), table.dtype),
                       pltpu.SemaphoreType.DMA],
        name="sc_gather",
    )(table, idx_flat)


def _sc_scatter_rows(rows, idx_kt, n_out):
    info = plsc.get_sparse_core_info()
    nw = info.num_cores * info.num_subcores
    t, width = rows.shape
    nk = idx_kt.shape[0]
    per_worker = t // nw
    assert per_worker * nw == t and per_worker % SC_WINDOW == 0
    mesh = plsc.VectorSubcoreMesh(core_axis_name="c", subcore_axis_name="s")

    def body(rows_hbm, idx_hbm, out_hbm, idx_v, rows_v, sem):
        wid = lax.axis_index("s") * info.num_cores + lax.axis_index("c")
        base = wid * per_worker

        @pl.loop(0, per_worker // SC_WINDOW)
        def _(w):
            off = pl.multiple_of(base + w * SC_WINDOW, SC_WINDOW)
            pltpu.sync_copy(rows_hbm.at[pl.ds(off, SC_WINDOW)], rows_v)
            pltpu.sync_copy(idx_hbm.at[:, pl.ds(off, SC_WINDOW)], idx_v)
            copies = [pltpu.async_copy(rows_v, out_hbm.at[idx_v.at[k]], sem) for k in range(nk)]
            for cp in copies:
                cp.wait()

    return pl.kernel(
        body,
        out_type=jax.ShapeDtypeStruct((n_out, width), rows.dtype),
        mesh=mesh,
        scratch_types=[pltpu.VMEM((nk, SC_WINDOW), jnp.int32),
                       pltpu.VMEM((SC_WINDOW, width), rows.dtype),
                       pltpu.SemaphoreType.DMA],
        name="sc_scatter",
    )(rows, idx_kt)


def _expert_kernel(first_ref, count_ref, used_ref, w1_ref, w3_ref, w2_ref, xs_ref, ys_ref,
                   xbuf, ybuf, sem_in, sem_out, w1f, w3f, w2f, sem_w, w1b, w3b, w2b, *, layer):
    e = pl.program_id(0)
    ne = pl.num_programs(0)
    r = xbuf.shape[1]
    n_used = used_ref[0]

    def x_copy(g, slot):
        return pltpu.make_async_copy(xs_ref.at[pl.ds(pl.multiple_of(g * r, r), r), :], xbuf.at[slot], sem_in.at[slot])

    def y_copy(g, slot):
        return pltpu.make_async_copy(ybuf.at[slot], ys_ref.at[pl.ds(pl.multiple_of(g * r, r), r), :], sem_out.at[slot])

    def w_copies(ex, slot):
        return [pltpu.make_async_copy(src.at[layer, ex], dst.at[slot], sem_w.at[slot])
                for src, dst in ((w1_ref, w1f), (w3_ref, w3f), (w2_ref, w2f))]

    nin = xbuf.shape[0]
    nout = ybuf.shape[0]

    nw = w1f.shape[0]

    @pl.when(e == 0)
    def _first_reads():
        for g in range(nin):
            @pl.when(g < n_used)
            def _(g=g):
                x_copy(g, g).start()
        for ex in range(nw - 1):
            for cp in w_copies(ex, ex):
                cp.start(priority=1)

    @pl.when(e + nw - 1 < ne)
    def _weights_ahead():
        for cp in w_copies(e + nw - 1, (e + nw - 1) % nw):
            cp.start(priority=1)

    wslot = e % nw
    for cp in w_copies(e, wslot):
        cp.wait()

    n = count_ref[e]

    @pl.when(n > 0)
    def _cast_weights():
        w1b[...] = w1f[wslot].astype(BF16)
        w3b[...] = w3f[wslot].astype(BF16)
        w2b[...] = w2f[wslot].astype(BF16)

    def run_tiles(g, count):
        tiles = [g + j for j in range(count)]
        for gj in tiles:
            x_copy(gj, gj % nin).wait()
        x = jnp.concatenate([xbuf[gj % nin] for gj in tiles], axis=0)
        for gj in tiles:
            @pl.when(gj + nin < n_used)
            def _(gj=gj):
                x_copy(gj + nin, gj % nin).start()
        lo, hi = _unpack_rows(x)
        lo, hi = lo.astype(BF16), hi.astype(BF16)
        a = (jnp.dot(lo, w1b[:HALF, :], preferred_element_type=F32)
             + jnp.dot(hi, w1b[HALF:, :], preferred_element_type=F32))
        u = (jnp.dot(lo, w3b[:HALF, :], preferred_element_type=F32)
             + jnp.dot(hi, w3b[HALF:, :], preferred_element_type=F32))
        y = _pack_rows(jnp.dot((_silu(a) * u).astype(BF16), w2b[...], preferred_element_type=F32))
        for j, gj in enumerate(tiles):
            @pl.when(gj >= nout)
            def _(gj=gj):
                y_copy(gj - nout, gj % nout).wait()
            ybuf[gj % nout] = y[j * r:(j + 1) * r]
            y_copy(gj, gj % nout).start()

    g0 = first_ref[e]
    pair = EXPERT_TILES_PER_MATMUL

    def pair_body(p, carry):
        run_tiles(g0 + p * pair, pair)
        return carry

    lax.fori_loop(0, n // pair, pair_body, 0)
    for left in range(1, pair):
        pl.when(n % pair == left)(functools.partial(run_tiles, g0 + n - left, left))

    @pl.when(e == ne - 1)
    def _drain_writes():
        for back in range(nout, 0, -1):
            @pl.when(n_used >= back)
            def _(back=back):
                y_copy(n_used - back, (n_used - back) % nout).wait()


def _experts(tile_first, tile_count, n_used, xs, w1, w3, w2, layer):
    n_rows = xs.shape[0]
    r = EXPERT_TILE
    any_spec = pl.BlockSpec(memory_space=pl.ANY)
    return pl.pallas_call(
        functools.partial(_expert_kernel, layer=layer),
        grid_spec=pltpu.PrefetchScalarGridSpec(
            num_scalar_prefetch=3,
            grid=(N_EXPERTS,),
            in_specs=[any_spec, any_spec, any_spec, any_spec],
            out_specs=any_spec,
            scratch_shapes=[pltpu.VMEM((EXPERT_IN_RING, r, HALF), jnp.uint32),
                            pltpu.VMEM((EXPERT_OUT_RING, r, HALF), jnp.uint32),
                            pltpu.SemaphoreType.DMA((EXPERT_IN_RING,)),
                            pltpu.SemaphoreType.DMA((EXPERT_OUT_RING,)),
                            pltpu.VMEM((EXPERT_WEIGHT_RING, D_MODEL, EXPERT_FF), F32),
                            pltpu.VMEM((EXPERT_WEIGHT_RING, D_MODEL, EXPERT_FF), F32),
                            pltpu.VMEM((EXPERT_WEIGHT_RING, EXPERT_FF, D_MODEL), F32),
                            pltpu.SemaphoreType.DMA((EXPERT_WEIGHT_RING,)),
                            pltpu.VMEM((D_MODEL, EXPERT_FF), BF16),
                            pltpu.VMEM((D_MODEL, EXPERT_FF), BF16),
                            pltpu.VMEM((EXPERT_FF, D_MODEL), BF16)],
        ),
        out_shape=jax.ShapeDtypeStruct((n_rows, HALF), jnp.uint32),
        compiler_params=_params(1),
        name="experts",
    )(tile_first, tile_count, n_used, w1, w3, w2, xs)


def _combine_dense_kernel(base_ref, g2_ref, w_ref, yg_ref, o_ref):
    acc_lo = acc_hi = None
    for k in range(TOP_K):
        lo, hi = _unpack_rows(yg_ref[k])
        wk = w_ref[:, k:k + 1]
        acc_lo = wk * lo if acc_lo is None else acc_lo + wk * lo
        acc_hi = wk * hi if acc_hi is None else acc_hi + wk * hi
    o_ref[:, :HALF] = base_ref[:, :HALF] + g2_ref[:, :HALF] * acc_lo
    o_ref[:, HALF:] = base_ref[:, HALF:] + g2_ref[:, HALF:] * acc_hi


def _combine_dense(base, g2, w_tok, yg, batch):
    t = base.shape[0]
    seq = yg.shape[1]
    nt = ROUTE_TILE
    tpb = seq // nt
    rows = lambda i: (batch * tpb + i, 0)
    return pl.pallas_call(
        _combine_dense_kernel,
        grid=(tpb,),
        in_specs=[pl.BlockSpec((nt, D_MODEL), rows),
                  pl.BlockSpec((None, 1, D_MODEL), lambda i: (batch, 0, 0)),
                  pl.BlockSpec((nt, TOP_K), rows),
                  pl.BlockSpec((TOP_K, nt, HALF), lambda i: (0, i, 0))],
        out_specs=pl.BlockSpec((nt, D_MODEL), rows),
        out_shape=jax.ShapeDtypeStruct((t, D_MODEL), F32),
        input_output_aliases={0: 0},
        compiler_params=_params(1),
        name="combine_dense",
    )(base, g2, w_tok, yg)


def _layer(layer, x, c, w_ada, b_ada, norm1_w, norm2_w, w_in, q_norm_w, k_norm_w, rel_bias, w_alpha, b_alpha,
           moba_out_w, gla_out_w, w_out, w_router, e_bias, w1, w3, w2, ws1, ws3, ws2):
    b, s, d = x.shape
    t = b * s
    x2 = x.reshape(t, d)

    mod = _mod(c, w_ada, b_ada)
    sh1, sc1, g1, sh2, sc2, g2 = [mod[:, j * d:(j + 1) * d].reshape(b, 1, d) for j in range(6)]

    w_main = w_in[:, :D_MAIN].astype(BF16)
    w_ga = jnp.zeros((d, LANES), BF16).at[:, :GLA_GATE_RANK].set(w_in[:, D_MAIN:].astype(BF16))
    per_chunk = 256 // MOBA_HEAD_DIM
    qw = jnp.tile(q_norm_w.astype(F32), per_chunk).reshape(1, 256) * (MOBA_HEAD_DIM ** -0.5 * LOG2E)
    kw = jnp.tile(k_norm_w.astype(F32), per_chunk).reshape(1, 256)
    proj, ga = _inproj(x2, sc1, sh1, norm1_w.reshape(1, d), w_main, w_ga, qw, kw, s)
    proj3 = proj.reshape(b, s, D_MAIN)

    near, far = _moba_bias_tables(rel_bias)
    ow = jnp.tile(moba_out_w.astype(F32), 2).reshape(1, LANES)
    o_a = _moba(proj3, near, far, ow)

    wal = jnp.zeros((LANES, GLA_KEY_WIDTH), F32).at[:GLA_GATE_RANK].set(w_alpha)
    o_b = _gla(proj3, ga.reshape(b, s, LANES), wal, b_alpha.reshape(1, GLA_KEY_WIDTH),
               gla_out_w.reshape(1, GLA_DV))

    base, h2, scores_t = _outproj(
        o_a.reshape(t, MOBA_WIDTH), o_b.reshape(t, GLA_WIDTH), x2, g1, sc2, sh2, g2,
        norm2_w.reshape(1, d), w_out.astype(BF16), ws1.astype(BF16), ws3.astype(BF16), ws2.astype(BF16),
        w_router.T.astype(BF16), s)

    eb = jnp.broadcast_to(e_bias.astype(F32)[:, None], (N_EXPERTS, ROUTE_TILE))
    code_t, w_t, counts = _route(scores_t, eb)

    r = EXPERT_TILE
    n_tiles = (t * TOP_K + N_EXPERTS * (r - 1) + r - 1) // r
    n_rows = n_tiles * r
    cnt = counts[:, 0].astype(jnp.int32)
    padded = (cnt + r - 1) // r * r
    pend = jnp.cumsum(padded)
    pstart = pend - padded
    n_used = (pend[-1:] // r).astype(jnp.int32)
    dest_t = _slots(pstart, code_t)

    xs = _sc_scatter_rows(h2, dest_t, n_rows)
    ys = _experts(pstart // r, padded // r, n_used, xs, w1, w3, w2, layer)
    w_tok = w_t.T
    out = base
    for bi in range(b):
        idx = dest_t[:, bi * s:(bi + 1) * s].reshape(TOP_K * s)
        yg = _sc_gather_rows(ys, idx).reshape(TOP_K, s, HALF)
        out = _combine_dense(out, g2, w_tok, yg, bi)
    return out.reshape(b, s, d)


def kernel(x, c, w_ada, b_ada, norm1_w, norm2_w, w_in, q_norm_w, k_norm_w, rel_bias, w_alpha, b_alpha,
           moba_out_w, gla_out_w, w_out, w_router, e_bias, w1, w3, w2, ws1, ws3, ws2):
    for l in range(w_ada.shape[0]):
        x = _layer(l, x, c, w_ada[l], b_ada[l], norm1_w[l], norm2_w[l], w_in[l], q_norm_w[l], k_norm_w[l],
                   rel_bias, w_alpha[l], b_alpha[l], moba_out_w[l], gla_out_w[l], w_out[l], w_router[l],
                   e_bias[l], w1, w3, w2, ws1[l], ws3[l], ws2[l])
    return x
```

```python
import functools
import math

import numpy as np
import jax
import jax.numpy as jnp
from jax import lax
from jax.experimental import pallas as pl
from jax.experimental.pallas import tpu as pltpu
from jax.experimental.pallas import tpu_sc as plsc

D_MODEL = 1024
MOBA_HEADS = 8
MOBA_HEAD_DIM = 64
MOBA_WIDTH = MOBA_HEADS * MOBA_HEAD_DIM
MOBA_BLOCK = 256
MOBA_TOPK = 3
GLA_HEADS = 4
GLA_DK = 64
GLA_DV = 128
GLA_KEY_WIDTH = GLA_HEADS * GLA_DK
GLA_WIDTH = GLA_HEADS * GLA_DV
GLA_GATE_RANK = 16
GLA_GATE_TAU = 16.0
GLA_CHUNK = 64
REL_BUCKETS = 32
REL_MAX_DIST = 128
N_EXPERTS = 256
TOP_K = 8
N_GROUPS = 8
TOPK_GROUPS = 4
GROUP_SIZE = N_EXPERTS // N_GROUPS
EXPERT_FF = 256
SHARED_FF = 256
ROUTED_SCALE = 2.5
NORM_EPS = 1e-6
LOG2E = math.log2(math.e)

D_MAIN = 3 * MOBA_WIDTH + 2 * GLA_KEY_WIDTH + 2 * GLA_WIDTH
LANES = 128
VMEM_LIMIT = 56 * 1024 * 1024

ROW_TILE = 512
ROUTE_TILE = 256
EXPERT_TILE = 128
EXPERT_TILES_PER_MATMUL = 6
EXPERT_IN_RING = 16
EXPERT_OUT_RING = 12
EXPERT_WEIGHT_RING = 3

F32 = jnp.float32
BF16 = jnp.bfloat16
NT_DIMS = (((1,), (1,)), ((), ()))
TN_DIMS = (((0,), (0,)), ((), ()))


def _params(n_axes):
    return pltpu.CompilerParams(dimension_semantics=("arbitrary",) * n_axes,
                                vmem_limit_bytes=VMEM_LIMIT)


def _silu(v):
    return v * jax.nn.sigmoid(v)


def _mod_kernel(c_ref, w_ref, b_ref, o_ref):
    o_ref[...] = jnp.dot(_silu(c_ref[...]), w_ref[...], preferred_element_type=F32) + b_ref[...]


def _mod(c, w, b):
    rows = 8
    cp = jnp.zeros((rows, D_MODEL), F32).at[:c.shape[0]].set(c)
    n = w.shape[1]
    tn = 1024
    out = pl.pallas_call(
        _mod_kernel,
        grid=(n // tn,),
        in_specs=[pl.BlockSpec((rows, D_MODEL), lambda j: (0, 0)),
                  pl.BlockSpec((D_MODEL, tn), lambda j: (0, j)),
                  pl.BlockSpec((1, tn), lambda j: (0, j))],
        out_specs=pl.BlockSpec((rows, tn), lambda j: (0, j)),
        out_shape=jax.ShapeDtypeStruct((rows, n), F32),
        compiler_params=_params(1),
        name="mod",
    )(cp, w, b.reshape(1, n))
    return out[:c.shape[0]]


def _group_rms_inv(a, group):
    lane = lax.broadcasted_iota(jnp.int32, (1, a.shape[1]), 1)
    a2 = a * a
    inv = jnp.zeros_like(a)
    for g in range(a.shape[1] // group):
        m = (lane >= g * group) & (lane < (g + 1) * group)
        ss = jnp.sum(jnp.where(m, a2, 0.0), axis=-1, keepdims=True)
        inv = jnp.where(m, lax.rsqrt(ss * (1.0 / group) + NORM_EPS), inv)
    return inv


def _inproj_kernel(x_ref, sc_ref, sh_ref, nw_ref, w_ref, wga_ref, qw_ref, kw_ref, o_ref, ga_ref):
    x = x_ref[...]
    ms = jnp.mean(x * x, axis=-1, keepdims=True)
    h = x * lax.rsqrt(ms + NORM_EPS) * nw_ref[...]
    h = h * (1.0 + sc_ref[...]) + sh_ref[...]
    hb = h.astype(BF16)
    cw = 256
    for j in range(D_MAIN // cw):
        acc = jnp.dot(hb, w_ref[:, j * cw:(j + 1) * cw], preferred_element_type=F32)
        if j < 2 * MOBA_WIDTH // cw:
            nw = qw_ref if j < MOBA_WIDTH // cw else kw_ref
            acc = acc * _group_rms_inv(acc, MOBA_HEAD_DIM) * nw[...]
        o_ref[:, j * cw:(j + 1) * cw] = acc.astype(BF16)
    ga_ref[...] = jnp.dot(hb, wga_ref[...], preferred_element_type=F32)


def _inproj(x2, sc, sh, nw, w_main, w_ga, qw, kw, seq):
    t = x2.shape[0]
    tpb = seq // ROW_TILE
    vec = lambda: pl.BlockSpec((None, 1, D_MODEL), lambda i: (i // tpb, 0, 0))
    full = lambda a: pl.BlockSpec(a.shape, lambda i: (0,) * a.ndim)
    return pl.pallas_call(
        _inproj_kernel,
        grid=(t // ROW_TILE,),
        in_specs=[pl.BlockSpec((ROW_TILE, D_MODEL), lambda i: (i, 0)), vec(), vec(),
                  full(nw), full(w_main), full(w_ga), full(qw), full(kw)],
        out_specs=[pl.BlockSpec((ROW_TILE, D_MAIN), lambda i: (i, 0)),
                   pl.BlockSpec((ROW_TILE, LANES), lambda i: (i, 0))],
        out_shape=[jax.ShapeDtypeStruct((t, D_MAIN), BF16),
                   jax.ShapeDtypeStruct((t, LANES), F32)],
        compiler_params=_params(1),
        name="inproj",
    )(x2, sc, sh, nw, w_main, w_ga, qw, kw)


def _t5_bucket_np(rel):
    max_exact = REL_BUCKETS // 2
    relf = np.maximum(rel, 1).astype(np.float64)
    large = max_exact + (np.log(relf / max_exact) / math.log(REL_MAX_DIST / max_exact)
                         * (REL_BUCKETS - max_exact)).astype(np.int32)
    large = np.minimum(large, REL_BUCKETS - 1)
    return np.where(rel < max_exact, rel, large)


def _bias_kernel(rb_ref, idx_ref, o_ref):
    h = pl.program_id(0)
    idx = idx_ref[...]
    tab = jnp.full(idx.shape, -jnp.inf, F32)
    for bk in range(REL_BUCKETS):
        tab = jnp.where(idx == bk, rb_ref[bk * MOBA_HEADS + h], tab)
    o_ref[...] = tab


def _moba_bias_tables(rel_bias):
    j = np.arange(MOBA_BLOCK)[:, None]
    i = np.arange(MOBA_BLOCK)[None, :]
    own_idx = np.where(j <= i, _t5_bucket_np(np.maximum(i - j, 0)), -1)
    prev_idx = _t5_bucket_np(MOBA_BLOCK + i - j)
    idx = jnp.asarray(np.concatenate([prev_idx, own_idx], axis=0).astype(np.int32))
    assert int(_t5_bucket_np(np.array([MOBA_BLOCK + 1]))[0]) == REL_BUCKETS - 1
    rb = rel_bias.astype(F32) * LOG2E
    near = pl.pallas_call(
        _bias_kernel,
        grid=(MOBA_HEADS,),
        in_specs=[pl.BlockSpec(memory_space=pltpu.SMEM),
                  pl.BlockSpec(idx.shape, lambda h: (0, 0))],
        out_specs=pl.BlockSpec((None,) + idx.shape, lambda h: (h, 0, 0)),
        out_shape=jax.ShapeDtypeStruct((MOBA_HEADS,) + idx.shape, F32),
        compiler_params=_params(1),
        name="bias",
    )(rb.reshape(-1), idx)
    return near, rb[REL_BUCKETS - 1]


FAR_GROUP = 4


MOBA_INTERLEAVE = 4


def _moba_kernel(*refs):
    hp = pl.program_id(1)
    nsets = refs[2].shape[0] // (MOBA_INTERLEAVE * MOBA_BLOCK)
    _moba_body(None, hp, *refs, prepare=True)

    def block_set(j, carry):
        for jj in range(nsets):
            blocks = list(range(jj * MOBA_INTERLEAVE, (jj + 1) * MOBA_INTERLEAVE))
            pl.when(j == jj)(functools.partial(_moba_body, blocks, hp, *refs, prepare=False))
        return carry

    lax.fori_loop(0, nsets, block_set, 0)


def _moba_body(blocks, hp, far_ref, q_ref, k_ref, v_ref, near_ref, ow_ref, o_ref,
               vt_ref, vtg_ref, acc_ref, m_ref, sel_ref, s_ref, mx_ref, *, prepare):
    nblk = k_ref.shape[0] // MOBA_BLOCK
    ngrp = nblk // FAR_GROUP
    hd = MOBA_HEAD_DIM
    bs = MOBA_BLOCK
    lane = lax.broadcasted_iota(jnp.int32, (bs, LANES), 1)

    def split_heads(qb):
        zero = jnp.zeros_like(qb)
        return jnp.where(lane < hd, qb, zero), jnp.where(lane < hd, zero, qb)

    def _prepare():
        row = lax.broadcasted_iota(jnp.int32, (LANES, bs), 0)
        kmeans = []
        for n in range(nblk):
            kb = k_ref[n * bs:(n + 1) * bs, :].astype(F32)
            kmeans.append(jnp.mean(kb, axis=0, keepdims=True))
            vt = v_ref[n * bs:(n + 1) * bs, :].astype(F32).T
            vt0 = jnp.where(row < hd, vt, 1.0).astype(BF16)
            vt1 = jnp.where(row < hd, 1.0, vt).astype(BF16)
            vt_ref[0, n] = vt0
            vt_ref[1, n] = vt1
            gcols = slice((n % FAR_GROUP) * bs, (n % FAR_GROUP + 1) * bs)
            vtg_ref[0, n // FAR_GROUP, :, gcols] = vt0
            vtg_ref[1, n // FAR_GROUP, :, gcols] = vt1
        kmean = jnp.concatenate(kmeans, axis=0)
        km_hi = kmean.astype(BF16)
        km_lo = (kmean - km_hi.astype(F32)).astype(BF16)
        blk = lax.broadcasted_iota(jnp.int32, (nblk, bs), 0)
        for ib in range(nblk):
            qparts = split_heads(q_ref[ib * bs:(ib + 1) * bs, :])
            for h in range(2):
                gt = (lax.dot_general(km_hi, qparts[h], NT_DIMS, preferred_element_type=F32)
                      + lax.dot_general(km_lo, qparts[h], NT_DIMS, preferred_element_type=F32))
                gt = jnp.where(blk < ib, gt, -jnp.inf)
                cnt = jnp.zeros(gt.shape, jnp.int32)
                for m in range(ib):
                    gm = gt[m:m + 1, :]
                    cnt = cnt + jnp.where((gm > gt) | ((gm == gt) & (blk > m)), 1, 0)
                keep = (blk < ib) & (cnt < MOBA_TOPK)
                sel_ref[0, h, ib] = jnp.where(keep, 1.0, 0.0)
                sel_ref[1, h, ib] = jnp.where(keep & (blk < ib - 1), 1.0, 0.0)

    if prepare:
        _prepare()
        return

    gk = FAR_GROUP * bs

    def rows_of(ib):
        if isinstance(ib, int):
            return slice(ib * bs, (ib + 1) * bs)
        return pl.ds(pl.multiple_of(ib * bs, bs), bs)

    def finish(ib, slot):
        a0 = acc_ref[slot, 0]
        a1 = acc_ref[slot, 1]
        row = lax.broadcasted_iota(jnp.int32, a0.shape, 0)
        ot = jnp.where(row < hd, a0 / a0[hd:hd + 1, :], a1 / a1[0:1, :])
        o2 = ot * ot
        ss0 = jnp.sum(jnp.where(row < hd, o2, 0.0), axis=0, keepdims=True)
        ss1 = jnp.sum(jnp.where(row < hd, 0.0, o2), axis=0, keepdims=True)
        inv = jnp.where(row < hd, lax.rsqrt(ss0 * (1.0 / hd) + NORM_EPS), lax.rsqrt(ss1 * (1.0 / hd) + NORM_EPS))
        o_ref[rows_of(ib), :] = ((ot * inv).T * ow_ref[...]).astype(o_ref.dtype)

    def far_scores(g, slot, qh, nb):
        kb = k_ref[g * gk:g * gk + nb * bs, :]
        for h in range(2):
            s = lax.dot_general(kb, qh[h], NT_DIMS, preferred_element_type=F32)
            s_ref[slot, g % 2, h, 0:nb * bs, :] = s
            for j in range(nb):
                mx_ref[slot, g % 2, h, j] = jnp.max(s[j * bs:(j + 1) * bs], axis=0, keepdims=True)

    def near_scores(ib, qh):
        kbs = (k_ref[rows_of(ib - 1), :], k_ref[rows_of(ib), :])
        return [[lax.dot_general(kbs[w], qh[h], NT_DIMS, preferred_element_type=F32)
                 + near_ref[h, w * bs:(w + 1) * bs, :] for w in range(2)] for h in range(2)]

    def near_values(ib, slot, ss):
        ps, ms = [], []
        for h in range(2):
            s_prev, s_own = ss[h]
            keep = sel_ref[0, h, ib, pl.ds(ib - 1, 1), :] > 0.5
            mx = jnp.where(keep, jnp.max(s_prev, axis=0, keepdims=True), -jnp.inf)
            m_new = jnp.maximum(jnp.max(s_own, axis=0, keepdims=True), mx)
            ps.append((jnp.exp2(s_prev - jnp.where(keep, m_new, jnp.inf)).astype(BF16),
                       jnp.exp2(s_own - m_new).astype(BF16)))
            ms.append(m_new)
        for h in range(2):
            acc_ref[slot, h] = (jnp.dot(vt_ref[h, ib - 1], ps[h][0], preferred_element_type=F32)
                                + jnp.dot(vt_ref[h, ib], ps[h][1], preferred_element_type=F32))
            m_ref[slot, h] = ms[h]

    def far_group(g, ib, slot, nb):
        for h in range(2):
            fb = far_ref[2 * hp + h]
            m_old = m_ref[slot, h]
            m_new = m_old
            keeps = []
            for j in range(nb):
                keep = sel_ref[1, h, ib, pl.ds(g * FAR_GROUP + j, 1), :] > 0.5
                m_new = jnp.maximum(m_new, jnp.where(keep, mx_ref[slot, g % 2, h, j] + fb, -jnp.inf))
                keeps.append(keep)
            p = jnp.concatenate(
                [jnp.exp2(s_ref[slot, g % 2, h, j * bs:(j + 1) * bs, :]
                          - jnp.where(keeps[j], m_new - fb, jnp.inf)).astype(BF16)
                 for j in range(nb)], axis=0)
            pv = jnp.dot(vtg_ref[h, g, :, 0:nb * bs], p, preferred_element_type=F32)
            acc_ref[slot, h] = acc_ref[slot, h] * jnp.exp2(m_old - m_new) + pv
            m_ref[slot, h] = m_new

    def step_body(blocks):
        slots = list(enumerate(blocks))
        nbs = [[min(FAR_GROUP, ib - 1 - g * FAR_GROUP) for g in range((ib - 2) // FAR_GROUP + 1)] if ib >= 2 else []
               for ib in blocks]
        qhs = [split_heads(q_ref[rows_of(ib), :]) for ib in blocks]
        near = [near_scores(ib, qhs[slot]) for slot, ib in slots]
        for slot, ib in slots:
            if nbs[slot]:
                far_scores(0, slot, qhs[slot], nbs[slot][0])
        for slot, ib in slots:
            near_values(ib, slot, near[slot])
        for g in range(max(len(n) for n in nbs)):
            for slot, ib in slots:
                if g + 1 < len(nbs[slot]):
                    far_scores(g + 1, slot, qhs[slot], nbs[slot][g + 1])
            for slot, ib in slots:
                if g < len(nbs[slot]):
                    far_group(g, ib, slot, nbs[slot][g])
        for slot, ib in slots:
            finish(ib, slot)

    if blocks[0] == 0:
        qh = split_heads(q_ref[0:bs, :])
        kb = k_ref[0:bs, :]
        for h in range(2):
            s = lax.dot_general(kb, qh[h], NT_DIMS, preferred_element_type=F32) + near_ref[h, bs:2 * bs, :]
            p = jnp.exp2(s - jnp.max(s, axis=0, keepdims=True)).astype(BF16)
            acc_ref[MOBA_INTERLEAVE - 1, h] = jnp.dot(vt_ref[h, 0], p, preferred_element_type=F32)
        finish(0, MOBA_INTERLEAVE - 1)
        blocks = blocks[1:]
    step_body(blocks)


def _moba(proj3, near, far, ow):
    b, s, _ = proj3.shape
    nblk = s // MOBA_BLOCK
    assert nblk % FAR_GROUP == 0
    npair = MOBA_HEADS // 2
    kcol = MOBA_WIDTH // LANES
    return pl.pallas_call(
        _moba_kernel,
        grid=(b, npair),
        in_specs=[pl.BlockSpec(memory_space=pltpu.SMEM),
                  pl.BlockSpec((None, s, LANES), lambda bb, hp: (bb, 0, hp)),
                  pl.BlockSpec((None, s, LANES), lambda bb, hp: (bb, 0, kcol + hp)),
                  pl.BlockSpec((None, s, LANES), lambda bb, hp: (bb, 0, 2 * kcol + hp)),
                  pl.BlockSpec((2, 2 * MOBA_BLOCK, MOBA_BLOCK), lambda bb, hp: (hp, 0, 0)),
                  pl.BlockSpec((1, LANES), lambda bb, hp: (0, 0))],
        out_specs=pl.BlockSpec((None, s, LANES), lambda bb, hp: (bb, 0, hp)),
        out_shape=jax.ShapeDtypeStruct((b, s, MOBA_WIDTH), BF16),
        scratch_shapes=[pltpu.VMEM((2, nblk, LANES, MOBA_BLOCK), BF16),
                        pltpu.VMEM((2, nblk // FAR_GROUP, LANES, FAR_GROUP * MOBA_BLOCK), BF16),
                        pltpu.VMEM((MOBA_INTERLEAVE, 2, LANES, MOBA_BLOCK), F32),
                        pltpu.VMEM((MOBA_INTERLEAVE, 2, 1, MOBA_BLOCK), F32),
                        pltpu.VMEM((2, 2, nblk, nblk, MOBA_BLOCK), F32),
                        pltpu.VMEM((MOBA_INTERLEAVE, 2, 2, FAR_GROUP * MOBA_BLOCK, MOBA_BLOCK), F32),
                        pltpu.VMEM((MOBA_INTERLEAVE, 2, 2, FAR_GROUP, 1, MOBA_BLOCK), F32)],
        compiler_params=_params(2),
        name="moba",
    )(far, proj3, proj3, proj3, near, ow)


def _split3(v):
    hi = v.astype(BF16)
    r1 = v - hi.astype(F32)
    mid = r1.astype(BF16)
    lo = (r1 - mid.astype(F32)).astype(BF16)
    return hi, mid, lo


GLA_UNROLL = 16


def _gla_kernel(q_ref, k_ref, v_ref, g_ref, ga_ref, wal_ref, bal_ref, gw_ref, o_ref, b_ref, st_ref):
    seq = q_ref.shape[0]
    c = GLA_CHUNK
    pc = 256

    rr = lax.broadcasted_iota(jnp.int32, (pc, pc), 0)
    cc = lax.broadcasted_iota(jnp.int32, (pc, pc), 1)
    tri = jnp.where((rr >= cc) & (rr // c == cc // c), 1.0, 0.0).astype(BF16)

    def decay_body(j, carry):
        rows = [pl.ds(pl.multiple_of((j * GLA_UNROLL + u) * pc, pc), pc) for u in range(GLA_UNROLL)]
        xg = [jnp.dot(ga_ref[r, :], wal_ref[...], preferred_element_type=F32) + bal_ref[...] for r in rows]
        parts = [_split3((jnp.minimum(x, 0.0) - jnp.log(1.0 + jnp.exp(-jnp.abs(x)))) * (1.0 / GLA_GATE_TAU))
                 for x in xg]
        sums = [[jnp.dot(tri, term, preferred_element_type=F32) for term in p] for p in parts]
        for r, (hi, mid, lo) in zip(rows, sums):
            b_ref[r, :] = hi + mid + lo
        return carry

    lax.fori_loop(0, seq // (pc * GLA_UNROLL), decay_body, 0)

    st_ref[...] = jnp.zeros_like(st_ref)
    lane = lax.broadcasted_iota(jnp.int32, (c, LANES), 1)
    head_mask = (lane < GLA_DK, lane >= GLA_DK)
    causal = lax.broadcasted_iota(jnp.int32, (c, c), 0) >= lax.broadcasted_iota(jnp.int32, (c, c), 1)

    units = [(u, h) for u in range(GLA_UNROLL) for h in range(2)]

    def chunk_body(ci, carry):
        rows = [pl.ds(pl.multiple_of((ci * GLA_UNROLL + u) * c, c), c) for u in range(GLA_UNROLL)]
        qt, kt, qs, ke, e_last = [], [], [], [], []
        for u in range(GLA_UNROLL):
            b = b_ref[rows[u], :]
            ref_row = b[c // 2 - 1:c // 2, :]
            last = b[c - 1:c, :]
            q = q_ref[rows[u], :].astype(F32) * (GLA_DK ** -0.5)
            k = k_ref[rows[u], :].astype(F32)
            qt.append(q * jnp.exp(b - ref_row))
            kt.append((k * jnp.exp(ref_row - b)).astype(BF16))
            qs.append(q * jnp.exp(b))
            ke.append((k * jnp.exp(last - b)).astype(BF16))
            e_last.append(jnp.exp(last))
        vs = {(u, h): v_ref[rows[u], h * GLA_DV:(h + 1) * GLA_DV] for u, h in units}
        a = {(u, h): lax.dot_general(jnp.where(head_mask[h], qt[u], 0.0).astype(BF16), kt[u], NT_DIMS,
                                     preferred_element_type=F32) for u, h in units}
        inc = {(u, h): lax.dot_general(vs[u, h], ke[u], TN_DIMS, preferred_element_type=F32) for u, h in units}
        o = {(u, h): jnp.dot(jnp.where(causal, a[u, h], 0.0).astype(BF16), vs[u, h], preferred_element_type=F32)
             for u, h in units}
        states = {}
        for h in range(2):
            st = st_ref[h]
            for u in range(GLA_UNROLL):
                states[u, h] = st
                st = st * e_last[u] + inc[u, h]
            st_ref[h] = st
        for u, h in units:
            cols = slice(h * GLA_DV, (h + 1) * GLA_DV)
            ou = o[u, h] + lax.dot_general(jnp.where(head_mask[h], qs[u], 0.0).astype(BF16),
                                           states[u, h].astype(BF16), NT_DIMS, preferred_element_type=F32)
            ms = jnp.mean(ou * ou, axis=-1, keepdims=True)
            on = ou * lax.rsqrt(ms + NORM_EPS) * gw_ref[...]
            g = g_ref[rows[u], cols].astype(F32)
            o_ref[rows[u], cols] = (on * _silu(g)).astype(o_ref.dtype)
        return carry

    lax.fori_loop(0, seq // (c * GLA_UNROLL), chunk_body, 0)


def _gla(proj3, ga3, wal, bal, gw):
    b, s, _ = proj3.shape
    npair = GLA_HEADS // 2
    qcol = 3 * MOBA_WIDTH // LANES
    kcol = qcol + GLA_KEY_WIDTH // LANES
    vcol = (3 * MOBA_WIDTH + 2 * GLA_KEY_WIDTH) // (2 * GLA_DV)
    gcol = vcol + npair
    return pl.pallas_call(
        _gla_kernel,
        grid=(b, npair),
        in_specs=[pl.BlockSpec((None, s, LANES), lambda bb, hp: (bb, 0, qcol + hp)),
                  pl.BlockSpec((None, s, LANES), lambda bb, hp: (bb, 0, kcol + hp)),
                  pl.BlockSpec((None, s, 2 * GLA_DV), lambda bb, hp: (bb, 0, vcol + hp)),
                  pl.BlockSpec((None, s, 2 * GLA_DV), lambda bb, hp: (bb, 0, gcol + hp)),
                  pl.BlockSpec((None, s, LANES), lambda bb, hp: (bb, 0, 0)),
                  pl.BlockSpec((LANES, LANES), lambda bb, hp: (0, hp)),
                  pl.BlockSpec((1, LANES), lambda bb, hp: (0, hp)),
                  pl.BlockSpec((1, GLA_DV), lambda bb, hp: (0, 0))],
        out_specs=pl.BlockSpec((None, s, 2 * GLA_DV), lambda bb, hp: (bb, 0, hp)),
        out_shape=jax.ShapeDtypeStruct((b, s, GLA_WIDTH), BF16),
        scratch_shapes=[pltpu.VMEM((s, LANES), F32),
                        pltpu.VMEM((2, GLA_DV, LANES), F32)],
        compiler_params=_params(2),
        name="gla",
    )(proj3, proj3, proj3, proj3, ga3, wal, bal, gw)


HALF = D_MODEL // 2


def _pack_rows(v):
    return pltpu.pack_elementwise([v[:, :HALF], v[:, HALF:]], packed_dtype=BF16)


def _unpack_rows(w):
    return (pltpu.unpack_elementwise(w, index=0, packed_dtype=BF16, unpacked_dtype=F32),
            pltpu.unpack_elementwise(w, index=1, packed_dtype=BF16, unpacked_dtype=F32))


def _outproj_kernel(oa_ref, ob_ref, x_ref, g1_ref, sc_ref, sh_ref, g2_ref, nw_ref, wo_ref,
                    ws1_ref, ws3_ref, ws2_ref, wrt_ref, base_ref, h_ref, st_ref):
    mix = (jnp.dot(oa_ref[...], wo_ref[:MOBA_WIDTH, :], preferred_element_type=F32)
           + jnp.dot(ob_ref[...], wo_ref[MOBA_WIDTH:, :], preferred_element_type=F32))
    x1 = x_ref[...] + g1_ref[...] * mix
    ms = jnp.mean(x1 * x1, axis=-1, keepdims=True)
    h = x1 * lax.rsqrt(ms + NORM_EPS) * nw_ref[...]
    h = h * (1.0 + sc_ref[...]) + sh_ref[...]
    h_ref[...] = _pack_rows(h)
    hb = h.astype(BF16)
    a = jnp.dot(hb, ws1_ref[...], preferred_element_type=F32)
    u = jnp.dot(hb, ws3_ref[...], preferred_element_type=F32)
    shared = jnp.dot((_silu(a) * u).astype(BF16), ws2_ref[...], preferred_element_type=F32)
    base_ref[...] = x1 + g2_ref[...] * shared
    logits_t = lax.dot_general(wrt_ref[...], hb, NT_DIMS, preferred_element_type=F32)
    st_ref[...] = jax.nn.sigmoid(logits_t)


def _outproj(oa, ob, x2, g1, sc, sh, g2, nw, wo, ws1, ws3, ws2, wrt, seq):
    t = x2.shape[0]
    tpb = seq // ROW_TILE
    vec = lambda: pl.BlockSpec((None, 1, D_MODEL), lambda i: (i // tpb, 0, 0))
    full = lambda a: pl.BlockSpec(a.shape, lambda i: (0,) * a.ndim)
    rows = lambda w: pl.BlockSpec((ROW_TILE, w), lambda i: (i, 0))
    return pl.pallas_call(
        _outproj_kernel,
        grid=(t // ROW_TILE,),
        in_specs=[rows(MOBA_WIDTH), rows(GLA_WIDTH), rows(D_MODEL), vec(), vec(), vec(), vec(),
                  full(nw), full(wo), full(ws1), full(ws3), full(ws2), full(wrt)],
        out_specs=[rows(D_MODEL), rows(HALF), pl.BlockSpec((N_EXPERTS, ROW_TILE), lambda i: (0, i))],
        out_shape=[jax.ShapeDtypeStruct((t, D_MODEL), F32),
                   jax.ShapeDtypeStruct((t, HALF), jnp.uint32),
                   jax.ShapeDtypeStruct((N_EXPERTS, t), F32)],
        compiler_params=_params(1),
        name="outproj",
    )(oa, ob, x2, g1, sc, sh, g2, nw, wo, ws1, ws3, ws2, wrt)


SLOT_CODE_SHIFT = 16
SLOT_CODE_BASE = 1 << SLOT_CODE_SHIFT


def _route_kernel(s_ref, eb_ref, code_ref, w_ref, cnt_ref, carry_ref):
    i = pl.program_id(0)
    ne, nt = s_ref.shape

    @pl.when(i == 0)
    def _init():
        carry_ref[...] = jnp.zeros_like(carry_ref)

    s = s_ref[...]
    choice = s + eb_ref[...]
    gio = lax.broadcasted_iota(jnp.int32, (GROUP_SIZE, nt), 0)
    gscore = []
    for g in range(N_GROUPS):
        cg = choice[g * GROUP_SIZE:(g + 1) * GROUP_SIZE, :]
        top1 = jnp.max(cg, axis=0, keepdims=True)
        first = jnp.min(jnp.where(cg == top1, gio, GROUP_SIZE), axis=0, keepdims=True)
        top2 = jnp.max(jnp.where(gio == first, -jnp.inf, cg), axis=0, keepdims=True)
        gscore.append(top1 + top2)
    gs = jnp.concatenate(gscore, axis=0)
    gidx = lax.broadcasted_iota(jnp.int32, gs.shape, 0)
    beaten = jnp.zeros(gs.shape, jnp.int32)
    for m in range(N_GROUPS):
        gm = gs[m:m + 1, :]
        beaten = beaten + jnp.where((gm > gs) | ((gm == gs) & (gidx > m)), 1, 0)
    gkeep = beaten < TOPK_GROUPS
    masked = jnp.concatenate(
        [jnp.where(gkeep[g:g + 1, :], choice[g * GROUP_SIZE:(g + 1) * GROUP_SIZE, :], -jnp.inf)
         for g in range(N_GROUPS)], axis=0)

    eio = lax.broadcasted_iota(jnp.int32, (ne, nt), 0)
    picked = jnp.zeros((ne, nt), F32)
    idx_rows, w_rows, hits = [], [], []
    for _ in range(TOP_K):
        mx = jnp.max(masked, axis=0, keepdims=True)
        idx = jnp.min(jnp.where(masked == mx, eio, ne), axis=0, keepdims=True)
        hit = eio == idx
        w_rows.append(jnp.sum(jnp.where(hit, s, 0.0), axis=0, keepdims=True))
        idx_rows.append(idx)
        hits.append(hit)
        masked = jnp.where(hit, -jnp.inf, masked)
        picked = jnp.where(hit, 1.0, picked)
    wk = jnp.concatenate(w_rows, axis=0)
    w_ref[...] = wk / jnp.sum(wk, axis=0, keepdims=True) * ROUTED_SCALE

    tr = lax.broadcasted_iota(jnp.int32, (nt, nt), 0)
    tc = lax.broadcasted_iota(jnp.int32, (nt, nt), 1)
    before = jnp.where(tr < tc, 1.0, 0.0).astype(BF16)
    pb = picked.astype(BF16)
    pos = carry_ref[...] + jnp.dot(pb, before, preferred_element_type=F32)
    rank = jnp.concatenate(
        [jnp.sum(jnp.where(hit, pos, 0.0), axis=0, keepdims=True) for hit in hits], axis=0).astype(jnp.int32)
    code_ref[...] = jnp.concatenate(idx_rows, axis=0) * SLOT_CODE_BASE + rank
    total = carry_ref[...] + jnp.dot(pb, jnp.ones((nt, nt), BF16), preferred_element_type=F32)
    carry_ref[...] = total
    cnt_ref[...] = total


def _route(scores_t, eb):
    ne, t = scores_t.shape
    assert t <= SLOT_CODE_BASE
    nt = ROUTE_TILE
    tok = lambda dt: jax.ShapeDtypeStruct((TOP_K, t), dt)
    return pl.pallas_call(
        _route_kernel,
        grid=(t // nt,),
        in_specs=[pl.BlockSpec((ne, nt), lambda i: (0, i)),
                  pl.BlockSpec((ne, nt), lambda i: (0, 0))],
        out_specs=[pl.BlockSpec((TOP_K, nt), lambda i: (0, i)),
                   pl.BlockSpec((TOP_K, nt), lambda i: (0, i)),
                   pl.BlockSpec((ne, nt), lambda i: (0, 0))],
        out_shape=[tok(jnp.int32), tok(F32), jax.ShapeDtypeStruct((ne, nt), F32)],
        scratch_shapes=[pltpu.VMEM((ne, nt), F32)],
        compiler_params=_params(1),
        name="route",
    )(scores_t, eb)


SLOT_TILE = 2048


def _slots_kernel(pstart_ref, code_ref, o_ref):
    code = code_ref[...]
    expert = lax.shift_right_logical(code, SLOT_CODE_SHIFT)

    def body(e, acc):
        return jnp.where(expert == e, pstart_ref[e], acc)

    start = lax.fori_loop(0, N_EXPERTS, body, jnp.zeros_like(code), unroll=8)
    o_ref[...] = start + (code & (SLOT_CODE_BASE - 1))


def _slots(pstart, code_t):
    k, t = code_t.shape
    return pl.pallas_call(
        _slots_kernel,
        grid_spec=pltpu.PrefetchScalarGridSpec(
            num_scalar_prefetch=1,
            grid=(t // SLOT_TILE,),
            in_specs=[pl.BlockSpec((k, SLOT_TILE), lambda i, p: (0, i))],
            out_specs=pl.BlockSpec((k, SLOT_TILE), lambda i, p: (0, i)),
        ),
        out_shape=jax.ShapeDtypeStruct((k, t), jnp.int32),
        compiler_params=_params(1),
        name="slots",
    )(pstart, code_t)


SC_WINDOW = 128


def _sc_gather_rows(table, idx_flat):
    info = plsc.get_sparse_core_info()
    nw = info.num_cores * info.num_subcores
    n = idx_flat.shape[0]
    width = table.shape[1]
    per_worker = n // nw
    assert per_worker * nw == n and per_worker % SC_WINDOW == 0
    mesh = plsc.VectorSubcoreMesh(core_axis_name="c", subcore_axis_name="s")

    def body(table_hbm, idx_hbm, out_hbm, idx_v, rows_v, sem):
        wid = lax.axis_index("s") * info.num_cores + lax.axis_index("c")
        base = wid * per_worker

        @pl.loop(0, per_worker // SC_WINDOW)
        def _(w):
            off = pl.multiple_of(base + w * SC_WINDOW, SC_WINDOW)
            pltpu.sync_copy(idx_hbm.at[pl.ds(off, SC_WINDOW)], idx_v)
            pltpu.async_copy(table_hbm.at[idx_v], rows_v, sem).wait()
            pltpu.sync_copy(rows_v, out_hbm.at[pl.ds(off, SC_WINDOW)])

    return pl.kernel(
        body,
        out_type=jax.ShapeDtypeStruct((n, width), table.dtype),
        mesh=mesh,
        scratch_types=[pltpu.VMEM((SC_WINDOW,), jnp.int32),
                       pltpu.VMEM((SC_WINDOW, width), table.dtype),
                       pltpu.SemaphoreType.DMA],
        name="sc_gather",
    )(table, idx_flat)


def _sc_scatter_rows(rows, idx_kt, n_out):
    info = plsc.get_sparse_core_info()
    nw = info.num_cores * info.num_subcores
    t, width = rows.shape
    nk = idx_kt.shape[0]
    per_worker = t // nw
    assert per_worker * nw == t and per_worker % SC_WINDOW == 0
    mesh = plsc.VectorSubcoreMesh(core_axis_name="c", subcore_axis_name="s")

    def body(rows_hbm, idx_hbm, out_hbm, idx_v, rows_v, sem):
        wid = lax.axis_index("s") * info.num_cores + lax.axis_index("c")
        base = wid * per_worker

        @pl.loop(0, per_worker // SC_WINDOW)
        def _(w):
            off = pl.multiple_of(base + w * SC_WINDOW, SC_WINDOW)
            pltpu.sync_copy(rows_hbm.at[pl.ds(off, SC_WINDOW)], rows_v)
            pltpu.sync_copy(idx_hbm.at[:, pl.ds(off, SC_WINDOW)], idx_v)
            copies = [pltpu.async_copy(rows_v, out_hbm.at[idx_v.at[k]], sem) for k in range(nk)]
            for cp in copies:
                cp.wait()

    return pl.kernel(
        body,
        out_type=jax.ShapeDtypeStruct((n_out, width), rows.dtype),
        mesh=mesh,
        scratch_types=[pltpu.VMEM((nk, SC_WINDOW), jnp.int32),
                       pltpu.VMEM((SC_WINDOW, width), rows.dtype),
                       pltpu.SemaphoreType.DMA],
        name="sc_scatter",
    )(rows, idx_kt)


def _expert_kernel(first_ref, count_ref, used_ref, w1_ref, w3_ref, w2_ref, xs_ref, ys_ref,
                   xbuf, ybuf, sem_in, sem_out, w1f, w3f, w2f, sem_w, w1b, w3b, w2b, *, layer):
    e = pl.program_id(0)
    ne = pl.num_programs(0)
    r = xbuf.shape[1]
    n_used = used_ref[0]

    def x_copy(g, slot):
        return pltpu.make_async_copy(xs_ref.at[pl.ds(pl.multiple_of(g * r, r), r), :], xbuf.at[slot], sem_in.at[slot])

    def y_copy(g, slot):
        return pltpu.make_async_copy(ybuf.at[slot], ys_ref.at[pl.ds(pl.multiple_of(g * r, r), r), :], sem_out.at[slot])

    def w_copies(ex, slot):
        return [pltpu.make_async_copy(src.at[layer, ex], dst.at[slot], sem_w.at[slot])
                for src, dst in ((w1_ref, w1f), (w3_ref, w3f), (w2_ref, w2f))]

    nin = xbuf.shape[0]
    nout = ybuf.shape[0]

    nw = w1f.shape[0]

    @pl.when(e == 0)
    def _first_reads():
        for g in range(nin):
            @pl.when(g < n_used)
            def _(g=g):
                x_copy(g, g).start()
        for ex in range(nw - 1):
            for cp in w_copies(ex, ex):
                cp.start(priority=1)

    @pl.when(e + nw - 1 < ne)
    def _weights_ahead():
        for cp in w_copies(e + nw - 1, (e + nw - 1) % nw):
            cp.start(priority=1)

    wslot = e % nw
    for cp in w_copies(e, wslot):
        cp.wait()

    n = count_ref[e]

    @pl.when(n > 0)
    def _cast_weights():
        w1b[...] = w1f[wslot].astype(BF16)
        w3b[...] = w3f[wslot].astype(BF16)
        w2b[...] = w2f[wslot].astype(BF16)

    def run_tiles(g, count):
        tiles = [g + j for j in range(count)]
        for gj in tiles:
            x_copy(gj, gj % nin).wait()
        x = jnp.concatenate([xbuf[gj % nin] for gj in tiles], axis=0)
        for gj in tiles:
            @pl.when(gj + nin < n_used)
            def _(gj=gj):
                x_copy(gj + nin, gj % nin).start()
        lo, hi = _unpack_rows(x)
        lo, hi = lo.astype(BF16), hi.astype(BF16)
        a = (jnp.dot(lo, w1b[:HALF, :], preferred_element_type=F32)
             + jnp.dot(hi, w1b[HALF:, :], preferred_element_type=F32))
        u = (jnp.dot(lo, w3b[:HALF, :], preferred_element_type=F32)
             + jnp.dot(hi, w3b[HALF:, :], preferred_element_type=F32))
        y = _pack_rows(jnp.dot((_silu(a) * u).astype(BF16), w2b[...], preferred_element_type=F32))
        for j, gj in enumerate(tiles):
            @pl.when(gj >= nout)
            def _(gj=gj):
                y_copy(gj - nout, gj % nout).wait()
            ybuf[gj % nout] = y[j * r:(j + 1) * r]
            y_copy(gj, gj % nout).start()

    g0 = first_ref[e]
    pair = EXPERT_TILES_PER_MATMUL

    def pair_body(p, carry):
        run_tiles(g0 + p * pair, pair)
        return carry

    lax.fori_loop(0, n // pair, pair_body, 0)
    for left in range(1, pair):
        pl.when(n % pair == left)(functools.partial(run_tiles, g0 + n - left, left))

    @pl.when(e == ne - 1)
    def _drain_writes():
        for back in range(nout, 0, -1):
            @pl.when(n_used >= back)
            def _(back=back):
                y_copy(n_used - back, (n_used - back) % nout).wait()


def _experts(tile_first, tile_count, n_used, xs, w1, w3, w2, layer):
    n_rows = xs.shape[0]
    r = EXPERT_TILE
    any_spec = pl.BlockSpec(memory_space=pl.ANY)
    return pl.pallas_call(
        functools.partial(_expert_kernel, layer=layer),
        grid_spec=pltpu.PrefetchScalarGridSpec(
            num_scalar_prefetch=3,
            grid=(N_EXPERTS,),
            in_specs=[any_spec, any_spec, any_spec, any_spec],
            out_specs=any_spec,
            scratch_shapes=[pltpu.VMEM((EXPERT_IN_RING, r, HALF), jnp.uint32),
                            pltpu.VMEM((EXPERT_OUT_RING, r, HALF), jnp.uint32),
                            pltpu.SemaphoreType.DMA((EXPERT_IN_RING,)),
                            pltpu.SemaphoreType.DMA((EXPERT_OUT_RING,)),
                            pltpu.VMEM((EXPERT_WEIGHT_RING, D_MODEL, EXPERT_FF), F32),
                            pltpu.VMEM((EXPERT_WEIGHT_RING, D_MODEL, EXPERT_FF), F32),
                            pltpu.VMEM((EXPERT_WEIGHT_RING, EXPERT_FF, D_MODEL), F32),
                            pltpu.SemaphoreType.DMA((EXPERT_WEIGHT_RING,)),
                            pltpu.VMEM((D_MODEL, EXPERT_FF), BF16),
                            pltpu.VMEM((D_MODEL, EXPERT_FF), BF16),
                            pltpu.VMEM((EXPERT_FF, D_MODEL), BF16)],
        ),
        out_shape=jax.ShapeDtypeStruct((n_rows, HALF), jnp.uint32),
        compiler_params=_params(1),
        name="experts",
    )(tile_first, tile_count, n_used, w1, w3, w2, xs)


def _combine_dense_kernel(base_ref, g2_ref, w_ref, yg_ref, o_ref):
    acc_lo = acc_hi = None
    for k in range(TOP_K):
        lo, hi = _unpack_rows(yg_ref[k])
        wk = w_ref[:, k:k + 1]
        acc_lo = wk * lo if acc_lo is None else acc_lo + wk * lo
        acc_hi = wk * hi if acc_hi is None else acc_hi + wk * hi
    o_ref[:, :HALF] = base_ref[:, :HALF] + g2_ref[:, :HALF] * acc_lo
    o_ref[:, HALF:] = base_ref[:, HALF:] + g2_ref[:, HALF:] * acc_hi


def _combine_dense(base, g2, w_tok, yg, batch):
    t = base.shape[0]
    seq = yg.shape[1]
    nt = ROUTE_TILE
    tpb = seq // nt
    rows = lambda i: (batch * tpb + i, 0)
    return pl.pallas_call(
        _combine_dense_kernel,
        grid=(tpb,),
        in_specs=[pl.BlockSpec((nt, D_MODEL), rows),
                  pl.BlockSpec((None, 1, D_MODEL), lambda i: (batch, 0, 0)),
                  pl.BlockSpec((nt, TOP_K), rows),
                  pl.BlockSpec((TOP_K, nt, HALF), lambda i: (0, i, 0))],
        out_specs=pl.BlockSpec((nt, D_MODEL), rows),
        out_shape=jax.ShapeDtypeStruct((t, D_MODEL), F32),
        input_output_aliases={0: 0},
        compiler_params=_params(1),
        name="combine_dense",
    )(base, g2, w_tok, yg)


def _layer(layer, x, c, w_ada, b_ada, norm1_w, norm2_w, w_in, q_norm_w, k_norm_w, rel_bias, w_alpha, b_alpha,
           moba_out_w, gla_out_w, w_out, w_router, e_bias, w1, w3, w2, ws1, ws3, ws2):
    b, s, d = x.shape
    t = b * s
    x2 = x.reshape(t, d)

    mod = _mod(c, w_ada, b_ada)
    sh1, sc1, g1, sh2, sc2, g2 = [mod[:, j * d:(j + 1) * d].reshape(b, 1, d) for j in range(6)]

    w_main = w_in[:, :D_MAIN].astype(BF16)
    w_ga = jnp.zeros((d, LANES), BF16).at[:, :GLA_GATE_RANK].set(w_in[:, D_MAIN:].astype(BF16))
    per_chunk = 256 // MOBA_HEAD_DIM
    qw = jnp.tile(q_norm_w.astype(F32), per_chunk).reshape(1, 256) * (MOBA_HEAD_DIM ** -0.5 * LOG2E)
    kw = jnp.tile(k_norm_w.astype(F32), per_chunk).reshape(1, 256)
    proj, ga = _inproj(x2, sc1, sh1, norm1_w.reshape(1, d), w_main, w_ga, qw, kw, s)
    proj3 = proj.reshape(b, s, D_MAIN)

    near, far = _moba_bias_tables(rel_bias)
    ow = jnp.tile(moba_out_w.astype(F32), 2).reshape(1, LANES)
    o_a = _moba(proj3, near, far, ow)

    wal = jnp.zeros((LANES, GLA_KEY_WIDTH), F32).at[:GLA_GATE_RANK].set(w_alpha)
    o_b = _gla(proj3, ga.reshape(b, s, LANES), wal, b_alpha.reshape(1, GLA_KEY_WIDTH),
               gla_out_w.reshape(1, GLA_DV))

    base, h2, scores_t = _outproj(
        o_a.reshape(t, MOBA_WIDTH), o_b.reshape(t, GLA_WIDTH), x2, g1, sc2, sh2, g2,
        norm2_w.reshape(1, d), w_out.astype(BF16), ws1.astype(BF16), ws3.astype(BF16), ws2.astype(BF16),
        w_router.T.astype(BF16), s)

    eb = jnp.broadcast_to(e_bias.astype(F32)[:, None], (N_EXPERTS, ROUTE_TILE))
    code_t, w_t, counts = _route(scores_t, eb)

    r = EXPERT_TILE
    n_tiles = (t * TOP_K + N_EXPERTS * (r - 1) + r - 1) // r
    n_rows = n_tiles * r
    cnt = counts[:, 0].astype(jnp.int32)
    padded = (cnt + r - 1) // r * r
    pend = jnp.cumsum(padded)
    pstart = pend - padded
    n_used = (pend[-1:] // r).astype(jnp.int32)
    dest_t = _slots(pstart, code_t)

    xs = _sc_scatter_rows(h2, dest_t, n_rows)
    ys = _experts(pstart // r, padded // r, n_used, xs, w1, w3, w2, layer)
    w_tok = w_t.T
    out = base
    for bi in range(b):
        idx = dest_t[:, bi * s:(bi + 1) * s].reshape(TOP_K * s)
        yg = _sc_gather_rows(ys, idx).reshape(TOP_K, s, HALF)
        out = _combine_dense(out, g2, w_tok, yg, bi)
    return out.reshape(b, s, d)


def kernel(x, c, w_ada, b_ada, norm1_w, norm2_w, w_in, q_norm_w, k_norm_w, rel_bias, w_alpha, b_alpha,
           moba_out_w, gla_out_w, w_out, w_router, e_bias, w1, w3, w2, ws1, ws3, ws2):
    for l in range(w_ada.shape[0]):
        x = _layer(l, x, c, w_ada[l], b_ada[l], norm1_w[l], norm2_w[l], w_in[l], q_norm_w[l], k_norm_w[l],
                   rel_bias, w_alpha[l], b_alpha[l], moba_out_w[l], gla_out_w[l], w_out[l], w_router[l],
                   e_bias[l], w1, w3, w2, ws1[l], ws3[l], ws2[l])
    return x
```

```python
import functools
import math

import numpy as np
import jax
import jax.numpy as jnp
from jax import lax
from jax.experimental import pallas as pl
from jax.experimental.pallas import tpu as pltpu
from jax.experimental.pallas import tpu_sc as plsc

D_MODEL = 1024
MOBA_HEADS = 8
MOBA_HEAD_DIM = 64
MOBA_WIDTH = MOBA_HEADS * MOBA_HEAD_DIM
MOBA_BLOCK = 256
MOBA_TOPK = 3
GLA_HEADS = 4
GLA_DK = 64
GLA_DV = 128
GLA_KEY_WIDTH = GLA_HEADS * GLA_DK
GLA_WIDTH = GLA_HEADS * GLA_DV
GLA_GATE_RANK = 16
GLA_GATE_TAU = 16.0
GLA_CHUNK = 64
REL_BUCKETS = 32
REL_MAX_DIST = 128
N_EXPERTS = 256
TOP_K = 8
N_GROUPS = 8
TOPK_GROUPS = 4
GROUP_SIZE = N_EXPERTS // N_GROUPS
EXPERT_FF = 256
SHARED_FF = 256
ROUTED_SCALE = 2.5
NORM_EPS = 1e-6
LOG2E = math.log2(math.e)

D_MAIN = 3 * MOBA_WIDTH + 2 * GLA_KEY_WIDTH + 2 * GLA_WIDTH
LANES = 128
VMEM_LIMIT = 56 * 1024 * 1024

ROW_TILE = 512
ROUTE_TILE = 256
EXPERT_TILE = 128
EXPERT_TILES_PER_MATMUL = 6
EXPERT_IN_RING = 16
EXPERT_OUT_RING = 12
EXPERT_WEIGHT_RING = 4

F32 = jnp.float32
BF16 = jnp.bfloat16
NT_DIMS = (((1,), (1,)), ((), ()))
TN_DIMS = (((0,), (0,)), ((), ()))


def _params(n_axes):
    return pltpu.CompilerParams(dimension_semantics=("arbitrary",) * n_axes,
                                vmem_limit_bytes=VMEM_LIMIT)


def _silu(v):
    return v * jax.nn.sigmoid(v)


def _mod_kernel(c_ref, w_ref, b_ref, o_ref):
    o_ref[...] = jnp.dot(_silu(c_ref[...]), w_ref[...], preferred_element_type=F32) + b_ref[...]


def _mod(c, w, b):
    rows = 8
    cp = jnp.zeros((rows, D_MODEL), F32).at[:c.shape[0]].set(c)
    n = w.shape[1]
    tn = 1024
    out = pl.pallas_call(
        _mod_kernel,
        grid=(n // tn,),
        in_specs=[pl.BlockSpec((rows, D_MODEL), lambda j: (0, 0)),
                  pl.BlockSpec((D_MODEL, tn), lambda j: (0, j)),
                  pl.BlockSpec((1, tn), lambda j: (0, j))],
        out_specs=pl.BlockSpec((rows, tn), lambda j: (0, j)),
        out_shape=jax.ShapeDtypeStruct((rows, n), F32),
        compiler_params=_params(1),
        name="mod",
    )(cp, w, b.reshape(1, n))
    return out[:c.shape[0]]


def _group_rms_inv(a, group):
    lane = lax.broadcasted_iota(jnp.int32, (1, a.shape[1]), 1)
    a2 = a * a
    inv = jnp.zeros_like(a)
    for g in range(a.shape[1] // group):
        m = (lane >= g * group) & (lane < (g + 1) * group)
        ss = jnp.sum(jnp.where(m, a2, 0.0), axis=-1, keepdims=True)
        inv = jnp.where(m, lax.rsqrt(ss * (1.0 / group) + NORM_EPS), inv)
    return inv


def _inproj_kernel(x_ref, sc_ref, sh_ref, nw_ref, w_ref, wga_ref, qw_ref, kw_ref, o_ref, ga_ref):
    x = x_ref[...]
    ms = jnp.mean(x * x, axis=-1, keepdims=True)
    h = x * lax.rsqrt(ms + NORM_EPS) * nw_ref[...]
    h = h * (1.0 + sc_ref[...]) + sh_ref[...]
    hb = h.astype(BF16)
    cw = 256
    for j in range(D_MAIN // cw):
        acc = jnp.dot(hb, w_ref[:, j * cw:(j + 1) * cw], preferred_element_type=F32)
        if j < 2 * MOBA_WIDTH // cw:
            nw = qw_ref if j < MOBA_WIDTH // cw else kw_ref
            acc = acc * _group_rms_inv(acc, MOBA_HEAD_DIM) * nw[...]
        o_ref[:, j * cw:(j + 1) * cw] = acc.astype(BF16)
    ga_ref[...] = jnp.dot(hb, wga_ref[...], preferred_element_type=F32)


def _inproj(x2, sc, sh, nw, w_main, w_ga, qw, kw, seq):
    t = x2.shape[0]
    tpb = seq // ROW_TILE
    vec = lambda: pl.BlockSpec((None, 1, D_MODEL), lambda i: (i // tpb, 0, 0))
    full = lambda a: pl.BlockSpec(a.shape, lambda i: (0,) * a.ndim)
    return pl.pallas_call(
        _inproj_kernel,
        grid=(t // ROW_TILE,),
        in_specs=[pl.BlockSpec((ROW_TILE, D_MODEL), lambda i: (i, 0)), vec(), vec(),
                  full(nw), full(w_main), full(w_ga), full(qw), full(kw)],
        out_specs=[pl.BlockSpec((ROW_TILE, D_MAIN), lambda i: (i, 0)),
                   pl.BlockSpec((ROW_TILE, LANES), lambda i: (i, 0))],
        out_shape=[jax.ShapeDtypeStruct((t, D_MAIN), BF16),
                   jax.ShapeDtypeStruct((t, LANES), F32)],
        compiler_params=_params(1),
        name="inproj",
    )(x2, sc, sh, nw, w_main, w_ga, qw, kw)


def _t5_bucket_np(rel):
    max_exact = REL_BUCKETS // 2
    relf = np.maximum(rel, 1).astype(np.float64)
    large = max_exact + (np.log(relf / max_exact) / math.log(REL_MAX_DIST / max_exact)
                         * (REL_BUCKETS - max_exact)).astype(np.int32)
    large = np.minimum(large, REL_BUCKETS - 1)
    return np.where(rel < max_exact, rel, large)


def _bias_kernel(rb_ref, idx_ref, o_ref):
    h = pl.program_id(0)
    idx = idx_ref[...]
    tab = jnp.full(idx.shape, -jnp.inf, F32)
    for bk in range(REL_BUCKETS):
        tab = jnp.where(idx == bk, rb_ref[bk * MOBA_HEADS + h], tab)
    o_ref[...] = tab


def _moba_bias_tables(rel_bias):
    j = np.arange(MOBA_BLOCK)[:, None]
    i = np.arange(MOBA_BLOCK)[None, :]
    own_idx = np.where(j <= i, _t5_bucket_np(np.maximum(i - j, 0)), -1)
    prev_idx = _t5_bucket_np(MOBA_BLOCK + i - j)
    idx = jnp.asarray(np.concatenate([prev_idx, own_idx], axis=0).astype(np.int32))
    assert int(_t5_bucket_np(np.array([MOBA_BLOCK + 1]))[0]) == REL_BUCKETS - 1
    rb = rel_bias.astype(F32) * LOG2E
    near = pl.pallas_call(
        _bias_kernel,
        grid=(MOBA_HEADS,),
        in_specs=[pl.BlockSpec(memory_space=pltpu.SMEM),
                  pl.BlockSpec(idx.shape, lambda h: (0, 0))],
        out_specs=pl.BlockSpec((None,) + idx.shape, lambda h: (h, 0, 0)),
        out_shape=jax.ShapeDtypeStruct((MOBA_HEADS,) + idx.shape, F32),
        compiler_params=_params(1),
        name="bias",
    )(rb.reshape(-1), idx)
    return near, rb[REL_BUCKETS - 1]


FAR_GROUP = 4


MOBA_INTERLEAVE = 4


def _moba_kernel(*refs):
    hp = pl.program_id(1)
    nsets = refs[2].shape[0] // (MOBA_INTERLEAVE * MOBA_BLOCK)
    _moba_body(None, hp, *refs, prepare=True)

    def block_set(j, carry):
        for jj in range(nsets):
            blocks = list(range(jj * MOBA_INTERLEAVE, (jj + 1) * MOBA_INTERLEAVE))
            pl.when(j == jj)(functools.partial(_moba_body, blocks, hp, *refs, prepare=False))
        return carry

    lax.fori_loop(0, nsets, block_set, 0)


def _moba_body(blocks, hp, far_ref, q_ref, k_ref, v_ref, near_ref, ow_ref, o_ref,
               vt_ref, vtg_ref, acc_ref, m_ref, sel_ref, s_ref, mx_ref, *, prepare):
    nblk = k_ref.shape[0] // MOBA_BLOCK
    ngrp = nblk // FAR_GROUP
    hd = MOBA_HEAD_DIM
    bs = MOBA_BLOCK
    lane = lax.broadcasted_iota(jnp.int32, (bs, LANES), 1)

    def split_heads(qb):
        zero = jnp.zeros_like(qb)
        return jnp.where(lane < hd, qb, zero), jnp.where(lane < hd, zero, qb)

    def _prepare():
        row = lax.broadcasted_iota(jnp.int32, (LANES, bs), 0)
        kmeans = []
        for n in range(nblk):
            kb = k_ref[n * bs:(n + 1) * bs, :].astype(F32)
            kmeans.append(jnp.mean(kb, axis=0, keepdims=True))
            vt = v_ref[n * bs:(n + 1) * bs, :].astype(F32).T
            vt0 = jnp.where(row < hd, vt, 1.0).astype(BF16)
            vt1 = jnp.where(row < hd, 1.0, vt).astype(BF16)
            vt_ref[0, n] = vt0
            vt_ref[1, n] = vt1
            gcols = slice((n % FAR_GROUP) * bs, (n % FAR_GROUP + 1) * bs)
            vtg_ref[0, n // FAR_GROUP, :, gcols] = vt0
            vtg_ref[1, n // FAR_GROUP, :, gcols] = vt1
        kmean = jnp.concatenate(kmeans, axis=0)
        km_hi = kmean.astype(BF16)
        km_lo = (kmean - km_hi.astype(F32)).astype(BF16)
        blk = lax.broadcasted_iota(jnp.int32, (nblk, bs), 0)
        for ib in range(nblk):
            qparts = split_heads(q_ref[ib * bs:(ib + 1) * bs, :])
            for h in range(2):
                gt = (lax.dot_general(km_hi, qparts[h], NT_DIMS, preferred_element_type=F32)
                      + lax.dot_general(km_lo, qparts[h], NT_DIMS, preferred_element_type=F32))
                gt = jnp.where(blk < ib, gt, -jnp.inf)
                cnt = jnp.zeros(gt.shape, jnp.int32)
                for m in range(ib):
                    gm = gt[m:m + 1, :]
                    cnt = cnt + jnp.where((gm > gt) | ((gm == gt) & (blk > m)), 1, 0)
                keep = (blk < ib) & (cnt < MOBA_TOPK)
                sel_ref[0, h, ib] = jnp.where(keep, 1.0, 0.0)
                sel_ref[1, h, ib] = jnp.where(keep & (blk < ib - 1), 1.0, 0.0)

    if prepare:
        _prepare()
        return

    gk = FAR_GROUP * bs

    def rows_of(ib):
        if isinstance(ib, int):
            return slice(ib * bs, (ib + 1) * bs)
        return pl.ds(pl.multiple_of(ib * bs, bs), bs)

    def finish(ib, slot):
        a0 = acc_ref[slot, 0]
        a1 = acc_ref[slot, 1]
        row = lax.broadcasted_iota(jnp.int32, a0.shape, 0)
        ot = jnp.where(row < hd, a0 / a0[hd:hd + 1, :], a1 / a1[0:1, :])
        o2 = ot * ot
        ss0 = jnp.sum(jnp.where(row < hd, o2, 0.0), axis=0, keepdims=True)
        ss1 = jnp.sum(jnp.where(row < hd, 0.0, o2), axis=0, keepdims=True)
        inv = jnp.where(row < hd, lax.rsqrt(ss0 * (1.0 / hd) + NORM_EPS), lax.rsqrt(ss1 * (1.0 / hd) + NORM_EPS))
        o_ref[rows_of(ib), :] = ((ot * inv).T * ow_ref[...]).astype(o_ref.dtype)

    def far_scores(g, slot, qh, nb):
        kb = k_ref[g * gk:g * gk + nb * bs, :]
        for h in range(2):
            s = lax.dot_general(kb, qh[h], NT_DIMS, preferred_element_type=F32)
            s_ref[slot, g % 2, h, 0:nb * bs, :] = s
            for j in range(nb):
                mx_ref[slot, g % 2, h, j] = jnp.max(s[j * bs:(j + 1) * bs], axis=0, keepdims=True)

    def near_scores(ib, qh):
        kbs = (k_ref[rows_of(ib - 1), :], k_ref[rows_of(ib), :])
        return [[lax.dot_general(kbs[w], qh[h], NT_DIMS, preferred_element_type=F32)
                 + near_ref[h, w * bs:(w + 1) * bs, :] for w in range(2)] for h in range(2)]

    def near_values(ib, slot, ss):
        ps, ms = [], []
        for h in range(2):
            s_prev, s_own = ss[h]
            keep = sel_ref[0, h, ib, pl.ds(ib - 1, 1), :] > 0.5
            mx = jnp.where(keep, jnp.max(s_prev, axis=0, keepdims=True), -jnp.inf)
            m_new = jnp.maximum(jnp.max(s_own, axis=0, keepdims=True), mx)
            ps.append((jnp.exp2(s_prev - jnp.where(keep, m_new, jnp.inf)).astype(BF16),
                       jnp.exp2(s_own - m_new).astype(BF16)))
            ms.append(m_new)
        for h in range(2):
            acc_ref[slot, h] = (jnp.dot(vt_ref[h, ib - 1], ps[h][0], preferred_element_type=F32)
                                + jnp.dot(vt_ref[h, ib], ps[h][1], preferred_element_type=F32))
            m_ref[slot, h] = ms[h]

    def far_group(g, ib, slot, nb):
        for h in range(2):
            fb = far_ref[2 * hp + h]
            m_old = m_ref[slot, h]
            m_new = m_old
            keeps = []
            for j in range(nb):
                keep = sel_ref[1, h, ib, pl.ds(g * FAR_GROUP + j, 1), :] > 0.5
                m_new = jnp.maximum(m_new, jnp.where(keep, mx_ref[slot, g % 2, h, j] + fb, -jnp.inf))
                keeps.append(keep)
            p = jnp.concatenate(
                [jnp.exp2(s_ref[slot, g % 2, h, j * bs:(j + 1) * bs, :]
                          - jnp.where(keeps[j], m_new - fb, jnp.inf)).astype(BF16)
                 for j in range(nb)], axis=0)
            pv = jnp.dot(vtg_ref[h, g, :, 0:nb * bs], p, preferred_element_type=F32)
            acc_ref[slot, h] = acc_ref[slot, h] * jnp.exp2(m_old - m_new) + pv
            m_ref[slot, h] = m_new

    def step_body(blocks):
        slots = list(enumerate(blocks))
        nbs = [[min(FAR_GROUP, ib - 1 - g * FAR_GROUP) for g in range((ib - 2) // FAR_GROUP + 1)] if ib >= 2 else []
               for ib in blocks]
        qhs = [split_heads(q_ref[rows_of(ib), :]) for ib in blocks]
        near = [near_scores(ib, qhs[slot]) for slot, ib in slots]
        for slot, ib in slots:
            if nbs[slot]:
                far_scores(0, slot, qhs[slot], nbs[slot][0])
        for slot, ib in slots:
            near_values(ib, slot, near[slot])
        for g in range(max(len(n) for n in nbs)):
            for slot, ib in slots:
                if g + 1 < len(nbs[slot]):
                    far_scores(g + 1, slot, qhs[slot], nbs[slot][g + 1])
            for slot, ib in slots:
                if g < len(nbs[slot]):
                    far_group(g, ib, slot, nbs[slot][g])
        for slot, ib in slots:
            finish(ib, slot)

    if blocks[0] == 0:
        qh = split_heads(q_ref[0:bs, :])
        kb = k_ref[0:bs, :]
        for h in range(2):
            s = lax.dot_general(kb, qh[h], NT_DIMS, preferred_element_type=F32) + near_ref[h, bs:2 * bs, :]
            p = jnp.exp2(s - jnp.max(s, axis=0, keepdims=True)).astype(BF16)
            acc_ref[MOBA_INTERLEAVE - 1, h] = jnp.dot(vt_ref[h, 0], p, preferred_element_type=F32)
        finish(0, MOBA_INTERLEAVE - 1)
        blocks = blocks[1:]
    step_body(blocks)


def _moba(proj3, near, far, ow):
    b, s, _ = proj3.shape
    nblk = s // MOBA_BLOCK
    assert nblk % FAR_GROUP == 0
    npair = MOBA_HEADS // 2
    kcol = MOBA_WIDTH // LANES
    return pl.pallas_call(
        _moba_kernel,
        grid=(b, npair),
        in_specs=[pl.BlockSpec(memory_space=pltpu.SMEM),
                  pl.BlockSpec((None, s, LANES), lambda bb, hp: (bb, 0, hp)),
                  pl.BlockSpec((None, s, LANES), lambda bb, hp: (bb, 0, kcol + hp)),
                  pl.BlockSpec((None, s, LANES), lambda bb, hp: (bb, 0, 2 * kcol + hp)),
                  pl.BlockSpec((2, 2 * MOBA_BLOCK, MOBA_BLOCK), lambda bb, hp: (hp, 0, 0)),
                  pl.BlockSpec((1, LANES), lambda bb, hp: (0, 0))],
        out_specs=pl.BlockSpec((None, s, LANES), lambda bb, hp: (bb, 0, hp)),
        out_shape=jax.ShapeDtypeStruct((b, s, MOBA_WIDTH), BF16),
        scratch_shapes=[pltpu.VMEM((2, nblk, LANES, MOBA_BLOCK), BF16),
                        pltpu.VMEM((2, nblk // FAR_GROUP, LANES, FAR_GROUP * MOBA_BLOCK), BF16),
                        pltpu.VMEM((MOBA_INTERLEAVE, 2, LANES, MOBA_BLOCK), F32),
                        pltpu.VMEM((MOBA_INTERLEAVE, 2, 1, MOBA_BLOCK), F32),
                        pltpu.VMEM((2, 2, nblk, nblk, MOBA_BLOCK), F32),
                        pltpu.VMEM((MOBA_INTERLEAVE, 2, 2, FAR_GROUP * MOBA_BLOCK, MOBA_BLOCK), F32),
                        pltpu.VMEM((MOBA_INTERLEAVE, 2, 2, FAR_GROUP, 1, MOBA_BLOCK), F32)],
        compiler_params=_params(2),
        name="moba",
    )(far, proj3, proj3, proj3, near, ow)


def _split3(v):
    hi = v.astype(BF16)
    r1 = v - hi.astype(F32)
    mid = r1.astype(BF16)
    lo = (r1 - mid.astype(F32)).astype(BF16)
    return hi, mid, lo


GLA_UNROLL = 16


def _gla_kernel(q_ref, k_ref, v_ref, g_ref, ga_ref, wal_ref, bal_ref, gw_ref, o_ref, b_ref, st_ref):
    seq = q_ref.shape[0]
    c = GLA_CHUNK
    pc = 256

    rr = lax.broadcasted_iota(jnp.int32, (pc, pc), 0)
    cc = lax.broadcasted_iota(jnp.int32, (pc, pc), 1)
    tri = jnp.where((rr >= cc) & (rr // c == cc // c), 1.0, 0.0).astype(BF16)

    def decay_body(j, carry):
        rows = [pl.ds(pl.multiple_of((j * GLA_UNROLL + u) * pc, pc), pc) for u in range(GLA_UNROLL)]
        xg = [jnp.dot(ga_ref[r, :], wal_ref[...], preferred_element_type=F32) + bal_ref[...] for r in rows]
        parts = [_split3((jnp.minimum(x, 0.0) - jnp.log(1.0 + jnp.exp(-jnp.abs(x)))) * (1.0 / GLA_GATE_TAU))
                 for x in xg]
        sums = [[jnp.dot(tri, term, preferred_element_type=F32) for term in p] for p in parts]
        for r, (hi, mid, lo) in zip(rows, sums):
            b_ref[r, :] = hi + mid + lo
        return carry

    lax.fori_loop(0, seq // (pc * GLA_UNROLL), decay_body, 0)

    st_ref[...] = jnp.zeros_like(st_ref)
    lane = lax.broadcasted_iota(jnp.int32, (c, LANES), 1)
    head_mask = (lane < GLA_DK, lane >= GLA_DK)
    causal = lax.broadcasted_iota(jnp.int32, (c, c), 0) >= lax.broadcasted_iota(jnp.int32, (c, c), 1)

    units = [(u, h) for u in range(GLA_UNROLL) for h in range(2)]

    def chunk_body(ci, carry):
        rows = [pl.ds(pl.multiple_of((ci * GLA_UNROLL + u) * c, c), c) for u in range(GLA_UNROLL)]
        qt, kt, qs, ke, e_last = [], [], [], [], []
        for u in range(GLA_UNROLL):
            b = b_ref[rows[u], :]
            ref_row = b[c // 2 - 1:c // 2, :]
            last = b[c - 1:c, :]
            q = q_ref[rows[u], :].astype(F32) * (GLA_DK ** -0.5)
            k = k_ref[rows[u], :].astype(F32)
            qt.append(q * jnp.exp(b - ref_row))
            kt.append((k * jnp.exp(ref_row - b)).astype(BF16))
            qs.append(q * jnp.exp(b))
            ke.append((k * jnp.exp(last - b)).astype(BF16))
            e_last.append(jnp.exp(last))
        vs = {(u, h): v_ref[rows[u], h * GLA_DV:(h + 1) * GLA_DV] for u, h in units}
        a = {(u, h): lax.dot_general(jnp.where(head_mask[h], qt[u], 0.0).astype(BF16), kt[u], NT_DIMS,
                                     preferred_element_type=F32) for u, h in units}
        inc = {(u, h): lax.dot_general(vs[u, h], ke[u], TN_DIMS, preferred_element_type=F32) for u, h in units}
        o = {(u, h): jnp.dot(jnp.where(causal, a[u, h], 0.0).astype(BF16), vs[u, h], preferred_element_type=F32)
             for u, h in units}
        states = {}
        for h in range(2):
            st = st_ref[h]
            for u in range(GLA_UNROLL):
                states[u, h] = st
                st = st * e_last[u] + inc[u, h]
            st_ref[h] = st
        for u, h in units:
            cols = slice(h * GLA_DV, (h + 1) * GLA_DV)
            ou = o[u, h] + lax.dot_general(jnp.where(head_mask[h], qs[u], 0.0).astype(BF16),
                                           states[u, h].astype(BF16), NT_DIMS, preferred_element_type=F32)
            ms = jnp.mean(ou * ou, axis=-1, keepdims=True)
            on = ou * lax.rsqrt(ms + NORM_EPS) * gw_ref[...]
            g = g_ref[rows[u], cols].astype(F32)
            o_ref[rows[u], cols] = (on * _silu(g)).astype(o_ref.dtype)
        return carry

    lax.fori_loop(0, seq // (c * GLA_UNROLL), chunk_body, 0)


def _gla(proj3, ga3, wal, bal, gw):
    b, s, _ = proj3.shape
    npair = GLA_HEADS // 2
    qcol = 3 * MOBA_WIDTH // LANES
    kcol = qcol + GLA_KEY_WIDTH // LANES
    vcol = (3 * MOBA_WIDTH + 2 * GLA_KEY_WIDTH) // (2 * GLA_DV)
    gcol = vcol + npair
    return pl.pallas_call(
        _gla_kernel,
        grid=(b, npair),
        in_specs=[pl.BlockSpec((None, s, LANES), lambda bb, hp: (bb, 0, qcol + hp)),
                  pl.BlockSpec((None, s, LANES), lambda bb, hp: (bb, 0, kcol + hp)),
                  pl.BlockSpec((None, s, 2 * GLA_DV), lambda bb, hp: (bb, 0, vcol + hp)),
                  pl.BlockSpec((None, s, 2 * GLA_DV), lambda bb, hp: (bb, 0, gcol + hp)),
                  pl.BlockSpec((None, s, LANES), lambda bb, hp: (bb, 0, 0)),
                  pl.BlockSpec((LANES, LANES), lambda bb, hp: (0, hp)),
                  pl.BlockSpec((1, LANES), lambda bb, hp: (0, hp)),
                  pl.BlockSpec((1, GLA_DV), lambda bb, hp: (0, 0))],
        out_specs=pl.BlockSpec((None, s, 2 * GLA_DV), lambda bb, hp: (bb, 0, hp)),
        out_shape=jax.ShapeDtypeStruct((b, s, GLA_WIDTH), BF16),
        scratch_shapes=[pltpu.VMEM((s, LANES), F32),
                        pltpu.VMEM((2, GLA_DV, LANES), F32)],
        compiler_params=_params(2),
        name="gla",
    )(proj3, proj3, proj3, proj3, ga3, wal, bal, gw)


HALF = D_MODEL // 2


def _pack_rows(v):
    return pltpu.pack_elementwise([v[:, :HALF], v[:, HALF:]], packed_dtype=BF16)


def _unpack_rows(w):
    return (pltpu.unpack_elementwise(w, index=0, packed_dtype=BF16, unpacked_dtype=F32),
            pltpu.unpack_elementwise(w, index=1, packed_dtype=BF16, unpacked_dtype=F32))


def _outproj_kernel(oa_ref, ob_ref, x_ref, g1_ref, sc_ref, sh_ref, g2_ref, nw_ref, wo_ref,
                    ws1_ref, ws3_ref, ws2_ref, wrt_ref, base_ref, h_ref, st_ref):
    mix = (jnp.dot(oa_ref[...], wo_ref[:MOBA_WIDTH, :], preferred_element_type=F32)
           + jnp.dot(ob_ref[...], wo_ref[MOBA_WIDTH:, :], preferred_element_type=F32))
    x1 = x_ref[...] + g1_ref[...] * mix
    ms = jnp.mean(x1 * x1, axis=-1, keepdims=True)
    h = x1 * lax.rsqrt(ms + NORM_EPS) * nw_ref[...]
    h = h * (1.0 + sc_ref[...]) + sh_ref[...]
    h_ref[...] = _pack_rows(h)
    hb = h.astype(BF16)
    a = jnp.dot(hb, ws1_ref[...], preferred_element_type=F32)
    u = jnp.dot(hb, ws3_ref[...], preferred_element_type=F32)
    shared = jnp.dot((_silu(a) * u).astype(BF16), ws2_ref[...], preferred_element_type=F32)
    base_ref[...] = x1 + g2_ref[...] * shared
    logits_t = lax.dot_general(wrt_ref[...], hb, NT_DIMS, preferred_element_type=F32)
    st_ref[...] = jax.nn.sigmoid(logits_t)


def _outproj(oa, ob, x2, g1, sc, sh, g2, nw, wo, ws1, ws3, ws2, wrt, seq):
    t = x2.shape[0]
    tpb = seq // ROW_TILE
    vec = lambda: pl.BlockSpec((None, 1, D_MODEL), lambda i: (i // tpb, 0, 0))
    full = lambda a: pl.BlockSpec(a.shape, lambda i: (0,) * a.ndim)
    rows = lambda w: pl.BlockSpec((ROW_TILE, w), lambda i: (i, 0))
    return pl.pallas_call(
        _outproj_kernel,
        grid=(t // ROW_TILE,),
        in_specs=[rows(MOBA_WIDTH), rows(GLA_WIDTH), rows(D_MODEL), vec(), vec(), vec(), vec(),
                  full(nw), full(wo), full(ws1), full(ws3), full(ws2), full(wrt)],
        out_specs=[rows(D_MODEL), rows(HALF), pl.BlockSpec((N_EXPERTS, ROW_TILE), lambda i: (0, i))],
        out_shape=[jax.ShapeDtypeStruct((t, D_MODEL), F32),
                   jax.ShapeDtypeStruct((t, HALF), jnp.uint32),
                   jax.ShapeDtypeStruct((N_EXPERTS, t), F32)],
        compiler_params=_params(1),
        name="outproj",
    )(oa, ob, x2, g1, sc, sh, g2, nw, wo, ws1, ws3, ws2, wrt)


SLOT_CODE_SHIFT = 16
SLOT_CODE_BASE = 1 << SLOT_CODE_SHIFT


def _route_kernel(s_ref, eb_ref, code_ref, w_ref, cnt_ref, carry_ref):
    i = pl.program_id(0)
    ne, nt = s_ref.shape

    @pl.when(i == 0)
    def _init():
        carry_ref[...] = jnp.zeros_like(carry_ref)

    s = s_ref[...]
    choice = s + eb_ref[...]
    gio = lax.broadcasted_iota(jnp.int32, (GROUP_SIZE, nt), 0)
    gscore = []
    for g in range(N_GROUPS):
        cg = choice[g * GROUP_SIZE:(g + 1) * GROUP_SIZE, :]
        top1 = jnp.max(cg, axis=0, keepdims=True)
        first = jnp.min(jnp.where(cg == top1, gio, GROUP_SIZE), axis=0, keepdims=True)
        top2 = jnp.max(jnp.where(gio == first, -jnp.inf, cg), axis=0, keepdims=True)
        gscore.append(top1 + top2)
    gs = jnp.concatenate(gscore, axis=0)
    gidx = lax.broadcasted_iota(jnp.int32, gs.shape, 0)
    beaten = jnp.zeros(gs.shape, jnp.int32)
    for m in range(N_GROUPS):
        gm = gs[m:m + 1, :]
        beaten = beaten + jnp.where((gm > gs) | ((gm == gs) & (gidx > m)), 1, 0)
    gkeep = beaten < TOPK_GROUPS
    masked = jnp.concatenate(
        [jnp.where(gkeep[g:g + 1, :], choice[g * GROUP_SIZE:(g + 1) * GROUP_SIZE, :], -jnp.inf)
         for g in range(N_GROUPS)], axis=0)

    eio = lax.broadcasted_iota(jnp.int32, (ne, nt), 0)
    picked = jnp.zeros((ne, nt), F32)
    idx_rows, w_rows, hits = [], [], []
    for _ in range(TOP_K):
        mx = jnp.max(masked, axis=0, keepdims=True)
        idx = jnp.min(jnp.where(masked == mx, eio, ne), axis=0, keepdims=True)
        hit = eio == idx
        w_rows.append(jnp.sum(jnp.where(hit, s, 0.0), axis=0, keepdims=True))
        idx_rows.append(idx)
        hits.append(hit)
        masked = jnp.where(hit, -jnp.inf, masked)
        picked = jnp.where(hit, 1.0, picked)
    wk = jnp.concatenate(w_rows, axis=0)
    w_ref[...] = wk / jnp.sum(wk, axis=0, keepdims=True) * ROUTED_SCALE

    tr = lax.broadcasted_iota(jnp.int32, (nt, nt), 0)
    tc = lax.broadcasted_iota(jnp.int32, (nt, nt), 1)
    before = jnp.where(tr < tc, 1.0, 0.0).astype(BF16)
    pb = picked.astype(BF16)
    pos = carry_ref[...] + jnp.dot(pb, before, preferred_element_type=F32)
    rank = jnp.concatenate(
        [jnp.sum(jnp.where(hit, pos, 0.0), axis=0, keepdims=True) for hit in hits], axis=0).astype(jnp.int32)
    code_ref[...] = jnp.concatenate(idx_rows, axis=0) * SLOT_CODE_BASE + rank
    total = carry_ref[...] + jnp.dot(pb, jnp.ones((nt, nt), BF16), preferred_element_type=F32)
    carry_ref[...] = total
    cnt_ref[...] = total


def _route(scores_t, eb):
    ne, t = scores_t.shape
    assert t <= SLOT_CODE_BASE
    nt = ROUTE_TILE
    tok = lambda dt: jax.ShapeDtypeStruct((TOP_K, t), dt)
    return pl.pallas_call(
        _route_kernel,
        grid=(t // nt,),
        in_specs=[pl.BlockSpec((ne, nt), lambda i: (0, i)),
                  pl.BlockSpec((ne, nt), lambda i: (0, 0))],
        out_specs=[pl.BlockSpec((TOP_K, nt), lambda i: (0, i)),
                   pl.BlockSpec((TOP_K, nt), lambda i: (0, i)),
                   pl.BlockSpec((ne, nt), lambda i: (0, 0))],
        out_shape=[tok(jnp.int32), tok(F32), jax.ShapeDtypeStruct((ne, nt), F32)],
        scratch_shapes=[pltpu.VMEM((ne, nt), F32)],
        compiler_params=_params(1),
        name="route",
    )(scores_t, eb)


SLOT_TILE = 2048


def _slots_kernel(pstart_ref, code_ref, o_ref):
    code = code_ref[...]
    expert = lax.shift_right_logical(code, SLOT_CODE_SHIFT)

    def body(e, acc):
        return jnp.where(expert == e, pstart_ref[e], acc)

    start = lax.fori_loop(0, N_EXPERTS, body, jnp.zeros_like(code), unroll=8)
    o_ref[...] = start + (code & (SLOT_CODE_BASE - 1))


def _slots(pstart, code_t):
    k, t = code_t.shape
    return pl.pallas_call(
        _slots_kernel,
        grid_spec=pltpu.PrefetchScalarGridSpec(
            num_scalar_prefetch=1,
            grid=(t // SLOT_TILE,),
            in_specs=[pl.BlockSpec((k, SLOT_TILE), lambda i, p: (0, i))],
            out_specs=pl.BlockSpec((k, SLOT_TILE), lambda i, p: (0, i)),
        ),
        out_shape=jax.ShapeDtypeStruct((k, t), jnp.int32),
        compiler_params=_params(1),
        name="slots",
    )(pstart, code_t)


SC_WINDOW = 128


def _sc_gather_rows(table, idx_flat):
    info = plsc.get_sparse_core_info()
    nw = info.num_cores * info.num_subcores
    n = idx_flat.shape[0]
    width = table.shape[1]
    per_worker = n // nw
    assert per_worker * nw == n and per_worker % SC_WINDOW == 0
    mesh = plsc.VectorSubcoreMesh(core_axis_name="c", subcore_axis_name="s")

    def body(table_hbm, idx_hbm, out_hbm, idx_v, rows_v, sem):
        wid = lax.axis_index("s") * info.num_cores + lax.axis_index("c")
        base = wid * per_worker

        @pl.loop(0, per_worker // SC_WINDOW)
        def _(w):
            off = pl.multiple_of(base + w * SC_WINDOW, SC_WINDOW)
            pltpu.sync_copy(idx_hbm.at[pl.ds(off, SC_WINDOW)], idx_v)
            pltpu.async_copy(table_hbm.at[idx_v], rows_v, sem).wait()
            pltpu.sync_copy(rows_v, out_hbm.at[pl.ds(off, SC_WINDOW)])

    return pl.kernel(
        body,
        out_type=jax.ShapeDtypeStruct((n, width), table.dtype),
        mesh=mesh,
        scratch_types=[pltpu.VMEM((SC_WINDOW,), jnp.int32),
                       pltpu.VMEM((SC_WINDOW, width), table.dtype),
                       pltpu.SemaphoreType.DMA],
        name="sc_gather",
    )(table, idx_flat)


def _sc_scatter_rows(rows, idx_kt, n_out):
    info = plsc.get_sparse_core_info()
    nw = info.num_cores * info.num_subcores
    t, width = rows.shape
    nk = idx_kt.shape[0]
    per_worker = t // nw
    assert per_worker * nw == t and per_worker % SC_WINDOW == 0
    mesh = plsc.VectorSubcoreMesh(core_axis_name="c", subcore_axis_name="s")

    def body(rows_hbm, idx_hbm, out_hbm, idx_v, rows_v, sem):
        wid = lax.axis_index("s") * info.num_cores + lax.axis_index("c")
        base = wid * per_worker

        @pl.loop(0, per_worker // SC_WINDOW)
        def _(w):
            off = pl.multiple_of(base + w * SC_WINDOW, SC_WINDOW)
            pltpu.sync_copy(rows_hbm.at[pl.ds(off, SC_WINDOW)], rows_v)
            pltpu.sync_copy(idx_hbm.at[:, pl.ds(off, SC_WINDOW)], idx_v)
            copies = [pltpu.async_copy(rows_v, out_hbm.at[idx_v.at[k]], sem) for k in range(nk)]
            for cp in copies:
                cp.wait()

    return pl.kernel(
        body,
        out_type=jax.ShapeDtypeStruct((n_out, width), rows.dtype),
        mesh=mesh,
        scratch_types=[pltpu.VMEM((nk, SC_WINDOW), jnp.int32),
                       pltpu.VMEM((SC_WINDOW, width), rows.dtype),
                       pltpu.SemaphoreType.DMA],
        name="sc_scatter",
    )(rows, idx_kt)


def _expert_kernel(first_ref, count_ref, used_ref, w1_ref, w3_ref, w2_ref, xs_ref, ys_ref,
                   xbuf, ybuf, sem_in, sem_out, w1f, w3f, w2f, sem_w, w1b, w3b, w2b, *, layer):
    e = pl.program_id(0)
    ne = pl.num_programs(0)
    r = xbuf.shape[1]
    n_used = used_ref[0]

    def x_copy(g, slot):
        return pltpu.make_async_copy(xs_ref.at[pl.ds(pl.multiple_of(g * r, r), r), :], xbuf.at[slot], sem_in.at[slot])

    def y_copy(g, slot):
        return pltpu.make_async_copy(ybuf.at[slot], ys_ref.at[pl.ds(pl.multiple_of(g * r, r), r), :], sem_out.at[slot])

    def w_copies(ex, slot):
        return [pltpu.make_async_copy(src.at[layer, ex], dst.at[slot], sem_w.at[slot])
                for src, dst in ((w1_ref, w1f), (w3_ref, w3f), (w2_ref, w2f))]

    nin = xbuf.shape[0]
    nout = ybuf.shape[0]

    nw = w1f.shape[0]

    @pl.when(e == 0)
    def _first_reads():
        for g in range(nin):
            @pl.when(g < n_used)
            def _(g=g):
                x_copy(g, g).start()
        for ex in range(nw - 1):
            for cp in w_copies(ex, ex):
                cp.start(priority=1)

    @pl.when(e + nw - 1 < ne)
    def _weights_ahead():
        for cp in w_copies(e + nw - 1, (e + nw - 1) % nw):
            cp.start(priority=1)

    wslot = e % nw
    for cp in w_copies(e, wslot):
        cp.wait()

    n = count_ref[e]

    @pl.when(n > 0)
    def _cast_weights():
        w1b[...] = w1f[wslot].astype(BF16)
        w3b[...] = w3f[wslot].astype(BF16)
        w2b[...] = w2f[wslot].astype(BF16)

    def run_tiles(g, count):
        tiles = [g + j for j in range(count)]
        for gj in tiles:
            x_copy(gj, gj % nin).wait()
        x = jnp.concatenate([xbuf[gj % nin] for gj in tiles], axis=0)
        for gj in tiles:
            @pl.when(gj + nin < n_used)
            def _(gj=gj):
                x_copy(gj + nin, gj % nin).start()
        lo, hi = _unpack_rows(x)
        lo, hi = lo.astype(BF16), hi.astype(BF16)
        a = (jnp.dot(lo, w1b[:HALF, :], preferred_element_type=F32)
             + jnp.dot(hi, w1b[HALF:, :], preferred_element_type=F32))
        u = (jnp.dot(lo, w3b[:HALF, :], preferred_element_type=F32)
             + jnp.dot(hi, w3b[HALF:, :], preferred_element_type=F32))
        y = _pack_rows(jnp.dot((_silu(a) * u).astype(BF16), w2b[...], preferred_element_type=F32))
        for j, gj in enumerate(tiles):
            @pl.when(gj >= nout)
            def _(gj=gj):
                y_copy(gj - nout, gj % nout).wait()
            ybuf[gj % nout] = y[j * r:(j + 1) * r]
            y_copy(gj, gj % nout).start()

    g0 = first_ref[e]
    pair = EXPERT_TILES_PER_MATMUL

    def pair_body(p, carry):
        run_tiles(g0 + p * pair, pair)
        return carry

    lax.fori_loop(0, n // pair, pair_body, 0)
    for left in range(1, pair):
        pl.when(n % pair == left)(functools.partial(run_tiles, g0 + n - left, left))

    @pl.when(e == ne - 1)
    def _drain_writes():
        for back in range(nout, 0, -1):
            @pl.when(n_used >= back)
            def _(back=back):
                y_copy(n_used - back, (n_used - back) % nout).wait()


def _experts(tile_first, tile_count, n_used, xs, w1, w3, w2, layer):
    n_rows = xs.shape[0]
    r = EXPERT_TILE
    any_spec = pl.BlockSpec(memory_space=pl.ANY)
    return pl.pallas_call(
        functools.partial(_expert_kernel, layer=layer),
        grid_spec=pltpu.PrefetchScalarGridSpec(
            num_scalar_prefetch=3,
            grid=(N_EXPERTS,),
            in_specs=[any_spec, any_spec, any_spec, any_spec],
            out_specs=any_spec,
            scratch_shapes=[pltpu.VMEM((EXPERT_IN_RING, r, HALF), jnp.uint32),
                            pltpu.VMEM((EXPERT_OUT_RING, r, HALF), jnp.uint32),
                            pltpu.SemaphoreType.DMA((EXPERT_IN_RING,)),
                            pltpu.SemaphoreType.DMA((EXPERT_OUT_RING,)),
                            pltpu.VMEM((EXPERT_WEIGHT_RING, D_MODEL, EXPERT_FF), F32),
                            pltpu.VMEM((EXPERT_WEIGHT_RING, D_MODEL, EXPERT_FF), F32),
                            pltpu.VMEM((EXPERT_WEIGHT_RING, EXPERT_FF, D_MODEL), F32),
                            pltpu.SemaphoreType.DMA((EXPERT_WEIGHT_RING,)),
                            pltpu.VMEM((D_MODEL, EXPERT_FF), BF16),
                            pltpu.VMEM((D_MODEL, EXPERT_FF), BF16),
                            pltpu.VMEM((EXPERT_FF, D_MODEL), BF16)],
        ),
        out_shape=jax.ShapeDtypeStruct((n_rows, HALF), jnp.uint32),
        compiler_params=_params(1),
        name="experts",
    )(tile_first, tile_count, n_used, w1, w3, w2, xs)


def _combine_dense_kernel(base_ref, g2_ref, w_ref, yg_ref, o_ref):
    acc_lo = acc_hi = None
    for k in range(TOP_K):
        lo, hi = _unpack_rows(yg_ref[k])
        wk = w_ref[:, k:k + 1]
        acc_lo = wk * lo if acc_lo is None else acc_lo + wk * lo
        acc_hi = wk * hi if acc_hi is None else acc_hi + wk * hi
    o_ref[:, :HALF] = base_ref[:, :HALF] + g2_ref[:, :HALF] * acc_lo
    o_ref[:, HALF:] = base_ref[:, HALF:] + g2_ref[:, HALF:] * acc_hi


def _combine_dense(base, g2, w_tok, yg, batch):
    t = base.shape[0]
    seq = yg.shape[1]
    nt = ROUTE_TILE
    tpb = seq // nt
    rows = lambda i: (batch * tpb + i, 0)
    return pl.pallas_call(
        _combine_dense_kernel,
        grid=(tpb,),
        in_specs=[pl.BlockSpec((nt, D_MODEL), rows),
                  pl.BlockSpec((None, 1, D_MODEL), lambda i: (batch, 0, 0)),
                  pl.BlockSpec((nt, TOP_K), rows),
                  pl.BlockSpec((TOP_K, nt, HALF), lambda i: (0, i, 0))],
        out_specs=pl.BlockSpec((nt, D_MODEL), rows),
        out_shape=jax.ShapeDtypeStruct((t, D_MODEL), F32),
        input_output_aliases={0: 0},
        compiler_params=_params(1),
        name="combine_dense",
    )(base, g2, w_tok, yg)


def _layer(layer, x, c, w_ada, b_ada, norm1_w, norm2_w, w_in, q_norm_w, k_norm_w, rel_bias, w_alpha, b_alpha,
           moba_out_w, gla_out_w, w_out, w_router, e_bias, w1, w3, w2, ws1, ws3, ws2):
    b, s, d = x.shape
    t = b * s
    x2 = x.reshape(t, d)

    mod = _mod(c, w_ada, b_ada)
    sh1, sc1, g1, sh2, sc2, g2 = [mod[:, j * d:(j + 1) * d].reshape(b, 1, d) for j in range(6)]

    w_main = w_in[:, :D_MAIN].astype(BF16)
    w_ga = jnp.zeros((d, LANES), BF16).at[:, :GLA_GATE_RANK].set(w_in[:, D_MAIN:].astype(BF16))
    per_chunk = 256 // MOBA_HEAD_DIM
    qw = jnp.tile(q_norm_w.astype(F32), per_chunk).reshape(1, 256) * (MOBA_HEAD_DIM ** -0.5 * LOG2E)
    kw = jnp.tile(k_norm_w.astype(F32), per_chunk).reshape(1, 256)
    proj, ga = _inproj(x2, sc1, sh1, norm1_w.reshape(1, d), w_main, w_ga, qw, kw, s)
    proj3 = proj.reshape(b, s, D_MAIN)

    near, far = _moba_bias_tables(rel_bias)
    ow = jnp.tile(moba_out_w.astype(F32), 2).reshape(1, LANES)
    o_a = _moba(proj3, near, far, ow)

    wal = jnp.zeros((LANES, GLA_KEY_WIDTH), F32).at[:GLA_GATE_RANK].set(w_alpha)
    o_b = _gla(proj3, ga.reshape(b, s, LANES), wal, b_alpha.reshape(1, GLA_KEY_WIDTH),
               gla_out_w.reshape(1, GLA_DV))

    base, h2, scores_t = _outproj(
        o_a.reshape(t, MOBA_WIDTH), o_b.reshape(t, GLA_WIDTH), x2, g1, sc2, sh2, g2,
        norm2_w.reshape(1, d), w_out.astype(BF16), ws1.astype(BF16), ws3.astype(BF16), ws2.astype(BF16),
        w_router.T.astype(BF16), s)

    eb = jnp.broadcast_to(e_bias.astype(F32)[:, None], (N_EXPERTS, ROUTE_TILE))
    code_t, w_t, counts = _route(scores_t, eb)

    r = EXPERT_TILE
    n_tiles = (t * TOP_K + N_EXPERTS * (r - 1) + r - 1) // r
    n_rows = n_tiles * r
    cnt = counts[:, 0].astype(jnp.int32)
    padded = (cnt + r - 1) // r * r
    pend = jnp.cumsum(padded)
    pstart = pend - padded
    n_used = (pend[-1:] // r).astype(jnp.int32)
    dest_t = _slots(pstart, code_t)

    xs = _sc_scatter_rows(h2, dest_t, n_rows)
    ys = _experts(pstart // r, padded // r, n_used, xs, w1, w3, w2, layer)
    w_tok = w_t.T
    out = base
    for bi in range(b):
        idx = dest_t[:, bi * s:(bi + 1) * s].reshape(TOP_K * s)
        yg = _sc_gather_rows(ys, idx).reshape(TOP_K, s, HALF)
        out = _combine_dense(out, g2, w_tok, yg, bi)
    return out.reshape(b, s, d)


def kernel(x, c, w_ada, b_ada, norm1_w, norm2_w, w_in, q_norm_w, k_norm_w, rel_bias, w_alpha, b_alpha,
           moba_out_w, gla_out_w, w_out, w_router, e_bias, w1, w3, w2, ws1, ws3, ws2):
    for l in range(w_ada.shape[0]):
        x = _layer(l, x, c, w_ada[l], b_ada[l], norm1_w[l], norm2_w[l], w_in[l], q_norm_w[l], k_norm_w[l],
                   rel_bias, w_alpha[l], b_alpha[l], moba_out_w[l], gla_out_w[l], w_out[l], w_router[l],
                   e_bias[l], w1, w3, w2, ws1[l], ws3[l], ws2[l])
    return x
```

```python
import functools
import math

import numpy as np
import jax
import jax.numpy as jnp
from jax import lax
from jax.experimental import pallas as pl
from jax.experimental.pallas import tpu as pltpu
from jax.experimental.pallas import tpu_sc as plsc

D_MODEL = 1024
MOBA_HEADS = 8
MOBA_HEAD_DIM = 64
MOBA_WIDTH = MOBA_HEADS * MOBA_HEAD_DIM
MOBA_BLOCK = 256
MOBA_TOPK = 3
GLA_HEADS = 4
GLA_DK = 64
GLA_DV = 128
GLA_KEY_WIDTH = GLA_HEADS * GLA_DK
GLA_WIDTH = GLA_HEADS * GLA_DV
GLA_GATE_RANK = 16
GLA_GATE_TAU = 16.0
GLA_CHUNK = 64
REL_BUCKETS = 32
REL_MAX_DIST = 128
N_EXPERTS = 256
TOP_K = 8
N_GROUPS = 8
TOPK_GROUPS = 4
GROUP_SIZE = N_EXPERTS // N_GROUPS
EXPERT_FF = 256
SHARED_FF = 256
ROUTED_SCALE = 2.5
NORM_EPS = 1e-6
LOG2E = math.log2(math.e)

D_MAIN = 3 * MOBA_WIDTH + 2 * GLA_KEY_WIDTH + 2 * GLA_WIDTH
LANES = 128
VMEM_LIMIT = 56 * 1024 * 1024

ROW_TILE = 512
ROUTE_TILE = 256
EXPERT_TILE = 128
EXPERT_TILES_PER_MATMUL = 6
EXPERT_IN_RING = 16
EXPERT_OUT_RING = 12
EXPERT_WEIGHT_RING = 4

F32 = jnp.float32
BF16 = jnp.bfloat16
NT_DIMS = (((1,), (1,)), ((), ()))
TN_DIMS = (((0,), (0,)), ((), ()))


def _params(n_axes):
    return pltpu.CompilerParams(dimension_semantics=("arbitrary",) * n_axes,
                                vmem_limit_bytes=VMEM_LIMIT)


def _silu(v):
    return v * jax.nn.sigmoid(v)


def _mod_kernel(c_ref, w_ref, b_ref, o_ref):
    o_ref[...] = jnp.dot(_silu(c_ref[...]), w_ref[...], preferred_element_type=F32) + b_ref[...]


def _mod(c, w, b):
    rows = 8
    cp = jnp.zeros((rows, D_MODEL), F32).at[:c.shape[0]].set(c)
    n = w.shape[1]
    tn = 1024
    out = pl.pallas_call(
        _mod_kernel,
        grid=(n // tn,),
        in_specs=[pl.BlockSpec((rows, D_MODEL), lambda j: (0, 0)),
                  pl.BlockSpec((D_MODEL, tn), lambda j: (0, j)),
                  pl.BlockSpec((1, tn), lambda j: (0, j))],
        out_specs=pl.BlockSpec((rows, tn), lambda j: (0, j)),
        out_shape=jax.ShapeDtypeStruct((rows, n), F32),
        compiler_params=_params(1),
        name="mod",
    )(cp, w, b.reshape(1, n))
    return out[:c.shape[0]]


def _group_rms_inv(a, group):
    lane = lax.broadcasted_iota(jnp.int32, (1, a.shape[1]), 1)
    a2 = a * a
    inv = jnp.zeros_like(a)
    for g in range(a.shape[1] // group):
        m = (lane >= g * group) & (lane < (g + 1) * group)
        ss = jnp.sum(jnp.where(m, a2, 0.0), axis=-1, keepdims=True)
        inv = jnp.where(m, lax.rsqrt(ss * (1.0 / group) + NORM_EPS), inv)
    return inv


def _inproj_kernel(x_ref, sc_ref, sh_ref, nw_ref, w_ref, wga_ref, qw_ref, kw_ref, o_ref, ga_ref):
    x = x_ref[...]
    ms = jnp.mean(x * x, axis=-1, keepdims=True)
    h = x * lax.rsqrt(ms + NORM_EPS) * nw_ref[...]
    h = h * (1.0 + sc_ref[...]) + sh_ref[...]
    hb = h.astype(BF16)
    cw = 256
    for j in range(D_MAIN // cw):
        acc = jnp.dot(hb, w_ref[:, j * cw:(j + 1) * cw], preferred_element_type=F32)
        if j < 2 * MOBA_WIDTH // cw:
            nw = qw_ref if j < MOBA_WIDTH // cw else kw_ref
            acc = acc * _group_rms_inv(acc, MOBA_HEAD_DIM) * nw[...]
        o_ref[:, j * cw:(j + 1) * cw] = acc.astype(BF16)
    ga_ref[...] = jnp.dot(hb, wga_ref[...], preferred_element_type=F32)


def _inproj(x2, sc, sh, nw, w_main, w_ga, qw, kw, seq):
    t = x2.shape[0]
    tpb = seq // ROW_TILE
    vec = lambda: pl.BlockSpec((None, 1, D_MODEL), lambda i: (i // tpb, 0, 0))
    full = lambda a: pl.BlockSpec(a.shape, lambda i: (0,) * a.ndim)
    return pl.pallas_call(
        _inproj_kernel,
        grid=(t // ROW_TILE,),
        in_specs=[pl.BlockSpec((ROW_TILE, D_MODEL), lambda i: (i, 0)), vec(), vec(),
                  full(nw), full(w_main), full(w_ga), full(qw), full(kw)],
        out_specs=[pl.BlockSpec((ROW_TILE, D_MAIN), lambda i: (i, 0)),
                   pl.BlockSpec((ROW_TILE, LANES), lambda i: (i, 0))],
        out_shape=[jax.ShapeDtypeStruct((t, D_MAIN), BF16),
                   jax.ShapeDtypeStruct((t, LANES), F32)],
        compiler_params=_params(1),
        name="inproj",
    )(x2, sc, sh, nw, w_main, w_ga, qw, kw)


def _t5_bucket_np(rel):
    max_exact = REL_BUCKETS // 2
    relf = np.maximum(rel, 1).astype(np.float64)
    large = max_exact + (np.log(relf / max_exact) / math.log(REL_MAX_DIST / max_exact)
                         * (REL_BUCKETS - max_exact)).astype(np.int32)
    large = np.minimum(large, REL_BUCKETS - 1)
    return np.where(rel < max_exact, rel, large)


def _bias_kernel(rb_ref, idx_ref, o_ref):
    h = pl.program_id(0)
    idx = idx_ref[...]
    tab = jnp.full(idx.shape, -jnp.inf, F32)
    for bk in range(REL_BUCKETS):
        tab = jnp.where(idx == bk, rb_ref[bk * MOBA_HEADS + h], tab)
    o_ref[...] = tab


def _moba_bias_tables(rel_bias):
    j = np.arange(MOBA_BLOCK)[:, None]
    i = np.arange(MOBA_BLOCK)[None, :]
    own_idx = np.where(j <= i, _t5_bucket_np(np.maximum(i - j, 0)), -1)
    prev_idx = _t5_bucket_np(MOBA_BLOCK + i - j)
    idx = jnp.asarray(np.concatenate([prev_idx, own_idx], axis=0).astype(np.int32))
    assert int(_t5_bucket_np(np.array([MOBA_BLOCK + 1]))[0]) == REL_BUCKETS - 1
    rb = rel_bias.astype(F32) * LOG2E
    near = pl.pallas_call(
        _bias_kernel,
        grid=(MOBA_HEADS,),
        in_specs=[pl.BlockSpec(memory_space=pltpu.SMEM),
                  pl.BlockSpec(idx.shape, lambda h: (0, 0))],
        out_specs=pl.BlockSpec((None,) + idx.shape, lambda h: (h, 0, 0)),
        out_shape=jax.ShapeDtypeStruct((MOBA_HEADS,) + idx.shape, F32),
        compiler_params=_params(1),
        name="bias",
    )(rb.reshape(-1), idx)
    return near, rb[REL_BUCKETS - 1]


FAR_GROUP = 4


MOBA_INTERLEAVE = 4


def _moba_kernel(*refs):
    hp = pl.program_id(1)
    nsets = refs[2].shape[0] // (MOBA_INTERLEAVE * MOBA_BLOCK)
    _moba_body(None, hp, *refs, prepare=True)

    def block_set(j, carry):
        for jj in range(nsets):
            blocks = list(range(jj * MOBA_INTERLEAVE, (jj + 1) * MOBA_INTERLEAVE))
            pl.when(j == jj)(functools.partial(_moba_body, blocks, hp, *refs, prepare=False))
        return carry

    lax.fori_loop(0, nsets, block_set, 0)


def _moba_body(blocks, hp, far_ref, q_ref, k_ref, v_ref, near_ref, ow_ref, o_ref,
               vt_ref, vtg_ref, acc_ref, m_ref, sel_ref, s_ref, mx_ref, *, prepare):
    nblk = k_ref.shape[0] // MOBA_BLOCK
    ngrp = nblk // FAR_GROUP
    hd = MOBA_HEAD_DIM
    bs = MOBA_BLOCK
    lane = lax.broadcasted_iota(jnp.int32, (bs, LANES), 1)

    def split_heads(qb):
        zero = jnp.zeros_like(qb)
        return jnp.where(lane < hd, qb, zero), jnp.where(lane < hd, zero, qb)

    def _prepare():
        row = lax.broadcasted_iota(jnp.int32, (LANES, bs), 0)
        kmeans = []
        for n in range(nblk):
            kb = k_ref[n * bs:(n + 1) * bs, :].astype(F32)
            kmeans.append(jnp.mean(kb, axis=0, keepdims=True))
            vt = v_ref[n * bs:(n + 1) * bs, :].astype(F32).T
            vt0 = jnp.where(row < hd, vt, 1.0).astype(BF16)
            vt1 = jnp.where(row < hd, 1.0, vt).astype(BF16)
            vt_ref[0, n] = vt0
            vt_ref[1, n] = vt1
            gcols = slice((n % FAR_GROUP) * bs, (n % FAR_GROUP + 1) * bs)
            vtg_ref[0, n // FAR_GROUP, :, gcols] = vt0
            vtg_ref[1, n // FAR_GROUP, :, gcols] = vt1
        kmean = jnp.concatenate(kmeans, axis=0)
        km_hi = kmean.astype(BF16)
        km_lo = (kmean - km_hi.astype(F32)).astype(BF16)
        blk = lax.broadcasted_iota(jnp.int32, (nblk, bs), 0)
        for ib in range(nblk):
            qparts = split_heads(q_ref[ib * bs:(ib + 1) * bs, :])
            for h in range(2):
                gt = (lax.dot_general(km_hi, qparts[h], NT_DIMS, preferred_element_type=F32)
                      + lax.dot_general(km_lo, qparts[h], NT_DIMS, preferred_element_type=F32))
                gt = jnp.where(blk < ib, gt, -jnp.inf)
                cnt = jnp.zeros(gt.shape, jnp.int32)
                for m in range(ib):
                    gm = gt[m:m + 1, :]
                    cnt = cnt + jnp.where((gm > gt) | ((gm == gt) & (blk > m)), 1, 0)
                keep = (blk < ib) & (cnt < MOBA_TOPK)
                sel_ref[0, h, ib] = jnp.where(keep, 1.0, 0.0)
                sel_ref[1, h, ib] = jnp.where(keep & (blk < ib - 1), 1.0, 0.0)

    if prepare:
        _prepare()
        return

    gk = FAR_GROUP * bs

    def rows_of(ib):
        if isinstance(ib, int):
            return slice(ib * bs, (ib + 1) * bs)
        return pl.ds(pl.multiple_of(ib * bs, bs), bs)

    def finish(ib, slot):
        a0 = acc_ref[slot, 0]
        a1 = acc_ref[slot, 1]
        row = lax.broadcasted_iota(jnp.int32, a0.shape, 0)
        ot = jnp.where(row < hd, a0 / a0[hd:hd + 1, :], a1 / a1[0:1, :])
        o2 = ot * ot
        ss0 = jnp.sum(jnp.where(row < hd, o2, 0.0), axis=0, keepdims=True)
        ss1 = jnp.sum(jnp.where(row < hd, 0.0, o2), axis=0, keepdims=True)
        inv = jnp.where(row < hd, lax.rsqrt(ss0 * (1.0 / hd) + NORM_EPS), lax.rsqrt(ss1 * (1.0 / hd) + NORM_EPS))
        o_ref[rows_of(ib), :] = ((ot * inv).T * ow_ref[...]).astype(o_ref.dtype)

    def far_scores(g, slot, qh, nb):
        kb = k_ref[g * gk:g * gk + nb * bs, :]
        for h in range(2):
            s = lax.dot_general(kb, qh[h], NT_DIMS, preferred_element_type=F32)
            s_ref[slot, g % 2, h, 0:nb * bs, :] = s
            for j in range(nb):
                mx_ref[slot, g % 2, h, j] = jnp.max(s[j * bs:(j + 1) * bs], axis=0, keepdims=True)

    def near_scores(ib, qh):
        kbs = (k_ref[rows_of(ib - 1), :], k_ref[rows_of(ib), :])
        return [[lax.dot_general(kbs[w], qh[h], NT_DIMS, preferred_element_type=F32)
                 + near_ref[h, w * bs:(w + 1) * bs, :] for w in range(2)] for h in range(2)]

    def near_values(ib, slot, ss):
        ps, ms = [], []
        for h in range(2):
            s_prev, s_own = ss[h]
            keep = sel_ref[0, h, ib, pl.ds(ib - 1, 1), :] > 0.5
            mx = jnp.where(keep, jnp.max(s_prev, axis=0, keepdims=True), -jnp.inf)
            m_new = jnp.maximum(jnp.max(s_own, axis=0, keepdims=True), mx)
            ps.append((jnp.exp2(s_prev - jnp.where(keep, m_new, jnp.inf)).astype(BF16),
                       jnp.exp2(s_own - m_new).astype(BF16)))
            ms.append(m_new)
        for h in range(2):
            acc_ref[slot, h] = (jnp.dot(vt_ref[h, ib - 1], ps[h][0], preferred_element_type=F32)
                                + jnp.dot(vt_ref[h, ib], ps[h][1], preferred_element_type=F32))
            m_ref[slot, h] = ms[h]

    def far_group(g, ib, slot, nb):
        for h in range(2):
            fb = far_ref[2 * hp + h]
            m_old = m_ref[slot, h]
            m_new = m_old
            keeps = []
            for j in range(nb):
                keep = sel_ref[1, h, ib, pl.ds(g * FAR_GROUP + j, 1), :] > 0.5
                m_new = jnp.maximum(m_new, jnp.where(keep, mx_ref[slot, g % 2, h, j] + fb, -jnp.inf))
                keeps.append(keep)
            p = jnp.concatenate(
                [jnp.exp2(s_ref[slot, g % 2, h, j * bs:(j + 1) * bs, :]
                          - jnp.where(keeps[j], m_new - fb, jnp.inf)).astype(BF16)
                 for j in range(nb)], axis=0)
            pv = jnp.dot(vtg_ref[h, g, :, 0:nb * bs], p, preferred_element_type=F32)
            acc_ref[slot, h] = acc_ref[slot, h] * jnp.exp2(m_old - m_new) + pv
            m_ref[slot, h] = m_new

    def step_body(blocks):
        slots = list(enumerate(blocks))
        nbs = [[min(FAR_GROUP, ib - 1 - g * FAR_GROUP) for g in range((ib - 2) // FAR_GROUP + 1)] if ib >= 2 else []
               for ib in blocks]
        qhs = [split_heads(q_ref[rows_of(ib), :]) for ib in blocks]
        near = [near_scores(ib, qhs[slot]) for slot, ib in slots]
        for slot, ib in slots:
            if nbs[slot]:
                far_scores(0, slot, qhs[slot], nbs[slot][0])
        for slot, ib in slots:
            near_values(ib, slot, near[slot])
        for g in range(max(len(n) for n in nbs)):
            for slot, ib in slots:
                if g + 1 < len(nbs[slot]):
                    far_scores(g + 1, slot, qhs[slot], nbs[slot][g + 1])
            for slot, ib in slots:
                if g < len(nbs[slot]):
                    far_group(g, ib, slot, nbs[slot][g])
        for slot, ib in slots:
            finish(ib, slot)

    if blocks[0] == 0:
        qh = split_heads(q_ref[0:bs, :])
        kb = k_ref[0:bs, :]
        for h in range(2):
            s = lax.dot_general(kb, qh[h], NT_DIMS, preferred_element_type=F32) + near_ref[h, bs:2 * bs, :]
            p = jnp.exp2(s - jnp.max(s, axis=0, keepdims=True)).astype(BF16)
            acc_ref[MOBA_INTERLEAVE - 1, h] = jnp.dot(vt_ref[h, 0], p, preferred_element_type=F32)
        finish(0, MOBA_INTERLEAVE - 1)
        blocks = blocks[1:]
    step_body(blocks)


def _moba(proj3, near, far, ow):
    b, s, _ = proj3.shape
    nblk = s // MOBA_BLOCK
    assert nblk % FAR_GROUP == 0
    npair = MOBA_HEADS // 2
    kcol = MOBA_WIDTH // LANES
    return pl.pallas_call(
        _moba_kernel,
        grid=(b, npair),
        in_specs=[pl.BlockSpec(memory_space=pltpu.SMEM),
                  pl.BlockSpec((None, s, LANES), lambda bb, hp: (bb, 0, hp)),
                  pl.BlockSpec((None, s, LANES), lambda bb, hp: (bb, 0, kcol + hp)),
                  pl.BlockSpec((None, s, LANES), lambda bb, hp: (bb, 0, 2 * kcol + hp)),
                  pl.BlockSpec((2, 2 * MOBA_BLOCK, MOBA_BLOCK), lambda bb, hp: (hp, 0, 0)),
                  pl.BlockSpec((1, LANES), lambda bb, hp: (0, 0))],
        out_specs=pl.BlockSpec((None, s, LANES), lambda bb, hp: (bb, 0, hp)),
        out_shape=jax.ShapeDtypeStruct((b, s, MOBA_WIDTH), BF16),
        scratch_shapes=[pltpu.VMEM((2, nblk, LANES, MOBA_BLOCK), BF16),
                        pltpu.VMEM((2, nblk // FAR_GROUP, LANES, FAR_GROUP * MOBA_BLOCK), BF16),
                        pltpu.VMEM((MOBA_INTERLEAVE, 2, LANES, MOBA_BLOCK), F32),
                        pltpu.VMEM((MOBA_INTERLEAVE, 2, 1, MOBA_BLOCK), F32),
                        pltpu.VMEM((2, 2, nblk, nblk, MOBA_BLOCK), F32),
                        pltpu.VMEM((MOBA_INTERLEAVE, 2, 2, FAR_GROUP * MOBA_BLOCK, MOBA_BLOCK), F32),
                        pltpu.VMEM((MOBA_INTERLEAVE, 2, 2, FAR_GROUP, 1, MOBA_BLOCK), F32)],
        compiler_params=_params(2),
        name="moba",
    )(far, proj3, proj3, proj3, near, ow)


def _split3(v):
    hi = v.astype(BF16)
    r1 = v - hi.astype(F32)
    mid = r1.astype(BF16)
    lo = (r1 - mid.astype(F32)).astype(BF16)
    return hi, mid, lo


GLA_UNROLL = 16


def _gla_kernel(q_ref, k_ref, v_ref, g_ref, ga_ref, wal_ref, bal_ref, gw_ref, o_ref, b_ref, st_ref):
    seq = q_ref.shape[0]
    c = GLA_CHUNK
    pc = 256

    rr = lax.broadcasted_iota(jnp.int32, (pc, pc), 0)
    cc = lax.broadcasted_iota(jnp.int32, (pc, pc), 1)
    tri = jnp.where((rr >= cc) & (rr // c == cc // c), 1.0, 0.0).astype(BF16)

    def decay_body(j, carry):
        rows = [pl.ds(pl.multiple_of((j * GLA_UNROLL + u) * pc, pc), pc) for u in range(GLA_UNROLL)]
        xg = [jnp.dot(ga_ref[r, :], wal_ref[...], preferred_element_type=F32) + bal_ref[...] for r in rows]
        parts = [_split3((jnp.minimum(x, 0.0) - jnp.log(1.0 + jnp.exp(-jnp.abs(x)))) * (1.0 / GLA_GATE_TAU))
                 for x in xg]
        sums = [[jnp.dot(tri, term, preferred_element_type=F32) for term in p] for p in parts]
        for r, (hi, mid, lo) in zip(rows, sums):
            b_ref[r, :] = hi + mid + lo
        return carry

    lax.fori_loop(0, seq // (pc * GLA_UNROLL), decay_body, 0)

    st_ref[...] = jnp.zeros_like(st_ref)
    lane = lax.broadcasted_iota(jnp.int32, (c, LANES), 1)
    head_mask = (lane < GLA_DK, lane >= GLA_DK)
    causal = lax.broadcasted_iota(jnp.int32, (c, c), 0) >= lax.broadcasted_iota(jnp.int32, (c, c), 1)

    units = [(u, h) for u in range(GLA_UNROLL) for h in range(2)]

    def chunk_body(ci, carry):
        rows = [pl.ds(pl.multiple_of((ci * GLA_UNROLL + u) * c, c), c) for u in range(GLA_UNROLL)]
        qt, kt, qs, ke, e_last = [], [], [], [], []
        for u in range(GLA_UNROLL):
            b = b_ref[rows[u], :]
            ref_row = b[c // 2 - 1:c // 2, :]
            last = b[c - 1:c, :]
            q = q_ref[rows[u], :].astype(F32) * (GLA_DK ** -0.5)
            k = k_ref[rows[u], :].astype(F32)
            qt.append(q * jnp.exp(b - ref_row))
            kt.append((k * jnp.exp(ref_row - b)).astype(BF16))
            qs.append(q * jnp.exp(b))
            ke.append((k * jnp.exp(last - b)).astype(BF16))
            e_last.append(jnp.exp(last))
        vs = {(u, h): v_ref[rows[u], h * GLA_DV:(h + 1) * GLA_DV] for u, h in units}
        a = {(u, h): lax.dot_general(jnp.where(head_mask[h], qt[u], 0.0).astype(BF16), kt[u], NT_DIMS,
                                     preferred_element_type=F32) for u, h in units}
        inc = {(u, h): lax.dot_general(vs[u, h], ke[u], TN_DIMS, preferred_element_type=F32) for u, h in units}
        o = {(u, h): jnp.dot(jnp.where(causal, a[u, h], 0.0).astype(BF16), vs[u, h], preferred_element_type=F32)
             for u, h in units}
        states = {}
        for h in range(2):
            st = st_ref[h]
            for u in range(GLA_UNROLL):
                states[u, h] = st
                st = st * e_last[u] + inc[u, h]
            st_ref[h] = st
        for u, h in units:
            cols = slice(h * GLA_DV, (h + 1) * GLA_DV)
            ou = o[u, h] + lax.dot_general(jnp.where(head_mask[h], qs[u], 0.0).astype(BF16),
                                           states[u, h].astype(BF16), NT_DIMS, preferred_element_type=F32)
            ms = jnp.mean(ou * ou, axis=-1, keepdims=True)
            on = ou * lax.rsqrt(ms + NORM_EPS) * gw_ref[...]
            g = g_ref[rows[u], cols].astype(F32)
            o_ref[rows[u], cols] = (on * _silu(g)).astype(o_ref.dtype)
        return carry

    lax.fori_loop(0, seq // (c * GLA_UNROLL), chunk_body, 0)


def _gla(proj3, ga3, wal, bal, gw):
    b, s, _ = proj3.shape
    npair = GLA_HEADS // 2
    qcol = 3 * MOBA_WIDTH // LANES
    kcol = qcol + GLA_KEY_WIDTH // LANES
    vcol = (3 * MOBA_WIDTH + 2 * GLA_KEY_WIDTH) // (2 * GLA_DV)
    gcol = vcol + npair
    return pl.pallas_call(
        _gla_kernel,
        grid=(b, npair),
        in_specs=[pl.BlockSpec((None, s, LANES), lambda bb, hp: (bb, 0, qcol + hp)),
                  pl.BlockSpec((None, s, LANES), lambda bb, hp: (bb, 0, kcol + hp)),
                  pl.BlockSpec((None, s, 2 * GLA_DV), lambda bb, hp: (bb, 0, vcol + hp)),
                  pl.BlockSpec((None, s, 2 * GLA_DV), lambda bb, hp: (bb, 0, gcol + hp)),
                  pl.BlockSpec((None, s, LANES), lambda bb, hp: (bb, 0, 0)),
                  pl.BlockSpec((LANES, LANES), lambda bb, hp: (0, hp)),
                  pl.BlockSpec((1, LANES), lambda bb, hp: (0, hp)),
                  pl.BlockSpec((1, GLA_DV), lambda bb, hp: (0, 0))],
        out_specs=pl.BlockSpec((None, s, 2 * GLA_DV), lambda bb, hp: (bb, 0, hp)),
        out_shape=jax.ShapeDtypeStruct((b, s, GLA_WIDTH), BF16),
        scratch_shapes=[pltpu.VMEM((s, LANES), F32),
                        pltpu.VMEM((2, GLA_DV, LANES), F32)],
        compiler_params=_params(2),
        name="gla",
    )(proj3, proj3, proj3, proj3, ga3, wal, bal, gw)


HALF = D_MODEL // 2


def _pack_rows(v):
    return pltpu.pack_elementwise([v[:, :HALF], v[:, HALF:]], packed_dtype=BF16)


def _unpack_rows(w):
    return (pltpu.unpack_elementwise(w, index=0, packed_dtype=BF16, unpacked_dtype=F32),
            pltpu.unpack_elementwise(w, index=1, packed_dtype=BF16, unpacked_dtype=F32))


def _outproj_kernel(oa_ref, ob_ref, x_ref, g1_ref, sc_ref, sh_ref, g2_ref, nw_ref, wo_ref,
                    ws1_ref, ws3_ref, ws2_ref, wrt_ref, eb_ref, base_ref, h_ref, code_ref, w_ref, cnt_ref,
                    carry_ref):
    @pl.when(pl.program_id(0) == 0)
    def _init():
        carry_ref[...] = jnp.zeros_like(carry_ref)

    mix = (jnp.dot(oa_ref[...], wo_ref[:MOBA_WIDTH, :], preferred_element_type=F32)
           + jnp.dot(ob_ref[...], wo_ref[MOBA_WIDTH:, :], preferred_element_type=F32))
    x1 = x_ref[...] + g1_ref[...] * mix
    ms = jnp.mean(x1 * x1, axis=-1, keepdims=True)
    h = x1 * lax.rsqrt(ms + NORM_EPS) * nw_ref[...]
    h = h * (1.0 + sc_ref[...]) + sh_ref[...]
    h_ref[...] = _pack_rows(h)
    hb = h.astype(BF16)
    logits_t = lax.dot_general(wrt_ref[...], hb, NT_DIMS, preferred_element_type=F32)
    scores = jax.nn.sigmoid(logits_t)
    a = jnp.dot(hb, ws1_ref[...], preferred_element_type=F32)
    u = jnp.dot(hb, ws3_ref[...], preferred_element_type=F32)
    shared = jnp.dot((_silu(a) * u).astype(BF16), ws2_ref[...], preferred_element_type=F32)
    base_ref[...] = x1 + g2_ref[...] * shared
    nt = eb_ref.shape[1]
    for part in range(scores.shape[1] // nt):
        cols = slice(part * nt, (part + 1) * nt)
        code, weights = _route_tile(scores[:, cols], eb_ref[...], carry_ref)
        code_ref[:, cols] = code
        w_ref[:, cols] = weights
    cnt_ref[...] = carry_ref[...]


def _outproj(oa, ob, x2, g1, sc, sh, g2, nw, wo, ws1, ws3, ws2, wrt, eb, seq):
    t = x2.shape[0]
    assert t <= SLOT_CODE_BASE
    tpb = seq // ROW_TILE
    vec = lambda: pl.BlockSpec((None, 1, D_MODEL), lambda i: (i // tpb, 0, 0))
    full = lambda a: pl.BlockSpec(a.shape, lambda i: (0,) * a.ndim)
    rows = lambda w: pl.BlockSpec((ROW_TILE, w), lambda i: (i, 0))
    tok = lambda: pl.BlockSpec((TOP_K, ROW_TILE), lambda i: (0, i))
    return pl.pallas_call(
        _outproj_kernel,
        grid=(t // ROW_TILE,),
        in_specs=[rows(MOBA_WIDTH), rows(GLA_WIDTH), rows(D_MODEL), vec(), vec(), vec(), vec(),
                  full(nw), full(wo), full(ws1), full(ws3), full(ws2), full(wrt), full(eb)],
        out_specs=[rows(D_MODEL), rows(HALF), tok(), tok(), full(eb)],
        out_shape=[jax.ShapeDtypeStruct((t, D_MODEL), F32),
                   jax.ShapeDtypeStruct((t, HALF), jnp.uint32),
                   jax.ShapeDtypeStruct((TOP_K, t), jnp.int32),
                   jax.ShapeDtypeStruct((TOP_K, t), F32),
                   jax.ShapeDtypeStruct(eb.shape, F32)],
        scratch_shapes=[pltpu.VMEM(eb.shape, F32)],
        compiler_params=_params(1),
        name="outproj",
    )(oa, ob, x2, g1, sc, sh, g2, nw, wo, ws1, ws3, ws2, wrt, eb)


SLOT_CODE_SHIFT = 16
SLOT_CODE_BASE = 1 << SLOT_CODE_SHIFT


def _route_tile(s, eb, carry_ref):
    ne, nt = s.shape
    choice = s + eb
    gio = lax.broadcasted_iota(jnp.int32, (GROUP_SIZE, nt), 0)
    gscore = []
    for g in range(N_GROUPS):
        cg = choice[g * GROUP_SIZE:(g + 1) * GROUP_SIZE, :]
        top1 = jnp.max(cg, axis=0, keepdims=True)
        first = jnp.min(jnp.where(cg == top1, gio, GROUP_SIZE), axis=0, keepdims=True)
        top2 = jnp.max(jnp.where(gio == first, -jnp.inf, cg), axis=0, keepdims=True)
        gscore.append(top1 + top2)
    gs = jnp.concatenate(gscore, axis=0)
    gidx = lax.broadcasted_iota(jnp.int32, gs.shape, 0)
    beaten = jnp.zeros(gs.shape, jnp.int32)
    for m in range(N_GROUPS):
        gm = gs[m:m + 1, :]
        beaten = beaten + jnp.where((gm > gs) | ((gm == gs) & (gidx > m)), 1, 0)
    gkeep = beaten < TOPK_GROUPS
    masked = jnp.concatenate(
        [jnp.where(gkeep[g:g + 1, :], choice[g * GROUP_SIZE:(g + 1) * GROUP_SIZE, :], -jnp.inf)
         for g in range(N_GROUPS)], axis=0)

    eio = lax.broadcasted_iota(jnp.int32, (ne, nt), 0)
    picked = jnp.zeros((ne, nt), F32)
    idx_rows, w_rows, hits = [], [], []
    for _ in range(TOP_K):
        mx = jnp.max(masked, axis=0, keepdims=True)
        idx = jnp.min(jnp.where(masked == mx, eio, ne), axis=0, keepdims=True)
        hit = eio == idx
        w_rows.append(jnp.sum(jnp.where(hit, s, 0.0), axis=0, keepdims=True))
        idx_rows.append(idx)
        hits.append(hit)
        masked = jnp.where(hit, -jnp.inf, masked)
        picked = jnp.where(hit, 1.0, picked)
    wk = jnp.concatenate(w_rows, axis=0)
    weights = wk / jnp.sum(wk, axis=0, keepdims=True) * ROUTED_SCALE

    tr = lax.broadcasted_iota(jnp.int32, (nt, nt), 0)
    tc = lax.broadcasted_iota(jnp.int32, (nt, nt), 1)
    before = jnp.where(tr < tc, 1.0, 0.0).astype(BF16)
    pb = picked.astype(BF16)
    pos = carry_ref[...] + jnp.dot(pb, before, preferred_element_type=F32)
    rank = jnp.concatenate(
        [jnp.sum(jnp.where(hit, pos, 0.0), axis=0, keepdims=True) for hit in hits], axis=0).astype(jnp.int32)
    carry_ref[...] = carry_ref[...] + jnp.dot(pb, jnp.ones((nt, nt), BF16), preferred_element_type=F32)
    return jnp.concatenate(idx_rows, axis=0) * SLOT_CODE_BASE + rank, weights


SLOT_TILE = 2048


def _slots_kernel(pstart_ref, code_ref, o_ref):
    code = code_ref[...]
    expert = lax.shift_right_logical(code, SLOT_CODE_SHIFT)

    def body(e, acc):
        return jnp.where(expert == e, pstart_ref[e], acc)

    start = lax.fori_loop(0, N_EXPERTS, body, jnp.zeros_like(code), unroll=8)
    o_ref[...] = start + (code & (SLOT_CODE_BASE - 1))


def _slots(pstart, code_t):
    k, t = code_t.shape
    return pl.pallas_call(
        _slots_kernel,
        grid_spec=pltpu.PrefetchScalarGridSpec(
            num_scalar_prefetch=1,
            grid=(t // SLOT_TILE,),
            in_specs=[pl.BlockSpec((k, SLOT_TILE), lambda i, p: (0, i))],
            out_specs=pl.BlockSpec((k, SLOT_TILE), lambda i, p: (0, i)),
        ),
        out_shape=jax.ShapeDtypeStruct((k, t), jnp.int32),
        compiler_params=_params(1),
        name="slots",
    )(pstart, code_t)


SC_WINDOW = 128


def _sc_gather_rows(table, idx_flat):
    info = plsc.get_sparse_core_info()
    nw = info.num_cores * info.num_subcores
    n = idx_flat.shape[0]
    width = table.shape[1]
    per_worker = n // nw
    assert per_worker * nw == n and per_worker % SC_WINDOW == 0
    mesh = plsc.VectorSubcoreMesh(core_axis_name="c", subcore_axis_name="s")

    def body(table_hbm, idx_hbm, out_hbm, idx_v, rows_v, sem):
        wid = lax.axis_index("s") * info.num_cores + lax.axis_index("c")
        base = wid * per_worker

        @pl.loop(0, per_worker // SC_WINDOW)
        def _(w):
            off = pl.multiple_of(base + w * SC_WINDOW, SC_WINDOW)
            pltpu.sync_copy(idx_hbm.at[pl.ds(off, SC_WINDOW)], idx_v)
            pltpu.async_copy(table_hbm.at[idx_v], rows_v, sem).wait()
            pltpu.sync_copy(rows_v, out_hbm.at[pl.ds(off, SC_WINDOW)])

    return pl.kernel(
        body,
        out_type=jax.ShapeDtypeStruct((n, width), table.dtype),
        mesh=mesh,
        scratch_types=[pltpu.VMEM((SC_WINDOW,), jnp.int32),
                       pltpu.VMEM((SC_WINDOW, width), table.dtype),
                       pltpu.SemaphoreType.DMA],
        name="sc_gather",
    )(table, idx_flat)


def _sc_scatter_rows(rows, idx_kt, n_out):
    info = plsc.get_sparse_core_info()
    nw = info.num_cores * info.num_subcores
    t, width = rows.shape
    nk = idx_kt.shape[0]
    per_worker = t // nw
    assert per_worker * nw == t and per_worker % SC_WINDOW == 0
    mesh = plsc.VectorSubcoreMesh(core_axis_name="c", subcore_axis_name="s")

    def body(rows_hbm, idx_hbm, out_hbm, idx_v, rows_v, sem):
        wid = lax.axis_index("s") * info.num_cores + lax.axis_index("c")
        base = wid * per_worker

        @pl.loop(0, per_worker // SC_WINDOW)
        def _(w):
            off = pl.multiple_of(base + w * SC_WINDOW, SC_WINDOW)
            pltpu.sync_copy(rows_hbm.at[pl.ds(off, SC_WINDOW)], rows_v)
            pltpu.sync_copy(idx_hbm.at[:, pl.ds(off, SC_WINDOW)], idx_v)
            copies = [pltpu.async_copy(rows_v, out_hbm.at[idx_v.at[k]], sem) for k in range(nk)]
            for cp in copies:
                cp.wait()

    return pl.kernel(
        body,
        out_type=jax.ShapeDtypeStruct((n_out, width), rows.dtype),
        mesh=mesh,
        scratch_types=[pltpu.VMEM((nk, SC_WINDOW), jnp.int32),
                       pltpu.VMEM((SC_WINDOW, width), rows.dtype),
                       pltpu.SemaphoreType.DMA],
        name="sc_scatter",
    )(rows, idx_kt)


def _expert_kernel(first_ref, count_ref, used_ref, w1_ref, w3_ref, w2_ref, xs_ref, ys_ref,
                   xbuf, ybuf, sem_in, sem_out, w1f, w3f, w2f, sem_w, w1b, w3b, w2b, *, layer):
    e = pl.program_id(0)
    ne = pl.num_programs(0)
    r = xbuf.shape[1]
    n_used = used_ref[0]

    def x_copy(g, slot):
        return pltpu.make_async_copy(xs_ref.at[pl.ds(pl.multiple_of(g * r, r), r), :], xbuf.at[slot], sem_in.at[slot])

    def y_copy(g, slot):
        return pltpu.make_async_copy(ybuf.at[slot], ys_ref.at[pl.ds(pl.multiple_of(g * r, r), r), :], sem_out.at[slot])

    def w_copies(ex, slot):
        return [pltpu.make_async_copy(src.at[layer, ex], dst.at[slot], sem_w.at[slot])
                for src, dst in ((w1_ref, w1f), (w3_ref, w3f), (w2_ref, w2f))]

    nin = xbuf.shape[0]
    nout = ybuf.shape[0]

    nw = w1f.shape[0]

    @pl.when(e == 0)
    def _first_reads():
        for g in range(nin):
            @pl.when(g < n_used)
            def _(g=g):
                x_copy(g, g).start()
        for ex in range(nw - 1):
            for cp in w_copies(ex, ex):
                cp.start(priority=1)

    @pl.when(e + nw - 1 < ne)
    def _weights_ahead():
        for cp in w_copies(e + nw - 1, (e + nw - 1) % nw):
            cp.start(priority=1)

    wslot = e % nw
    for cp in w_copies(e, wslot):
        cp.wait()

    n = count_ref[e]

    @pl.when(n > 0)
    def _cast_weights():
        w1b[...] = w1f[wslot].astype(BF16)
        w3b[...] = w3f[wslot].astype(BF16)
        w2b[...] = w2f[wslot].astype(BF16)

    def run_tiles(g, count):
        tiles = [g + j for j in range(count)]
        for gj in tiles:
            x_copy(gj, gj % nin).wait()
        x = jnp.concatenate([xbuf[gj % nin] for gj in tiles], axis=0)
        for gj in tiles:
            @pl.when(gj + nin < n_used)
            def _(gj=gj):
                x_copy(gj + nin, gj % nin).start()
        lo, hi = _unpack_rows(x)
        lo, hi = lo.astype(BF16), hi.astype(BF16)
        a = (jnp.dot(lo, w1b[:HALF, :], preferred_element_type=F32)
             + jnp.dot(hi, w1b[HALF:, :], preferred_element_type=F32))
        u = (jnp.dot(lo, w3b[:HALF, :], preferred_element_type=F32)
             + jnp.dot(hi, w3b[HALF:, :], preferred_element_type=F32))
        y = _pack_rows(jnp.dot((_silu(a) * u).astype(BF16), w2b[...], preferred_element_type=F32))
        for j, gj in enumerate(tiles):
            @pl.when(gj >= nout)
            def _(gj=gj):
                y_copy(gj - nout, gj % nout).wait()
            ybuf[gj % nout] = y[j * r:(j + 1) * r]
            y_copy(gj, gj % nout).start()

    g0 = first_ref[e]
    pair = EXPERT_TILES_PER_MATMUL

    def pair_body(p, carry):
        run_tiles(g0 + p * pair, pair)
        return carry

    lax.fori_loop(0, n // pair, pair_body, 0)
    for left in range(1, pair):
        pl.when(n % pair == left)(functools.partial(run_tiles, g0 + n - left, left))

    @pl.when(e == ne - 1)
    def _drain_writes():
        for back in range(nout, 0, -1):
            @pl.when(n_used >= back)
            def _(back=back):
                y_copy(n_used - back, (n_used - back) % nout).wait()


def _experts(tile_first, tile_count, n_used, xs, w1, w3, w2, layer):
    n_rows = xs.shape[0]
    r = EXPERT_TILE
    any_spec = pl.BlockSpec(memory_space=pl.ANY)
    return pl.pallas_call(
        functools.partial(_expert_kernel, layer=layer),
        grid_spec=pltpu.PrefetchScalarGridSpec(
            num_scalar_prefetch=3,
            grid=(N_EXPERTS,),
            in_specs=[any_spec, any_spec, any_spec, any_spec],
            out_specs=any_spec,
            scratch_shapes=[pltpu.VMEM((EXPERT_IN_RING, r, HALF), jnp.uint32),
                            pltpu.VMEM((EXPERT_OUT_RING, r, HALF), jnp.uint32),
                            pltpu.SemaphoreType.DMA((EXPERT_IN_RING,)),
                            pltpu.SemaphoreType.DMA((EXPERT_OUT_RING,)),
                            pltpu.VMEM((EXPERT_WEIGHT_RING, D_MODEL, EXPERT_FF), F32),
                            pltpu.VMEM((EXPERT_WEIGHT_RING, D_MODEL, EXPERT_FF), F32),
                            pltpu.VMEM((EXPERT_WEIGHT_RING, EXPERT_FF, D_MODEL), F32),
                            pltpu.SemaphoreType.DMA((EXPERT_WEIGHT_RING,)),
                            pltpu.VMEM((D_MODEL, EXPERT_FF), BF16),
                            pltpu.VMEM((D_MODEL, EXPERT_FF), BF16),
                            pltpu.VMEM((EXPERT_FF, D_MODEL), BF16)],
        ),
        out_shape=jax.ShapeDtypeStruct((n_rows, HALF), jnp.uint32),
        compiler_params=_params(1),
        name="experts",
    )(tile_first, tile_count, n_used, w1, w3, w2, xs)


def _combine_dense_kernel(base_ref, g2_ref, w_ref, yg_ref, o_ref):
    acc_lo = acc_hi = None
    for k in range(TOP_K):
        lo, hi = _unpack_rows(yg_ref[k])
        wk = w_ref[:, k:k + 1]
        acc_lo = wk * lo if acc_lo is None else acc_lo + wk * lo
        acc_hi = wk * hi if acc_hi is None else acc_hi + wk * hi
    o_ref[:, :HALF] = base_ref[:, :HALF] + g2_ref[:, :HALF] * acc_lo
    o_ref[:, HALF:] = base_ref[:, HALF:] + g2_ref[:, HALF:] * acc_hi


def _combine_dense(base, g2, w_tok, yg, batch):
    t = base.shape[0]
    seq = yg.shape[1]
    nt = ROUTE_TILE
    tpb = seq // nt
    rows = lambda i: (batch * tpb + i, 0)
    return pl.pallas_call(
        _combine_dense_kernel,
        grid=(tpb,),
        in_specs=[pl.BlockSpec((nt, D_MODEL), rows),
                  pl.BlockSpec((None, 1, D_MODEL), lambda i: (batch, 0, 0)),
                  pl.BlockSpec((nt, TOP_K), rows),
                  pl.BlockSpec((TOP_K, nt, HALF), lambda i: (0, i, 0))],
        out_specs=pl.BlockSpec((nt, D_MODEL), rows),
        out_shape=jax.ShapeDtypeStruct((t, D_MODEL), F32),
        input_output_aliases={0: 0},
        compiler_params=_params(1),
        name="combine_dense",
    )(base, g2, w_tok, yg)


def _layer(layer, x, c, w_ada, b_ada, norm1_w, norm2_w, w_in, q_norm_w, k_norm_w, rel_bias, w_alpha, b_alpha,
           moba_out_w, gla_out_w, w_out, w_router, e_bias, w1, w3, w2, ws1, ws3, ws2):
    b, s, d = x.shape
    t = b * s
    x2 = x.reshape(t, d)

    mod = _mod(c, w_ada, b_ada)
    sh1, sc1, g1, sh2, sc2, g2 = [mod[:, j * d:(j + 1) * d].reshape(b, 1, d) for j in range(6)]

    w_main = w_in[:, :D_MAIN].astype(BF16)
    w_ga = jnp.zeros((d, LANES), BF16).at[:, :GLA_GATE_RANK].set(w_in[:, D_MAIN:].astype(BF16))
    per_chunk = 256 // MOBA_HEAD_DIM
    qw = jnp.tile(q_norm_w.astype(F32), per_chunk).reshape(1, 256) * (MOBA_HEAD_DIM ** -0.5 * LOG2E)
    kw = jnp.tile(k_norm_w.astype(F32), per_chunk).reshape(1, 256)
    proj, ga = _inproj(x2, sc1, sh1, norm1_w.reshape(1, d), w_main, w_ga, qw, kw, s)
    proj3 = proj.reshape(b, s, D_MAIN)

    near, far = _moba_bias_tables(rel_bias)
    ow = jnp.tile(moba_out_w.astype(F32), 2).reshape(1, LANES)
    o_a = _moba(proj3, near, far, ow)

    wal = jnp.zeros((LANES, GLA_KEY_WIDTH), F32).at[:GLA_GATE_RANK].set(w_alpha)
    o_b = _gla(proj3, ga.reshape(b, s, LANES), wal, b_alpha.reshape(1, GLA_KEY_WIDTH),
               gla_out_w.reshape(1, GLA_DV))

    eb = jnp.broadcast_to(e_bias.astype(F32)[:, None], (N_EXPERTS, ROUTE_TILE))
    base, h2, code_t, w_t, counts = _outproj(
        o_a.reshape(t, MOBA_WIDTH), o_b.reshape(t, GLA_WIDTH), x2, g1, sc2, sh2, g2,
        norm2_w.reshape(1, d), w_out.astype(BF16), ws1.astype(BF16), ws3.astype(BF16), ws2.astype(BF16),
        w_router.T.astype(BF16), eb, s)

    r = EXPERT_TILE
    n_tiles = (t * TOP_K + N_EXPERTS * (r - 1) + r - 1) // r
    n_rows = n_tiles * r
    cnt = counts[:, 0].astype(jnp.int32)
    padded = (cnt + r - 1) // r * r
    pend = jnp.cumsum(padded)
    pstart = pend - padded
    n_used = (pend[-1:] // r).astype(jnp.int32)
    dest_t = _slots(pstart, code_t)

    xs = _sc_scatter_rows(h2, dest_t, n_rows)
    ys = _experts(pstart // r, padded // r, n_used, xs, w1, w3, w2, layer)
    w_tok = w_t.T
    out = base
    for bi in range(b):
        idx = dest_t[:, bi * s:(bi + 1) * s].reshape(TOP_K * s)
        yg = _sc_gather_rows(ys, idx).reshape(TOP_K, s, HALF)
        out = _combine_dense(out, g2, w_tok, yg, bi)
    return out.reshape(b, s, d)


def kernel(x, c, w_ada, b_ada, norm1_w, norm2_w, w_in, q_norm_w, k_norm_w, rel_bias, w_alpha, b_alpha,
           moba_out_w, gla_out_w, w_out, w_router, e_bias, w1, w3, w2, ws1, ws3, ws2):
    for l in range(w_ada.shape[0]):
        x = _layer(l, x, c, w_ada[l], b_ada[l], norm1_w[l], norm2_w[l], w_in[l], q_norm_w[l], k_norm_w[l],
                   rel_bias, w_alpha[l], b_alpha[l], moba_out_w[l], gla_out_w[l], w_out[l], w_router[l],
                   e_bias[l], w1, w3, w2, ws1[l], ws3[l], ws2[l])
    return x
```

```python
import functools
import math

import numpy as np
import jax
import jax.numpy as jnp
from jax import lax
from jax.experimental import pallas as pl
from jax.experimental.pallas import tpu as pltpu
from jax.experimental.pallas import tpu_sc as plsc

D_MODEL = 1024
MOBA_HEADS = 8
MOBA_HEAD_DIM = 64
MOBA_WIDTH = MOBA_HEADS * MOBA_HEAD_DIM
MOBA_BLOCK = 256
MOBA_TOPK = 3
GLA_HEADS = 4
GLA_DK = 64
GLA_DV = 128
GLA_KEY_WIDTH = GLA_HEADS * GLA_DK
GLA_WIDTH = GLA_HEADS * GLA_DV
GLA_GATE_RANK = 16
GLA_GATE_TAU = 16.0
GLA_CHUNK = 64
REL_BUCKETS = 32
REL_MAX_DIST = 128
N_EXPERTS = 256
TOP_K = 8
N_GROUPS = 8
TOPK_GROUPS = 4
GROUP_SIZE = N_EXPERTS // N_GROUPS
EXPERT_FF = 256
SHARED_FF = 256
ROUTED_SCALE = 2.5
NORM_EPS = 1e-6
LOG2E = math.log2(math.e)

D_MAIN = 3 * MOBA_WIDTH + 2 * GLA_KEY_WIDTH + 2 * GLA_WIDTH
LANES = 128
VMEM_LIMIT = 56 * 1024 * 1024

ROW_TILE = 512
ROUTE_TILE = 256
EXPERT_TILE = 128
EXPERT_TILES_PER_MATMUL = 6
EXPERT_IN_RING = 16
EXPERT_OUT_RING = 12
EXPERT_WEIGHT_RING = 4

F32 = jnp.float32
BF16 = jnp.bfloat16
NT_DIMS = (((1,), (1,)), ((), ()))
TN_DIMS = (((0,), (0,)), ((), ()))


def _params(n_axes):
    return pltpu.CompilerParams(dimension_semantics=("arbitrary",) * n_axes,
                                vmem_limit_bytes=VMEM_LIMIT)


def _silu(v):
    return v * jax.nn.sigmoid(v)


def _mod_kernel(c_ref, w_ref, b_ref, o_ref):
    o_ref[...] = jnp.dot(_silu(c_ref[...]), w_ref[...], preferred_element_type=F32) + b_ref[...]


def _mod(c, w, b):
    rows = 8
    cp = jnp.zeros((rows, D_MODEL), F32).at[:c.shape[0]].set(c)
    n = w.shape[1]
    tn = 1024
    out = pl.pallas_call(
        _mod_kernel,
        grid=(n // tn,),
        in_specs=[pl.BlockSpec((rows, D_MODEL), lambda j: (0, 0)),
                  pl.BlockSpec((D_MODEL, tn), lambda j: (0, j)),
                  pl.BlockSpec((1, tn), lambda j: (0, j))],
        out_specs=pl.BlockSpec((rows, tn), lambda j: (0, j)),
        out_shape=jax.ShapeDtypeStruct((rows, n), F32),
        compiler_params=_params(1),
        name="mod",
    )(cp, w, b.reshape(1, n))
    return out[:c.shape[0]]


def _group_rms_inv(a, group):
    lane = lax.broadcasted_iota(jnp.int32, (1, a.shape[1]), 1)
    a2 = a * a
    inv = jnp.zeros_like(a)
    for g in range(a.shape[1] // group):
        m = (lane >= g * group) & (lane < (g + 1) * group)
        ss = jnp.sum(jnp.where(m, a2, 0.0), axis=-1, keepdims=True)
        inv = jnp.where(m, lax.rsqrt(ss * (1.0 / group) + NORM_EPS), inv)
    return inv


def _inproj_kernel(x_ref, sc_ref, sh_ref, nw_ref, w_ref, wga_ref, qw_ref, kw_ref, o_ref, ga_ref):
    x = x_ref[...]
    ms = jnp.mean(x * x, axis=-1, keepdims=True)
    h = x * lax.rsqrt(ms + NORM_EPS) * nw_ref[...]
    h = h * (1.0 + sc_ref[...]) + sh_ref[...]
    hb = h.astype(BF16)
    cw = 256
    for j in range(D_MAIN // cw):
        acc = jnp.dot(hb, w_ref[:, j * cw:(j + 1) * cw], preferred_element_type=F32)
        if j < 2 * MOBA_WIDTH // cw:
            nw = qw_ref if j < MOBA_WIDTH // cw else kw_ref
            acc = acc * _group_rms_inv(acc, MOBA_HEAD_DIM) * nw[...]
        o_ref[:, j * cw:(j + 1) * cw] = acc.astype(BF16)
    ga_ref[...] = jnp.dot(hb, wga_ref[...], preferred_element_type=F32)


def _inproj(x2, sc, sh, nw, w_main, w_ga, qw, kw, seq):
    t = x2.shape[0]
    tpb = seq // ROW_TILE
    vec = lambda: pl.BlockSpec((None, 1, D_MODEL), lambda i: (i // tpb, 0, 0))
    full = lambda a: pl.BlockSpec(a.shape, lambda i: (0,) * a.ndim)
    return pl.pallas_call(
        _inproj_kernel,
        grid=(t // ROW_TILE,),
        in_specs=[pl.BlockSpec((ROW_TILE, D_MODEL), lambda i: (i, 0)), vec(), vec(),
                  full(nw), full(w_main), full(w_ga), full(qw), full(kw)],
        out_specs=[pl.BlockSpec((ROW_TILE, D_MAIN), lambda i: (i, 0)),
                   pl.BlockSpec((ROW_TILE, LANES), lambda i: (i, 0))],
        out_shape=[jax.ShapeDtypeStruct((t, D_MAIN), BF16),
                   jax.ShapeDtypeStruct((t, LANES), F32)],
        compiler_params=_params(1),
        name="inproj",
    )(x2, sc, sh, nw, w_main, w_ga, qw, kw)


def _t5_bucket_np(rel):
    max_exact = REL_BUCKETS // 2
    relf = np.maximum(rel, 1).astype(np.float64)
    large = max_exact + (np.log(relf / max_exact) / math.log(REL_MAX_DIST / max_exact)
                         * (REL_BUCKETS - max_exact)).astype(np.int32)
    large = np.minimum(large, REL_BUCKETS - 1)
    return np.where(rel < max_exact, rel, large)


def _bias_kernel(rb_ref, idx_ref, o_ref):
    h = pl.program_id(0)
    idx = idx_ref[...]
    tab = jnp.full(idx.shape, -jnp.inf, F32)
    for bk in range(REL_BUCKETS):
        tab = jnp.where(idx == bk, rb_ref[bk * MOBA_HEADS + h], tab)
    o_ref[...] = tab


def _moba_bias_tables(rel_bias):
    j = np.arange(MOBA_BLOCK)[:, None]
    i = np.arange(MOBA_BLOCK)[None, :]
    own_idx = np.where(j <= i, _t5_bucket_np(np.maximum(i - j, 0)), -1)
    prev_idx = _t5_bucket_np(MOBA_BLOCK + i - j)
    idx = jnp.asarray(np.concatenate([prev_idx, own_idx], axis=0).astype(np.int32))
    assert int(_t5_bucket_np(np.array([MOBA_BLOCK + 1]))[0]) == REL_BUCKETS - 1
    rb = rel_bias.astype(F32) * LOG2E
    near = pl.pallas_call(
        _bias_kernel,
        grid=(MOBA_HEADS,),
        in_specs=[pl.BlockSpec(memory_space=pltpu.SMEM),
                  pl.BlockSpec(idx.shape, lambda h: (0, 0))],
        out_specs=pl.BlockSpec((None,) + idx.shape, lambda h: (h, 0, 0)),
        out_shape=jax.ShapeDtypeStruct((MOBA_HEADS,) + idx.shape, F32),
        compiler_params=_params(1),
        name="bias",
    )(rb.reshape(-1), idx)
    return near, rb[REL_BUCKETS - 1]


FAR_GROUP = 4


MOBA_INTERLEAVE = 4


def _moba_kernel(*refs):
    hp = pl.program_id(1)
    nsets = refs[2].shape[0] // (MOBA_INTERLEAVE * MOBA_BLOCK)
    _moba_body(None, hp, *refs, prepare=True)

    def block_set(j, carry):
        for jj in range(nsets):
            blocks = list(range(jj * MOBA_INTERLEAVE, (jj + 1) * MOBA_INTERLEAVE))
            pl.when(j == jj)(functools.partial(_moba_body, blocks, hp, *refs, prepare=False))
        return carry

    lax.fori_loop(0, nsets, block_set, 0)


def _moba_body(blocks, hp, far_ref, q_ref, k_ref, v_ref, near_ref, ow_ref, o_ref,
               vt_ref, vtg_ref, acc_ref, m_ref, sel_ref, s_ref, mx_ref, *, prepare):
    nblk = k_ref.shape[0] // MOBA_BLOCK
    ngrp = nblk // FAR_GROUP
    hd = MOBA_HEAD_DIM
    bs = MOBA_BLOCK
    lane = lax.broadcasted_iota(jnp.int32, (bs, LANES), 1)

    def split_heads(qb):
        zero = jnp.zeros_like(qb)
        return jnp.where(lane < hd, qb, zero), jnp.where(lane < hd, zero, qb)

    def _prepare():
        row = lax.broadcasted_iota(jnp.int32, (LANES, bs), 0)
        kmeans = []
        for n in range(nblk):
            kb = k_ref[n * bs:(n + 1) * bs, :].astype(F32)
            kmeans.append(jnp.mean(kb, axis=0, keepdims=True))
            vt = v_ref[n * bs:(n + 1) * bs, :].astype(F32).T
            vt0 = jnp.where(row < hd, vt, 1.0).astype(BF16)
            vt1 = jnp.where(row < hd, 1.0, vt).astype(BF16)
            vt_ref[0, n] = vt0
            vt_ref[1, n] = vt1
            gcols = slice((n % FAR_GROUP) * bs, (n % FAR_GROUP + 1) * bs)
            vtg_ref[0, n // FAR_GROUP, :, gcols] = vt0
            vtg_ref[1, n // FAR_GROUP, :, gcols] = vt1
        kmean = jnp.concatenate(kmeans, axis=0)
        km_hi = kmean.astype(BF16)
        km_lo = (kmean - km_hi.astype(F32)).astype(BF16)
        blk = lax.broadcasted_iota(jnp.int32, (nblk, bs), 0)
        for ib in range(nblk):
            qparts = split_heads(q_ref[ib * bs:(ib + 1) * bs, :])
            for h in range(2):
                gt = (lax.dot_general(km_hi, qparts[h], NT_DIMS, preferred_element_type=F32)
                      + lax.dot_general(km_lo, qparts[h], NT_DIMS, preferred_element_type=F32))
                gt = jnp.where(blk < ib, gt, -jnp.inf)
                cnt = jnp.zeros(gt.shape, jnp.int32)
                for m in range(ib):
                    gm = gt[m:m + 1, :]
                    cnt = cnt + jnp.where((gm > gt) | ((gm == gt) & (blk > m)), 1, 0)
                keep = (blk < ib) & (cnt < MOBA_TOPK)
                sel_ref[0, h, ib] = jnp.where(keep, 1.0, 0.0)
                sel_ref[1, h, ib] = jnp.where(keep & (blk < ib - 1), 1.0, 0.0)

    if prepare:
        _prepare()
        return

    gk = FAR_GROUP * bs

    def rows_of(ib):
        if isinstance(ib, int):
            return slice(ib * bs, (ib + 1) * bs)
        return pl.ds(pl.multiple_of(ib * bs, bs), bs)

    def finish(ib, slot):
        a0 = acc_ref[slot, 0]
        a1 = acc_ref[slot, 1]
        row = lax.broadcasted_iota(jnp.int32, a0.shape, 0)
        ot = jnp.where(row < hd, a0 / a0[hd:hd + 1, :], a1 / a1[0:1, :])
        o2 = ot * ot
        ss0 = jnp.sum(jnp.where(row < hd, o2, 0.0), axis=0, keepdims=True)
        ss1 = jnp.sum(jnp.where(row < hd, 0.0, o2), axis=0, keepdims=True)
        inv = jnp.where(row < hd, lax.rsqrt(ss0 * (1.0 / hd) + NORM_EPS), lax.rsqrt(ss1 * (1.0 / hd) + NORM_EPS))
        o_ref[rows_of(ib), :] = ((ot * inv).T * ow_ref[...]).astype(o_ref.dtype)

    def far_scores(g, slot, qh, nb):
        kb = k_ref[g * gk:g * gk + nb * bs, :]
        for h in range(2):
            s = lax.dot_general(kb, qh[h], NT_DIMS, preferred_element_type=F32)
            s_ref[slot, g % 2, h, 0:nb * bs, :] = s
            for j in range(nb):
                mx_ref[slot, g % 2, h, j] = jnp.max(s[j * bs:(j + 1) * bs], axis=0, keepdims=True)

    def near_scores(ib, qh):
        kbs = (k_ref[rows_of(ib - 1), :], k_ref[rows_of(ib), :])
        return [[lax.dot_general(kbs[w], qh[h], NT_DIMS, preferred_element_type=F32)
                 + near_ref[h, w * bs:(w + 1) * bs, :] for w in range(2)] for h in range(2)]

    def near_values(ib, slot, ss):
        ps, ms = [], []
        for h in range(2):
            s_prev, s_own = ss[h]
            keep = sel_ref[0, h, ib, pl.ds(ib - 1, 1), :] > 0.5
            mx = jnp.where(keep, jnp.max(s_prev, axis=0, keepdims=True), -jnp.inf)
            m_new = jnp.maximum(jnp.max(s_own, axis=0, keepdims=True), mx)
            ps.append((jnp.exp2(s_prev - jnp.where(keep, m_new, jnp.inf)).astype(BF16),
                       jnp.exp2(s_own - m_new).astype(BF16)))
            ms.append(m_new)
        for h in range(2):
            acc_ref[slot, h] = (jnp.dot(vt_ref[h, ib - 1], ps[h][0], preferred_element_type=F32)
                                + jnp.dot(vt_ref[h, ib], ps[h][1], preferred_element_type=F32))
            m_ref[slot, h] = ms[h]

    def far_group(g, ib, slot, nb):
        for h in range(2):
            fb = far_ref[2 * hp + h]
            m_old = m_ref[slot, h]
            m_new = m_old
            keeps = []
            for j in range(nb):
                keep = sel_ref[1, h, ib, pl.ds(g * FAR_GROUP + j, 1), :] > 0.5
                m_new = jnp.maximum(m_new, jnp.where(keep, mx_ref[slot, g % 2, h, j] + fb, -jnp.inf))
                keeps.append(keep)
            p = jnp.concatenate(
                [jnp.exp2(s_ref[slot, g % 2, h, j * bs:(j + 1) * bs, :]
                          - jnp.where(keeps[j], m_new - fb, jnp.inf)).astype(BF16)
                 for j in range(nb)], axis=0)
            pv = jnp.dot(vtg_ref[h, g, :, 0:nb * bs], p, preferred_element_type=F32)
            acc_ref[slot, h] = acc_ref[slot, h] * jnp.exp2(m_old - m_new) + pv
            m_ref[slot, h] = m_new

    def step_body(blocks):
        slots = list(enumerate(blocks))
        nbs = [[min(FAR_GROUP, ib - 1 - g * FAR_GROUP) for g in range((ib - 2) // FAR_GROUP + 1)] if ib >= 2 else []
               for ib in blocks]
        qhs = [split_heads(q_ref[rows_of(ib), :]) for ib in blocks]
        near = [near_scores(ib, qhs[slot]) for slot, ib in slots]
        for slot, ib in slots:
            if nbs[slot]:
                far_scores(0, slot, qhs[slot], nbs[slot][0])
        for slot, ib in slots:
            near_values(ib, slot, near[slot])
        for g in range(max(len(n) for n in nbs)):
            for slot, ib in slots:
                if g + 1 < len(nbs[slot]):
                    far_scores(g + 1, slot, qhs[slot], nbs[slot][g + 1])
            for slot, ib in slots:
                if g < len(nbs[slot]):
                    far_group(g, ib, slot, nbs[slot][g])
        for slot, ib in slots:
            finish(ib, slot)

    if blocks[0] == 0:
        qh = split_heads(q_ref[0:bs, :])
        kb = k_ref[0:bs, :]
        for h in range(2):
            s = lax.dot_general(kb, qh[h], NT_DIMS, preferred_element_type=F32) + near_ref[h, bs:2 * bs, :]
            p = jnp.exp2(s - jnp.max(s, axis=0, keepdims=True)).astype(BF16)
            acc_ref[MOBA_INTERLEAVE - 1, h] = jnp.dot(vt_ref[h, 0], p, preferred_element_type=F32)
        finish(0, MOBA_INTERLEAVE - 1)
        blocks = blocks[1:]
    step_body(blocks)


def _moba(proj3, near, far, ow):
    b, s, _ = proj3.shape
    nblk = s // MOBA_BLOCK
    assert nblk % FAR_GROUP == 0
    npair = MOBA_HEADS // 2
    kcol = MOBA_WIDTH // LANES
    return pl.pallas_call(
        _moba_kernel,
        grid=(b, npair),
        in_specs=[pl.BlockSpec(memory_space=pltpu.SMEM),
                  pl.BlockSpec((None, s, LANES), lambda bb, hp: (bb, 0, hp)),
                  pl.BlockSpec((None, s, LANES), lambda bb, hp: (bb, 0, kcol + hp)),
                  pl.BlockSpec((None, s, LANES), lambda bb, hp: (bb, 0, 2 * kcol + hp)),
                  pl.BlockSpec((2, 2 * MOBA_BLOCK, MOBA_BLOCK), lambda bb, hp: (hp, 0, 0)),
                  pl.BlockSpec((1, LANES), lambda bb, hp: (0, 0))],
        out_specs=pl.BlockSpec((None, s, LANES), lambda bb, hp: (bb, 0, hp)),
        out_shape=jax.ShapeDtypeStruct((b, s, MOBA_WIDTH), BF16),
        scratch_shapes=[pltpu.VMEM((2, nblk, LANES, MOBA_BLOCK), BF16),
                        pltpu.VMEM((2, nblk // FAR_GROUP, LANES, FAR_GROUP * MOBA_BLOCK), BF16),
                        pltpu.VMEM((MOBA_INTERLEAVE, 2, LANES, MOBA_BLOCK), F32),
                        pltpu.VMEM((MOBA_INTERLEAVE, 2, 1, MOBA_BLOCK), F32),
                        pltpu.VMEM((2, 2, nblk, nblk, MOBA_BLOCK), F32),
                        pltpu.VMEM((MOBA_INTERLEAVE, 2, 2, FAR_GROUP * MOBA_BLOCK, MOBA_BLOCK), F32),
                        pltpu.VMEM((MOBA_INTERLEAVE, 2, 2, FAR_GROUP, 1, MOBA_BLOCK), F32)],
        compiler_params=_params(2),
        name="moba",
    )(far, proj3, proj3, proj3, near, ow)


def _split3(v):
    hi = v.astype(BF16)
    r1 = v - hi.astype(F32)
    mid = r1.astype(BF16)
    lo = (r1 - mid.astype(F32)).astype(BF16)
    return hi, mid, lo


GLA_UNROLL = 16


def _gla_kernel(q_ref, k_ref, v_ref, g_ref, ga_ref, wal_ref, bal_ref, gw_ref, o_ref, b_ref, st_ref):
    seq = q_ref.shape[0]
    c = GLA_CHUNK
    pc = 256

    rr = lax.broadcasted_iota(jnp.int32, (pc, pc), 0)
    cc = lax.broadcasted_iota(jnp.int32, (pc, pc), 1)
    tri = jnp.where((rr >= cc) & (rr // c == cc // c), 1.0, 0.0).astype(BF16)

    def decay_body(j, carry):
        rows = [pl.ds(pl.multiple_of((j * GLA_UNROLL + u) * pc, pc), pc) for u in range(GLA_UNROLL)]
        xg = [jnp.dot(ga_ref[r, :], wal_ref[...], preferred_element_type=F32) + bal_ref[...] for r in rows]
        parts = [_split3((jnp.minimum(x, 0.0) - jnp.log(1.0 + jnp.exp(-jnp.abs(x)))) * (1.0 / GLA_GATE_TAU))
                 for x in xg]
        sums = [[jnp.dot(tri, term, preferred_element_type=F32) for term in p] for p in parts]
        for r, (hi, mid, lo) in zip(rows, sums):
            b_ref[r, :] = hi + mid + lo
        return carry

    lax.fori_loop(0, seq // (pc * GLA_UNROLL), decay_body, 0)

    st_ref[...] = jnp.zeros_like(st_ref)
    lane = lax.broadcasted_iota(jnp.int32, (c, LANES), 1)
    head_mask = (lane < GLA_DK, lane >= GLA_DK)
    causal = lax.broadcasted_iota(jnp.int32, (c, c), 0) >= lax.broadcasted_iota(jnp.int32, (c, c), 1)

    units = [(u, h) for u in range(GLA_UNROLL) for h in range(2)]

    def chunk_body(ci, carry):
        rows = [pl.ds(pl.multiple_of((ci * GLA_UNROLL + u) * c, c), c) for u in range(GLA_UNROLL)]
        qt, kt, qs, ke, e_last = [], [], [], [], []
        for u in range(GLA_UNROLL):
            b = b_ref[rows[u], :]
            ref_row = b[c // 2 - 1:c // 2, :]
            last = b[c - 1:c, :]
            q = q_ref[rows[u], :].astype(F32) * (GLA_DK ** -0.5)
            k = k_ref[rows[u], :].astype(F32)
            qt.append(q * jnp.exp(b - ref_row))
            kt.append((k * jnp.exp(ref_row - b)).astype(BF16))
            qs.append(q * jnp.exp(b))
            ke.append((k * jnp.exp(last - b)).astype(BF16))
            e_last.append(jnp.exp(last))
        vs = {(u, h): v_ref[rows[u], h * GLA_DV:(h + 1) * GLA_DV] for u, h in units}
        a = {(u, h): lax.dot_general(jnp.where(head_mask[h], qt[u], 0.0).astype(BF16), kt[u], NT_DIMS,
                                     preferred_element_type=F32) for u, h in units}
        inc = {(u, h): lax.dot_general(vs[u, h], ke[u], TN_DIMS, preferred_element_type=F32) for u, h in units}
        o = {(u, h): jnp.dot(jnp.where(causal, a[u, h], 0.0).astype(BF16), vs[u, h], preferred_element_type=F32)
             for u, h in units}
        states = {}
        for h in range(2):
            st = st_ref[h]
            for u in range(GLA_UNROLL):
                states[u, h] = st
                st = st * e_last[u] + inc[u, h]
            st_ref[h] = st
        for u, h in units:
            cols = slice(h * GLA_DV, (h + 1) * GLA_DV)
            ou = o[u, h] + lax.dot_general(jnp.where(head_mask[h], qs[u], 0.0).astype(BF16),
                                           states[u, h].astype(BF16), NT_DIMS, preferred_element_type=F32)
            ms = jnp.mean(ou * ou, axis=-1, keepdims=True)
            on = ou * lax.rsqrt(ms + NORM_EPS) * gw_ref[...]
            g = g_ref[rows[u], cols].astype(F32)
            o_ref[rows[u], cols] = (on * _silu(g)).astype(o_ref.dtype)
        return carry

    lax.fori_loop(0, seq // (c * GLA_UNROLL), chunk_body, 0)


def _gla(proj3, ga3, wal, bal, gw):
    b, s, _ = proj3.shape
    npair = GLA_HEADS // 2
    qcol = 3 * MOBA_WIDTH // LANES
    kcol = qcol + GLA_KEY_WIDTH // LANES
    vcol = (3 * MOBA_WIDTH + 2 * GLA_KEY_WIDTH) // (2 * GLA_DV)
    gcol = vcol + npair
    return pl.pallas_call(
        _gla_kernel,
        grid=(b, npair),
        in_specs=[pl.BlockSpec((None, s, LANES), lambda bb, hp: (bb, 0, qcol + hp)),
                  pl.BlockSpec((None, s, LANES), lambda bb, hp: (bb, 0, kcol + hp)),
                  pl.BlockSpec((None, s, 2 * GLA_DV), lambda bb, hp: (bb, 0, vcol + hp)),
                  pl.BlockSpec((None, s, 2 * GLA_DV), lambda bb, hp: (bb, 0, gcol + hp)),
                  pl.BlockSpec((None, s, LANES), lambda bb, hp: (bb, 0, 0)),
                  pl.BlockSpec((LANES, LANES), lambda bb, hp: (0, hp)),
                  pl.BlockSpec((1, LANES), lambda bb, hp: (0, hp)),
                  pl.BlockSpec((1, GLA_DV), lambda bb, hp: (0, 0))],
        out_specs=pl.BlockSpec((None, s, 2 * GLA_DV), lambda bb, hp: (bb, 0, hp)),
        out_shape=jax.ShapeDtypeStruct((b, s, GLA_WIDTH), BF16),
        scratch_shapes=[pltpu.VMEM((s, LANES), F32),
                        pltpu.VMEM((2, GLA_DV, LANES), F32)],
        compiler_params=_params(2),
        name="gla",
    )(proj3, proj3, proj3, proj3, ga3, wal, bal, gw)


HALF = D_MODEL // 2


def _pack_rows(v):
    return pltpu.pack_elementwise([v[:, :HALF], v[:, HALF:]], packed_dtype=BF16)


def _unpack_rows(w):
    return (pltpu.unpack_elementwise(w, index=0, packed_dtype=BF16, unpacked_dtype=F32),
            pltpu.unpack_elementwise(w, index=1, packed_dtype=BF16, unpacked_dtype=F32))


def _outproj_kernel(oa_ref, ob_ref, x_ref, g1_ref, sc_ref, sh_ref, g2_ref, nw_ref, wo_ref,
                    ws1_ref, ws3_ref, ws2_ref, wrt_ref, eb_ref, base_ref, h_ref, code_ref, w_ref, cnt_ref,
                    carry_ref):
    @pl.when(pl.program_id(0) == 0)
    def _init():
        carry_ref[...] = jnp.zeros_like(carry_ref)

    mix = (jnp.dot(oa_ref[...], wo_ref[:MOBA_WIDTH, :], preferred_element_type=F32)
           + jnp.dot(ob_ref[...], wo_ref[MOBA_WIDTH:, :], preferred_element_type=F32))
    x1 = x_ref[...] + g1_ref[...] * mix
    ms = jnp.mean(x1 * x1, axis=-1, keepdims=True)
    h = x1 * lax.rsqrt(ms + NORM_EPS) * nw_ref[...]
    h = h * (1.0 + sc_ref[...]) + sh_ref[...]
    h_ref[...] = _pack_rows(h)
    hb = h.astype(BF16)
    logits_t = lax.dot_general(wrt_ref[...], hb, NT_DIMS, preferred_element_type=F32)
    scores = jax.nn.sigmoid(logits_t)
    a = jnp.dot(hb, ws1_ref[...], preferred_element_type=F32)
    u = jnp.dot(hb, ws3_ref[...], preferred_element_type=F32)
    shared = jnp.dot((_silu(a) * u).astype(BF16), ws2_ref[...], preferred_element_type=F32)
    base_ref[...] = x1 + g2_ref[...] * shared
    nt = eb_ref.shape[1]
    for part in range(scores.shape[1] // nt):
        cols = slice(part * nt, (part + 1) * nt)
        code, weights = _route_tile(scores[:, cols], eb_ref[...], carry_ref)
        code_ref[:, cols] = code
        w_ref[:, cols] = weights
    cnt_ref[...] = carry_ref[...]


def _outproj(oa, ob, x2, g1, sc, sh, g2, nw, wo, ws1, ws3, ws2, wrt, eb, seq):
    t = x2.shape[0]
    assert t <= SLOT_CODE_BASE
    tpb = seq // ROW_TILE
    vec = lambda: pl.BlockSpec((None, 1, D_MODEL), lambda i: (i // tpb, 0, 0))
    full = lambda a: pl.BlockSpec(a.shape, lambda i: (0,) * a.ndim)
    rows = lambda w: pl.BlockSpec((ROW_TILE, w), lambda i: (i, 0))
    tok = lambda: pl.BlockSpec((TOP_K, ROW_TILE), lambda i: (0, i))
    return pl.pallas_call(
        _outproj_kernel,
        grid=(t // ROW_TILE,),
        in_specs=[rows(MOBA_WIDTH), rows(GLA_WIDTH), rows(D_MODEL), vec(), vec(), vec(), vec(),
                  full(nw), full(wo), full(ws1), full(ws3), full(ws2), full(wrt), full(eb)],
        out_specs=[rows(D_MODEL), rows(HALF), tok(), tok(), full(eb)],
        out_shape=[jax.ShapeDtypeStruct((t, D_MODEL), F32),
                   jax.ShapeDtypeStruct((t, HALF), jnp.uint32),
                   jax.ShapeDtypeStruct((TOP_K, t), jnp.int32),
                   jax.ShapeDtypeStruct((TOP_K, t), F32),
                   jax.ShapeDtypeStruct(eb.shape, F32)],
        scratch_shapes=[pltpu.VMEM(eb.shape, F32)],
        compiler_params=_params(1),
        name="outproj",
    )(oa, ob, x2, g1, sc, sh, g2, nw, wo, ws1, ws3, ws2, wrt, eb)


SLOT_CODE_SHIFT = 16
SLOT_CODE_BASE = 1 << SLOT_CODE_SHIFT


def _route_tile(s, eb, carry_ref):
    ne, nt = s.shape
    choice = s + eb
    gio = lax.broadcasted_iota(jnp.int32, (GROUP_SIZE, nt), 0)
    gscore = []
    for g in range(N_GROUPS):
        cg = choice[g * GROUP_SIZE:(g + 1) * GROUP_SIZE, :]
        top1 = jnp.max(cg, axis=0, keepdims=True)
        first = jnp.min(jnp.where(cg == top1, gio, GROUP_SIZE), axis=0, keepdims=True)
        top2 = jnp.max(jnp.where(gio == first, -jnp.inf, cg), axis=0, keepdims=True)
        gscore.append(top1 + top2)
    gs = jnp.concatenate(gscore, axis=0)
    gidx = lax.broadcasted_iota(jnp.int32, gs.shape, 0)
    beaten = jnp.zeros(gs.shape, jnp.int32)
    for m in range(N_GROUPS):
        gm = gs[m:m + 1, :]
        beaten = beaten + jnp.where((gm > gs) | ((gm == gs) & (gidx > m)), 1, 0)
    gkeep = beaten < TOPK_GROUPS
    masked = jnp.concatenate(
        [jnp.where(gkeep[g:g + 1, :], choice[g * GROUP_SIZE:(g + 1) * GROUP_SIZE, :], -jnp.inf)
         for g in range(N_GROUPS)], axis=0)

    eio = lax.broadcasted_iota(jnp.int32, (ne, nt), 0)
    candidates = masked
    idx_rows, w_rows, hits = [], [], []
    for _ in range(TOP_K):
        mx = jnp.max(masked, axis=0, keepdims=True)
        idx = jnp.min(jnp.where(masked == mx, eio, ne), axis=0, keepdims=True)
        hit = eio == idx
        w_rows.append(jnp.sum(jnp.where(hit, s, 0.0), axis=0, keepdims=True))
        idx_rows.append(idx)
        hits.append(hit)
        masked = jnp.where(hit, -jnp.inf, masked)
    picked = jnp.where(masked < candidates, 1.0, 0.0)
    wk = jnp.concatenate(w_rows, axis=0)
    weights = wk / jnp.sum(wk, axis=0, keepdims=True) * ROUTED_SCALE

    tr = lax.broadcasted_iota(jnp.int32, (nt, nt), 0)
    tc = lax.broadcasted_iota(jnp.int32, (nt, nt), 1)
    before = jnp.where(tr < tc, 1.0, 0.0).astype(BF16)
    pb = picked.astype(BF16)
    pos = carry_ref[...] + jnp.dot(pb, before, preferred_element_type=F32)
    rank = jnp.concatenate(
        [jnp.sum(jnp.where(hit, pos, 0.0), axis=0, keepdims=True) for hit in hits], axis=0).astype(jnp.int32)
    carry_ref[...] = carry_ref[...] + jnp.dot(pb, jnp.ones((nt, nt), BF16), preferred_element_type=F32)
    return jnp.concatenate(idx_rows, axis=0) * SLOT_CODE_BASE + rank, weights


SLOT_TILE = 2048


def _slots_kernel(pstart_ref, code_ref, o_ref):
    code = code_ref[...]
    expert = lax.shift_right_logical(code, SLOT_CODE_SHIFT)

    def body(e, acc):
        return jnp.where(expert == e, pstart_ref[e], acc)

    start = lax.fori_loop(0, N_EXPERTS, body, jnp.zeros_like(code), unroll=8)
    o_ref[...] = start + (code & (SLOT_CODE_BASE - 1))


def _slots(pstart, code_t):
    k, t = code_t.shape
    return pl.pallas_call(
        _slots_kernel,
        grid_spec=pltpu.PrefetchScalarGridSpec(
            num_scalar_prefetch=1,
            grid=(t // SLOT_TILE,),
            in_specs=[pl.BlockSpec((k, SLOT_TILE), lambda i, p: (0, i))],
            out_specs=pl.BlockSpec((k, SLOT_TILE), lambda i, p: (0, i)),
        ),
        out_shape=jax.ShapeDtypeStruct((k, t), jnp.int32),
        compiler_params=_params(1),
        name="slots",
    )(pstart, code_t)


SC_WINDOW = 128


def _sc_gather_rows(table, idx_flat):
    info = plsc.get_sparse_core_info()
    nw = info.num_cores * info.num_subcores
    n = idx_flat.shape[0]
    width = table.shape[1]
    per_worker = n // nw
    assert per_worker * nw == n and per_worker % SC_WINDOW == 0
    mesh = plsc.VectorSubcoreMesh(core_axis_name="c", subcore_axis_name="s")

    def body(table_hbm, idx_hbm, out_hbm, idx_v, rows_v, sem):
        wid = lax.axis_index("s") * info.num_cores + lax.axis_index("c")
        base = wid * per_worker

        @pl.loop(0, per_worker // SC_WINDOW)
        def _(w):
            off = pl.multiple_of(base + w * SC_WINDOW, SC_WINDOW)
            pltpu.sync_copy(idx_hbm.at[pl.ds(off, SC_WINDOW)], idx_v)
            pltpu.async_copy(table_hbm.at[idx_v], rows_v, sem).wait()
            pltpu.sync_copy(rows_v, out_hbm.at[pl.ds(off, SC_WINDOW)])

    return pl.kernel(
        body,
        out_type=jax.ShapeDtypeStruct((n, width), table.dtype),
        mesh=mesh,
        scratch_types=[pltpu.VMEM((SC_WINDOW,), jnp.int32),
                       pltpu.VMEM((SC_WINDOW, width), table.dtype),
                       pltpu.SemaphoreType.DMA],
        name="sc_gather",
    )(table, idx_flat)


def _sc_scatter_rows(rows, idx_kt, n_out):
    info = plsc.get_sparse_core_info()
    nw = info.num_cores * info.num_subcores
    t, width = rows.shape
    nk = idx_kt.shape[0]
    per_worker = t // nw
    assert per_worker * nw == t and per_worker % SC_WINDOW == 0
    mesh = plsc.VectorSubcoreMesh(core_axis_name="c", subcore_axis_name="s")

    def body(rows_hbm, idx_hbm, out_hbm, idx_v, rows_v, sem):
        wid = lax.axis_index("s") * info.num_cores + lax.axis_index("c")
        base = wid * per_worker

        @pl.loop(0, per_worker // SC_WINDOW)
        def _(w):
            off = pl.multiple_of(base + w * SC_WINDOW, SC_WINDOW)
            pltpu.sync_copy(rows_hbm.at[pl.ds(off, SC_WINDOW)], rows_v)
            pltpu.sync_copy(idx_hbm.at[:, pl.ds(off, SC_WINDOW)], idx_v)
            copies = [pltpu.async_copy(rows_v, out_hbm.at[idx_v.at[k]], sem) for k in range(nk)]
            for cp in copies:
                cp.wait()

    return pl.kernel(
        body,
        out_type=jax.ShapeDtypeStruct((n_out, width), rows.dtype),
        mesh=mesh,
        scratch_types=[pltpu.VMEM((nk, SC_WINDOW), jnp.int32),
                       pltpu.VMEM((SC_WINDOW, width), rows.dtype),
                       pltpu.SemaphoreType.DMA],
        name="sc_scatter",
    )(rows, idx_kt)


def _expert_kernel(first_ref, count_ref, used_ref, w1_ref, w3_ref, w2_ref, xs_ref, ys_ref,
                   xbuf, ybuf, sem_in, sem_out, w1f, w3f, w2f, sem_w, w1b, w3b, w2b, *, layer):
    e = pl.program_id(0)
    ne = pl.num_programs(0)
    r = xbuf.shape[1]
    n_used = used_ref[0]

    def x_copy(g, slot):
        return pltpu.make_async_copy(xs_ref.at[pl.ds(pl.multiple_of(g * r, r), r), :], xbuf.at[slot], sem_in.at[slot])

    def y_copy(g, slot):
        return pltpu.make_async_copy(ybuf.at[slot], ys_ref.at[pl.ds(pl.multiple_of(g * r, r), r), :], sem_out.at[slot])

    def w_copies(ex, slot):
        return [pltpu.make_async_copy(src.at[layer, ex], dst.at[slot], sem_w.at[slot])
                for src, dst in ((w1_ref, w1f), (w3_ref, w3f), (w2_ref, w2f))]

    nin = xbuf.shape[0]
    nout = ybuf.shape[0]

    nw = w1f.shape[0]

    @pl.when(e == 0)
    def _first_reads():
        for g in range(nin):
            @pl.when(g < n_used)
            def _(g=g):
                x_copy(g, g).start()
        for ex in range(nw - 1):
            for cp in w_copies(ex, ex):
                cp.start(priority=1)

    @pl.when(e + nw - 1 < ne)
    def _weights_ahead():
        for cp in w_copies(e + nw - 1, (e + nw - 1) % nw):
            cp.start(priority=1)

    wslot = e % nw
    for cp in w_copies(e, wslot):
        cp.wait()

    n = count_ref[e]

    @pl.when(n > 0)
    def _cast_weights():
        w1b[...] = w1f[wslot].astype(BF16)
        w3b[...] = w3f[wslot].astype(BF16)
        w2b[...] = w2f[wslot].astype(BF16)

    def run_tiles(g, count):
        tiles = [g + j for j in range(count)]
        for gj in tiles:
            x_copy(gj, gj % nin).wait()
        x = jnp.concatenate([xbuf[gj % nin] for gj in tiles], axis=0)
        for gj in tiles:
            @pl.when(gj + nin < n_used)
            def _(gj=gj):
                x_copy(gj + nin, gj % nin).start()
        lo, hi = _unpack_rows(x)
        lo, hi = lo.astype(BF16), hi.astype(BF16)
        a = (jnp.dot(lo, w1b[:HALF, :], preferred_element_type=F32)
             + jnp.dot(hi, w1b[HALF:, :], preferred_element_type=F32))
        u = (jnp.dot(lo, w3b[:HALF, :], preferred_element_type=F32)
             + jnp.dot(hi, w3b[HALF:, :], preferred_element_type=F32))
        y = _pack_rows(jnp.dot((_silu(a) * u).astype(BF16), w2b[...], preferred_element_type=F32))
        for j, gj in enumerate(tiles):
            @pl.when(gj >= nout)
            def _(gj=gj):
                y_copy(gj - nout, gj % nout).wait()
            ybuf[gj % nout] = y[j * r:(j + 1) * r]
            y_copy(gj, gj % nout).start()

    g0 = first_ref[e]
    pair = EXPERT_TILES_PER_MATMUL

    def pair_body(p, carry):
        run_tiles(g0 + p * pair, pair)
        return carry

    lax.fori_loop(0, n // pair, pair_body, 0)
    for left in range(1, pair):
        pl.when(n % pair == left)(functools.partial(run_tiles, g0 + n - left, left))

    @pl.when(e == ne - 1)
    def _drain_writes():
        for back in range(nout, 0, -1):
            @pl.when(n_used >= back)
            def _(back=back):
                y_copy(n_used - back, (n_used - back) % nout).wait()


def _experts(tile_first, tile_count, n_used, xs, w1, w3, w2, layer):
    n_rows = xs.shape[0]
    r = EXPERT_TILE
    any_spec = pl.BlockSpec(memory_space=pl.ANY)
    return pl.pallas_call(
        functools.partial(_expert_kernel, layer=layer),
        grid_spec=pltpu.PrefetchScalarGridSpec(
            num_scalar_prefetch=3,
            grid=(N_EXPERTS,),
            in_specs=[any_spec, any_spec, any_spec, any_spec],
            out_specs=any_spec,
            scratch_shapes=[pltpu.VMEM((EXPERT_IN_RING, r, HALF), jnp.uint32),
                            pltpu.VMEM((EXPERT_OUT_RING, r, HALF), jnp.uint32),
                            pltpu.SemaphoreType.DMA((EXPERT_IN_RING,)),
                            pltpu.SemaphoreType.DMA((EXPERT_OUT_RING,)),
                            pltpu.VMEM((EXPERT_WEIGHT_RING, D_MODEL, EXPERT_FF), F32),
                            pltpu.VMEM((EXPERT_WEIGHT_RING, D_MODEL, EXPERT_FF), F32),
                            pltpu.VMEM((EXPERT_WEIGHT_RING, EXPERT_FF, D_MODEL), F32),
                            pltpu.SemaphoreType.DMA((EXPERT_WEIGHT_RING,)),
                            pltpu.VMEM((D_MODEL, EXPERT_FF), BF16),
                            pltpu.VMEM((D_MODEL, EXPERT_FF), BF16),
                            pltpu.VMEM((EXPERT_FF, D_MODEL), BF16)],
        ),
        out_shape=jax.ShapeDtypeStruct((n_rows, HALF), jnp.uint32),
        compiler_params=_params(1),
        name="experts",
    )(tile_first, tile_count, n_used, w1, w3, w2, xs)


def _combine_dense_kernel(base_ref, g2_ref, w_ref, yg_ref, o_ref):
    acc_lo = acc_hi = None
    for k in range(TOP_K):
        lo, hi = _unpack_rows(yg_ref[k])
        wk = w_ref[:, k:k + 1]
        acc_lo = wk * lo if acc_lo is None else acc_lo + wk * lo
        acc_hi = wk * hi if acc_hi is None else acc_hi + wk * hi
    o_ref[:, :HALF] = base_ref[:, :HALF] + g2_ref[:, :HALF] * acc_lo
    o_ref[:, HALF:] = base_ref[:, HALF:] + g2_ref[:, HALF:] * acc_hi


def _combine_dense(base, g2, w_tok, yg, batch):
    t = base.shape[0]
    seq = yg.shape[1]
    nt = ROUTE_TILE
    tpb = seq // nt
    rows = lambda i: (batch * tpb + i, 0)
    return pl.pallas_call(
        _combine_dense_kernel,
        grid=(tpb,),
        in_specs=[pl.BlockSpec((nt, D_MODEL), rows),
                  pl.BlockSpec((None, 1, D_MODEL), lambda i: (batch, 0, 0)),
                  pl.BlockSpec((nt, TOP_K), rows),
                  pl.BlockSpec((TOP_K, nt, HALF), lambda i: (0, i, 0))],
        out_specs=pl.BlockSpec((nt, D_MODEL), rows),
        out_shape=jax.ShapeDtypeStruct((t, D_MODEL), F32),
        input_output_aliases={0: 0},
        compiler_params=_params(1),
        name="combine_dense",
    )(base, g2, w_tok, yg)


def _layer(layer, x, c, w_ada, b_ada, norm1_w, norm2_w, w_in, q_norm_w, k_norm_w, rel_bias, w_alpha, b_alpha,
           moba_out_w, gla_out_w, w_out, w_router, e_bias, w1, w3, w2, ws1, ws3, ws2):
    b, s, d = x.shape
    t = b * s
    x2 = x.reshape(t, d)

    mod = _mod(c, w_ada, b_ada)
    sh1, sc1, g1, sh2, sc2, g2 = [mod[:, j * d:(j + 1) * d].reshape(b, 1, d) for j in range(6)]

    w_main = w_in[:, :D_MAIN].astype(BF16)
    w_ga = jnp.zeros((d, LANES), BF16).at[:, :GLA_GATE_RANK].set(w_in[:, D_MAIN:].astype(BF16))
    per_chunk = 256 // MOBA_HEAD_DIM
    qw = jnp.tile(q_norm_w.astype(F32), per_chunk).reshape(1, 256) * (MOBA_HEAD_DIM ** -0.5 * LOG2E)
    kw = jnp.tile(k_norm_w.astype(F32), per_chunk).reshape(1, 256)
    proj, ga = _inproj(x2, sc1, sh1, norm1_w.reshape(1, d), w_main, w_ga, qw, kw, s)
    proj3 = proj.reshape(b, s, D_MAIN)

    near, far = _moba_bias_tables(rel_bias)
    ow = jnp.tile(moba_out_w.astype(F32), 2).reshape(1, LANES)
    o_a = _moba(proj3, near, far, ow)

    wal = jnp.zeros((LANES, GLA_KEY_WIDTH), F32).at[:GLA_GATE_RANK].set(w_alpha)
    o_b = _gla(proj3, ga.reshape(b, s, LANES), wal, b_alpha.reshape(1, GLA_KEY_WIDTH),
               gla_out_w.reshape(1, GLA_DV))

    eb = jnp.broadcast_to(e_bias.astype(F32)[:, None], (N_EXPERTS, ROUTE_TILE))
    base, h2, code_t, w_t, counts = _outproj(
        o_a.reshape(t, MOBA_WIDTH), o_b.reshape(t, GLA_WIDTH), x2, g1, sc2, sh2, g2,
        norm2_w.reshape(1, d), w_out.astype(BF16), ws1.astype(BF16), ws3.astype(BF16), ws2.astype(BF16),
        w_router.T.astype(BF16), eb, s)

    r = EXPERT_TILE
    n_tiles = (t * TOP_K + N_EXPERTS * (r - 1) + r - 1) // r
    n_rows = n_tiles * r
    cnt = counts[:, 0].astype(jnp.int32)
    padded = (cnt + r - 1) // r * r
    pend = jnp.cumsum(padded)
    pstart = pend - padded
    n_used = (pend[-1:] // r).astype(jnp.int32)
    dest_t = _slots(pstart, code_t)

    xs = _sc_scatter_rows(h2, dest_t, n_rows)
    ys = _experts(pstart // r, padded // r, n_used, xs, w1, w3, w2, layer)
    w_tok = w_t.T
    out = base
    for bi in range(b):
        idx = dest_t[:, bi * s:(bi + 1) * s].reshape(TOP_K * s)
        yg = _sc_gather_rows(ys, idx).reshape(TOP_K, s, HALF)
        out = _combine_dense(out, g2, w_tok, yg, bi)
    return out.reshape(b, s, d)


def kernel(x, c, w_ada, b_ada, norm1_w, norm2_w, w_in, q_norm_w, k_norm_w, rel_bias, w_alpha, b_alpha,
           moba_out_w, gla_out_w, w_out, w_router, e_bias, w1, w3, w2, ws1, ws3, ws2):
    for l in range(w_ada.shape[0]):
        x = _layer(l, x, c, w_ada[l], b_ada[l], norm1_w[l], norm2_w[l], w_in[l], q_norm_w[l], k_norm_w[l],
                   rel_bias, w_alpha[l], b_alpha[l], moba_out_w[l], gla_out_w[l], w_out[l], w_router[l],
                   e_bias[l], w1, w3, w2, ws1[l], ws3[l], ws2[l])
    return x
```

```python
import functools
import math

import numpy as np
import jax
import jax.numpy as jnp
from jax import lax
from jax.experimental import pallas as pl
from jax.experimental.pallas import tpu as pltpu
from jax.experimental.pallas import tpu_sc as plsc

D_MODEL = 1024
MOBA_HEADS = 8
MOBA_HEAD_DIM = 64
MOBA_WIDTH = MOBA_HEADS * MOBA_HEAD_DIM
MOBA_BLOCK = 256
MOBA_TOPK = 3
GLA_HEADS = 4
GLA_DK = 64
GLA_DV = 128
GLA_KEY_WIDTH = GLA_HEADS * GLA_DK
GLA_WIDTH = GLA_HEADS * GLA_DV
GLA_GATE_RANK = 16
GLA_GATE_TAU = 16.0
GLA_CHUNK = 64
REL_BUCKETS = 32
REL_MAX_DIST = 128
N_EXPERTS = 256
TOP_K = 8
N_GROUPS = 8
TOPK_GROUPS = 4
GROUP_SIZE = N_EXPERTS // N_GROUPS
EXPERT_FF = 256
SHARED_FF = 256
ROUTED_SCALE = 2.5
NORM_EPS = 1e-6
LOG2E = math.log2(math.e)

D_MAIN = 3 * MOBA_WIDTH + 2 * GLA_KEY_WIDTH + 2 * GLA_WIDTH
LANES = 128
SUBLANES = 8
MXU_COLS = 256
MOD_TILE = 1024
VMEM_LIMIT = 56 * 1024 * 1024

ROW_TILE = 512
ROUTE_TILE = 256
EXPERT_TILE = 128
EXPERT_TILES_PER_MATMUL = 6
EXPERT_IN_RING = 24
EXPERT_OUT_RING = 12
EXPERT_WEIGHT_RING = 4

F32 = jnp.float32
BF16 = jnp.bfloat16
NT_DIMS = (((1,), (1,)), ((), ()))
TN_DIMS = (((0,), (0,)), ((), ()))


def _params(n_axes):
    return pltpu.CompilerParams(dimension_semantics=("arbitrary",) * n_axes,
                                vmem_limit_bytes=VMEM_LIMIT)


def _silu(v):
    return v * jax.nn.sigmoid(v)


def _mod_kernel(c_ref, w_ref, b_ref, o_ref):
    o_ref[...] = jnp.dot(_silu(c_ref[...]), w_ref[...], preferred_element_type=F32) + b_ref[...]


def _mod(c, w, b):
    rows = SUBLANES
    cp = jnp.zeros((rows, D_MODEL), F32).at[:c.shape[0]].set(c)
    n = w.shape[1]
    tn = MOD_TILE
    out = pl.pallas_call(
        _mod_kernel,
        grid=(n // tn,),
        in_specs=[pl.BlockSpec((rows, D_MODEL), lambda j: (0, 0)),
                  pl.BlockSpec((D_MODEL, tn), lambda j: (0, j)),
                  pl.BlockSpec((1, tn), lambda j: (0, j))],
        out_specs=pl.BlockSpec((rows, tn), lambda j: (0, j)),
        out_shape=jax.ShapeDtypeStruct((rows, n), F32),
        compiler_params=_params(1),
        name="mod",
    )(cp, w, b.reshape(1, n))
    return out[:c.shape[0]]


def _group_rms_inv(a, group):
    lane = lax.broadcasted_iota(jnp.int32, (1, a.shape[1]), 1)
    a2 = a * a
    inv = jnp.zeros_like(a)
    for g in range(a.shape[1] // group):
        m = (lane >= g * group) & (lane < (g + 1) * group)
        ss = jnp.sum(jnp.where(m, a2, 0.0), axis=-1, keepdims=True)
        inv = jnp.where(m, lax.rsqrt(ss * (1.0 / group) + NORM_EPS), inv)
    return inv


def _inproj_kernel(x_ref, sc_ref, sh_ref, nw_ref, w_ref, wga_ref, qw_ref, kw_ref, o_ref, ga_ref):
    x = x_ref[...]
    ms = jnp.mean(x * x, axis=-1, keepdims=True)
    h = x * lax.rsqrt(ms + NORM_EPS) * nw_ref[...]
    h = h * (1.0 + sc_ref[...]) + sh_ref[...]
    hb = h.astype(BF16)
    cw = MXU_COLS
    for j in range(D_MAIN // cw):
        acc = jnp.dot(hb, w_ref[:, j * cw:(j + 1) * cw], preferred_element_type=F32)
        if j < 2 * MOBA_WIDTH // cw:
            nw = qw_ref if j < MOBA_WIDTH // cw else kw_ref
            acc = acc * _group_rms_inv(acc, MOBA_HEAD_DIM) * nw[...]
        o_ref[:, j * cw:(j + 1) * cw] = acc.astype(BF16)
    ga_ref[...] = jnp.dot(hb, wga_ref[...], preferred_element_type=F32)


def _inproj(x2, sc, sh, nw, w_main, w_ga, qw, kw, seq):
    t = x2.shape[0]
    tpb = seq // ROW_TILE
    vec = lambda: pl.BlockSpec((None, 1, D_MODEL), lambda i: (i // tpb, 0, 0))
    full = lambda a: pl.BlockSpec(a.shape, lambda i: (0,) * a.ndim)
    return pl.pallas_call(
        _inproj_kernel,
        grid=(t // ROW_TILE,),
        in_specs=[pl.BlockSpec((ROW_TILE, D_MODEL), lambda i: (i, 0)), vec(), vec(),
                  full(nw), full(w_main), full(w_ga), full(qw), full(kw)],
        out_specs=[pl.BlockSpec((ROW_TILE, D_MAIN), lambda i: (i, 0)),
                   pl.BlockSpec((ROW_TILE, LANES), lambda i: (i, 0))],
        out_shape=[jax.ShapeDtypeStruct((t, D_MAIN), BF16),
                   jax.ShapeDtypeStruct((t, LANES), F32)],
        compiler_params=_params(1),
        name="inproj",
    )(x2, sc, sh, nw, w_main, w_ga, qw, kw)


def _t5_bucket_np(rel):
    max_exact = REL_BUCKETS // 2
    relf = np.maximum(rel, 1).astype(np.float64)
    large = max_exact + (np.log(relf / max_exact) / math.log(REL_MAX_DIST / max_exact)
                         * (REL_BUCKETS - max_exact)).astype(np.int32)
    large = np.minimum(large, REL_BUCKETS - 1)
    return np.where(rel < max_exact, rel, large)


def _bias_kernel(rb_ref, idx_ref, o_ref):
    h = pl.program_id(0)
    idx = idx_ref[...]
    tab = jnp.full(idx.shape, -jnp.inf, F32)
    for bk in range(REL_BUCKETS):
        tab = jnp.where(idx == bk, rb_ref[bk * MOBA_HEADS + h], tab)
    o_ref[...] = tab


def _moba_bias_tables(rel_bias):
    j = np.arange(MOBA_BLOCK)[:, None]
    i = np.arange(MOBA_BLOCK)[None, :]
    own_idx = np.where(j <= i, _t5_bucket_np(np.maximum(i - j, 0)), -1)
    prev_idx = _t5_bucket_np(MOBA_BLOCK + i - j)
    idx = jnp.asarray(np.concatenate([prev_idx, own_idx], axis=0).astype(np.int32))
    assert int(_t5_bucket_np(np.array([MOBA_BLOCK + 1]))[0]) == REL_BUCKETS - 1
    rb = rel_bias.astype(F32) * LOG2E
    near = pl.pallas_call(
        _bias_kernel,
        grid=(MOBA_HEADS,),
        in_specs=[pl.BlockSpec(memory_space=pltpu.SMEM),
                  pl.BlockSpec(idx.shape, lambda h: (0, 0))],
        out_specs=pl.BlockSpec((None,) + idx.shape, lambda h: (h, 0, 0)),
        out_shape=jax.ShapeDtypeStruct((MOBA_HEADS,) + idx.shape, F32),
        compiler_params=_params(1),
        name="bias",
    )(rb.reshape(-1), idx)
    return near, rb[REL_BUCKETS - 1]


FAR_GROUP = 4


MOBA_INTERLEAVE = 4


def _moba_kernel(*refs):
    hp = pl.program_id(1)
    nsets = refs[2].shape[0] // (MOBA_INTERLEAVE * MOBA_BLOCK)
    _moba_body(None, hp, *refs, prepare=True)

    def block_set(j, carry):
        for jj in range(nsets):
            blocks = list(range(jj * MOBA_INTERLEAVE, (jj + 1) * MOBA_INTERLEAVE))
            pl.when(j == jj)(functools.partial(_moba_body, blocks, hp, *refs, prepare=False))
        return carry

    lax.fori_loop(0, nsets, block_set, 0)


def _moba_body(blocks, hp, far_ref, q_ref, k_ref, v_ref, near_ref, ow_ref, o_ref,
               vt_ref, vtg_ref, acc_ref, m_ref, sel_ref, s_ref, mx_ref, *, prepare):
    nblk = k_ref.shape[0] // MOBA_BLOCK
    ngrp = nblk // FAR_GROUP
    hd = MOBA_HEAD_DIM
    bs = MOBA_BLOCK
    lane = lax.broadcasted_iota(jnp.int32, (bs, LANES), 1)

    def split_heads(qb):
        zero = jnp.zeros_like(qb)
        return jnp.where(lane < hd, qb, zero), jnp.where(lane < hd, zero, qb)

    def _prepare():
        row = lax.broadcasted_iota(jnp.int32, (LANES, bs), 0)
        kmeans = []
        for n in range(nblk):
            kb = k_ref[n * bs:(n + 1) * bs, :].astype(F32)
            kmeans.append(jnp.mean(kb, axis=0, keepdims=True))
            vt = v_ref[n * bs:(n + 1) * bs, :].astype(F32).T
            vt0 = jnp.where(row < hd, vt, 1.0).astype(BF16)
            vt1 = jnp.where(row < hd, 1.0, vt).astype(BF16)
            vt_ref[0, n] = vt0
            vt_ref[1, n] = vt1
            gcols = slice((n % FAR_GROUP) * bs, (n % FAR_GROUP + 1) * bs)
            vtg_ref[0, n // FAR_GROUP, :, gcols] = vt0
            vtg_ref[1, n // FAR_GROUP, :, gcols] = vt1
        kmean = jnp.concatenate(kmeans, axis=0)
        km_hi = kmean.astype(BF16)
        km_lo = (kmean - km_hi.astype(F32)).astype(BF16)
        blk = lax.broadcasted_iota(jnp.int32, (nblk, bs), 0)
        for ib in range(nblk):
            qparts = split_heads(q_ref[ib * bs:(ib + 1) * bs, :])
            for h in range(2):
                gt = (lax.dot_general(km_hi, qparts[h], NT_DIMS, preferred_element_type=F32)
                      + lax.dot_general(km_lo, qparts[h], NT_DIMS, preferred_element_type=F32))
                gt = jnp.where(blk < ib, gt, -jnp.inf)
                cnt = jnp.zeros(gt.shape, jnp.int32)
                for m in range(ib):
                    gm = gt[m:m + 1, :]
                    cnt = cnt + jnp.where((gm > gt) | ((gm == gt) & (blk > m)), 1, 0)
                keep = (blk < ib) & (cnt < MOBA_TOPK)
                sel_ref[0, h, ib] = jnp.where(keep, 1.0, 0.0)
                sel_ref[1, h, ib] = jnp.where(keep & (blk < ib - 1), 1.0, 0.0)

    if prepare:
        _prepare()
        return

    gk = FAR_GROUP * bs

    def rows_of(ib):
        if isinstance(ib, int):
            return slice(ib * bs, (ib + 1) * bs)
        return pl.ds(pl.multiple_of(ib * bs, bs), bs)

    def finish(ib, slot):
        a0 = acc_ref[slot, 0]
        a1 = acc_ref[slot, 1]
        row = lax.broadcasted_iota(jnp.int32, a0.shape, 0)
        ot = jnp.where(row < hd, a0 / a0[hd:hd + 1, :], a1 / a1[0:1, :])
        o2 = ot * ot
        ss0 = jnp.sum(jnp.where(row < hd, o2, 0.0), axis=0, keepdims=True)
        ss1 = jnp.sum(jnp.where(row < hd, 0.0, o2), axis=0, keepdims=True)
        inv = jnp.where(row < hd, lax.rsqrt(ss0 * (1.0 / hd) + NORM_EPS), lax.rsqrt(ss1 * (1.0 / hd) + NORM_EPS))
        o_ref[rows_of(ib), :] = ((ot * inv).T * ow_ref[...]).astype(o_ref.dtype)

    def far_scores(g, slot, qh, nb):
        kb = k_ref[g * gk:g * gk + nb * bs, :]
        for h in range(2):
            s = lax.dot_general(kb, qh[h], NT_DIMS, preferred_element_type=F32)
            s_ref[slot, g % 2, h, 0:nb * bs, :] = s
            for j in range(nb):
                mx_ref[slot, g % 2, h, j] = jnp.max(s[j * bs:(j + 1) * bs], axis=0, keepdims=True)

    def near_scores(ib, qh):
        kbs = (k_ref[rows_of(ib - 1), :], k_ref[rows_of(ib), :])
        return [[lax.dot_general(kbs[w], qh[h], NT_DIMS, preferred_element_type=F32)
                 + near_ref[h, w * bs:(w + 1) * bs, :] for w in range(2)] for h in range(2)]

    def near_values(ib, slot, ss):
        ps, ms = [], []
        for h in range(2):
            s_prev, s_own = ss[h]
            keep = sel_ref[0, h, ib, pl.ds(ib - 1, 1), :] > 0.5
            mx = jnp.where(keep, jnp.max(s_prev, axis=0, keepdims=True), -jnp.inf)
            m_new = jnp.maximum(jnp.max(s_own, axis=0, keepdims=True), mx)
            ps.append((jnp.exp2(s_prev - jnp.where(keep, m_new, jnp.inf)).astype(BF16),
                       jnp.exp2(s_own - m_new).astype(BF16)))
            ms.append(m_new)
        for h in range(2):
            acc_ref[slot, h] = (jnp.dot(vt_ref[h, ib - 1], ps[h][0], preferred_element_type=F32)
                                + jnp.dot(vt_ref[h, ib], ps[h][1], preferred_element_type=F32))
            m_ref[slot, h] = ms[h]

    def far_group(g, ib, slot, nb):
        for h in range(2):
            fb = far_ref[2 * hp + h]
            m_old = m_ref[slot, h]
            m_new = m_old
            keeps = []
            for j in range(nb):
                keep = sel_ref[1, h, ib, pl.ds(g * FAR_GROUP + j, 1), :] > 0.5
                m_new = jnp.maximum(m_new, jnp.where(keep, mx_ref[slot, g % 2, h, j] + fb, -jnp.inf))
                keeps.append(keep)
            p = jnp.concatenate(
                [jnp.exp2(s_ref[slot, g % 2, h, j * bs:(j + 1) * bs, :]
                          - jnp.where(keeps[j], m_new - fb, jnp.inf)).astype(BF16)
                 for j in range(nb)], axis=0)
            pv = jnp.dot(vtg_ref[h, g, :, 0:nb * bs], p, preferred_element_type=F32)
            acc_ref[slot, h] = acc_ref[slot, h] * jnp.exp2(m_old - m_new) + pv
            m_ref[slot, h] = m_new

    def step_body(blocks):
        slots = list(enumerate(blocks))
        nbs = [[min(FAR_GROUP, ib - 1 - g * FAR_GROUP) for g in range((ib - 2) // FAR_GROUP + 1)] if ib >= 2 else []
               for ib in blocks]
        qhs = [split_heads(q_ref[rows_of(ib), :]) for ib in blocks]
        near = [near_scores(ib, qhs[slot]) for slot, ib in slots]
        for slot, ib in slots:
            if nbs[slot]:
                far_scores(0, slot, qhs[slot], nbs[slot][0])
        for slot, ib in slots:
            near_values(ib, slot, near[slot])
        for g in range(max(len(n) for n in nbs)):
            for slot, ib in slots:
                if g + 1 < len(nbs[slot]):
                    far_scores(g + 1, slot, qhs[slot], nbs[slot][g + 1])
            for slot, ib in slots:
                if g < len(nbs[slot]):
                    far_group(g, ib, slot, nbs[slot][g])
        for slot, ib in slots:
            finish(ib, slot)

    if blocks[0] == 0:
        qh = split_heads(q_ref[0:bs, :])
        kb = k_ref[0:bs, :]
        for h in range(2):
            s = lax.dot_general(kb, qh[h], NT_DIMS, preferred_element_type=F32) + near_ref[h, bs:2 * bs, :]
            p = jnp.exp2(s - jnp.max(s, axis=0, keepdims=True)).astype(BF16)
            acc_ref[MOBA_INTERLEAVE - 1, h] = jnp.dot(vt_ref[h, 0], p, preferred_element_type=F32)
        finish(0, MOBA_INTERLEAVE - 1)
        blocks = blocks[1:]
    step_body(blocks)


def _moba(proj3, near, far, ow):
    b, s, _ = proj3.shape
    nblk = s // MOBA_BLOCK
    assert nblk % FAR_GROUP == 0
    npair = MOBA_HEADS // 2
    kcol = MOBA_WIDTH // LANES
    return pl.pallas_call(
        _moba_kernel,
        grid=(b, npair),
        in_specs=[pl.BlockSpec(memory_space=pltpu.SMEM),
                  pl.BlockSpec((None, s, LANES), lambda bb, hp: (bb, 0, hp)),
                  pl.BlockSpec((None, s, LANES), lambda bb, hp: (bb, 0, kcol + hp)),
                  pl.BlockSpec((None, s, LANES), lambda bb, hp: (bb, 0, 2 * kcol + hp)),
                  pl.BlockSpec((2, 2 * MOBA_BLOCK, MOBA_BLOCK), lambda bb, hp: (hp, 0, 0)),
                  pl.BlockSpec((1, LANES), lambda bb, hp: (0, 0))],
        out_specs=pl.BlockSpec((None, s, LANES), lambda bb, hp: (bb, 0, hp)),
        out_shape=jax.ShapeDtypeStruct((b, s, MOBA_WIDTH), BF16),
        scratch_shapes=[pltpu.VMEM((2, nblk, LANES, MOBA_BLOCK), BF16),
                        pltpu.VMEM((2, nblk // FAR_GROUP, LANES, FAR_GROUP * MOBA_BLOCK), BF16),
                        pltpu.VMEM((MOBA_INTERLEAVE, 2, LANES, MOBA_BLOCK), F32),
                        pltpu.VMEM((MOBA_INTERLEAVE, 2, 1, MOBA_BLOCK), F32),
                        pltpu.VMEM((2, 2, nblk, nblk, MOBA_BLOCK), F32),
                        pltpu.VMEM((MOBA_INTERLEAVE, 2, 2, FAR_GROUP * MOBA_BLOCK, MOBA_BLOCK), F32),
                        pltpu.VMEM((MOBA_INTERLEAVE, 2, 2, FAR_GROUP, 1, MOBA_BLOCK), F32)],
        compiler_params=_params(2),
        name="moba",
    )(far, proj3, proj3, proj3, near, ow)


def _split3(v):
    hi = v.astype(BF16)
    r1 = v - hi.astype(F32)
    mid = r1.astype(BF16)
    lo = (r1 - mid.astype(F32)).astype(BF16)
    return hi, mid, lo


GLA_UNROLL = 16


def _gla_kernel(q_ref, k_ref, v_ref, g_ref, ga_ref, wal_ref, bal_ref, gw_ref, o_ref, b_ref, st_ref):
    seq = q_ref.shape[0]
    c = GLA_CHUNK
    pc = 256

    rr = lax.broadcasted_iota(jnp.int32, (pc, pc), 0)
    cc = lax.broadcasted_iota(jnp.int32, (pc, pc), 1)
    tri = jnp.where((rr >= cc) & (rr // c == cc // c), 1.0, 0.0).astype(BF16)

    def decay_body(j, carry):
        rows = [pl.ds(pl.multiple_of((j * GLA_UNROLL + u) * pc, pc), pc) for u in range(GLA_UNROLL)]
        xg = [jnp.dot(ga_ref[r, :], wal_ref[...], preferred_element_type=F32) + bal_ref[...] for r in rows]
        parts = [_split3((jnp.minimum(x, 0.0) - jnp.log(1.0 + jnp.exp(-jnp.abs(x)))) * (1.0 / GLA_GATE_TAU))
                 for x in xg]
        sums = [[jnp.dot(tri, term, preferred_element_type=F32) for term in p] for p in parts]
        for r, (hi, mid, lo) in zip(rows, sums):
            b_ref[r, :] = hi + mid + lo
        return carry

    lax.fori_loop(0, seq // (pc * GLA_UNROLL), decay_body, 0)

    st_ref[...] = jnp.zeros_like(st_ref)
    lane = lax.broadcasted_iota(jnp.int32, (c, LANES), 1)
    head_mask = (lane < GLA_DK, lane >= GLA_DK)
    causal = lax.broadcasted_iota(jnp.int32, (c, c), 0) >= lax.broadcasted_iota(jnp.int32, (c, c), 1)

    units = [(u, h) for u in range(GLA_UNROLL) for h in range(2)]

    def chunk_body(ci, carry):
        rows = [pl.ds(pl.multiple_of((ci * GLA_UNROLL + u) * c, c), c) for u in range(GLA_UNROLL)]
        qt, kt, qs, ke, e_last = [], [], [], [], []
        for u in range(GLA_UNROLL):
            b = b_ref[rows[u], :]
            ref_row = b[c // 2 - 1:c // 2, :]
            last = b[c - 1:c, :]
            q = q_ref[rows[u], :].astype(F32) * (GLA_DK ** -0.5)
            k = k_ref[rows[u], :].astype(F32)
            qt.append(q * jnp.exp(b - ref_row))
            kt.append((k * jnp.exp(ref_row - b)).astype(BF16))
            qs.append(q * jnp.exp(b))
            ke.append((k * jnp.exp(last - b)).astype(BF16))
            e_last.append(jnp.exp(last))
        vs = {(u, h): v_ref[rows[u], h * GLA_DV:(h + 1) * GLA_DV] for u, h in units}
        a = {(u, h): lax.dot_general(jnp.where(head_mask[h], qt[u], 0.0).astype(BF16), kt[u], NT_DIMS,
                                     preferred_element_type=F32) for u, h in units}
        inc = {(u, h): lax.dot_general(vs[u, h], ke[u], TN_DIMS, preferred_element_type=F32) for u, h in units}
        o = {(u, h): jnp.dot(jnp.where(causal, a[u, h], 0.0).astype(BF16), vs[u, h], preferred_element_type=F32)
             for u, h in units}
        states = {}
        for h in range(2):
            st = st_ref[h]
            for u in range(GLA_UNROLL):
                states[u, h] = st
                st = st * e_last[u] + inc[u, h]
            st_ref[h] = st
        for u, h in units:
            cols = slice(h * GLA_DV, (h + 1) * GLA_DV)
            ou = o[u, h] + lax.dot_general(jnp.where(head_mask[h], qs[u], 0.0).astype(BF16),
                                           states[u, h].astype(BF16), NT_DIMS, preferred_element_type=F32)
            ms = jnp.mean(ou * ou, axis=-1, keepdims=True)
            on = ou * lax.rsqrt(ms + NORM_EPS) * gw_ref[...]
            g = g_ref[rows[u], cols].astype(F32)
            o_ref[rows[u], cols] = (on * _silu(g)).astype(o_ref.dtype)
        return carry

    lax.fori_loop(0, seq // (c * GLA_UNROLL), chunk_body, 0)


def _gla(proj3, ga3, wal, bal, gw):
    b, s, _ = proj3.shape
    npair = GLA_HEADS // 2
    qcol = 3 * MOBA_WIDTH // LANES
    kcol = qcol + GLA_KEY_WIDTH // LANES
    vcol = (3 * MOBA_WIDTH + 2 * GLA_KEY_WIDTH) // (2 * GLA_DV)
    gcol = vcol + npair
    return pl.pallas_call(
        _gla_kernel,
        grid=(b, npair),
        in_specs=[pl.BlockSpec((None, s, LANES), lambda bb, hp: (bb, 0, qcol + hp)),
                  pl.BlockSpec((None, s, LANES), lambda bb, hp: (bb, 0, kcol + hp)),
                  pl.BlockSpec((None, s, 2 * GLA_DV), lambda bb, hp: (bb, 0, vcol + hp)),
                  pl.BlockSpec((None, s, 2 * GLA_DV), lambda bb, hp: (bb, 0, gcol + hp)),
                  pl.BlockSpec((None, s, LANES), lambda bb, hp: (bb, 0, 0)),
                  pl.BlockSpec((LANES, LANES), lambda bb, hp: (0, hp)),
                  pl.BlockSpec((1, LANES), lambda bb, hp: (0, hp)),
                  pl.BlockSpec((1, GLA_DV), lambda bb, hp: (0, 0))],
        out_specs=pl.BlockSpec((None, s, 2 * GLA_DV), lambda bb, hp: (bb, 0, hp)),
        out_shape=jax.ShapeDtypeStruct((b, s, GLA_WIDTH), BF16),
        scratch_shapes=[pltpu.VMEM((s, LANES), F32),
                        pltpu.VMEM((2, GLA_DV, LANES), F32)],
        compiler_params=_params(2),
        name="gla",
    )(proj3, proj3, proj3, proj3, ga3, wal, bal, gw)


HALF = D_MODEL // 2


def _pack_rows(v):
    return pltpu.pack_elementwise([v[:, :HALF], v[:, HALF:]], packed_dtype=BF16)


def _unpack_rows(w):
    return (pltpu.unpack_elementwise(w, index=0, packed_dtype=BF16, unpacked_dtype=F32),
            pltpu.unpack_elementwise(w, index=1, packed_dtype=BF16, unpacked_dtype=F32))


def _outproj_kernel(oa_ref, ob_ref, x_ref, g1_ref, sc_ref, sh_ref, g2_ref, nw_ref, wo_ref,
                    ws1_ref, ws3_ref, ws2_ref, wrt_ref, eb_ref, base_ref, h_ref, code_ref, w_ref, cnt_ref,
                    carry_ref):
    @pl.when(pl.program_id(0) == 0)
    def _init():
        carry_ref[...] = jnp.zeros_like(carry_ref)

    mix = (jnp.dot(oa_ref[...], wo_ref[:MOBA_WIDTH, :], preferred_element_type=F32)
           + jnp.dot(ob_ref[...], wo_ref[MOBA_WIDTH:, :], preferred_element_type=F32))
    x1 = x_ref[...] + g1_ref[...] * mix
    ms = jnp.mean(x1 * x1, axis=-1, keepdims=True)
    h = x1 * lax.rsqrt(ms + NORM_EPS) * nw_ref[...]
    h = h * (1.0 + sc_ref[...]) + sh_ref[...]
    h_ref[...] = _pack_rows(h)
    hb = h.astype(BF16)
    logits_t = lax.dot_general(wrt_ref[...], hb, NT_DIMS, preferred_element_type=F32)
    scores = jax.nn.sigmoid(logits_t)
    a = jnp.dot(hb, ws1_ref[...], preferred_element_type=F32)
    u = jnp.dot(hb, ws3_ref[...], preferred_element_type=F32)
    shared = jnp.dot((_silu(a) * u).astype(BF16), ws2_ref[...], preferred_element_type=F32)
    base_ref[...] = x1 + g2_ref[...] * shared
    nt = eb_ref.shape[1]
    for part in range(scores.shape[1] // nt):
        cols = slice(part * nt, (part + 1) * nt)
        code, weights = _route_tile(scores[:, cols], eb_ref[...], carry_ref)
        code_ref[:, cols] = code
        w_ref[:, cols] = weights
    cnt_ref[...] = carry_ref[...]


def _outproj(oa, ob, x2, g1, sc, sh, g2, nw, wo, ws1, ws3, ws2, wrt, eb, seq):
    t = x2.shape[0]
    assert t <= SLOT_CODE_BASE
    tpb = seq // ROW_TILE
    vec = lambda: pl.BlockSpec((None, 1, D_MODEL), lambda i: (i // tpb, 0, 0))
    full = lambda a: pl.BlockSpec(a.shape, lambda i: (0,) * a.ndim)
    rows = lambda w: pl.BlockSpec((ROW_TILE, w), lambda i: (i, 0))
    tok = lambda: pl.BlockSpec((TOP_K, ROW_TILE), lambda i: (0, i))
    return pl.pallas_call(
        _outproj_kernel,
        grid=(t // ROW_TILE,),
        in_specs=[rows(MOBA_WIDTH), rows(GLA_WIDTH), rows(D_MODEL), vec(), vec(), vec(), vec(),
                  full(nw), full(wo), full(ws1), full(ws3), full(ws2), full(wrt), full(eb)],
        out_specs=[rows(D_MODEL), rows(HALF), tok(), tok(), full(eb)],
        out_shape=[jax.ShapeDtypeStruct((t, D_MODEL), F32),
                   jax.ShapeDtypeStruct((t, HALF), jnp.uint32),
                   jax.ShapeDtypeStruct((TOP_K, t), jnp.int32),
                   jax.ShapeDtypeStruct((TOP_K, t), F32),
                   jax.ShapeDtypeStruct(eb.shape, F32)],
        scratch_shapes=[pltpu.VMEM(eb.shape, F32)],
        compiler_params=_params(1),
        name="outproj",
    )(oa, ob, x2, g1, sc, sh, g2, nw, wo, ws1, ws3, ws2, wrt, eb)


SLOT_CODE_SHIFT = 16
SLOT_CODE_BASE = 1 << SLOT_CODE_SHIFT


def _route_tile(s, eb, carry_ref):
    ne, nt = s.shape
    choice = s + eb
    gio = lax.broadcasted_iota(jnp.int32, (GROUP_SIZE, nt), 0)
    gscore = []
    for g in range(N_GROUPS):
        cg = choice[g * GROUP_SIZE:(g + 1) * GROUP_SIZE, :]
        top1 = jnp.max(cg, axis=0, keepdims=True)
        first = jnp.min(jnp.where(cg == top1, gio, GROUP_SIZE), axis=0, keepdims=True)
        top2 = jnp.max(jnp.where(gio == first, -jnp.inf, cg), axis=0, keepdims=True)
        gscore.append(top1 + top2)
    gs = jnp.concatenate(gscore, axis=0)
    gidx = lax.broadcasted_iota(jnp.int32, gs.shape, 0)
    beaten = jnp.zeros(gs.shape, jnp.int32)
    for m in range(N_GROUPS):
        gm = gs[m:m + 1, :]
        beaten = beaten + jnp.where((gm > gs) | ((gm == gs) & (gidx > m)), 1, 0)
    gkeep = beaten < TOPK_GROUPS
    masked = jnp.concatenate(
        [jnp.where(gkeep[g:g + 1, :], choice[g * GROUP_SIZE:(g + 1) * GROUP_SIZE, :], -jnp.inf)
         for g in range(N_GROUPS)], axis=0)

    eio = lax.broadcasted_iota(jnp.int32, (ne, nt), 0)
    candidates = masked
    idx_rows, w_rows, hits = [], [], []
    for _ in range(TOP_K):
        mx = jnp.max(masked, axis=0, keepdims=True)
        idx = jnp.min(jnp.where(masked == mx, eio, ne), axis=0, keepdims=True)
        hit = eio == idx
        w_rows.append(jnp.sum(jnp.where(hit, s, 0.0), axis=0, keepdims=True))
        idx_rows.append(idx)
        hits.append(hit)
        masked = jnp.where(hit, -jnp.inf, masked)
    picked = jnp.where(masked < candidates, 1.0, 0.0)
    wk = jnp.concatenate(w_rows, axis=0)
    weights = wk / jnp.sum(wk, axis=0, keepdims=True) * ROUTED_SCALE

    tr = lax.broadcasted_iota(jnp.int32, (nt, nt), 0)
    tc = lax.broadcasted_iota(jnp.int32, (nt, nt), 1)
    before = jnp.where(tr < tc, 1.0, 0.0).astype(BF16)
    pb = picked.astype(BF16)
    pos = carry_ref[...] + jnp.dot(pb, before, preferred_element_type=F32)
    rank = jnp.concatenate(
        [jnp.sum(jnp.where(hit, pos, 0.0), axis=0, keepdims=True) for hit in hits], axis=0).astype(jnp.int32)
    carry_ref[...] = carry_ref[...] + jnp.dot(pb, jnp.ones((nt, nt), BF16), preferred_element_type=F32)
    return jnp.concatenate(idx_rows, axis=0) * SLOT_CODE_BASE + rank, weights


SLOT_TILE = 2048


def _slots_kernel(pstart_ref, code_ref, o_ref):
    code = code_ref[...]
    expert = lax.shift_right_logical(code, SLOT_CODE_SHIFT)

    def body(e, acc):
        return jnp.where(expert == e, pstart_ref[e], acc)

    start = lax.fori_loop(0, N_EXPERTS, body, jnp.zeros_like(code), unroll=8)
    o_ref[...] = start + (code & (SLOT_CODE_BASE - 1))


def _slots(pstart, code_t):
    k, t = code_t.shape
    return pl.pallas_call(
        _slots_kernel,
        grid_spec=pltpu.PrefetchScalarGridSpec(
            num_scalar_prefetch=1,
            grid=(t // SLOT_TILE,),
            in_specs=[pl.BlockSpec((k, SLOT_TILE), lambda i, p: (0, i))],
            out_specs=pl.BlockSpec((k, SLOT_TILE), lambda i, p: (0, i)),
        ),
        out_shape=jax.ShapeDtypeStruct((k, t), jnp.int32),
        compiler_params=_params(1),
        name="slots",
    )(pstart, code_t)


SC_WINDOW = 128


def _sc_gather_rows(table, idx_flat):
    info = plsc.get_sparse_core_info()
    nw = info.num_cores * info.num_subcores
    n = idx_flat.shape[0]
    width = table.shape[1]
    per_worker = n // nw
    assert per_worker * nw == n and per_worker % SC_WINDOW == 0
    mesh = plsc.VectorSubcoreMesh(core_axis_name="c", subcore_axis_name="s")

    def body(table_hbm, idx_hbm, out_hbm, idx_v, rows_v, sem):
        wid = lax.axis_index("s") * info.num_cores + lax.axis_index("c")
        base = wid * per_worker

        @pl.loop(0, per_worker // SC_WINDOW)
        def _(w):
            off = pl.multiple_of(base + w * SC_WINDOW, SC_WINDOW)
            pltpu.sync_copy(idx_hbm.at[pl.ds(off, SC_WINDOW)], idx_v)
            pltpu.async_copy(table_hbm.at[idx_v], rows_v, sem).wait()
            pltpu.sync_copy(rows_v, out_hbm.at[pl.ds(off, SC_WINDOW)])

    return pl.kernel(
        body,
        out_type=jax.ShapeDtypeStruct((n, width), table.dtype),
        mesh=mesh,
        scratch_types=[pltpu.VMEM((SC_WINDOW,), jnp.int32),
                       pltpu.VMEM((SC_WINDOW, width), table.dtype),
                       pltpu.SemaphoreType.DMA],
        name="sc_gather",
    )(table, idx_flat)


def _sc_scatter_rows(rows, idx_kt, n_out):
    info = plsc.get_sparse_core_info()
    nw = info.num_cores * info.num_subcores
    t, width = rows.shape
    nk = idx_kt.shape[0]
    per_worker = t // nw
    assert per_worker * nw == t and per_worker % SC_WINDOW == 0
    mesh = plsc.VectorSubcoreMesh(core_axis_name="c", subcore_axis_name="s")

    def body(rows_hbm, idx_hbm, out_hbm, idx_v, rows_v, sem):
        wid = lax.axis_index("s") * info.num_cores + lax.axis_index("c")
        base = wid * per_worker

        @pl.loop(0, per_worker // SC_WINDOW)
        def _(w):
            off = pl.multiple_of(base + w * SC_WINDOW, SC_WINDOW)
            pltpu.sync_copy(rows_hbm.at[pl.ds(off, SC_WINDOW)], rows_v)
            pltpu.sync_copy(idx_hbm.at[:, pl.ds(off, SC_WINDOW)], idx_v)
            copies = [pltpu.async_copy(rows_v, out_hbm.at[idx_v.at[k]], sem) for k in range(nk)]
            for cp in copies:
                cp.wait()

    return pl.kernel(
        body,
        out_type=jax.ShapeDtypeStruct((n_out, width), rows.dtype),
        mesh=mesh,
        scratch_types=[pltpu.VMEM((nk, SC_WINDOW), jnp.int32),
                       pltpu.VMEM((SC_WINDOW, width), rows.dtype),
                       pltpu.SemaphoreType.DMA],
        name="sc_scatter",
    )(rows, idx_kt)


def _expert_kernel(first_ref, count_ref, used_ref, w1_ref, w3_ref, w2_ref, xs_ref, ys_ref,
                   xbuf, ybuf, sem_in, sem_out, w1f, w3f, w2f, sem_w, w1b, w3b, w2b, *, layer):
    e = pl.program_id(0)
    ne = pl.num_programs(0)
    r = xbuf.shape[1]
    n_used = used_ref[0]

    def x_copy(g, slot):
        return pltpu.make_async_copy(xs_ref.at[pl.ds(pl.multiple_of(g * r, r), r), :], xbuf.at[slot], sem_in.at[slot])

    def y_copy(g, slot):
        return pltpu.make_async_copy(ybuf.at[slot], ys_ref.at[pl.ds(pl.multiple_of(g * r, r), r), :], sem_out.at[slot])

    def w_copies(ex, slot):
        return [pltpu.make_async_copy(src.at[layer, ex], dst.at[slot], sem_w.at[slot])
                for src, dst in ((w1_ref, w1f), (w3_ref, w3f), (w2_ref, w2f))]

    nin = xbuf.shape[0]
    nout = ybuf.shape[0]

    nw = w1f.shape[0]

    @pl.when(e == 0)
    def _first_reads():
        for g in range(nin):
            @pl.when(g < n_used)
            def _(g=g):
                x_copy(g, g).start()
        for ex in range(nw - 1):
            for cp in w_copies(ex, ex):
                cp.start(priority=1)

    @pl.when(e + nw - 1 < ne)
    def _weights_ahead():
        for cp in w_copies(e + nw - 1, (e + nw - 1) % nw):
            cp.start(priority=1)

    wslot = e % nw
    for cp in w_copies(e, wslot):
        cp.wait()

    n = count_ref[e]

    @pl.when(n > 0)
    def _cast_weights():
        w1b[...] = w1f[wslot].astype(BF16)
        w3b[...] = w3f[wslot].astype(BF16)
        w2b[...] = w2f[wslot].astype(BF16)

    def run_tiles(g, count):
        tiles = [g + j for j in range(count)]
        for gj in tiles:
            x_copy(gj, gj % nin).wait()
        x = jnp.concatenate([xbuf[gj % nin] for gj in tiles], axis=0)
        for gj in tiles:
            @pl.when(gj + nin < n_used)
            def _(gj=gj):
                x_copy(gj + nin, gj % nin).start()
        lo, hi = _unpack_rows(x)
        lo, hi = lo.astype(BF16), hi.astype(BF16)
        a = (jnp.dot(lo, w1b[:HALF, :], preferred_element_type=F32)
             + jnp.dot(hi, w1b[HALF:, :], preferred_element_type=F32))
        u = (jnp.dot(lo, w3b[:HALF, :], preferred_element_type=F32)
             + jnp.dot(hi, w3b[HALF:, :], preferred_element_type=F32))
        y = _pack_rows(jnp.dot((_silu(a) * u).astype(BF16), w2b[...], preferred_element_type=F32))
        for j, gj in enumerate(tiles):
            @pl.when(gj >= nout)
            def _(gj=gj):
                y_copy(gj - nout, gj % nout).wait()
            ybuf[gj % nout] = y[j * r:(j + 1) * r]
            y_copy(gj, gj % nout).start()

    g0 = first_ref[e]
    pair = EXPERT_TILES_PER_MATMUL

    def pair_body(p, carry):
        run_tiles(g0 + p * pair, pair)
        return carry

    lax.fori_loop(0, n // pair, pair_body, 0)
    for left in range(1, pair):
        pl.when(n % pair == left)(functools.partial(run_tiles, g0 + n - left, left))

    @pl.when(e == ne - 1)
    def _drain_writes():
        for back in range(nout, 0, -1):
            @pl.when(n_used >= back)
            def _(back=back):
                y_copy(n_used - back, (n_used - back) % nout).wait()


def _experts(tile_first, tile_count, n_used, xs, w1, w3, w2, layer):
    n_rows = xs.shape[0]
    r = EXPERT_TILE
    any_spec = pl.BlockSpec(memory_space=pl.ANY)
    return pl.pallas_call(
        functools.partial(_expert_kernel, layer=layer),
        grid_spec=pltpu.PrefetchScalarGridSpec(
            num_scalar_prefetch=3,
            grid=(N_EXPERTS,),
            in_specs=[any_spec, any_spec, any_spec, any_spec],
            out_specs=any_spec,
            scratch_shapes=[pltpu.VMEM((EXPERT_IN_RING, r, HALF), jnp.uint32),
                            pltpu.VMEM((EXPERT_OUT_RING, r, HALF), jnp.uint32),
                            pltpu.SemaphoreType.DMA((EXPERT_IN_RING,)),
                            pltpu.SemaphoreType.DMA((EXPERT_OUT_RING,)),
                            pltpu.VMEM((EXPERT_WEIGHT_RING, D_MODEL, EXPERT_FF), F32),
                            pltpu.VMEM((EXPERT_WEIGHT_RING, D_MODEL, EXPERT_FF), F32),
                            pltpu.VMEM((EXPERT_WEIGHT_RING, EXPERT_FF, D_MODEL), F32),
                            pltpu.SemaphoreType.DMA((EXPERT_WEIGHT_RING,)),
                            pltpu.VMEM((D_MODEL, EXPERT_FF), BF16),
                            pltpu.VMEM((D_MODEL, EXPERT_FF), BF16),
                            pltpu.VMEM((EXPERT_FF, D_MODEL), BF16)],
        ),
        out_shape=jax.ShapeDtypeStruct((n_rows, HALF), jnp.uint32),
        compiler_params=_params(1),
        name="experts",
    )(tile_first, tile_count, n_used, w1, w3, w2, xs)


def _combine_dense_kernel(base_ref, g2_ref, w_ref, yg_ref, o_ref):
    acc_lo = acc_hi = None
    for k in range(TOP_K):
        lo, hi = _unpack_rows(yg_ref[k])
        wk = w_ref[:, k:k + 1]
        acc_lo = wk * lo if acc_lo is None else acc_lo + wk * lo
        acc_hi = wk * hi if acc_hi is None else acc_hi + wk * hi
    o_ref[:, :HALF] = base_ref[:, :HALF] + g2_ref[:, :HALF] * acc_lo
    o_ref[:, HALF:] = base_ref[:, HALF:] + g2_ref[:, HALF:] * acc_hi


def _combine_dense(base, g2, w_tok, yg, batch):
    t = base.shape[0]
    seq = yg.shape[1]
    nt = ROUTE_TILE
    tpb = seq // nt
    rows = lambda i: (batch * tpb + i, 0)
    return pl.pallas_call(
        _combine_dense_kernel,
        grid=(tpb,),
        in_specs=[pl.BlockSpec((nt, D_MODEL), rows),
                  pl.BlockSpec((None, 1, D_MODEL), lambda i: (batch, 0, 0)),
                  pl.BlockSpec((nt, TOP_K), rows),
                  pl.BlockSpec((TOP_K, nt, HALF), lambda i: (0, i, 0))],
        out_specs=pl.BlockSpec((nt, D_MODEL), rows),
        out_shape=jax.ShapeDtypeStruct((t, D_MODEL), F32),
        input_output_aliases={0: 0},
        compiler_params=_params(1),
        name="combine_dense",
    )(base, g2, w_tok, yg)


def _layer(layer, x, c, w_ada, b_ada, norm1_w, norm2_w, w_in, q_norm_w, k_norm_w, rel_bias, w_alpha, b_alpha,
           moba_out_w, gla_out_w, w_out, w_router, e_bias, w1, w3, w2, ws1, ws3, ws2):
    b, s, d = x.shape
    t = b * s
    x2 = x.reshape(t, d)

    mod = _mod(c, w_ada, b_ada)
    sh1, sc1, g1, sh2, sc2, g2 = [mod[:, j * d:(j + 1) * d].reshape(b, 1, d) for j in range(6)]

    w_main = w_in[:, :D_MAIN].astype(BF16)
    w_ga = jnp.zeros((d, LANES), BF16).at[:, :GLA_GATE_RANK].set(w_in[:, D_MAIN:].astype(BF16))
    per_chunk = MXU_COLS // MOBA_HEAD_DIM
    qw = jnp.tile(q_norm_w.astype(F32), per_chunk).reshape(1, MXU_COLS) * (MOBA_HEAD_DIM ** -0.5 * LOG2E)
    kw = jnp.tile(k_norm_w.astype(F32), per_chunk).reshape(1, MXU_COLS)
    proj, ga = _inproj(x2, sc1, sh1, norm1_w.reshape(1, d), w_main, w_ga, qw, kw, s)
    proj3 = proj.reshape(b, s, D_MAIN)

    near, far = _moba_bias_tables(rel_bias)
    ow = jnp.tile(moba_out_w.astype(F32), 2).reshape(1, LANES)
    o_a = _moba(proj3, near, far, ow)

    wal = jnp.zeros((LANES, GLA_KEY_WIDTH), F32).at[:GLA_GATE_RANK].set(w_alpha)
    o_b = _gla(proj3, ga.reshape(b, s, LANES), wal, b_alpha.reshape(1, GLA_KEY_WIDTH),
               gla_out_w.reshape(1, GLA_DV))

    eb = jnp.broadcast_to(e_bias.astype(F32)[:, None], (N_EXPERTS, ROUTE_TILE))
    base, h2, code_t, w_t, counts = _outproj(
        o_a.reshape(t, MOBA_WIDTH), o_b.reshape(t, GLA_WIDTH), x2, g1, sc2, sh2, g2,
        norm2_w.reshape(1, d), w_out.astype(BF16), ws1.astype(BF16), ws3.astype(BF16), ws2.astype(BF16),
        w_router.T.astype(BF16), eb, s)

    r = EXPERT_TILE
    n_tiles = (t * TOP_K + N_EXPERTS * (r - 1) + r - 1) // r
    n_rows = n_tiles * r
    cnt = counts[:, 0].astype(jnp.int32)
    padded = (cnt + r - 1) // r * r
    pend = jnp.cumsum(padded)
    pstart = pend - padded
    n_used = (pend[-1:] // r).astype(jnp.int32)
    dest_t = _slots(pstart, code_t)

    xs = _sc_scatter_rows(h2, dest_t, n_rows)
    ys = _experts(pstart // r, padded // r, n_used, xs, w1, w3, w2, layer)
    w_tok = w_t.T
    out = base
    for bi in range(b):
        idx = dest_t[:, bi * s:(bi + 1) * s].reshape(TOP_K * s)
        yg = _sc_gather_rows(ys, idx).reshape(TOP_K, s, HALF)
        out = _combine_dense(out, g2, w_tok, yg, bi)
    return out.reshape(b, s, d)


def kernel(x, c, w_ada, b_ada, norm1_w, norm2_w, w_in, q_norm_w, k_norm_w, rel_bias, w_alpha, b_alpha,
           moba_out_w, gla_out_w, w_out, w_router, e_bias, w1, w3, w2, ws1, ws3, ws2):
    for l in range(w_ada.shape[0]):
        x = _layer(l, x, c, w_ada[l], b_ada[l], norm1_w[l], norm2_w[l], w_in[l], q_norm_w[l], k_norm_w[l],
                   rel_bias, w_alpha[l], b_alpha[l], moba_out_w[l], gla_out_w[l], w_out[l], w_router[l],
                   e_bias[l], w1, w3, w2, ws1[l], ws3[l], ws2[l])
    return x
```

```python
import functools
import math

import numpy as np
import jax
import jax.numpy as jnp
from jax import lax
from jax.experimental import pallas as pl
from jax.experimental.pallas import tpu as pltpu
from jax.experimental.pallas import tpu_sc as plsc

D_MODEL = 1024
MOBA_HEADS = 8
MOBA_HEAD_DIM = 64
MOBA_WIDTH = MOBA_HEADS * MOBA_HEAD_DIM
MOBA_BLOCK = 256
MOBA_TOPK = 3
GLA_HEADS = 4
GLA_DK = 64
GLA_DV = 128
GLA_KEY_WIDTH = GLA_HEADS * GLA_DK
GLA_WIDTH = GLA_HEADS * GLA_DV
GLA_GATE_RANK = 16
GLA_GATE_TAU = 16.0
GLA_CHUNK = 64
REL_BUCKETS = 32
REL_MAX_DIST = 128
N_EXPERTS = 256
TOP_K = 8
N_GROUPS = 8
TOPK_GROUPS = 4
GROUP_SIZE = N_EXPERTS // N_GROUPS
EXPERT_FF = 256
SHARED_FF = 256
ROUTED_SCALE = 2.5
NORM_EPS = 1e-6
LOG2E = math.log2(math.e)

D_MAIN = 3 * MOBA_WIDTH + 2 * GLA_KEY_WIDTH + 2 * GLA_WIDTH
LANES = 128
SUBLANES = 8
MXU_COLS = 256
MOD_TILE = 1024
VMEM_LIMIT = 56 * 1024 * 1024

ROW_TILE = 512
ROUTE_TILE = 256
EXPERT_TILE = 128
EXPERT_TILES_PER_MATMUL = 6
EXPERT_IN_RING = 16
EXPERT_OUT_RING = 12
EXPERT_WEIGHT_RING = 4

F32 = jnp.float32
BF16 = jnp.bfloat16
NT_DIMS = (((1,), (1,)), ((), ()))
TN_DIMS = (((0,), (0,)), ((), ()))


def _params(n_axes):
    return pltpu.CompilerParams(dimension_semantics=("arbitrary",) * n_axes,
                                vmem_limit_bytes=VMEM_LIMIT)


def _silu(v):
    return v * jax.nn.sigmoid(v)


def _mod_kernel(c_ref, w_ref, b_ref, o_ref):
    o_ref[...] = jnp.dot(_silu(c_ref[...]), w_ref[...], preferred_element_type=F32) + b_ref[...]


def _mod(c, w, b):
    rows = SUBLANES
    cp = jnp.zeros((rows, D_MODEL), F32).at[:c.shape[0]].set(c)
    n = w.shape[1]
    tn = MOD_TILE
    out = pl.pallas_call(
        _mod_kernel,
        grid=(n // tn,),
        in_specs=[pl.BlockSpec((rows, D_MODEL), lambda j: (0, 0)),
                  pl.BlockSpec((D_MODEL, tn), lambda j: (0, j)),
                  pl.BlockSpec((1, tn), lambda j: (0, j))],
        out_specs=pl.BlockSpec((rows, tn), lambda j: (0, j)),
        out_shape=jax.ShapeDtypeStruct((rows, n), F32),
        compiler_params=_params(1),
        name="mod",
    )(cp, w, b.reshape(1, n))
    return out[:c.shape[0]]


def _group_rms_inv(a, group):
    lane = lax.broadcasted_iota(jnp.int32, (1, a.shape[1]), 1)
    a2 = a * a
    inv = jnp.zeros_like(a)
    for g in range(a.shape[1] // group):
        m = (lane >= g * group) & (lane < (g + 1) * group)
        ss = jnp.sum(jnp.where(m, a2, 0.0), axis=-1, keepdims=True)
        inv = jnp.where(m, lax.rsqrt(ss * (1.0 / group) + NORM_EPS), inv)
    return inv


def _inproj_kernel(x_ref, sc_ref, sh_ref, nw_ref, w_ref, wga_ref, qw_ref, kw_ref, o_ref, ga_ref):
    x = x_ref[...]
    ms = jnp.mean(x * x, axis=-1, keepdims=True)
    h = x * lax.rsqrt(ms + NORM_EPS) * nw_ref[...]
    h = h * (1.0 + sc_ref[...]) + sh_ref[...]
    hb = h.astype(BF16)
    cw = MXU_COLS
    for j in range(D_MAIN // cw):
        acc = jnp.dot(hb, w_ref[:, j * cw:(j + 1) * cw], preferred_element_type=F32)
        if j < 2 * MOBA_WIDTH // cw:
            nw = qw_ref if j < MOBA_WIDTH // cw else kw_ref
            acc = acc * _group_rms_inv(acc, MOBA_HEAD_DIM) * nw[...]
        o_ref[:, j * cw:(j + 1) * cw] = acc.astype(BF16)
    ga_ref[...] = jnp.dot(hb, wga_ref[...], preferred_element_type=F32)


def _inproj(x2, sc, sh, nw, w_main, w_ga, qw, kw, seq):
    t = x2.shape[0]
    tpb = seq // ROW_TILE
    vec = lambda: pl.BlockSpec((None, 1, D_MODEL), lambda i: (i // tpb, 0, 0))
    full = lambda a: pl.BlockSpec(a.shape, lambda i: (0,) * a.ndim)
    return pl.pallas_call(
        _inproj_kernel,
        grid=(t // ROW_TILE,),
        in_specs=[pl.BlockSpec((ROW_TILE, D_MODEL), lambda i: (i, 0)), vec(), vec(),
                  full(nw), full(w_main), full(w_ga), full(qw), full(kw)],
        out_specs=[pl.BlockSpec((ROW_TILE, D_MAIN), lambda i: (i, 0)),
                   pl.BlockSpec((ROW_TILE, LANES), lambda i: (i, 0))],
        out_shape=[jax.ShapeDtypeStruct((t, D_MAIN), BF16),
                   jax.ShapeDtypeStruct((t, LANES), F32)],
        compiler_params=_params(1),
        name="inproj",
    )(x2, sc, sh, nw, w_main, w_ga, qw, kw)


def _t5_bucket_np(rel):
    max_exact = REL_BUCKETS // 2
    relf = np.maximum(rel, 1).astype(np.float64)
    large = max_exact + (np.log(relf / max_exact) / math.log(REL_MAX_DIST / max_exact)
                         * (REL_BUCKETS - max_exact)).astype(np.int32)
    large = np.minimum(large, REL_BUCKETS - 1)
    return np.where(rel < max_exact, rel, large)


def _bias_kernel(rb_ref, idx_ref, o_ref):
    h = pl.program_id(0)
    idx = idx_ref[...]
    tab = jnp.full(idx.shape, -jnp.inf, F32)
    for bk in range(REL_BUCKETS):
        tab = jnp.where(idx == bk, rb_ref[bk * MOBA_HEADS + h], tab)
    o_ref[...] = tab


def _moba_bias_tables(rel_bias):
    j = np.arange(MOBA_BLOCK)[:, None]
    i = np.arange(MOBA_BLOCK)[None, :]
    own_idx = np.where(j <= i, _t5_bucket_np(np.maximum(i - j, 0)), -1)
    prev_idx = _t5_bucket_np(MOBA_BLOCK + i - j)
    idx = jnp.asarray(np.concatenate([prev_idx, own_idx], axis=0).astype(np.int32))
    assert int(_t5_bucket_np(np.array([MOBA_BLOCK + 1]))[0]) == REL_BUCKETS - 1
    rb = rel_bias.astype(F32) * LOG2E
    near = pl.pallas_call(
        _bias_kernel,
        grid=(MOBA_HEADS,),
        in_specs=[pl.BlockSpec(memory_space=pltpu.SMEM),
                  pl.BlockSpec(idx.shape, lambda h: (0, 0))],
        out_specs=pl.BlockSpec((None,) + idx.shape, lambda h: (h, 0, 0)),
        out_shape=jax.ShapeDtypeStruct((MOBA_HEADS,) + idx.shape, F32),
        compiler_params=_params(1),
        name="bias",
    )(rb.reshape(-1), idx)
    return near, rb[REL_BUCKETS - 1]


FAR_GROUP = 4


MOBA_INTERLEAVE = 4


def _moba_kernel(*refs):
    hp = pl.program_id(1)
    nsets = refs[2].shape[0] // (MOBA_INTERLEAVE * MOBA_BLOCK)
    _moba_body(None, hp, *refs, prepare=True)

    def block_set(j, carry):
        for jj in range(nsets):
            blocks = list(range(jj * MOBA_INTERLEAVE, (jj + 1) * MOBA_INTERLEAVE))
            pl.when(j == jj)(functools.partial(_moba_body, blocks, hp, *refs, prepare=False))
        return carry

    lax.fori_loop(0, nsets, block_set, 0)


def _moba_body(blocks, hp, far_ref, q_ref, k_ref, v_ref, near_ref, ow_ref, o_ref,
               vt_ref, vtg_ref, acc_ref, m_ref, sel_ref, s_ref, mx_ref, *, prepare):
    nblk = k_ref.shape[0] // MOBA_BLOCK
    ngrp = nblk // FAR_GROUP
    hd = MOBA_HEAD_DIM
    bs = MOBA_BLOCK
    lane = lax.broadcasted_iota(jnp.int32, (bs, LANES), 1)

    def split_heads(qb):
        zero = jnp.zeros_like(qb)
        return jnp.where(lane < hd, qb, zero), jnp.where(lane < hd, zero, qb)

    def _prepare():
        row = lax.broadcasted_iota(jnp.int32, (LANES, bs), 0)
        kmeans = []
        for n in range(nblk):
            kb = k_ref[n * bs:(n + 1) * bs, :].astype(F32)
            kmeans.append(jnp.mean(kb, axis=0, keepdims=True))
            vt = v_ref[n * bs:(n + 1) * bs, :].astype(F32).T
            vt0 = jnp.where(row < hd, vt, 1.0).astype(BF16)
            vt1 = jnp.where(row < hd, 1.0, vt).astype(BF16)
            vt_ref[0, n] = vt0
            vt_ref[1, n] = vt1
            gcols = slice((n % FAR_GROUP) * bs, (n % FAR_GROUP + 1) * bs)
            vtg_ref[0, n // FAR_GROUP, :, gcols] = vt0
            vtg_ref[1, n // FAR_GROUP, :, gcols] = vt1
        kmean = jnp.concatenate(kmeans, axis=0)
        km_hi = kmean.astype(BF16)
        km_lo = (kmean - km_hi.astype(F32)).astype(BF16)
        blk = lax.broadcasted_iota(jnp.int32, (nblk, bs), 0)
        for ib in range(nblk):
            qparts = split_heads(q_ref[ib * bs:(ib + 1) * bs, :])
            for h in range(2):
                gt = (lax.dot_general(km_hi, qparts[h], NT_DIMS, preferred_element_type=F32)
                      + lax.dot_general(km_lo, qparts[h], NT_DIMS, preferred_element_type=F32))
                gt = jnp.where(blk < ib, gt, -jnp.inf)
                cnt = jnp.zeros(gt.shape, jnp.int32)
                for m in range(ib):
                    gm = gt[m:m + 1, :]
                    cnt = cnt + jnp.where((gm > gt) | ((gm == gt) & (blk > m)), 1, 0)
                keep = (blk < ib) & (cnt < MOBA_TOPK)
                sel_ref[0, h, ib] = jnp.where(keep, 1.0, 0.0)
                sel_ref[1, h, ib] = jnp.where(keep & (blk < ib - 1), 1.0, 0.0)

    if prepare:
        _prepare()
        return

    gk = FAR_GROUP * bs

    def rows_of(ib):
        if isinstance(ib, int):
            return slice(ib * bs, (ib + 1) * bs)
        return pl.ds(pl.multiple_of(ib * bs, bs), bs)

    def finish(ib, slot):
        a0 = acc_ref[slot, 0]
        a1 = acc_ref[slot, 1]
        row = lax.broadcasted_iota(jnp.int32, a0.shape, 0)
        ot = jnp.where(row < hd, a0 / a0[hd:hd + 1, :], a1 / a1[0:1, :])
        o2 = ot * ot
        ss0 = jnp.sum(jnp.where(row < hd, o2, 0.0), axis=0, keepdims=True)
        ss1 = jnp.sum(jnp.where(row < hd, 0.0, o2), axis=0, keepdims=True)
        inv = jnp.where(row < hd, lax.rsqrt(ss0 * (1.0 / hd) + NORM_EPS), lax.rsqrt(ss1 * (1.0 / hd) + NORM_EPS))
        o_ref[rows_of(ib), :] = ((ot * inv).T * ow_ref[...]).astype(o_ref.dtype)

    def far_scores(g, slot, qh, nb):
        kb = k_ref[g * gk:g * gk + nb * bs, :]
        for h in range(2):
            s = lax.dot_general(kb, qh[h], NT_DIMS, preferred_element_type=F32)
            s_ref[slot, g % 2, h, 0:nb * bs, :] = s
            for j in range(nb):
                mx_ref[slot, g % 2, h, j] = jnp.max(s[j * bs:(j + 1) * bs], axis=0, keepdims=True)

    def near_scores(ib, qh):
        kbs = (k_ref[rows_of(ib - 1), :], k_ref[rows_of(ib), :])
        return [[lax.dot_general(kbs[w], qh[h], NT_DIMS, preferred_element_type=F32)
                 + near_ref[h, w * bs:(w + 1) * bs, :] for w in range(2)] for h in range(2)]

    def near_values(ib, slot, ss):
        ps, ms = [], []
        for h in range(2):
            s_prev, s_own = ss[h]
            keep = sel_ref[0, h, ib, pl.ds(ib - 1, 1), :] > 0.5
            mx = jnp.where(keep, jnp.max(s_prev, axis=0, keepdims=True), -jnp.inf)
            m_new = jnp.maximum(jnp.max(s_own, axis=0, keepdims=True), mx)
            ps.append((jnp.exp2(s_prev - jnp.where(keep, m_new, jnp.inf)).astype(BF16),
                       jnp.exp2(s_own - m_new).astype(BF16)))
            ms.append(m_new)
        for h in range(2):
            acc_ref[slot, h] = (jnp.dot(vt_ref[h, ib - 1], ps[h][0], preferred_element_type=F32)
                                + jnp.dot(vt_ref[h, ib], ps[h][1], preferred_element_type=F32))
            m_ref[slot, h] = ms[h]

    def far_group(g, ib, slot, nb):
        for h in range(2):
            fb = far_ref[2 * hp + h]
            m_old = m_ref[slot, h]
            m_new = m_old
            keeps = []
            for j in range(nb):
                keep = sel_ref[1, h, ib, pl.ds(g * FAR_GROUP + j, 1), :] > 0.5
                m_new = jnp.maximum(m_new, jnp.where(keep, mx_ref[slot, g % 2, h, j] + fb, -jnp.inf))
                keeps.append(keep)
            p = jnp.concatenate(
                [jnp.exp2(s_ref[slot, g % 2, h, j * bs:(j + 1) * bs, :]
                          - jnp.where(keeps[j], m_new - fb, jnp.inf)).astype(BF16)
                 for j in range(nb)], axis=0)
            pv = jnp.dot(vtg_ref[h, g, :, 0:nb * bs], p, preferred_element_type=F32)
            acc_ref[slot, h] = acc_ref[slot, h] * jnp.exp2(m_old - m_new) + pv
            m_ref[slot, h] = m_new

    def step_body(blocks):
        slots = list(enumerate(blocks))
        nbs = [[min(FAR_GROUP, ib - 1 - g * FAR_GROUP) for g in range((ib - 2) // FAR_GROUP + 1)] if ib >= 2 else []
               for ib in blocks]
        qhs = [split_heads(q_ref[rows_of(ib), :]) for ib in blocks]
        near = [near_scores(ib, qhs[slot]) for slot, ib in slots]
        for slot, ib in slots:
            if nbs[slot]:
                far_scores(0, slot, qhs[slot], nbs[slot][0])
        for slot, ib in slots:
            near_values(ib, slot, near[slot])
        for g in range(max(len(n) for n in nbs)):
            for slot, ib in slots:
                if g + 1 < len(nbs[slot]):
                    far_scores(g + 1, slot, qhs[slot], nbs[slot][g + 1])
            for slot, ib in slots:
                if g < len(nbs[slot]):
                    far_group(g, ib, slot, nbs[slot][g])
        for slot, ib in slots:
            finish(ib, slot)

    if blocks[0] == 0:
        qh = split_heads(q_ref[0:bs, :])
        kb = k_ref[0:bs, :]
        for h in range(2):
            s = lax.dot_general(kb, qh[h], NT_DIMS, preferred_element_type=F32) + near_ref[h, bs:2 * bs, :]
            p = jnp.exp2(s - jnp.max(s, axis=0, keepdims=True)).astype(BF16)
            acc_ref[MOBA_INTERLEAVE - 1, h] = jnp.dot(vt_ref[h, 0], p, preferred_element_type=F32)
        finish(0, MOBA_INTERLEAVE - 1)
        blocks = blocks[1:]
    step_body(blocks)


def _moba(proj3, near, far, ow):
    b, s, _ = proj3.shape
    nblk = s // MOBA_BLOCK
    assert nblk % FAR_GROUP == 0
    npair = MOBA_HEADS // 2
    kcol = MOBA_WIDTH // LANES
    return pl.pallas_call(
        _moba_kernel,
        grid=(b, npair),
        in_specs=[pl.BlockSpec(memory_space=pltpu.SMEM),
                  pl.BlockSpec((None, s, LANES), lambda bb, hp: (bb, 0, hp)),
                  pl.BlockSpec((None, s, LANES), lambda bb, hp: (bb, 0, kcol + hp)),
                  pl.BlockSpec((None, s, LANES), lambda bb, hp: (bb, 0, 2 * kcol + hp)),
                  pl.BlockSpec((2, 2 * MOBA_BLOCK, MOBA_BLOCK), lambda bb, hp: (hp, 0, 0)),
                  pl.BlockSpec((1, LANES), lambda bb, hp: (0, 0))],
        out_specs=pl.BlockSpec((None, s, LANES), lambda bb, hp: (bb, 0, hp)),
        out_shape=jax.ShapeDtypeStruct((b, s, MOBA_WIDTH), BF16),
        scratch_shapes=[pltpu.VMEM((2, nblk, LANES, MOBA_BLOCK), BF16),
                        pltpu.VMEM((2, nblk // FAR_GROUP, LANES, FAR_GROUP * MOBA_BLOCK), BF16),
                        pltpu.VMEM((MOBA_INTERLEAVE, 2, LANES, MOBA_BLOCK), F32),
                        pltpu.VMEM((MOBA_INTERLEAVE, 2, 1, MOBA_BLOCK), F32),
                        pltpu.VMEM((2, 2, nblk, nblk, MOBA_BLOCK), F32),
                        pltpu.VMEM((MOBA_INTERLEAVE, 2, 2, FAR_GROUP * MOBA_BLOCK, MOBA_BLOCK), F32),
                        pltpu.VMEM((MOBA_INTERLEAVE, 2, 2, FAR_GROUP, 1, MOBA_BLOCK), F32)],
        compiler_params=_params(2),
        name="moba",
    )(far, proj3, proj3, proj3, near, ow)


def _split3(v):
    hi = v.astype(BF16)
    r1 = v - hi.astype(F32)
    mid = r1.astype(BF16)
    lo = (r1 - mid.astype(F32)).astype(BF16)
    return hi, mid, lo


GLA_UNROLL = 16


def _gla_kernel(q_ref, k_ref, v_ref, g_ref, ga_ref, wal_ref, bal_ref, gw_ref, o_ref, b_ref, st_ref):
    seq = q_ref.shape[0]
    c = GLA_CHUNK
    pc = 256

    rr = lax.broadcasted_iota(jnp.int32, (pc, pc), 0)
    cc = lax.broadcasted_iota(jnp.int32, (pc, pc), 1)
    tri = jnp.where((rr >= cc) & (rr // c == cc // c), 1.0, 0.0).astype(BF16)

    def decay_body(j, carry):
        rows = [pl.ds(pl.multiple_of((j * GLA_UNROLL + u) * pc, pc), pc) for u in range(GLA_UNROLL)]
        xg = [jnp.dot(ga_ref[r, :], wal_ref[...], preferred_element_type=F32) + bal_ref[...] for r in rows]
        parts = [_split3((jnp.minimum(x, 0.0) - jnp.log(1.0 + jnp.exp(-jnp.abs(x)))) * (1.0 / GLA_GATE_TAU))
                 for x in xg]
        sums = [[jnp.dot(tri, term, preferred_element_type=F32) for term in p] for p in parts]
        for r, (hi, mid, lo) in zip(rows, sums):
            b_ref[r, :] = hi + mid + lo
        return carry

    lax.fori_loop(0, seq // (pc * GLA_UNROLL), decay_body, 0)

    st_ref[...] = jnp.zeros_like(st_ref)
    lane = lax.broadcasted_iota(jnp.int32, (c, LANES), 1)
    head_mask = (lane < GLA_DK, lane >= GLA_DK)
    causal = lax.broadcasted_iota(jnp.int32, (c, c), 0) >= lax.broadcasted_iota(jnp.int32, (c, c), 1)

    units = [(u, h) for u in range(GLA_UNROLL) for h in range(2)]

    def chunk_body(ci, carry):
        rows = [pl.ds(pl.multiple_of((ci * GLA_UNROLL + u) * c, c), c) for u in range(GLA_UNROLL)]
        qt, kt, qs, ke, e_last = [], [], [], [], []
        for u in range(GLA_UNROLL):
            b = b_ref[rows[u], :]
            ref_row = b[c // 2 - 1:c // 2, :]
            last = b[c - 1:c, :]
            q = q_ref[rows[u], :].astype(F32) * (GLA_DK ** -0.5)
            k = k_ref[rows[u], :].astype(F32)
            qt.append(q * jnp.exp(b - ref_row))
            kt.append((k * jnp.exp(ref_row - b)).astype(BF16))
            qs.append(q * jnp.exp(b))
            ke.append((k * jnp.exp(last - b)).astype(BF16))
            e_last.append(jnp.exp(last))
        vs = {(u, h): v_ref[rows[u], h * GLA_DV:(h + 1) * GLA_DV] for u, h in units}
        a = {(u, h): lax.dot_general(jnp.where(head_mask[h], qt[u], 0.0).astype(BF16), kt[u], NT_DIMS,
                                     preferred_element_type=F32) for u, h in units}
        inc = {(u, h): lax.dot_general(vs[u, h], ke[u], TN_DIMS, preferred_element_type=F32) for u, h in units}
        o = {(u, h): jnp.dot(jnp.where(causal, a[u, h], 0.0).astype(BF16), vs[u, h], preferred_element_type=F32)
             for u, h in units}
        states = {}
        for h in range(2):
            st = st_ref[h]
            for u in range(GLA_UNROLL):
                states[u, h] = st
                st = st * e_last[u] + inc[u, h]
            st_ref[h] = st
        for u, h in units:
            cols = slice(h * GLA_DV, (h + 1) * GLA_DV)
            ou = o[u, h] + lax.dot_general(jnp.where(head_mask[h], qs[u], 0.0).astype(BF16),
                                           states[u, h].astype(BF16), NT_DIMS, preferred_element_type=F32)
            ms = jnp.mean(ou * ou, axis=-1, keepdims=True)
            on = ou * lax.rsqrt(ms + NORM_EPS) * gw_ref[...]
            g = g_ref[rows[u], cols].astype(F32)
            o_ref[rows[u], cols] = (on * _silu(g)).astype(o_ref.dtype)
        return carry

    lax.fori_loop(0, seq // (c * GLA_UNROLL), chunk_body, 0)


def _gla(proj3, ga3, wal, bal, gw):
    b, s, _ = proj3.shape
    npair = GLA_HEADS // 2
    qcol = 3 * MOBA_WIDTH // LANES
    kcol = qcol + GLA_KEY_WIDTH // LANES
    vcol = (3 * MOBA_WIDTH + 2 * GLA_KEY_WIDTH) // (2 * GLA_DV)
    gcol = vcol + npair
    return pl.pallas_call(
        _gla_kernel,
        grid=(b, npair),
        in_specs=[pl.BlockSpec((None, s, LANES), lambda bb, hp: (bb, 0, qcol + hp)),
                  pl.BlockSpec((None, s, LANES), lambda bb, hp: (bb, 0, kcol + hp)),
                  pl.BlockSpec((None, s, 2 * GLA_DV), lambda bb, hp: (bb, 0, vcol + hp)),
                  pl.BlockSpec((None, s, 2 * GLA_DV), lambda bb, hp: (bb, 0, gcol + hp)),
                  pl.BlockSpec((None, s, LANES), lambda bb, hp: (bb, 0, 0)),
                  pl.BlockSpec((LANES, LANES), lambda bb, hp: (0, hp)),
                  pl.BlockSpec((1, LANES), lambda bb, hp: (0, hp)),
                  pl.BlockSpec((1, GLA_DV), lambda bb, hp: (0, 0))],
        out_specs=pl.BlockSpec((None, s, 2 * GLA_DV), lambda bb, hp: (bb, 0, hp)),
        out_shape=jax.ShapeDtypeStruct((b, s, GLA_WIDTH), BF16),
        scratch_shapes=[pltpu.VMEM((s, LANES), F32),
                        pltpu.VMEM((2, GLA_DV, LANES), F32)],
        compiler_params=_params(2),
        name="gla",
    )(proj3, proj3, proj3, proj3, ga3, wal, bal, gw)


HALF = D_MODEL // 2


def _pack_rows(v):
    return pltpu.pack_elementwise([v[:, :HALF], v[:, HALF:]], packed_dtype=BF16)


def _unpack_rows(w):
    return (pltpu.unpack_elementwise(w, index=0, packed_dtype=BF16, unpacked_dtype=F32),
            pltpu.unpack_elementwise(w, index=1, packed_dtype=BF16, unpacked_dtype=F32))


def _outproj_kernel(oa_ref, ob_ref, x_ref, g1_ref, sc_ref, sh_ref, g2_ref, nw_ref, wo_ref,
                    ws1_ref, ws3_ref, ws2_ref, wrt_ref, eb_ref, base_ref, h_ref, code_ref, w_ref, cnt_ref,
                    carry_ref):
    @pl.when(pl.program_id(0) == 0)
    def _init():
        carry_ref[...] = jnp.zeros_like(carry_ref)

    mix = (jnp.dot(oa_ref[...], wo_ref[:MOBA_WIDTH, :], preferred_element_type=F32)
           + jnp.dot(ob_ref[...], wo_ref[MOBA_WIDTH:, :], preferred_element_type=F32))
    x1 = x_ref[...] + g1_ref[...] * mix
    ms = jnp.mean(x1 * x1, axis=-1, keepdims=True)
    h = x1 * lax.rsqrt(ms + NORM_EPS) * nw_ref[...]
    h = h * (1.0 + sc_ref[...]) + sh_ref[...]
    h_ref[...] = _pack_rows(h)
    hb = h.astype(BF16)
    logits_t = lax.dot_general(wrt_ref[...], hb, NT_DIMS, preferred_element_type=F32)
    scores = jax.nn.sigmoid(logits_t)
    a = jnp.dot(hb, ws1_ref[...], preferred_element_type=F32)
    u = jnp.dot(hb, ws3_ref[...], preferred_element_type=F32)
    shared = jnp.dot((_silu(a) * u).astype(BF16), ws2_ref[...], preferred_element_type=F32)
    base_ref[...] = x1 + g2_ref[...] * shared
    nt = eb_ref.shape[1]
    for part in range(scores.shape[1] // nt):
        cols = slice(part * nt, (part + 1) * nt)
        code, weights = _route_tile(scores[:, cols], eb_ref[...], carry_ref)
        code_ref[:, cols] = code
        w_ref[:, cols] = weights
    cnt_ref[...] = carry_ref[...]


def _outproj(oa, ob, x2, g1, sc, sh, g2, nw, wo, ws1, ws3, ws2, wrt, eb, seq):
    t = x2.shape[0]
    assert t <= SLOT_CODE_BASE
    tpb = seq // ROW_TILE
    vec = lambda: pl.BlockSpec((None, 1, D_MODEL), lambda i: (i // tpb, 0, 0))
    full = lambda a: pl.BlockSpec(a.shape, lambda i: (0,) * a.ndim)
    rows = lambda w: pl.BlockSpec((ROW_TILE, w), lambda i: (i, 0))
    tok = lambda: pl.BlockSpec((TOP_K, ROW_TILE), lambda i: (0, i))
    return pl.pallas_call(
        _outproj_kernel,
        grid=(t // ROW_TILE,),
        in_specs=[rows(MOBA_WIDTH), rows(GLA_WIDTH), rows(D_MODEL), vec(), vec(), vec(), vec(),
                  full(nw), full(wo), full(ws1), full(ws3), full(ws2), full(wrt), full(eb)],
        out_specs=[rows(D_MODEL), rows(HALF), tok(), tok(), full(eb)],
        out_shape=[jax.ShapeDtypeStruct((t, D_MODEL), F32),
                   jax.ShapeDtypeStruct((t, HALF), jnp.uint32),
                   jax.ShapeDtypeStruct((TOP_K, t), jnp.int32),
                   jax.ShapeDtypeStruct((TOP_K, t), F32),
                   jax.ShapeDtypeStruct(eb.shape, F32)],
        scratch_shapes=[pltpu.VMEM(eb.shape, F32)],
        compiler_params=_params(1),
        name="outproj",
    )(oa, ob, x2, g1, sc, sh, g2, nw, wo, ws1, ws3, ws2, wrt, eb)


SLOT_CODE_SHIFT = 16
SLOT_CODE_BASE = 1 << SLOT_CODE_SHIFT


def _route_tile(s, eb, carry_ref):
    ne, nt = s.shape
    choice = s + eb
    gio = lax.broadcasted_iota(jnp.int32, (GROUP_SIZE, nt), 0)
    gscore = []
    for g in range(N_GROUPS):
        cg = choice[g * GROUP_SIZE:(g + 1) * GROUP_SIZE, :]
        top1 = jnp.max(cg, axis=0, keepdims=True)
        first = jnp.min(jnp.where(cg == top1, gio, GROUP_SIZE), axis=0, keepdims=True)
        top2 = jnp.max(jnp.where(gio == first, -jnp.inf, cg), axis=0, keepdims=True)
        gscore.append(top1 + top2)
    gs = jnp.concatenate(gscore, axis=0)
    gidx = lax.broadcasted_iota(jnp.int32, gs.shape, 0)
    beaten = jnp.zeros(gs.shape, jnp.int32)
    for m in range(N_GROUPS):
        gm = gs[m:m + 1, :]
        beaten = beaten + jnp.where((gm > gs) | ((gm == gs) & (gidx > m)), 1, 0)
    gkeep = beaten < TOPK_GROUPS
    masked = jnp.concatenate(
        [jnp.where(gkeep[g:g + 1, :], choice[g * GROUP_SIZE:(g + 1) * GROUP_SIZE, :], -jnp.inf)
         for g in range(N_GROUPS)], axis=0)

    eio = lax.broadcasted_iota(jnp.int32, (ne, nt), 0)
    candidates = masked
    idx_rows, w_rows, hits = [], [], []
    for _ in range(TOP_K):
        mx = jnp.max(masked, axis=0, keepdims=True)
        idx = jnp.min(jnp.where(masked == mx, eio, ne), axis=0, keepdims=True)
        hit = eio == idx
        w_rows.append(jnp.sum(jnp.where(hit, s, 0.0), axis=0, keepdims=True))
        idx_rows.append(idx)
        hits.append(hit)
        masked = jnp.where(hit, -jnp.inf, masked)
    picked = jnp.where(masked < candidates, 1.0, 0.0)
    wk = jnp.concatenate(w_rows, axis=0)
    weights = wk / jnp.sum(wk, axis=0, keepdims=True) * ROUTED_SCALE

    tr = lax.broadcasted_iota(jnp.int32, (nt, nt), 0)
    tc = lax.broadcasted_iota(jnp.int32, (nt, nt), 1)
    before = jnp.where(tr < tc, 1.0, 0.0).astype(BF16)
    pb = picked.astype(BF16)
    pos = carry_ref[...] + jnp.dot(pb, before, preferred_element_type=F32)
    rank = jnp.concatenate(
        [jnp.sum(jnp.where(hit, pos, 0.0), axis=0, keepdims=True) for hit in hits], axis=0).astype(jnp.int32)
    carry_ref[...] = carry_ref[...] + jnp.dot(pb, jnp.ones((nt, nt), BF16), preferred_element_type=F32)
    return jnp.concatenate(idx_rows, axis=0) * SLOT_CODE_BASE + rank, weights


SLOT_TILE = 2048


def _slots_kernel(pstart_ref, code_ref, o_ref):
    code = code_ref[...]
    expert = lax.shift_right_logical(code, SLOT_CODE_SHIFT)

    def body(e, acc):
        return jnp.where(expert == e, pstart_ref[e], acc)

    start = lax.fori_loop(0, N_EXPERTS, body, jnp.zeros_like(code), unroll=8)
    o_ref[...] = start + (code & (SLOT_CODE_BASE - 1))


def _slots(pstart, code_t):
    k, t = code_t.shape
    return pl.pallas_call(
        _slots_kernel,
        grid_spec=pltpu.PrefetchScalarGridSpec(
            num_scalar_prefetch=1,
            grid=(t // SLOT_TILE,),
            in_specs=[pl.BlockSpec((k, SLOT_TILE), lambda i, p: (0, i))],
            out_specs=pl.BlockSpec((k, SLOT_TILE), lambda i, p: (0, i)),
        ),
        out_shape=jax.ShapeDtypeStruct((k, t), jnp.int32),
        compiler_params=_params(1),
        name="slots",
    )(pstart, code_t)


SC_WINDOW = 128


def _sc_gather_rows(table, idx_flat):
    info = plsc.get_sparse_core_info()
    nw = info.num_cores * info.num_subcores
    n = idx_flat.shape[0]
    width = table.shape[1]
    per_worker = n // nw
    assert per_worker * nw == n and per_worker % SC_WINDOW == 0
    mesh = plsc.VectorSubcoreMesh(core_axis_name="c", subcore_axis_name="s")

    def body(table_hbm, idx_hbm, out_hbm, idx_v, rows_v, sem):
        wid = lax.axis_index("s") * info.num_cores + lax.axis_index("c")
        base = wid * per_worker

        @pl.loop(0, per_worker // SC_WINDOW)
        def _(w):
            off = pl.multiple_of(base + w * SC_WINDOW, SC_WINDOW)
            pltpu.sync_copy(idx_hbm.at[pl.ds(off, SC_WINDOW)], idx_v)
            pltpu.async_copy(table_hbm.at[idx_v], rows_v, sem).wait()
            pltpu.sync_copy(rows_v, out_hbm.at[pl.ds(off, SC_WINDOW)])

    return pl.kernel(
        body,
        out_type=jax.ShapeDtypeStruct((n, width), table.dtype),
        mesh=mesh,
        scratch_types=[pltpu.VMEM((SC_WINDOW,), jnp.int32),
                       pltpu.VMEM((SC_WINDOW, width), table.dtype),
                       pltpu.SemaphoreType.DMA],
        name="sc_gather",
    )(table, idx_flat)


def _sc_scatter_rows(rows, idx_kt, n_out):
    info = plsc.get_sparse_core_info()
    nw = info.num_cores * info.num_subcores
    t, width = rows.shape
    nk = idx_kt.shape[0]
    per_worker = t // nw
    assert per_worker * nw == t and per_worker % SC_WINDOW == 0
    mesh = plsc.VectorSubcoreMesh(core_axis_name="c", subcore_axis_name="s")

    def body(rows_hbm, idx_hbm, out_hbm, idx_v, rows_v, sem):
        wid = lax.axis_index("s") * info.num_cores + lax.axis_index("c")
        base = wid * per_worker

        @pl.loop(0, per_worker // SC_WINDOW)
        def _(w):
            off = pl.multiple_of(base + w * SC_WINDOW, SC_WINDOW)
            pltpu.sync_copy(rows_hbm.at[pl.ds(off, SC_WINDOW)], rows_v)
            pltpu.sync_copy(idx_hbm.at[:, pl.ds(off, SC_WINDOW)], idx_v)
            copies = [pltpu.async_copy(rows_v, out_hbm.at[idx_v.at[k]], sem) for k in range(nk)]
            for cp in copies:
                cp.wait()

    return pl.kernel(
        body,
        out_type=jax.ShapeDtypeStruct((n_out, width), rows.dtype),
        mesh=mesh,
        scratch_types=[pltpu.VMEM((nk, SC_WINDOW), jnp.int32),
                       pltpu.VMEM((SC_WINDOW, width), rows.dtype),
                       pltpu.SemaphoreType.DMA],
        name="sc_scatter",
    )(rows, idx_kt)


def _expert_kernel(first_ref, count_ref, used_ref, w1_ref, w3_ref, w2_ref, xs_ref, ys_ref,
                   xbuf, ybuf, sem_in, sem_out, w1f, w3f, w2f, sem_w, w1b, w3b, w2b, *, layer):
    e = pl.program_id(0)
    ne = pl.num_programs(0)
    r = xbuf.shape[1]
    n_used = used_ref[0]

    def x_copy(g, slot):
        return pltpu.make_async_copy(xs_ref.at[pl.ds(pl.multiple_of(g * r, r), r), :], xbuf.at[slot], sem_in.at[slot])

    def y_copy(g, slot):
        return pltpu.make_async_copy(ybuf.at[slot], ys_ref.at[pl.ds(pl.multiple_of(g * r, r), r), :], sem_out.at[slot])

    def w_copies(ex, slot):
        return [pltpu.make_async_copy(src.at[layer, ex], dst.at[slot], sem_w.at[slot])
                for src, dst in ((w1_ref, w1f), (w3_ref, w3f), (w2_ref, w2f))]

    nin = xbuf.shape[0]
    nout = ybuf.shape[0]

    nw = w1f.shape[0]

    @pl.when(e == 0)
    def _first_reads():
        for g in range(nin):
            @pl.when(g < n_used)
            def _(g=g):
                x_copy(g, g).start()
        for ex in range(nw - 1):
            for cp in w_copies(ex, ex):
                cp.start(priority=1)

    @pl.when(e + nw - 1 < ne)
    def _weights_ahead():
        for cp in w_copies(e + nw - 1, (e + nw - 1) % nw):
            cp.start(priority=1)

    wslot = e % nw
    for cp in w_copies(e, wslot):
        cp.wait()

    n = count_ref[e]

    @pl.when(n > 0)
    def _cast_weights():
        w1b[...] = w1f[wslot].astype(BF16)
        w3b[...] = w3f[wslot].astype(BF16)
        w2b[...] = w2f[wslot].astype(BF16)

    def run_tiles(g, count):
        tiles = [g + j for j in range(count)]
        for gj in tiles:
            x_copy(gj, gj % nin).wait()
        x = jnp.concatenate([xbuf[gj % nin] for gj in tiles], axis=0)
        for gj in tiles:
            @pl.when(gj + nin < n_used)
            def _(gj=gj):
                x_copy(gj + nin, gj % nin).start()
        lo, hi = _unpack_rows(x)
        lo, hi = lo.astype(BF16), hi.astype(BF16)
        a = (jnp.dot(lo, w1b[:HALF, :], preferred_element_type=F32)
             + jnp.dot(hi, w1b[HALF:, :], preferred_element_type=F32))
        u = (jnp.dot(lo, w3b[:HALF, :], preferred_element_type=F32)
             + jnp.dot(hi, w3b[HALF:, :], preferred_element_type=F32))
        y = _pack_rows(jnp.dot((_silu(a) * u).astype(BF16), w2b[...], preferred_element_type=F32))
        for j, gj in enumerate(tiles):
            @pl.when(gj >= nout)
            def _(gj=gj):
                y_copy(gj - nout, gj % nout).wait()
            ybuf[gj % nout] = y[j * r:(j + 1) * r]
            y_copy(gj, gj % nout).start()

    g0 = first_ref[e]
    pair = EXPERT_TILES_PER_MATMUL

    def pair_body(p, carry):
        run_tiles(g0 + p * pair, pair)
        return carry

    lax.fori_loop(0, n // pair, pair_body, 0)
    for left in range(1, pair):
        pl.when(n % pair == left)(functools.partial(run_tiles, g0 + n - left, left))

    @pl.when(e == ne - 1)
    def _drain_writes():
        for back in range(nout, 0, -1):
            @pl.when(n_used >= back)
            def _(back=back):
                y_copy(n_used - back, (n_used - back) % nout).wait()


def _experts(tile_first, tile_count, n_used, xs, w1, w3, w2, layer):
    n_rows = xs.shape[0]
    r = EXPERT_TILE
    any_spec = pl.BlockSpec(memory_space=pl.ANY)
    return pl.pallas_call(
        functools.partial(_expert_kernel, layer=layer),
        grid_spec=pltpu.PrefetchScalarGridSpec(
            num_scalar_prefetch=3,
            grid=(N_EXPERTS,),
            in_specs=[any_spec, any_spec, any_spec, any_spec],
            out_specs=any_spec,
            scratch_shapes=[pltpu.VMEM((EXPERT_IN_RING, r, HALF), jnp.uint32),
                            pltpu.VMEM((EXPERT_OUT_RING, r, HALF), jnp.uint32),
                            pltpu.SemaphoreType.DMA((EXPERT_IN_RING,)),
                            pltpu.SemaphoreType.DMA((EXPERT_OUT_RING,)),
                            pltpu.VMEM((EXPERT_WEIGHT_RING, D_MODEL, EXPERT_FF), F32),
                            pltpu.VMEM((EXPERT_WEIGHT_RING, D_MODEL, EXPERT_FF), F32),
                            pltpu.VMEM((EXPERT_WEIGHT_RING, EXPERT_FF, D_MODEL), F32),
                            pltpu.SemaphoreType.DMA((EXPERT_WEIGHT_RING,)),
                            pltpu.VMEM((D_MODEL, EXPERT_FF), BF16),
                            pltpu.VMEM((D_MODEL, EXPERT_FF), BF16),
                            pltpu.VMEM((EXPERT_FF, D_MODEL), BF16)],
        ),
        out_shape=jax.ShapeDtypeStruct((n_rows, HALF), jnp.uint32),
        compiler_params=_params(1),
        name="experts",
    )(tile_first, tile_count, n_used, w1, w3, w2, xs)


def _combine_dense_kernel(base_ref, g2_ref, w_ref, yg_ref, o_ref):
    acc_lo = acc_hi = None
    for k in range(TOP_K):
        lo, hi = _unpack_rows(yg_ref[k])
        wk = w_ref[:, k:k + 1]
        acc_lo = wk * lo if acc_lo is None else acc_lo + wk * lo
        acc_hi = wk * hi if acc_hi is None else acc_hi + wk * hi
    o_ref[:, :HALF] = base_ref[:, :HALF] + g2_ref[:, :HALF] * acc_lo
    o_ref[:, HALF:] = base_ref[:, HALF:] + g2_ref[:, HALF:] * acc_hi


def _combine_dense(base, g2, w_tok, yg, batch):
    t = base.shape[0]
    seq = yg.shape[1]
    nt = ROUTE_TILE
    tpb = seq // nt
    rows = lambda i: (batch * tpb + i, 0)
    return pl.pallas_call(
        _combine_dense_kernel,
        grid=(tpb,),
        in_specs=[pl.BlockSpec((nt, D_MODEL), rows),
                  pl.BlockSpec((None, 1, D_MODEL), lambda i: (batch, 0, 0)),
                  pl.BlockSpec((nt, TOP_K), rows),
                  pl.BlockSpec((TOP_K, nt, HALF), lambda i: (0, i, 0))],
        out_specs=pl.BlockSpec((nt, D_MODEL), rows),
        out_shape=jax.ShapeDtypeStruct((t, D_MODEL), F32),
        input_output_aliases={0: 0},
        compiler_params=_params(1),
        name="combine_dense",
    )(base, g2, w_tok, yg)


def _layer(layer, x, c, w_ada, b_ada, norm1_w, norm2_w, w_in, q_norm_w, k_norm_w, rel_bias, w_alpha, b_alpha,
           moba_out_w, gla_out_w, w_out, w_router, e_bias, w1, w3, w2, ws1, ws3, ws2):
    b, s, d = x.shape
    t = b * s
    x2 = x.reshape(t, d)

    mod = _mod(c, w_ada, b_ada)
    sh1, sc1, g1, sh2, sc2, g2 = [mod[:, j * d:(j + 1) * d].reshape(b, 1, d) for j in range(6)]

    w_main = w_in[:, :D_MAIN].astype(BF16)
    w_ga = jnp.zeros((d, LANES), BF16).at[:, :GLA_GATE_RANK].set(w_in[:, D_MAIN:].astype(BF16))
    per_chunk = MXU_COLS // MOBA_HEAD_DIM
    qw = jnp.tile(q_norm_w.astype(F32), per_chunk).reshape(1, MXU_COLS) * (MOBA_HEAD_DIM ** -0.5 * LOG2E)
    kw = jnp.tile(k_norm_w.astype(F32), per_chunk).reshape(1, MXU_COLS)
    proj, ga = _inproj(x2, sc1, sh1, norm1_w.reshape(1, d), w_main, w_ga, qw, kw, s)
    proj3 = proj.reshape(b, s, D_MAIN)

    near, far = _moba_bias_tables(rel_bias)
    ow = jnp.tile(moba_out_w.astype(F32), 2).reshape(1, LANES)
    o_a = _moba(proj3, near, far, ow)

    wal = jnp.zeros((LANES, GLA_KEY_WIDTH), F32).at[:GLA_GATE_RANK].set(w_alpha)
    o_b = _gla(proj3, ga.reshape(b, s, LANES), wal, b_alpha.reshape(1, GLA_KEY_WIDTH),
               gla_out_w.reshape(1, GLA_DV))

    eb = jnp.broadcast_to(e_bias.astype(F32)[:, None], (N_EXPERTS, ROUTE_TILE))
    base, h2, code_t, w_t, counts = _outproj(
        o_a.reshape(t, MOBA_WIDTH), o_b.reshape(t, GLA_WIDTH), x2, g1, sc2, sh2, g2,
        norm2_w.reshape(1, d), w_out.astype(BF16), ws1.astype(BF16), ws3.astype(BF16), ws2.astype(BF16),
        w_router.T.astype(BF16), eb, s)

    r = EXPERT_TILE
    n_tiles = (t * TOP_K + N_EXPERTS * (r - 1) + r - 1) // r
    n_rows = n_tiles * r
    cnt = counts[:, 0].astype(jnp.int32)
    padded = (cnt + r - 1) // r * r
    pend = jnp.cumsum(padded)
    pstart = pend - padded
    n_used = (pend[-1:] // r).astype(jnp.int32)
    dest_t = _slots(pstart, code_t)

    xs = _sc_scatter_rows(h2, dest_t, n_rows)
    ys = _experts(pstart // r, padded // r, n_used, xs, w1, w3, w2, layer)
    w_tok = w_t.T
    out = base
    for bi in range(b):
        idx = dest_t[:, bi * s:(bi + 1) * s].reshape(TOP_K * s)
        yg = _sc_gather_rows(ys, idx).reshape(TOP_K, s, HALF)
        out = _combine_dense(out, g2, w_tok, yg, bi)
    return out.reshape(b, s, d)


def kernel(x, c, w_ada, b_ada, norm1_w, norm2_w, w_in, q_norm_w, k_norm_w, rel_bias, w_alpha, b_alpha,
           moba_out_w, gla_out_w, w_out, w_router, e_bias, w1, w3, w2, ws1, ws3, ws2):
    for l in range(w_ada.shape[0]):
        x = _layer(l, x, c, w_ada[l], b_ada[l], norm1_w[l], norm2_w[l], w_in[l], q_norm_w[l], k_norm_w[l],
                   rel_bias, w_alpha[l], b_alpha[l], moba_out_w[l], gla_out_w[l], w_out[l], w_router[l],
                   e_bias[l], w1, w3, w2, ws1[l], ws3[l], ws2[l])
    return x
```

```python
import functools
import math

import numpy as np
import jax
import jax.numpy as jnp
from jax import lax
from jax.experimental import pallas as pl
from jax.experimental.pallas import tpu as pltpu
from jax.experimental.pallas import tpu_sc as plsc

D_MODEL = 1024
MOBA_HEADS = 8
MOBA_HEAD_DIM = 64
MOBA_WIDTH = MOBA_HEADS * MOBA_HEAD_DIM
MOBA_BLOCK = 256
MOBA_TOPK = 3
GLA_HEADS = 4
GLA_DK = 64
GLA_DV = 128
GLA_KEY_WIDTH = GLA_HEADS * GLA_DK
GLA_WIDTH = GLA_HEADS * GLA_DV
GLA_GATE_RANK = 16
GLA_GATE_TAU = 16.0
GLA_CHUNK = 64
REL_BUCKETS = 32
REL_MAX_DIST = 128
N_EXPERTS = 256
TOP_K = 8
N_GROUPS = 8
TOPK_GROUPS = 4
GROUP_SIZE = N_EXPERTS // N_GROUPS
EXPERT_FF = 256
SHARED_FF = 256
ROUTED_SCALE = 2.5
NORM_EPS = 1e-6
LOG2E = math.log2(math.e)

D_MAIN = 3 * MOBA_WIDTH + 2 * GLA_KEY_WIDTH + 2 * GLA_WIDTH
LANES = 128
VMEM_LIMIT = 56 * 1024 * 1024

ROW_TILE = 512
ROUTE_TILE = 256
EXPERT_TILE = 128
EXPERT_TILES_PER_MATMUL = 6
EXPERT_IN_RING = 16
EXPERT_OUT_RING = 12
EXPERT_WEIGHT_RING = 4

F32 = jnp.float32
BF16 = jnp.bfloat16
NT_DIMS = (((1,), (1,)), ((), ()))
TN_DIMS = (((0,), (0,)), ((), ()))


def _params(n_axes):
    return pltpu.CompilerParams(dimension_semantics=("arbitrary",) * n_axes,
                                vmem_limit_bytes=VMEM_LIMIT)


def _silu(v):
    return v * jax.nn.sigmoid(v)


def _mod_kernel(c_ref, w_ref, b_ref, o_ref):
    o_ref[...] = jnp.dot(_silu(c_ref[...]), w_ref[...], preferred_element_type=F32) + b_ref[...]


def _mod(c, w, b):
    rows = 8
    cp = jnp.zeros((rows, D_MODEL), F32).at[:c.shape[0]].set(c)
    n = w.shape[1]
    tn = 1024
    out = pl.pallas_call(
        _mod_kernel,
        grid=(n // tn,),
        in_specs=[pl.BlockSpec((rows, D_MODEL), lambda j: (0, 0)),
                  pl.BlockSpec((D_MODEL, tn), lambda j: (0, j)),
                  pl.BlockSpec((1, tn), lambda j: (0, j))],
        out_specs=pl.BlockSpec((rows, tn), lambda j: (0, j)),
        out_shape=jax.ShapeDtypeStruct((rows, n), F32),
        compiler_params=_params(1),
        name="mod",
    )(cp, w, b.reshape(1, n))
    return out[:c.shape[0]]


def _group_rms_inv(a, group):
    lane = lax.broadcasted_iota(jnp.int32, (1, a.shape[1]), 1)
    a2 = a * a
    inv = jnp.zeros_like(a)
    for g in range(a.shape[1] // group):
        m = (lane >= g * group) & (lane < (g + 1) * group)
        ss = jnp.sum(jnp.where(m, a2, 0.0), axis=-1, keepdims=True)
        inv = jnp.where(m, lax.rsqrt(ss * (1.0 / group) + NORM_EPS), inv)
    return inv


def _inproj_kernel(x_ref, sc_ref, sh_ref, nw_ref, w_ref, wga_ref, qw_ref, kw_ref, o_ref, ga_ref):
    x = x_ref[...]
    ms = jnp.mean(x * x, axis=-1, keepdims=True)
    h = x * lax.rsqrt(ms + NORM_EPS) * nw_ref[...]
    h = h * (1.0 + sc_ref[...]) + sh_ref[...]
    hb = h.astype(BF16)
    cw = 256
    for j in range(D_MAIN // cw):
        acc = jnp.dot(hb, w_ref[:, j * cw:(j + 1) * cw], preferred_element_type=F32)
        if j < 2 * MOBA_WIDTH // cw:
            nw = qw_ref if j < MOBA_WIDTH // cw else kw_ref
            acc = acc * _group_rms_inv(acc, MOBA_HEAD_DIM) * nw[...]
        o_ref[:, j * cw:(j + 1) * cw] = acc.astype(BF16)
    ga_ref[...] = jnp.dot(hb, wga_ref[...], preferred_element_type=F32)


def _inproj(x2, sc, sh, nw, w_main, w_ga, qw, kw, seq):
    t = x2.shape[0]
    tpb = seq // ROW_TILE
    vec = lambda: pl.BlockSpec((None, 1, D_MODEL), lambda i: (i // tpb, 0, 0))
    full = lambda a: pl.BlockSpec(a.shape, lambda i: (0,) * a.ndim)
    return pl.pallas_call(
        _inproj_kernel,
        grid=(t // ROW_TILE,),
        in_specs=[pl.BlockSpec((ROW_TILE, D_MODEL), lambda i: (i, 0)), vec(), vec(),
                  full(nw), full(w_main), full(w_ga), full(qw), full(kw)],
        out_specs=[pl.BlockSpec((ROW_TILE, D_MAIN), lambda i: (i, 0)),
                   pl.BlockSpec((ROW_TILE, LANES), lambda i: (i, 0))],
        out_shape=[jax.ShapeDtypeStruct((t, D_MAIN), BF16),
                   jax.ShapeDtypeStruct((t, LANES), F32)],
        compiler_params=_params(1),
        name="inproj",
    )(x2, sc, sh, nw, w_main, w_ga, qw, kw)


def _t5_bucket_np(rel):
    max_exact = REL_BUCKETS // 2
    relf = np.maximum(rel, 1).astype(np.float64)
    large = max_exact + (np.log(relf / max_exact) / math.log(REL_MAX_DIST / max_exact)
                         * (REL_BUCKETS - max_exact)).astype(np.int32)
    large = np.minimum(large, REL_BUCKETS - 1)
    return np.where(rel < max_exact, rel, large)


def _bias_kernel(rb_ref, idx_ref, o_ref):
    h = pl.program_id(0)
    idx = idx_ref[...]
    tab = jnp.full(idx.shape, -jnp.inf, F32)
    for bk in range(REL_BUCKETS):
        tab = jnp.where(idx == bk, rb_ref[bk * MOBA_HEADS + h], tab)
    o_ref[...] = tab


def _moba_bias_tables(rel_bias):
    j = np.arange(MOBA_BLOCK)[:, None]
    i = np.arange(MOBA_BLOCK)[None, :]
    own_idx = np.where(j <= i, _t5_bucket_np(np.maximum(i - j, 0)), -1)
    prev_idx = _t5_bucket_np(MOBA_BLOCK + i - j)
    idx = jnp.asarray(np.concatenate([prev_idx, own_idx], axis=0).astype(np.int32))
    assert int(_t5_bucket_np(np.array([MOBA_BLOCK + 1]))[0]) == REL_BUCKETS - 1
    rb = rel_bias.astype(F32) * LOG2E
    near = pl.pallas_call(
        _bias_kernel,
        grid=(MOBA_HEADS,),
        in_specs=[pl.BlockSpec(memory_space=pltpu.SMEM),
                  pl.BlockSpec(idx.shape, lambda h: (0, 0))],
        out_specs=pl.BlockSpec((None,) + idx.shape, lambda h: (h, 0, 0)),
        out_shape=jax.ShapeDtypeStruct((MOBA_HEADS,) + idx.shape, F32),
        compiler_params=_params(1),
        name="bias",
    )(rb.reshape(-1), idx)
    return near, rb[REL_BUCKETS - 1]


FAR_GROUP = 4


MOBA_INTERLEAVE = 4


def _moba_kernel(*refs):
    hp = pl.program_id(1)
    nsets = refs[2].shape[0] // (MOBA_INTERLEAVE * MOBA_BLOCK)
    _moba_body(None, hp, *refs, prepare=True)

    def block_set(j, carry):
        for jj in range(nsets):
            blocks = list(range(jj * MOBA_INTERLEAVE, (jj + 1) * MOBA_INTERLEAVE))
            pl.when(j == jj)(functools.partial(_moba_body, blocks, hp, *refs, prepare=False))
        return carry

    lax.fori_loop(0, nsets, block_set, 0)


def _moba_body(blocks, hp, far_ref, q_ref, k_ref, v_ref, near_ref, ow_ref, o_ref,
               vt_ref, vtg_ref, acc_ref, m_ref, sel_ref, s_ref, mx_ref, *, prepare):
    nblk = k_ref.shape[0] // MOBA_BLOCK
    ngrp = nblk // FAR_GROUP
    hd = MOBA_HEAD_DIM
    bs = MOBA_BLOCK
    lane = lax.broadcasted_iota(jnp.int32, (bs, LANES), 1)

    def split_heads(qb):
        zero = jnp.zeros_like(qb)
        return jnp.where(lane < hd, qb, zero), jnp.where(lane < hd, zero, qb)

    def _prepare():
        row = lax.broadcasted_iota(jnp.int32, (LANES, bs), 0)
        kmeans = []
        for n in range(nblk):
            kb = k_ref[n * bs:(n + 1) * bs, :].astype(F32)
            kmeans.append(jnp.mean(kb, axis=0, keepdims=True))
            vt = v_ref[n * bs:(n + 1) * bs, :].astype(F32).T
            vt0 = jnp.where(row < hd, vt, 1.0).astype(BF16)
            vt1 = jnp.where(row < hd, 1.0, vt).astype(BF16)
            vt_ref[0, n] = vt0
            vt_ref[1, n] = vt1
            gcols = slice((n % FAR_GROUP) * bs, (n % FAR_GROUP + 1) * bs)
            vtg_ref[0, n // FAR_GROUP, :, gcols] = vt0
            vtg_ref[1, n // FAR_GROUP, :, gcols] = vt1
        kmean = jnp.concatenate(kmeans, axis=0)
        km_hi = kmean.astype(BF16)
        km_lo = (kmean - km_hi.astype(F32)).astype(BF16)
        blk = lax.broadcasted_iota(jnp.int32, (nblk, bs), 0)
        for ib in range(nblk):
            qparts = split_heads(q_ref[ib * bs:(ib + 1) * bs, :])
            for h in range(2):
                gt = (lax.dot_general(km_hi, qparts[h], NT_DIMS, preferred_element_type=F32)
                      + lax.dot_general(km_lo, qparts[h], NT_DIMS, preferred_element_type=F32))
                gt = jnp.where(blk < ib, gt, -jnp.inf)
                cnt = jnp.zeros(gt.shape, jnp.int32)
                for m in range(ib):
                    gm = gt[m:m + 1, :]
                    cnt = cnt + jnp.where((gm > gt) | ((gm == gt) & (blk > m)), 1, 0)
                keep = (blk < ib) & (cnt < MOBA_TOPK)
                sel_ref[0, h, ib] = jnp.where(keep, 1.0, 0.0)
                sel_ref[1, h, ib] = jnp.where(keep & (blk < ib - 1), 1.0, 0.0)

    if prepare:
        _prepare()
        return

    gk = FAR_GROUP * bs

    def rows_of(ib):
        if isinstance(ib, int):
            return slice(ib * bs, (ib + 1) * bs)
        return pl.ds(pl.multiple_of(ib * bs, bs), bs)

    def finish(ib, slot):
        a0 = acc_ref[slot, 0]
        a1 = acc_ref[slot, 1]
        row = lax.broadcasted_iota(jnp.int32, a0.shape, 0)
        ot = jnp.where(row < hd, a0 / a0[hd:hd + 1, :], a1 / a1[0:1, :])
        o2 = ot * ot
        ss0 = jnp.sum(jnp.where(row < hd, o2, 0.0), axis=0, keepdims=True)
        ss1 = jnp.sum(jnp.where(row < hd, 0.0, o2), axis=0, keepdims=True)
        inv = jnp.where(row < hd, lax.rsqrt(ss0 * (1.0 / hd) + NORM_EPS), lax.rsqrt(ss1 * (1.0 / hd) + NORM_EPS))
        o_ref[rows_of(ib), :] = ((ot * inv).T * ow_ref[...]).astype(o_ref.dtype)

    def far_scores(g, slot, qh, nb):
        kb = k_ref[g * gk:g * gk + nb * bs, :]
        for h in range(2):
            s = lax.dot_general(kb, qh[h], NT_DIMS, preferred_element_type=F32)
            s_ref[slot, g % 2, h, 0:nb * bs, :] = s
            for j in range(nb):
                mx_ref[slot, g % 2, h, j] = jnp.max(s[j * bs:(j + 1) * bs], axis=0, keepdims=True)

    def near_scores(ib, qh):
        kbs = (k_ref[rows_of(ib - 1), :], k_ref[rows_of(ib), :])
        return [[lax.dot_general(kbs[w], qh[h], NT_DIMS, preferred_element_type=F32)
                 + near_ref[h, w * bs:(w + 1) * bs, :] for w in range(2)] for h in range(2)]

    def near_values(ib, slot, ss):
        ps, ms = [], []
        for h in range(2):
            s_prev, s_own = ss[h]
            keep = sel_ref[0, h, ib, pl.ds(ib - 1, 1), :] > 0.5
            mx = jnp.where(keep, jnp.max(s_prev, axis=0, keepdims=True), -jnp.inf)
            m_new = jnp.maximum(jnp.max(s_own, axis=0, keepdims=True), mx)
            ps.append((jnp.exp2(s_prev - jnp.where(keep, m_new, jnp.inf)).astype(BF16),
                       jnp.exp2(s_own - m_new).astype(BF16)))
            ms.append(m_new)
        for h in range(2):
            acc_ref[slot, h] = (jnp.dot(vt_ref[h, ib - 1], ps[h][0], preferred_element_type=F32)
                                + jnp.dot(vt_ref[h, ib], ps[h][1], preferred_element_type=F32))
            m_ref[slot, h] = ms[h]

    def far_group(g, ib, slot, nb):
        for h in range(2):
            fb = far_ref[2 * hp + h]
            m_old = m_ref[slot, h]
            m_new = m_old
            keeps = []
            for j in range(nb):
                keep = sel_ref[1, h, ib, pl.ds(g * FAR_GROUP + j, 1), :] > 0.5
                m_new = jnp.maximum(m_new, jnp.where(keep, mx_ref[slot, g % 2, h, j] + fb, -jnp.inf))
                keeps.append(keep)
            p = jnp.concatenate(
                [jnp.exp2(s_ref[slot, g % 2, h, j * bs:(j + 1) * bs, :]
                          - jnp.where(keeps[j], m_new - fb, jnp.inf)).astype(BF16)
                 for j in range(nb)], axis=0)
            pv = jnp.dot(vtg_ref[h, g, :, 0:nb * bs], p, preferred_element_type=F32)
            acc_ref[slot, h] = acc_ref[slot, h] * jnp.exp2(m_old - m_new) + pv
            m_ref[slot, h] = m_new

    def step_body(blocks):
        slots = list(enumerate(blocks))
        nbs = [[min(FAR_GROUP, ib - 1 - g * FAR_GROUP) for g in range((ib - 2) // FAR_GROUP + 1)] if ib >= 2 else []
               for ib in blocks]
        qhs = [split_heads(q_ref[rows_of(ib), :]) for ib in blocks]
        near = [near_scores(ib, qhs[slot]) for slot, ib in slots]
        for slot, ib in slots:
            if nbs[slot]:
                far_scores(0, slot, qhs[slot], nbs[slot][0])
        for slot, ib in slots:
            near_values(ib, slot, near[slot])
        for g in range(max(len(n) for n in nbs)):
            for slot, ib in slots:
                if g + 1 < len(nbs[slot]):
                    far_scores(g + 1, slot, qhs[slot], nbs[slot][g + 1])
            for slot, ib in slots:
                if g < len(nbs[slot]):
                    far_group(g, ib, slot, nbs[slot][g])
        for slot, ib in slots:
            finish(ib, slot)

    if blocks[0] == 0:
        qh = split_heads(q_ref[0:bs, :])
        kb = k_ref[0:bs, :]
        for h in range(2):
            s = lax.dot_general(kb, qh[h], NT_DIMS, preferred_element_type=F32) + near_ref[h, bs:2 * bs, :]
            p = jnp.exp2(s - jnp.max(s, axis=0, keepdims=True)).astype(BF16)
            acc_ref[MOBA_INTERLEAVE - 1, h] = jnp.dot(vt_ref[h, 0], p, preferred_element_type=F32)
        finish(0, MOBA_INTERLEAVE - 1)
        blocks = blocks[1:]
    step_body(blocks)


def _moba(proj3, near, far, ow):
    b, s, _ = proj3.shape
    nblk = s // MOBA_BLOCK
    assert nblk % FAR_GROUP == 0
    npair = MOBA_HEADS // 2
    kcol = MOBA_WIDTH // LANES
    return pl.pallas_call(
        _moba_kernel,
        grid=(b, npair),
        in_specs=[pl.BlockSpec(memory_space=pltpu.SMEM),
                  pl.BlockSpec((None, s, LANES), lambda bb, hp: (bb, 0, hp)),
                  pl.BlockSpec((None, s, LANES), lambda bb, hp: (bb, 0, kcol + hp)),
                  pl.BlockSpec((None, s, LANES), lambda bb, hp: (bb, 0, 2 * kcol + hp)),
                  pl.BlockSpec((2, 2 * MOBA_BLOCK, MOBA_BLOCK), lambda bb, hp: (hp, 0, 0)),
                  pl.BlockSpec((1, LANES), lambda bb, hp: (0, 0))],
        out_specs=pl.BlockSpec((None, s, LANES), lambda bb, hp: (bb, 0, hp)),
        out_shape=jax.ShapeDtypeStruct((b, s, MOBA_WIDTH), BF16),
        scratch_shapes=[pltpu.VMEM((2, nblk, LANES, MOBA_BLOCK), BF16),
                        pltpu.VMEM((2, nblk // FAR_GROUP, LANES, FAR_GROUP * MOBA_BLOCK), BF16),
                        pltpu.VMEM((MOBA_INTERLEAVE, 2, LANES, MOBA_BLOCK), F32),
                        pltpu.VMEM((MOBA_INTERLEAVE, 2, 1, MOBA_BLOCK), F32),
                        pltpu.VMEM((2, 2, nblk, nblk, MOBA_BLOCK), F32),
                        pltpu.VMEM((MOBA_INTERLEAVE, 2, 2, FAR_GROUP * MOBA_BLOCK, MOBA_BLOCK), F32),
                        pltpu.VMEM((MOBA_INTERLEAVE, 2, 2, FAR_GROUP, 1, MOBA_BLOCK), F32)],
        compiler_params=_params(2),
        name="moba",
    )(far, proj3, proj3, proj3, near, ow)


def _split3(v):
    hi = v.astype(BF16)
    r1 = v - hi.astype(F32)
    mid = r1.astype(BF16)
    lo = (r1 - mid.astype(F32)).astype(BF16)
    return hi, mid, lo


GLA_UNROLL = 16


def _gla_kernel(q_ref, k_ref, v_ref, g_ref, ga_ref, wal_ref, bal_ref, gw_ref, o_ref, b_ref, st_ref):
    seq = q_ref.shape[0]
    c = GLA_CHUNK
    pc = 256

    rr = lax.broadcasted_iota(jnp.int32, (pc, pc), 0)
    cc = lax.broadcasted_iota(jnp.int32, (pc, pc), 1)
    tri = jnp.where((rr >= cc) & (rr // c == cc // c), 1.0, 0.0).astype(BF16)

    def decay_body(j, carry):
        rows = [pl.ds(pl.multiple_of((j * GLA_UNROLL + u) * pc, pc), pc) for u in range(GLA_UNROLL)]
        xg = [jnp.dot(ga_ref[r, :], wal_ref[...], preferred_element_type=F32) + bal_ref[...] for r in rows]
        parts = [_split3((jnp.minimum(x, 0.0) - jnp.log(1.0 + jnp.exp(-jnp.abs(x)))) * (1.0 / GLA_GATE_TAU))
                 for x in xg]
        sums = [[jnp.dot(tri, term, preferred_element_type=F32) for term in p] for p in parts]
        for r, (hi, mid, lo) in zip(rows, sums):
            b_ref[r, :] = hi + mid + lo
        return carry

    lax.fori_loop(0, seq // (pc * GLA_UNROLL), decay_body, 0)

    st_ref[...] = jnp.zeros_like(st_ref)
    lane = lax.broadcasted_iota(jnp.int32, (c, LANES), 1)
    head_mask = (lane < GLA_DK, lane >= GLA_DK)
    causal = lax.broadcasted_iota(jnp.int32, (c, c), 0) >= lax.broadcasted_iota(jnp.int32, (c, c), 1)

    units = [(u, h) for u in range(GLA_UNROLL) for h in range(2)]

    def chunk_body(ci, carry):
        rows = [pl.ds(pl.multiple_of((ci * GLA_UNROLL + u) * c, c), c) for u in range(GLA_UNROLL)]
        qt, kt, qs, ke, e_last = [], [], [], [], []
        for u in range(GLA_UNROLL):
            b = b_ref[rows[u], :]
            ref_row = b[c // 2 - 1:c // 2, :]
            last = b[c - 1:c, :]
            q = q_ref[rows[u], :].astype(F32) * (GLA_DK ** -0.5)
            k = k_ref[rows[u], :].astype(F32)
            qt.append(q * jnp.exp(b - ref_row))
            kt.append((k * jnp.exp(ref_row - b)).astype(BF16))
            qs.append(q * jnp.exp(b))
            ke.append((k * jnp.exp(last - b)).astype(BF16))
            e_last.append(jnp.exp(last))
        vs = {(u, h): v_ref[rows[u], h * GLA_DV:(h + 1) * GLA_DV] for u, h in units}
        a = {(u, h): lax.dot_general(jnp.where(head_mask[h], qt[u], 0.0).astype(BF16), kt[u], NT_DIMS,
                                     preferred_element_type=F32) for u, h in units}
        inc = {(u, h): lax.dot_general(vs[u, h], ke[u], TN_DIMS, preferred_element_type=F32) for u, h in units}
        o = {(u, h): jnp.dot(jnp.where(causal, a[u, h], 0.0).astype(BF16), vs[u, h], preferred_element_type=F32)
             for u, h in units}
        states = {}
        for h in range(2):
            st = st_ref[h]
            for u in range(GLA_UNROLL):
                states[u, h] = st
                st = st * e_last[u] + inc[u, h]
            st_ref[h] = st
        for u, h in units:
            cols = slice(h * GLA_DV, (h + 1) * GLA_DV)
            ou = o[u, h] + lax.dot_general(jnp.where(head_mask[h], qs[u], 0.0).astype(BF16),
                                           states[u, h].astype(BF16), NT_DIMS, preferred_element_type=F32)
            ms = jnp.mean(ou * ou, axis=-1, keepdims=True)
            on = ou * lax.rsqrt(ms + NORM_EPS) * gw_ref[...]
            g = g_ref[rows[u], cols].astype(F32)
            o_ref[rows[u], cols] = (on * _silu(g)).astype(o_ref.dtype)
        return carry

    lax.fori_loop(0, seq // (c * GLA_UNROLL), chunk_body, 0)


def _gla(proj3, ga3, wal, bal, gw):
    b, s, _ = proj3.shape
    npair = GLA_HEADS // 2
    qcol = 3 * MOBA_WIDTH // LANES
    kcol = qcol + GLA_KEY_WIDTH // LANES
    vcol = (3 * MOBA_WIDTH + 2 * GLA_KEY_WIDTH) // (2 * GLA_DV)
    gcol = vcol + npair
    return pl.pallas_call(
        _gla_kernel,
        grid=(b, npair),
        in_specs=[pl.BlockSpec((None, s, LANES), lambda bb, hp: (bb, 0, qcol + hp)),
                  pl.BlockSpec((None, s, LANES), lambda bb, hp: (bb, 0, kcol + hp)),
                  pl.BlockSpec((None, s, 2 * GLA_DV), lambda bb, hp: (bb, 0, vcol + hp)),
                  pl.BlockSpec((None, s, 2 * GLA_DV), lambda bb, hp: (bb, 0, gcol + hp)),
                  pl.BlockSpec((None, s, LANES), lambda bb, hp: (bb, 0, 0)),
                  pl.BlockSpec((LANES, LANES), lambda bb, hp: (0, hp)),
                  pl.BlockSpec((1, LANES), lambda bb, hp: (0, hp)),
                  pl.BlockSpec((1, GLA_DV), lambda bb, hp: (0, 0))],
        out_specs=pl.BlockSpec((None, s, 2 * GLA_DV), lambda bb, hp: (bb, 0, hp)),
        out_shape=jax.ShapeDtypeStruct((b, s, GLA_WIDTH), BF16),
        scratch_shapes=[pltpu.VMEM((s, LANES), F32),
                        pltpu.VMEM((2, GLA_DV, LANES), F32)],
        compiler_params=_params(2),
        name="gla",
    )(proj3, proj3, proj3, proj3, ga3, wal, bal, gw)


HALF = D_MODEL // 2


def _pack_rows(v):
    return pltpu.pack_elementwise([v[:, :HALF], v[:, HALF:]], packed_dtype=BF16)


def _unpack_rows(w):
    return (pltpu.unpack_elementwise(w, index=0, packed_dtype=BF16, unpacked_dtype=F32),
            pltpu.unpack_elementwise(w, index=1, packed_dtype=BF16, unpacked_dtype=F32))


def _outproj_kernel(oa_ref, ob_ref, x_ref, g1_ref, sc_ref, sh_ref, g2_ref, nw_ref, wo_ref,
                    ws1_ref, ws3_ref, ws2_ref, wrt_ref, eb_ref, base_ref, h_ref, code_ref, w_ref, cnt_ref,
                    carry_ref):
    @pl.when(pl.program_id(0) == 0)
    def _init():
        carry_ref[...] = jnp.zeros_like(carry_ref)

    mix = (jnp.dot(oa_ref[...], wo_ref[:MOBA_WIDTH, :], preferred_element_type=F32)
           + jnp.dot(ob_ref[...], wo_ref[MOBA_WIDTH:, :], preferred_element_type=F32))
    x1 = x_ref[...] + g1_ref[...] * mix
    ms = jnp.mean(x1 * x1, axis=-1, keepdims=True)
    h = x1 * lax.rsqrt(ms + NORM_EPS) * nw_ref[...]
    h = h * (1.0 + sc_ref[...]) + sh_ref[...]
    h_ref[...] = _pack_rows(h)
    hb = h.astype(BF16)
    logits_t = lax.dot_general(wrt_ref[...], hb, NT_DIMS, preferred_element_type=F32)
    scores = jax.nn.sigmoid(logits_t)
    a = jnp.dot(hb, ws1_ref[...], preferred_element_type=F32)
    u = jnp.dot(hb, ws3_ref[...], preferred_element_type=F32)
    shared = jnp.dot((_silu(a) * u).astype(BF16), ws2_ref[...], preferred_element_type=F32)
    base_ref[...] = x1 + g2_ref[...] * shared
    nt = eb_ref.shape[1]
    for part in range(scores.shape[1] // nt):
        cols = slice(part * nt, (part + 1) * nt)
        code, weights = _route_tile(scores[:, cols], eb_ref[...], carry_ref)
        code_ref[:, cols] = code
        w_ref[:, cols] = weights
    cnt_ref[...] = carry_ref[...]


def _outproj(oa, ob, x2, g1, sc, sh, g2, nw, wo, ws1, ws3, ws2, wrt, eb, seq):
    t = x2.shape[0]
    assert t <= SLOT_CODE_BASE
    tpb = seq // ROW_TILE
    vec = lambda: pl.BlockSpec((None, 1, D_MODEL), lambda i: (i // tpb, 0, 0))
    full = lambda a: pl.BlockSpec(a.shape, lambda i: (0,) * a.ndim)
    rows = lambda w: pl.BlockSpec((ROW_TILE, w), lambda i: (i, 0))
    tok = lambda: pl.BlockSpec((TOP_K, ROW_TILE), lambda i: (0, i))
    return pl.pallas_call(
        _outproj_kernel,
        grid=(t // ROW_TILE,),
        in_specs=[rows(MOBA_WIDTH), rows(GLA_WIDTH), rows(D_MODEL), vec(), vec(), vec(), vec(),
                  full(nw), full(wo), full(ws1), full(ws3), full(ws2), full(wrt), full(eb)],
        out_specs=[rows(D_MODEL), rows(HALF), tok(), tok(), full(eb)],
        out_shape=[jax.ShapeDtypeStruct((t, D_MODEL), F32),
                   jax.ShapeDtypeStruct((t, HALF), jnp.uint32),
                   jax.ShapeDtypeStruct((TOP_K, t), jnp.int32),
                   jax.ShapeDtypeStruct((TOP_K, t), F32),
                   jax.ShapeDtypeStruct(eb.shape, F32)],
        scratch_shapes=[pltpu.VMEM(eb.shape, F32)],
        compiler_params=_params(1),
        name="outproj",
    )(oa, ob, x2, g1, sc, sh, g2, nw, wo, ws1, ws3, ws2, wrt, eb)


SLOT_CODE_SHIFT = 16
SLOT_CODE_BASE = 1 << SLOT_CODE_SHIFT


def _route_tile(s, eb, carry_ref):
    ne, nt = s.shape
    choice = s + eb
    gio = lax.broadcasted_iota(jnp.int32, (GROUP_SIZE, nt), 0)
    gscore = []
    for g in range(N_GROUPS):
        cg = choice[g * GROUP_SIZE:(g + 1) * GROUP_SIZE, :]
        top1 = jnp.max(cg, axis=0, keepdims=True)
        first = jnp.min(jnp.where(cg == top1, gio, GROUP_SIZE), axis=0, keepdims=True)
        top2 = jnp.max(jnp.where(gio == first, -jnp.inf, cg), axis=0, keepdims=True)
        gscore.append(top1 + top2)
    gs = jnp.concatenate(gscore, axis=0)
    gidx = lax.broadcasted_iota(jnp.int32, gs.shape, 0)
    beaten = jnp.zeros(gs.shape, jnp.int32)
    for m in range(N_GROUPS):
        gm = gs[m:m + 1, :]
        beaten = beaten + jnp.where((gm > gs) | ((gm == gs) & (gidx > m)), 1, 0)
    gkeep = beaten < TOPK_GROUPS
    masked = jnp.concatenate(
        [jnp.where(gkeep[g:g + 1, :], choice[g * GROUP_SIZE:(g + 1) * GROUP_SIZE, :], -jnp.inf)
         for g in range(N_GROUPS)], axis=0)

    eio = lax.broadcasted_iota(jnp.int32, (ne, nt), 0)
    candidates = masked
    idx_rows, w_rows, hits = [], [], []
    for _ in range(TOP_K):
        mx = jnp.max(masked, axis=0, keepdims=True)
        idx = jnp.min(jnp.where(masked == mx, eio, ne), axis=0, keepdims=True)
        hit = eio == idx
        w_rows.append(jnp.sum(jnp.where(hit, s, 0.0), axis=0, keepdims=True))
        idx_rows.append(idx)
        hits.append(hit)
        masked = jnp.where(hit, -jnp.inf, masked)
    picked = jnp.where(masked < candidates, 1.0, 0.0)
    wk = jnp.concatenate(w_rows, axis=0)
    weights = wk / jnp.sum(wk, axis=0, keepdims=True) * ROUTED_SCALE

    tr = lax.broadcasted_iota(jnp.int32, (nt, nt), 0)
    tc = lax.broadcasted_iota(jnp.int32, (nt, nt), 1)
    before = jnp.where(tr < tc, 1.0, 0.0).astype(BF16)
    pb = picked.astype(BF16)
    pos = carry_ref[...] + jnp.dot(pb, before, preferred_element_type=F32)
    rank = jnp.concatenate(
        [jnp.sum(jnp.where(hit, pos, 0.0), axis=0, keepdims=True) for hit in hits], axis=0).astype(jnp.int32)
    carry_ref[...] = carry_ref[...] + jnp.dot(pb, jnp.ones((nt, nt), BF16), preferred_element_type=F32)
    return jnp.concatenate(idx_rows, axis=0) * SLOT_CODE_BASE + rank, weights


SLOT_TILE = 2048


def _slots_kernel(pstart_ref, code_ref, o_ref):
    code = code_ref[...]
    expert = lax.shift_right_logical(code, SLOT_CODE_SHIFT)

    def body(e, acc):
        return jnp.where(expert == e, pstart_ref[e], acc)

    start = lax.fori_loop(0, N_EXPERTS, body, jnp.zeros_like(code), unroll=8)
    o_ref[...] = start + (code & (SLOT_CODE_BASE - 1))


def _slots(pstart, code_t):
    k, t = code_t.shape
    return pl.pallas_call(
        _slots_kernel,
        grid_spec=pltpu.PrefetchScalarGridSpec(
            num_scalar_prefetch=1,
            grid=(t // SLOT_TILE,),
            in_specs=[pl.BlockSpec((k, SLOT_TILE), lambda i, p: (0, i))],
            out_specs=pl.BlockSpec((k, SLOT_TILE), lambda i, p: (0, i)),
        ),
        out_shape=jax.ShapeDtypeStruct((k, t), jnp.int32),
        compiler_params=_params(1),
        name="slots",
    )(pstart, code_t)


COMBINE_BATCHES = 2
SC_WINDOW = 128


def _sc_gather_rows(table, idx_flat):
    info = plsc.get_sparse_core_info()
    nw = info.num_cores * info.num_subcores
    n = idx_flat.shape[0]
    width = table.shape[1]
    per_worker = n // nw
    assert per_worker * nw == n and per_worker % SC_WINDOW == 0
    mesh = plsc.VectorSubcoreMesh(core_axis_name="c", subcore_axis_name="s")

    def body(table_hbm, idx_hbm, out_hbm, idx_v, rows_v, sem):
        wid = lax.axis_index("s") * info.num_cores + lax.axis_index("c")
        base = wid * per_worker

        @pl.loop(0, per_worker // SC_WINDOW)
        def _(w):
            off = pl.multiple_of(base + w * SC_WINDOW, SC_WINDOW)
            pltpu.sync_copy(idx_hbm.at[pl.ds(off, SC_WINDOW)], idx_v)
            pltpu.async_copy(table_hbm.at[idx_v], rows_v, sem).wait()
            pltpu.sync_copy(rows_v, out_hbm.at[pl.ds(off, SC_WINDOW)])

    return pl.kernel(
        body,
        out_type=jax.ShapeDtypeStruct((n, width), table.dtype),
        mesh=mesh,
        scratch_types=[pltpu.VMEM((SC_WINDOW,), jnp.int32),
                       pltpu.VMEM((SC_WINDOW, width), table.dtype),
                       pltpu.SemaphoreType.DMA],
        name="sc_gather",
    )(table, idx_flat)


def _sc_scatter_rows(rows, idx_kt, n_out):
    info = plsc.get_sparse_core_info()
    nw = info.num_cores * info.num_subcores
    t, width = rows.shape
    nk = idx_kt.shape[0]
    per_worker = t // nw
    assert per_worker * nw == t and per_worker % SC_WINDOW == 0
    mesh = plsc.VectorSubcoreMesh(core_axis_name="c", subcore_axis_name="s")

    def body(rows_hbm, idx_hbm, out_hbm, idx_v, rows_v, sem):
        wid = lax.axis_index("s") * info.num_cores + lax.axis_index("c")
        base = wid * per_worker

        @pl.loop(0, per_worker // SC_WINDOW)
        def _(w):
            off = pl.multiple_of(base + w * SC_WINDOW, SC_WINDOW)
            pltpu.sync_copy(rows_hbm.at[pl.ds(off, SC_WINDOW)], rows_v)
            pltpu.sync_copy(idx_hbm.at[:, pl.ds(off, SC_WINDOW)], idx_v)
            copies = [pltpu.async_copy(rows_v, out_hbm.at[idx_v.at[k]], sem) for k in range(nk)]
            for cp in copies:
                cp.wait()

    return pl.kernel(
        body,
        out_type=jax.ShapeDtypeStruct((n_out, width), rows.dtype),
        mesh=mesh,
        scratch_types=[pltpu.VMEM((nk, SC_WINDOW), jnp.int32),
                       pltpu.VMEM((SC_WINDOW, width), rows.dtype),
                       pltpu.SemaphoreType.DMA],
        name="sc_scatter",
    )(rows, idx_kt)


def _expert_kernel(first_ref, count_ref, used_ref, w1_ref, w3_ref, w2_ref, xs_ref, ys_ref,
                   xbuf, ybuf, sem_in, sem_out, w1f, w3f, w2f, sem_w, w1b, w3b, w2b, *, layer):
    e = pl.program_id(0)
    ne = pl.num_programs(0)
    r = xbuf.shape[1]
    n_used = used_ref[0]

    def x_copy(g, slot):
        return pltpu.make_async_copy(xs_ref.at[pl.ds(pl.multiple_of(g * r, r), r), :], xbuf.at[slot], sem_in.at[slot])

    def y_copy(g, slot):
        return pltpu.make_async_copy(ybuf.at[slot], ys_ref.at[pl.ds(pl.multiple_of(g * r, r), r), :], sem_out.at[slot])

    def w_copies(ex, slot):
        return [pltpu.make_async_copy(src.at[layer, ex], dst.at[slot], sem_w.at[slot])
                for src, dst in ((w1_ref, w1f), (w3_ref, w3f), (w2_ref, w2f))]

    nin = xbuf.shape[0]
    nout = ybuf.shape[0]

    nw = w1f.shape[0]

    @pl.when(e == 0)
    def _first_reads():
        for g in range(nin):
            @pl.when(g < n_used)
            def _(g=g):
                x_copy(g, g).start()
        for ex in range(nw - 1):
            for cp in w_copies(ex, ex):
                cp.start(priority=1)

    @pl.when(e + nw - 1 < ne)
    def _weights_ahead():
        for cp in w_copies(e + nw - 1, (e + nw - 1) % nw):
            cp.start(priority=1)

    wslot = e % nw
    for cp in w_copies(e, wslot):
        cp.wait()

    n = count_ref[e]

    @pl.when(n > 0)
    def _cast_weights():
        w1b[...] = w1f[wslot].astype(BF16)
        w3b[...] = w3f[wslot].astype(BF16)
        w2b[...] = w2f[wslot].astype(BF16)

    def run_tiles(g, count):
        tiles = [g + j for j in range(count)]
        for gj in tiles:
            x_copy(gj, gj % nin).wait()
        x = jnp.concatenate([xbuf[gj % nin] for gj in tiles], axis=0)
        for gj in tiles:
            @pl.when(gj + nin < n_used)
            def _(gj=gj):
                x_copy(gj + nin, gj % nin).start()
        lo, hi = _unpack_rows(x)
        lo, hi = lo.astype(BF16), hi.astype(BF16)
        a = (jnp.dot(lo, w1b[:HALF, :], preferred_element_type=F32)
             + jnp.dot(hi, w1b[HALF:, :], preferred_element_type=F32))
        u = (jnp.dot(lo, w3b[:HALF, :], preferred_element_type=F32)
             + jnp.dot(hi, w3b[HALF:, :], preferred_element_type=F32))
        y = _pack_rows(jnp.dot((_silu(a) * u).astype(BF16), w2b[...], preferred_element_type=F32))
        for j, gj in enumerate(tiles):
            @pl.when(gj >= nout)
            def _(gj=gj):
                y_copy(gj - nout, gj % nout).wait()
            ybuf[gj % nout] = y[j * r:(j + 1) * r]
            y_copy(gj, gj % nout).start()

    g0 = first_ref[e]
    pair = EXPERT_TILES_PER_MATMUL

    def pair_body(p, carry):
        run_tiles(g0 + p * pair, pair)
        return carry

    lax.fori_loop(0, n // pair, pair_body, 0)
    for left in range(1, pair):
        pl.when(n % pair == left)(functools.partial(run_tiles, g0 + n - left, left))

    @pl.when(e == ne - 1)
    def _drain_writes():
        for back in range(nout, 0, -1):
            @pl.when(n_used >= back)
            def _(back=back):
                y_copy(n_used - back, (n_used - back) % nout).wait()


def _experts(tile_first, tile_count, n_used, xs, w1, w3, w2, layer):
    n_rows = xs.shape[0]
    r = EXPERT_TILE
    any_spec = pl.BlockSpec(memory_space=pl.ANY)
    return pl.pallas_call(
        functools.partial(_expert_kernel, layer=layer),
        grid_spec=pltpu.PrefetchScalarGridSpec(
            num_scalar_prefetch=3,
            grid=(N_EXPERTS,),
            in_specs=[any_spec, any_spec, any_spec, any_spec],
            out_specs=any_spec,
            scratch_shapes=[pltpu.VMEM((EXPERT_IN_RING, r, HALF), jnp.uint32),
                            pltpu.VMEM((EXPERT_OUT_RING, r, HALF), jnp.uint32),
                            pltpu.SemaphoreType.DMA((EXPERT_IN_RING,)),
                            pltpu.SemaphoreType.DMA((EXPERT_OUT_RING,)),
                            pltpu.VMEM((EXPERT_WEIGHT_RING, D_MODEL, EXPERT_FF), F32),
                            pltpu.VMEM((EXPERT_WEIGHT_RING, D_MODEL, EXPERT_FF), F32),
                            pltpu.VMEM((EXPERT_WEIGHT_RING, EXPERT_FF, D_MODEL), F32),
                            pltpu.SemaphoreType.DMA((EXPERT_WEIGHT_RING,)),
                            pltpu.VMEM((D_MODEL, EXPERT_FF), BF16),
                            pltpu.VMEM((D_MODEL, EXPERT_FF), BF16),
                            pltpu.VMEM((EXPERT_FF, D_MODEL), BF16)],
        ),
        out_shape=jax.ShapeDtypeStruct((n_rows, HALF), jnp.uint32),
        compiler_params=_params(1),
        name="experts",
    )(tile_first, tile_count, n_used, w1, w3, w2, xs)


def _combine_dense_kernel(base_ref, g2_ref, w_ref, yg_ref, o_ref):
    acc_lo = acc_hi = None
    for k in range(TOP_K):
        lo, hi = _unpack_rows(yg_ref[k])
        wk = w_ref[:, k:k + 1]
        acc_lo = wk * lo if acc_lo is None else acc_lo + wk * lo
        acc_hi = wk * hi if acc_hi is None else acc_hi + wk * hi
    o_ref[:, :HALF] = base_ref[:, :HALF] + g2_ref[:, :HALF] * acc_lo
    o_ref[:, HALF:] = base_ref[:, HALF:] + g2_ref[:, HALF:] * acc_hi


def _combine_dense(base, g2, w_tok, yg, batch, seq):
    t = base.shape[0]
    nt = ROUTE_TILE
    tpb = seq // nt
    rows = lambda i: (batch * tpb + i, 0)
    return pl.pallas_call(
        _combine_dense_kernel,
        grid=(yg.shape[1] // nt,),
        in_specs=[pl.BlockSpec((nt, D_MODEL), rows),
                  pl.BlockSpec((None, 1, D_MODEL), lambda i: (batch + i // tpb, 0, 0)),
                  pl.BlockSpec((nt, TOP_K), rows),
                  pl.BlockSpec((TOP_K, nt, HALF), lambda i: (0, i, 0))],
        out_specs=pl.BlockSpec((nt, D_MODEL), rows),
        out_shape=jax.ShapeDtypeStruct((t, D_MODEL), F32),
        input_output_aliases={0: 0},
        compiler_params=_params(1),
        name="combine_dense",
    )(base, g2, w_tok, yg)


def _layer(layer, x, c, w_ada, b_ada, norm1_w, norm2_w, w_in, q_norm_w, k_norm_w, rel_bias, w_alpha, b_alpha,
           moba_out_w, gla_out_w, w_out, w_router, e_bias, w1, w3, w2, ws1, ws3, ws2):
    b, s, d = x.shape
    t = b * s
    x2 = x.reshape(t, d)

    mod = _mod(c, w_ada, b_ada)
    sh1, sc1, g1, sh2, sc2, g2 = [mod[:, j * d:(j + 1) * d].reshape(b, 1, d) for j in range(6)]

    w_main = w_in[:, :D_MAIN].astype(BF16)
    w_ga = jnp.zeros((d, LANES), BF16).at[:, :GLA_GATE_RANK].set(w_in[:, D_MAIN:].astype(BF16))
    per_chunk = 256 // MOBA_HEAD_DIM
    qw = jnp.tile(q_norm_w.astype(F32), per_chunk).reshape(1, 256) * (MOBA_HEAD_DIM ** -0.5 * LOG2E)
    kw = jnp.tile(k_norm_w.astype(F32), per_chunk).reshape(1, 256)
    proj, ga = _inproj(x2, sc1, sh1, norm1_w.reshape(1, d), w_main, w_ga, qw, kw, s)
    proj3 = proj.reshape(b, s, D_MAIN)

    near, far = _moba_bias_tables(rel_bias)
    ow = jnp.tile(moba_out_w.astype(F32), 2).reshape(1, LANES)
    o_a = _moba(proj3, near, far, ow)

    wal = jnp.zeros((LANES, GLA_KEY_WIDTH), F32).at[:GLA_GATE_RANK].set(w_alpha)
    o_b = _gla(proj3, ga.reshape(b, s, LANES), wal, b_alpha.reshape(1, GLA_KEY_WIDTH),
               gla_out_w.reshape(1, GLA_DV))

    eb = jnp.broadcast_to(e_bias.astype(F32)[:, None], (N_EXPERTS, ROUTE_TILE))
    base, h2, code_t, w_t, counts = _outproj(
        o_a.reshape(t, MOBA_WIDTH), o_b.reshape(t, GLA_WIDTH), x2, g1, sc2, sh2, g2,
        norm2_w.reshape(1, d), w_out.astype(BF16), ws1.astype(BF16), ws3.astype(BF16), ws2.astype(BF16),
        w_router.T.astype(BF16), eb, s)

    r = EXPERT_TILE
    n_tiles = (t * TOP_K + N_EXPERTS * (r - 1) + r - 1) // r
    n_rows = n_tiles * r
    cnt = counts[:, 0].astype(jnp.int32)
    padded = (cnt + r - 1) // r * r
    pend = jnp.cumsum(padded)
    pstart = pend - padded
    n_used = (pend[-1:] // r).astype(jnp.int32)
    dest_t = _slots(pstart, code_t)

    xs = _sc_scatter_rows(h2, dest_t, n_rows)
    ys = _experts(pstart // r, padded // r, n_used, xs, w1, w3, w2, layer)
    w_tok = w_t.T
    out = base
    per = COMBINE_BATCHES * s
    for bi in range(0, b, COMBINE_BATCHES):
        idx = dest_t[:, bi * s:bi * s + per].reshape(TOP_K * per)
        yg = _sc_gather_rows(ys, idx).reshape(TOP_K, per, HALF)
        out = _combine_dense(out, g2, w_tok, yg, bi, s)
    return out.reshape(b, s, d)


def kernel(x, c, w_ada, b_ada, norm1_w, norm2_w, w_in, q_norm_w, k_norm_w, rel_bias, w_alpha, b_alpha,
           moba_out_w, gla_out_w, w_out, w_router, e_bias, w1, w3, w2, ws1, ws3, ws2):
    for l in range(w_ada.shape[0]):
        x = _layer(l, x, c, w_ada[l], b_ada[l], norm1_w[l], norm2_w[l], w_in[l], q_norm_w[l], k_norm_w[l],
                   rel_bias, w_alpha[l], b_alpha[l], moba_out_w[l], gla_out_w[l], w_out[l], w_router[l],
                   e_bias[l], w1, w3, w2, ws1[l], ws3[l], ws2[l])
    return x
```

```python
import functools
import math

import numpy as np
import jax
import jax.numpy as jnp
from jax import lax
from jax.experimental import pallas as pl
from jax.experimental.pallas import tpu as pltpu
from jax.experimental.pallas import tpu_sc as plsc

D_MODEL = 1024
MOBA_HEADS = 8
MOBA_HEAD_DIM = 64
MOBA_WIDTH = MOBA_HEADS * MOBA_HEAD_DIM
MOBA_BLOCK = 256
MOBA_TOPK = 3
GLA_HEADS = 4
GLA_DK = 64
GLA_DV = 128
GLA_KEY_WIDTH = GLA_HEADS * GLA_DK
GLA_WIDTH = GLA_HEADS * GLA_DV
GLA_GATE_RANK = 16
GLA_GATE_TAU = 16.0
GLA_CHUNK = 64
REL_BUCKETS = 32
REL_MAX_DIST = 128
N_EXPERTS = 256
TOP_K = 8
N_GROUPS = 8
TOPK_GROUPS = 4
GROUP_SIZE = N_EXPERTS // N_GROUPS
EXPERT_FF = 256
SHARED_FF = 256
ROUTED_SCALE = 2.5
NORM_EPS = 1e-6
LOG2E = math.log2(math.e)

D_MAIN = 3 * MOBA_WIDTH + 2 * GLA_KEY_WIDTH + 2 * GLA_WIDTH
LANES = 128
VMEM_LIMIT = 56 * 1024 * 1024

ROW_TILE = 512
ROUTE_TILE = 256
EXPERT_TILE = 128
EXPERT_TILES_PER_MATMUL = 6
EXPERT_IN_RING = 16
EXPERT_OUT_RING = 12
EXPERT_WEIGHT_RING = 4

F32 = jnp.float32
BF16 = jnp.bfloat16
NT_DIMS = (((1,), (1,)), ((), ()))
TN_DIMS = (((0,), (0,)), ((), ()))


def _params(n_axes):
    return pltpu.CompilerParams(dimension_semantics=("arbitrary",) * n_axes,
                                vmem_limit_bytes=VMEM_LIMIT)


def _silu(v):
    return v * jax.nn.sigmoid(v)


def _mod_kernel(c_ref, w_ref, b_ref, o_ref):
    o_ref[...] = jnp.dot(_silu(c_ref[...]), w_ref[...], preferred_element_type=F32) + b_ref[...]


def _mod(c, w, b):
    rows = 8
    cp = jnp.zeros((rows, D_MODEL), F32).at[:c.shape[0]].set(c)
    n = w.shape[1]
    tn = 1024
    out = pl.pallas_call(
        _mod_kernel,
        grid=(n // tn,),
        in_specs=[pl.BlockSpec((rows, D_MODEL), lambda j: (0, 0)),
                  pl.BlockSpec((D_MODEL, tn), lambda j: (0, j)),
                  pl.BlockSpec((1, tn), lambda j: (0, j))],
        out_specs=pl.BlockSpec((rows, tn), lambda j: (0, j)),
        out_shape=jax.ShapeDtypeStruct((rows, n), F32),
        compiler_params=_params(1),
        name="mod",
    )(cp, w, b.reshape(1, n))
    return out[:c.shape[0]]


def _group_rms_inv(a, group):
    lane = lax.broadcasted_iota(jnp.int32, (1, a.shape[1]), 1)
    a2 = a * a
    inv = jnp.zeros_like(a)
    for g in range(a.shape[1] // group):
        m = (lane >= g * group) & (lane < (g + 1) * group)
        ss = jnp.sum(jnp.where(m, a2, 0.0), axis=-1, keepdims=True)
        inv = jnp.where(m, lax.rsqrt(ss * (1.0 / group) + NORM_EPS), inv)
    return inv


def _inproj_kernel(x_ref, sc_ref, sh_ref, nw_ref, w_ref, wga_ref, qw_ref, kw_ref, o_ref, ga_ref):
    x = x_ref[...]
    ms = jnp.mean(x * x, axis=-1, keepdims=True)
    h = x * lax.rsqrt(ms + NORM_EPS) * nw_ref[...]
    h = h * (1.0 + sc_ref[...]) + sh_ref[...]
    hb = h.astype(BF16)
    cw = 256
    for j in range(D_MAIN // cw):
        acc = jnp.dot(hb, w_ref[:, j * cw:(j + 1) * cw], preferred_element_type=F32)
        if j < 2 * MOBA_WIDTH // cw:
            nw = qw_ref if j < MOBA_WIDTH // cw else kw_ref
            acc = acc * _group_rms_inv(acc, MOBA_HEAD_DIM) * nw[...]
        o_ref[:, j * cw:(j + 1) * cw] = acc.astype(BF16)
    ga_ref[...] = lax.dot_general(wga_ref[...], hb, NT_DIMS, preferred_element_type=F32)


def _inproj(x2, sc, sh, nw, w_main, w_ga, qw, kw, seq):
    t = x2.shape[0]
    tpb = seq // ROW_TILE
    vec = lambda: pl.BlockSpec((None, 1, D_MODEL), lambda i: (i // tpb, 0, 0))
    full = lambda a: pl.BlockSpec(a.shape, lambda i: (0,) * a.ndim)
    return pl.pallas_call(
        _inproj_kernel,
        grid=(t // ROW_TILE,),
        in_specs=[pl.BlockSpec((ROW_TILE, D_MODEL), lambda i: (i, 0)), vec(), vec(),
                  full(nw), full(w_main), full(w_ga), full(qw), full(kw)],
        out_specs=[pl.BlockSpec((ROW_TILE, D_MAIN), lambda i: (i, 0)),
                   pl.BlockSpec((GLA_GATE_RANK, ROW_TILE), lambda i: (0, i))],
        out_shape=[jax.ShapeDtypeStruct((t, D_MAIN), BF16),
                   jax.ShapeDtypeStruct((GLA_GATE_RANK, t), F32)],
        compiler_params=_params(1),
        name="inproj",
    )(x2, sc, sh, nw, w_main, w_ga, qw, kw)


def _t5_bucket_np(rel):
    max_exact = REL_BUCKETS // 2
    relf = np.maximum(rel, 1).astype(np.float64)
    large = max_exact + (np.log(relf / max_exact) / math.log(REL_MAX_DIST / max_exact)
                         * (REL_BUCKETS - max_exact)).astype(np.int32)
    large = np.minimum(large, REL_BUCKETS - 1)
    return np.where(rel < max_exact, rel, large)


def _bias_kernel(rb_ref, idx_ref, o_ref):
    h = pl.program_id(0)
    idx = idx_ref[...]
    tab = jnp.full(idx.shape, -jnp.inf, F32)
    for bk in range(REL_BUCKETS):
        tab = jnp.where(idx == bk, rb_ref[bk * MOBA_HEADS + h], tab)
    o_ref[...] = tab


def _moba_bias_tables(rel_bias):
    j = np.arange(MOBA_BLOCK)[:, None]
    i = np.arange(MOBA_BLOCK)[None, :]
    own_idx = np.where(j <= i, _t5_bucket_np(np.maximum(i - j, 0)), -1)
    prev_idx = _t5_bucket_np(MOBA_BLOCK + i - j)
    idx = jnp.asarray(np.concatenate([prev_idx, own_idx], axis=0).astype(np.int32))
    assert int(_t5_bucket_np(np.array([MOBA_BLOCK + 1]))[0]) == REL_BUCKETS - 1
    rb = rel_bias.astype(F32) * LOG2E
    near = pl.pallas_call(
        _bias_kernel,
        grid=(MOBA_HEADS,),
        in_specs=[pl.BlockSpec(memory_space=pltpu.SMEM),
                  pl.BlockSpec(idx.shape, lambda h: (0, 0))],
        out_specs=pl.BlockSpec((None,) + idx.shape, lambda h: (h, 0, 0)),
        out_shape=jax.ShapeDtypeStruct((MOBA_HEADS,) + idx.shape, F32),
        compiler_params=_params(1),
        name="bias",
    )(rb.reshape(-1), idx)
    return near, rb[REL_BUCKETS - 1]


FAR_GROUP = 4


MOBA_INTERLEAVE = 4


def _moba_kernel(*refs):
    hp = pl.program_id(1)
    nsets = refs[2].shape[0] // (MOBA_INTERLEAVE * MOBA_BLOCK)
    _moba_body(None, hp, *refs, prepare=True)

    def block_set(j, carry):
        for jj in range(nsets):
            blocks = list(range(jj * MOBA_INTERLEAVE, (jj + 1) * MOBA_INTERLEAVE))
            pl.when(j == jj)(functools.partial(_moba_body, blocks, hp, *refs, prepare=False))
        return carry

    lax.fori_loop(0, nsets, block_set, 0)


def _moba_body(blocks, hp, far_ref, q_ref, k_ref, v_ref, near_ref, ow_ref, o_ref,
               vt_ref, vtg_ref, acc_ref, m_ref, sel_ref, s_ref, mx_ref, *, prepare):
    nblk = k_ref.shape[0] // MOBA_BLOCK
    ngrp = nblk // FAR_GROUP
    hd = MOBA_HEAD_DIM
    bs = MOBA_BLOCK
    lane = lax.broadcasted_iota(jnp.int32, (bs, LANES), 1)

    def split_heads(qb):
        zero = jnp.zeros_like(qb)
        return jnp.where(lane < hd, qb, zero), jnp.where(lane < hd, zero, qb)

    def _prepare():
        row = lax.broadcasted_iota(jnp.int32, (LANES, bs), 0)
        kmeans = []
        for n in range(nblk):
            kb = k_ref[n * bs:(n + 1) * bs, :].astype(F32)
            kmeans.append(jnp.mean(kb, axis=0, keepdims=True))
            vt = v_ref[n * bs:(n + 1) * bs, :].astype(F32).T
            vt0 = jnp.where(row < hd, vt, 1.0).astype(BF16)
            vt1 = jnp.where(row < hd, 1.0, vt).astype(BF16)
            vt_ref[0, n] = vt0
            vt_ref[1, n] = vt1
            gcols = slice((n % FAR_GROUP) * bs, (n % FAR_GROUP + 1) * bs)
            vtg_ref[0, n // FAR_GROUP, :, gcols] = vt0
            vtg_ref[1, n // FAR_GROUP, :, gcols] = vt1
        kmean = jnp.concatenate(kmeans, axis=0)
        km_hi = kmean.astype(BF16)
        km_lo = (kmean - km_hi.astype(F32)).astype(BF16)
        blk = lax.broadcasted_iota(jnp.int32, (nblk, bs), 0)
        for ib in range(nblk):
            qparts = split_heads(q_ref[ib * bs:(ib + 1) * bs, :])
            for h in range(2):
                gt = (lax.dot_general(km_hi, qparts[h], NT_DIMS, preferred_element_type=F32)
                      + lax.dot_general(km_lo, qparts[h], NT_DIMS, preferred_element_type=F32))
                gt = jnp.where(blk < ib, gt, -jnp.inf)
                cnt = jnp.zeros(gt.shape, jnp.int32)
                for m in range(ib):
                    gm = gt[m:m + 1, :]
                    cnt = cnt + jnp.where((gm > gt) | ((gm == gt) & (blk > m)), 1, 0)
                keep = (blk < ib) & (cnt < MOBA_TOPK)
                sel_ref[0, h, ib] = jnp.where(keep, 1.0, 0.0)
                sel_ref[1, h, ib] = jnp.where(keep & (blk < ib - 1), 1.0, 0.0)

    if prepare:
        _prepare()
        return

    gk = FAR_GROUP * bs

    def rows_of(ib):
        if isinstance(ib, int):
            return slice(ib * bs, (ib + 1) * bs)
        return pl.ds(pl.multiple_of(ib * bs, bs), bs)

    def finish(ib, slot):
        a0 = acc_ref[slot, 0]
        a1 = acc_ref[slot, 1]
        row = lax.broadcasted_iota(jnp.int32, a0.shape, 0)
        ot = jnp.where(row < hd, a0 / a0[hd:hd + 1, :], a1 / a1[0:1, :])
        o2 = ot * ot
        ss0 = jnp.sum(jnp.where(row < hd, o2, 0.0), axis=0, keepdims=True)
        ss1 = jnp.sum(jnp.where(row < hd, 0.0, o2), axis=0, keepdims=True)
        inv = jnp.where(row < hd, lax.rsqrt(ss0 * (1.0 / hd) + NORM_EPS), lax.rsqrt(ss1 * (1.0 / hd) + NORM_EPS))
        o_ref[rows_of(ib), :] = ((ot * inv).T * ow_ref[...]).astype(o_ref.dtype)

    def far_scores(g, slot, qh, nb):
        kb = k_ref[g * gk:g * gk + nb * bs, :]
        for h in range(2):
            s = lax.dot_general(kb, qh[h], NT_DIMS, preferred_element_type=F32)
            s_ref[slot, g % 2, h, 0:nb * bs, :] = s
            for j in range(nb):
                mx_ref[slot, g % 2, h, j] = jnp.max(s[j * bs:(j + 1) * bs], axis=0, keepdims=True)

    def near_scores(ib, qh):
        kbs = (k_ref[rows_of(ib - 1), :], k_ref[rows_of(ib), :])
        return [[lax.dot_general(kbs[w], qh[h], NT_DIMS, preferred_element_type=F32)
                 + near_ref[h, w * bs:(w + 1) * bs, :] for w in range(2)] for h in range(2)]

    def near_values(ib, slot, ss):
        ps, ms = [], []
        for h in range(2):
            s_prev, s_own = ss[h]
            keep = sel_ref[0, h, ib, pl.ds(ib - 1, 1), :] > 0.5
            mx = jnp.where(keep, jnp.max(s_prev, axis=0, keepdims=True), -jnp.inf)
            m_new = jnp.maximum(jnp.max(s_own, axis=0, keepdims=True), mx)
            ps.append((jnp.exp2(s_prev - jnp.where(keep, m_new, jnp.inf)).astype(BF16),
                       jnp.exp2(s_own - m_new).astype(BF16)))
            ms.append(m_new)
        for h in range(2):
            acc_ref[slot, h] = (jnp.dot(vt_ref[h, ib - 1], ps[h][0], preferred_element_type=F32)
                                + jnp.dot(vt_ref[h, ib], ps[h][1], preferred_element_type=F32))
            m_ref[slot, h] = ms[h]

    def far_group(g, ib, slot, nb):
        for h in range(2):
            fb = far_ref[2 * hp + h]
            m_old = m_ref[slot, h]
            m_new = m_old
            keeps = []
            for j in range(nb):
                keep = sel_ref[1, h, ib, pl.ds(g * FAR_GROUP + j, 1), :] > 0.5
                m_new = jnp.maximum(m_new, jnp.where(keep, mx_ref[slot, g % 2, h, j] + fb, -jnp.inf))
                keeps.append(keep)
            p = jnp.concatenate(
                [jnp.exp2(s_ref[slot, g % 2, h, j * bs:(j + 1) * bs, :]
                          - jnp.where(keeps[j], m_new - fb, jnp.inf)).astype(BF16)
                 for j in range(nb)], axis=0)
            pv = jnp.dot(vtg_ref[h, g, :, 0:nb * bs], p, preferred_element_type=F32)
            acc_ref[slot, h] = acc_ref[slot, h] * jnp.exp2(m_old - m_new) + pv
            m_ref[slot, h] = m_new

    def step_body(blocks):
        slots = list(enumerate(blocks))
        nbs = [[min(FAR_GROUP, ib - 1 - g * FAR_GROUP) for g in range((ib - 2) // FAR_GROUP + 1)] if ib >= 2 else []
               for ib in blocks]
        qhs = [split_heads(q_ref[rows_of(ib), :]) for ib in blocks]
        near = [near_scores(ib, qhs[slot]) for slot, ib in slots]
        for slot, ib in slots:
            if nbs[slot]:
                far_scores(0, slot, qhs[slot], nbs[slot][0])
        for slot, ib in slots:
            near_values(ib, slot, near[slot])
        for g in range(max(len(n) for n in nbs)):
            for slot, ib in slots:
                if g + 1 < len(nbs[slot]):
                    far_scores(g + 1, slot, qhs[slot], nbs[slot][g + 1])
            for slot, ib in slots:
                if g < len(nbs[slot]):
                    far_group(g, ib, slot, nbs[slot][g])
        for slot, ib in slots:
            finish(ib, slot)

    if blocks[0] == 0:
        qh = split_heads(q_ref[0:bs, :])
        kb = k_ref[0:bs, :]
        for h in range(2):
            s = lax.dot_general(kb, qh[h], NT_DIMS, preferred_element_type=F32) + near_ref[h, bs:2 * bs, :]
            p = jnp.exp2(s - jnp.max(s, axis=0, keepdims=True)).astype(BF16)
            acc_ref[MOBA_INTERLEAVE - 1, h] = jnp.dot(vt_ref[h, 0], p, preferred_element_type=F32)
        finish(0, MOBA_INTERLEAVE - 1)
        blocks = blocks[1:]
    step_body(blocks)


def _moba(proj3, near, far, ow):
    b, s, _ = proj3.shape
    nblk = s // MOBA_BLOCK
    assert nblk % FAR_GROUP == 0
    npair = MOBA_HEADS // 2
    kcol = MOBA_WIDTH // LANES
    return pl.pallas_call(
        _moba_kernel,
        grid=(b, npair),
        in_specs=[pl.BlockSpec(memory_space=pltpu.SMEM),
                  pl.BlockSpec((None, s, LANES), lambda bb, hp: (bb, 0, hp)),
                  pl.BlockSpec((None, s, LANES), lambda bb, hp: (bb, 0, kcol + hp)),
                  pl.BlockSpec((None, s, LANES), lambda bb, hp: (bb, 0, 2 * kcol + hp)),
                  pl.BlockSpec((2, 2 * MOBA_BLOCK, MOBA_BLOCK), lambda bb, hp: (hp, 0, 0)),
                  pl.BlockSpec((1, LANES), lambda bb, hp: (0, 0))],
        out_specs=pl.BlockSpec((None, s, LANES), lambda bb, hp: (bb, 0, hp)),
        out_shape=jax.ShapeDtypeStruct((b, s, MOBA_WIDTH), BF16),
        scratch_shapes=[pltpu.VMEM((2, nblk, LANES, MOBA_BLOCK), BF16),
                        pltpu.VMEM((2, nblk // FAR_GROUP, LANES, FAR_GROUP * MOBA_BLOCK), BF16),
                        pltpu.VMEM((MOBA_INTERLEAVE, 2, LANES, MOBA_BLOCK), F32),
                        pltpu.VMEM((MOBA_INTERLEAVE, 2, 1, MOBA_BLOCK), F32),
                        pltpu.VMEM((2, 2, nblk, nblk, MOBA_BLOCK), F32),
                        pltpu.VMEM((MOBA_INTERLEAVE, 2, 2, FAR_GROUP * MOBA_BLOCK, MOBA_BLOCK), F32),
                        pltpu.VMEM((MOBA_INTERLEAVE, 2, 2, FAR_GROUP, 1, MOBA_BLOCK), F32)],
        compiler_params=_params(2),
        name="moba",
    )(far, proj3, proj3, proj3, near, ow)


def _split3(v):
    hi = v.astype(BF16)
    r1 = v - hi.astype(F32)
    mid = r1.astype(BF16)
    lo = (r1 - mid.astype(F32)).astype(BF16)
    return hi, mid, lo


GLA_UNROLL = 16


def _gla_kernel(q_ref, k_ref, v_ref, g_ref, ga_ref, wal_ref, bal_ref, gw_ref, o_ref, b_ref, st_ref):
    seq = q_ref.shape[0]
    c = GLA_CHUNK
    pc = 256

    rr = lax.broadcasted_iota(jnp.int32, (pc, pc), 0)
    cc = lax.broadcasted_iota(jnp.int32, (pc, pc), 1)
    tri = jnp.where((rr >= cc) & (rr // c == cc // c), 1.0, 0.0).astype(BF16)

    for j in range(seq // (pc * GLA_UNROLL)):
        rows = [slice((j * GLA_UNROLL + u) * pc, (j * GLA_UNROLL + u + 1) * pc) for u in range(GLA_UNROLL)]
        xg = [lax.dot_general(ga_ref[:, r], wal_ref[...], TN_DIMS, preferred_element_type=F32) + bal_ref[...]
              for r in rows]
        parts = [_split3((jnp.minimum(x, 0.0) - jnp.log(1.0 + jnp.exp(-jnp.abs(x)))) * (1.0 / GLA_GATE_TAU))
                 for x in xg]
        sums = [[jnp.dot(tri, term, preferred_element_type=F32) for term in p] for p in parts]
        for r, (hi, mid, lo) in zip(rows, sums):
            b_ref[r, :] = hi + mid + lo

    st_ref[...] = jnp.zeros_like(st_ref)
    lane = lax.broadcasted_iota(jnp.int32, (c, LANES), 1)
    head_mask = (lane < GLA_DK, lane >= GLA_DK)
    causal = lax.broadcasted_iota(jnp.int32, (c, c), 0) >= lax.broadcasted_iota(jnp.int32, (c, c), 1)

    units = [(u, h) for u in range(GLA_UNROLL) for h in range(2)]

    def chunk_body(ci, carry):
        rows = [pl.ds(pl.multiple_of((ci * GLA_UNROLL + u) * c, c), c) for u in range(GLA_UNROLL)]
        qt, kt, qs, ke, e_last = [], [], [], [], []
        for u in range(GLA_UNROLL):
            b = b_ref[rows[u], :]
            ref_row = b[c // 2 - 1:c // 2, :]
            last = b[c - 1:c, :]
            q = q_ref[rows[u], :].astype(F32) * (GLA_DK ** -0.5)
            k = k_ref[rows[u], :].astype(F32)
            qt.append(q * jnp.exp(b - ref_row))
            kt.append((k * jnp.exp(ref_row - b)).astype(BF16))
            qs.append(q * jnp.exp(b))
            ke.append((k * jnp.exp(last - b)).astype(BF16))
            e_last.append(jnp.exp(last))
        vs = {(u, h): v_ref[rows[u], h * GLA_DV:(h + 1) * GLA_DV] for u, h in units}
        a = {(u, h): lax.dot_general(jnp.where(head_mask[h], qt[u], 0.0).astype(BF16), kt[u], NT_DIMS,
                                     preferred_element_type=F32) for u, h in units}
        inc = {(u, h): lax.dot_general(vs[u, h], ke[u], TN_DIMS, preferred_element_type=F32) for u, h in units}
        o = {(u, h): jnp.dot(jnp.where(causal, a[u, h], 0.0).astype(BF16), vs[u, h], preferred_element_type=F32)
             for u, h in units}
        states = {}
        for h in range(2):
            st = st_ref[h]
            for u in range(GLA_UNROLL):
                states[u, h] = st
                st = st * e_last[u] + inc[u, h]
            st_ref[h] = st
        for u, h in units:
            cols = slice(h * GLA_DV, (h + 1) * GLA_DV)
            ou = o[u, h] + lax.dot_general(jnp.where(head_mask[h], qs[u], 0.0).astype(BF16),
                                           states[u, h].astype(BF16), NT_DIMS, preferred_element_type=F32)
            ms = jnp.mean(ou * ou, axis=-1, keepdims=True)
            on = ou * lax.rsqrt(ms + NORM_EPS) * gw_ref[...]
            g = g_ref[rows[u], cols].astype(F32)
            o_ref[rows[u], cols] = (on * _silu(g)).astype(o_ref.dtype)
        return carry

    lax.fori_loop(0, seq // (c * GLA_UNROLL), chunk_body, 0)


def _gla(proj3, ga3, wal, bal, gw):
    b, s, _ = proj3.shape
    npair = GLA_HEADS // 2
    qcol = 3 * MOBA_WIDTH // LANES
    kcol = qcol + GLA_KEY_WIDTH // LANES
    vcol = (3 * MOBA_WIDTH + 2 * GLA_KEY_WIDTH) // (2 * GLA_DV)
    gcol = vcol + npair
    return pl.pallas_call(
        _gla_kernel,
        grid=(b, npair),
        in_specs=[pl.BlockSpec((None, s, LANES), lambda bb, hp: (bb, 0, qcol + hp)),
                  pl.BlockSpec((None, s, LANES), lambda bb, hp: (bb, 0, kcol + hp)),
                  pl.BlockSpec((None, s, 2 * GLA_DV), lambda bb, hp: (bb, 0, vcol + hp)),
                  pl.BlockSpec((None, s, 2 * GLA_DV), lambda bb, hp: (bb, 0, gcol + hp)),
                  pl.BlockSpec((GLA_GATE_RANK, s), lambda bb, hp: (0, bb)),
                  pl.BlockSpec((GLA_GATE_RANK, LANES), lambda bb, hp: (0, hp)),
                  pl.BlockSpec((1, LANES), lambda bb, hp: (0, hp)),
                  pl.BlockSpec((1, GLA_DV), lambda bb, hp: (0, 0))],
        out_specs=pl.BlockSpec((None, s, 2 * GLA_DV), lambda bb, hp: (bb, 0, hp)),
        out_shape=jax.ShapeDtypeStruct((b, s, GLA_WIDTH), BF16),
        scratch_shapes=[pltpu.VMEM((s, LANES), F32),
                        pltpu.VMEM((2, GLA_DV, LANES), F32)],
        compiler_params=_params(2),
        name="gla",
    )(proj3, proj3, proj3, proj3, ga3, wal, bal, gw)


HALF = D_MODEL // 2


def _pack_rows(v):
    return pltpu.pack_elementwise([v[:, :HALF], v[:, HALF:]], packed_dtype=BF16)


def _unpack_rows(w):
    return (pltpu.unpack_elementwise(w, index=0, packed_dtype=BF16, unpacked_dtype=F32),
            pltpu.unpack_elementwise(w, index=1, packed_dtype=BF16, unpacked_dtype=F32))


def _outproj_kernel(oa_ref, ob_ref, x_ref, g1_ref, sc_ref, sh_ref, g2_ref, nw_ref, wo_ref,
                    ws1_ref, ws3_ref, ws2_ref, wrt_ref, eb_ref, base_ref, h_ref, code_ref, w_ref, cnt_ref,
                    carry_ref):
    @pl.when(pl.program_id(0) == 0)
    def _init():
        carry_ref[...] = jnp.zeros_like(carry_ref)

    mix = (jnp.dot(oa_ref[...], wo_ref[:MOBA_WIDTH, :], preferred_element_type=F32)
           + jnp.dot(ob_ref[...], wo_ref[MOBA_WIDTH:, :], preferred_element_type=F32))
    x1 = x_ref[...] + g1_ref[...] * mix
    ms = jnp.mean(x1 * x1, axis=-1, keepdims=True)
    h = x1 * lax.rsqrt(ms + NORM_EPS) * nw_ref[...]
    h = h * (1.0 + sc_ref[...]) + sh_ref[...]
    h_ref[...] = _pack_rows(h)
    hb = h.astype(BF16)
    logits_t = lax.dot_general(wrt_ref[...], hb, NT_DIMS, preferred_element_type=F32)
    scores = jax.nn.sigmoid(logits_t)
    a = jnp.dot(hb, ws1_ref[...], preferred_element_type=F32)
    u = jnp.dot(hb, ws3_ref[...], preferred_element_type=F32)
    shared = jnp.dot((_silu(a) * u).astype(BF16), ws2_ref[...], preferred_element_type=F32)
    base_ref[...] = x1 + g2_ref[...] * shared
    nt = eb_ref.shape[1]
    for part in range(scores.shape[1] // nt):
        cols = slice(part * nt, (part + 1) * nt)
        code, weights = _route_tile(scores[:, cols], eb_ref[...], carry_ref)
        code_ref[:, cols] = code
        w_ref[:, cols] = weights
    cnt_ref[...] = carry_ref[...]


def _outproj(oa, ob, x2, g1, sc, sh, g2, nw, wo, ws1, ws3, ws2, wrt, eb, seq):
    t = x2.shape[0]
    assert t <= SLOT_CODE_BASE
    tpb = seq // ROW_TILE
    vec = lambda: pl.BlockSpec((None, 1, D_MODEL), lambda i: (i // tpb, 0, 0))
    full = lambda a: pl.BlockSpec(a.shape, lambda i: (0,) * a.ndim)
    rows = lambda w: pl.BlockSpec((ROW_TILE, w), lambda i: (i, 0))
    tok = lambda: pl.BlockSpec((TOP_K, ROW_TILE), lambda i: (0, i))
    return pl.pallas_call(
        _outproj_kernel,
        grid=(t // ROW_TILE,),
        in_specs=[rows(MOBA_WIDTH), rows(GLA_WIDTH), rows(D_MODEL), vec(), vec(), vec(), vec(),
                  full(nw), full(wo), full(ws1), full(ws3), full(ws2), full(wrt), full(eb)],
        out_specs=[rows(D_MODEL), rows(HALF), tok(), tok(), full(eb)],
        out_shape=[jax.ShapeDtypeStruct((t, D_MODEL), F32),
                   jax.ShapeDtypeStruct((t, HALF), jnp.uint32),
                   jax.ShapeDtypeStruct((TOP_K, t), jnp.int32),
                   jax.ShapeDtypeStruct((TOP_K, t), F32),
                   jax.ShapeDtypeStruct(eb.shape, F32)],
        scratch_shapes=[pltpu.VMEM(eb.shape, F32)],
        compiler_params=_params(1),
        name="outproj",
    )(oa, ob, x2, g1, sc, sh, g2, nw, wo, ws1, ws3, ws2, wrt, eb)


SLOT_CODE_SHIFT = 16
SLOT_CODE_BASE = 1 << SLOT_CODE_SHIFT


def _route_tile(s, eb, carry_ref):
    ne, nt = s.shape
    choice = s + eb
    gio = lax.broadcasted_iota(jnp.int32, (GROUP_SIZE, nt), 0)
    gscore = []
    for g in range(N_GROUPS):
        cg = choice[g * GROUP_SIZE:(g + 1) * GROUP_SIZE, :]
        top1 = jnp.max(cg, axis=0, keepdims=True)
        first = jnp.min(jnp.where(cg == top1, gio, GROUP_SIZE), axis=0, keepdims=True)
        top2 = jnp.max(jnp.where(gio == first, -jnp.inf, cg), axis=0, keepdims=True)
        gscore.append(top1 + top2)
    gs = jnp.concatenate(gscore, axis=0)
    gidx = lax.broadcasted_iota(jnp.int32, gs.shape, 0)
    beaten = jnp.zeros(gs.shape, jnp.int32)
    for m in range(N_GROUPS):
        gm = gs[m:m + 1, :]
        beaten = beaten + jnp.where((gm > gs) | ((gm == gs) & (gidx > m)), 1, 0)
    gkeep = beaten < TOPK_GROUPS
    masked = jnp.concatenate(
        [jnp.where(gkeep[g:g + 1, :], choice[g * GROUP_SIZE:(g + 1) * GROUP_SIZE, :], -jnp.inf)
         for g in range(N_GROUPS)], axis=0)

    eio = lax.broadcasted_iota(jnp.int32, (ne, nt), 0)
    candidates = masked
    idx_rows, w_rows, hits = [], [], []
    for _ in range(TOP_K):
        mx = jnp.max(masked, axis=0, keepdims=True)
        idx = jnp.min(jnp.where(masked == mx, eio, ne), axis=0, keepdims=True)
        hit = eio == idx
        w_rows.append(jnp.sum(jnp.where(hit, s, 0.0), axis=0, keepdims=True))
        idx_rows.append(idx)
        hits.append(hit)
        masked = jnp.where(hit, -jnp.inf, masked)
    picked = jnp.where(masked < candidates, 1.0, 0.0)
    wk = jnp.concatenate(w_rows, axis=0)
    weights = wk / jnp.sum(wk, axis=0, keepdims=True) * ROUTED_SCALE

    tr = lax.broadcasted_iota(jnp.int32, (nt, nt), 0)
    tc = lax.broadcasted_iota(jnp.int32, (nt, nt), 1)
    before = jnp.where(tr < tc, 1.0, 0.0).astype(BF16)
    pb = picked.astype(BF16)
    pos = carry_ref[...] + jnp.dot(pb, before, preferred_element_type=F32)
    rank = jnp.concatenate(
        [jnp.sum(jnp.where(hit, pos, 0.0), axis=0, keepdims=True) for hit in hits], axis=0).astype(jnp.int32)
    carry_ref[...] = carry_ref[...] + jnp.dot(pb, jnp.ones((nt, nt), BF16), preferred_element_type=F32)
    return jnp.concatenate(idx_rows, axis=0) * SLOT_CODE_BASE + rank, weights


SLOT_TILE = 2048


def _slots_kernel(pstart_ref, code_ref, o_ref):
    code = code_ref[...]
    expert = lax.shift_right_logical(code, SLOT_CODE_SHIFT)

    def body(e, acc):
        return jnp.where(expert == e, pstart_ref[e], acc)

    start = lax.fori_loop(0, N_EXPERTS, body, jnp.zeros_like(code), unroll=8)
    o_ref[...] = start + (code & (SLOT_CODE_BASE - 1))


def _slots(pstart, code_t):
    k, t = code_t.shape
    return pl.pallas_call(
        _slots_kernel,
        grid_spec=pltpu.PrefetchScalarGridSpec(
            num_scalar_prefetch=1,
            grid=(t // SLOT_TILE,),
            in_specs=[pl.BlockSpec((k, SLOT_TILE), lambda i, p: (0, i))],
            out_specs=pl.BlockSpec((k, SLOT_TILE), lambda i, p: (0, i)),
        ),
        out_shape=jax.ShapeDtypeStruct((k, t), jnp.int32),
        compiler_params=_params(1),
        name="slots",
    )(pstart, code_t)


COMBINE_BATCHES = 2
SC_WINDOW = 128


def _sc_gather_rows(table, idx_flat):
    info = plsc.get_sparse_core_info()
    nw = info.num_cores * info.num_subcores
    n = idx_flat.shape[0]
    width = table.shape[1]
    per_worker = n // nw
    assert per_worker * nw == n and per_worker % SC_WINDOW == 0
    mesh = plsc.VectorSubcoreMesh(core_axis_name="c", subcore_axis_name="s")

    def body(table_hbm, idx_hbm, out_hbm, idx_v, rows_v, sem):
        wid = lax.axis_index("s") * info.num_cores + lax.axis_index("c")
        base = wid * per_worker

        @pl.loop(0, per_worker // SC_WINDOW)
        def _(w):
            off = pl.multiple_of(base + w * SC_WINDOW, SC_WINDOW)
            pltpu.sync_copy(idx_hbm.at[pl.ds(off, SC_WINDOW)], idx_v)
            pltpu.async_copy(table_hbm.at[idx_v], rows_v, sem).wait()
            pltpu.sync_copy(rows_v, out_hbm.at[pl.ds(off, SC_WINDOW)])

    return pl.kernel(
        body,
        out_type=jax.ShapeDtypeStruct((n, width), table.dtype),
        mesh=mesh,
        scratch_types=[pltpu.VMEM((SC_WINDOW,), jnp.int32),
                       pltpu.VMEM((SC_WINDOW, width), table.dtype),
                       pltpu.SemaphoreType.DMA],
        name="sc_gather",
    )(table, idx_flat)


def _sc_scatter_rows(rows, idx_kt, n_out):
    info = plsc.get_sparse_core_info()
    nw = info.num_cores * info.num_subcores
    t, width = rows.shape
    nk = idx_kt.shape[0]
    per_worker = t // nw
    assert per_worker * nw == t and per_worker % SC_WINDOW == 0
    mesh = plsc.VectorSubcoreMesh(core_axis_name="c", subcore_axis_name="s")

    def body(rows_hbm, idx_hbm, out_hbm, idx_v, rows_v, sem):
        wid = lax.axis_index("s") * info.num_cores + lax.axis_index("c")
        base = wid * per_worker

        @pl.loop(0, per_worker // SC_WINDOW)
        def _(w):
            off = pl.multiple_of(base + w * SC_WINDOW, SC_WINDOW)
            pltpu.sync_copy(rows_hbm.at[pl.ds(off, SC_WINDOW)], rows_v)
            pltpu.sync_copy(idx_hbm.at[:, pl.ds(off, SC_WINDOW)], idx_v)
            copies = [pltpu.async_copy(rows_v, out_hbm.at[idx_v.at[k]], sem) for k in range(nk)]
            for cp in copies:
                cp.wait()

    return pl.kernel(
        body,
        out_type=jax.ShapeDtypeStruct((n_out, width), rows.dtype),
        mesh=mesh,
        scratch_types=[pltpu.VMEM((nk, SC_WINDOW), jnp.int32),
                       pltpu.VMEM((SC_WINDOW, width), rows.dtype),
                       pltpu.SemaphoreType.DMA],
        name="sc_scatter",
    )(rows, idx_kt)


def _expert_kernel(first_ref, count_ref, used_ref, w1_ref, w3_ref, w2_ref, xs_ref, ys_ref,
                   xbuf, ybuf, sem_in, sem_out, w1f, w3f, w2f, sem_w, w1b, w3b, w2b, *, layer):
    e = pl.program_id(0)
    ne = pl.num_programs(0)
    r = xbuf.shape[1]
    n_used = used_ref[0]

    def x_copy(g, slot):
        return pltpu.make_async_copy(xs_ref.at[pl.ds(pl.multiple_of(g * r, r), r), :], xbuf.at[slot], sem_in.at[slot])

    def y_copy(g, slot):
        return pltpu.make_async_copy(ybuf.at[slot], ys_ref.at[pl.ds(pl.multiple_of(g * r, r), r), :], sem_out.at[slot])

    def w_copies(ex, slot):
        return [pltpu.make_async_copy(src.at[layer, ex], dst.at[slot], sem_w.at[slot])
                for src, dst in ((w1_ref, w1f), (w3_ref, w3f), (w2_ref, w2f))]

    nin = xbuf.shape[0]
    nout = ybuf.shape[0]

    nw = w1f.shape[0]

    @pl.when(e == 0)
    def _first_reads():
        for g in range(nin):
            @pl.when(g < n_used)
            def _(g=g):
                x_copy(g, g).start()
        for ex in range(nw - 1):
            for cp in w_copies(ex, ex):
                cp.start(priority=1)

    @pl.when(e + nw - 1 < ne)
    def _weights_ahead():
        for cp in w_copies(e + nw - 1, (e + nw - 1) % nw):
            cp.start(priority=1)

    wslot = e % nw
    for cp in w_copies(e, wslot):
        cp.wait()

    n = count_ref[e]

    @pl.when(n > 0)
    def _cast_weights():
        w1b[...] = w1f[wslot].astype(BF16)
        w3b[...] = w3f[wslot].astype(BF16)
        w2b[...] = w2f[wslot].astype(BF16)

    def run_tiles(g, count):
        tiles = [g + j for j in range(count)]
        for gj in tiles:
            x_copy(gj, gj % nin).wait()
        x = jnp.concatenate([xbuf[gj % nin] for gj in tiles], axis=0)
        for gj in tiles:
            @pl.when(gj + nin < n_used)
            def _(gj=gj):
                x_copy(gj + nin, gj % nin).start()
        lo, hi = _unpack_rows(x)
        lo, hi = lo.astype(BF16), hi.astype(BF16)
        a = (jnp.dot(lo, w1b[:HALF, :], preferred_element_type=F32)
             + jnp.dot(hi, w1b[HALF:, :], preferred_element_type=F32))
        u = (jnp.dot(lo, w3b[:HALF, :], preferred_element_type=F32)
             + jnp.dot(hi, w3b[HALF:, :], preferred_element_type=F32))
        y = _pack_rows(jnp.dot((_silu(a) * u).astype(BF16), w2b[...], preferred_element_type=F32))
        for j, gj in enumerate(tiles):
            @pl.when(gj >= nout)
            def _(gj=gj):
                y_copy(gj - nout, gj % nout).wait()
            ybuf[gj % nout] = y[j * r:(j + 1) * r]
            y_copy(gj, gj % nout).start()

    g0 = first_ref[e]
    pair = EXPERT_TILES_PER_MATMUL

    def pair_body(p, carry):
        run_tiles(g0 + p * pair, pair)
        return carry

    lax.fori_loop(0, n // pair, pair_body, 0)
    for left in range(1, pair):
        pl.when(n % pair == left)(functools.partial(run_tiles, g0 + n - left, left))

    @pl.when(e == ne - 1)
    def _drain_writes():
        for back in range(nout, 0, -1):
            @pl.when(n_used >= back)
            def _(back=back):
                y_copy(n_used - back, (n_used - back) % nout).wait()


def _experts(tile_first, tile_count, n_used, xs, w1, w3, w2, layer):
    n_rows = xs.shape[0]
    r = EXPERT_TILE
    any_spec = pl.BlockSpec(memory_space=pl.ANY)
    return pl.pallas_call(
        functools.partial(_expert_kernel, layer=layer),
        grid_spec=pltpu.PrefetchScalarGridSpec(
            num_scalar_prefetch=3,
            grid=(N_EXPERTS,),
            in_specs=[any_spec, any_spec, any_spec, any_spec],
            out_specs=any_spec,
            scratch_shapes=[pltpu.VMEM((EXPERT_IN_RING, r, HALF), jnp.uint32),
                            pltpu.VMEM((EXPERT_OUT_RING, r, HALF), jnp.uint32),
                            pltpu.SemaphoreType.DMA((EXPERT_IN_RING,)),
                            pltpu.SemaphoreType.DMA((EXPERT_OUT_RING,)),
                            pltpu.VMEM((EXPERT_WEIGHT_RING, D_MODEL, EXPERT_FF), F32),
                            pltpu.VMEM((EXPERT_WEIGHT_RING, D_MODEL, EXPERT_FF), F32),
                            pltpu.VMEM((EXPERT_WEIGHT_RING, EXPERT_FF, D_MODEL), F32),
                            pltpu.SemaphoreType.DMA((EXPERT_WEIGHT_RING,)),
                            pltpu.VMEM((D_MODEL, EXPERT_FF), BF16),
                            pltpu.VMEM((D_MODEL, EXPERT_FF), BF16),
                            pltpu.VMEM((EXPERT_FF, D_MODEL), BF16)],
        ),
        out_shape=jax.ShapeDtypeStruct((n_rows, HALF), jnp.uint32),
        compiler_params=_params(1),
        name="experts",
    )(tile_first, tile_count, n_used, w1, w3, w2, xs)


def _combine_dense_kernel(base_ref, g2_ref, w_ref, yg_ref, o_ref):
    acc_lo = acc_hi = None
    for k in range(TOP_K):
        lo, hi = _unpack_rows(yg_ref[k])
        wk = w_ref[:, k:k + 1]
        acc_lo = wk * lo if acc_lo is None else acc_lo + wk * lo
        acc_hi = wk * hi if acc_hi is None else acc_hi + wk * hi
    o_ref[:, :HALF] = base_ref[:, :HALF] + g2_ref[:, :HALF] * acc_lo
    o_ref[:, HALF:] = base_ref[:, HALF:] + g2_ref[:, HALF:] * acc_hi


def _combine_dense(base, g2, w_tok, yg, batch, seq):
    t = base.shape[0]
    nt = ROUTE_TILE
    tpb = seq // nt
    rows = lambda i: (batch * tpb + i, 0)
    return pl.pallas_call(
        _combine_dense_kernel,
        grid=(yg.shape[1] // nt,),
        in_specs=[pl.BlockSpec((nt, D_MODEL), rows),
                  pl.BlockSpec((None, 1, D_MODEL), lambda i: (batch + i // tpb, 0, 0)),
                  pl.BlockSpec((nt, TOP_K), rows),
                  pl.BlockSpec((TOP_K, nt, HALF), lambda i: (0, i, 0))],
        out_specs=pl.BlockSpec((nt, D_MODEL), rows),
        out_shape=jax.ShapeDtypeStruct((t, D_MODEL), F32),
        input_output_aliases={0: 0},
        compiler_params=_params(1),
        name="combine_dense",
    )(base, g2, w_tok, yg)


def _layer(layer, x, c, w_ada, b_ada, norm1_w, norm2_w, w_in, q_norm_w, k_norm_w, rel_bias, w_alpha, b_alpha,
           moba_out_w, gla_out_w, w_out, w_router, e_bias, w1, w3, w2, ws1, ws3, ws2):
    b, s, d = x.shape
    t = b * s
    x2 = x.reshape(t, d)

    mod = _mod(c, w_ada, b_ada)
    sh1, sc1, g1, sh2, sc2, g2 = [mod[:, j * d:(j + 1) * d].reshape(b, 1, d) for j in range(6)]

    w_main = w_in[:, :D_MAIN].astype(BF16)
    w_ga = w_in[:, D_MAIN:].T.astype(BF16)
    per_chunk = 256 // MOBA_HEAD_DIM
    qw = jnp.tile(q_norm_w.astype(F32), per_chunk).reshape(1, 256) * (MOBA_HEAD_DIM ** -0.5 * LOG2E)
    kw = jnp.tile(k_norm_w.astype(F32), per_chunk).reshape(1, 256)
    proj, ga = _inproj(x2, sc1, sh1, norm1_w.reshape(1, d), w_main, w_ga, qw, kw, s)
    proj3 = proj.reshape(b, s, D_MAIN)

    near, far = _moba_bias_tables(rel_bias)
    ow = jnp.tile(moba_out_w.astype(F32), 2).reshape(1, LANES)
    o_a = _moba(proj3, near, far, ow)

    o_b = _gla(proj3, ga, w_alpha.astype(F32), b_alpha.reshape(1, GLA_KEY_WIDTH),
               gla_out_w.reshape(1, GLA_DV))

    eb = jnp.broadcast_to(e_bias.astype(F32)[:, None], (N_EXPERTS, ROUTE_TILE))
    base, h2, code_t, w_t, counts = _outproj(
        o_a.reshape(t, MOBA_WIDTH), o_b.reshape(t, GLA_WIDTH), x2, g1, sc2, sh2, g2,
        norm2_w.reshape(1, d), w_out.astype(BF16), ws1.astype(BF16), ws3.astype(BF16), ws2.astype(BF16),
        w_router.T.astype(BF16), eb, s)

    r = EXPERT_TILE
    n_tiles = (t * TOP_K + N_EXPERTS * (r - 1) + r - 1) // r
    n_rows = n_tiles * r
    cnt = counts[:, 0].astype(jnp.int32)
    padded = (cnt + r - 1) // r * r
    pend = jnp.cumsum(padded)
    pstart = pend - padded
    n_used = (pend[-1:] // r).astype(jnp.int32)
    dest_t = _slots(pstart, code_t)

    xs = _sc_scatter_rows(h2, dest_t, n_rows)
    ys = _experts(pstart // r, padded // r, n_used, xs, w1, w3, w2, layer)
    w_tok = w_t.T
    out = base
    per = COMBINE_BATCHES * s
    for bi in range(0, b, COMBINE_BATCHES):
        idx = dest_t[:, bi * s:bi * s + per].reshape(TOP_K * per)
        yg = _sc_gather_rows(ys, idx).reshape(TOP_K, per, HALF)
        out = _combine_dense(out, g2, w_tok, yg, bi, s)
    return out.reshape(b, s, d)


def kernel(x, c, w_ada, b_ada, norm1_w, norm2_w, w_in, q_norm_w, k_norm_w, rel_bias, w_alpha, b_alpha,
           moba_out_w, gla_out_w, w_out, w_router, e_bias, w1, w3, w2, ws1, ws3, ws2):
    for l in range(w_ada.shape[0]):
        x = _layer(l, x, c, w_ada[l], b_ada[l], norm1_w[l], norm2_w[l], w_in[l], q_norm_w[l], k_norm_w[l],
                   rel_bias, w_alpha[l], b_alpha[l], moba_out_w[l], gla_out_w[l], w_out[l], w_router[l],
                   e_bias[l], w1, w3, w2, ws1[l], ws3[l], ws2[l])
    return x
```
